```python
import math
import numpy as np
import jax
import jax.numpy as jnp
from jax import lax

D_MODEL = 1024
BATCH = 8
SEQ = 8192
DEPTH = 1

CTX_LEN = 256
GRID_W = 64
EPS = 1e-6
CONV_K = 3
CHUNK = 64
N_DIR = 2
N_BRANCH = 2
GDN_HEADS = 8
GDN_DK = 128
GDN_DV = 128
GDN_QK = GDN_HEADS * GDN_DK
GDN_WIDTH = GDN_HEADS * GDN_DV
SSM_INNER = 2 * D_MODEL
SSM_HEAD_DIM = 64
SSM_HEADS = SSM_INNER // SSM_HEAD_DIM
SSM_GROUPS = 4
SSM_STATE = 128
SSM_XBC = SSM_INNER + 2 * SSM_GROUPS * SSM_STATE
D_FF = ((8 * D_MODEL // 3 + 255) // 256) * 256
IN_SIZES = (
    2 * GDN_QK + GDN_WIDTH,
    GDN_WIDTH,
    N_DIR * GDN_HEADS,
    N_DIR * GDN_HEADS,
    SSM_INNER,
    SSM_XBC,
    N_DIR * SSM_HEADS,
    N_BRANCH * D_MODEL,
)
D_IN_PROJ = sum(IN_SIZES)

kernel_name = "bidir_gdn_mamba2_griffin_merge_prefix_ctx"


def _split_cols(t, sizes):
    idx = np.cumsum(np.array(sizes))[:-1].tolist()
    return jnp.split(t, idx, axis=-1)


def _rmsnorm(x, w):
    x32 = x.astype(jnp.float32)
    y = x32 * lax.rsqrt(jnp.mean(x32 * x32, axis=-1, keepdims=True) + EPS)
    return y.astype(x.dtype) * w


def _l2norm(x):
    x32 = x.astype(jnp.float32)
    return (x32 * lax.rsqrt(jnp.sum(x32 * x32, axis=-1, keepdims=True) + EPS)).astype(x.dtype)


def _flip(t):
    return jnp.flip(t, axis=1)


def _dwconv_centred(u, w, b):
    pad = CONV_K // 2
    length = u.shape[-2]
    up = jnp.pad(u, [(0, 0)] * (u.ndim - 2) + [(pad, pad), (0, 0)])
    out = b
    for j in range(CONV_K):
        out = out + up[..., j:j + length, :] * w[j]
    return out


def _short_conv(u, w, b, latent):
    if latent:
        bsz, length, ch = u.shape
        rows = length // GRID_W
        return _dwconv_centred(u.reshape(bsz, rows, GRID_W, ch), w, b).reshape(bsz, length, ch)
    return _dwconv_centred(u, w, b)


def _to_chunks(t):
    bsz, length = t.shape[:2]
    return jnp.moveaxis(t.reshape(bsz, length // CHUNK, CHUNK, *t.shape[2:]), 1, 0)


def _from_chunks(t):
    t = jnp.moveaxis(t, 0, 1)
    return t.reshape(t.shape[0], t.shape[1] * t.shape[2], *t.shape[3:])


def _gdn_chunked(q, k, v, g, beta, s0):
    out_dtype = v.dtype
    q, k, v, g, beta = (t.astype(jnp.float32) for t in (q, k, v, g, beta))
    idx = jnp.arange(CHUNK)
    incl = idx[:, None] >= idx[None, :]
    strict = idx[:, None] > idx[None, :]
    eye = jnp.eye(CHUNK, dtype=jnp.float32)

    def step(s, inp):
        qc, kc, vc, gc, bc = inp
        qh, kh, vh = (jnp.swapaxes(t, 1, 2) for t in (qc, kc, vc))
        gcum = jnp.cumsum(jnp.swapaxes(gc, 1, 2), axis=-1)
        bh = jnp.swapaxes(bc, 1, 2)[..., None]
        decay = jnp.exp(jnp.where(incl, gcum[..., :, None] - gcum[..., None, :], -jnp.inf))
        kb = kh * bh
        lower = jnp.where(strict, jnp.einsum('bhid,bhjd->bhij', kb, kh) * decay, 0.0) + eye
        rhs = jnp.concatenate([kb * jnp.exp(gcum)[..., None], vh * bh], axis=-1)
        sol = lax.linalg.triangular_solve(lower, rhs, left_side=True, lower=True, unit_diagonal=True)
        w_c, u_c = sol[..., :GDN_DK], sol[..., GDN_DK:]
        v_new = u_c - jnp.einsum('bhck,bhkv->bhcv', w_c, s)
        attn = jnp.einsum('bhik,bhjk->bhij', qh, kh) * decay
        o = (jnp.einsum('bhck,bhkv->bhcv', qh * jnp.exp(gcum)[..., None], s)
             + jnp.einsum('bhij,bhjv->bhiv', attn, v_new))
        g_last = gcum[..., -1:]
        s = (s * jnp.exp(g_last)[..., None]
             + jnp.einsum('bhck,bhcv->bhkv', kh * jnp.exp(g_last - gcum)[..., None], v_new))
        return s, jnp.swapaxes(o, 1, 2)

    s_fin, o = lax.scan(step, s0, tuple(_to_chunks(t) for t in (q, k, v, g, beta)))
    return _from_chunks(o).astype(out_dtype), s_fin


def _ssd_chunked(x, dt, a, bm, cm, h0):
    out_dtype = x.dtype
    x, dt, bm, cm = (t.astype(jnp.float32) for t in (x, dt, bm, cm))
    bsz = x.shape[0]
    rep = SSM_HEADS // SSM_GROUPS
    idx = jnp.arange(CHUNK)
    incl = idx[:, None] >= idx[None, :]

    def step(h, inp):
        xc, dtc, bc, cc = inp
        acum = jnp.cumsum(jnp.swapaxes(dtc * a, 1, 2), axis=-1).reshape(bsz, SSM_GROUPS, rep, CHUNK)
        seg = jnp.exp(jnp.where(incl, acum[..., :, None] - acum[..., None, :], -jnp.inf))
        xdt = (xc * dtc[..., None]).reshape(bsz, CHUNK, SSM_GROUPS, rep, SSM_HEAD_DIM)
        cb = jnp.einsum('bign,bjgn->bgij', cc, bc)
        y_diag = jnp.einsum('bgij,bgrij,bjgrp->bigrp', cb, seg, xdt)
        hg = h.reshape(bsz, SSM_GROUPS, rep, SSM_HEAD_DIM, SSM_STATE)
        y_off = jnp.einsum('bign,bgrpn,bgri->bigrp', cc, hg, jnp.exp(acum))
        a_last = acum[..., -1:]
        new = jnp.einsum('bjgn,bgrj,bjgrp->bgrpn', bc, jnp.exp(a_last - acum), xdt)
        hg = hg * jnp.exp(a_last)[..., None] + new
        y = (y_diag + y_off).reshape(bsz, CHUNK, SSM_HEADS, SSM_HEAD_DIM)
        return hg.reshape(h.shape), y

    h_fin, y = lax.scan(step, h0, tuple(_to_chunks(t) for t in (x, dt, bm, cm)))
    return _from_chunks(y).astype(out_dtype), h_fin


def _gdn_bidir(q, k, v, g, beta, s_f, s_b):
    o_f, s_f = _gdn_chunked(q, k, v, g[:, :, 0], beta[:, :, 0], s_f)
    o_b, s_b = _gdn_chunked(_flip(q), _flip(k), _flip(v), _flip(g[:, :, 1]), _flip(beta[:, :, 1]), s_b)
    return o_f + _flip(o_b), s_f, s_b


def _ssd_bidir(xs, dt, a_log, bm, cm, h_f, h_b):
    a = -jnp.exp(a_log.astype(jnp.float32))
    y_f, h_f = _ssd_chunked(xs, dt[:, :, 0], a[0], bm, cm, h_f)
    y_b, h_b = _ssd_chunked(_flip(xs), _flip(dt[:, :, 1]), a[1], _flip(bm), _flip(cm), h_b)
    return y_f + _flip(y_b), h_f, h_b


def _mixer_inputs(h, latent, w_in, gdn_conv_w, gdn_conv_b, gdn_a_log, gdn_dt_bias,
                  ssm_conv_w, ssm_conv_b, ssm_dt_bias):
    bsz, length, _ = h.shape
    qkv, z_gdn, a_gdn, b_gdn, z_ssm, xbc, dt_raw, br_gate = _split_cols(h @ w_in, IN_SIZES)
    qkv = jax.nn.silu(_short_conv(qkv, gdn_conv_w, gdn_conv_b, latent))
    q, k, v = _split_cols(qkv, (GDN_QK, GDN_QK, GDN_WIDTH))
    q = _l2norm(q.reshape(bsz, length, GDN_HEADS, GDN_DK)) * GDN_DK ** -0.5
    k = _l2norm(k.reshape(bsz, length, GDN_HEADS, GDN_DK))
    v = v.reshape(bsz, length, GDN_HEADS, GDN_DV)
    g = -jnp.exp(gdn_a_log) * jax.nn.softplus(a_gdn.reshape(bsz, length, N_DIR, GDN_HEADS) + gdn_dt_bias)
    beta = jax.nn.sigmoid(b_gdn.reshape(bsz, length, N_DIR, GDN_HEADS))
    xbc = jax.nn.silu(_short_conv(xbc, ssm_conv_w, ssm_conv_b, latent))
    xs, bm, cm = _split_cols(xbc, (SSM_INNER, SSM_GROUPS * SSM_STATE, SSM_GROUPS * SSM_STATE))
    xs = xs.reshape(bsz, length, SSM_HEADS, SSM_HEAD_DIM)
    bm = bm.reshape(bsz, length, SSM_GROUPS, SSM_STATE)
    cm = cm.reshape(bsz, length, SSM_GROUPS, SSM_STATE)
    dt = jax.nn.softplus(dt_raw.reshape(bsz, length, N_DIR, SSM_HEADS) + ssm_dt_bias)
    return q, k, v, g, beta, z_gdn, xs, bm, cm, dt, z_ssm, br_gate


def _mixer_output(o_gdn, z_gdn, y_ssm, xs, z_ssm, br_gate, gdn_norm_w, ssm_d, ssm_norm_w,
                  w_br_gdn, w_br_ssm, w_out):
    bsz, length = o_gdn.shape[:2]
    o = _rmsnorm(o_gdn, gdn_norm_w) * jax.nn.silu(z_gdn.reshape(bsz, length, GDN_HEADS, GDN_DV))
    p_gdn = o.reshape(bsz, length, GDN_WIDTH) @ w_br_gdn
    y = (y_ssm + ssm_d[:, None] * xs).reshape(bsz, length, SSM_INNER) * jax.nn.silu(z_ssm)
    y = _rmsnorm(y.reshape(bsz, length, SSM_GROUPS, SSM_INNER // SSM_GROUPS),
                 ssm_norm_w.reshape(SSM_GROUPS, SSM_INNER // SSM_GROUPS)).reshape(bsz, length, SSM_INNER)
    p_ssm = y @ w_br_ssm
    gate_gdn, gate_ssm = jnp.split(br_gate, N_BRANCH, axis=-1)
    merged = jax.nn.sigmoid(gate_gdn) * p_gdn + jax.nn.sigmoid(gate_ssm) * p_ssm
    return merged @ w_out


def _swiglu(h, w_ffn_in, w_ffn_out):
    gate, up = jnp.split(h @ w_ffn_in, 2, axis=-1)
    return (jax.nn.silu(gate) * up) @ w_ffn_out


def _fwd_setup_inputs(seed: int = 0) -> dict:
    key = jax.random.key(seed)
    ks = jax.random.split(key, 26)

    def nrm(i, shape, scale):
        return scale * jax.random.normal(ks[i], shape, jnp.float32)

    def gain(i, shape):
        return 1.0 + 0.02 * jax.random.normal(ks[i], shape, jnp.float32)

    def a_log(i, shape):
        return jnp.log(jax.random.uniform(ks[i], shape, jnp.float32, 1.0, 16.0))

    def dt_bias(i, shape):
        dt = jnp.exp(jax.random.uniform(ks[i], shape, jnp.float32, math.log(1e-3), math.log(1e-1)))
        return dt + jnp.log(-jnp.expm1(-dt))

    return {
        "x": nrm(0, (BATCH, SEQ, D_MODEL), 1.0),
        "c": nrm(1, (BATCH, D_MODEL), 1.0),
        "ctx": nrm(2, (BATCH, CTX_LEN, D_MODEL), 1.0),
        "c_ctx": nrm(3, (D_MODEL,), 1.0),
        "ada_w": nrm(4, (DEPTH, D_MODEL, 6 * D_MODEL), 0.5 * D_MODEL ** -0.5),
        "ada_b": nrm(5, (DEPTH, 6 * D_MODEL), 0.01),
        "norm1_w": gain(6, (DEPTH, D_MODEL)),
        "w_in": nrm(7, (DEPTH, D_MODEL, D_IN_PROJ), D_MODEL ** -0.5),
        "gdn_conv_w": nrm(8, (DEPTH, CONV_K, 2 * GDN_QK + GDN_WIDTH), CONV_K ** -0.5),
        "gdn_conv_b": nrm(9, (DEPTH, 2 * GDN_QK + GDN_WIDTH), 0.01),
        "gdn_a_log": a_log(10, (DEPTH, N_DIR, GDN_HEADS)),
        "gdn_dt_bias": dt_bias(11, (DEPTH, N_DIR, GDN_HEADS)),
        "gdn_norm_w": gain(12, (DEPTH, GDN_DV)),
        "ssm_conv_w": nrm(13, (DEPTH, CONV_K, SSM_XBC), CONV_K ** -0.5),
        "ssm_conv_b": nrm(14, (DEPTH, SSM_XBC), 0.01),
        "ssm_a_log": a_log(15, (DEPTH, N_DIR, SSM_HEADS)),
        "ssm_dt_bias": dt_bias(16, (DEPTH, N_DIR, SSM_HEADS)),
        "ssm_d": gain(17, (DEPTH, SSM_HEADS)),
        "ssm_norm_w": gain(18, (DEPTH, SSM_INNER)),
        "w_br_gdn": nrm(19, (DEPTH, GDN_WIDTH, D_MODEL), GDN_WIDTH ** -0.5),
        "w_br_ssm": nrm(20, (DEPTH, SSM_INNER, D_MODEL), SSM_INNER ** -0.5),
        "w_out": nrm(21, (DEPTH, D_MODEL, D_MODEL), D_MODEL ** -0.5),
        "norm2_w": gain(22, (DEPTH, D_MODEL)),
        "w_ffn_in": nrm(23, (DEPTH, D_MODEL, 2 * D_FF), D_MODEL ** -0.5),
        "w_ffn_out": nrm(24, (DEPTH, D_FF, D_MODEL), D_FF ** -0.5),
        "norm_f_w": gain(25, (D_MODEL,)),
    }


def _fwd_reference(x, c, ctx, c_ctx, ada_w, ada_b, norm1_w, w_in, gdn_conv_w, gdn_conv_b, gdn_a_log,
              gdn_dt_bias, gdn_norm_w, ssm_conv_w, ssm_conv_b, ssm_a_log, ssm_dt_bias, ssm_d,
              ssm_norm_w, w_br_gdn, w_br_ssm, w_out, norm2_w, w_ffn_in, w_ffn_out, norm_f_w):
    bsz = x.shape[0]
    silu_c = jax.nn.silu(c)[:, None, :]
    silu_cc = jax.nn.silu(c_ctx)
    h_lat, h_ctx = x, ctx
    for i in range(DEPTH):
        sh1, sc1, g1, sh2, sc2, g2 = jnp.split(silu_c @ ada_w[i] + ada_b[i], 6, axis=-1)
        csh1, csc1, cg1, csh2, csc2, cg2 = jnp.split(silu_cc @ ada_w[i] + ada_b[i], 6, axis=-1)
        prm = (w_in[i], gdn_conv_w[i], gdn_conv_b[i], gdn_a_log[i], gdn_dt_bias[i],
               ssm_conv_w[i], ssm_conv_b[i], ssm_dt_bias[i])
        a_lat = _rmsnorm(h_lat, norm1_w[i]) * (1 + sc1) + sh1
        a_ctx = _rmsnorm(h_ctx, norm1_w[i]) * (1 + csc1) + csh1
        (cq, ck, cv, cgd, cbeta, cz_gdn, cxs, cbm, ccm, cdt, cz_ssm, cgate) = _mixer_inputs(a_ctx, False, *prm)
        (lq, lk, lv, lgd, lbeta, lz_gdn, lxs, lbm, lcm, ldt, lz_ssm, lgate) = _mixer_inputs(a_lat, True, *prm)
        s0 = jnp.zeros((bsz, GDN_HEADS, GDN_DK, GDN_DV), jnp.float32)
        h0 = jnp.zeros((bsz, SSM_HEADS, SSM_HEAD_DIM, SSM_STATE), jnp.float32)
        co_gdn, s_f, s_b = _gdn_bidir(cq, ck, cv, cgd, cbeta, s0, s0)
        lo_gdn, _, _ = _gdn_bidir(lq, lk, lv, lgd, lbeta, s_f, s_b)
        cy_ssm, hf, hb = _ssd_bidir(cxs, cdt, ssm_a_log[i], cbm, ccm, h0, h0)
        ly_ssm, _, _ = _ssd_bidir(lxs, ldt, ssm_a_log[i], lbm, lcm, hf, hb)
        out_prm = (gdn_norm_w[i], ssm_d[i], ssm_norm_w[i], w_br_gdn[i], w_br_ssm[i], w_out[i])
        h_lat = h_lat + g1 * _mixer_output(lo_gdn, lz_gdn, ly_ssm, lxs, lz_ssm, lgate, *out_prm)
        f_lat = _rmsnorm(h_lat, norm2_w[i]) * (1 + sc2) + sh2
        h_lat = h_lat + g2 * _swiglu(f_lat, w_ffn_in[i], w_ffn_out[i])
        if i < DEPTH - 1:
            h_ctx = h_ctx + cg1 * _mixer_output(co_gdn, cz_gdn, cy_ssm, cxs, cz_ssm, cgate, *out_prm)
            f_ctx = _rmsnorm(h_ctx, norm2_w[i]) * (1 + csc2) + csh2
            h_ctx = h_ctx + cg2 * _swiglu(f_ctx, w_ffn_in[i], w_ffn_out[i])
    return _rmsnorm(h_lat, norm_f_w)


import jax as _jax
import jax.numpy as _jnp

TWIN_FORMAT = 'train_step'
FWD_PARAMS = ['x', 'c', 'ctx', 'c_ctx', 'ada_w', 'ada_b', 'norm1_w', 'w_in', 'gdn_conv_w', 'gdn_conv_b', 'gdn_a_log', 'gdn_dt_bias', 'gdn_norm_w', 'ssm_conv_w', 'ssm_conv_b', 'ssm_a_log', 'ssm_dt_bias', 'ssm_d', 'ssm_norm_w', 'w_br_gdn', 'w_br_ssm', 'w_out', 'norm2_w', 'w_ffn_in', 'w_ffn_out', 'norm_f_w']
TWIN_WEIGHTS = ['c_ctx', 'ada_w', 'ada_b', 'norm1_w', 'w_in', 'gdn_conv_w', 'gdn_conv_b', 'gdn_a_log', 'gdn_dt_bias', 'gdn_norm_w', 'ssm_conv_w', 'ssm_conv_b', 'ssm_a_log', 'ssm_dt_bias', 'ssm_d', 'ssm_norm_w', 'w_br_gdn', 'w_br_ssm', 'w_out', 'norm2_w', 'w_ffn_in', 'w_ffn_out', 'norm_f_w']
TWIN_DIFF_INPUT = 'x'
TWIN_INPUTS = ['x', 'c', 'ctx', 'c_ctx', 'ada_w', 'ada_b', 'norm1_w', 'w_in', 'gdn_conv_w', 'gdn_conv_b', 'gdn_a_log', 'gdn_dt_bias', 'gdn_norm_w', 'ssm_conv_w', 'ssm_conv_b', 'ssm_a_log', 'ssm_dt_bias', 'ssm_d', 'ssm_norm_w', 'w_br_gdn', 'w_br_ssm', 'w_out', 'norm2_w', 'w_ffn_in', 'w_ffn_out', 'norm_f_w', 'loss_target', 'm_c_ctx', 'm_ada_w', 'm_ada_b', 'm_norm1_w', 'm_w_in', 'm_gdn_conv_w', 'm_gdn_conv_b', 'm_gdn_a_log', 'm_gdn_dt_bias', 'm_gdn_norm_w', 'm_ssm_conv_w', 'm_ssm_conv_b', 'm_ssm_a_log', 'm_ssm_dt_bias', 'm_ssm_d', 'm_ssm_norm_w', 'm_w_br_gdn', 'm_w_br_ssm', 'm_w_out', 'm_norm2_w', 'm_w_ffn_in', 'm_w_ffn_out', 'm_norm_f_w', 'v_c_ctx', 'v_ada_w', 'v_ada_b', 'v_norm1_w', 'v_w_in', 'v_gdn_conv_w', 'v_gdn_conv_b', 'v_gdn_a_log', 'v_gdn_dt_bias', 'v_gdn_norm_w', 'v_ssm_conv_w', 'v_ssm_conv_b', 'v_ssm_a_log', 'v_ssm_dt_bias', 'v_ssm_d', 'v_ssm_norm_w', 'v_w_br_gdn', 'v_w_br_ssm', 'v_w_out', 'v_norm2_w', 'v_w_ffn_in', 'v_w_ffn_out', 'v_norm_f_w']
TWIN_OUTPUTS = ['loss', 'grad_x', 'grad_c_ctx', 'grad_ada_w', 'grad_ada_b', 'grad_norm1_w', 'grad_w_in', 'grad_gdn_conv_w', 'grad_gdn_conv_b', 'grad_gdn_a_log', 'grad_gdn_dt_bias', 'grad_gdn_norm_w', 'grad_ssm_conv_w', 'grad_ssm_conv_b', 'grad_ssm_a_log', 'grad_ssm_dt_bias', 'grad_ssm_d', 'grad_ssm_norm_w', 'grad_w_br_gdn', 'grad_w_br_ssm', 'grad_w_out', 'grad_norm2_w', 'grad_w_ffn_in', 'grad_w_ffn_out', 'grad_norm_f_w', 'delta_c_ctx', 'delta_ada_w', 'delta_ada_b', 'delta_norm1_w', 'delta_w_in', 'delta_gdn_conv_w', 'delta_gdn_conv_b', 'delta_gdn_a_log', 'delta_gdn_dt_bias', 'delta_gdn_norm_w', 'delta_ssm_conv_w', 'delta_ssm_conv_b', 'delta_ssm_a_log', 'delta_ssm_dt_bias', 'delta_ssm_d', 'delta_ssm_norm_w', 'delta_w_br_gdn', 'delta_w_br_ssm', 'delta_w_out', 'delta_norm2_w', 'delta_w_ffn_in', 'delta_w_ffn_out', 'delta_norm_f_w', 'new_m_c_ctx', 'new_m_ada_w', 'new_m_ada_b', 'new_m_norm1_w', 'new_m_w_in', 'new_m_gdn_conv_w', 'new_m_gdn_conv_b', 'new_m_gdn_a_log', 'new_m_gdn_dt_bias', 'new_m_gdn_norm_w', 'new_m_ssm_conv_w', 'new_m_ssm_conv_b', 'new_m_ssm_a_log', 'new_m_ssm_dt_bias', 'new_m_ssm_d', 'new_m_ssm_norm_w', 'new_m_w_br_gdn', 'new_m_w_br_ssm', 'new_m_w_out', 'new_m_norm2_w', 'new_m_w_ffn_in', 'new_m_w_ffn_out', 'new_m_norm_f_w', 'new_v_c_ctx', 'new_v_ada_w', 'new_v_ada_b', 'new_v_norm1_w', 'new_v_w_in', 'new_v_gdn_conv_w', 'new_v_gdn_conv_b', 'new_v_gdn_a_log', 'new_v_gdn_dt_bias', 'new_v_gdn_norm_w', 'new_v_ssm_conv_w', 'new_v_ssm_conv_b', 'new_v_ssm_a_log', 'new_v_ssm_dt_bias', 'new_v_ssm_d', 'new_v_ssm_norm_w', 'new_v_w_br_gdn', 'new_v_w_br_ssm', 'new_v_w_out', 'new_v_norm2_w', 'new_v_w_ffn_in', 'new_v_w_ffn_out', 'new_v_norm_f_w']
TWIN_LEAF_KINDS = {'loss': 'loss', 'grad_x': 'grad_x', 'grad_c_ctx': 'grad_w', 'grad_ada_w': 'grad_w', 'grad_ada_b': 'grad_w', 'grad_norm1_w': 'grad_w', 'grad_w_in': 'grad_w', 'grad_gdn_conv_w': 'grad_w', 'grad_gdn_conv_b': 'grad_w', 'grad_gdn_a_log': 'grad_w', 'grad_gdn_dt_bias': 'grad_w', 'grad_gdn_norm_w': 'grad_w', 'grad_ssm_conv_w': 'grad_w', 'grad_ssm_conv_b': 'grad_w', 'grad_ssm_a_log': 'grad_w', 'grad_ssm_dt_bias': 'grad_w', 'grad_ssm_d': 'grad_w', 'grad_ssm_norm_w': 'grad_w', 'grad_w_br_gdn': 'grad_w', 'grad_w_br_ssm': 'grad_w', 'grad_w_out': 'grad_w', 'grad_norm2_w': 'grad_w', 'grad_w_ffn_in': 'grad_w', 'grad_w_ffn_out': 'grad_w', 'grad_norm_f_w': 'grad_w', 'delta_c_ctx': 'delta_w', 'delta_ada_w': 'delta_w', 'delta_ada_b': 'delta_w', 'delta_norm1_w': 'delta_w', 'delta_w_in': 'delta_w', 'delta_gdn_conv_w': 'delta_w', 'delta_gdn_conv_b': 'delta_w', 'delta_gdn_a_log': 'delta_w', 'delta_gdn_dt_bias': 'delta_w', 'delta_gdn_norm_w': 'delta_w', 'delta_ssm_conv_w': 'delta_w', 'delta_ssm_conv_b': 'delta_w', 'delta_ssm_a_log': 'delta_w', 'delta_ssm_dt_bias': 'delta_w', 'delta_ssm_d': 'delta_w', 'delta_ssm_norm_w': 'delta_w', 'delta_w_br_gdn': 'delta_w', 'delta_w_br_ssm': 'delta_w', 'delta_w_out': 'delta_w', 'delta_norm2_w': 'delta_w', 'delta_w_ffn_in': 'delta_w', 'delta_w_ffn_out': 'delta_w', 'delta_norm_f_w': 'delta_w', 'new_m_c_ctx': 'new_m', 'new_m_ada_w': 'new_m', 'new_m_ada_b': 'new_m', 'new_m_norm1_w': 'new_m', 'new_m_w_in': 'new_m', 'new_m_gdn_conv_w': 'new_m', 'new_m_gdn_conv_b': 'new_m', 'new_m_gdn_a_log': 'new_m', 'new_m_gdn_dt_bias': 'new_m', 'new_m_gdn_norm_w': 'new_m', 'new_m_ssm_conv_w': 'new_m', 'new_m_ssm_conv_b': 'new_m', 'new_m_ssm_a_log': 'new_m', 'new_m_ssm_dt_bias': 'new_m', 'new_m_ssm_d': 'new_m', 'new_m_ssm_norm_w': 'new_m', 'new_m_w_br_gdn': 'new_m', 'new_m_w_br_ssm': 'new_m', 'new_m_w_out': 'new_m', 'new_m_norm2_w': 'new_m', 'new_m_w_ffn_in': 'new_m', 'new_m_w_ffn_out': 'new_m', 'new_m_norm_f_w': 'new_m', 'new_v_c_ctx': 'new_v', 'new_v_ada_w': 'new_v', 'new_v_ada_b': 'new_v', 'new_v_norm1_w': 'new_v', 'new_v_w_in': 'new_v', 'new_v_gdn_conv_w': 'new_v', 'new_v_gdn_conv_b': 'new_v', 'new_v_gdn_a_log': 'new_v', 'new_v_gdn_dt_bias': 'new_v', 'new_v_gdn_norm_w': 'new_v', 'new_v_ssm_conv_w': 'new_v', 'new_v_ssm_conv_b': 'new_v', 'new_v_ssm_a_log': 'new_v', 'new_v_ssm_dt_bias': 'new_v', 'new_v_ssm_d': 'new_v', 'new_v_ssm_norm_w': 'new_v', 'new_v_w_br_gdn': 'new_v', 'new_v_w_br_ssm': 'new_v', 'new_v_w_out': 'new_v', 'new_v_norm2_w': 'new_v', 'new_v_w_ffn_in': 'new_v', 'new_v_w_ffn_out': 'new_v', 'new_v_norm_f_w': 'new_v'}


def _forward(args):
    return _fwd_reference(*[args[k] for k in FWD_PARAMS])


def _output_shape():
    def fwd():
        inp = _fwd_setup_inputs(0)
        return _fwd_reference(*[inp[k] for k in FWD_PARAMS])
    out = _jax.eval_shape(fwd)
    return out.shape, out.dtype

N_MICROBATCH = 1
ADAM_LR = 0.001
ADAM_B1 = 0.9
ADAM_B2 = 0.999
ADAM_EPS = 1e-08
ADAM_WD = 0.01
ADAM_STEP = 10
PER_EXAMPLE_BATCH_AXIS = {'x': 0, 'c': 0, 'ctx': 0, 'loss_target': 0}
SHARED_INPUTS = []
_WEIGHT_DTYPES = {'c_ctx': _jnp.float32, 'ada_w': _jnp.float32, 'ada_b': _jnp.float32, 'norm1_w': _jnp.float32, 'w_in': _jnp.float32, 'gdn_conv_w': _jnp.float32, 'gdn_conv_b': _jnp.float32, 'gdn_a_log': _jnp.float32, 'gdn_dt_bias': _jnp.float32, 'gdn_norm_w': _jnp.float32, 'ssm_conv_w': _jnp.float32, 'ssm_conv_b': _jnp.float32, 'ssm_a_log': _jnp.float32, 'ssm_dt_bias': _jnp.float32, 'ssm_d': _jnp.float32, 'ssm_norm_w': _jnp.float32, 'w_br_gdn': _jnp.float32, 'w_br_ssm': _jnp.float32, 'w_out': _jnp.float32, 'norm2_w': _jnp.float32, 'w_ffn_in': _jnp.float32, 'w_ffn_out': _jnp.float32, 'norm_f_w': _jnp.float32}
MOMENT_SCALE = {'c_ctx': 3.655752e-03, 'ada_w': 7.311950e-02, 'ada_b': 1.259739e-01, 'norm1_w': 8.070510e-02, 'w_in': 2.366992e-02, 'gdn_conv_w': 1.995819e-02, 'gdn_conv_b': 2.410419e-02, 'gdn_a_log': 9.246648e-02, 'gdn_dt_bias': 9.483099e-02, 'gdn_norm_w': 7.885085e-02, 'ssm_conv_w': 2.580070e-02, 'ssm_conv_b': 3.565806e-02, 'ssm_a_log': 8.901251e-02, 'ssm_dt_bias': 5.156600e-02, 'ssm_d': 9.387881e-02, 'ssm_norm_w': 3.205029e-02, 'w_br_gdn': 2.710302e-02, 'w_br_ssm': 4.140831e-02, 'w_out': 4.965825e-02, 'norm2_w': 7.261828e-02, 'w_ffn_in': 3.197977e-02, 'w_ffn_out': 5.244106e-02, 'norm_f_w': 6.401548e+01}


def _to_microbatches(a, axis):
    t = _jnp.moveaxis(a, axis, 0)
    t = t.reshape((N_MICROBATCH, t.shape[0] // N_MICROBATCH) + t.shape[1:])
    return _jnp.moveaxis(t, 1, axis + 1)


def setup_inputs(seed: int = 0) -> dict:
    inp = _fwd_setup_inputs(seed)
    key = _jax.random.fold_in(_jax.random.key(seed), 7919)
    shape, _ = _output_shape()
    out = dict(inp)
    out["loss_target"] = _jax.random.normal(_jax.random.fold_in(key, 0), shape, _jnp.float32)
    for i, name in enumerate(TWIN_WEIGHTS):
        w = inp[name].astype(_jnp.float32)
        if MOMENT_SCALE is None:
            s = _jnp.sqrt(_jnp.mean(_jnp.square(w)) + 1e-30)
        else:
            s = MOMENT_SCALE[name]
        km, kv = _jax.random.split(_jax.random.fold_in(key, i + 1))
        out[name] = w
        out["m_" + name] = s * _jax.random.normal(km, w.shape, _jnp.float32)
        out["v_" + name] = (s * s) * _jax.random.uniform(kv, w.shape, _jnp.float32, 0.5, 1.5)
    if N_MICROBATCH > 1:
        for name, axis in PER_EXAMPLE_BATCH_AXIS.items():
            out[name] = _to_microbatches(out[name], axis)
    return {'x': out['x'], 'c': out['c'], 'ctx': out['ctx'], 'c_ctx': out['c_ctx'], 'ada_w': out['ada_w'], 'ada_b': out['ada_b'], 'norm1_w': out['norm1_w'], 'w_in': out['w_in'], 'gdn_conv_w': out['gdn_conv_w'], 'gdn_conv_b': out['gdn_conv_b'], 'gdn_a_log': out['gdn_a_log'], 'gdn_dt_bias': out['gdn_dt_bias'], 'gdn_norm_w': out['gdn_norm_w'], 'ssm_conv_w': out['ssm_conv_w'], 'ssm_conv_b': out['ssm_conv_b'], 'ssm_a_log': out['ssm_a_log'], 'ssm_dt_bias': out['ssm_dt_bias'], 'ssm_d': out['ssm_d'], 'ssm_norm_w': out['ssm_norm_w'], 'w_br_gdn': out['w_br_gdn'], 'w_br_ssm': out['w_br_ssm'], 'w_out': out['w_out'], 'norm2_w': out['norm2_w'], 'w_ffn_in': out['w_ffn_in'], 'w_ffn_out': out['w_ffn_out'], 'norm_f_w': out['norm_f_w'], 'loss_target': out['loss_target'], 'm_c_ctx': out['m_c_ctx'], 'm_ada_w': out['m_ada_w'], 'm_ada_b': out['m_ada_b'], 'm_norm1_w': out['m_norm1_w'], 'm_w_in': out['m_w_in'], 'm_gdn_conv_w': out['m_gdn_conv_w'], 'm_gdn_conv_b': out['m_gdn_conv_b'], 'm_gdn_a_log': out['m_gdn_a_log'], 'm_gdn_dt_bias': out['m_gdn_dt_bias'], 'm_gdn_norm_w': out['m_gdn_norm_w'], 'm_ssm_conv_w': out['m_ssm_conv_w'], 'm_ssm_conv_b': out['m_ssm_conv_b'], 'm_ssm_a_log': out['m_ssm_a_log'], 'm_ssm_dt_bias': out['m_ssm_dt_bias'], 'm_ssm_d': out['m_ssm_d'], 'm_ssm_norm_w': out['m_ssm_norm_w'], 'm_w_br_gdn': out['m_w_br_gdn'], 'm_w_br_ssm': out['m_w_br_ssm'], 'm_w_out': out['m_w_out'], 'm_norm2_w': out['m_norm2_w'], 'm_w_ffn_in': out['m_w_ffn_in'], 'm_w_ffn_out': out['m_w_ffn_out'], 'm_norm_f_w': out['m_norm_f_w'], 'v_c_ctx': out['v_c_ctx'], 'v_ada_w': out['v_ada_w'], 'v_ada_b': out['v_ada_b'], 'v_norm1_w': out['v_norm1_w'], 'v_w_in': out['v_w_in'], 'v_gdn_conv_w': out['v_gdn_conv_w'], 'v_gdn_conv_b': out['v_gdn_conv_b'], 'v_gdn_a_log': out['v_gdn_a_log'], 'v_gdn_dt_bias': out['v_gdn_dt_bias'], 'v_gdn_norm_w': out['v_gdn_norm_w'], 'v_ssm_conv_w': out['v_ssm_conv_w'], 'v_ssm_conv_b': out['v_ssm_conv_b'], 'v_ssm_a_log': out['v_ssm_a_log'], 'v_ssm_dt_bias': out['v_ssm_dt_bias'], 'v_ssm_d': out['v_ssm_d'], 'v_ssm_norm_w': out['v_ssm_norm_w'], 'v_w_br_gdn': out['v_w_br_gdn'], 'v_w_br_ssm': out['v_w_br_ssm'], 'v_w_out': out['v_w_out'], 'v_norm2_w': out['v_norm2_w'], 'v_w_ffn_in': out['v_w_ffn_in'], 'v_w_ffn_out': out['v_w_ffn_out'], 'v_norm_f_w': out['v_norm_f_w']}


def _loss(weights, diff, rest, loss_target):
    with _jax.named_scope("forward"):
        args = {**rest, TWIN_DIFF_INPUT: diff, **{k: w.astype(_WEIGHT_DTYPES[k]) for k, w in weights.items()}}
        y = _forward(args)
    with _jax.named_scope("loss_head"):
        err = _jnp.square(y.astype(_jnp.float32) - loss_target)
        return 0.5 * _jnp.sum(_jnp.mean(err, axis=-1)) if err.ndim else 0.5 * err


def _adamw(w, g, m, v):
    m = ADAM_B1 * m + (1.0 - ADAM_B1) * g
    v = ADAM_B2 * v + (1.0 - ADAM_B2) * _jnp.square(g)
    m_hat = m / (1.0 - ADAM_B1 ** ADAM_STEP)
    v_hat = v / (1.0 - ADAM_B2 ** ADAM_STEP)
    delta = -ADAM_LR * (m_hat / (_jnp.sqrt(v_hat) + ADAM_EPS) + ADAM_WD * w)
    return delta, m, v


def reference(x, c, ctx, c_ctx, ada_w, ada_b, norm1_w, w_in, gdn_conv_w, gdn_conv_b, gdn_a_log, gdn_dt_bias, gdn_norm_w, ssm_conv_w, ssm_conv_b, ssm_a_log, ssm_dt_bias, ssm_d, ssm_norm_w, w_br_gdn, w_br_ssm, w_out, norm2_w, w_ffn_in, w_ffn_out, norm_f_w, loss_target, m_c_ctx, m_ada_w, m_ada_b, m_norm1_w, m_w_in, m_gdn_conv_w, m_gdn_conv_b, m_gdn_a_log, m_gdn_dt_bias, m_gdn_norm_w, m_ssm_conv_w, m_ssm_conv_b, m_ssm_a_log, m_ssm_dt_bias, m_ssm_d, m_ssm_norm_w, m_w_br_gdn, m_w_br_ssm, m_w_out, m_norm2_w, m_w_ffn_in, m_w_ffn_out, m_norm_f_w, v_c_ctx, v_ada_w, v_ada_b, v_norm1_w, v_w_in, v_gdn_conv_w, v_gdn_conv_b, v_gdn_a_log, v_gdn_dt_bias, v_gdn_norm_w, v_ssm_conv_w, v_ssm_conv_b, v_ssm_a_log, v_ssm_dt_bias, v_ssm_d, v_ssm_norm_w, v_w_br_gdn, v_w_br_ssm, v_w_out, v_norm2_w, v_w_ffn_in, v_w_ffn_out, v_norm_f_w):
    given = dict(x=x, c=c, ctx=ctx, c_ctx=c_ctx, ada_w=ada_w, ada_b=ada_b, norm1_w=norm1_w, w_in=w_in, gdn_conv_w=gdn_conv_w, gdn_conv_b=gdn_conv_b, gdn_a_log=gdn_a_log, gdn_dt_bias=gdn_dt_bias, gdn_norm_w=gdn_norm_w, ssm_conv_w=ssm_conv_w, ssm_conv_b=ssm_conv_b, ssm_a_log=ssm_a_log, ssm_dt_bias=ssm_dt_bias, ssm_d=ssm_d, ssm_norm_w=ssm_norm_w, w_br_gdn=w_br_gdn, w_br_ssm=w_br_ssm, w_out=w_out, norm2_w=norm2_w, w_ffn_in=w_ffn_in, w_ffn_out=w_ffn_out, norm_f_w=norm_f_w, loss_target=loss_target, m_c_ctx=m_c_ctx, m_ada_w=m_ada_w, m_ada_b=m_ada_b, m_norm1_w=m_norm1_w, m_w_in=m_w_in, m_gdn_conv_w=m_gdn_conv_w, m_gdn_conv_b=m_gdn_conv_b, m_gdn_a_log=m_gdn_a_log, m_gdn_dt_bias=m_gdn_dt_bias, m_gdn_norm_w=m_gdn_norm_w, m_ssm_conv_w=m_ssm_conv_w, m_ssm_conv_b=m_ssm_conv_b, m_ssm_a_log=m_ssm_a_log, m_ssm_dt_bias=m_ssm_dt_bias, m_ssm_d=m_ssm_d, m_ssm_norm_w=m_ssm_norm_w, m_w_br_gdn=m_w_br_gdn, m_w_br_ssm=m_w_br_ssm, m_w_out=m_w_out, m_norm2_w=m_norm2_w, m_w_ffn_in=m_w_ffn_in, m_w_ffn_out=m_w_ffn_out, m_norm_f_w=m_norm_f_w, v_c_ctx=v_c_ctx, v_ada_w=v_ada_w, v_ada_b=v_ada_b, v_norm1_w=v_norm1_w, v_w_in=v_w_in, v_gdn_conv_w=v_gdn_conv_w, v_gdn_conv_b=v_gdn_conv_b, v_gdn_a_log=v_gdn_a_log, v_gdn_dt_bias=v_gdn_dt_bias, v_gdn_norm_w=v_gdn_norm_w, v_ssm_conv_w=v_ssm_conv_w, v_ssm_conv_b=v_ssm_conv_b, v_ssm_a_log=v_ssm_a_log, v_ssm_dt_bias=v_ssm_dt_bias, v_ssm_d=v_ssm_d, v_ssm_norm_w=v_ssm_norm_w, v_w_br_gdn=v_w_br_gdn, v_w_br_ssm=v_w_br_ssm, v_w_out=v_w_out, v_norm2_w=v_norm2_w, v_w_ffn_in=v_w_ffn_in, v_w_ffn_out=v_w_ffn_out, v_norm_f_w=v_norm_f_w)
    weights = {n: given[n] for n in TWIN_WEIGHTS}
    shared = {n: given[n] for n in SHARED_INPUTS}
    per_example = {n: given[n] for n in ['x', 'c', 'ctx']}
    grad_fn = _jax.value_and_grad(_loss, argnums=(0, 1))

    def one_microbatch(ex, loss_target):
        ex = dict(ex)
        diff = ex.pop(TWIN_DIFF_INPUT)
        return grad_fn(weights, diff, {**shared, **ex}, loss_target)

    if N_MICROBATCH == 1:
        loss, (grad_w, grad_x) = one_microbatch(per_example, given["loss_target"])
    else:
        def body(carry, xs):
            loss_sum, grad_sum = carry
            l_k, (gw_k, gx_k) = one_microbatch(xs[0], xs[1])
            with _jax.named_scope("update"):
                return (loss_sum + l_k, _jax.tree.map(_jnp.add, grad_sum, gw_k)), gx_k

        init = (_jnp.zeros((), _jnp.float32), _jax.tree.map(_jnp.zeros_like, weights))
        (loss, grad_w), grad_x = _jax.lax.scan(body, init, (per_example, given["loss_target"]))
    with _jax.named_scope("update"):
        delta_w, new_m, new_v = {}, {}, {}
        for n in TWIN_WEIGHTS:
            delta_w[n], new_m[n], new_v[n] = _adamw(weights[n], grad_w[n], given["m_" + n], given["v_" + n])
    return (loss, grad_x, *[grad_w[n] for n in TWIN_WEIGHTS], *[delta_w[n] for n in TWIN_WEIGHTS],
            *[new_m[n] for n in TWIN_WEIGHTS], *[new_v[n] for n in TWIN_WEIGHTS])
```

```python
import functools
import math

import jax
import jax.numpy as jnp
from jax import lax
from jax.experimental import pallas as pl
from jax.experimental.pallas import tpu as pltpu

F32 = jnp.float32
HI = lax.Precision.HIGHEST
_MXU_DTYPE = jnp.bfloat16

D_MODEL = 1024
EPS = 1e-6
CHUNK = 64
GRID_W = 64
GDN_HEADS = 8
GDN_DK = 128
SSM_HEADS = 32
SSM_PAIRS = 16
D_FF = 2816
D_IN_PROJ = 11360
P_QKV, P_ZG, P_ZS, P_XBC, P_GATE, P_SMALL, P_TOTAL = 0, 3072, 4096, 6144, 9216, 11264, 11776
ADAM_LR, ADAM_B1, ADAM_B2, ADAM_EPS, ADAM_WD, ADAM_STEP = 0.001, 0.9, 0.999, 1e-08, 0.01, 10

VMEM_LIMIT = 48 * 1024 * 1024

WEIGHTS = ['c_ctx', 'ada_w', 'ada_b', 'norm1_w', 'w_in', 'gdn_conv_w', 'gdn_conv_b', 'gdn_a_log', 'gdn_dt_bias',
           'gdn_norm_w', 'ssm_conv_w', 'ssm_conv_b', 'ssm_a_log', 'ssm_dt_bias', 'ssm_d', 'ssm_norm_w', 'w_br_gdn',
           'w_br_ssm', 'w_out', 'norm2_w', 'w_ffn_in', 'w_ffn_out', 'norm_f_w']
SHARD_AXIS = {'ada_w': 1, 'w_in': 1, 'gdn_conv_w': 1, 'ssm_conv_w': 1, 'w_br_gdn': 0, 'w_br_ssm': 0, 'w_out': 0,
              'w_ffn_in': 1, 'w_ffn_out': 0}
MXU_WEIGHTS = ['ada_w', 'w_in', 'w_br_gdn', 'w_br_ssm', 'w_out', 'w_ffn_in', 'w_ffn_out']
F32_GATHERED = ['gdn_conv_w', 'ssm_conv_w']


def _cp(sem):
    return pltpu.CompilerParams(dimension_semantics=sem, vmem_limit_bytes=VMEM_LIMIT)


def _pick(n, cands):
    for c in cands:
        if n % c == 0:
            return c
    raise ValueError(f"no tile for {n}")


def _matmul(a, b, form, name, out_dtype=F32):
    if form == 'nn':
        (m, k), (k2, n) = a.shape, b.shape
    elif form == 'nt':
        (m, k), (n, k2) = a.shape, b.shape
    else:
        (k, m), (k2, n) = a.shape, b.shape
    assert k == k2, (a.shape, b.shape, form)
    if form == 'tn':
        tm = m if m <= 3072 else _pick(m, (1024, 512, 256, 128))
        tn = _pick(n, (512, 256, 128))
        tk = _pick(k, (768, 512, 256, 128))
    else:
        tm = _pick(m, (768, 512, 256, 128))
        tn = _pick(n, (512, 256, 128))
        tk = k if k <= 2048 else _pick(k, (512, 256, 128))
    nk = k // tk
    dims = {'nn': (((1,), (0,)), ((), ())), 'nt': (((1,), (1,)), ((), ())), 'tn': (((0,), (0,)), ((), ()))}[form]

    def body(a_ref, b_ref, o_ref, acc_ref):
        kk = pl.program_id(2)
        part = lax.dot_general(a_ref[...].astype(_MXU_DTYPE), b_ref[...].astype(_MXU_DTYPE), dims,
                               preferred_element_type=F32)

        @pl.when(kk == 0)
        def _():
            acc_ref[...] = part

        @pl.when(kk > 0)
        def _():
            acc_ref[...] += part

        @pl.when(kk == nk - 1)
        def _():
            o_ref[...] = acc_ref[...].astype(o_ref.dtype)

    if form == 'nn':
        a_spec = pl.BlockSpec((tm, tk), lambda i, j, kk: (i, kk))
        b_spec = pl.BlockSpec((tk, tn), lambda i, j, kk: (kk, j))
    elif form == 'nt':
        a_spec = pl.BlockSpec((tm, tk), lambda i, j, kk: (i, kk))
        b_spec = pl.BlockSpec((tn, tk), lambda i, j, kk: (j, kk))
    else:
        a_spec = pl.BlockSpec((tk, tm), lambda i, j, kk: (kk, i))
        b_spec = pl.BlockSpec((tk, tn), lambda i, j, kk: (kk, j))
    return pl.pallas_call(
        body, grid=(m // tm, n // tn, nk), in_specs=[a_spec, b_spec],
        out_specs=pl.BlockSpec((tm, tn), lambda i, j, kk: (i, j)),
        out_shape=jax.ShapeDtypeStruct((m, n), out_dtype),
        scratch_shapes=[pltpu.VMEM((tm, tn), F32)],
        compiler_params=_cp(("parallel", "parallel", "arbitrary")), name=name)(a, b)


class Op:
    def __init__(self, arr, bs, im, load=None, dtype=None):
        self.arr, self.bs, self.im = arr, bs, im
        self.load = load or (lambda r: r[...].astype(F32))
        self.dtype = dtype

    def spec(self):
        return pl.BlockSpec(self.bs, self.im)


def _sum_dirs(r):
    return r[0].astype(F32) + r[1].astype(F32)


def _tw_fwd(name, fn, grid, ins, outs):
    n_in = len(ins)

    def body(*refs):
        info = (pl.program_id(0), pl.program_id(1))
        vals = [op.load(r) for op, r in zip(ins, refs[:n_in])]
        res = fn(info, *vals)
        for r, v in zip(refs[n_in:], res):
            r[...] = v.astype(r.dtype)

    return pl.pallas_call(
        body, grid=grid, in_specs=[op.spec() for op in ins], out_specs=[op.spec() for op in outs],
        out_shape=[jax.ShapeDtypeStruct(*op.arr) for op in outs],
        compiler_params=_cp(("parallel", "arbitrary")), name=name)(*[op.arr for op in ins])


def _tw_bwd(name, fn, grid, tok, par, cots, tok_out, par_out):
    n_tok, n_par, n_cot = len(tok), len(par), len(cots)
    tok_keep = [i for i, o in enumerate(tok_out) if o is not None]
    par_keep = [i for i, o in enumerate(par_out) if o is not None]
    out_ops = [tok_out[i] for i in tok_keep] + [par_out[i] for i in par_keep]

    def body(*refs):
        info = (pl.program_id(0), pl.program_id(1))
        it = iter(refs)
        tok_v = [op.load(next(it)) for op in tok]
        par_v = [op.load(next(it)) for op in par]
        cot_v = [op.load(next(it)) for op in cots]
        _, pull = jax.vjp(lambda *a: fn(info, *a), *tok_v, *par_v)
        grads = pull(tuple(cot_v))
        for i in tok_keep:
            r = next(it)
            r[...] = grads[i].astype(r.dtype)
        first = pl.program_id(1) == 0
        for i in par_keep:
            r = next(it)
            g = grads[n_tok + i]

            @pl.when(first)
            def _(r=r, g=g):
                r[...] = g.astype(r.dtype)

            @pl.when(jnp.logical_not(first))
            def _(r=r, g=g):
                r[...] += g.astype(r.dtype)

    ops = tok + par + cots
    return pl.pallas_call(
        body, grid=grid, in_specs=[op.spec() for op in ops], out_specs=[op.spec() for op in out_ops],
        out_shape=[jax.ShapeDtypeStruct(*op.arr) for op in out_ops],
        compiler_params=_cp(("parallel", "arbitrary")), name=name)(*[op.arr for op in ops])


def _silu(x):
    return x * jax.nn.sigmoid(x)


def _rms(x):
    return x * lax.rsqrt(jnp.mean(x * x, axis=-1, keepdims=True) + EPS)


@functools.partial(jax.custom_vjp, nondiff_argnums=(1,))
def _shift_rows(x, k):
    return pltpu.roll(x, k % x.shape[0], 0)


def _shift_rows_fwd(x, k):
    return _shift_rows(x, k), None


def _shift_rows_bwd(k, _, g):
    return (_shift_rows(g, -k),)


_shift_rows.defvjp(_shift_rows_fwd, _shift_rows_bwd)


def _prenorm_fn(nctx_t, info, x, w, sc8, sh8):
    is_ctx = info[1] < nctx_t
    sc = jnp.where(is_ctx, sc8[1:2], sc8[0:1])
    sh = jnp.where(is_ctx, sh8[1:2], sh8[0:1])
    return (_rms(x) * w * (1.0 + sc) + sh,)


def _conv_fn(nctx_t, mode, info, x, w, b):
    n, c = x.shape
    is_ctx = info[1] < nctx_t
    idx = lax.broadcasted_iota(jnp.int32, (n, 1), 0)
    rr = jnp.where(is_ctx, idx, idx % GRID_W)
    first = rr == 0
    last = rr == jnp.where(is_ctx, n - 1, GRID_W - 1)
    prev = jnp.where(first, 0.0, _shift_rows(x, 1))
    nxt = jnp.where(last, 0.0, _shift_rows(x, -1))
    y = b + prev * w[0:1] + x * w[1:2] + nxt * w[2:3]
    y = _silu(y)
    if mode == 'none':
        return (y,)
    scale = GDN_DK ** -0.5 if mode == 'q' else 1.0
    outs = []
    for h in range(c // 128):
        yh = y[:, h * 128:(h + 1) * 128]
        outs.append(yh * lax.rsqrt(jnp.sum(yh * yh, axis=-1, keepdims=True) + EPS) * scale)
    return (jnp.concatenate(outs, axis=1),)


def _act_fn(info, x, p0, p1):
    lane = lax.broadcasted_iota(jnp.int32, x.shape, 1)
    sp = jax.nn.softplus(x + p1)
    g = -jnp.exp(p0) * sp
    bt = jax.nn.sigmoid(x)
    return (jnp.where(lane < 16, g, jnp.where(lane < 32, bt, jnp.where(lane < 96, sp, 0.0))),)


def _mixg_fn(info, o, zg, gnw):
    outs = []
    for h in range(GDN_HEADS):
        outs.append(_rms(o[:, h * 128:(h + 1) * 128]) * gnw)
    return (jnp.concatenate(outs, axis=1) * _silu(zg),)


def _mixs_fn(info, y, xs, zs, dl, snw):
    yy = (y + dl * xs) * _silu(zs)
    outs = []
    for g in range(4):
        outs.append(_rms(yy[:, g * 512:(g + 1) * 512]))
    return (jnp.concatenate(outs, axis=1) * snw,)


def _merge_fn(info, gg, gs, pg, ps):
    return (jax.nn.sigmoid(gg) * pg + jax.nn.sigmoid(gs) * ps,)


def _norm2_fn(info, xt, mo, g8, w, sc8, sh8):
    h1 = xt + g8[0:1] * mo
    return (h1, _rms(h1) * w * (1.0 + sc8[0:1]) + sh8[0:1])


def _swiglu_fn(info, ug, uu):
    return (_silu(ug) * uu,)


def _mm(a, b):
    return lax.dot_general(a, b, (((1,), (0,)), ((), ())), precision=HI, preferred_element_type=F32)


def _mm_nt(a, b):
    return lax.dot_general(a, b, (((1,), (1,)), ((), ())), precision=HI, preferred_element_type=F32)


def _mm_tn(a, b):
    return lax.dot_general(a, b, (((0,), (0,)), ((), ())), precision=HI, preferred_element_type=F32)


def _order_masks(d):
    i = lax.broadcasted_iota(jnp.int32, (CHUNK, CHUNK), 0)
    j = lax.broadcasted_iota(jnp.int32, (CHUNK, CHUNK), 1)
    s = jnp.where(d == 0, 1, -1) * (i - j)
    return (s >= 0).astype(F32), (s > 0).astype(F32), (i == j).astype(F32)


@jax.custom_vjp
def _unit_tri_inv(a):
    i = lax.broadcasted_iota(jnp.int32, a.shape, 0)
    j = lax.broadcasted_iota(jnp.int32, a.shape, 1)
    eye = (i == j).astype(F32)
    p = -a
    t = eye + p
    for _ in range(5):
        p = _mm(p, p)
        t = t + _mm(t, p)
    return t


def _uti_fwd(a):
    t = _unit_tri_inv(a)
    return t, t


def _uti_bwd(t, g):
    return (-_mm_tn(t, _mm_nt(g, t)),)


_unit_tri_inv.defvjp(_uti_fwd, _uti_bwd)


def _lane_col(blk, lane_idx):
    lane = lax.broadcasted_iota(jnp.int32, blk.shape, 1)
    return jnp.sum(jnp.where(lane == lane_idx, blk, 0.0), axis=1, keepdims=True)


def _decay_mat(cum, incl):
    cb = jnp.broadcast_to(cum, (CHUNK, CHUNK))
    return jnp.exp(jnp.minimum(cb - cb.T, 0.0)) * incl


def _gdn_chunk(d, h, q, k, v, spb, s):
    incl, strict, _ = _order_masks(d)
    g = _lane_col(spb, d * GDN_HEADS + h)
    beta = _lane_col(spb, 16 + d * GDN_HEADS + h)
    gcum = _mm(incl, g)
    decay = _decay_mat(gcum, incl)
    kb = k * beta
    t = _unit_tri_inv(_mm_nt(kb, k) * decay * strict)
    egc = jnp.exp(gcum)
    w = _mm(t, kb * egc)
    u = _mm(t, v * beta)
    vn = u - _mm(w, s)
    o = _mm(q * egc, s) + _mm(_mm_nt(q, k) * decay, vn)
    glast = jnp.sum(g, axis=0, keepdims=True)
    s_new = s * jnp.exp(glast) + _mm_tn(k * jnp.exp(glast - gcum), vn)
    return o, s_new


def _ssd_chunk(d, pair, x, bm, cm, spb, al, hst):
    incl, _, _ = _order_masks(d)
    lane = lax.broadcasted_iota(jnp.int32, (CHUNK, 128), 1)
    low = lane < 64
    a_l = -jnp.exp(al)
    dt0 = _lane_col(spb, 32 + d * SSM_HEADS + 2 * pair)
    dt1 = _lane_col(spb, 33 + d * SSM_HEADS + 2 * pair)
    adt0 = dt0 * a_l[:, 0:1]
    adt1 = dt1 * a_l[:, 64:65]
    ac0 = _mm(incl, adt0)
    ac1 = _mm(incl, adt1)
    seg0 = _decay_mat(ac0, incl)
    seg1 = _decay_mat(ac1, incl)
    xdt = x * jnp.where(low, dt0, dt1)
    cb = _mm_nt(cm, bm)
    y_diag = _mm(cb * seg0, jnp.where(low, xdt, 0.0)) + _mm(cb * seg1, jnp.where(low, 0.0, xdt))
    y_off = jnp.exp(jnp.where(low, ac0, ac1)) * _mm_nt(cm, hst)
    al0 = jnp.sum(adt0, axis=0, keepdims=True)
    al1 = jnp.sum(adt1, axis=0, keepdims=True)
    e2 = jnp.exp(jnp.where(low, al0 - ac0, al1 - ac1))
    new = _mm_tn(xdt * e2, bm)
    row = lax.broadcasted_iota(jnp.int32, (128, 1), 0)
    h_new = hst * jnp.exp(jnp.where(row < 64, al0, al1)) + new
    return y_diag + y_off, h_new


def _chunk_of(d, p, nctx, nc):
    return jnp.where(d == 0, p, jnp.where(p < nctx, nctx - 1 - p, nctx + nc - 1 - p))


def _gdn_scan_fwd(q, k, v, sp, nctx):
    t = q.shape[0]
    nc = t // CHUNK

    def body(q_ref, k_ref, v_ref, sp_ref, o_ref, ss_ref, s_scr):
        d, p, h = pl.program_id(0), pl.program_id(1), pl.program_id(2)

        @pl.when(p == 0)
        def _():
            s_scr[h] = jnp.zeros((128, 128), F32)

        s = s_scr[h]
        ss_ref[0, 0, 0] = s
        o, s_new = _gdn_chunk(d, h, q_ref[...], k_ref[...], v_ref[...], sp_ref[...], s)
        o_ref[0] = o
        s_scr[h] = s_new

    cm = lambda d, p, h: (_chunk_of(d, p, nctx, nc), h)
    return pl.pallas_call(
        body, grid=(2, nc, GDN_HEADS),
        in_specs=[pl.BlockSpec((CHUNK, 128), cm)] * 3 + [pl.BlockSpec((CHUNK, 128), lambda d, p, h: (_chunk_of(d, p, nctx, nc), 0))],
        out_specs=[pl.BlockSpec((1, CHUNK, 128), lambda d, p, h: (d, _chunk_of(d, p, nctx, nc), h)),
                   pl.BlockSpec((1, 1, 1, 128, 128), lambda d, p, h: (d, p, h, 0, 0))],
        out_shape=[jax.ShapeDtypeStruct((2, t, 1024), F32), jax.ShapeDtypeStruct((2, nc, GDN_HEADS, 128, 128), F32)],
        scratch_shapes=[pltpu.VMEM((GDN_HEADS, 128, 128), F32)],
        compiler_params=_cp(("arbitrary", "arbitrary", "arbitrary")), name="gdn_scan_fwd")(q, k, v, sp)


def _gdn_scan_bwd(q, k, v, sp, ss, do, nctx):
    t = q.shape[0]
    nc = t // CHUNK

    def body(q_ref, k_ref, v_ref, sp_ref, ss_ref, do_ref, dq_ref, dk_ref, dv_ref, dsp_ref, ds_scr):
        d, pr, h = pl.program_id(0), pl.program_id(1), pl.program_id(2)

        @pl.when(pr == 0)
        def _():
            ds_scr[h] = jnp.zeros((128, 128), F32)

        f = functools.partial(_gdn_chunk, d, h)
        _, pull = jax.vjp(f, q_ref[...], k_ref[...], v_ref[...], sp_ref[...], ss_ref[0, 0, 0])
        dq, dk, dv, dsp, ds = pull((do_ref[...], ds_scr[h]))
        dq_ref[0] = dq
        dk_ref[0] = dk
        dv_ref[0] = dv
        ds_scr[h] = ds

        @pl.when(h == 0)
        def _():
            dsp_ref[0] = dsp

        @pl.when(h > 0)
        def _():
            dsp_ref[0] += dsp

    ch = lambda d, pr: _chunk_of(d, nc - 1 - pr, nctx, nc)
    tok = pl.BlockSpec((CHUNK, 128), lambda d, pr, h: (ch(d, pr), h))
    out3 = pl.BlockSpec((1, CHUNK, 128), lambda d, pr, h: (d, ch(d, pr), h))
    return pl.pallas_call(
        body, grid=(2, nc, GDN_HEADS),
        in_specs=[tok, tok, tok, pl.BlockSpec((CHUNK, 128), lambda d, pr, h: (ch(d, pr), 0)),
                  pl.BlockSpec((1, 1, 1, 128, 128), lambda d, pr, h: (d, nc - 1 - pr, h, 0, 0)), tok],
        out_specs=[out3, out3, out3, pl.BlockSpec((1, CHUNK, 128), lambda d, pr, h: (d, ch(d, pr), 0))],
        out_shape=[jax.ShapeDtypeStruct((2, t, 1024), F32)] * 3 + [jax.ShapeDtypeStruct((2, t, 128), F32)],
        scratch_shapes=[pltpu.VMEM((GDN_HEADS, 128, 128), F32)],
        compiler_params=_cp(("arbitrary", "arbitrary", "arbitrary")), name="gdn_scan_bwd")(q, k, v, sp, ss, do)


def _ssd_scan_fwd(xs, bm, cm, sp, al, nctx):
    t = xs.shape[0]
    nc = t // CHUNK

    def body(x_ref, b_ref, c_ref, sp_ref, al_ref, y_ref, hs_ref, h_scr):
        d, p, pair = pl.program_id(0), pl.program_id(1), pl.program_id(2)

        @pl.when(p == 0)
        def _():
            h_scr[pair] = jnp.zeros((128, 128), F32)

        hst = h_scr[pair]
        hs_ref[0, 0, 0] = hst
        y, h_new = _ssd_chunk(d, pair, x_ref[...], b_ref[...], c_ref[...], sp_ref[...], al_ref[0], hst)
        y_ref[0] = y
        h_scr[pair] = h_new

    ch = lambda d, p: _chunk_of(d, p, nctx, nc)
    return pl.pallas_call(
        body, grid=(2, nc, SSM_PAIRS),
        in_specs=[pl.BlockSpec((CHUNK, 128), lambda d, p, r: (ch(d, p), r)),
                  pl.BlockSpec((CHUNK, 128), lambda d, p, r: (ch(d, p), r // 4)),
                  pl.BlockSpec((CHUNK, 128), lambda d, p, r: (ch(d, p), r // 4)),
                  pl.BlockSpec((CHUNK, 128), lambda d, p, r: (ch(d, p), 0)),
                  pl.BlockSpec((1, 1, 128), lambda d, p, r: (d * SSM_PAIRS + r, 0, 0))],
        out_specs=[pl.BlockSpec((1, CHUNK, 128), lambda d, p, r: (d, ch(d, p), r)),
                   pl.BlockSpec((1, 1, 1, 128, 128), lambda d, p, r: (d, p, r, 0, 0))],
        out_shape=[jax.ShapeDtypeStruct((2, t, 2048), F32), jax.ShapeDtypeStruct((2, nc, SSM_PAIRS, 128, 128), F32)],
        scratch_shapes=[pltpu.VMEM((SSM_PAIRS, 128, 128), F32)],
        compiler_params=_cp(("arbitrary", "arbitrary", "arbitrary")), name="ssd_scan_fwd")(xs, bm, cm, sp, al)


def _ssd_scan_bwd(xs, bm, cm, sp, al, hs, dy, nctx):
    t = xs.shape[0]
    nc = t // CHUNK

    def body(x_ref, b_ref, c_ref, sp_ref, al_ref, hs_ref, dy_ref, dx_ref, db_ref, dc_ref, dsp_ref, dal_ref, dh_scr):
        d, pr, pair = pl.program_id(0), pl.program_id(1), pl.program_id(2)

        @pl.when(pr == 0)
        def _():
            dh_scr[pair] = jnp.zeros((128, 128), F32)

        @pl.when((d == 0) & (pr == 0) & (pair == 0))
        def _():
            dal_ref[...] = jnp.zeros(dal_ref.shape, F32)

        f = functools.partial(_ssd_chunk, d, pair)
        _, pull = jax.vjp(f, x_ref[...], b_ref[...], c_ref[...], sp_ref[...], al_ref[0], hs_ref[0, 0, 0])
        dx, db, dc, dsp, dal, dh = pull((dy_ref[...], dh_scr[pair]))
        dx_ref[0] = dx
        dh_scr[pair] = dh
        idx = d * SSM_PAIRS + pair
        dal_ref[idx] = dal_ref[idx] + dal

        @pl.when(pair % 4 == 0)
        def _():
            db_ref[0] = db
            dc_ref[0] = dc

        @pl.when(pair % 4 != 0)
        def _():
            db_ref[0] += db
            dc_ref[0] += dc

        @pl.when(pair == 0)
        def _():
            dsp_ref[0] = dsp

        @pl.when(pair > 0)
        def _():
            dsp_ref[0] += dsp

    ch = lambda d, pr: _chunk_of(d, nc - 1 - pr, nctx, nc)
    tok = pl.BlockSpec((CHUNK, 128), lambda d, pr, r: (ch(d, pr), r))
    grp = pl.BlockSpec((CHUNK, 128), lambda d, pr, r: (ch(d, pr), r // 4))
    return pl.pallas_call(
        body, grid=(2, nc, SSM_PAIRS),
        in_specs=[tok, grp, grp, pl.BlockSpec((CHUNK, 128), lambda d, pr, r: (ch(d, pr), 0)),
                  pl.BlockSpec((1, 1, 128), lambda d, pr, r: (d * SSM_PAIRS + r, 0, 0)),
                  pl.BlockSpec((1, 1, 1, 128, 128), lambda d, pr, r: (d, nc - 1 - pr, r, 0, 0)), tok],
        out_specs=[pl.BlockSpec((1, CHUNK, 128), lambda d, pr, r: (d, ch(d, pr), r)),
                   pl.BlockSpec((1, CHUNK, 128), lambda d, pr, r: (d, ch(d, pr), r // 4)),
                   pl.BlockSpec((1, CHUNK, 128), lambda d, pr, r: (d, ch(d, pr), r // 4)),
                   pl.BlockSpec((1, CHUNK, 128), lambda d, pr, r: (d, ch(d, pr), 0)),
                   pl.BlockSpec((2 * SSM_PAIRS, 1, 128), lambda d, pr, r: (0, 0, 0))],
        out_shape=[jax.ShapeDtypeStruct((2, t, 2048), F32), jax.ShapeDtypeStruct((2, t, 512), F32),
                   jax.ShapeDtypeStruct((2, t, 512), F32), jax.ShapeDtypeStruct((2, t, 128), F32),
                   jax.ShapeDtypeStruct((2 * SSM_PAIRS, 1, 128), F32)],
        scratch_shapes=[pltpu.VMEM((SSM_PAIRS, 128, 128), F32)],
        compiler_params=_cp(("arbitrary", "arbitrary", "arbitrary")), name="ssd_scan_bwd")(xs, bm, cm, sp, al, hs, dy)


def _ada_fwd(cc, ada_w, ada_b):
    n = ada_w.shape[1]
    tn = 512

    def body(c_ref, w_ref, b_ref, o_ref):
        s = _silu(c_ref[...]).astype(_MXU_DTYPE)
        o_ref[...] = jnp.dot(s, w_ref[...].astype(_MXU_DTYPE), preferred_element_type=F32) + b_ref[...]

    return pl.pallas_call(
        body, grid=(n // tn,),
        in_specs=[pl.BlockSpec((8, D_MODEL), lambda j: (0, 0)), pl.BlockSpec((D_MODEL, tn), lambda j: (0, j)),
                  pl.BlockSpec((1, tn), lambda j: (0, j))],
        out_specs=pl.BlockSpec((8, tn), lambda j: (0, j)), out_shape=jax.ShapeDtypeStruct((8, n), F32),
        compiler_params=_cp(("parallel",)), name="ada_fwd")(cc, ada_w, ada_b)


def _ada_bwd(cc, ada_w, dmods):
    n = ada_w.shape[1]
    tn = 512
    nj = n // tn

    def body(c_ref, w_ref, g_ref, dw_ref, db_ref, dc_ref):
        j = pl.program_id(0)
        c = c_ref[...]
        g = g_ref[...]
        row = lax.broadcasted_iota(jnp.int32, g.shape, 0)
        g = jnp.where(row < 2, g, 0.0)
        s, pull = jax.vjp(_silu, c)
        dw_ref[...] = lax.dot_general(s.astype(_MXU_DTYPE), g.astype(_MXU_DTYPE), (((0,), (0,)), ((), ())),
                                      preferred_element_type=F32)
        db_ref[...] = jnp.sum(g, axis=0, keepdims=True)
        ds = lax.dot_general(g.astype(_MXU_DTYPE), w_ref[...].astype(_MXU_DTYPE), (((1,), (1,)), ((), ())),
                             preferred_element_type=F32)

        @pl.when(j == 0)
        def _():
            dc_ref[...] = ds

        @pl.when(j > 0)
        def _():
            dc_ref[...] += ds

        @pl.when(j == nj - 1)
        def _():
            dc_ref[...] = pull(dc_ref[...])[0]

    return pl.pallas_call(
        body, grid=(nj,),
        in_specs=[pl.BlockSpec((8, D_MODEL), lambda j: (0, 0)), pl.BlockSpec((D_MODEL, tn), lambda j: (0, j)),
                  pl.BlockSpec((8, tn), lambda j: (0, j))],
        out_specs=[pl.BlockSpec((D_MODEL, tn), lambda j: (0, j)), pl.BlockSpec((1, tn), lambda j: (0, j)),
                   pl.BlockSpec((8, D_MODEL), lambda j: (0, 0))],
        out_shape=[jax.ShapeDtypeStruct((D_MODEL, n), F32), jax.ShapeDtypeStruct((1, n), F32),
                   jax.ShapeDtypeStruct((8, D_MODEL), F32)],
        compiler_params=_cp(("arbitrary",)), name="ada_bwd")(cc, ada_w, dmods)


def _tail(h1, ff, mods, wf, tgt, nctx_t, tl):
    t = h1.shape[0]
    nt = t // tl

    def loss_fn(valid, h1v, ffv, g8, w, tg):
        h2 = h1v + g8[0:1] * ffv
        y = _rms(h2) * w
        err = (y - tg) ** 2
        return 0.5 * jnp.sum(jnp.mean(err, axis=-1, keepdims=True), axis=0, keepdims=True) * valid

    def body(h1_ref, ff_ref, g_ref, w_ref, t_ref, loss_ref, dh_ref, dff_ref, dg_ref, dw_ref):
        i = pl.program_id(0)
        valid = jnp.where(i < nctx_t, 0.0, 1.0)
        lv, pull = jax.vjp(functools.partial(loss_fn, valid), h1_ref[...], ff_ref[...].astype(F32), g_ref[...],
                           w_ref[...], t_ref[...])
        dh, dff, dg, dw, _ = pull(jnp.ones((1, 1), F32))
        dh_ref[...] = dh
        dff_ref[...] = dff.astype(dff_ref.dtype)
        lb = jnp.broadcast_to(lv, loss_ref.shape)

        @pl.when(i == 0)
        def _():
            loss_ref[...] = lb
            dg_ref[...] = dg
            dw_ref[...] = dw

        @pl.when(i > 0)
        def _():
            loss_ref[...] += lb
            dg_ref[...] += dg
            dw_ref[...] += dw

    tok = pl.BlockSpec((tl, D_MODEL), lambda i: (i, 0))
    return pl.pallas_call(
        body, grid=(nt,),
        in_specs=[tok, tok, pl.BlockSpec((8, D_MODEL), lambda i: (0, 5)), pl.BlockSpec((1, D_MODEL), lambda i: (0, 0)),
                  pl.BlockSpec((tl, D_MODEL), lambda i: (jnp.maximum(i - nctx_t, 0), 0))],
        out_specs=[pl.BlockSpec((8, 128), lambda i: (0, 0)), tok, tok, pl.BlockSpec((8, D_MODEL), lambda i: (0, 0)),
                   pl.BlockSpec((1, D_MODEL), lambda i: (0, 0))],
        out_shape=[jax.ShapeDtypeStruct((8, 128), F32), jax.ShapeDtypeStruct((t, D_MODEL), F32),
                   jax.ShapeDtypeStruct((t, D_MODEL), _MXU_DTYPE), jax.ShapeDtypeStruct((8, D_MODEL), F32),
                   jax.ShapeDtypeStruct((1, D_MODEL), F32)],
        compiler_params=_cp(("arbitrary",)), name="tail_loss")(h1, ff, mods, wf, tgt)


def _pack_w_in(w_in):
    s = [0, 3072, 4096, 4112, 4128, 6176, 9248, 9312, 11360]
    qkv, zg, a, b, zs, xbc, dt, gate = [w_in[:, s[i]:s[i + 1]] for i in range(8)]
    pad = jnp.zeros((w_in.shape[0], P_TOTAL - P_SMALL - 96), w_in.dtype)
    return jnp.concatenate([qkv, zg, zs, xbc, gate, a, b, dt, pad], axis=1)


def _unpack_w_in(g):
    return jnp.concatenate([g[:, 0:3072], g[:, 3072:4096], g[:, P_SMALL:P_SMALL + 16], g[:, P_SMALL + 16:P_SMALL + 32],
                            g[:, 4096:6144], g[:, 6144:9216], g[:, P_SMALL + 32:P_SMALL + 96], g[:, 9216:11264]], axis=1)


def _local_step(x, c, ctx, tgt, W):
    lc, l = ctx.shape[0], x.shape[0]
    t = lc + l
    tl = 256
    assert lc == tl and l % tl == 0 and lc % CHUNK == 0
    nt, nctx_t, nctx = t // tl, lc // tl, lc // CHUNK
    act = _MXU_DTYPE
    r1 = lambda v: v.reshape(1, -1)

    xt = jnp.concatenate([ctx, x], axis=0)
    cc = jnp.concatenate([c, r1(W['c_ctx']), jnp.zeros((6, D_MODEL), F32)], axis=0)
    mods = _ada_fwd(cc, W['ada_w'], r1(W['ada_b']))

    def mod(kk):
        return Op(mods, (8, D_MODEL), lambda j, i, kk=kk: (0, kk))

    def tokop(arr, w=D_MODEL, off=0, tl_=tl, load=None):
        if arr.ndim == 3:
            return Op(arr, (2, tl_, w), lambda j, i: (0, i, off + j), load=_sum_dirs)
        return Op(arr, (tl_, w), lambda j, i: (i, off + j), load=load)

    def outop(n, dtype, w=D_MODEL, tl_=tl):
        return Op(((t, n), dtype), (tl_, w), lambda j, i: (i, j))

    def parop(arr, w, off=0):
        return Op(arr, (arr.shape[0], w), lambda j, i: (0, off + j))

    def parout(rows, n, w):
        return Op(((rows, n), F32), (rows, w), lambda j, i: (0, j))

    n1w = r1(W['norm1_w'])
    pre_fn = functools.partial(_prenorm_fn, nctx_t)
    pre_tok, pre_par = [tokop(xt)], [parop(n1w, D_MODEL), mod(1), mod(0)]
    (a,) = _tw_fwd("prenorm_fwd", pre_fn, (1, nt), pre_tok + pre_par, [outop(D_MODEL, act)])
    wp = _pack_w_in(W['w_in'])
    proj = _matmul(a, wp, 'nn', "in_proj")

    gcw, gcb = W['gdn_conv_w'], r1(W['gdn_conv_b'])
    scw, scb = W['ssm_conv_w'], r1(W['ssm_conv_b'])
    conv_parts = {}

    def conv_part(name, mode, poff, cw, cb, woff, width):
        fn = functools.partial(_conv_fn, nctx_t, mode)
        tok_ = [tokop(proj, 512, poff // 512)]
        par_ = [parop(cw, 512, woff // 512), parop(cb, 512, woff // 512)]
        conv_parts[name] = (fn, tok_, par_, width)
        (o,) = _tw_fwd("conv_" + name, fn, (width // 512, nt), tok_ + par_, [outop(width, F32, 512)])
        return o

    q = conv_part('q', 'q', P_QKV, gcw, gcb, 0, 1024)
    k = conv_part('k', 'k', P_QKV + 1024, gcw, gcb, 1024, 1024)
    v = conv_part('v', 'none', P_QKV + 2048, gcw, gcb, 2048, 1024)
    xs = conv_part('xs', 'none', P_XBC, scw, scb, 0, 2048)
    bm = conv_part('bm', 'none', P_XBC + 2048, scw, scb, 2048, 512)
    cm = conv_part('cm', 'none', P_XBC + 2560, scw, scb, 2560, 512)

    z16 = jnp.zeros((16,), F32)
    p0 = jnp.concatenate([W['gdn_a_log'].reshape(-1), jnp.zeros((112,), F32)]).reshape(1, 128)
    p1 = jnp.concatenate([W['gdn_dt_bias'].reshape(-1), z16, W['ssm_dt_bias'].reshape(-1),
                          jnp.zeros((32,), F32)]).reshape(1, 128)
    act_tok, act_par = [tokop(proj, 128, P_SMALL // 128)], [parop(p0, 128), parop(p1, 128)]
    (sp,) = _tw_fwd("small_act", _act_fn, (1, nt), act_tok + act_par, [outop(128, F32, 128)])

    al = jnp.repeat(W['ssm_a_log'].reshape(2, SSM_PAIRS, 2), 64, axis=-1).reshape(2 * SSM_PAIRS, 1, 128)
    o2, ss = _gdn_scan_fwd(q, k, v, sp, nctx)
    y2, hs = _ssd_scan_fwd(xs, bm, cm, sp, al, nctx)

    tlm = 128
    ntm = t // tlm
    gnw = r1(W['gdn_norm_w'])
    dl = jnp.repeat(W['ssm_d'], 64).reshape(1, 2048)
    snw = r1(W['ssm_norm_w'])
    mg_tok = [tokop(o2, 1024, 0, tlm), tokop(proj, 1024, P_ZG // 1024, tlm)]
    mg_par = [parop(gnw, 128)]
    (og,) = _tw_fwd("mix_gdn", _mixg_fn, (1, ntm), mg_tok + mg_par, [outop(1024, act, 1024, tlm)])
    ms_tok = [tokop(y2, 2048, 0, tlm), tokop(xs, 2048, 0, tlm), tokop(proj, 2048, P_ZS // 2048, tlm)]
    ms_par = [parop(dl, 2048), parop(snw, 2048)]
    (yg,) = _tw_fwd("mix_ssm", _mixs_fn, (1, ntm), ms_tok + ms_par, [outop(2048, act, 2048, tlm)])

    pg = _matmul(og, W['w_br_gdn'], 'nn', "br_gdn")
    ps = _matmul(yg, W['w_br_ssm'], 'nn', "br_ssm")
    mr_tok = [tokop(proj, 1024, P_GATE // 1024), tokop(proj, 1024, P_GATE // 1024 + 1), tokop(pg), tokop(ps)]
    (mrg,) = _tw_fwd("merge", _merge_fn, (1, nt), mr_tok, [outop(1024, act)])
    mo = _matmul(mrg, W['w_out'], 'nn', "w_out")

    n2w = r1(W['norm2_w'])
    n2_tok, n2_par = [tokop(xt), tokop(mo)], [mod(2), parop(n2w, D_MODEL), mod(4), mod(3)]
    h1, f = _tw_fwd("norm2_fwd", _norm2_fn, (1, nt), n2_tok + n2_par, [outop(1024, F32), outop(1024, act)])
    u = _matmul(f, W['w_ffn_in'], 'nn', "ffn_in")
    sw_tok = [tokop(u, 256, 0), tokop(u, 256, D_FF // 256)]
    (sw,) = _tw_fwd("swiglu", _swiglu_fn, (D_FF // 256, nt), sw_tok, [outop(D_FF, act, 256)])
    ff = _matmul(sw, W['w_ffn_out'], 'nn', "ffn_out", out_dtype=F32)

    loss8, dh1, dff, dg2, dnf = _tail(h1, ff, mods, r1(W['norm_f_w']), tgt, nctx_t, tl)
    loss = loss8[0, 0]

    G = {}
    G['norm_f_w'] = dnf.reshape(-1)
    G['w_ffn_out'] = _matmul(sw, dff, 'tn', "d_ffn_out")
    dsw = _matmul(dff, W['w_ffn_out'], 'nt', "d_sw")
    dug, duu = _tw_bwd("swiglu_bwd", _swiglu_fn, (D_FF // 256, nt), sw_tok, [], [tokop(dsw, 256)],
                       [outop(D_FF, act, 256), outop(D_FF, act, 256)], [])
    du = jnp.concatenate([dug, duu], axis=1)
    G['w_ffn_in'] = _matmul(f, du, 'tn', "d_ffn_in")
    df = _matmul(du, W['w_ffn_in'], 'nt', "d_f")
    dxt1, dmo, dg1, dn2, dsc2, dsh2 = _tw_bwd(
        "norm2_bwd", _norm2_fn, (1, nt), n2_tok, n2_par, [tokop(dh1), tokop(df)],
        [outop(1024, F32), outop(1024, act)], [parout(8, 1024, 1024), parout(1, 1024, 1024), parout(8, 1024, 1024),
                                                 parout(8, 1024, 1024)])
    G['norm2_w'] = dn2.reshape(-1)
    G['w_out'] = _matmul(mrg, dmo, 'tn', "d_w_out")
    dmrg = _matmul(dmo, W['w_out'], 'nt', "d_mrg")
    dgg, dgs, dpg, dps = _tw_bwd("merge_bwd", _merge_fn, (1, nt), mr_tok, [], [tokop(dmrg)],
                                 [outop(1024, act), outop(1024, act), outop(1024, act), outop(1024, act)], [])
    G['w_br_gdn'] = _matmul(og, dpg, 'tn', "d_br_gdn")
    G['w_br_ssm'] = _matmul(yg, dps, 'tn', "d_br_ssm")
    dog = _matmul(dpg, W['w_br_gdn'], 'nt', "d_og")
    dyg = _matmul(dps, W['w_br_ssm'], 'nt', "d_yg")

    do, dzg, dgnw = _tw_bwd("mix_gdn_bwd", _mixg_fn, (1, ntm), [tokop(o2, 1024, 0, tlm), mg_tok[1]], mg_par,
                            [tokop(dog, 1024, 0, tlm)], [outop(1024, F32, 1024, tlm), outop(1024, act, 1024, tlm)],
                            [parout(1, 128, 128)])
    G['gdn_norm_w'] = dgnw.reshape(-1)
    dy, dxs_a, dzs, ddl, dsnw = _tw_bwd(
        "mix_ssm_bwd", _mixs_fn, (1, ntm), ms_tok, ms_par, [tokop(dyg, 2048, 0, tlm)],
        [outop(2048, F32, 2048, tlm), outop(2048, F32, 2048, tlm), outop(2048, act, 2048, tlm)],
        [parout(1, 2048, 2048), parout(1, 2048, 2048)])
    G['ssm_d'] = ddl.reshape(SSM_HEADS, 64).sum(axis=1)
    G['ssm_norm_w'] = dsnw.reshape(-1)

    dq2, dk2, dv2, dsp_g = _gdn_scan_bwd(q, k, v, sp, ss, do, nctx)
    dxs2, dbm2, dcm2, dsp_s, dal = _ssd_scan_bwd(xs, bm, cm, sp, al, hs, dy, nctx)
    G['ssm_a_log'] = dal.reshape(2, SSM_PAIRS, 2, 64).sum(axis=-1).reshape(2, SSM_HEADS)

    dconv_w, dconv_b, dpre = {}, {}, {}

    def conv_bwd(name, cot_ops):
        fn, tok_, par_, width = conv_parts[name]
        dpre[name], dconv_w[name], dconv_b[name] = _tw_bwd(
            "conv_" + name + "_bwd", fn, (width // 512, nt), tok_, par_, cot_ops, [outop(width, act, 512)],
            [parout(3, width, 512), parout(1, width, 512)])

    conv_bwd('q', [tokop(dq2, 512)])
    conv_bwd('k', [tokop(dk2, 512)])
    conv_bwd('v', [tokop(dv2, 512)])

    def xs_two_cots(name):
        fn, tok_, par_, width = conv_parts[name]
        ops = tok_ + par_ + [tokop(dxs2, 512), tokop(dxs_a, 512)]

        def body(x_ref, w_ref, b_ref, c1_ref, c2_ref, dx_ref, dw_ref, db_ref):
            info = (pl.program_id(0), pl.program_id(1))
            _, pull = jax.vjp(lambda *a_: fn(info, *a_), x_ref[...], w_ref[...], b_ref[...])
            dx, dw, db = pull((_sum_dirs(c1_ref) + c2_ref[...],))
            dx_ref[...] = dx.astype(dx_ref.dtype)
            first = pl.program_id(1) == 0

            @pl.when(first)
            def _():
                dw_ref[...] = dw
                db_ref[...] = db

            @pl.when(jnp.logical_not(first))
            def _():
                dw_ref[...] += dw
                db_ref[...] += db

        outs = [outop(width, act, 512), parout(3, width, 512), parout(1, width, 512)]
        return pl.pallas_call(
            body, grid=(width // 512, nt), in_specs=[op.spec() for op in ops], out_specs=[op.spec() for op in outs],
            out_shape=[jax.ShapeDtypeStruct(*op.arr) for op in outs],
            compiler_params=_cp(("parallel", "arbitrary")), name="conv_xs_bwd")(*[op.arr for op in ops])

    dpre['xs'], dconv_w['xs'], dconv_b['xs'] = xs_two_cots('xs')
    conv_bwd('bm', [tokop(dbm2, 512)])
    conv_bwd('cm', [tokop(dcm2, 512)])
    G['gdn_conv_w'] = jnp.concatenate([dconv_w['q'], dconv_w['k'], dconv_w['v']], axis=1)
    G['gdn_conv_b'] = jnp.concatenate([dconv_b['q'], dconv_b['k'], dconv_b['v']], axis=1).reshape(-1)
    G['ssm_conv_w'] = jnp.concatenate([dconv_w['xs'], dconv_w['bm'], dconv_w['cm']], axis=1)
    G['ssm_conv_b'] = jnp.concatenate([dconv_b['xs'], dconv_b['bm'], dconv_b['cm']], axis=1).reshape(-1)

    def act_bwd():
        ops = act_tok + act_par + [tokop(dsp_g, 128), tokop(dsp_s, 128)]

        def body(x_ref, p0_ref, p1_ref, c1_ref, c2_ref, dx_ref, d0_ref, d1_ref):
            info = (pl.program_id(0), pl.program_id(1))
            _, pull = jax.vjp(lambda *a_: _act_fn(info, *a_), x_ref[...], p0_ref[...], p1_ref[...])
            dx, d0, d1 = pull((_sum_dirs(c1_ref) + _sum_dirs(c2_ref),))
            dx_ref[...] = dx.astype(dx_ref.dtype)
            first = pl.program_id(1) == 0

            @pl.when(first)
            def _():
                d0_ref[...] = d0
                d1_ref[...] = d1

            @pl.when(jnp.logical_not(first))
            def _():
                d0_ref[...] += d0
                d1_ref[...] += d1

        outs = [outop(128, act, 128), parout(1, 128, 128), parout(1, 128, 128)]
        return pl.pallas_call(
            body, grid=(1, nt), in_specs=[op.spec() for op in ops], out_specs=[op.spec() for op in outs],
            out_shape=[jax.ShapeDtypeStruct(*op.arr) for op in outs],
            compiler_params=_cp(("parallel", "arbitrary")), name="small_act_bwd")(*[op.arr for op in ops])

    dsmall, dp0, dp1 = act_bwd()
    G['gdn_a_log'] = dp0[0, 0:16].reshape(2, GDN_HEADS)
    G['gdn_dt_bias'] = dp1[0, 0:16].reshape(2, GDN_HEADS)
    G['ssm_dt_bias'] = dp1[0, 32:96].reshape(2, SSM_HEADS)

    zpad = jnp.zeros((t, P_TOTAL - P_SMALL - 128), act)
    dproj = jnp.concatenate([dpre['q'], dpre['k'], dpre['v'], dzg, dzs, dpre['xs'], dpre['bm'], dpre['cm'],
                             dgg, dgs, dsmall, zpad], axis=1)
    G['w_in'] = _unpack_w_in(_matmul(a, dproj, 'tn', "d_w_in"))
    da = _matmul(dproj, wp, 'nt', "d_a")

    def pre_bwd():
        ops = pre_tok + pre_par + [tokop(da), tokop(dxt1)]

        def body(x_ref, w_ref, sc_ref, sh_ref, c_ref, r_ref, dx_ref, dw_ref, dsc_ref, dsh_ref):
            info = (pl.program_id(0), pl.program_id(1))
            _, pull = jax.vjp(lambda *a_: pre_fn(info, *a_), x_ref[...], w_ref[...], sc_ref[...], sh_ref[...])
            dx, dw, dsc, dsh = pull((c_ref[...],))
            dx_ref[...] = dx + r_ref[...]
            first = pl.program_id(1) == 0

            @pl.when(first)
            def _():
                dw_ref[...] = dw
                dsc_ref[...] = dsc
                dsh_ref[...] = dsh

            @pl.when(jnp.logical_not(first))
            def _():
                dw_ref[...] += dw
                dsc_ref[...] += dsc
                dsh_ref[...] += dsh

        outs = [Op(((l, D_MODEL), F32), (tl, D_MODEL), lambda j, i: (jnp.maximum(i - nctx_t, 0), 0)),
                parout(1, 1024, 1024), parout(8, 1024, 1024), parout(8, 1024, 1024)]
        return pl.pallas_call(
            body, grid=(1, nt), in_specs=[op.spec() for op in ops], out_specs=[op.spec() for op in outs],
            out_shape=[jax.ShapeDtypeStruct(*op.arr) for op in outs],
            compiler_params=_cp(("arbitrary", "arbitrary")), name="prenorm_bwd")(*[op.arr for op in ops])

    grad_x, dn1, dsc1, dsh1 = pre_bwd()
    G['norm1_w'] = dn1.reshape(-1)
    dmods = jnp.concatenate([dsh1, dsc1, dg1, dsh2, dsc2, dg2], axis=1)
    G['ada_w'], dab, dcc = _ada_bwd(cc, W['ada_w'], dmods)
    G['ada_b'] = dab.reshape(-1)
    G['c_ctx'] = dcc[1]
    return loss, grad_x, G


HBM = pl.BlockSpec(memory_space=pl.ANY)
MESH = pl.DeviceIdType.MESH


def _plane_peers():
    x, y, c = lax.axis_index("x"), lax.axis_index("y"), lax.axis_index("c")
    return (x, y, c), [(1 - x, y, c), (x, 1 - y, c), (1 - x, 1 - y, c)]


def _all_gather_plane(shard):
    r, w = shard.shape

    def body(s_ref, o_ref, send_sems, recv_sems, local_sem):
        (x, y, c), peers = _plane_peers()
        me = 2 * x + y
        mine = pltpu.make_async_copy(s_ref, o_ref.at[me], local_sem)
        mine.start()
        copies = [pltpu.make_async_remote_copy(src_ref=s_ref, dst_ref=o_ref.at[me], send_sem=send_sems.at[kk],
                                               recv_sem=recv_sems.at[kk], device_id=peer, device_id_type=MESH)
                  for kk, peer in enumerate(peers)]
        for cp in copies:
            cp.start()
        for cp in copies:
            cp.wait()
        mine.wait()

    return pl.pallas_call(
        body, in_specs=[HBM], out_specs=HBM, out_shape=jax.ShapeDtypeStruct((4, r, w), shard.dtype),
        scratch_shapes=[pltpu.SemaphoreType.DMA((3,)), pltpu.SemaphoreType.DMA((3,)), pltpu.SemaphoreType.DMA],
        name="all_gather_plane")(shard)


def _scatter_plane(g):
    _, r, w = g.shape

    def body(g_ref, o_ref, send_sems, recv_sems, local_sem):
        (x, y, c), peers = _plane_peers()
        me = 2 * x + y
        mine = pltpu.make_async_copy(g_ref.at[me], o_ref.at[me], local_sem)
        mine.start()
        copies = [pltpu.make_async_remote_copy(src_ref=g_ref.at[2 * px + py], dst_ref=o_ref.at[me],
                                               send_sem=send_sems.at[kk], recv_sem=recv_sems.at[kk],
                                               device_id=(px, py, pc), device_id_type=MESH)
                  for kk, (px, py, pc) in enumerate(peers)]
        for cp in copies:
            cp.start()
        for cp in copies:
            cp.wait()
        mine.wait()

    return pl.pallas_call(
        body, in_specs=[HBM], out_specs=HBM, out_shape=jax.ShapeDtypeStruct(g.shape, g.dtype),
        scratch_shapes=[pltpu.SemaphoreType.DMA((3,)), pltpu.SemaphoreType.DMA((3,)), pltpu.SemaphoreType.DMA],
        name="scatter_plane")(g)


def _swap_sibling(p):
    def body(p_ref, o_ref, send_sem, recv_sem):
        x, y, c = lax.axis_index("x"), lax.axis_index("y"), lax.axis_index("c")
        cp = pltpu.make_async_remote_copy(src_ref=p_ref, dst_ref=o_ref, send_sem=send_sem, recv_sem=recv_sem,
                                          device_id=(x, y, 1 - c), device_id_type=MESH)
        cp.start()
        cp.wait()

    return pl.pallas_call(
        body, in_specs=[HBM], out_specs=HBM, out_shape=jax.ShapeDtypeStruct(p.shape, p.dtype),
        scratch_shapes=[pltpu.SemaphoreType.DMA, pltpu.SemaphoreType.DMA], name="swap_sibling")(p)


def _sum4(rv):
    _, r, w = rv.shape
    tr = _pick(r, (512, 256, 128, 64, 32, 16, 8))

    def body(r_ref, o_ref):
        o_ref[...] = ((r_ref[0] + r_ref[1]) + r_ref[2]) + r_ref[3]

    return pl.pallas_call(
        body, grid=(r // tr,), in_specs=[pl.BlockSpec((4, tr, w), lambda i: (0, i, 0))],
        out_specs=pl.BlockSpec((tr, w), lambda i: (i, 0)), out_shape=jax.ShapeDtypeStruct((r, w), rv.dtype),
        compiler_params=_cp(("parallel",)), name="sum4")(rv)


def _adamw(w, m, v, p, q):
    r, wd = w.shape
    tr = _pick(r, (512, 256, 128, 64, 32, 16, 8))

    def body(w_ref, m_ref, v_ref, p_ref, q_ref, g_ref, d_ref, mo_ref, vo_ref):
        g = p_ref[...] + q_ref[...]
        mn = ADAM_B1 * m_ref[...] + (1.0 - ADAM_B1) * g
        vn = ADAM_B2 * v_ref[...] + (1.0 - ADAM_B2) * jnp.square(g)
        m_hat = mn / (1.0 - ADAM_B1 ** ADAM_STEP)
        v_hat = vn / (1.0 - ADAM_B2 ** ADAM_STEP)
        g_ref[...] = g
        d_ref[...] = -ADAM_LR * (m_hat / (jnp.sqrt(v_hat) + ADAM_EPS) + ADAM_WD * w_ref[...])
        mo_ref[...] = mn
        vo_ref[...] = vn

    spec = pl.BlockSpec((tr, wd), lambda i: (i, 0))
    return pl.pallas_call(
        body, grid=(r // tr,), in_specs=[spec] * 5, out_specs=[spec] * 4,
        out_shape=[jax.ShapeDtypeStruct((r, wd), F32)] * 4, compiler_params=_cp(("parallel",)), name="adamw")(w, m, v, p, q)


def _flat(pieces, dtype, row_mult):
    v = jnp.concatenate([p.reshape(-1).astype(dtype) for p in pieces])
    n = v.shape[0]
    rows = -(-n // 1024)
    rows = -(-rows // row_mult) * row_mult
    return jnp.pad(v, (0, rows * 1024 - n)).reshape(rows, 1024)


def _unflat(buf, shapes):
    v = buf.reshape(-1)
    out, off = [], 0
    for s in shapes:
        n = math.prod(s)
        out.append(v[off:off + n].reshape(s))
        off += n
    return out


def _f32_as_bf16(a):
    return lax.bitcast_convert_type(a, jnp.bfloat16).reshape(-1)


def _bf16_as_f32(v, shape):
    return lax.bitcast_convert_type(v.reshape(-1, 2), F32).reshape(shape)


def kernel(x, c, ctx, c_ctx, ada_w, ada_b, norm1_w, w_in, gdn_conv_w, gdn_conv_b, gdn_a_log, gdn_dt_bias, gdn_norm_w, ssm_conv_w, ssm_conv_b, ssm_a_log, ssm_dt_bias, ssm_d, ssm_norm_w, w_br_gdn, w_br_ssm, w_out, norm2_w, w_ffn_in, w_ffn_out, norm_f_w, loss_target, m_c_ctx, m_ada_w, m_ada_b, m_norm1_w, m_w_in, m_gdn_conv_w, m_gdn_conv_b, m_gdn_a_log, m_gdn_dt_bias, m_gdn_norm_w, m_ssm_conv_w, m_ssm_conv_b, m_ssm_a_log, m_ssm_dt_bias, m_ssm_d, m_ssm_norm_w, m_w_br_gdn, m_w_br_ssm, m_w_out, m_norm2_w, m_w_ffn_in, m_w_ffn_out, m_norm_f_w, v_c_ctx, v_ada_w, v_ada_b, v_norm1_w, v_w_in, v_gdn_conv_w, v_gdn_conv_b, v_gdn_a_log, v_gdn_dt_bias, v_gdn_norm_w, v_ssm_conv_w, v_ssm_conv_b, v_ssm_a_log, v_ssm_dt_bias, v_ssm_d, v_ssm_norm_w, v_w_br_gdn, v_w_br_ssm, v_w_out, v_norm2_w, v_w_ffn_in, v_w_ffn_out, v_norm_f_w):
    args = dict(locals())
    wl = {n: args[n] for n in WEIGHTS}
    ml = {n: args['m_' + n] for n in WEIGHTS}
    vl = {n: args['v_' + n] for n in WEIGHTS}

    def nodepth(n, a):
        return a if n in ('c_ctx', 'norm_f_w') else a[0]

    pieces = [nodepth(n, wl[n]).astype(jnp.bfloat16).reshape(-1) for n in MXU_WEIGHTS]
    pieces += [_f32_as_bf16(nodepth(n, wl[n])) for n in F32_GATHERED]
    shard_buf = _flat(pieces, jnp.bfloat16, 16)
    gathered = _all_gather_plane(shard_buf)
    full = {}
    per_pos = []
    for j in range(4):
        v = gathered[j].reshape(-1)
        off, d = 0, {}
        for n in MXU_WEIGHTS:
            s = nodepth(n, wl[n]).shape
            d[n] = v[off:off + math.prod(s)].reshape(s)
            off += math.prod(s)
        for n in F32_GATHERED:
            s = nodepth(n, wl[n]).shape
            d[n] = _bf16_as_f32(v[off:off + 2 * math.prod(s)], s)
            off += 2 * math.prod(s)
        per_pos.append(d)
    for n in MXU_WEIGHTS + F32_GATHERED:
        full[n] = jnp.concatenate([per_pos[j][n] for j in range(4)], axis=SHARD_AXIS[n])
    for n in WEIGHTS:
        if n not in full:
            full[n] = nodepth(n, wl[n])

    loss_local, grad_x, G = _local_step(x[0], c, ctx[0], loss_target[0], full)
    loss = lax.psum(loss_local, ("x", "y", "c"))

    def dest_block(j):
        ps = []
        for n in WEIGHTS:
            g = G[n]
            if n in SHARD_AXIS:
                ax = SHARD_AXIS[n]
                sz = g.shape[ax] // 4
                g = lax.slice_in_dim(g, j * sz, (j + 1) * sz, axis=ax)
            ps.append(g)
        return _flat(ps, F32, 8)

    send = jnp.stack([dest_block(j) for j in range(4)])
    plane_sum = _sum4(_scatter_plane(send))
    other = _swap_sibling(plane_sum)

    order = [nodepth(n, wl[n]) for n in WEIGHTS]
    shapes = [a.shape for a in order]
    w_flat = _flat(order, F32, 8)
    m_flat = _flat([nodepth(n, ml[n]) for n in WEIGHTS], F32, 8)
    v_flat = _flat([nodepth(n, vl[n]) for n in WEIGHTS], F32, 8)
    outs = _adamw(w_flat, m_flat, v_flat, plane_sum, other)
    res = []
    for buf in outs:
        res += [a.reshape(wl[n].shape) for n, a in zip(WEIGHTS, _unflat(buf, shapes))]
    return (loss, grad_x[None], *res)
```

```python
import functools

import jax
import jax.numpy as jnp
from jax import lax
from jax.experimental import pallas as pl
from jax.experimental.pallas import tpu as pltpu

F32 = jnp.float32
HI = lax.Precision.HIGHEST
_MXU_DTYPE = jnp.bfloat16
_SCAN_DTYPE = jnp.bfloat16

D_MODEL = 1024
EPS = 1e-6
CHUNK = 64
GRID_W = 64
GDN_HEADS = 8
GDN_DK = 128
SSM_HEADS = 32
SSM_PAIRS = 16
D_FF = 2816
D_IN_PROJ = 11360
N_PLANE = 4
P_QKV, P_ZG, P_ZS, P_XBC, P_GATE, P_SMALL, P_TOTAL = 0, 3072, 4096, 6144, 9216, 11264, 11776
IN_SEGMENTS = [(0, 3072, P_QKV), (3072, 4096, P_ZG), (4096, 4112, P_SMALL), (4112, 4128, P_SMALL + 16),
               (4128, 6176, P_ZS), (6176, 9248, P_XBC), (9248, 9312, P_SMALL + 32), (9312, 11360, P_GATE)]
ADAM_LR, ADAM_B1, ADAM_B2, ADAM_EPS, ADAM_WD, ADAM_STEP = 0.001, 0.9, 0.999, 1e-08, 0.01, 10

VMEM_LIMIT = 48 * 1024 * 1024

WEIGHTS = ['c_ctx', 'ada_w', 'ada_b', 'norm1_w', 'w_in', 'gdn_conv_w', 'gdn_conv_b', 'gdn_a_log', 'gdn_dt_bias',
           'gdn_norm_w', 'ssm_conv_w', 'ssm_conv_b', 'ssm_a_log', 'ssm_dt_bias', 'ssm_d', 'ssm_norm_w', 'w_br_gdn',
           'w_br_ssm', 'w_out', 'norm2_w', 'w_ffn_in', 'w_ffn_out', 'norm_f_w']
SHARD_AXIS = {'ada_w': 1, 'w_in': 1, 'gdn_conv_w': 1, 'ssm_conv_w': 1, 'w_br_gdn': 0, 'w_br_ssm': 0, 'w_out': 0,
              'w_ffn_in': 1, 'w_ffn_out': 0}
SHARDED = [n for n in WEIGHTS if n in SHARD_AXIS]
SMALL = [n for n in WEIGHTS if n not in SHARD_AXIS]
MXU_WEIGHTS = ['ada_w', 'w_in', 'w_br_gdn', 'w_br_ssm', 'w_out', 'w_ffn_in', 'w_ffn_out']


def _cp(sem):
    return pltpu.CompilerParams(dimension_semantics=sem, vmem_limit_bytes=VMEM_LIMIT)


def _pick(n, cands):
    for c in cands:
        if n % c == 0:
            return c
    raise ValueError(f"no tile for {n}")


def _matmul(a, b, form, name, out_dtype=F32, stacked_out=False):
    dims = {'nn': (((1,), (0,)), ((), ())), 'nt': (((1,), (1,)), ((), ())), 'tn': (((0,), (0,)), ((), ()))}[form]
    stacked_b = b.ndim == 3
    ns = b.shape[2] if stacked_b else None
    if form == 'nn':
        m, k = a.shape
        n = b.shape[0] * ns if stacked_b else b.shape[1]
    elif form == 'nt':
        m, k = a.shape
        n = b.shape[1] if stacked_b else b.shape[0]
    else:
        k, m = a.shape
        n = b.shape[1]
    if form == 'tn':
        tm = m if m <= 3072 else _pick(m, (1024, 512, 256, 128))
        tn = n // N_PLANE if stacked_out else _pick(n, (512, 256, 128))
        tk = _pick(k, (768, 512, 256, 128))
    else:
        tm = _pick(m, (768, 512, 256, 128))
        tn = ns if (stacked_b and form == 'nn') else _pick(n, (512, 256, 128))
        tk = ns if (stacked_b and form == 'nt') else (k if k <= 2048 else _pick(k, (512, 256, 128)))
    nk = k // tk

    def body(a_ref, b_ref, o_ref, acc_ref):
        kk = pl.program_id(2)
        bv = b_ref[0] if stacked_b else b_ref[...]
        part = lax.dot_general(a_ref[...].astype(_MXU_DTYPE), bv.astype(_MXU_DTYPE), dims, preferred_element_type=F32)

        @pl.when(kk == 0)
        def _():
            acc_ref[...] = part

        @pl.when(kk > 0)
        def _():
            acc_ref[...] += part

        @pl.when(kk == nk - 1)
        def _():
            if stacked_out:
                o_ref[0] = acc_ref[...].astype(o_ref.dtype)
            else:
                o_ref[...] = acc_ref[...].astype(o_ref.dtype)

    if form == 'nn':
        a_spec = pl.BlockSpec((tm, tk), lambda i, j, kk: (i, kk))
        b_spec = (pl.BlockSpec((1, tk, tn), lambda i, j, kk: (j, kk, 0)) if stacked_b
                  else pl.BlockSpec((tk, tn), lambda i, j, kk: (kk, j)))
    elif form == 'nt':
        a_spec = pl.BlockSpec((tm, tk), lambda i, j, kk: (i, kk))
        b_spec = (pl.BlockSpec((1, tn, tk), lambda i, j, kk: (kk, j, 0)) if stacked_b
                  else pl.BlockSpec((tn, tk), lambda i, j, kk: (j, kk)))
    else:
        a_spec = pl.BlockSpec((tk, tm), lambda i, j, kk: (kk, i))
        b_spec = pl.BlockSpec((tk, tn), lambda i, j, kk: (kk, j))
    if stacked_out:
        o_spec = pl.BlockSpec((1, tm, tn), lambda i, j, kk: (j, i, 0))
        o_shape = jax.ShapeDtypeStruct((N_PLANE, m, tn), out_dtype)
    else:
        o_spec = pl.BlockSpec((tm, tn), lambda i, j, kk: (i, j))
        o_shape = jax.ShapeDtypeStruct((m, n), out_dtype)
    return pl.pallas_call(
        body, grid=(m // tm, n // tn, nk), in_specs=[a_spec, b_spec], out_specs=o_spec, out_shape=o_shape,
        scratch_shapes=[pltpu.VMEM((tm, tn), F32)],
        compiler_params=_cp(("parallel", "parallel", "arbitrary")), name=name)(a, b)


class Op:
    def __init__(self, arr, bs, im, load=None):
        self.arr, self.bs, self.im = arr, bs, im
        self.load = load or (lambda r: r[...].astype(F32))

    def spec(self):
        return pl.BlockSpec(self.bs, self.im)


def _sum_dirs(r):
    return r[0].astype(F32) + r[1].astype(F32)


def _tw_fwd(name, fn, grid, ins, outs):
    n_in = len(ins)

    def body(*refs):
        info = (pl.program_id(0), pl.program_id(1))
        vals = [op.load(r) for op, r in zip(ins, refs[:n_in])]
        res = fn(info, *vals)
        for r, v in zip(refs[n_in:], res):
            r[...] = v.astype(r.dtype)

    return pl.pallas_call(
        body, grid=grid, in_specs=[op.spec() for op in ins], out_specs=[op.spec() for op in outs],
        out_shape=[jax.ShapeDtypeStruct(*op.arr) for op in outs],
        compiler_params=_cp(("parallel", "arbitrary")), name=name)(*[op.arr for op in ins])


def _tw_bwd(name, fn, grid, tok, par, cots, tok_out, par_out, tok_add=None, sem=("parallel", "arbitrary")):
    n_tok = len(tok)
    flat_cots = [op for group in cots for op in group]
    extra = [tok_add] if tok_add is not None else []
    out_ops = list(tok_out) + list(par_out)

    def body(*refs):
        info = (pl.program_id(0), pl.program_id(1))
        it = iter(refs)
        tok_v = [op.load(next(it)) for op in tok]
        par_v = [op.load(next(it)) for op in par]
        cot_v = []
        for group in cots:
            vals = [op.load(next(it)) for op in group]
            cot_v.append(functools.reduce(lambda u, w: u + w, vals))
        add_v = [op.load(next(it)) for op in extra]
        _, pull = jax.vjp(lambda *a: fn(info, *a), *tok_v, *par_v)
        grads = pull(tuple(cot_v))
        for i in range(n_tok):
            r = next(it)
            g = grads[i] + add_v[0] if (i == 0 and add_v) else grads[i]
            r[...] = g.astype(r.dtype)
        first = pl.program_id(1) == 0
        for i in range(len(par)):
            r = next(it)
            g = grads[n_tok + i]

            @pl.when(first)
            def _(r=r, g=g):
                r[...] = g

            @pl.when(jnp.logical_not(first))
            def _(r=r, g=g):
                r[...] += g

    ops = tok + par + flat_cots + extra
    return pl.pallas_call(
        body, grid=grid, in_specs=[op.spec() for op in ops], out_specs=[op.spec() for op in out_ops],
        out_shape=[jax.ShapeDtypeStruct(*op.arr) for op in out_ops],
        compiler_params=_cp(sem), name=name)(*[op.arr for op in ops])


def _silu(x):
    return x * jax.nn.sigmoid(x)


def _rms(x):
    return x * lax.rsqrt(jnp.mean(x * x, axis=-1, keepdims=True) + EPS)


@functools.partial(jax.custom_vjp, nondiff_argnums=(1,))
def _shift_rows(x, k):
    return pltpu.roll(x, k % x.shape[0], 0)


def _shift_rows_fwd(x, k):
    return _shift_rows(x, k), None


def _shift_rows_bwd(k, _, g):
    return (_shift_rows(g, -k),)


_shift_rows.defvjp(_shift_rows_fwd, _shift_rows_bwd)


def _prenorm_fn(nctx_t, info, x, w, sc8, sh8):
    is_ctx = info[1] < nctx_t
    sc = jnp.where(is_ctx, sc8[1:2], sc8[0:1])
    sh = jnp.where(is_ctx, sh8[1:2], sh8[0:1])
    return (_rms(x) * w * (1.0 + sc) + sh,)


def _conv_fn(nctx_t, mode, info, x, w, b):
    n, c = x.shape
    is_ctx = info[1] < nctx_t
    idx = lax.broadcasted_iota(jnp.int32, (n, 1), 0)
    rr = jnp.where(is_ctx, idx, idx % GRID_W)
    first = rr == 0
    last = rr == jnp.where(is_ctx, n - 1, GRID_W - 1)
    prev = jnp.where(first, 0.0, _shift_rows(x, 1))
    nxt = jnp.where(last, 0.0, _shift_rows(x, -1))
    y = b + prev * w[0:1] + x * w[1:2] + nxt * w[2:3]
    y = _silu(y)
    if mode == 'none':
        return (y,)
    scale = GDN_DK ** -0.5 if mode == 'q' else 1.0
    outs = []
    for h in range(c // 128):
        yh = y[:, h * 128:(h + 1) * 128]
        outs.append(yh * lax.rsqrt(jnp.sum(yh * yh, axis=-1, keepdims=True) + EPS) * scale)
    return (jnp.concatenate(outs, axis=1),)


def _act_fn(info, x, p0, p1):
    lane = lax.broadcasted_iota(jnp.int32, x.shape, 1)
    sp = jax.nn.softplus(x + p1)
    g = -jnp.exp(p0) * sp
    bt = jax.nn.sigmoid(x)
    return (jnp.where(lane < 16, g, jnp.where(lane < 32, bt, jnp.where(lane < 96, sp, 0.0))),)


def _mixg_fn(info, o, zg, gnw):
    outs = []
    for h in range(GDN_HEADS):
        outs.append(_rms(o[:, h * 128:(h + 1) * 128]) * gnw)
    return (jnp.concatenate(outs, axis=1) * _silu(zg),)


def _mixs_fn(info, y, xs, zs, dl, snw):
    yy = (y + dl * xs) * _silu(zs)
    outs = []
    for g in range(4):
        outs.append(_rms(yy[:, g * 512:(g + 1) * 512]))
    return (jnp.concatenate(outs, axis=1) * snw,)


def _merge_fn(info, gg, gs, pg, ps):
    return (jax.nn.sigmoid(gg) * pg + jax.nn.sigmoid(gs) * ps,)


def _norm2_fn(info, xt, mo, g8, w, sc8, sh8):
    h1 = xt + g8[0:1] * mo
    return (h1, _rms(h1) * w * (1.0 + sc8[0:1]) + sh8[0:1])


def _swiglu_fn(info, ug, uu):
    return (_silu(ug) * uu,)


_NN = (((1,), (0,)), ((), ()))
_NT = (((1,), (1,)), ((), ()))
_TN = (((0,), (0,)), ((), ()))


def _mmh(a, b):
    return lax.dot_general(a, b, _NN, precision=HI, preferred_element_type=F32)


def _dot1(a, b, dims):
    return lax.dot_general(a.astype(_SCAN_DTYPE), b.astype(_SCAN_DTYPE), dims, preferred_element_type=F32)


def _mm(a, b):
    return _dot1(a, b, _NN)


def _mm_nt(a, b):
    return _dot1(a, b, _NT)


def _mm_tn(a, b):
    return _dot1(a, b, _TN)


def _split2(a):
    hi = a.astype(_SCAN_DTYPE)
    return hi, (a - hi.astype(F32)).astype(_SCAN_DTYPE)


def _dot3(a, b, dims):
    ah, al = _split2(a)
    bh, bl = _split2(b)
    d = lambda u, w: lax.dot_general(u, w, dims, preferred_element_type=F32)
    return d(ah, bh) + (d(ah, bl) + d(al, bh))


def _order_masks(d):
    i = lax.broadcasted_iota(jnp.int32, (CHUNK, CHUNK), 0)
    j = lax.broadcasted_iota(jnp.int32, (CHUNK, CHUNK), 1)
    s = jnp.where(d == 0, 1, -1) * (i - j)
    return (s >= 0).astype(F32), (s > 0).astype(F32)


@jax.custom_vjp
def _unit_tri_inv(mats):
    i = lax.broadcasted_iota(jnp.int32, (CHUNK, CHUNK), 0)
    j = lax.broadcasted_iota(jnp.int32, (CHUNK, CHUNK), 1)
    eye = (i == j).astype(F32)
    ps = [-a for a in mats]
    ts = [eye + p for p in ps]
    for _ in range(5):
        ps = [_dot3(p, p, _NN) for p in ps]
        ts = [t + _dot3(t, p, _NN) for t, p in zip(ts, ps)]
    return tuple(ts)


def _uti_fwd(mats):
    ts = _unit_tri_inv(mats)
    return ts, ts


def _uti_bwd(ts, gs):
    inner = [_dot3(g, t, _NT) for g, t in zip(gs, ts)]
    return (tuple(-_dot3(t, u, _TN) for t, u in zip(ts, inner)),)


_unit_tri_inv.defvjp(_uti_fwd, _uti_bwd)


def _lane_col(blk, lane_idx):
    lane = lax.broadcasted_iota(jnp.int32, blk.shape, 1)
    return jnp.sum(jnp.where(lane == lane_idx, blk, 0.0), axis=1, keepdims=True)


def _decay_mat(cum, incl):
    cb = jnp.broadcast_to(cum, (CHUNK, CHUNK))
    return jnp.exp(jnp.minimum(cb - cb.T, 0.0)) * incl


def _gdn_step(d, q, k, v, spb, *states):
    incl, strict = _order_masks(d)
    cum = _mmh(incl, spb)
    tot = jnp.sum(spb, axis=0, keepdims=True)
    hs = range(GDN_HEADS)
    cat = jnp.concatenate
    qs = [q[:, h * 128:(h + 1) * 128] for h in hs]
    ks = [k[:, h * 128:(h + 1) * 128] for h in hs]
    vs = [v[:, h * 128:(h + 1) * 128] for h in hs]
    gcum = [_lane_col(cum, d * GDN_HEADS + h) for h in hs]
    glast = [_lane_col(tot, d * GDN_HEADS + h) for h in hs]
    beta = [_lane_col(spb, 16 + d * GDN_HEADS + h) for h in hs]
    decay = [_decay_mat(gcum[h], incl) for h in hs]
    egc = [jnp.exp(gcum[h]) for h in hs]
    kb = [ks[h] * beta[h] for h in hs]
    kq = [_mm_nt(cat([kb[h], qs[h]], axis=0), ks[h]) for h in hs]
    ts = _unit_tri_inv(tuple(kq[h][:CHUNK] * decay[h] * strict for h in hs))
    wu = [_mm(ts[h], cat([kb[h] * egc[h], vs[h] * beta[h]], axis=1)) for h in hs]
    ws = [_mm(cat([wu[h][:, :128], qs[h] * egc[h]], axis=0), states[h]) for h in hs]
    vn = [wu[h][:, 128:] - ws[h][:CHUNK] for h in hs]
    outs = [ws[h][CHUNK:] + _mm(kq[h][CHUNK:] * decay[h], vn[h]) for h in hs]
    new_states = [states[h] * jnp.exp(glast[h]) + _mm_tn(ks[h] * jnp.exp(glast[h] - gcum[h]), vn[h]) for h in hs]
    return (cat(outs, axis=1), *new_states)


def _ssd_step(d, x, bm, cm, spb, alog, *states):
    incl, _ = _order_masks(d)
    lane1 = lax.broadcasted_iota(jnp.int32, (1, 128), 1)
    lo_lane = 32 + d * SSM_HEADS
    a_vec = jnp.where(lane1 >= lo_lane, jnp.where(lane1 < lo_lane + SSM_HEADS, -jnp.exp(alog), 0.0), 0.0)
    adt = spb * a_vec
    acum = _mmh(incl, adt)
    alast = jnp.sum(adt, axis=0, keepdims=True)
    low = lax.broadcasted_iota(jnp.int32, (CHUNK, 128), 1) < 64
    row_low = lax.broadcasted_iota(jnp.int32, (128, 1), 0) < 64
    ps = range(SSM_PAIRS)
    bg = [bm[:, g * 128:(g + 1) * 128] for g in range(4)]
    cg = [cm[:, g * 128:(g + 1) * 128] for g in range(4)]
    cb = [_mm_nt(cg[g], bg[g]) for g in range(4)]
    dt0 = [_lane_col(spb, lo_lane + 2 * p) for p in ps]
    dt1 = [_lane_col(spb, lo_lane + 2 * p + 1) for p in ps]
    ac0 = [_lane_col(acum, lo_lane + 2 * p) for p in ps]
    ac1 = [_lane_col(acum, lo_lane + 2 * p + 1) for p in ps]
    al0 = [_lane_col(alast, lo_lane + 2 * p) for p in ps]
    al1 = [_lane_col(alast, lo_lane + 2 * p + 1) for p in ps]
    xdt = [x[:, p * 128:(p + 1) * 128] * jnp.where(low, dt0[p], dt1[p]) for p in ps]
    yd0 = [_mm(cb[p // 4] * _decay_mat(ac0[p], incl), jnp.where(low, xdt[p], 0.0)) for p in ps]
    yd1 = [_mm(cb[p // 4] * _decay_mat(ac1[p], incl), jnp.where(low, 0.0, xdt[p])) for p in ps]
    yo = [_mm_nt(cg[p // 4], states[p]) for p in ps]
    ys = [yd0[p] + yd1[p] + jnp.exp(jnp.where(low, ac0[p], ac1[p])) * yo[p] for p in ps]
    e2 = [jnp.exp(jnp.where(low, al0[p] - ac0[p], al1[p] - ac1[p])) for p in ps]
    new = [_mm_tn(xdt[p] * e2[p], bg[p // 4]) for p in ps]
    new_states = [states[p] * jnp.exp(jnp.where(row_low, al0[p], al1[p])) + new[p] for p in ps]
    return (jnp.concatenate(ys, axis=1), *new_states)


def _chunk_of(d, p, nctx, nc):
    return jnp.where(d == 0, p, jnp.where(p < nctx, nctx - 1 - p, nctx + nc - 1 - p))


def _scan_fwd(name, step, toks, pars, out_width, n_state, nctx):
    t = toks[0].shape[0]
    nc = t // CHUNK
    n_tok, n_par = len(toks), len(pars)

    def body(*refs):
        tok_refs, par_refs = refs[:n_tok], refs[n_tok:n_tok + n_par]
        o_ref, ss_ref, s_scr = refs[n_tok + n_par:]
        d, p = pl.program_id(0), pl.program_id(1)

        @pl.when(p == 0)
        def _():
            s_scr[...] = jnp.zeros(s_scr.shape, F32)

        ss_ref[0, 0] = s_scr[...]
        res = step(d, *[r[...] for r in tok_refs], *[r[...] for r in par_refs], *[s_scr[h] for h in range(n_state)])
        o_ref[0] = res[0]
        for h in range(n_state):
            s_scr[h] = res[1 + h]

    ch = lambda d, p: _chunk_of(d, p, nctx, nc)
    in_specs = [pl.BlockSpec((CHUNK, a.shape[1]), lambda d, p: (ch(d, p), 0)) for a in toks]
    in_specs += [pl.BlockSpec(a.shape, lambda d, p: (0, 0)) for a in pars]
    return pl.pallas_call(
        body, grid=(2, nc), in_specs=in_specs,
        out_specs=[pl.BlockSpec((1, CHUNK, out_width), lambda d, p: (d, ch(d, p), 0)),
                   pl.BlockSpec((1, 1, n_state, 128, 128), lambda d, p: (d, p, 0, 0, 0))],
        out_shape=[jax.ShapeDtypeStruct((2, t, out_width), F32), jax.ShapeDtypeStruct((2, nc, n_state, 128, 128), F32)],
        scratch_shapes=[pltpu.VMEM((n_state, 128, 128), F32)],
        compiler_params=_cp(("arbitrary", "arbitrary")), name=name)(*toks, *pars)


def _scan_bwd(name, step, toks, pars, ss, dout, n_state, nctx):
    t = toks[0].shape[0]
    nc = t // CHUNK
    n_tok, n_par = len(toks), len(pars)

    def body(*refs):
        it = iter(refs)
        tok_refs = [next(it) for _ in range(n_tok)]
        par_refs = [next(it) for _ in range(n_par)]
        ss_ref, do_ref = next(it), next(it)
        dtok_refs = [next(it) for _ in range(n_tok)]
        dpar_refs = [next(it) for _ in range(n_par)]
        ds_scr = next(it)
        d, pr = pl.program_id(0), pl.program_id(1)

        @pl.when(pr == 0)
        def _():
            ds_scr[...] = jnp.zeros(ds_scr.shape, F32)

        _, pull = jax.vjp(functools.partial(step, d), *[r[...] for r in tok_refs], *[r[...] for r in par_refs],
                          *[ss_ref[0, 0, h] for h in range(n_state)])
        grads = pull((do_ref[...], *[ds_scr[h] for h in range(n_state)]))
        for r, g in zip(dtok_refs, grads[:n_tok]):
            r[0] = g
        for h in range(n_state):
            ds_scr[h] = grads[n_tok + n_par + h]
        first = (d == 0) & (pr == 0)
        for r, g in zip(dpar_refs, grads[n_tok:n_tok + n_par]):
            @pl.when(first)
            def _(r=r, g=g):
                r[...] = g

            @pl.when(jnp.logical_not(first))
            def _(r=r, g=g):
                r[...] += g

    ch = lambda d, pr: _chunk_of(d, nc - 1 - pr, nctx, nc)
    in_specs = [pl.BlockSpec((CHUNK, a.shape[1]), lambda d, pr: (ch(d, pr), 0)) for a in toks]
    in_specs += [pl.BlockSpec(a.shape, lambda d, pr: (0, 0)) for a in pars]
    in_specs += [pl.BlockSpec((1, 1, n_state, 128, 128), lambda d, pr: (d, nc - 1 - pr, 0, 0, 0)),
                 pl.BlockSpec((CHUNK, dout.shape[1]), lambda d, pr: (ch(d, pr), 0))]
    out_specs = [pl.BlockSpec((1, CHUNK, a.shape[1]), lambda d, pr: (d, ch(d, pr), 0)) for a in toks]
    out_specs += [pl.BlockSpec(a.shape, lambda d, pr: (0, 0)) for a in pars]
    out_shape = [jax.ShapeDtypeStruct((2, t, a.shape[1]), F32) for a in toks]
    out_shape += [jax.ShapeDtypeStruct(a.shape, F32) for a in pars]
    return pl.pallas_call(
        body, grid=(2, nc), in_specs=in_specs, out_specs=out_specs, out_shape=out_shape,
        scratch_shapes=[pltpu.VMEM((n_state, 128, 128), F32)],
        compiler_params=_cp(("arbitrary", "arbitrary")), name=name)(*toks, *pars, ss, dout)


ADA_TN = 512


def _ada_fwd(cc, ada_w4, ada_b):
    per = ada_w4.shape[2] // ADA_TN
    n = N_PLANE * ada_w4.shape[2]

    def body(c_ref, w_ref, b_ref, o_ref):
        s = _silu(c_ref[...]).astype(_MXU_DTYPE)
        o_ref[...] = jnp.dot(s, w_ref[0].astype(_MXU_DTYPE), preferred_element_type=F32) + b_ref[...]

    return pl.pallas_call(
        body, grid=(n // ADA_TN,),
        in_specs=[pl.BlockSpec((8, D_MODEL), lambda j: (0, 0)),
                  pl.BlockSpec((1, D_MODEL, ADA_TN), lambda j: (j // per, 0, j % per)),
                  pl.BlockSpec((1, ADA_TN), lambda j: (0, j))],
        out_specs=pl.BlockSpec((8, ADA_TN), lambda j: (0, j)), out_shape=jax.ShapeDtypeStruct((8, n), F32),
        compiler_params=_cp(("parallel",)), name="ada_fwd")(cc, ada_w4, ada_b)


def _ada_bwd(cc, ada_w4, dmods):
    per = ada_w4.shape[2] // ADA_TN
    n = N_PLANE * ada_w4.shape[2]
    nj = n // ADA_TN

    def body(c_ref, w_ref, g_ref, dw_ref, db_ref, dc_ref):
        j = pl.program_id(0)
        g = g_ref[...]
        row = lax.broadcasted_iota(jnp.int32, g.shape, 0)
        g = jnp.where(row < 2, g, 0.0)
        s, pull = jax.vjp(_silu, c_ref[...])
        dw_ref[0] = lax.dot_general(s.astype(_MXU_DTYPE), g.astype(_MXU_DTYPE), _TN, preferred_element_type=F32)
        db_ref[...] = jnp.sum(g, axis=0, keepdims=True)
        ds = lax.dot_general(g.astype(_MXU_DTYPE), w_ref[0].astype(_MXU_DTYPE), _NT, preferred_element_type=F32)

        @pl.when(j == 0)
        def _():
            dc_ref[...] = ds

        @pl.when(j > 0)
        def _():
            dc_ref[...] += ds

        @pl.when(j == nj - 1)
        def _():
            dc_ref[...] = pull(dc_ref[...])[0]

    wspec = pl.BlockSpec((1, D_MODEL, ADA_TN), lambda j: (j // per, 0, j % per))
    return pl.pallas_call(
        body, grid=(nj,),
        in_specs=[pl.BlockSpec((8, D_MODEL), lambda j: (0, 0)), wspec, pl.BlockSpec((8, ADA_TN), lambda j: (0, j))],
        out_specs=[wspec, pl.BlockSpec((1, ADA_TN), lambda j: (0, j)), pl.BlockSpec((8, D_MODEL), lambda j: (0, 0))],
        out_shape=[jax.ShapeDtypeStruct(ada_w4.shape, F32), jax.ShapeDtypeStruct((1, n), F32),
                   jax.ShapeDtypeStruct((8, D_MODEL), F32)],
        compiler_params=_cp(("arbitrary",)), name="ada_bwd")(cc, ada_w4, dmods)


def _tail(h1, ff, mods, wf, tgt, nctx_t, tl):
    t = h1.shape[0]
    nt = t // tl

    def loss_fn(valid, h1v, ffv, g8, w, tg):
        h2 = h1v + g8[0:1] * ffv
        y = _rms(h2) * w
        err = (y - tg) ** 2
        return 0.5 * jnp.sum(jnp.mean(err, axis=-1, keepdims=True), axis=0, keepdims=True) * valid

    def body(h1_ref, ff_ref, g_ref, w_ref, t_ref, loss_ref, dh_ref, dff_ref, dg_ref, dw_ref):
        i = pl.program_id(0)
        valid = jnp.where(i < nctx_t, 0.0, 1.0)
        lv, pull = jax.vjp(functools.partial(loss_fn, valid), h1_ref[...], ff_ref[...].astype(F32), g_ref[...],
                           w_ref[...], t_ref[...])
        dh, dff, dg, dw, _ = pull(jnp.ones((1, 1), F32))
        dh_ref[...] = dh
        dff_ref[...] = dff.astype(dff_ref.dtype)
        lb = jnp.broadcast_to(lv, loss_ref.shape)

        @pl.when(i == 0)
        def _():
            loss_ref[...] = lb
            dg_ref[...] = dg
            dw_ref[...] = dw

        @pl.when(i > 0)
        def _():
            loss_ref[...] += lb
            dg_ref[...] += dg
            dw_ref[...] += dw

    tok = pl.BlockSpec((tl, D_MODEL), lambda i: (i, 0))
    return pl.pallas_call(
        body, grid=(nt,),
        in_specs=[tok, tok, pl.BlockSpec((8, D_MODEL), lambda i: (0, 5)), pl.BlockSpec((1, D_MODEL), lambda i: (0, 0)),
                  pl.BlockSpec((tl, D_MODEL), lambda i: (jnp.maximum(i - nctx_t, 0), 0))],
        out_specs=[pl.BlockSpec((8, 128), lambda i: (0, 0)), tok, tok, pl.BlockSpec((8, D_MODEL), lambda i: (0, 0)),
                   pl.BlockSpec((1, D_MODEL), lambda i: (0, 0))],
        out_shape=[jax.ShapeDtypeStruct((8, 128), F32), jax.ShapeDtypeStruct((t, D_MODEL), F32),
                   jax.ShapeDtypeStruct((t, D_MODEL), _MXU_DTYPE), jax.ShapeDtypeStruct((8, D_MODEL), F32),
                   jax.ShapeDtypeStruct((1, D_MODEL), F32)],
        compiler_params=_cp(("arbitrary",)), name="tail_loss")(h1, ff, mods, wf, tgt)


def _pack_w_in(w4):
    ns = w4.shape[2]
    placed = []
    for s0, s1, p0 in IN_SEGMENTS:
        for j in range(N_PLANE):
            lo, hi = max(s0, j * ns), min(s1, (j + 1) * ns)
            if lo < hi:
                placed.append((p0 + lo - s0, w4[j][:, lo - j * ns:hi - j * ns]))
    placed.sort(key=lambda e: e[0])
    pieces, end = [], 0
    for pos, piece in placed:
        assert pos == end, (pos, end)
        pieces.append(piece)
        end = pos + piece.shape[1]
    pieces.append(jnp.zeros((w4.shape[1], P_TOTAL - end), w4.dtype))
    return jnp.concatenate(pieces, axis=1)


def _unpack_w_in(g):
    ns = D_IN_PROJ // N_PLANE
    shards = []
    for j in range(N_PLANE):
        pieces = []
        for s0, s1, p0 in IN_SEGMENTS:
            lo, hi = max(s0, j * ns), min(s1, (j + 1) * ns)
            if lo < hi:
                pieces.append(g[:, p0 + lo - s0:p0 + hi - s0])
        shards.append(jnp.concatenate(pieces, axis=1))
    return jnp.stack(shards)


def _local_step(x, c, ctx, tgt, W):
    lc, l = ctx.shape[0], x.shape[0]
    t = lc + l
    tl = 256
    assert lc == tl and l % tl == 0 and lc % CHUNK == 0
    nt, nctx_t, nctx = t // tl, lc // tl, lc // CHUNK
    act = _MXU_DTYPE
    r1 = lambda v: v.reshape(1, -1)

    xt = jnp.concatenate([ctx, x], axis=0)
    cc = jnp.concatenate([c, r1(W['c_ctx']), jnp.zeros((6, D_MODEL), F32)], axis=0)
    mods = _ada_fwd(cc, W['ada_w'], r1(W['ada_b']))

    def mod(kk):
        return Op(mods, (8, D_MODEL), lambda j, i, kk=kk: (0, kk))

    def tokop(arr, w=D_MODEL, off=0, tl_=tl):
        if arr.ndim == 3:
            return Op(arr, (2, tl_, w), lambda j, i: (0, i, off + j), load=_sum_dirs)
        return Op(arr, (tl_, w), lambda j, i: (i, off + j))

    def outop(n, dtype, w=D_MODEL, tl_=tl):
        return Op(((t, n), dtype), (tl_, w), lambda j, i: (i, j))

    def parop(arr, w, off=0):
        return Op(arr, (arr.shape[0], w), lambda j, i: (0, off + j))

    def parout(rows, n, w):
        return Op(((rows, n), F32), (rows, w), lambda j, i: (0, j))

    n1w = r1(W['norm1_w'])
    pre_fn = functools.partial(_prenorm_fn, nctx_t)
    pre_tok, pre_par = [tokop(xt)], [parop(n1w, D_MODEL), mod(1), mod(0)]
    (a,) = _tw_fwd("prenorm_fwd", pre_fn, (1, nt), pre_tok + pre_par, [outop(D_MODEL, act)])
    wp = _pack_w_in(W['w_in'])
    proj = _matmul(a, wp, 'nn', "in_proj")

    gcw, gcb = W['gdn_conv_w'], r1(W['gdn_conv_b'])
    scw, scb = W['ssm_conv_w'], r1(W['ssm_conv_b'])
    conv_parts = {}

    def conv_part(name, mode, poff, cw, cb, woff, width):
        fn = functools.partial(_conv_fn, nctx_t, mode)
        tok_ = [tokop(proj, 512, poff // 512)]
        par_ = [parop(cw, 512, woff // 512), parop(cb, 512, woff // 512)]
        conv_parts[name] = (fn, tok_, par_, width)
        (o,) = _tw_fwd("conv_" + name, fn, (width // 512, nt), tok_ + par_, [outop(width, F32, 512)])
        return o

    q = conv_part('q', 'q', P_QKV, gcw, gcb, 0, 1024)
    k = conv_part('k', 'k', P_QKV + 1024, gcw, gcb, 1024, 1024)
    v = conv_part('v', 'none', P_QKV + 2048, gcw, gcb, 2048, 1024)
    xs = conv_part('xs', 'none', P_XBC, scw, scb, 0, 2048)
    bm = conv_part('bm', 'none', P_XBC + 2048, scw, scb, 2048, 512)
    cm = conv_part('cm', 'none', P_XBC + 2560, scw, scb, 2560, 512)

    z16, z32 = jnp.zeros((16,), F32), jnp.zeros((32,), F32)
    p0 = jnp.concatenate([W['gdn_a_log'].reshape(-1), jnp.zeros((112,), F32)]).reshape(1, 128)
    p1 = jnp.concatenate([W['gdn_dt_bias'].reshape(-1), z16, W['ssm_dt_bias'].reshape(-1), z32]).reshape(1, 128)
    alog = jnp.concatenate([z32, W['ssm_a_log'].reshape(-1), z32]).reshape(1, 128)
    act_tok, act_par = [tokop(proj, 128, P_SMALL // 128)], [parop(p0, 128), parop(p1, 128)]
    (sp,) = _tw_fwd("small_act", _act_fn, (1, nt), act_tok + act_par, [outop(128, F32, 128)])

    o2, ss = _scan_fwd("gdn_scan_fwd", _gdn_step, [q, k, v, sp], [], 1024, GDN_HEADS, nctx)
    y2, hs = _scan_fwd("ssd_scan_fwd", _ssd_step, [xs, bm, cm, sp], [alog], 2048, SSM_PAIRS, nctx)

    tlm = 128
    ntm = t // tlm
    gnw = r1(W['gdn_norm_w'])
    dl = jnp.repeat(W['ssm_d'], 64).reshape(1, 2048)
    snw = r1(W['ssm_norm_w'])
    mg_tok = [tokop(o2, 1024, 0, tlm), tokop(proj, 1024, P_ZG // 1024, tlm)]
    mg_par = [parop(gnw, 128)]
    (og,) = _tw_fwd("mix_gdn", _mixg_fn, (1, ntm), mg_tok + mg_par, [outop(1024, act, 1024, tlm)])
    ms_tok = [tokop(y2, 2048, 0, tlm), tokop(xs, 2048, 0, tlm), tokop(proj, 2048, P_ZS // 2048, tlm)]
    ms_par = [parop(dl, 2048), parop(snw, 2048)]
    (yg,) = _tw_fwd("mix_ssm", _mixs_fn, (1, ntm), ms_tok + ms_par, [outop(2048, act, 2048, tlm)])

    pg = _matmul(og, W['w_br_gdn'], 'nn', "br_gdn")
    ps = _matmul(yg, W['w_br_ssm'], 'nn', "br_ssm")
    mr_tok = [tokop(proj, 1024, P_GATE // 1024), tokop(proj, 1024, P_GATE // 1024 + 1), tokop(pg), tokop(ps)]
    (mrg,) = _tw_fwd("merge", _merge_fn, (1, nt), mr_tok, [outop(1024, act)])
    mo = _matmul(mrg, W['w_out'], 'nn', "w_out")

    n2w = r1(W['norm2_w'])
    n2_tok, n2_par = [tokop(xt), tokop(mo)], [mod(2), parop(n2w, D_MODEL), mod(4), mod(3)]
    h1, f = _tw_fwd("norm2_fwd", _norm2_fn, (1, nt), n2_tok + n2_par, [outop(1024, F32), outop(1024, act)])
    u = _matmul(f, W['w_ffn_in'], 'nn', "ffn_in")
    sw_tok = [tokop(u, 256, 0), tokop(u, 256, D_FF // 256)]
    (sw,) = _tw_fwd("swiglu", _swiglu_fn, (D_FF // 256, nt), sw_tok, [outop(D_FF, act, 256)])
    ff = _matmul(sw, W['w_ffn_out'], 'nn', "ffn_out")

    loss8, dh1, dff, dg2, dnf = _tail(h1, ff, mods, r1(W['norm_f_w']), tgt, nctx_t, tl)
    loss = loss8[0, 0]

    G = {}
    G['norm_f_w'] = dnf.reshape(-1)
    G['w_ffn_out'] = _matmul(sw, dff, 'tn', "d_ffn_out")
    dsw = _matmul(dff, W['w_ffn_out'], 'nt', "d_sw")
    dug, duu = _tw_bwd("swiglu_bwd", _swiglu_fn, (D_FF // 256, nt), sw_tok, [], [[tokop(dsw, 256)]],
                       [outop(D_FF, act, 256), outop(D_FF, act, 256)], [])
    du = jnp.concatenate([dug, duu], axis=1)
    G['w_ffn_in'] = _matmul(f, du, 'tn', "d_ffn_in", stacked_out=True)
    df = _matmul(du, W['w_ffn_in'], 'nt', "d_f")
    dxt1, dmo, dg1, dn2, dsc2, dsh2 = _tw_bwd(
        "norm2_bwd", _norm2_fn, (1, nt), n2_tok, n2_par, [[tokop(dh1)], [tokop(df)]],
        [outop(1024, F32), outop(1024, act)],
        [parout(8, 1024, 1024), parout(1, 1024, 1024), parout(8, 1024, 1024), parout(8, 1024, 1024)])
    G['norm2_w'] = dn2.reshape(-1)
    G['w_out'] = _matmul(mrg, dmo, 'tn', "d_w_out")
    dmrg = _matmul(dmo, W['w_out'], 'nt', "d_mrg")
    dgg, dgs, dpg, dps = _tw_bwd("merge_bwd", _merge_fn, (1, nt), mr_tok, [], [[tokop(dmrg)]],
                                 [outop(1024, act), outop(1024, act), outop(1024, act), outop(1024, act)], [])
    G['w_br_gdn'] = _matmul(og, dpg, 'tn', "d_br_gdn")
    G['w_br_ssm'] = _matmul(yg, dps, 'tn', "d_br_ssm")
    dog = _matmul(dpg, W['w_br_gdn'], 'nt', "d_og")
    dyg = _matmul(dps, W['w_br_ssm'], 'nt', "d_yg")

    do, dzg, dgnw = _tw_bwd("mix_gdn_bwd", _mixg_fn, (1, ntm), mg_tok, mg_par, [[tokop(dog, 1024, 0, tlm)]],
                            [outop(1024, F32, 1024, tlm), outop(1024, act, 1024, tlm)], [parout(1, 128, 128)])
    G['gdn_norm_w'] = dgnw.reshape(-1)
    dy, dxs_a, dzs, ddl, dsnw = _tw_bwd(
        "mix_ssm_bwd", _mixs_fn, (1, ntm), ms_tok, ms_par, [[tokop(dyg, 2048, 0, tlm)]],
        [outop(2048, F32, 2048, tlm), outop(2048, F32, 2048, tlm), outop(2048, act, 2048, tlm)],
        [parout(1, 2048, 2048), parout(1, 2048, 2048)])
    G['ssm_d'] = ddl.reshape(SSM_HEADS, 64).sum(axis=1)
    G['ssm_norm_w'] = dsnw.reshape(-1)

    dq2, dk2, dv2, dsp_g = _scan_bwd("gdn_scan_bwd", _gdn_step, [q, k, v, sp], [], ss, do, GDN_HEADS, nctx)
    dxs2, dbm2, dcm2, dsp_s, dalog = _scan_bwd("ssd_scan_bwd", _ssd_step, [xs, bm, cm, sp], [alog], hs, dy,
                                               SSM_PAIRS, nctx)
    G['ssm_a_log'] = dalog[0, 32:96].reshape(2, SSM_HEADS)

    dconv_w, dconv_b, dpre = {}, {}, {}

    def conv_bwd(name, cot_ops):
        fn, tok_, par_, width = conv_parts[name]
        dpre[name], dconv_w[name], dconv_b[name] = _tw_bwd(
            "conv_" + name + "_bwd", fn, (width // 512, nt), tok_, par_, [cot_ops], [outop(width, act, 512)],
            [parout(3, width, 512), parout(1, width, 512)])

    conv_bwd('q', [tokop(dq2, 512)])
    conv_bwd('k', [tokop(dk2, 512)])
    conv_bwd('v', [tokop(dv2, 512)])
    conv_bwd('xs', [tokop(dxs2, 512), tokop(dxs_a, 512)])
    conv_bwd('bm', [tokop(dbm2, 512)])
    conv_bwd('cm', [tokop(dcm2, 512)])
    G['gdn_conv_w'] = jnp.concatenate([dconv_w['q'], dconv_w['k'], dconv_w['v']], axis=1)
    G['gdn_conv_b'] = jnp.concatenate([dconv_b['q'], dconv_b['k'], dconv_b['v']], axis=1).reshape(-1)
    G['ssm_conv_w'] = jnp.concatenate([dconv_w['xs'], dconv_w['bm'], dconv_w['cm']], axis=1)
    G['ssm_conv_b'] = jnp.concatenate([dconv_b['xs'], dconv_b['bm'], dconv_b['cm']], axis=1).reshape(-1)

    dsmall, dp0, dp1 = _tw_bwd("small_act_bwd", _act_fn, (1, nt), act_tok, act_par,
                               [[tokop(dsp_g, 128), tokop(dsp_s, 128)]], [outop(128, act, 128)],
                               [parout(1, 128, 128), parout(1, 128, 128)])
    G['gdn_a_log'] = dp0[0, 0:16].reshape(2, GDN_HEADS)
    G['gdn_dt_bias'] = dp1[0, 0:16].reshape(2, GDN_HEADS)
    G['ssm_dt_bias'] = dp1[0, 32:96].reshape(2, SSM_HEADS)

    zpad = jnp.zeros((t, P_TOTAL - P_SMALL - 128), act)
    dproj = jnp.concatenate([dpre['q'], dpre['k'], dpre['v'], dzg, dzs, dpre['xs'], dpre['bm'], dpre['cm'],
                             dgg, dgs, dsmall, zpad], axis=1)
    G['w_in'] = _unpack_w_in(_matmul(a, dproj, 'tn', "d_w_in"))
    da = _matmul(dproj, wp, 'nt', "d_a")

    gx_out = Op(((l, D_MODEL), F32), (tl, D_MODEL), lambda j, i: (jnp.maximum(i - nctx_t, 0), 0))
    grad_x, dn1, dsc1, dsh1 = _tw_bwd(
        "prenorm_bwd", pre_fn, (1, nt), pre_tok, pre_par, [[tokop(da)]], [gx_out],
        [parout(1, 1024, 1024), parout(8, 1024, 1024), parout(8, 1024, 1024)], tok_add=tokop(dxt1),
        sem=("arbitrary", "arbitrary"))
    G['norm1_w'] = dn1.reshape(-1)
    dmods = jnp.concatenate([dsh1, dsc1, dg1, dsh2, dsc2, dg2], axis=1)
    G['ada_w'], dab, dcc = _ada_bwd(cc, W['ada_w'], dmods)
    G['ada_b'] = dab.reshape(-1)
    G['c_ctx'] = dcc[1]
    return loss, grad_x, G


HBM = pl.BlockSpec(memory_space=pl.ANY)
MESH = pl.DeviceIdType.MESH


def _plane_peers():
    x, y, c = lax.axis_index("x"), lax.axis_index("y"), lax.axis_index("c")
    return (x, y, c), [(1 - x, y, c), (x, 1 - y, c), (1 - x, 1 - y, c)]


def _plane_exchange(name, arrs, gather):
    n = len(arrs)

    def body(*refs):
        ins, outs = refs[:n], refs[n:2 * n]
        send_sems, recv_sems, local_sems = refs[2 * n:]
        (x, y, c), peers = _plane_peers()
        me = 2 * x + y
        local, remote = [], []
        for ti in range(n):
            lc = pltpu.make_async_copy(ins[ti] if gather else ins[ti].at[me], outs[ti].at[me], local_sems.at[ti])
            lc.start()
            local.append(lc)
            for kk, (px, py, pc) in enumerate(peers):
                src = ins[ti] if gather else ins[ti].at[2 * px + py]
                cp = pltpu.make_async_remote_copy(
                    src_ref=src, dst_ref=outs[ti].at[me], send_sem=send_sems.at[3 * ti + kk],
                    recv_sem=recv_sems.at[3 * ti + kk], device_id=(px, py, pc), device_id_type=MESH)
                cp.start()
                remote.append(cp)
        for cp in remote:
            cp.wait()
        for lc in local:
            lc.wait()

    shapes = [((N_PLANE,) + a.shape if gather else a.shape, a.dtype) for a in arrs]
    return pl.pallas_call(
        body, in_specs=[HBM] * n, out_specs=[HBM] * n, out_shape=[jax.ShapeDtypeStruct(*s) for s in shapes],
        scratch_shapes=[pltpu.SemaphoreType.DMA((3 * n,)), pltpu.SemaphoreType.DMA((3 * n,)),
                        pltpu.SemaphoreType.DMA((n,))], name=name)(*arrs)


def _swap_sibling(arrs):
    n = len(arrs)

    def body(*refs):
        ins, outs, send_sems, recv_sems = refs[:n], refs[n:2 * n], refs[2 * n], refs[2 * n + 1]
        x, y, c = lax.axis_index("x"), lax.axis_index("y"), lax.axis_index("c")
        copies = [pltpu.make_async_remote_copy(src_ref=ins[ti], dst_ref=outs[ti], send_sem=send_sems.at[ti],
                                               recv_sem=recv_sems.at[ti], device_id=(x, y, 1 - c), device_id_type=MESH)
                  for ti in range(n)]
        for cp in copies:
            cp.start()
        for cp in copies:
            cp.wait()

    return pl.pallas_call(
        body, in_specs=[HBM] * n, out_specs=[HBM] * n, out_shape=[jax.ShapeDtypeStruct(a.shape, a.dtype) for a in arrs],
        scratch_shapes=[pltpu.SemaphoreType.DMA((n,)), pltpu.SemaphoreType.DMA((n,))], name="swap_sibling")(*arrs)


def _row_tile(r, c):
    for tr in (512, 256, 128, 64, 32, 16, 8):
        if r % tr == 0 and tr * c * 4 <= (1 << 20):
            return tr
    return r


def _sum4(name, rv):
    _, r, c = rv.shape
    tr = _row_tile(r, c)

    def body(r_ref, o_ref):
        o_ref[...] = ((r_ref[0] + r_ref[1]) + r_ref[2]) + r_ref[3]

    return pl.pallas_call(
        body, grid=(r // tr,), in_specs=[pl.BlockSpec((N_PLANE, tr, c), lambda i: (0, i, 0))],
        out_specs=pl.BlockSpec((tr, c), lambda i: (i, 0)), out_shape=jax.ShapeDtypeStruct((r, c), rv.dtype),
        compiler_params=_cp(("parallel",)), name=name)(rv)


def _adamw(name, w, m, v, p, q):
    r, c = w.shape
    tr = _row_tile(r, c)

    def body(w_ref, m_ref, v_ref, p_ref, q_ref, g_ref, d_ref, mo_ref, vo_ref):
        g = p_ref[...] + q_ref[...]
        mn = ADAM_B1 * m_ref[...] + (1.0 - ADAM_B1) * g
        vn = ADAM_B2 * v_ref[...] + (1.0 - ADAM_B2) * jnp.square(g)
        m_hat = mn / (1.0 - ADAM_B1 ** ADAM_STEP)
        v_hat = vn / (1.0 - ADAM_B2 ** ADAM_STEP)
        g_ref[...] = g
        d_ref[...] = -ADAM_LR * (m_hat / (jnp.sqrt(v_hat) + ADAM_EPS) + ADAM_WD * w_ref[...])
        mo_ref[...] = mn
        vo_ref[...] = vn

    spec = pl.BlockSpec((tr, c), lambda i: (i, 0))
    return pl.pallas_call(
        body, grid=(r // tr,), in_specs=[spec] * 5, out_specs=[spec] * 4,
        out_shape=[jax.ShapeDtypeStruct((r, c), F32)] * 4, compiler_params=_cp(("parallel",)), name=name)(w, m, v, p, q)


SMALL_ROWS = 24


def _pack_small(d):
    v = jnp.concatenate([d[n].reshape(-1) for n in SMALL])
    return jnp.pad(v, (0, SMALL_ROWS * 1024 - v.shape[0])).reshape(SMALL_ROWS, 1024)


def _unpack_small(buf, like):
    v = buf.reshape(-1)
    out, off = {}, 0
    for n in SMALL:
        sz = like[n].size
        out[n] = v[off:off + sz].reshape(like[n].shape)
        off += sz
    return out


def kernel(x, c, ctx, c_ctx, ada_w, ada_b, norm1_w, w_in, gdn_conv_w, gdn_conv_b, gdn_a_log, gdn_dt_bias, gdn_norm_w, ssm_conv_w, ssm_conv_b, ssm_a_log, ssm_dt_bias, ssm_d, ssm_norm_w, w_br_gdn, w_br_ssm, w_out, norm2_w, w_ffn_in, w_ffn_out, norm_f_w, loss_target, m_c_ctx, m_ada_w, m_ada_b, m_norm1_w, m_w_in, m_gdn_conv_w, m_gdn_conv_b, m_gdn_a_log, m_gdn_dt_bias, m_gdn_norm_w, m_ssm_conv_w, m_ssm_conv_b, m_ssm_a_log, m_ssm_dt_bias, m_ssm_d, m_ssm_norm_w, m_w_br_gdn, m_w_br_ssm, m_w_out, m_norm2_w, m_w_ffn_in, m_w_ffn_out, m_norm_f_w, v_c_ctx, v_ada_w, v_ada_b, v_norm1_w, v_w_in, v_gdn_conv_w, v_gdn_conv_b, v_gdn_a_log, v_gdn_dt_bias, v_gdn_norm_w, v_ssm_conv_w, v_ssm_conv_b, v_ssm_a_log, v_ssm_dt_bias, v_ssm_d, v_ssm_norm_w, v_w_br_gdn, v_w_br_ssm, v_w_out, v_norm2_w, v_w_ffn_in, v_w_ffn_out, v_norm_f_w):
    args = dict(locals())
    wl = {n: args[n] for n in WEIGHTS}
    ml = {n: args['m_' + n] for n in WEIGHTS}
    vl = {n: args['v_' + n] for n in WEIGHTS}

    def nodepth(n, a):
        return a if n in ('c_ctx', 'norm_f_w') else a[0]

    shards = [nodepth(n, wl[n]).astype(_MXU_DTYPE if n in MXU_WEIGHTS else F32) for n in SHARDED]
    gathered = dict(zip(SHARDED, _plane_exchange("all_gather_plane", shards, gather=True)))
    W = {n: nodepth(n, wl[n]) for n in SMALL}
    for n in SHARDED:
        g = gathered[n]
        if n in ('ada_w', 'w_in', 'w_ffn_in'):
            W[n] = g
        elif SHARD_AXIS[n] == 0:
            W[n] = g.reshape(N_PLANE * g.shape[1], g.shape[2])
        else:
            W[n] = jnp.concatenate([g[j] for j in range(N_PLANE)], axis=1)

    loss_local, grad_x, G = _local_step(x[0], c, ctx[0], loss_target[0], W)
    loss = lax.psum(loss_local, ("x", "y", "c"))

    send = []
    for n in SHARDED:
        g = G[n]
        if g.ndim == 3:
            send.append(g)
        elif SHARD_AXIS[n] == 0:
            send.append(g.reshape(N_PLANE, g.shape[0] // N_PLANE, g.shape[1]))
        else:
            sz = g.shape[1] // N_PLANE
            send.append(jnp.stack([g[:, j * sz:(j + 1) * sz] for j in range(N_PLANE)]))
    small_g = _pack_small(G)
    send.append(jnp.broadcast_to(small_g[None], (N_PLANE,) + small_g.shape))
    names = SHARDED + ['small']
    received = _plane_exchange("scatter_plane", send, gather=False)
    plane_sum = [_sum4("sum4_" + n, r) for n, r in zip(names, received)]
    other = _swap_sibling(plane_sum)

    wd = {n: nodepth(n, wl[n]) for n in WEIGHTS}
    md = {n: nodepth(n, ml[n]) for n in WEIGHTS}
    vd = {n: nodepth(n, vl[n]) for n in WEIGHTS}
    res = {}
    for n, p, q in zip(names, plane_sum, other):
        if n == 'small':
            outs = _adamw("adamw_small", _pack_small(wd), _pack_small(md), _pack_small(vd), p, q)
            unpacked = [_unpack_small(o, wd) for o in outs]
            for sn in SMALL:
                res[sn] = [u[sn].reshape(wl[sn].shape) for u in unpacked]
        else:
            outs = _adamw("adamw_" + n, wd[n], md[n], vd[n], p, q)
            res[n] = [o.reshape(wl[n].shape) for o in outs]
    flat = [res[n][kind] for kind in range(4) for n in WEIGHTS]
    return (loss, grad_x[None], *flat)
```

```python
import functools

import jax
import jax.numpy as jnp
from jax import lax
from jax.experimental import pallas as pl
from jax.experimental.pallas import tpu as pltpu

F32 = jnp.float32
HI = lax.Precision.HIGHEST
_MXU_DTYPE = jnp.bfloat16
_SCAN_DTYPE = jnp.bfloat16

D_MODEL = 1024
EPS = 1e-6
CHUNK = 64
GRID_W = 64
GDN_HEADS = 8
GDN_DK = 128
SSM_HEADS = 32
SSM_PAIRS = 16
D_FF = 2816
D_IN_PROJ = 11360
N_PLANE = 4
P_QKV, P_ZG, P_ZS, P_XBC, P_GATE, P_SMALL, P_TOTAL = 0, 3072, 4096, 6144, 9216, 11264, 11776
IN_SEGMENTS = [(0, 3072, P_QKV), (3072, 4096, P_ZG), (4096, 4112, P_SMALL), (4112, 4128, P_SMALL + 16),
               (4128, 6176, P_ZS), (6176, 9248, P_XBC), (9248, 9312, P_SMALL + 32), (9312, 11360, P_GATE)]
ADAM_LR, ADAM_B1, ADAM_B2, ADAM_EPS, ADAM_WD, ADAM_STEP = 0.001, 0.9, 0.999, 1e-08, 0.01, 10

VMEM_LIMIT = 48 * 1024 * 1024

WEIGHTS = ['c_ctx', 'ada_w', 'ada_b', 'norm1_w', 'w_in', 'gdn_conv_w', 'gdn_conv_b', 'gdn_a_log', 'gdn_dt_bias',
           'gdn_norm_w', 'ssm_conv_w', 'ssm_conv_b', 'ssm_a_log', 'ssm_dt_bias', 'ssm_d', 'ssm_norm_w', 'w_br_gdn',
           'w_br_ssm', 'w_out', 'norm2_w', 'w_ffn_in', 'w_ffn_out', 'norm_f_w']
SHARD_AXIS = {'ada_w': 1, 'w_in': 1, 'gdn_conv_w': 1, 'ssm_conv_w': 1, 'w_br_gdn': 0, 'w_br_ssm': 0, 'w_out': 0,
              'w_ffn_in': 1, 'w_ffn_out': 0}
SHARDED = [n for n in WEIGHTS if n in SHARD_AXIS]
SMALL = [n for n in WEIGHTS if n not in SHARD_AXIS]
MXU_WEIGHTS = ['ada_w', 'w_in', 'w_br_gdn', 'w_br_ssm', 'w_out', 'w_ffn_in', 'w_ffn_out']


def _cp(sem):
    return pltpu.CompilerParams(dimension_semantics=sem, vmem_limit_bytes=VMEM_LIMIT)


def _pick(n, cands):
    for c in cands:
        if n % c == 0:
            return c
    raise ValueError(f"no tile for {n}")


HBM = pl.BlockSpec(memory_space=pl.ANY)
MESH = pl.DeviceIdType.MESH


def _plane_peers():
    x, y, c = lax.axis_index("x"), lax.axis_index("y"), lax.axis_index("c")
    return (x, y, c), [(1 - x, y, c), (x, 1 - y, c), (1 - x, 1 - y, c)]


class PlaneExchange:
    def __init__(self, arrs, gather):
        self.arrs, self.gather, self.n = list(arrs), gather, len(arrs)
        self.in_specs = [HBM] * self.n
        self.out_specs = [HBM] * self.n
        self.out_shape = [jax.ShapeDtypeStruct((N_PLANE,) + a.shape if gather else a.shape, a.dtype) for a in self.arrs]
        self.scratch = [pltpu.SemaphoreType.DMA((3 * self.n,)), pltpu.SemaphoreType.DMA((3 * self.n,)),
                        pltpu.SemaphoreType.DMA((self.n,))]

    def _copies(self, ins, outs, sems):
        send_sems, recv_sems, local_sems = sems
        (x, y, c), peers = _plane_peers()
        me = 2 * x + y
        copies = []
        for ti in range(self.n):
            src = ins[ti] if self.gather else ins[ti].at[me]
            copies.append(pltpu.make_async_copy(src, outs[ti].at[me], local_sems.at[ti]))
            for kk, (px, py, pc) in enumerate(peers):
                src = ins[ti] if self.gather else ins[ti].at[2 * px + py]
                copies.append(pltpu.make_async_remote_copy(
                    src_ref=src, dst_ref=outs[ti].at[me], send_sem=send_sems.at[3 * ti + kk],
                    recv_sem=recv_sems.at[3 * ti + kk], device_id=(px, py, pc), device_id_type=MESH))
        return copies

    def start(self, ins, outs, sems):
        for cp in self._copies(ins, outs, sems):
            cp.start()

    def wait(self, ins, outs, sems):
        for cp in self._copies(ins, outs, sems):
            cp.wait()


def _plane_exchange(name, arrs, gather):
    ex = PlaneExchange(arrs, gather)
    n = ex.n

    def body(*refs):
        ins, outs, sems = refs[:n], refs[n:2 * n], refs[2 * n:]
        ex.start(ins, outs, sems)
        ex.wait(ins, outs, sems)

    return pl.pallas_call(body, in_specs=ex.in_specs, out_specs=ex.out_specs, out_shape=ex.out_shape,
                          scratch_shapes=ex.scratch, name=name)(*arrs)


def _swap_sibling(arrs):
    n = len(arrs)

    def body(*refs):
        ins, outs, send_sems, recv_sems = refs[:n], refs[n:2 * n], refs[2 * n], refs[2 * n + 1]
        x, y, c = lax.axis_index("x"), lax.axis_index("y"), lax.axis_index("c")
        copies = [pltpu.make_async_remote_copy(src_ref=ins[ti], dst_ref=outs[ti], send_sem=send_sems.at[ti],
                                               recv_sem=recv_sems.at[ti], device_id=(x, y, 1 - c), device_id_type=MESH)
                  for ti in range(n)]
        for cp in copies:
            cp.start()
        for cp in copies:
            cp.wait()

    return pl.pallas_call(
        body, in_specs=[HBM] * n, out_specs=[HBM] * n, out_shape=[jax.ShapeDtypeStruct(a.shape, a.dtype) for a in arrs],
        scratch_shapes=[pltpu.SemaphoreType.DMA((n,)), pltpu.SemaphoreType.DMA((n,))], name="swap_sibling")(*arrs)


def _matmul(a, b, form, name, out_dtype=F32, stacked_out=False, ride=None):
    dims = {'nn': (((1,), (0,)), ((), ())), 'nt': (((1,), (1,)), ((), ())), 'tn': (((0,), (0,)), ((), ()))}[form]
    stacked_b = b.ndim == 3
    ns = b.shape[2] if stacked_b else None
    if form == 'nn':
        m, k = a.shape
        n = b.shape[0] * ns if stacked_b else b.shape[1]
    elif form == 'nt':
        m, k = a.shape
        n = b.shape[1] if stacked_b else b.shape[0]
    else:
        k, m = a.shape
        n = b.shape[1]
    if form == 'tn':
        tm = m if m <= 3072 else _pick(m, (1024, 512, 256, 128))
        tn = n // N_PLANE if stacked_out else _pick(n, (512, 256, 128))
        tk = _pick(k, (768, 512, 256, 128))
    else:
        tm = _pick(m, (768, 512, 256, 128))
        tn = ns if (stacked_b and form == 'nn') else (n if n <= 1024 else _pick(n, (512, 256, 128)))
        tk = ns if (stacked_b and form == 'nt') else (k if k <= 2048 else _pick(k, (512, 256, 128)))
    nk = k // tk
    grid = (m // tm, n // tn, nk)
    n_ride = ride.n if ride is not None else 0

    def body(*refs):
        a_ref, b_ref = refs[0], refs[1]
        ride_in = refs[2:2 + n_ride]
        o_ref = refs[2 + n_ride]
        ride_out = refs[3 + n_ride:3 + 2 * n_ride]
        acc_ref = refs[3 + 2 * n_ride]
        sems = refs[4 + 2 * n_ride:]
        i, j, kk = pl.program_id(0), pl.program_id(1), pl.program_id(2)
        if ride is not None:
            @pl.when((i == 0) & (j == 0) & (kk == 0))
            def _():
                ride.start(ride_in, ride_out, sems)

        def put(val):
            if stacked_out:
                o_ref[0] = val.astype(o_ref.dtype)
            else:
                o_ref[...] = val.astype(o_ref.dtype)

        bv = b_ref[0] if stacked_b else b_ref[...]
        part = lax.dot_general(a_ref[...].astype(_MXU_DTYPE), bv.astype(_MXU_DTYPE), dims, preferred_element_type=F32)
        if nk == 1:
            put(part)
        else:
            @pl.when(kk == 0)
            def _():
                acc_ref[...] = part

            @pl.when(kk > 0)
            def _():
                acc_ref[...] += part

            @pl.when(kk == nk - 1)
            def _():
                put(acc_ref[...])

        if ride is not None:
            @pl.when((i == grid[0] - 1) & (j == grid[1] - 1) & (kk == nk - 1))
            def _():
                ride.wait(ride_in, ride_out, sems)

    if form == 'nn':
        a_spec = pl.BlockSpec((tm, tk), lambda i, j, kk: (i, kk))
        b_spec = (pl.BlockSpec((1, tk, tn), lambda i, j, kk: (j, kk, 0)) if stacked_b
                  else pl.BlockSpec((tk, tn), lambda i, j, kk: (kk, j)))
    elif form == 'nt':
        a_spec = pl.BlockSpec((tm, tk), lambda i, j, kk: (i, kk))
        b_spec = (pl.BlockSpec((1, tn, tk), lambda i, j, kk: (kk, j, 0)) if stacked_b
                  else pl.BlockSpec((tn, tk), lambda i, j, kk: (j, kk)))
    else:
        a_spec = pl.BlockSpec((tk, tm), lambda i, j, kk: (kk, i))
        b_spec = pl.BlockSpec((tk, tn), lambda i, j, kk: (kk, j))
    if stacked_out:
        o_spec = pl.BlockSpec((1, tm, tn), lambda i, j, kk: (j, i, 0))
        o_shape = jax.ShapeDtypeStruct((N_PLANE, m, tn), out_dtype)
    else:
        o_spec = pl.BlockSpec((tm, tn), lambda i, j, kk: (i, j))
        o_shape = jax.ShapeDtypeStruct((m, n), out_dtype)
    acc_shape = (tm, tn) if nk > 1 else (8, 128)
    if ride is None:
        return pl.pallas_call(
            body, grid=grid, in_specs=[a_spec, b_spec], out_specs=o_spec, out_shape=o_shape,
            scratch_shapes=[pltpu.VMEM(acc_shape, F32)],
            compiler_params=_cp(("parallel", "parallel", "arbitrary")), name=name)(a, b)
    res = pl.pallas_call(
        body, grid=grid, in_specs=[a_spec, b_spec] + ride.in_specs, out_specs=[o_spec] + ride.out_specs,
        out_shape=[o_shape] + ride.out_shape, scratch_shapes=[pltpu.VMEM(acc_shape, F32)] + ride.scratch,
        compiler_params=_cp(("arbitrary", "arbitrary", "arbitrary")), name=name)(a, b, *ride.arrs)
    return res[0], res[1:]


class Op:
    def __init__(self, arr, bs, im, load=None):
        self.arr, self.bs, self.im = arr, bs, im
        self.load = load or (lambda r: r[...].astype(F32))

    def spec(self):
        return pl.BlockSpec(self.bs, self.im)


def _sum_dirs(r):
    return r[0].astype(F32) + r[1].astype(F32)


def _tw_fwd(name, fn, grid, ins, outs):
    n_in = len(ins)

    def body(*refs):
        info = (pl.program_id(0), pl.program_id(1))
        vals = [op.load(r) for op, r in zip(ins, refs[:n_in])]
        res = fn(info, *vals)
        for r, v in zip(refs[n_in:], res):
            r[...] = v.astype(r.dtype)

    return pl.pallas_call(
        body, grid=grid, in_specs=[op.spec() for op in ins], out_specs=[op.spec() for op in outs],
        out_shape=[jax.ShapeDtypeStruct(*op.arr) for op in outs],
        compiler_params=_cp(("parallel", "arbitrary")), name=name)(*[op.arr for op in ins])


def _tw_bwd(name, fn, grid, tok, par, cots, tok_out, par_out, tok_add=None, sem=("parallel", "arbitrary")):
    n_tok = len(tok)
    flat_cots = [op for group in cots for op in group]
    extra = [tok_add] if tok_add is not None else []
    out_ops = list(tok_out) + list(par_out)

    def body(*refs):
        info = (pl.program_id(0), pl.program_id(1))
        it = iter(refs)
        tok_v = [op.load(next(it)) for op in tok]
        par_v = [op.load(next(it)) for op in par]
        cot_v = []
        for group in cots:
            vals = [op.load(next(it)) for op in group]
            cot_v.append(functools.reduce(lambda u, w: u + w, vals))
        add_v = [op.load(next(it)) for op in extra]
        _, pull = jax.vjp(lambda *a: fn(info, *a), *tok_v, *par_v)
        grads = pull(tuple(cot_v))
        for i in range(n_tok):
            r = next(it)
            g = grads[i] + add_v[0] if (i == 0 and add_v) else grads[i]
            r[...] = g.astype(r.dtype)
        first = pl.program_id(1) == 0
        for i in range(len(par)):
            r = next(it)
            g = grads[n_tok + i]

            @pl.when(first)
            def _(r=r, g=g):
                r[...] = g

            @pl.when(jnp.logical_not(first))
            def _(r=r, g=g):
                r[...] += g

    ops = tok + par + flat_cots + extra
    return pl.pallas_call(
        body, grid=grid, in_specs=[op.spec() for op in ops], out_specs=[op.spec() for op in out_ops],
        out_shape=[jax.ShapeDtypeStruct(*op.arr) for op in out_ops],
        compiler_params=_cp(sem), name=name)(*[op.arr for op in ops])


def _silu(x):
    return x * jax.nn.sigmoid(x)


def _rms(x):
    return x * lax.rsqrt(jnp.mean(x * x, axis=-1, keepdims=True) + EPS)


@functools.partial(jax.custom_vjp, nondiff_argnums=(1,))
def _shift_rows(x, k):
    return pltpu.roll(x, k % x.shape[0], 0)


def _shift_rows_fwd(x, k):
    return _shift_rows(x, k), None


def _shift_rows_bwd(k, _, g):
    return (_shift_rows(g, -k),)


_shift_rows.defvjp(_shift_rows_fwd, _shift_rows_bwd)


def _prenorm_fn(nctx_t, info, x, w, sc8, sh8):
    is_ctx = info[1] < nctx_t
    sc = jnp.where(is_ctx, sc8[1:2], sc8[0:1])
    sh = jnp.where(is_ctx, sh8[1:2], sh8[0:1])
    return (_rms(x) * w * (1.0 + sc) + sh,)


def _conv_fn(nctx_t, mode, info, x, w, b):
    n, c = x.shape
    is_ctx = info[1] < nctx_t
    idx = lax.broadcasted_iota(jnp.int32, (n, 1), 0)
    rr = jnp.where(is_ctx, idx, idx % GRID_W)
    first = rr == 0
    last = rr == jnp.where(is_ctx, n - 1, GRID_W - 1)
    prev = jnp.where(first, 0.0, _shift_rows(x, 1))
    nxt = jnp.where(last, 0.0, _shift_rows(x, -1))
    y = b + prev * w[0:1] + x * w[1:2] + nxt * w[2:3]
    y = _silu(y)
    if mode == 'none':
        return (y,)
    scale = GDN_DK ** -0.5 if mode == 'q' else 1.0
    outs = []
    for h in range(c // 128):
        yh = y[:, h * 128:(h + 1) * 128]
        outs.append(yh * lax.rsqrt(jnp.sum(yh * yh, axis=-1, keepdims=True) + EPS) * scale)
    return (jnp.concatenate(outs, axis=1),)


def _act_fn(info, x, p0, p1):
    lane = lax.broadcasted_iota(jnp.int32, x.shape, 1)
    sp = jax.nn.softplus(x + p1)
    g = -jnp.exp(p0) * sp
    bt = jax.nn.sigmoid(x)
    return (jnp.where(lane < 16, g, jnp.where(lane < 32, bt, jnp.where(lane < 96, sp, 0.0))),)


def _mixg_fn(info, o, zg, gnw):
    outs = []
    for h in range(GDN_HEADS):
        outs.append(_rms(o[:, h * 128:(h + 1) * 128]) * gnw)
    return (jnp.concatenate(outs, axis=1) * _silu(zg),)


def _mixs_fn(info, y, xs, zs, dl, snw):
    yy = (y + dl * xs) * _silu(zs)
    outs = []
    for g in range(4):
        outs.append(_rms(yy[:, g * 512:(g + 1) * 512]))
    return (jnp.concatenate(outs, axis=1) * snw,)


def _merge_fn(info, gg, gs, pg, ps):
    return (jax.nn.sigmoid(gg) * pg + jax.nn.sigmoid(gs) * ps,)


def _norm2_fn(info, xt, mo, g8, w, sc8, sh8):
    h1 = xt + g8[0:1] * mo
    return (h1, _rms(h1) * w * (1.0 + sc8[0:1]) + sh8[0:1])


def _swiglu_fn(info, ug, uu):
    return (_silu(ug) * uu,)


_NN = (((1,), (0,)), ((), ()))
_NT = (((1,), (1,)), ((), ()))
_TN = (((0,), (0,)), ((), ()))


def _mmh(a, b):
    return lax.dot_general(a, b, _NN, precision=HI, preferred_element_type=F32)


def _dot1(a, b, dims):
    return lax.dot_general(a.astype(_SCAN_DTYPE), b.astype(_SCAN_DTYPE), dims, preferred_element_type=F32)


def _mm(a, b):
    return _dot1(a, b, _NN)


def _mm_nt(a, b):
    return _dot1(a, b, _NT)


def _mm_tn(a, b):
    return _dot1(a, b, _TN)


def _split2(a):
    hi = a.astype(_SCAN_DTYPE)
    return hi, (a - hi.astype(F32)).astype(_SCAN_DTYPE)


def _dot3(a, b, dims):
    ah, al = _split2(a)
    bh, bl = _split2(b)
    d = lambda u, w: lax.dot_general(u, w, dims, preferred_element_type=F32)
    return d(ah, bh) + (d(ah, bl) + d(al, bh))


def _order_masks(d):
    i = lax.broadcasted_iota(jnp.int32, (CHUNK, CHUNK), 0)
    j = lax.broadcasted_iota(jnp.int32, (CHUNK, CHUNK), 1)
    s = jnp.where(d == 0, 1, -1) * (i - j)
    return (s >= 0).astype(F32), (s > 0).astype(F32)


@jax.custom_vjp
def _unit_tri_inv(mats):
    i = lax.broadcasted_iota(jnp.int32, (CHUNK, CHUNK), 0)
    j = lax.broadcasted_iota(jnp.int32, (CHUNK, CHUNK), 1)
    eye = (i == j).astype(F32)
    ps = [-a for a in mats]
    ts = [eye + p for p in ps]
    for _ in range(5):
        ps = [_dot3(p, p, _NN) for p in ps]
        ts = [t + _dot3(t, p, _NN) for t, p in zip(ts, ps)]
    return tuple(ts)


def _uti_fwd(mats):
    ts = _unit_tri_inv(mats)
    return ts, ts


def _uti_bwd(ts, gs):
    inner = [_dot3(g, t, _NT) for g, t in zip(gs, ts)]
    return (tuple(-_dot3(t, u, _TN) for t, u in zip(ts, inner)),)


_unit_tri_inv.defvjp(_uti_fwd, _uti_bwd)


def _lane_col(blk, lane_idx):
    lane = lax.broadcasted_iota(jnp.int32, blk.shape, 1)
    return jnp.sum(jnp.where(lane == lane_idx, blk, 0.0), axis=1, keepdims=True)


def _decay_mat(cum, incl):
    cb = jnp.broadcast_to(cum, (CHUNK, CHUNK))
    return jnp.exp(jnp.minimum(cb - cb.T, 0.0)) * incl


def _gdn_step(d, q, k, v, spb, *states):
    incl, strict = _order_masks(d)
    cum = _mmh(incl, spb)
    tot = jnp.sum(spb, axis=0, keepdims=True)
    hs = range(GDN_HEADS)
    cat = jnp.concatenate
    qs = [q[:, h * 128:(h + 1) * 128] for h in hs]
    ks = [k[:, h * 128:(h + 1) * 128] for h in hs]
    vs = [v[:, h * 128:(h + 1) * 128] for h in hs]
    gcum = [_lane_col(cum, d * GDN_HEADS + h) for h in hs]
    glast = [_lane_col(tot, d * GDN_HEADS + h) for h in hs]
    beta = [_lane_col(spb, 16 + d * GDN_HEADS + h) for h in hs]
    decay = [_decay_mat(gcum[h], incl) for h in hs]
    egc = [jnp.exp(gcum[h]) for h in hs]
    kb = [ks[h] * beta[h] for h in hs]
    kq = [_mm_nt(cat([kb[h], qs[h]], axis=0), ks[h]) for h in hs]
    ts = _unit_tri_inv(tuple(kq[h][:CHUNK] * decay[h] * strict for h in hs))
    wu = [_mm(ts[h], cat([kb[h] * egc[h], vs[h] * beta[h]], axis=1)) for h in hs]
    ws = [_mm(cat([wu[h][:, :128], qs[h] * egc[h]], axis=0), states[h]) for h in hs]
    vn = [wu[h][:, 128:] - ws[h][:CHUNK] for h in hs]
    outs = [ws[h][CHUNK:] + _mm(kq[h][CHUNK:] * decay[h], vn[h]) for h in hs]
    new_states = [states[h] * jnp.exp(glast[h]) + _mm_tn(ks[h] * jnp.exp(glast[h] - gcum[h]), vn[h]) for h in hs]
    return (cat(outs, axis=1), *new_states)


def _ssd_step(d, x, bm, cm, spb, alog, *states):
    incl, _ = _order_masks(d)
    lane1 = lax.broadcasted_iota(jnp.int32, (1, 128), 1)
    lo_lane = 32 + d * SSM_HEADS
    a_vec = jnp.where(lane1 >= lo_lane, jnp.where(lane1 < lo_lane + SSM_HEADS, -jnp.exp(alog), 0.0), 0.0)
    adt = spb * a_vec
    acum = _mmh(incl, adt)
    alast = jnp.sum(adt, axis=0, keepdims=True)
    low = lax.broadcasted_iota(jnp.int32, (CHUNK, 128), 1) < 64
    row_low = lax.broadcasted_iota(jnp.int32, (128, 1), 0) < 64
    ps = range(SSM_PAIRS)
    bg = [bm[:, g * 128:(g + 1) * 128] for g in range(4)]
    cg = [cm[:, g * 128:(g + 1) * 128] for g in range(4)]
    cb = [_mm_nt(cg[g], bg[g]) for g in range(4)]
    dt0 = [_lane_col(spb, lo_lane + 2 * p) for p in ps]
    dt1 = [_lane_col(spb, lo_lane + 2 * p + 1) for p in ps]
    ac0 = [_lane_col(acum, lo_lane + 2 * p) for p in ps]
    ac1 = [_lane_col(acum, lo_lane + 2 * p + 1) for p in ps]
    al0 = [_lane_col(alast, lo_lane + 2 * p) for p in ps]
    al1 = [_lane_col(alast, lo_lane + 2 * p + 1) for p in ps]
    xdt = [x[:, p * 128:(p + 1) * 128] * jnp.where(low, dt0[p], dt1[p]) for p in ps]
    yd0 = [_mm(cb[p // 4] * _decay_mat(ac0[p], incl), jnp.where(low, xdt[p], 0.0)) for p in ps]
    yd1 = [_mm(cb[p // 4] * _decay_mat(ac1[p], incl), jnp.where(low, 0.0, xdt[p])) for p in ps]
    yo = [_mm_nt(cg[p // 4], states[p]) for p in ps]
    ys = [yd0[p] + yd1[p] + jnp.exp(jnp.where(low, ac0[p], ac1[p])) * yo[p] for p in ps]
    e2 = [jnp.exp(jnp.where(low, al0[p] - ac0[p], al1[p] - ac1[p])) for p in ps]
    new = [_mm_tn(xdt[p] * e2[p], bg[p // 4]) for p in ps]
    new_states = [states[p] * jnp.exp(jnp.where(row_low, al0[p], al1[p])) + new[p] for p in ps]
    return (jnp.concatenate(ys, axis=1), *new_states)


def _chunk_of(d, p, nctx, nc):
    return jnp.where(d == 0, p, jnp.where(p < nctx, nctx - 1 - p, nctx + nc - 1 - p))


class _NoRide:
    n, arrs, in_specs, out_specs, out_shape, scratch = 0, [], [], [], [], []


def _scan_fwd(name, step, toks, pars, out_width, n_state, nctx, ride=None):
    t = toks[0].shape[0]
    nc = t // CHUNK
    n_tok, n_par = len(toks), len(pars)
    rd = ride if ride is not None else _NoRide

    def body(*refs):
        it = iter(refs)
        tok_refs = [next(it) for _ in range(n_tok)]
        par_refs = [next(it) for _ in range(n_par)]
        ride_in = [next(it) for _ in range(rd.n)]
        o_ref, ss_ref = next(it), next(it)
        ride_out = [next(it) for _ in range(rd.n)]
        s_scr = next(it)
        sems = list(it)
        d, p = pl.program_id(0), pl.program_id(1)
        if ride is not None:
            @pl.when((d == 0) & (p == 0))
            def _():
                ride.start(ride_in, ride_out, sems)

        @pl.when(p == 0)
        def _():
            s_scr[...] = jnp.zeros(s_scr.shape, F32)

        ss_ref[0, 0] = s_scr[...]
        res = step(d, *[r[...] for r in tok_refs], *[r[...] for r in par_refs], *[s_scr[h] for h in range(n_state)])
        o_ref[0] = res[0]
        for h in range(n_state):
            s_scr[h] = res[1 + h]
        if ride is not None:
            @pl.when((d == 1) & (p == nc - 1))
            def _():
                ride.wait(ride_in, ride_out, sems)

    ch = lambda d, p: _chunk_of(d, p, nctx, nc)
    in_specs = [pl.BlockSpec((CHUNK, a.shape[1]), lambda d, p: (ch(d, p), 0)) for a in toks]
    in_specs += [pl.BlockSpec(a.shape, lambda d, p: (0, 0)) for a in pars]
    return pl.pallas_call(
        body, grid=(2, nc), in_specs=in_specs + rd.in_specs,
        out_specs=[pl.BlockSpec((1, CHUNK, out_width), lambda d, p: (d, ch(d, p), 0)),
                   pl.BlockSpec((1, 1, n_state, 128, 128), lambda d, p: (d, p, 0, 0, 0))] + rd.out_specs,
        out_shape=[jax.ShapeDtypeStruct((2, t, out_width), F32),
                   jax.ShapeDtypeStruct((2, nc, n_state, 128, 128), F32)] + rd.out_shape,
        scratch_shapes=[pltpu.VMEM((n_state, 128, 128), F32)] + rd.scratch,
        compiler_params=_cp(("arbitrary", "arbitrary")), name=name)(*toks, *pars, *rd.arrs)


def _scan_bwd(name, step, toks, pars, ss, dout, n_state, nctx, ride=None):
    t = toks[0].shape[0]
    nc = t // CHUNK
    n_tok, n_par = len(toks), len(pars)
    rd = ride if ride is not None else _NoRide

    def body(*refs):
        it = iter(refs)
        tok_refs = [next(it) for _ in range(n_tok)]
        par_refs = [next(it) for _ in range(n_par)]
        ss_ref, do_ref = next(it), next(it)
        ride_in = [next(it) for _ in range(rd.n)]
        dtok_refs = [next(it) for _ in range(n_tok)]
        dpar_refs = [next(it) for _ in range(n_par)]
        ride_out = [next(it) for _ in range(rd.n)]
        ds_scr = next(it)
        sems = list(it)
        d, pr = pl.program_id(0), pl.program_id(1)
        if ride is not None:
            @pl.when((d == 0) & (pr == 0))
            def _():
                ride.start(ride_in, ride_out, sems)

            @pl.when((d == 1) & (pr == nc - 1))
            def _():
                ride.wait(ride_in, ride_out, sems)

        @pl.when(pr == 0)
        def _():
            ds_scr[...] = jnp.zeros(ds_scr.shape, F32)

        _, pull = jax.vjp(functools.partial(step, d), *[r[...] for r in tok_refs], *[r[...] for r in par_refs],
                          *[ss_ref[0, 0, h] for h in range(n_state)])
        grads = pull((do_ref[...], *[ds_scr[h] for h in range(n_state)]))
        for r, g in zip(dtok_refs, grads[:n_tok]):
            r[0] = g
        for h in range(n_state):
            ds_scr[h] = grads[n_tok + n_par + h]
        first = (d == 0) & (pr == 0)
        for r, g in zip(dpar_refs, grads[n_tok:n_tok + n_par]):
            @pl.when(first)
            def _(r=r, g=g):
                r[...] = g

            @pl.when(jnp.logical_not(first))
            def _(r=r, g=g):
                r[...] += g

    ch = lambda d, pr: _chunk_of(d, nc - 1 - pr, nctx, nc)
    in_specs = [pl.BlockSpec((CHUNK, a.shape[1]), lambda d, pr: (ch(d, pr), 0)) for a in toks]
    in_specs += [pl.BlockSpec(a.shape, lambda d, pr: (0, 0)) for a in pars]
    in_specs += [pl.BlockSpec((1, 1, n_state, 128, 128), lambda d, pr: (d, nc - 1 - pr, 0, 0, 0)),
                 pl.BlockSpec((CHUNK, dout.shape[1]), lambda d, pr: (ch(d, pr), 0))]
    out_specs = [pl.BlockSpec((1, CHUNK, a.shape[1]), lambda d, pr: (d, ch(d, pr), 0)) for a in toks]
    out_specs += [pl.BlockSpec(a.shape, lambda d, pr: (0, 0)) for a in pars]
    out_shape = [jax.ShapeDtypeStruct((2, t, a.shape[1]), F32) for a in toks]
    out_shape += [jax.ShapeDtypeStruct(a.shape, F32) for a in pars]
    return pl.pallas_call(
        body, grid=(2, nc), in_specs=in_specs + rd.in_specs, out_specs=out_specs + rd.out_specs,
        out_shape=out_shape + rd.out_shape, scratch_shapes=[pltpu.VMEM((n_state, 128, 128), F32)] + rd.scratch,
        compiler_params=_cp(("arbitrary", "arbitrary")), name=name)(*toks, *pars, ss, dout, *rd.arrs)


ADA_TN = 512


def _ada_fwd(cc, ada_w4, ada_b):
    per = ada_w4.shape[2] // ADA_TN
    n = N_PLANE * ada_w4.shape[2]

    def body(c_ref, w_ref, b_ref, o_ref):
        s = _silu(c_ref[...]).astype(_MXU_DTYPE)
        o_ref[...] = jnp.dot(s, w_ref[0].astype(_MXU_DTYPE), preferred_element_type=F32) + b_ref[...]

    return pl.pallas_call(
        body, grid=(n // ADA_TN,),
        in_specs=[pl.BlockSpec((8, D_MODEL), lambda j: (0, 0)),
                  pl.BlockSpec((1, D_MODEL, ADA_TN), lambda j: (j // per, 0, j % per)),
                  pl.BlockSpec((1, ADA_TN), lambda j: (0, j))],
        out_specs=pl.BlockSpec((8, ADA_TN), lambda j: (0, j)), out_shape=jax.ShapeDtypeStruct((8, n), F32),
        compiler_params=_cp(("parallel",)), name="ada_fwd")(cc, ada_w4, ada_b)


def _ada_bwd(cc, ada_w4, dmods):
    per = ada_w4.shape[2] // ADA_TN
    n = N_PLANE * ada_w4.shape[2]
    nj = n // ADA_TN

    def body(c_ref, w_ref, g_ref, dw_ref, db_ref, dc_ref):
        j = pl.program_id(0)
        g = g_ref[...]
        row = lax.broadcasted_iota(jnp.int32, g.shape, 0)
        g = jnp.where(row < 2, g, 0.0)
        s, pull = jax.vjp(_silu, c_ref[...])
        dw_ref[0] = lax.dot_general(s.astype(_MXU_DTYPE), g.astype(_MXU_DTYPE), _TN, preferred_element_type=F32)
        db_ref[...] = jnp.sum(g, axis=0, keepdims=True)
        ds = lax.dot_general(g.astype(_MXU_DTYPE), w_ref[0].astype(_MXU_DTYPE), _NT, preferred_element_type=F32)

        @pl.when(j == 0)
        def _():
            dc_ref[...] = ds

        @pl.when(j > 0)
        def _():
            dc_ref[...] += ds

        @pl.when(j == nj - 1)
        def _():
            dc_ref[...] = pull(dc_ref[...])[0]

    wspec = pl.BlockSpec((1, D_MODEL, ADA_TN), lambda j: (j // per, 0, j % per))
    return pl.pallas_call(
        body, grid=(nj,),
        in_specs=[pl.BlockSpec((8, D_MODEL), lambda j: (0, 0)), wspec, pl.BlockSpec((8, ADA_TN), lambda j: (0, j))],
        out_specs=[wspec, pl.BlockSpec((1, ADA_TN), lambda j: (0, j)), pl.BlockSpec((8, D_MODEL), lambda j: (0, 0))],
        out_shape=[jax.ShapeDtypeStruct(ada_w4.shape, F32), jax.ShapeDtypeStruct((1, n), F32),
                   jax.ShapeDtypeStruct((8, D_MODEL), F32)],
        compiler_params=_cp(("arbitrary",)), name="ada_bwd")(cc, ada_w4, dmods)


def _tail(h1, ff, mods, wf, tgt, nctx_t, tl):
    t = h1.shape[0]
    nt = t // tl

    def loss_fn(valid, h1v, ffv, g8, w, tg):
        h2 = h1v + g8[0:1] * ffv
        y = _rms(h2) * w
        err = (y - tg) ** 2
        return 0.5 * jnp.sum(jnp.mean(err, axis=-1, keepdims=True), axis=0, keepdims=True) * valid

    def body(h1_ref, ff_ref, g_ref, w_ref, t_ref, loss_ref, dh_ref, dff_ref, dg_ref, dw_ref):
        i = pl.program_id(0)
        valid = jnp.where(i < nctx_t, 0.0, 1.0)
        lv, pull = jax.vjp(functools.partial(loss_fn, valid), h1_ref[...], ff_ref[...].astype(F32), g_ref[...],
                           w_ref[...], t_ref[...])
        dh, dff, dg, dw, _ = pull(jnp.ones((1, 1), F32))
        dh_ref[...] = dh
        dff_ref[...] = dff.astype(dff_ref.dtype)
        lb = jnp.broadcast_to(lv, loss_ref.shape)

        @pl.when(i == 0)
        def _():
            loss_ref[...] = lb
            dg_ref[...] = dg
            dw_ref[...] = dw

        @pl.when(i > 0)
        def _():
            loss_ref[...] += lb
            dg_ref[...] += dg
            dw_ref[...] += dw

    tok = pl.BlockSpec((tl, D_MODEL), lambda i: (i, 0))
    return pl.pallas_call(
        body, grid=(nt,),
        in_specs=[tok, tok, pl.BlockSpec((8, D_MODEL), lambda i: (0, 5)), pl.BlockSpec((1, D_MODEL), lambda i: (0, 0)),
                  pl.BlockSpec((tl, D_MODEL), lambda i: (jnp.maximum(i - nctx_t, 0), 0))],
        out_specs=[pl.BlockSpec((8, 128), lambda i: (0, 0)), tok, tok, pl.BlockSpec((8, D_MODEL), lambda i: (0, 0)),
                   pl.BlockSpec((1, D_MODEL), lambda i: (0, 0))],
        out_shape=[jax.ShapeDtypeStruct((8, 128), F32), jax.ShapeDtypeStruct((t, D_MODEL), F32),
                   jax.ShapeDtypeStruct((t, D_MODEL), _MXU_DTYPE), jax.ShapeDtypeStruct((8, D_MODEL), F32),
                   jax.ShapeDtypeStruct((1, D_MODEL), F32)],
        compiler_params=_cp(("arbitrary",)), name="tail_loss")(h1, ff, mods, wf, tgt)


def _pack_w_in(w4):
    ns = w4.shape[2]
    placed = []
    for s0, s1, p0 in IN_SEGMENTS:
        for j in range(N_PLANE):
            lo, hi = max(s0, j * ns), min(s1, (j + 1) * ns)
            if lo < hi:
                placed.append((p0 + lo - s0, w4[j][:, lo - j * ns:hi - j * ns]))
    placed.sort(key=lambda e: e[0])
    pieces, end = [], 0
    for pos, piece in placed:
        assert pos == end, (pos, end)
        pieces.append(piece)
        end = pos + piece.shape[1]
    pieces.append(jnp.zeros((w4.shape[1], P_TOTAL - end), w4.dtype))
    return jnp.concatenate(pieces, axis=1)


def _unpack_w_in(g):
    ns = D_IN_PROJ // N_PLANE
    shards = []
    for j in range(N_PLANE):
        pieces = []
        for s0, s1, p0 in IN_SEGMENTS:
            lo, hi = max(s0, j * ns), min(s1, (j + 1) * ns)
            if lo < hi:
                pieces.append(g[:, p0 + lo - s0:p0 + hi - s0])
        shards.append(jnp.concatenate(pieces, axis=1))
    return jnp.stack(shards)


LATE_WEIGHTS = ['w_br_gdn', 'w_br_ssm', 'w_out', 'w_ffn_in', 'w_ffn_out']
COL_STACKED = ('ada_w', 'w_in', 'w_ffn_in')


def _from_shards(n, g):
    if n in COL_STACKED:
        return g
    if SHARD_AXIS[n] == 0:
        return g.reshape(N_PLANE * g.shape[1], g.shape[2])
    return jnp.concatenate([g[j] for j in range(N_PLANE)], axis=1)


def _to_dest_blocks(n, g):
    if g.ndim == 3:
        return g
    if SHARD_AXIS[n] == 0:
        return g.reshape(N_PLANE, g.shape[0] // N_PLANE, g.shape[1])
    sz = g.shape[1] // N_PLANE
    return jnp.stack([g[:, j * sz:(j + 1) * sz] for j in range(N_PLANE)])


def _local_step(x, c, ctx, tgt, W, late_shards=None):
    distributed = late_shards is not None
    lc, l = ctx.shape[0], x.shape[0]
    t = lc + l
    tl = 256
    assert lc == tl and l % tl == 0 and lc % CHUNK == 0
    nt, nctx_t, nctx = t // tl, lc // tl, lc // CHUNK
    act = _MXU_DTYPE
    r1 = lambda v: v.reshape(1, -1)

    xt = jnp.concatenate([ctx, x], axis=0)
    cc = jnp.concatenate([c, r1(W['c_ctx']), jnp.zeros((6, D_MODEL), F32)], axis=0)
    mods = _ada_fwd(cc, W['ada_w'], r1(W['ada_b']))

    def mod(kk):
        return Op(mods, (8, D_MODEL), lambda j, i, kk=kk: (0, kk))

    def tokop(arr, w=D_MODEL, off=0, tl_=tl):
        if arr.ndim == 3:
            return Op(arr, (2, tl_, w), lambda j, i: (0, i, off + j), load=_sum_dirs)
        return Op(arr, (tl_, w), lambda j, i: (i, off + j))

    def outop(n, dtype, w=D_MODEL, tl_=tl):
        return Op(((t, n), dtype), (tl_, w), lambda j, i: (i, j))

    def parop(arr, w, off=0):
        return Op(arr, (arr.shape[0], w), lambda j, i: (0, off + j))

    def parout(rows, n, w):
        return Op(((rows, n), F32), (rows, w), lambda j, i: (0, j))

    n1w = r1(W['norm1_w'])
    pre_fn = functools.partial(_prenorm_fn, nctx_t)
    pre_tok, pre_par = [tokop(xt)], [parop(n1w, D_MODEL), mod(1), mod(0)]
    (a,) = _tw_fwd("prenorm_fwd", pre_fn, (1, nt), pre_tok + pre_par, [outop(D_MODEL, act)])
    wp = _pack_w_in(W['w_in'])
    proj = _matmul(a, wp, 'nn', "in_proj")

    gcw, gcb = W['gdn_conv_w'], r1(W['gdn_conv_b'])
    scw, scb = W['ssm_conv_w'], r1(W['ssm_conv_b'])
    conv_parts = {}

    def conv_part(name, mode, poff, cw, cb, woff, width):
        fn = functools.partial(_conv_fn, nctx_t, mode)
        tok_ = [tokop(proj, 512, poff // 512)]
        par_ = [parop(cw, 512, woff // 512), parop(cb, 512, woff // 512)]
        conv_parts[name] = (fn, tok_, par_, width)
        (o,) = _tw_fwd("conv_" + name, fn, (width // 512, nt), tok_ + par_, [outop(width, F32, 512)])
        return o

    q = conv_part('q', 'q', P_QKV, gcw, gcb, 0, 1024)
    k = conv_part('k', 'k', P_QKV + 1024, gcw, gcb, 1024, 1024)
    v = conv_part('v', 'none', P_QKV + 2048, gcw, gcb, 2048, 1024)
    xs = conv_part('xs', 'none', P_XBC, scw, scb, 0, 2048)
    bm = conv_part('bm', 'none', P_XBC + 2048, scw, scb, 2048, 512)
    cm = conv_part('cm', 'none', P_XBC + 2560, scw, scb, 2560, 512)

    z16, z32 = jnp.zeros((16,), F32), jnp.zeros((32,), F32)
    p0 = jnp.concatenate([W['gdn_a_log'].reshape(-1), jnp.zeros((112,), F32)]).reshape(1, 128)
    p1 = jnp.concatenate([W['gdn_dt_bias'].reshape(-1), z16, W['ssm_dt_bias'].reshape(-1), z32]).reshape(1, 128)
    alog = jnp.concatenate([z32, W['ssm_a_log'].reshape(-1), z32]).reshape(1, 128)
    act_tok, act_par = [tokop(proj, 128, P_SMALL // 128)], [parop(p0, 128), parop(p1, 128)]
    (sp,) = _tw_fwd("small_act", _act_fn, (1, nt), act_tok + act_par, [outop(128, F32, 128)])

    gather_late = PlaneExchange([late_shards[n] for n in LATE_WEIGHTS], gather=True) if distributed else None
    o2, ss, *late = _scan_fwd("gdn_scan_fwd", _gdn_step, [q, k, v, sp], [], 1024, GDN_HEADS, nctx, ride=gather_late)
    if distributed:
        W = dict(W, **{n: _from_shards(n, g) for n, g in zip(LATE_WEIGHTS, late)})
    y2, hs = _scan_fwd("ssd_scan_fwd", _ssd_step, [xs, bm, cm, sp], [alog], 2048, SSM_PAIRS, nctx)

    tlm = 128
    ntm = t // tlm
    gnw = r1(W['gdn_norm_w'])
    dl = jnp.repeat(W['ssm_d'], 64).reshape(1, 2048)
    snw = r1(W['ssm_norm_w'])
    mg_tok = [tokop(o2, 1024, 0, tlm), tokop(proj, 1024, P_ZG // 1024, tlm)]
    mg_par = [parop(gnw, 128)]
    (og,) = _tw_fwd("mix_gdn", _mixg_fn, (1, ntm), mg_tok + mg_par, [outop(1024, act, 1024, tlm)])
    ms_tok = [tokop(y2, 2048, 0, tlm), tokop(xs, 2048, 0, tlm), tokop(proj, 2048, P_ZS // 2048, tlm)]
    ms_par = [parop(dl, 2048), parop(snw, 2048)]
    (yg,) = _tw_fwd("mix_ssm", _mixs_fn, (1, ntm), ms_tok + ms_par, [outop(2048, act, 2048, tlm)])

    pg = _matmul(og, W['w_br_gdn'], 'nn', "br_gdn")
    ps = _matmul(yg, W['w_br_ssm'], 'nn', "br_ssm")
    mr_tok = [tokop(proj, 1024, P_GATE // 1024), tokop(proj, 1024, P_GATE // 1024 + 1), tokop(pg), tokop(ps)]
    (mrg,) = _tw_fwd("merge", _merge_fn, (1, nt), mr_tok, [outop(1024, act)])
    mo = _matmul(mrg, W['w_out'], 'nn', "w_out")

    n2w = r1(W['norm2_w'])
    n2_tok, n2_par = [tokop(xt), tokop(mo)], [mod(2), parop(n2w, D_MODEL), mod(4), mod(3)]
    h1, f = _tw_fwd("norm2_fwd", _norm2_fn, (1, nt), n2_tok + n2_par, [outop(1024, F32), outop(1024, act)])
    u = _matmul(f, W['w_ffn_in'], 'nn', "ffn_in")
    swb = D_FF // 2
    sw_tok = [tokop(u, swb, 0), tokop(u, swb, D_FF // swb)]
    (sw,) = _tw_fwd("swiglu", _swiglu_fn, (D_FF // swb, nt), sw_tok, [outop(D_FF, act, swb)])
    ff = _matmul(sw, W['w_ffn_out'], 'nn', "ffn_out")

    loss8, dh1, dff, dg2, dnf = _tail(h1, ff, mods, r1(W['norm_f_w']), tgt, nctx_t, tl)
    loss = loss8[0, 0]

    G = {}
    G['norm_f_w'] = dnf.reshape(-1)
    G['w_ffn_out'] = _matmul(sw, dff, 'tn', "d_ffn_out")
    dsw = _matmul(dff, W['w_ffn_out'], 'nt', "d_sw")
    dug, duu = _tw_bwd("swiglu_bwd", _swiglu_fn, (D_FF // swb, nt), sw_tok, [], [[tokop(dsw, swb)]],
                       [outop(D_FF, act, swb), outop(D_FF, act, swb)], [])
    du = jnp.concatenate([dug, duu], axis=1)
    G['w_ffn_in'] = _matmul(f, du, 'tn', "d_ffn_in", stacked_out=True)
    df = _matmul(du, W['w_ffn_in'], 'nt', "d_f")
    dxt1, dmo, dg1, dn2, dsc2, dsh2 = _tw_bwd(
        "norm2_bwd", _norm2_fn, (1, nt), n2_tok, n2_par, [[tokop(dh1)], [tokop(df)]],
        [outop(1024, F32), outop(1024, act)],
        [parout(8, 1024, 1024), parout(1, 1024, 1024), parout(8, 1024, 1024), parout(8, 1024, 1024)])
    G['norm2_w'] = dn2.reshape(-1)
    G['w_out'] = _matmul(mrg, dmo, 'tn', "d_w_out")
    dmrg = _matmul(dmo, W['w_out'], 'nt', "d_mrg")
    dgg, dgs, dpg, dps = _tw_bwd("merge_bwd", _merge_fn, (1, nt), mr_tok, [], [[tokop(dmrg)]],
                                 [outop(1024, act), outop(1024, act), outop(1024, act), outop(1024, act)], [])
    G['w_br_gdn'] = _matmul(og, dpg, 'tn', "d_br_gdn")
    G['w_br_ssm'] = _matmul(yg, dps, 'tn', "d_br_ssm")
    dog = _matmul(dpg, W['w_br_gdn'], 'nt', "d_og")
    dyg = _matmul(dps, W['w_br_ssm'], 'nt', "d_yg")

    do, dzg, dgnw = _tw_bwd("mix_gdn_bwd", _mixg_fn, (1, ntm), mg_tok, mg_par, [[tokop(dog, 1024, 0, tlm)]],
                            [outop(1024, F32, 1024, tlm), outop(1024, act, 1024, tlm)], [parout(1, 128, 128)])
    G['gdn_norm_w'] = dgnw.reshape(-1)
    dy, dxs_a, dzs, ddl, dsnw = _tw_bwd(
        "mix_ssm_bwd", _mixs_fn, (1, ntm), ms_tok, ms_par, [[tokop(dyg, 2048, 0, tlm)]],
        [outop(2048, F32, 2048, tlm), outop(2048, F32, 2048, tlm), outop(2048, act, 2048, tlm)],
        [parout(1, 2048, 2048), parout(1, 2048, 2048)])
    G['ssm_d'] = ddl.reshape(SSM_HEADS, 64).sum(axis=1)
    G['ssm_norm_w'] = dsnw.reshape(-1)

    dq2, dk2, dv2, dsp_g = _scan_bwd("gdn_scan_bwd", _gdn_step, [q, k, v, sp], [], ss, do, GDN_HEADS, nctx)
    received = {}
    scatter_late = (PlaneExchange([_to_dest_blocks(n, G.pop(n)) for n in LATE_WEIGHTS], gather=False)
                    if distributed else None)
    dxs2, dbm2, dcm2, dsp_s, dalog, *got = _scan_bwd("ssd_scan_bwd", _ssd_step, [xs, bm, cm, sp], [alog], hs, dy,
                                                     SSM_PAIRS, nctx, ride=scatter_late)
    received.update(zip(LATE_WEIGHTS, got))
    G['ssm_a_log'] = dalog[0, 32:96].reshape(2, SSM_HEADS)

    dconv_w, dconv_b, dpre = {}, {}, {}

    def conv_bwd(name, cot_ops):
        fn, tok_, par_, width = conv_parts[name]
        dpre[name], dconv_w[name], dconv_b[name] = _tw_bwd(
            "conv_" + name + "_bwd", fn, (width // 512, nt), tok_, par_, [cot_ops], [outop(width, act, 512)],
            [parout(3, width, 512), parout(1, width, 512)])

    conv_bwd('q', [tokop(dq2, 512)])
    conv_bwd('k', [tokop(dk2, 512)])
    conv_bwd('v', [tokop(dv2, 512)])
    conv_bwd('xs', [tokop(dxs2, 512), tokop(dxs_a, 512)])
    conv_bwd('bm', [tokop(dbm2, 512)])
    conv_bwd('cm', [tokop(dcm2, 512)])
    G['gdn_conv_w'] = jnp.concatenate([dconv_w['q'], dconv_w['k'], dconv_w['v']], axis=1)
    G['gdn_conv_b'] = jnp.concatenate([dconv_b['q'], dconv_b['k'], dconv_b['v']], axis=1).reshape(-1)
    G['ssm_conv_w'] = jnp.concatenate([dconv_w['xs'], dconv_w['bm'], dconv_w['cm']], axis=1)
    G['ssm_conv_b'] = jnp.concatenate([dconv_b['xs'], dconv_b['bm'], dconv_b['cm']], axis=1).reshape(-1)

    dsmall, dp0, dp1 = _tw_bwd("small_act_bwd", _act_fn, (1, nt), act_tok, act_par,
                               [[tokop(dsp_g, 128), tokop(dsp_s, 128)]], [outop(128, act, 128)],
                               [parout(1, 128, 128), parout(1, 128, 128)])
    G['gdn_a_log'] = dp0[0, 0:16].reshape(2, GDN_HEADS)
    G['gdn_dt_bias'] = dp1[0, 0:16].reshape(2, GDN_HEADS)
    G['ssm_dt_bias'] = dp1[0, 32:96].reshape(2, SSM_HEADS)

    zpad = jnp.zeros((t, P_TOTAL - P_SMALL - 128), act)
    dproj = jnp.concatenate([dpre['q'], dpre['k'], dpre['v'], dzg, dzs, dpre['xs'], dpre['bm'], dpre['cm'],
                             dgg, dgs, dsmall, zpad], axis=1)
    G['w_in'] = _unpack_w_in(_matmul(a, dproj, 'tn', "d_w_in"))
    if distributed:
        da, (received['w_in'],) = _matmul(dproj, wp, 'nt', "d_a", ride=PlaneExchange([G.pop('w_in')], gather=False))
    else:
        da = _matmul(dproj, wp, 'nt', "d_a")

    gx_out = Op(((l, D_MODEL), F32), (tl, D_MODEL), lambda j, i: (jnp.maximum(i - nctx_t, 0), 0))
    grad_x, dn1, dsc1, dsh1 = _tw_bwd(
        "prenorm_bwd", pre_fn, (1, nt), pre_tok, pre_par, [[tokop(da)]], [gx_out],
        [parout(1, 1024, 1024), parout(8, 1024, 1024), parout(8, 1024, 1024)], tok_add=tokop(dxt1),
        sem=("arbitrary", "arbitrary"))
    G['norm1_w'] = dn1.reshape(-1)
    dmods = jnp.concatenate([dsh1, dsc1, dg1, dsh2, dsc2, dg2], axis=1)
    G['ada_w'], dab, dcc = _ada_bwd(cc, W['ada_w'], dmods)
    G['ada_b'] = dab.reshape(-1)
    G['c_ctx'] = dcc[1]
    return loss, grad_x, G, received


def _row_tile(r, c):
    for tr in (512, 256, 128, 64, 32, 16, 8):
        if r % tr == 0 and tr * c * 4 <= (1 << 20):
            return tr
    return r


def _sum4(name, rv):
    _, r, c = rv.shape
    tr = _row_tile(r, c)

    def body(r_ref, o_ref):
        o_ref[...] = ((r_ref[0] + r_ref[1]) + r_ref[2]) + r_ref[3]

    return pl.pallas_call(
        body, grid=(r // tr,), in_specs=[pl.BlockSpec((N_PLANE, tr, c), lambda i: (0, i, 0))],
        out_specs=pl.BlockSpec((tr, c), lambda i: (i, 0)), out_shape=jax.ShapeDtypeStruct((r, c), rv.dtype),
        compiler_params=_cp(("parallel",)), name=name)(rv)


def _adamw(name, w, m, v, p, q):
    r, c = w.shape
    tr = _row_tile(r, c)

    def body(w_ref, m_ref, v_ref, p_ref, q_ref, g_ref, d_ref, mo_ref, vo_ref):
        g = p_ref[...] + q_ref[...]
        mn = ADAM_B1 * m_ref[...] + (1.0 - ADAM_B1) * g
        vn = ADAM_B2 * v_ref[...] + (1.0 - ADAM_B2) * jnp.square(g)
        m_hat = mn / (1.0 - ADAM_B1 ** ADAM_STEP)
        v_hat = vn / (1.0 - ADAM_B2 ** ADAM_STEP)
        g_ref[...] = g
        d_ref[...] = -ADAM_LR * (m_hat / (jnp.sqrt(v_hat) + ADAM_EPS) + ADAM_WD * w_ref[...])
        mo_ref[...] = mn
        vo_ref[...] = vn

    spec = pl.BlockSpec((tr, c), lambda i: (i, 0))
    return pl.pallas_call(
        body, grid=(r // tr,), in_specs=[spec] * 5, out_specs=[spec] * 4,
        out_shape=[jax.ShapeDtypeStruct((r, c), F32)] * 4, compiler_params=_cp(("parallel",)), name=name)(w, m, v, p, q)


SMALL_ROWS = 24


def _pack_small(d):
    v = jnp.concatenate([d[n].reshape(-1) for n in SMALL])
    return jnp.pad(v, (0, SMALL_ROWS * 1024 - v.shape[0])).reshape(SMALL_ROWS, 1024)


def _unpack_small(buf, like):
    v = buf.reshape(-1)
    out, off = {}, 0
    for n in SMALL:
        sz = like[n].size
        out[n] = v[off:off + sz].reshape(like[n].shape)
        off += sz
    return out


def kernel(x, c, ctx, c_ctx, ada_w, ada_b, norm1_w, w_in, gdn_conv_w, gdn_conv_b, gdn_a_log, gdn_dt_bias, gdn_norm_w, ssm_conv_w, ssm_conv_b, ssm_a_log, ssm_dt_bias, ssm_d, ssm_norm_w, w_br_gdn, w_br_ssm, w_out, norm2_w, w_ffn_in, w_ffn_out, norm_f_w, loss_target, m_c_ctx, m_ada_w, m_ada_b, m_norm1_w, m_w_in, m_gdn_conv_w, m_gdn_conv_b, m_gdn_a_log, m_gdn_dt_bias, m_gdn_norm_w, m_ssm_conv_w, m_ssm_conv_b, m_ssm_a_log, m_ssm_dt_bias, m_ssm_d, m_ssm_norm_w, m_w_br_gdn, m_w_br_ssm, m_w_out, m_norm2_w, m_w_ffn_in, m_w_ffn_out, m_norm_f_w, v_c_ctx, v_ada_w, v_ada_b, v_norm1_w, v_w_in, v_gdn_conv_w, v_gdn_conv_b, v_gdn_a_log, v_gdn_dt_bias, v_gdn_norm_w, v_ssm_conv_w, v_ssm_conv_b, v_ssm_a_log, v_ssm_dt_bias, v_ssm_d, v_ssm_norm_w, v_w_br_gdn, v_w_br_ssm, v_w_out, v_norm2_w, v_w_ffn_in, v_w_ffn_out, v_norm_f_w):
    args = dict(locals())
    wl = {n: args[n] for n in WEIGHTS}
    ml = {n: args['m_' + n] for n in WEIGHTS}
    vl = {n: args['v_' + n] for n in WEIGHTS}

    def nodepth(n, a):
        return a if n in ('c_ctx', 'norm_f_w') else a[0]

    shard = {n: nodepth(n, wl[n]).astype(_MXU_DTYPE if n in MXU_WEIGHTS else F32) for n in SHARDED}
    first = [n for n in SHARDED if n not in LATE_WEIGHTS]
    W = {n: nodepth(n, wl[n]) for n in SMALL}
    for n, g in zip(first, _plane_exchange("all_gather_plane", [shard[n] for n in first], gather=True)):
        W[n] = _from_shards(n, g)

    loss_local, grad_x, G, received = _local_step(x[0], c, ctx[0], loss_target[0], W,
                                                  late_shards={n: shard[n] for n in LATE_WEIGHTS})
    loss = lax.psum(loss_local, ("x", "y", "c"))

    small_g = _pack_small(G)
    last = [n for n in SHARDED if n not in received]
    send = [_to_dest_blocks(n, G[n]) for n in last] + [jnp.broadcast_to(small_g[None], (N_PLANE,) + small_g.shape)]
    received.update(zip(last + ['small'], _plane_exchange("scatter_plane", send, gather=False)))
    names = SHARDED + ['small']
    plane_sum = [_sum4("sum4_" + n, received[n]) for n in names]
    other = _swap_sibling(plane_sum)

    wd = {n: nodepth(n, wl[n]) for n in WEIGHTS}
    md = {n: nodepth(n, ml[n]) for n in WEIGHTS}
    vd = {n: nodepth(n, vl[n]) for n in WEIGHTS}
    res = {}
    for n, p, q in zip(names, plane_sum, other):
        if n == 'small':
            outs = _adamw("adamw_small", _pack_small(wd), _pack_small(md), _pack_small(vd), p, q)
            unpacked = [_unpack_small(o, wd) for o in outs]
            for sn in SMALL:
                res[sn] = [u[sn].reshape(wl[sn].shape) for u in unpacked]
        else:
            outs = _adamw("adamw_" + n, wd[n], md[n], vd[n], p, q)
            res[n] = [o.reshape(wl[n].shape) for o in outs]
    flat = [res[n][kind] for kind in range(4) for n in WEIGHTS]
    return (loss, grad_x[None], *flat)
```

```python
import functools

import jax
import jax.numpy as jnp
from jax import lax
from jax.experimental import pallas as pl
from jax.experimental.pallas import tpu as pltpu

F32 = jnp.float32
HI = lax.Precision.HIGHEST
_MXU_DTYPE = jnp.bfloat16
_SCAN_DTYPE = jnp.bfloat16

D_MODEL = 1024
EPS = 1e-6
CHUNK = 64
GRID_W = 64
GDN_HEADS = 8
GDN_DK = 128
SSM_HEADS = 32
SSM_PAIRS = 16
D_FF = 2816
D_IN_PROJ = 11360
N_PLANE = 4
P_QKV, P_ZG, P_ZS, P_XBC, P_GATE, P_SMALL, P_TOTAL = 0, 3072, 4096, 6144, 9216, 11264, 11776
IN_SEGMENTS = [(0, 3072, P_QKV), (3072, 4096, P_ZG), (4096, 4112, P_SMALL), (4112, 4128, P_SMALL + 16),
               (4128, 6176, P_ZS), (6176, 9248, P_XBC), (9248, 9312, P_SMALL + 32), (9312, 11360, P_GATE)]
ADAM_LR, ADAM_B1, ADAM_B2, ADAM_EPS, ADAM_WD, ADAM_STEP = 0.001, 0.9, 0.999, 1e-08, 0.01, 10

VMEM_LIMIT = 48 * 1024 * 1024

WEIGHTS = ['c_ctx', 'ada_w', 'ada_b', 'norm1_w', 'w_in', 'gdn_conv_w', 'gdn_conv_b', 'gdn_a_log', 'gdn_dt_bias',
           'gdn_norm_w', 'ssm_conv_w', 'ssm_conv_b', 'ssm_a_log', 'ssm_dt_bias', 'ssm_d', 'ssm_norm_w', 'w_br_gdn',
           'w_br_ssm', 'w_out', 'norm2_w', 'w_ffn_in', 'w_ffn_out', 'norm_f_w']
SHARD_AXIS = {'ada_w': 1, 'w_in': 1, 'gdn_conv_w': 1, 'ssm_conv_w': 1, 'w_br_gdn': 0, 'w_br_ssm': 0, 'w_out': 0,
              'w_ffn_in': 1, 'w_ffn_out': 0}
SHARDED = [n for n in WEIGHTS if n in SHARD_AXIS]
SMALL = [n for n in WEIGHTS if n not in SHARD_AXIS]
MXU_WEIGHTS = ['ada_w', 'w_in', 'w_br_gdn', 'w_br_ssm', 'w_out', 'w_ffn_in', 'w_ffn_out']


def _cp(sem):
    return pltpu.CompilerParams(dimension_semantics=sem, vmem_limit_bytes=VMEM_LIMIT)


def _pick(n, cands):
    for c in cands:
        if n % c == 0:
            return c
    raise ValueError(f"no tile for {n}")


HBM = pl.BlockSpec(memory_space=pl.ANY)
MESH = pl.DeviceIdType.MESH


def _plane_peers():
    x, y, c = lax.axis_index("x"), lax.axis_index("y"), lax.axis_index("c")
    return (x, y, c), [(1 - x, y, c), (x, 1 - y, c), (1 - x, 1 - y, c)]


class PlaneExchange:
    def __init__(self, arrs, gather):
        self.arrs, self.gather, self.n = list(arrs), gather, len(arrs)
        self.in_specs = [HBM] * self.n
        self.out_specs = [HBM] * self.n
        self.out_shape = [jax.ShapeDtypeStruct((N_PLANE,) + a.shape if gather else a.shape, a.dtype) for a in self.arrs]
        self.scratch = [pltpu.SemaphoreType.DMA((3 * self.n,)), pltpu.SemaphoreType.DMA((3 * self.n,)),
                        pltpu.SemaphoreType.DMA((self.n,))]

    def _copies(self, ins, outs, sems):
        send_sems, recv_sems, local_sems = sems
        (x, y, c), peers = _plane_peers()
        me = 2 * x + y
        copies = []
        for ti in range(self.n):
            src = ins[ti] if self.gather else ins[ti].at[me]
            copies.append(pltpu.make_async_copy(src, outs[ti].at[me], local_sems.at[ti]))
            for kk, (px, py, pc) in enumerate(peers):
                src = ins[ti] if self.gather else ins[ti].at[2 * px + py]
                copies.append(pltpu.make_async_remote_copy(
                    src_ref=src, dst_ref=outs[ti].at[me], send_sem=send_sems.at[3 * ti + kk],
                    recv_sem=recv_sems.at[3 * ti + kk], device_id=(px, py, pc), device_id_type=MESH))
        return copies

    def start(self, ins, outs, sems):
        for cp in self._copies(ins, outs, sems):
            cp.start()

    def wait(self, ins, outs, sems):
        for cp in self._copies(ins, outs, sems):
            cp.wait()


def _plane_exchange(name, arrs, gather):
    ex = PlaneExchange(arrs, gather)
    n = ex.n

    def body(*refs):
        ins, outs, sems = refs[:n], refs[n:2 * n], refs[2 * n:]
        ex.start(ins, outs, sems)
        ex.wait(ins, outs, sems)

    return pl.pallas_call(body, in_specs=ex.in_specs, out_specs=ex.out_specs, out_shape=ex.out_shape,
                          scratch_shapes=ex.scratch, name=name)(*arrs)


def _swap_sibling(arrs):
    n = len(arrs)

    def body(*refs):
        ins, outs, send_sems, recv_sems = refs[:n], refs[n:2 * n], refs[2 * n], refs[2 * n + 1]
        x, y, c = lax.axis_index("x"), lax.axis_index("y"), lax.axis_index("c")
        copies = [pltpu.make_async_remote_copy(src_ref=ins[ti], dst_ref=outs[ti], send_sem=send_sems.at[ti],
                                               recv_sem=recv_sems.at[ti], device_id=(x, y, 1 - c), device_id_type=MESH)
                  for ti in range(n)]
        for cp in copies:
            cp.start()
        for cp in copies:
            cp.wait()

    return pl.pallas_call(
        body, in_specs=[HBM] * n, out_specs=[HBM] * n, out_shape=[jax.ShapeDtypeStruct(a.shape, a.dtype) for a in arrs],
        scratch_shapes=[pltpu.SemaphoreType.DMA((n,)), pltpu.SemaphoreType.DMA((n,))], name="swap_sibling")(*arrs)


def _matmul(a, b, form, name, out_dtype=F32, stacked_out=False, ride=None):
    dims = {'nn': (((1,), (0,)), ((), ())), 'nt': (((1,), (1,)), ((), ())), 'tn': (((0,), (0,)), ((), ()))}[form]
    stacked_b = b.ndim == 3
    ns = b.shape[2] if stacked_b else None
    if form == 'nn':
        m, k = a.shape
        n = b.shape[0] * ns if stacked_b else b.shape[1]
    elif form == 'nt':
        m, k = a.shape
        n = b.shape[1] if stacked_b else b.shape[0]
    else:
        k, m = a.shape
        n = b.shape[1]
    if stacked_out:
        tm, tn, tk = m, n // N_PLANE, _pick(k, (768, 512, 256, 128))
    elif form == 'tn' or (form == 'nn' and k > 4096):
        tm = m if m <= 3072 else _pick(m, (1024, 512, 256, 128))
        tn = _pick(n, (512, 256, 128))
        tk = _pick(k, (768, 512, 256, 128))
    else:
        tm = _pick(m, (768, 512, 256, 128))
        tn = ns if (stacked_b and form == 'nn') else (n if n <= 1024 else _pick(n, (512, 256, 128)))
        tk = ns if (stacked_b and form == 'nt') else (k if k <= 2048 else _pick(k, (512, 256, 128)))
    nk = k // tk
    grid = (m // tm, n // tn, nk)
    n_ride = ride.n if ride is not None else 0

    def body(*refs):
        a_ref, b_ref = refs[0], refs[1]
        ride_in = refs[2:2 + n_ride]
        o_ref = refs[2 + n_ride]
        ride_out = refs[3 + n_ride:3 + 2 * n_ride]
        acc_ref = refs[3 + 2 * n_ride]
        sems = refs[4 + 2 * n_ride:]
        i, j, kk = pl.program_id(0), pl.program_id(1), pl.program_id(2)
        if ride is not None:
            @pl.when((i == 0) & (j == 0) & (kk == 0))
            def _():
                ride.start(ride_in, ride_out, sems)

        def put(val):
            if stacked_out:
                o_ref[0] = val.astype(o_ref.dtype)
            else:
                o_ref[...] = val.astype(o_ref.dtype)

        bv = b_ref[0] if stacked_b else b_ref[...]
        part = lax.dot_general(a_ref[...].astype(_MXU_DTYPE), bv.astype(_MXU_DTYPE), dims, preferred_element_type=F32)
        if nk == 1:
            put(part)
        else:
            @pl.when(kk == 0)
            def _():
                acc_ref[...] = part

            @pl.when(kk > 0)
            def _():
                acc_ref[...] += part

            @pl.when(kk == nk - 1)
            def _():
                put(acc_ref[...])

        if ride is not None:
            @pl.when((i == grid[0] - 1) & (j == grid[1] - 1) & (kk == nk - 1))
            def _():
                ride.wait(ride_in, ride_out, sems)

    if form == 'nn':
        a_spec = pl.BlockSpec((tm, tk), lambda i, j, kk: (i, kk))
        b_spec = (pl.BlockSpec((1, tk, tn), lambda i, j, kk: (j, kk, 0)) if stacked_b
                  else pl.BlockSpec((tk, tn), lambda i, j, kk: (kk, j)))
    elif form == 'nt':
        a_spec = pl.BlockSpec((tm, tk), lambda i, j, kk: (i, kk))
        b_spec = (pl.BlockSpec((1, tn, tk), lambda i, j, kk: (kk, j, 0)) if stacked_b
                  else pl.BlockSpec((tn, tk), lambda i, j, kk: (j, kk)))
    else:
        a_spec = pl.BlockSpec((tk, tm), lambda i, j, kk: (kk, i))
        b_spec = pl.BlockSpec((tk, tn), lambda i, j, kk: (kk, j))
    if stacked_out:
        o_spec = pl.BlockSpec((1, tm, tn), lambda i, j, kk: (j, i, 0))
        o_shape = jax.ShapeDtypeStruct((N_PLANE, m, tn), out_dtype)
    else:
        o_spec = pl.BlockSpec((tm, tn), lambda i, j, kk: (i, j))
        o_shape = jax.ShapeDtypeStruct((m, n), out_dtype)
    acc_shape = (tm, tn) if nk > 1 else (8, 128)
    if ride is None:
        return pl.pallas_call(
            body, grid=grid, in_specs=[a_spec, b_spec], out_specs=o_spec, out_shape=o_shape,
            scratch_shapes=[pltpu.VMEM(acc_shape, F32)],
            compiler_params=_cp(("parallel", "parallel", "arbitrary")), name=name)(a, b)
    res = pl.pallas_call(
        body, grid=grid, in_specs=[a_spec, b_spec] + ride.in_specs, out_specs=[o_spec] + ride.out_specs,
        out_shape=[o_shape] + ride.out_shape, scratch_shapes=[pltpu.VMEM(acc_shape, F32)] + ride.scratch,
        compiler_params=_cp(("arbitrary", "arbitrary", "arbitrary")), name=name)(a, b, *ride.arrs)
    return res[0], res[1:]


class Op:
    def __init__(self, arr, bs, im, load=None):
        self.arr, self.bs, self.im = arr, bs, im
        self.load = load or (lambda r: r[...].astype(F32))

    def spec(self):
        return pl.BlockSpec(self.bs, self.im)


def _sum_dirs(r):
    return r[0].astype(F32) + r[1].astype(F32)


def _tw_fwd(name, fn, grid, ins, outs):
    n_in = len(ins)

    def body(*refs):
        info = (pl.program_id(0), pl.program_id(1))
        vals = [op.load(r) for op, r in zip(ins, refs[:n_in])]
        res = fn(info, *vals)
        for r, v in zip(refs[n_in:], res):
            r[...] = v.astype(r.dtype)

    return pl.pallas_call(
        body, grid=grid, in_specs=[op.spec() for op in ins], out_specs=[op.spec() for op in outs],
        out_shape=[jax.ShapeDtypeStruct(*op.arr) for op in outs],
        compiler_params=_cp(("parallel", "arbitrary")), name=name)(*[op.arr for op in ins])


def _tw_bwd(name, fn, grid, tok, par, cots, tok_out, par_out, tok_add=None, sem=("parallel", "arbitrary")):
    n_tok = len(tok)
    flat_cots = [op for group in cots for op in group]
    extra = [tok_add] if tok_add is not None else []
    out_ops = list(tok_out) + list(par_out)

    def body(*refs):
        info = (pl.program_id(0), pl.program_id(1))
        it = iter(refs)
        tok_v = [op.load(next(it)) for op in tok]
        par_v = [op.load(next(it)) for op in par]
        cot_v = []
        for group in cots:
            vals = [op.load(next(it)) for op in group]
            cot_v.append(functools.reduce(lambda u, w: u + w, vals))
        add_v = [op.load(next(it)) for op in extra]
        _, pull = jax.vjp(lambda *a: fn(info, *a), *tok_v, *par_v)
        grads = pull(tuple(cot_v))
        for i in range(n_tok):
            r = next(it)
            g = grads[i] + add_v[0] if (i == 0 and add_v) else grads[i]
            r[...] = g.astype(r.dtype)
        first = pl.program_id(1) == 0
        for i in range(len(par)):
            r = next(it)
            g = grads[n_tok + i]

            @pl.when(first)
            def _(r=r, g=g):
                r[...] = g

            @pl.when(jnp.logical_not(first))
            def _(r=r, g=g):
                r[...] += g

    ops = tok + par + flat_cots + extra
    return pl.pallas_call(
        body, grid=grid, in_specs=[op.spec() for op in ops], out_specs=[op.spec() for op in out_ops],
        out_shape=[jax.ShapeDtypeStruct(*op.arr) for op in out_ops],
        compiler_params=_cp(sem), name=name)(*[op.arr for op in ops])


def _silu(x):
    return x * jax.nn.sigmoid(x)


def _rms(x):
    return x * lax.rsqrt(jnp.mean(x * x, axis=-1, keepdims=True) + EPS)


@functools.partial(jax.custom_vjp, nondiff_argnums=(1,))
def _shift_rows(x, k):
    return pltpu.roll(x, k % x.shape[0], 0)


def _shift_rows_fwd(x, k):
    return _shift_rows(x, k), None


def _shift_rows_bwd(k, _, g):
    return (_shift_rows(g, -k),)


_shift_rows.defvjp(_shift_rows_fwd, _shift_rows_bwd)


def _prenorm_fn(nctx_t, info, x, w, sc8, sh8):
    is_ctx = info[1] < nctx_t
    sc = jnp.where(is_ctx, sc8[1:2], sc8[0:1])
    sh = jnp.where(is_ctx, sh8[1:2], sh8[0:1])
    return (_rms(x) * w * (1.0 + sc) + sh,)


def _conv_fn(nctx_t, mode, info, x, w, b):
    n, c = x.shape
    is_ctx = info[1] < nctx_t
    idx = lax.broadcasted_iota(jnp.int32, (n, 1), 0)
    rr = jnp.where(is_ctx, idx, idx % GRID_W)
    first = rr == 0
    last = rr == jnp.where(is_ctx, n - 1, GRID_W - 1)
    prev = jnp.where(first, 0.0, _shift_rows(x, 1))
    nxt = jnp.where(last, 0.0, _shift_rows(x, -1))
    y = b + prev * w[0:1] + x * w[1:2] + nxt * w[2:3]
    y = _silu(y)
    if mode == 'none':
        return (y,)
    scale = GDN_DK ** -0.5 if mode == 'q' else 1.0
    outs = []
    for h in range(c // 128):
        yh = y[:, h * 128:(h + 1) * 128]
        outs.append(yh * lax.rsqrt(jnp.sum(yh * yh, axis=-1, keepdims=True) + EPS) * scale)
    return (jnp.concatenate(outs, axis=1),)


def _act_fn(info, x, p0, p1):
    lane = lax.broadcasted_iota(jnp.int32, x.shape, 1)
    sp = jax.nn.softplus(x + p1)
    g = -jnp.exp(p0) * sp
    bt = jax.nn.sigmoid(x)
    return (jnp.where(lane < 16, g, jnp.where(lane < 32, bt, jnp.where(lane < 96, sp, 0.0))),)


def _mixg_fn(info, o, zg, gnw):
    outs = []
    for h in range(GDN_HEADS):
        outs.append(_rms(o[:, h * 128:(h + 1) * 128]) * gnw)
    return (jnp.concatenate(outs, axis=1) * _silu(zg),)


def _mixs_fn(info, y, xs, zs, dl, snw):
    yy = (y + dl * xs) * _silu(zs)
    outs = []
    for g in range(4):
        outs.append(_rms(yy[:, g * 512:(g + 1) * 512]))
    return (jnp.concatenate(outs, axis=1) * snw,)


def _merge_fn(info, gg, gs, pg, ps):
    return (jax.nn.sigmoid(gg) * pg + jax.nn.sigmoid(gs) * ps,)


def _norm2_fn(info, xt, mo, g8, w, sc8, sh8):
    h1 = xt + g8[0:1] * mo
    return (h1, _rms(h1) * w * (1.0 + sc8[0:1]) + sh8[0:1])


def _swiglu_fn(info, ug, uu):
    return (_silu(ug) * uu,)


_NN = (((1,), (0,)), ((), ()))
_NT = (((1,), (1,)), ((), ()))
_TN = (((0,), (0,)), ((), ()))


def _mmh(a, b):
    return lax.dot_general(a, b, _NN, precision=HI, preferred_element_type=F32)


def _dot1(a, b, dims):
    return lax.dot_general(a.astype(_SCAN_DTYPE), b.astype(_SCAN_DTYPE), dims, preferred_element_type=F32)


def _mm(a, b):
    return _dot1(a, b, _NN)


def _mm_nt(a, b):
    return _dot1(a, b, _NT)


def _mm_tn(a, b):
    return _dot1(a, b, _TN)


def _split2(a):
    hi = a.astype(_SCAN_DTYPE)
    return hi, (a - hi.astype(F32)).astype(_SCAN_DTYPE)


def _dot3(a, b, dims):
    ah, al = _split2(a)
    bh, bl = _split2(b)
    d = lambda u, w: lax.dot_general(u, w, dims, preferred_element_type=F32)
    return d(ah, bh) + (d(ah, bl) + d(al, bh))


def _order_masks(d):
    i = lax.broadcasted_iota(jnp.int32, (CHUNK, CHUNK), 0)
    j = lax.broadcasted_iota(jnp.int32, (CHUNK, CHUNK), 1)
    s = jnp.where(d == 0, 1, -1) * (i - j)
    return (s >= 0).astype(F32), (s > 0).astype(F32)


@jax.custom_vjp
def _unit_tri_inv(mats):
    i = lax.broadcasted_iota(jnp.int32, (CHUNK, CHUNK), 0)
    j = lax.broadcasted_iota(jnp.int32, (CHUNK, CHUNK), 1)
    eye = (i == j).astype(F32)
    ps = [-a for a in mats]
    ts = [eye + p for p in ps]
    for _ in range(5):
        ps = [_dot3(p, p, _NN) for p in ps]
        ts = [t + _dot3(t, p, _NN) for t, p in zip(ts, ps)]
    return tuple(ts)


def _uti_fwd(mats):
    ts = _unit_tri_inv(mats)
    return ts, ts


def _uti_bwd(ts, gs):
    inner = [_dot3(g, t, _NT) for g, t in zip(gs, ts)]
    return (tuple(-_dot3(t, u, _TN) for t, u in zip(ts, inner)),)


_unit_tri_inv.defvjp(_uti_fwd, _uti_bwd)


@jax.custom_vjp
def _unit_tri_inv_given(mats, ts):
    return ts


def _utig_fwd(mats, ts):
    return ts, ts


def _utig_bwd(ts, gs):
    return _uti_bwd(ts, gs)[0], tuple(jnp.zeros_like(t) for t in ts)


_unit_tri_inv_given.defvjp(_utig_fwd, _utig_bwd)


def _lane_col(blk, lane_idx):
    lane = lax.broadcasted_iota(jnp.int32, blk.shape, 1)
    return jnp.sum(jnp.where(lane == lane_idx, blk, 0.0), axis=1, keepdims=True)


def _decay_mat(cum, incl):
    cb = jnp.broadcast_to(cum, (CHUNK, CHUNK))
    return jnp.exp(jnp.minimum(cb - cb.T, 0.0)) * incl


def _gdn_step(d, q, k, v, spb, *states, aux=None, want_aux=False):
    incl, strict = _order_masks(d)
    cum = _mmh(incl, spb)
    tot = jnp.sum(spb, axis=0, keepdims=True)
    hs = range(GDN_HEADS)
    cat = jnp.concatenate
    qs = [q[:, h * 128:(h + 1) * 128] for h in hs]
    ks = [k[:, h * 128:(h + 1) * 128] for h in hs]
    vs = [v[:, h * 128:(h + 1) * 128] for h in hs]
    gcum = [_lane_col(cum, d * GDN_HEADS + h) for h in hs]
    glast = [_lane_col(tot, d * GDN_HEADS + h) for h in hs]
    beta = [_lane_col(spb, 16 + d * GDN_HEADS + h) for h in hs]
    decay = [_decay_mat(gcum[h], incl) for h in hs]
    egc = [jnp.exp(gcum[h]) for h in hs]
    kb = [ks[h] * beta[h] for h in hs]
    kq = [_mm_nt(cat([kb[h], qs[h]], axis=0), ks[h]) for h in hs]
    mats = tuple(kq[h][:CHUNK] * decay[h] * strict for h in hs)
    ts = _unit_tri_inv(mats) if aux is None else _unit_tri_inv_given(mats, tuple(aux))
    wu = [_mm(ts[h], cat([kb[h] * egc[h], vs[h] * beta[h]], axis=1)) for h in hs]
    ws = [_mm(cat([wu[h][:, :128], qs[h] * egc[h]], axis=0), states[h]) for h in hs]
    vn = [wu[h][:, 128:] - ws[h][:CHUNK] for h in hs]
    outs = [ws[h][CHUNK:] + _mm(kq[h][CHUNK:] * decay[h], vn[h]) for h in hs]
    new_states = [states[h] * jnp.exp(glast[h]) + _mm_tn(ks[h] * jnp.exp(glast[h] - gcum[h]), vn[h]) for h in hs]
    return (cat(outs, axis=1), *new_states, *(ts if want_aux else ()))


def _split3(a):
    a1 = a.astype(_SCAN_DTYPE)
    r = a - a1.astype(F32)
    a2 = r.astype(_SCAN_DTYPE)
    return a1, a2, (r - a2.astype(F32)).astype(_SCAN_DTYPE)


def _exact_dot(a, e, dims, split_lhs):
    parts = _split3(a if split_lhs else e)
    d = (lambda u: lax.dot_general(u, e, dims, preferred_element_type=F32)) if split_lhs else \
        (lambda u: lax.dot_general(a, u, dims, preferred_element_type=F32))
    return d(parts[0]) + (d(parts[1]) + d(parts[2]))


@jax.custom_vjp
def _spread(a, e):
    return _exact_dot(a, e, _NN, True)


def _spread_fwd(a, e):
    return _spread(a, e), e


def _spread_bwd(e, g):
    return _exact_dot(g, e, _NT, True), jnp.zeros_like(e)


_spread.defvjp(_spread_fwd, _spread_bwd)


@jax.custom_vjp
def _colsum_bcast(z):
    return _exact_dot(jnp.ones((z.shape[0], z.shape[0]), _SCAN_DTYPE), z, _NN, False)


def _colsum_fwd(z):
    return _colsum_bcast(z), None


def _colsum_bwd(_, g):
    return (_colsum_bcast(g),)


_colsum_bcast.defvjp(_colsum_fwd, _colsum_bwd)


def _ssd_step(d, x, bm, cm, spb, alog, *states):
    wdt = x.shape[1]
    incl, _ = _order_masks(d)
    lane1 = lax.broadcasted_iota(jnp.int32, (1, 128), 1)
    lo_lane = 32 + d * SSM_HEADS
    a_vec = jnp.where(lane1 >= lo_lane, jnp.where(lane1 < lo_lane + SSM_HEADS, -jnp.exp(alog), 0.0), 0.0)
    adt = spb * a_vec
    acum = _mmh(incl, adt)
    alast = jnp.sum(adt, axis=0, keepdims=True)
    e = (lax.broadcasted_iota(jnp.int32, (128, wdt), 0)
         == lo_lane + lax.broadcasted_iota(jnp.int32, (128, wdt), 1) // 64).astype(_SCAN_DTYPE)
    dt2 = _spread(spb, e)
    ac2 = _spread(acum, e)
    al2 = _spread(jnp.broadcast_to(alast, (8, 128)), e)[0:1]
    ci = lax.broadcasted_iota(jnp.int32, (CHUNK, wdt), 0)
    pos = lax.broadcasted_iota(jnp.int32, (CHUNK, wdt), 1) % 64
    row = _colsum_bcast(jnp.where(ci == pos, ac2, 0.0))
    incl_t = (jnp.where(d == 0, 1, -1) * (ci - pos) >= 0).astype(F32)
    seg = jnp.exp(jnp.minimum(ac2 - row, 0.0)) * incl_t
    xdt = x * dt2
    gam = jnp.exp(ac2)
    xe = xdt * jnp.exp(al2 - ac2)
    low = lax.broadcasted_iota(jnp.int32, (CHUNK, 128), 1) < 64
    row_low = lax.broadcasted_iota(jnp.int32, (128, 1), 0) < 64
    ps = range(SSM_PAIRS)
    sl = [slice(p * 128, (p + 1) * 128) for p in ps]
    bg = [bm[:, g * 128:(g + 1) * 128] for g in range(4)]
    cg = [cm[:, g * 128:(g + 1) * 128] for g in range(4)]
    cb2 = [_mm_nt(cg[g], jnp.concatenate([bg[g], bg[g]], axis=0)) for g in range(4)]
    xd = [jnp.concatenate([jnp.where(low, xdt[:, sl[p]], 0.0), jnp.where(low, 0.0, xdt[:, sl[p]])], axis=0) for p in ps]
    yd = [_mm(cb2[p // 4] * seg[:, sl[p]], xd[p]) for p in ps]
    yo = [_mm_nt(cg[p // 4], states[p]) for p in ps]
    ys = [yd[p] + gam[:, sl[p]] * yo[p] for p in ps]
    new = [_mm_tn(xe[:, sl[p]], bg[p // 4]) for p in ps]
    al0 = [_lane_col(alast, lo_lane + 2 * p) for p in ps]
    al1 = [_lane_col(alast, lo_lane + 2 * p + 1) for p in ps]
    new_states = [states[p] * jnp.exp(jnp.where(row_low, al0[p], al1[p])) + new[p] for p in ps]
    return (jnp.concatenate(ys, axis=1), *new_states)


def _chunk_of(d, p, nctx, nc):
    return jnp.where(d == 0, p, jnp.where(p < nctx, nctx - 1 - p, nctx + nc - 1 - p))


class _NoRide:
    n, arrs, in_specs, out_specs, out_shape, scratch = 0, [], [], [], [], []


def _scan_fwd(name, step, toks, pars, out_width, n_state, nctx, ride=None, n_aux=0):
    t = toks[0].shape[0]
    nc = t // CHUNK
    n_tok, n_par = len(toks), len(pars)
    rd = ride if ride is not None else _NoRide
    kw = dict(want_aux=True) if n_aux else {}

    def body(*refs):
        it = iter(refs)
        tok_refs = [next(it) for _ in range(n_tok)]
        par_refs = [next(it) for _ in range(n_par)]
        ride_in = [next(it) for _ in range(rd.n)]
        o_ref, ss_ref = next(it), next(it)
        aux_ref = next(it) if n_aux else None
        ride_out = [next(it) for _ in range(rd.n)]
        s_scr = next(it)
        sems = list(it)
        d, p = pl.program_id(0), pl.program_id(1)
        if ride is not None:
            @pl.when((d == 0) & (p == 0))
            def _():
                ride.start(ride_in, ride_out, sems)

        @pl.when(p == 0)
        def _():
            s_scr[...] = jnp.zeros(s_scr.shape, F32)

        ss_ref[0, 0] = s_scr[...]
        res = step(d, *[r[...] for r in tok_refs], *[r[...] for r in par_refs], *[s_scr[h] for h in range(n_state)],
                   **kw)
        o_ref[0] = res[0]
        for h in range(n_state):
            s_scr[h] = res[1 + h]
        for i in range(n_aux):
            aux_ref[0, 0, i] = res[1 + n_state + i]
        if ride is not None:
            @pl.when((d == 1) & (p == nc - 1))
            def _():
                ride.wait(ride_in, ride_out, sems)

    ch = lambda d, p: _chunk_of(d, p, nctx, nc)
    in_specs = [pl.BlockSpec((CHUNK, a.shape[1]), lambda d, p: (ch(d, p), 0)) for a in toks]
    in_specs += [pl.BlockSpec(a.shape, lambda d, p: (0, 0)) for a in pars]
    aux_specs = [pl.BlockSpec((1, 1, n_aux, CHUNK, CHUNK), lambda d, p: (d, p, 0, 0, 0))] if n_aux else []
    aux_shape = [jax.ShapeDtypeStruct((2, nc, n_aux, CHUNK, CHUNK), F32)] if n_aux else []
    return pl.pallas_call(
        body, grid=(2, nc), in_specs=in_specs + rd.in_specs,
        out_specs=[pl.BlockSpec((1, CHUNK, out_width), lambda d, p: (d, ch(d, p), 0)),
                   pl.BlockSpec((1, 1, n_state, 128, 128), lambda d, p: (d, p, 0, 0, 0))] + aux_specs + rd.out_specs,
        out_shape=[jax.ShapeDtypeStruct((2, t, out_width), F32),
                   jax.ShapeDtypeStruct((2, nc, n_state, 128, 128), F32)] + aux_shape + rd.out_shape,
        scratch_shapes=[pltpu.VMEM((n_state, 128, 128), F32)] + rd.scratch,
        compiler_params=_cp(("arbitrary", "arbitrary")), name=name)(*toks, *pars, *rd.arrs)


def _scan_bwd(name, step, toks, pars, ss, dout, n_state, nctx, ride=None, aux=None):
    t = toks[0].shape[0]
    nc = t // CHUNK
    n_tok, n_par = len(toks), len(pars)
    rd = ride if ride is not None else _NoRide
    n_aux = aux.shape[2] if aux is not None else 0

    def body(*refs):
        it = iter(refs)
        tok_refs = [next(it) for _ in range(n_tok)]
        par_refs = [next(it) for _ in range(n_par)]
        ss_ref, do_ref = next(it), next(it)
        aux_ref = next(it) if n_aux else None
        ride_in = [next(it) for _ in range(rd.n)]
        dtok_refs = [next(it) for _ in range(n_tok)]
        dpar_refs = [next(it) for _ in range(n_par)]
        ride_out = [next(it) for _ in range(rd.n)]
        ds_scr = next(it)
        sems = list(it)
        d, pr = pl.program_id(0), pl.program_id(1)
        if ride is not None:
            @pl.when((d == 0) & (pr == 0))
            def _():
                ride.start(ride_in, ride_out, sems)

            @pl.when((d == 1) & (pr == nc - 1))
            def _():
                ride.wait(ride_in, ride_out, sems)

        @pl.when(pr == 0)
        def _():
            ds_scr[...] = jnp.zeros(ds_scr.shape, F32)

        kw = dict(aux=[aux_ref[0, 0, i] for i in range(n_aux)]) if n_aux else {}
        _, pull = jax.vjp(functools.partial(step, d, **kw), *[r[...] for r in tok_refs], *[r[...] for r in par_refs],
                          *[ss_ref[0, 0, h] for h in range(n_state)])
        grads = pull((do_ref[...], *[ds_scr[h] for h in range(n_state)]))
        for r, g in zip(dtok_refs, grads[:n_tok]):
            r[0] = g
        for h in range(n_state):
            ds_scr[h] = grads[n_tok + n_par + h]
        first = (d == 0) & (pr == 0)
        for r, g in zip(dpar_refs, grads[n_tok:n_tok + n_par]):
            @pl.when(first)
            def _(r=r, g=g):
                r[...] = g

            @pl.when(jnp.logical_not(first))
            def _(r=r, g=g):
                r[...] += g

    ch = lambda d, pr: _chunk_of(d, nc - 1 - pr, nctx, nc)
    in_specs = [pl.BlockSpec((CHUNK, a.shape[1]), lambda d, pr: (ch(d, pr), 0)) for a in toks]
    in_specs += [pl.BlockSpec(a.shape, lambda d, pr: (0, 0)) for a in pars]
    in_specs += [pl.BlockSpec((1, 1, n_state, 128, 128), lambda d, pr: (d, nc - 1 - pr, 0, 0, 0)),
                 pl.BlockSpec((CHUNK, dout.shape[1]), lambda d, pr: (ch(d, pr), 0))]
    if n_aux:
        in_specs += [pl.BlockSpec((1, 1, n_aux, CHUNK, CHUNK), lambda d, pr: (d, nc - 1 - pr, 0, 0, 0))]
    out_specs = [pl.BlockSpec((1, CHUNK, a.shape[1]), lambda d, pr: (d, ch(d, pr), 0)) for a in toks]
    out_specs += [pl.BlockSpec(a.shape, lambda d, pr: (0, 0)) for a in pars]
    out_shape = [jax.ShapeDtypeStruct((2, t, a.shape[1]), F32) for a in toks]
    out_shape += [jax.ShapeDtypeStruct(a.shape, F32) for a in pars]
    return pl.pallas_call(
        body, grid=(2, nc), in_specs=in_specs + rd.in_specs, out_specs=out_specs + rd.out_specs,
        out_shape=out_shape + rd.out_shape, scratch_shapes=[pltpu.VMEM((n_state, 128, 128), F32)] + rd.scratch,
        compiler_params=_cp(("arbitrary", "arbitrary")), name=name)(
            *toks, *pars, ss, dout, *([aux] if n_aux else []), *rd.arrs)


ADA_TN = 512


def _ada_fwd(cc, ada_w4, ada_b):
    per = ada_w4.shape[2] // ADA_TN
    n = N_PLANE * ada_w4.shape[2]

    def body(c_ref, w_ref, b_ref, o_ref):
        s = _silu(c_ref[...]).astype(_MXU_DTYPE)
        o_ref[...] = jnp.dot(s, w_ref[0].astype(_MXU_DTYPE), preferred_element_type=F32) + b_ref[...]

    return pl.pallas_call(
        body, grid=(n // ADA_TN,),
        in_specs=[pl.BlockSpec((8, D_MODEL), lambda j: (0, 0)),
                  pl.BlockSpec((1, D_MODEL, ADA_TN), lambda j: (j // per, 0, j % per)),
                  pl.BlockSpec((1, ADA_TN), lambda j: (0, j))],
        out_specs=pl.BlockSpec((8, ADA_TN), lambda j: (0, j)), out_shape=jax.ShapeDtypeStruct((8, n), F32),
        compiler_params=_cp(("parallel",)), name="ada_fwd")(cc, ada_w4, ada_b)


def _ada_bwd(cc, ada_w4, dmods):
    per = ada_w4.shape[2] // ADA_TN
    n = N_PLANE * ada_w4.shape[2]
    nj = n // ADA_TN

    def body(c_ref, w_ref, g_ref, dw_ref, db_ref, dc_ref):
        j = pl.program_id(0)
        g = g_ref[...]
        row = lax.broadcasted_iota(jnp.int32, g.shape, 0)
        g = jnp.where(row < 2, g, 0.0)
        s, pull = jax.vjp(_silu, c_ref[...])
        dw_ref[0] = lax.dot_general(s.astype(_MXU_DTYPE), g.astype(_MXU_DTYPE), _TN, preferred_element_type=F32)
        db_ref[...] = jnp.sum(g, axis=0, keepdims=True)
        ds = lax.dot_general(g.astype(_MXU_DTYPE), w_ref[0].astype(_MXU_DTYPE), _NT, preferred_element_type=F32)

        @pl.when(j == 0)
        def _():
            dc_ref[...] = ds

        @pl.when(j > 0)
        def _():
            dc_ref[...] += ds

        @pl.when(j == nj - 1)
        def _():
            dc_ref[...] = pull(dc_ref[...])[0]

    wspec = pl.BlockSpec((1, D_MODEL, ADA_TN), lambda j: (j // per, 0, j % per))
    return pl.pallas_call(
        body, grid=(nj,),
        in_specs=[pl.BlockSpec((8, D_MODEL), lambda j: (0, 0)), wspec, pl.BlockSpec((8, ADA_TN), lambda j: (0, j))],
        out_specs=[wspec, pl.BlockSpec((1, ADA_TN), lambda j: (0, j)), pl.BlockSpec((8, D_MODEL), lambda j: (0, 0))],
        out_shape=[jax.ShapeDtypeStruct(ada_w4.shape, F32), jax.ShapeDtypeStruct((1, n), F32),
                   jax.ShapeDtypeStruct((8, D_MODEL), F32)],
        compiler_params=_cp(("arbitrary",)), name="ada_bwd")(cc, ada_w4, dmods)


def _tail(h1, ff, mods, wf, tgt, nctx_t, tl):
    t = h1.shape[0]
    nt = t // tl

    def loss_fn(valid, h1v, ffv, g8, w, tg):
        h2 = h1v + g8[0:1] * ffv
        y = _rms(h2) * w
        err = (y - tg) ** 2
        return 0.5 * jnp.sum(jnp.mean(err, axis=-1, keepdims=True), axis=0, keepdims=True) * valid

    def body(h1_ref, ff_ref, g_ref, w_ref, t_ref, loss_ref, dh_ref, dff_ref, dg_ref, dw_ref):
        i = pl.program_id(0)
        valid = jnp.where(i < nctx_t, 0.0, 1.0)
        lv, pull = jax.vjp(functools.partial(loss_fn, valid), h1_ref[...], ff_ref[...].astype(F32), g_ref[...],
                           w_ref[...], t_ref[...])
        dh, dff, dg, dw, _ = pull(jnp.ones((1, 1), F32))
        dh_ref[...] = dh
        dff_ref[...] = dff.astype(dff_ref.dtype)
        lb = jnp.broadcast_to(lv, loss_ref.shape)

        @pl.when(i == 0)
        def _():
            loss_ref[...] = lb
            dg_ref[...] = dg
            dw_ref[...] = dw

        @pl.when(i > 0)
        def _():
            loss_ref[...] += lb
            dg_ref[...] += dg
            dw_ref[...] += dw

    tok = pl.BlockSpec((tl, D_MODEL), lambda i: (i, 0))
    return pl.pallas_call(
        body, grid=(nt,),
        in_specs=[tok, tok, pl.BlockSpec((8, D_MODEL), lambda i: (0, 5)), pl.BlockSpec((1, D_MODEL), lambda i: (0, 0)),
                  pl.BlockSpec((tl, D_MODEL), lambda i: (jnp.maximum(i - nctx_t, 0), 0))],
        out_specs=[pl.BlockSpec((8, 128), lambda i: (0, 0)), tok, tok, pl.BlockSpec((8, D_MODEL), lambda i: (0, 0)),
                   pl.BlockSpec((1, D_MODEL), lambda i: (0, 0))],
        out_shape=[jax.ShapeDtypeStruct((8, 128), F32), jax.ShapeDtypeStruct((t, D_MODEL), F32),
                   jax.ShapeDtypeStruct((t, D_MODEL), _MXU_DTYPE), jax.ShapeDtypeStruct((8, D_MODEL), F32),
                   jax.ShapeDtypeStruct((1, D_MODEL), F32)],
        compiler_params=_cp(("arbitrary",)), name="tail_loss")(h1, ff, mods, wf, tgt)


def _pack_w_in(w4):
    ns = w4.shape[2]
    placed = []
    for s0, s1, p0 in IN_SEGMENTS:
        for j in range(N_PLANE):
            lo, hi = max(s0, j * ns), min(s1, (j + 1) * ns)
            if lo < hi:
                placed.append((p0 + lo - s0, w4[j][:, lo - j * ns:hi - j * ns]))
    placed.sort(key=lambda e: e[0])
    pieces, end = [], 0
    for pos, piece in placed:
        assert pos == end, (pos, end)
        pieces.append(piece)
        end = pos + piece.shape[1]
    pieces.append(jnp.zeros((w4.shape[1], P_TOTAL - end), w4.dtype))
    return jnp.concatenate(pieces, axis=1)


def _unpack_w_in(g):
    ns = D_IN_PROJ // N_PLANE
    shards = []
    for j in range(N_PLANE):
        pieces = []
        for s0, s1, p0 in IN_SEGMENTS:
            lo, hi = max(s0, j * ns), min(s1, (j + 1) * ns)
            if lo < hi:
                pieces.append(g[:, p0 + lo - s0:p0 + hi - s0])
        shards.append(jnp.concatenate(pieces, axis=1))
    return jnp.stack(shards)


LATE_WEIGHTS = ['w_br_gdn', 'w_br_ssm', 'w_out', 'w_ffn_in', 'w_ffn_out']
COL_STACKED = ('ada_w', 'w_in', 'w_ffn_in')


def _from_shards(n, g):
    if n in COL_STACKED:
        return g
    if SHARD_AXIS[n] == 0:
        return g.reshape(N_PLANE * g.shape[1], g.shape[2])
    return jnp.concatenate([g[j] for j in range(N_PLANE)], axis=1)


def _to_dest_blocks(n, g):
    if g.ndim == 3:
        return g
    if SHARD_AXIS[n] == 0:
        return g.reshape(N_PLANE, g.shape[0] // N_PLANE, g.shape[1])
    sz = g.shape[1] // N_PLANE
    return jnp.stack([g[:, j * sz:(j + 1) * sz] for j in range(N_PLANE)])


def _local_step(x, c, ctx, tgt, W, late_shards=None):
    distributed = late_shards is not None
    lc, l = ctx.shape[0], x.shape[0]
    t = lc + l
    tl = 256
    assert lc == tl and l % tl == 0 and lc % CHUNK == 0
    nt, nctx_t, nctx = t // tl, lc // tl, lc // CHUNK
    act = _MXU_DTYPE
    r1 = lambda v: v.reshape(1, -1)

    xt = jnp.concatenate([ctx, x], axis=0)
    cc = jnp.concatenate([c, r1(W['c_ctx']), jnp.zeros((6, D_MODEL), F32)], axis=0)
    mods = _ada_fwd(cc, W['ada_w'], r1(W['ada_b']))

    def mod(kk):
        return Op(mods, (8, D_MODEL), lambda j, i, kk=kk: (0, kk))

    def tokop(arr, w=D_MODEL, off=0, tl_=tl):
        if arr.ndim == 3:
            return Op(arr, (2, tl_, w), lambda j, i: (0, i, off + j), load=_sum_dirs)
        return Op(arr, (tl_, w), lambda j, i: (i, off + j))

    def outop(n, dtype, w=D_MODEL, tl_=tl):
        return Op(((t, n), dtype), (tl_, w), lambda j, i: (i, j))

    def parop(arr, w, off=0):
        return Op(arr, (arr.shape[0], w), lambda j, i: (0, off + j))

    def parout(rows, n, w):
        return Op(((rows, n), F32), (rows, w), lambda j, i: (0, j))

    n1w = r1(W['norm1_w'])
    pre_fn = functools.partial(_prenorm_fn, nctx_t)
    pre_tok, pre_par = [tokop(xt)], [parop(n1w, D_MODEL), mod(1), mod(0)]
    (a,) = _tw_fwd("prenorm_fwd", pre_fn, (1, nt), pre_tok + pre_par, [outop(D_MODEL, act)])
    wp = _pack_w_in(W['w_in'])
    proj = _matmul(a, wp, 'nn', "in_proj")

    gcw, gcb = W['gdn_conv_w'], r1(W['gdn_conv_b'])
    scw, scb = W['ssm_conv_w'], r1(W['ssm_conv_b'])
    conv_parts = {}

    def conv_part(name, mode, poff, cw, cb, woff, width):
        fn = functools.partial(_conv_fn, nctx_t, mode)
        tok_ = [tokop(proj, 512, poff // 512)]
        par_ = [parop(cw, 512, woff // 512), parop(cb, 512, woff // 512)]
        conv_parts[name] = (fn, tok_, par_, width)
        (o,) = _tw_fwd("conv_" + name, fn, (width // 512, nt), tok_ + par_, [outop(width, F32, 512)])
        return o

    q = conv_part('q', 'q', P_QKV, gcw, gcb, 0, 1024)
    k = conv_part('k', 'k', P_QKV + 1024, gcw, gcb, 1024, 1024)
    v = conv_part('v', 'none', P_QKV + 2048, gcw, gcb, 2048, 1024)
    xs = conv_part('xs', 'none', P_XBC, scw, scb, 0, 2048)
    bm = conv_part('bm', 'none', P_XBC + 2048, scw, scb, 2048, 512)
    cm = conv_part('cm', 'none', P_XBC + 2560, scw, scb, 2560, 512)

    z16, z32 = jnp.zeros((16,), F32), jnp.zeros((32,), F32)
    p0 = jnp.concatenate([W['gdn_a_log'].reshape(-1), jnp.zeros((112,), F32)]).reshape(1, 128)
    p1 = jnp.concatenate([W['gdn_dt_bias'].reshape(-1), z16, W['ssm_dt_bias'].reshape(-1), z32]).reshape(1, 128)
    alog = jnp.concatenate([z32, W['ssm_a_log'].reshape(-1), z32]).reshape(1, 128)
    act_tok, act_par = [tokop(proj, 128, P_SMALL // 128)], [parop(p0, 128), parop(p1, 128)]
    (sp,) = _tw_fwd("small_act", _act_fn, (1, nt), act_tok + act_par, [outop(128, F32, 128)])

    gather_late = PlaneExchange([late_shards[n] for n in LATE_WEIGHTS], gather=True) if distributed else None
    o2, ss, tri, *late = _scan_fwd("gdn_scan_fwd", _gdn_step, [q, k, v, sp], [], 1024, GDN_HEADS, nctx,
                                   ride=gather_late, n_aux=GDN_HEADS)
    if distributed:
        W = dict(W, **{n: _from_shards(n, g) for n, g in zip(LATE_WEIGHTS, late)})
    y2, hs = _scan_fwd("ssd_scan_fwd", _ssd_step, [xs, bm, cm, sp], [alog], 2048, SSM_PAIRS, nctx)

    tlm = 128
    ntm = t // tlm
    gnw = r1(W['gdn_norm_w'])
    dl = jnp.repeat(W['ssm_d'], 64).reshape(1, 2048)
    snw = r1(W['ssm_norm_w'])
    mg_tok = [tokop(o2, 1024, 0, tlm), tokop(proj, 1024, P_ZG // 1024, tlm)]
    mg_par = [parop(gnw, 128)]
    (og,) = _tw_fwd("mix_gdn", _mixg_fn, (1, ntm), mg_tok + mg_par, [outop(1024, act, 1024, tlm)])
    ms_tok = [tokop(y2, 2048, 0, tlm), tokop(xs, 2048, 0, tlm), tokop(proj, 2048, P_ZS // 2048, tlm)]
    ms_par = [parop(dl, 2048), parop(snw, 2048)]
    (yg,) = _tw_fwd("mix_ssm", _mixs_fn, (1, ntm), ms_tok + ms_par, [outop(2048, act, 2048, tlm)])

    pg = _matmul(og, W['w_br_gdn'], 'nn', "br_gdn")
    ps = _matmul(yg, W['w_br_ssm'], 'nn', "br_ssm")
    mr_tok = [tokop(proj, 1024, P_GATE // 1024), tokop(proj, 1024, P_GATE // 1024 + 1), tokop(pg), tokop(ps)]
    (mrg,) = _tw_fwd("merge", _merge_fn, (1, nt), mr_tok, [outop(1024, act)])
    mo = _matmul(mrg, W['w_out'], 'nn', "w_out")

    n2w = r1(W['norm2_w'])
    n2_tok, n2_par = [tokop(xt), tokop(mo)], [mod(2), parop(n2w, D_MODEL), mod(4), mod(3)]
    h1, f = _tw_fwd("norm2_fwd", _norm2_fn, (1, nt), n2_tok + n2_par, [outop(1024, F32), outop(1024, act)])
    u = _matmul(f, W['w_ffn_in'], 'nn', "ffn_in")
    swb = D_FF // 2
    sw_tok = [tokop(u, swb, 0), tokop(u, swb, D_FF // swb)]
    (sw,) = _tw_fwd("swiglu", _swiglu_fn, (D_FF // swb, nt), sw_tok, [outop(D_FF, act, swb)])
    ff = _matmul(sw, W['w_ffn_out'], 'nn', "ffn_out")

    loss8, dh1, dff, dg2, dnf = _tail(h1, ff, mods, r1(W['norm_f_w']), tgt, nctx_t, tl)
    loss = loss8[0, 0]

    G = {}
    G['norm_f_w'] = dnf.reshape(-1)
    G['w_ffn_out'] = _matmul(sw.T, dff, 'nn', "d_ffn_out")
    dsw = _matmul(dff, W['w_ffn_out'], 'nt', "d_sw")
    dug, duu = _tw_bwd("swiglu_bwd", _swiglu_fn, (D_FF // swb, nt), sw_tok, [], [[tokop(dsw, swb)]],
                       [outop(D_FF, act, swb), outop(D_FF, act, swb)], [])
    du = jnp.concatenate([dug, duu], axis=1)
    G['w_ffn_in'] = _matmul(f.T, du, 'nn', "d_ffn_in", stacked_out=True)
    df = _matmul(du, W['w_ffn_in'], 'nt', "d_f")
    dxt1, dmo, dg1, dn2, dsc2, dsh2 = _tw_bwd(
        "norm2_bwd", _norm2_fn, (1, nt), n2_tok, n2_par, [[tokop(dh1)], [tokop(df)]],
        [outop(1024, F32), outop(1024, act)],
        [parout(8, 1024, 1024), parout(1, 1024, 1024), parout(8, 1024, 1024), parout(8, 1024, 1024)])
    G['norm2_w'] = dn2.reshape(-1)
    G['w_out'] = _matmul(mrg.T, dmo, 'nn', "d_w_out")
    dmrg = _matmul(dmo, W['w_out'], 'nt', "d_mrg")
    dgg, dgs, dpg, dps = _tw_bwd("merge_bwd", _merge_fn, (1, nt), mr_tok, [], [[tokop(dmrg)]],
                                 [outop(1024, act), outop(1024, act), outop(1024, act), outop(1024, act)], [])
    G['w_br_gdn'] = _matmul(og.T, dpg, 'nn', "d_br_gdn")
    G['w_br_ssm'] = _matmul(yg.T, dps, 'nn', "d_br_ssm")
    dog = _matmul(dpg, W['w_br_gdn'], 'nt', "d_og")
    dyg = _matmul(dps, W['w_br_ssm'], 'nt', "d_yg")

    do, dzg, dgnw = _tw_bwd("mix_gdn_bwd", _mixg_fn, (1, ntm), mg_tok, mg_par, [[tokop(dog, 1024, 0, tlm)]],
                            [outop(1024, F32, 1024, tlm), outop(1024, act, 1024, tlm)], [parout(1, 128, 128)])
    G['gdn_norm_w'] = dgnw.reshape(-1)
    dy, dxs_a, dzs, ddl, dsnw = _tw_bwd(
        "mix_ssm_bwd", _mixs_fn, (1, ntm), ms_tok, ms_par, [[tokop(dyg, 2048, 0, tlm)]],
        [outop(2048, F32, 2048, tlm), outop(2048, F32, 2048, tlm), outop(2048, act, 2048, tlm)],
        [parout(1, 2048, 2048), parout(1, 2048, 2048)])
    G['ssm_d'] = ddl.reshape(SSM_HEADS, 64).sum(axis=1)
    G['ssm_norm_w'] = dsnw.reshape(-1)

    dq2, dk2, dv2, dsp_g = _scan_bwd("gdn_scan_bwd", _gdn_step, [q, k, v, sp], [], ss, do, GDN_HEADS, nctx, aux=tri)
    received = {}
    scatter_late = (PlaneExchange([_to_dest_blocks(n, G.pop(n)) for n in LATE_WEIGHTS], gather=False)
                    if distributed else None)
    dxs2, dbm2, dcm2, dsp_s, dalog, *got = _scan_bwd("ssd_scan_bwd", _ssd_step, [xs, bm, cm, sp], [alog], hs, dy,
                                                     SSM_PAIRS, nctx, ride=scatter_late)
    received.update(zip(LATE_WEIGHTS, got))
    G['ssm_a_log'] = dalog[0, 32:96].reshape(2, SSM_HEADS)

    dconv_w, dconv_b, dpre = {}, {}, {}

    def conv_bwd(name, cot_ops):
        fn, tok_, par_, width = conv_parts[name]
        dpre[name], dconv_w[name], dconv_b[name] = _tw_bwd(
            "conv_" + name + "_bwd", fn, (width // 512, nt), tok_, par_, [cot_ops], [outop(width, act, 512)],
            [parout(3, width, 512), parout(1, width, 512)])

    conv_bwd('q', [tokop(dq2, 512)])
    conv_bwd('k', [tokop(dk2, 512)])
    conv_bwd('v', [tokop(dv2, 512)])
    conv_bwd('xs', [tokop(dxs2, 512), tokop(dxs_a, 512)])
    conv_bwd('bm', [tokop(dbm2, 512)])
    conv_bwd('cm', [tokop(dcm2, 512)])
    G['gdn_conv_w'] = jnp.concatenate([dconv_w['q'], dconv_w['k'], dconv_w['v']], axis=1)
    G['gdn_conv_b'] = jnp.concatenate([dconv_b['q'], dconv_b['k'], dconv_b['v']], axis=1).reshape(-1)
    G['ssm_conv_w'] = jnp.concatenate([dconv_w['xs'], dconv_w['bm'], dconv_w['cm']], axis=1)
    G['ssm_conv_b'] = jnp.concatenate([dconv_b['xs'], dconv_b['bm'], dconv_b['cm']], axis=1).reshape(-1)

    dsmall, dp0, dp1 = _tw_bwd("small_act_bwd", _act_fn, (1, nt), act_tok, act_par,
                               [[tokop(dsp_g, 128), tokop(dsp_s, 128)]], [outop(128, act, 128)],
                               [parout(1, 128, 128), parout(1, 128, 128)])
    G['gdn_a_log'] = dp0[0, 0:16].reshape(2, GDN_HEADS)
    G['gdn_dt_bias'] = dp1[0, 0:16].reshape(2, GDN_HEADS)
    G['ssm_dt_bias'] = dp1[0, 32:96].reshape(2, SSM_HEADS)

    zpad = jnp.zeros((t, P_TOTAL - P_SMALL - 128), act)
    dproj = jnp.concatenate([dpre['q'], dpre['k'], dpre['v'], dzg, dzs, dpre['xs'], dpre['bm'], dpre['cm'],
                             dgg, dgs, dsmall, zpad], axis=1)
    G['w_in'] = _unpack_w_in(_matmul(a.T, dproj, 'nn', "d_w_in"))
    if distributed:
        da, (received['w_in'],) = _matmul(dproj, wp, 'nt', "d_a", ride=PlaneExchange([G.pop('w_in')], gather=False))
    else:
        da = _matmul(dproj, wp, 'nt', "d_a")

    gx_out = Op(((l, D_MODEL), F32), (tl, D_MODEL), lambda j, i: (jnp.maximum(i - nctx_t, 0), 0))
    grad_x, dn1, dsc1, dsh1 = _tw_bwd(
        "prenorm_bwd", pre_fn, (1, nt), pre_tok, pre_par, [[tokop(da)]], [gx_out],
        [parout(1, 1024, 1024), parout(8, 1024, 1024), parout(8, 1024, 1024)], tok_add=tokop(dxt1),
        sem=("arbitrary", "arbitrary"))
    G['norm1_w'] = dn1.reshape(-1)
    dmods = jnp.concatenate([dsh1, dsc1, dg1, dsh2, dsc2, dg2], axis=1)
    G['ada_w'], dab, dcc = _ada_bwd(cc, W['ada_w'], dmods)
    G['ada_b'] = dab.reshape(-1)
    G['c_ctx'] = dcc[1]
    return loss, grad_x, G, received


def _row_tile(r, c):
    for tr in (512, 256, 128, 64, 32, 16, 8):
        if r % tr == 0 and tr * c * 4 <= (1 << 20):
            return tr
    return r


def _sum4(name, rv):
    _, r, c = rv.shape
    tr = _row_tile(r, c)

    def body(r_ref, o_ref):
        o_ref[...] = ((r_ref[0] + r_ref[1]) + r_ref[2]) + r_ref[3]

    return pl.pallas_call(
        body, grid=(r // tr,), in_specs=[pl.BlockSpec((N_PLANE, tr, c), lambda i: (0, i, 0))],
        out_specs=pl.BlockSpec((tr, c), lambda i: (i, 0)), out_shape=jax.ShapeDtypeStruct((r, c), rv.dtype),
        compiler_params=_cp(("parallel",)), name=name)(rv)


def _adamw(name, w, m, v, p, q):
    r, c = w.shape
    tr = _row_tile(r, c)

    def body(w_ref, m_ref, v_ref, p_ref, q_ref, g_ref, d_ref, mo_ref, vo_ref):
        g = p_ref[...] + q_ref[...]
        mn = ADAM_B1 * m_ref[...] + (1.0 - ADAM_B1) * g
        vn = ADAM_B2 * v_ref[...] + (1.0 - ADAM_B2) * jnp.square(g)
        m_hat = mn / (1.0 - ADAM_B1 ** ADAM_STEP)
        v_hat = vn / (1.0 - ADAM_B2 ** ADAM_STEP)
        g_ref[...] = g
        d_ref[...] = -ADAM_LR * (m_hat / (jnp.sqrt(v_hat) + ADAM_EPS) + ADAM_WD * w_ref[...])
        mo_ref[...] = mn
        vo_ref[...] = vn

    spec = pl.BlockSpec((tr, c), lambda i: (i, 0))
    return pl.pallas_call(
        body, grid=(r // tr,), in_specs=[spec] * 5, out_specs=[spec] * 4,
        out_shape=[jax.ShapeDtypeStruct((r, c), F32)] * 4, compiler_params=_cp(("parallel",)), name=name)(w, m, v, p, q)


SMALL_ROWS = 24


def _pack_small(d):
    v = jnp.concatenate([d[n].reshape(-1) for n in SMALL])
    return jnp.pad(v, (0, SMALL_ROWS * 1024 - v.shape[0])).reshape(SMALL_ROWS, 1024)


def _unpack_small(buf, like):
    v = buf.reshape(-1)
    out, off = {}, 0
    for n in SMALL:
        sz = like[n].size
        out[n] = v[off:off + sz].reshape(like[n].shape)
        off += sz
    return out


def kernel(x, c, ctx, c_ctx, ada_w, ada_b, norm1_w, w_in, gdn_conv_w, gdn_conv_b, gdn_a_log, gdn_dt_bias, gdn_norm_w, ssm_conv_w, ssm_conv_b, ssm_a_log, ssm_dt_bias, ssm_d, ssm_norm_w, w_br_gdn, w_br_ssm, w_out, norm2_w, w_ffn_in, w_ffn_out, norm_f_w, loss_target, m_c_ctx, m_ada_w, m_ada_b, m_norm1_w, m_w_in, m_gdn_conv_w, m_gdn_conv_b, m_gdn_a_log, m_gdn_dt_bias, m_gdn_norm_w, m_ssm_conv_w, m_ssm_conv_b, m_ssm_a_log, m_ssm_dt_bias, m_ssm_d, m_ssm_norm_w, m_w_br_gdn, m_w_br_ssm, m_w_out, m_norm2_w, m_w_ffn_in, m_w_ffn_out, m_norm_f_w, v_c_ctx, v_ada_w, v_ada_b, v_norm1_w, v_w_in, v_gdn_conv_w, v_gdn_conv_b, v_gdn_a_log, v_gdn_dt_bias, v_gdn_norm_w, v_ssm_conv_w, v_ssm_conv_b, v_ssm_a_log, v_ssm_dt_bias, v_ssm_d, v_ssm_norm_w, v_w_br_gdn, v_w_br_ssm, v_w_out, v_norm2_w, v_w_ffn_in, v_w_ffn_out, v_norm_f_w):
    args = dict(locals())
    wl = {n: args[n] for n in WEIGHTS}
    ml = {n: args['m_' + n] for n in WEIGHTS}
    vl = {n: args['v_' + n] for n in WEIGHTS}

    def nodepth(n, a):
        return a if n in ('c_ctx', 'norm_f_w') else a[0]

    shard = {n: nodepth(n, wl[n]).astype(_MXU_DTYPE if n in MXU_WEIGHTS else F32) for n in SHARDED}
    first = [n for n in SHARDED if n not in LATE_WEIGHTS]
    W = {n: nodepth(n, wl[n]) for n in SMALL}
    for n, g in zip(first, _plane_exchange("all_gather_plane", [shard[n] for n in first], gather=True)):
        W[n] = _from_shards(n, g)

    loss_local, grad_x, G, received = _local_step(x[0], c, ctx[0], loss_target[0], W,
                                                  late_shards={n: shard[n] for n in LATE_WEIGHTS})
    loss = lax.psum(loss_local, ("x", "y", "c"))

    small_g = _pack_small(G)
    last = [n for n in SHARDED if n not in received]
    send = [_to_dest_blocks(n, G[n]) for n in last] + [jnp.broadcast_to(small_g[None], (N_PLANE,) + small_g.shape)]
    received.update(zip(last + ['small'], _plane_exchange("scatter_plane", send, gather=False)))
    names = SHARDED + ['small']
    plane_sum = [_sum4("sum4_" + n, received[n]) for n in names]
    other = _swap_sibling(plane_sum)

    wd = {n: nodepth(n, wl[n]) for n in WEIGHTS}
    md = {n: nodepth(n, ml[n]) for n in WEIGHTS}
    vd = {n: nodepth(n, vl[n]) for n in WEIGHTS}
    res = {}
    for n, p, q in zip(names, plane_sum, other):
        if n == 'small':
            outs = _adamw("adamw_small", _pack_small(wd), _pack_small(md), _pack_small(vd), p, q)
            unpacked = [_unpack_small(o, wd) for o in outs]
            for sn in SMALL:
                res[sn] = [u[sn].reshape(wl[sn].shape) for u in unpacked]
        else:
            outs = _adamw("adamw_" + n, wd[n], md[n], vd[n], p, q)
            res[n] = [o.reshape(wl[n].shape) for o in outs]
    flat = [res[n][kind] for kind in range(4) for n in WEIGHTS]
    return (loss, grad_x[None], *flat)
```

```python
import functools

import jax
import jax.numpy as jnp
from jax import lax
from jax.experimental import pallas as pl
from jax.experimental.pallas import tpu as pltpu

F32 = jnp.float32
HI = lax.Precision.HIGHEST
_MXU_DTYPE = jnp.bfloat16
_SCAN_DTYPE = jnp.bfloat16

D_MODEL = 1024
EPS = 1e-6
CHUNK = 64
GRID_W = 64
GDN_HEADS = 8
GDN_DK = 128
SSM_HEADS = 32
SSM_PAIRS = 16
D_FF = 2816
D_IN_PROJ = 11360
N_PLANE = 4
P_QKV, P_ZG, P_ZS, P_XBC, P_GATE, P_SMALL, P_TOTAL = 0, 3072, 4096, 6144, 9216, 11264, 11776
IN_SEGMENTS = [(0, 3072, P_QKV), (3072, 4096, P_ZG), (4096, 4112, P_SMALL), (4112, 4128, P_SMALL + 16),
               (4128, 6176, P_ZS), (6176, 9248, P_XBC), (9248, 9312, P_SMALL + 32), (9312, 11360, P_GATE)]
ADAM_LR, ADAM_B1, ADAM_B2, ADAM_EPS, ADAM_WD, ADAM_STEP = 0.001, 0.9, 0.999, 1e-08, 0.01, 10

VMEM_LIMIT = 48 * 1024 * 1024

WEIGHTS = ['c_ctx', 'ada_w', 'ada_b', 'norm1_w', 'w_in', 'gdn_conv_w', 'gdn_conv_b', 'gdn_a_log', 'gdn_dt_bias',
           'gdn_norm_w', 'ssm_conv_w', 'ssm_conv_b', 'ssm_a_log', 'ssm_dt_bias', 'ssm_d', 'ssm_norm_w', 'w_br_gdn',
           'w_br_ssm', 'w_out', 'norm2_w', 'w_ffn_in', 'w_ffn_out', 'norm_f_w']
SHARD_AXIS = {'ada_w': 1, 'w_in': 1, 'gdn_conv_w': 1, 'ssm_conv_w': 1, 'w_br_gdn': 0, 'w_br_ssm': 0, 'w_out': 0,
              'w_ffn_in': 1, 'w_ffn_out': 0}
SHARDED = [n for n in WEIGHTS if n in SHARD_AXIS]
SMALL = [n for n in WEIGHTS if n not in SHARD_AXIS]
MXU_WEIGHTS = ['ada_w', 'w_in', 'w_br_gdn', 'w_br_ssm', 'w_out', 'w_ffn_in', 'w_ffn_out']


def _cp(sem):
    return pltpu.CompilerParams(dimension_semantics=sem, vmem_limit_bytes=VMEM_LIMIT)


def _pick(n, cands):
    for c in cands:
        if n % c == 0:
            return c
    raise ValueError(f"no tile for {n}")


HBM = pl.BlockSpec(memory_space=pl.ANY)
MESH = pl.DeviceIdType.MESH


def _plane_peers():
    x, y, c = lax.axis_index("x"), lax.axis_index("y"), lax.axis_index("c")
    return (x, y, c), [(1 - x, y, c), (x, 1 - y, c), (1 - x, 1 - y, c)]


class PlaneExchange:
    def __init__(self, arrs, gather):
        self.arrs, self.gather, self.n = list(arrs), gather, len(arrs)
        self.in_specs = [HBM] * self.n
        self.out_specs = [HBM] * self.n
        self.out_shape = [jax.ShapeDtypeStruct((N_PLANE,) + a.shape if gather else a.shape, a.dtype) for a in self.arrs]
        self.scratch = [pltpu.SemaphoreType.DMA((3 * self.n,)), pltpu.SemaphoreType.DMA((3 * self.n,)),
                        pltpu.SemaphoreType.DMA((self.n,))]

    def _copies(self, ins, outs, sems):
        send_sems, recv_sems, local_sems = sems
        (x, y, c), peers = _plane_peers()
        me = 2 * x + y
        copies = []
        for ti in range(self.n):
            src = ins[ti] if self.gather else ins[ti].at[me]
            copies.append(pltpu.make_async_copy(src, outs[ti].at[me], local_sems.at[ti]))
            for kk, (px, py, pc) in enumerate(peers):
                src = ins[ti] if self.gather else ins[ti].at[2 * px + py]
                copies.append(pltpu.make_async_remote_copy(
                    src_ref=src, dst_ref=outs[ti].at[me], send_sem=send_sems.at[3 * ti + kk],
                    recv_sem=recv_sems.at[3 * ti + kk], device_id=(px, py, pc), device_id_type=MESH))
        return copies

    def start(self, ins, outs, sems):
        for cp in self._copies(ins, outs, sems):
            cp.start()

    def wait(self, ins, outs, sems):
        for cp in self._copies(ins, outs, sems):
            cp.wait()


def _plane_exchange(name, arrs, gather):
    ex = PlaneExchange(arrs, gather)
    n = ex.n

    def body(*refs):
        ins, outs, sems = refs[:n], refs[n:2 * n], refs[2 * n:]
        ex.start(ins, outs, sems)
        ex.wait(ins, outs, sems)

    return pl.pallas_call(body, in_specs=ex.in_specs, out_specs=ex.out_specs, out_shape=ex.out_shape,
                          scratch_shapes=ex.scratch, name=name)(*arrs)


def _swap_sibling(arrs):
    n = len(arrs)

    def body(*refs):
        ins, outs, send_sems, recv_sems = refs[:n], refs[n:2 * n], refs[2 * n], refs[2 * n + 1]
        x, y, c = lax.axis_index("x"), lax.axis_index("y"), lax.axis_index("c")
        copies = [pltpu.make_async_remote_copy(src_ref=ins[ti], dst_ref=outs[ti], send_sem=send_sems.at[ti],
                                               recv_sem=recv_sems.at[ti], device_id=(x, y, 1 - c), device_id_type=MESH)
                  for ti in range(n)]
        for cp in copies:
            cp.start()
        for cp in copies:
            cp.wait()

    return pl.pallas_call(
        body, in_specs=[HBM] * n, out_specs=[HBM] * n, out_shape=[jax.ShapeDtypeStruct(a.shape, a.dtype) for a in arrs],
        scratch_shapes=[pltpu.SemaphoreType.DMA((n,)), pltpu.SemaphoreType.DMA((n,))], name="swap_sibling")(*arrs)


MATMUL_VMEM_BUDGET = 36 * 1024 * 1024
TILE_CANDIDATES = (2944, 2816, 1408, 1024, 768, 512, 256, 128)
STEP_COST_BYTES = 1 << 20


def _matmul_tiles(m, n, k, ab, bb, ob, tn_fixed=None, tk_fixed=None):
    def cands(dim, whole_up_to):
        out = [c for c in TILE_CANDIDATES if dim % c == 0]
        if dim <= whole_up_to and dim not in out:
            out.append(dim)
        return out

    best = None
    for tm in cands(m, 3072):
        for tn in ([tn_fixed] if tn_fixed else cands(n, 3072)):
            for tk in ([tk_fixed] if tk_fixed else cands(k, 2048)):
                gi, gj, gk = m // tm, n // tn, k // tk
                vmem = 2 * (tm * tk * ab + tk * tn * bb + tm * tn * ob) + (tm * tn * 4 if gk > 1 else 0)
                if vmem > MATMUL_VMEM_BUDGET:
                    continue
                a_reads = 1 if gk == 1 else gj
                b_reads = 1 if (gk == 1 and gj == 1) else gi
                cost = (m * k * ab * a_reads + k * n * bb * b_reads + m * n * ob + gi * gj * gk * STEP_COST_BYTES)
                if best is None or cost < best[0]:
                    best = (cost, tm, tn, tk)
    assert best is not None, (m, n, k)
    return best[1:]


def _matmul(a, b, form, name, out_dtype=F32, stacked_out=False, ride=None):
    dims = {'nn': (((1,), (0,)), ((), ())), 'nt': (((1,), (1,)), ((), ())), 'tn': (((0,), (0,)), ((), ()))}[form]
    stacked_b = b.ndim == 3
    ns = b.shape[2] if stacked_b else None
    if form == 'nn':
        m, k = a.shape
        n = b.shape[0] * ns if stacked_b else b.shape[1]
    elif form == 'nt':
        m, k = a.shape
        n = b.shape[1] if stacked_b else b.shape[0]
    else:
        k, m = a.shape
        n = b.shape[1]
    ob = jnp.dtype(out_dtype).itemsize
    tm, tn, tk = _matmul_tiles(m, n, k, a.dtype.itemsize, b.dtype.itemsize, ob,
                               tn_fixed=(n // N_PLANE if stacked_out else ns if (stacked_b and form == 'nn') else None),
                               tk_fixed=(ns if (stacked_b and form == 'nt') else None))
    nk = k // tk
    grid = (m // tm, n // tn, nk)
    n_ride = ride.n if ride is not None else 0

    def body(*refs):
        a_ref, b_ref = refs[0], refs[1]
        ride_in = refs[2:2 + n_ride]
        o_ref = refs[2 + n_ride]
        ride_out = refs[3 + n_ride:3 + 2 * n_ride]
        acc_ref = refs[3 + 2 * n_ride]
        sems = refs[4 + 2 * n_ride:]
        i, j, kk = pl.program_id(0), pl.program_id(1), pl.program_id(2)
        if ride is not None:
            @pl.when((i == 0) & (j == 0) & (kk == 0))
            def _():
                ride.start(ride_in, ride_out, sems)

        def put(val):
            if stacked_out:
                o_ref[0] = val.astype(o_ref.dtype)
            else:
                o_ref[...] = val.astype(o_ref.dtype)

        bv = b_ref[0] if stacked_b else b_ref[...]
        part = lax.dot_general(a_ref[...].astype(_MXU_DTYPE), bv.astype(_MXU_DTYPE), dims, preferred_element_type=F32)
        if nk == 1:
            put(part)
        else:
            @pl.when(kk == 0)
            def _():
                acc_ref[...] = part

            @pl.when(kk > 0)
            def _():
                acc_ref[...] += part

            @pl.when(kk == nk - 1)
            def _():
                put(acc_ref[...])

        if ride is not None:
            @pl.when((i == grid[0] - 1) & (j == grid[1] - 1) & (kk == nk - 1))
            def _():
                ride.wait(ride_in, ride_out, sems)

    if form == 'nn':
        a_spec = pl.BlockSpec((tm, tk), lambda i, j, kk: (i, kk))
        b_spec = (pl.BlockSpec((1, tk, tn), lambda i, j, kk: (j, kk, 0)) if stacked_b
                  else pl.BlockSpec((tk, tn), lambda i, j, kk: (kk, j)))
    elif form == 'nt':
        a_spec = pl.BlockSpec((tm, tk), lambda i, j, kk: (i, kk))
        b_spec = (pl.BlockSpec((1, tn, tk), lambda i, j, kk: (kk, j, 0)) if stacked_b
                  else pl.BlockSpec((tn, tk), lambda i, j, kk: (j, kk)))
    else:
        a_spec = pl.BlockSpec((tk, tm), lambda i, j, kk: (kk, i))
        b_spec = pl.BlockSpec((tk, tn), lambda i, j, kk: (kk, j))
    if stacked_out:
        o_spec = pl.BlockSpec((1, tm, tn), lambda i, j, kk: (j, i, 0))
        o_shape = jax.ShapeDtypeStruct((N_PLANE, m, tn), out_dtype)
    else:
        o_spec = pl.BlockSpec((tm, tn), lambda i, j, kk: (i, j))
        o_shape = jax.ShapeDtypeStruct((m, n), out_dtype)
    acc_shape = (tm, tn) if nk > 1 else (8, 128)
    if ride is None:
        return pl.pallas_call(
            body, grid=grid, in_specs=[a_spec, b_spec], out_specs=o_spec, out_shape=o_shape,
            scratch_shapes=[pltpu.VMEM(acc_shape, F32)],
            compiler_params=_cp(("parallel", "parallel", "arbitrary")), name=name)(a, b)
    res = pl.pallas_call(
        body, grid=grid, in_specs=[a_spec, b_spec] + ride.in_specs, out_specs=[o_spec] + ride.out_specs,
        out_shape=[o_shape] + ride.out_shape, scratch_shapes=[pltpu.VMEM(acc_shape, F32)] + ride.scratch,
        compiler_params=_cp(("arbitrary", "arbitrary", "arbitrary")), name=name)(a, b, *ride.arrs)
    return res[0], res[1:]


class Op:
    def __init__(self, arr, bs, im, load=None):
        self.arr, self.bs, self.im = arr, bs, im
        self.load = load or (lambda r: r[...].astype(F32))

    def spec(self):
        return pl.BlockSpec(self.bs, self.im)


def _sum_dirs(r):
    return r[0].astype(F32) + r[1].astype(F32)


def _tw_fwd(name, fn, grid, ins, outs):
    n_in = len(ins)

    def body(*refs):
        info = (pl.program_id(0), pl.program_id(1))
        vals = [op.load(r) for op, r in zip(ins, refs[:n_in])]
        res = fn(info, *vals)
        for r, v in zip(refs[n_in:], res):
            r[...] = v.astype(r.dtype)

    return pl.pallas_call(
        body, grid=grid, in_specs=[op.spec() for op in ins], out_specs=[op.spec() for op in outs],
        out_shape=[jax.ShapeDtypeStruct(*op.arr) for op in outs],
        compiler_params=_cp(("parallel", "arbitrary")), name=name)(*[op.arr for op in ins])


def _tw_bwd(name, fn, grid, tok, par, cots, tok_out, par_out, tok_add=None, sem=("parallel", "arbitrary")):
    n_tok = len(tok)
    flat_cots = [op for group in cots for op in group]
    extra = [tok_add] if tok_add is not None else []
    out_ops = list(tok_out) + list(par_out)

    def body(*refs):
        info = (pl.program_id(0), pl.program_id(1))
        it = iter(refs)
        tok_v = [op.load(next(it)) for op in tok]
        par_v = [op.load(next(it)) for op in par]
        cot_v = []
        for group in cots:
            vals = [op.load(next(it)) for op in group]
            cot_v.append(functools.reduce(lambda u, w: u + w, vals))
        add_v = [op.load(next(it)) for op in extra]
        _, pull = jax.vjp(lambda *a: fn(info, *a), *tok_v, *par_v)
        grads = pull(tuple(cot_v))
        for i in range(n_tok):
            r = next(it)
            g = grads[i] + add_v[0] if (i == 0 and add_v) else grads[i]
            r[...] = g.astype(r.dtype)
        first = pl.program_id(1) == 0
        for i in range(len(par)):
            r = next(it)
            g = grads[n_tok + i]

            @pl.when(first)
            def _(r=r, g=g):
                r[...] = g

            @pl.when(jnp.logical_not(first))
            def _(r=r, g=g):
                r[...] += g

    ops = tok + par + flat_cots + extra
    return pl.pallas_call(
        body, grid=grid, in_specs=[op.spec() for op in ops], out_specs=[op.spec() for op in out_ops],
        out_shape=[jax.ShapeDtypeStruct(*op.arr) for op in out_ops],
        compiler_params=_cp(sem), name=name)(*[op.arr for op in ops])


def _silu(x):
    return x * jax.nn.sigmoid(x)


def _rms(x):
    return x * lax.rsqrt(jnp.mean(x * x, axis=-1, keepdims=True) + EPS)


@functools.partial(jax.custom_vjp, nondiff_argnums=(1,))
def _shift_rows(x, k):
    return pltpu.roll(x, k % x.shape[0], 0)


def _shift_rows_fwd(x, k):
    return _shift_rows(x, k), None


def _shift_rows_bwd(k, _, g):
    return (_shift_rows(g, -k),)


_shift_rows.defvjp(_shift_rows_fwd, _shift_rows_bwd)


def _prenorm_fn(nctx_t, info, x, w, sc8, sh8):
    is_ctx = info[1] < nctx_t
    sc = jnp.where(is_ctx, sc8[1:2], sc8[0:1])
    sh = jnp.where(is_ctx, sh8[1:2], sh8[0:1])
    return (_rms(x) * w * (1.0 + sc) + sh,)


def _conv_fn(nctx_t, mode, info, x, w, b):
    n, c = x.shape
    is_ctx = info[1] < nctx_t
    idx = lax.broadcasted_iota(jnp.int32, (n, 1), 0)
    rr = jnp.where(is_ctx, idx, idx % GRID_W)
    first = rr == 0
    last = rr == jnp.where(is_ctx, n - 1, GRID_W - 1)
    prev = jnp.where(first, 0.0, _shift_rows(x, 1))
    nxt = jnp.where(last, 0.0, _shift_rows(x, -1))
    y = b + prev * w[0:1] + x * w[1:2] + nxt * w[2:3]
    y = _silu(y)
    if mode == 'none':
        return (y,)
    scale = GDN_DK ** -0.5 if mode == 'q' else 1.0
    outs = []
    for h in range(c // 128):
        yh = y[:, h * 128:(h + 1) * 128]
        outs.append(yh * lax.rsqrt(jnp.sum(yh * yh, axis=-1, keepdims=True) + EPS) * scale)
    return (jnp.concatenate(outs, axis=1),)


def _act_fn(info, x, p0, p1):
    lane = lax.broadcasted_iota(jnp.int32, x.shape, 1)
    sp = jax.nn.softplus(x + p1)
    g = -jnp.exp(p0) * sp
    bt = jax.nn.sigmoid(x)
    return (jnp.where(lane < 16, g, jnp.where(lane < 32, bt, jnp.where(lane < 96, sp, 0.0))),)


def _mixg_fn(info, o, zg, gnw):
    outs = []
    for h in range(GDN_HEADS):
        outs.append(_rms(o[:, h * 128:(h + 1) * 128]) * gnw)
    return (jnp.concatenate(outs, axis=1) * _silu(zg),)


def _mixs_fn(info, y, xs, zs, dl, snw):
    yy = (y + dl * xs) * _silu(zs)
    outs = []
    for g in range(4):
        outs.append(_rms(yy[:, g * 512:(g + 1) * 512]))
    return (jnp.concatenate(outs, axis=1) * snw,)


def _merge_fn(info, gg, gs, pg, ps):
    return (jax.nn.sigmoid(gg) * pg + jax.nn.sigmoid(gs) * ps,)


def _norm2_fn(info, xt, mo, g8, w, sc8, sh8):
    h1 = xt + g8[0:1] * mo
    return (h1, _rms(h1) * w * (1.0 + sc8[0:1]) + sh8[0:1])


def _swiglu_fn(info, ug, uu):
    return (_silu(ug) * uu,)


_NN = (((1,), (0,)), ((), ()))
_NT = (((1,), (1,)), ((), ()))
_TN = (((0,), (0,)), ((), ()))


def _mmh(a, b):
    return lax.dot_general(a, b, _NN, precision=HI, preferred_element_type=F32)


def _dot1(a, b, dims):
    return lax.dot_general(a.astype(_SCAN_DTYPE), b.astype(_SCAN_DTYPE), dims, preferred_element_type=F32)


def _mm(a, b):
    return _dot1(a, b, _NN)


def _mm_nt(a, b):
    return _dot1(a, b, _NT)


def _mm_tn(a, b):
    return _dot1(a, b, _TN)


def _split2(a):
    hi = a.astype(_SCAN_DTYPE)
    return hi, (a - hi.astype(F32)).astype(_SCAN_DTYPE)


def _dot3(a, b, dims):
    ah, al = _split2(a)
    bh, bl = _split2(b)
    d = lambda u, w: lax.dot_general(u, w, dims, preferred_element_type=F32)
    return d(ah, bh) + (d(ah, bl) + d(al, bh))


def _order_masks(d):
    i = lax.broadcasted_iota(jnp.int32, (CHUNK, CHUNK), 0)
    j = lax.broadcasted_iota(jnp.int32, (CHUNK, CHUNK), 1)
    s = jnp.where(d == 0, 1, -1) * (i - j)
    return (s >= 0).astype(F32), (s > 0).astype(F32)


@jax.custom_vjp
def _unit_tri_inv(mats):
    i = lax.broadcasted_iota(jnp.int32, (CHUNK, CHUNK), 0)
    j = lax.broadcasted_iota(jnp.int32, (CHUNK, CHUNK), 1)
    eye = (i == j).astype(F32)
    ps = [-a for a in mats]
    ts = [eye + p for p in ps]
    for _ in range(5):
        ps = [_dot3(p, p, _NN) for p in ps]
        ts = [t + _dot3(t, p, _NN) for t, p in zip(ts, ps)]
    return tuple(ts)


def _uti_fwd(mats):
    ts = _unit_tri_inv(mats)
    return ts, ts


def _uti_bwd(ts, gs):
    inner = [_dot3(g, t, _NT) for g, t in zip(gs, ts)]
    return (tuple(-_dot3(t, u, _TN) for t, u in zip(ts, inner)),)


_unit_tri_inv.defvjp(_uti_fwd, _uti_bwd)


@jax.custom_vjp
def _unit_tri_inv_given(mats, ts):
    return ts


def _utig_fwd(mats, ts):
    return ts, ts


def _utig_bwd(ts, gs):
    return _uti_bwd(ts, gs)[0], tuple(jnp.zeros_like(t) for t in ts)


_unit_tri_inv_given.defvjp(_utig_fwd, _utig_bwd)


def _lane_col(blk, lane_idx):
    lane = lax.broadcasted_iota(jnp.int32, blk.shape, 1)
    return jnp.sum(jnp.where(lane == lane_idx, blk, 0.0), axis=1, keepdims=True)


def _decay_mat(cum, incl):
    cb = jnp.broadcast_to(cum, (CHUNK, CHUNK))
    return jnp.exp(jnp.minimum(cb - cb.T, 0.0)) * incl


def _gdn_step(d, q, k, v, spb, *states, aux=None, want_aux=False):
    incl, strict = _order_masks(d)
    cum = _mmh(incl, spb)
    tot = jnp.sum(spb, axis=0, keepdims=True)
    hs = range(GDN_HEADS)
    cat = jnp.concatenate
    qs = [q[:, h * 128:(h + 1) * 128] for h in hs]
    ks = [k[:, h * 128:(h + 1) * 128] for h in hs]
    vs = [v[:, h * 128:(h + 1) * 128] for h in hs]
    gcum = [_lane_col(cum, d * GDN_HEADS + h) for h in hs]
    glast = [_lane_col(tot, d * GDN_HEADS + h) for h in hs]
    beta = [_lane_col(spb, 16 + d * GDN_HEADS + h) for h in hs]
    decay = [_decay_mat(gcum[h], incl) for h in hs]
    egc = [jnp.exp(gcum[h]) for h in hs]
    kb = [ks[h] * beta[h] for h in hs]
    kq = [_mm_nt(cat([kb[h], qs[h]], axis=0), ks[h]) for h in hs]
    mats = tuple(kq[h][:CHUNK] * decay[h] * strict for h in hs)
    ts = _unit_tri_inv(mats) if aux is None else _unit_tri_inv_given(mats, tuple(aux))
    wu = [_mm(ts[h], cat([kb[h] * egc[h], vs[h] * beta[h]], axis=1)) for h in hs]
    ws = [_mm(cat([wu[h][:, :128], qs[h] * egc[h]], axis=0), states[h]) for h in hs]
    vn = [wu[h][:, 128:] - ws[h][:CHUNK] for h in hs]
    outs = [ws[h][CHUNK:] + _mm(kq[h][CHUNK:] * decay[h], vn[h]) for h in hs]
    new_states = [states[h] * jnp.exp(glast[h]) + _mm_tn(ks[h] * jnp.exp(glast[h] - gcum[h]), vn[h]) for h in hs]
    return (cat(outs, axis=1), *new_states, *(ts if want_aux else ()))


def _split3(a):
    a1 = a.astype(_SCAN_DTYPE)
    r = a - a1.astype(F32)
    a2 = r.astype(_SCAN_DTYPE)
    return a1, a2, (r - a2.astype(F32)).astype(_SCAN_DTYPE)


def _exact_dot(a, e, dims, split_lhs):
    parts = _split3(a if split_lhs else e)
    d = (lambda u: lax.dot_general(u, e, dims, preferred_element_type=F32)) if split_lhs else \
        (lambda u: lax.dot_general(a, u, dims, preferred_element_type=F32))
    return d(parts[0]) + (d(parts[1]) + d(parts[2]))


@jax.custom_vjp
def _spread(a, e):
    return _exact_dot(a, e, _NN, True)


def _spread_fwd(a, e):
    return _spread(a, e), e


def _spread_bwd(e, g):
    return _exact_dot(g, e, _NT, True), jnp.zeros_like(e)


_spread.defvjp(_spread_fwd, _spread_bwd)


@jax.custom_vjp
def _colsum_bcast(z):
    return _exact_dot(jnp.ones((z.shape[0], z.shape[0]), _SCAN_DTYPE), z, _NN, False)


def _colsum_fwd(z):
    return _colsum_bcast(z), None


def _colsum_bwd(_, g):
    return (_colsum_bcast(g),)


_colsum_bcast.defvjp(_colsum_fwd, _colsum_bwd)


def _ssd_step(d, x, bm, cm, spb, alog, *states):
    wdt = x.shape[1]
    incl, _ = _order_masks(d)
    lane1 = lax.broadcasted_iota(jnp.int32, (1, 128), 1)
    lo_lane = 32 + d * SSM_HEADS
    a_vec = jnp.where(lane1 >= lo_lane, jnp.where(lane1 < lo_lane + SSM_HEADS, -jnp.exp(alog), 0.0), 0.0)
    adt = spb * a_vec
    acum = _mmh(incl, adt)
    alast = jnp.sum(adt, axis=0, keepdims=True)
    e = (lax.broadcasted_iota(jnp.int32, (128, wdt), 0)
         == lo_lane + lax.broadcasted_iota(jnp.int32, (128, wdt), 1) // 64).astype(_SCAN_DTYPE)
    dt2 = _spread(spb, e)
    ac2 = _spread(acum, e)
    al2 = _spread(jnp.broadcast_to(alast, (8, 128)), e)[0:1]
    ci = lax.broadcasted_iota(jnp.int32, (CHUNK, wdt), 0)
    pos = lax.broadcasted_iota(jnp.int32, (CHUNK, wdt), 1) % 64
    row = _colsum_bcast(jnp.where(ci == pos, ac2, 0.0))
    incl_t = (jnp.where(d == 0, 1, -1) * (ci - pos) >= 0).astype(F32)
    seg = jnp.exp(jnp.minimum(ac2 - row, 0.0)) * incl_t
    xdt = x * dt2
    gam = jnp.exp(ac2)
    xe = xdt * jnp.exp(al2 - ac2)
    low = lax.broadcasted_iota(jnp.int32, (CHUNK, 128), 1) < 64
    row_low = lax.broadcasted_iota(jnp.int32, (128, 1), 0) < 64
    ps = range(SSM_PAIRS)
    sl = [slice(p * 128, (p + 1) * 128) for p in ps]
    bg = [bm[:, g * 128:(g + 1) * 128] for g in range(4)]
    cg = [cm[:, g * 128:(g + 1) * 128] for g in range(4)]
    cb2 = [_mm_nt(cg[g], jnp.concatenate([bg[g], bg[g]], axis=0)) for g in range(4)]
    xd = [jnp.concatenate([jnp.where(low, xdt[:, sl[p]], 0.0), jnp.where(low, 0.0, xdt[:, sl[p]])], axis=0) for p in ps]
    yd = [_mm(cb2[p // 4] * seg[:, sl[p]], xd[p]) for p in ps]
    yo = [_mm_nt(cg[p // 4], states[p]) for p in ps]
    ys = [yd[p] + gam[:, sl[p]] * yo[p] for p in ps]
    new = [_mm_tn(xe[:, sl[p]], bg[p // 4]) for p in ps]
    al0 = [_lane_col(alast, lo_lane + 2 * p) for p in ps]
    al1 = [_lane_col(alast, lo_lane + 2 * p + 1) for p in ps]
    new_states = [states[p] * jnp.exp(jnp.where(row_low, al0[p], al1[p])) + new[p] for p in ps]
    return (jnp.concatenate(ys, axis=1), *new_states)


def _chunk_of(d, p, nctx, nc):
    return jnp.where(d == 0, p, jnp.where(p < nctx, nctx - 1 - p, nctx + nc - 1 - p))


class _NoRide:
    n, arrs, in_specs, out_specs, out_shape, scratch = 0, [], [], [], [], []


def _scan_fwd(name, step, toks, pars, out_width, n_state, nctx, ride=None, n_aux=0):
    t = toks[0].shape[0]
    nc = t // CHUNK
    n_tok, n_par = len(toks), len(pars)
    rd = ride if ride is not None else _NoRide
    kw = dict(want_aux=True) if n_aux else {}

    def body(*refs):
        it = iter(refs)
        tok_refs = [next(it) for _ in range(n_tok)]
        par_refs = [next(it) for _ in range(n_par)]
        ride_in = [next(it) for _ in range(rd.n)]
        o_ref, ss_ref = next(it), next(it)
        aux_ref = next(it) if n_aux else None
        ride_out = [next(it) for _ in range(rd.n)]
        s_scr = next(it)
        sems = list(it)
        d, p = pl.program_id(0), pl.program_id(1)
        if ride is not None:
            @pl.when((d == 0) & (p == 0))
            def _():
                ride.start(ride_in, ride_out, sems)

        @pl.when(p == 0)
        def _():
            s_scr[...] = jnp.zeros(s_scr.shape, F32)

        ss_ref[0, 0] = s_scr[...]
        res = step(d, *[r[...] for r in tok_refs], *[r[...] for r in par_refs], *[s_scr[h] for h in range(n_state)],
                   **kw)
        o_ref[0] = res[0]
        for h in range(n_state):
            s_scr[h] = res[1 + h]
        for i in range(n_aux):
            aux_ref[0, 0, i] = res[1 + n_state + i]
        if ride is not None:
            @pl.when((d == 1) & (p == nc - 1))
            def _():
                ride.wait(ride_in, ride_out, sems)

    ch = lambda d, p: _chunk_of(d, p, nctx, nc)
    in_specs = [pl.BlockSpec((CHUNK, a.shape[1]), lambda d, p: (ch(d, p), 0)) for a in toks]
    in_specs += [pl.BlockSpec(a.shape, lambda d, p: (0, 0)) for a in pars]
    aux_specs = [pl.BlockSpec((1, 1, n_aux, CHUNK, CHUNK), lambda d, p: (d, p, 0, 0, 0))] if n_aux else []
    aux_shape = [jax.ShapeDtypeStruct((2, nc, n_aux, CHUNK, CHUNK), F32)] if n_aux else []
    return pl.pallas_call(
        body, grid=(2, nc), in_specs=in_specs + rd.in_specs,
        out_specs=[pl.BlockSpec((1, CHUNK, out_width), lambda d, p: (d, ch(d, p), 0)),
                   pl.BlockSpec((1, 1, n_state, 128, 128), lambda d, p: (d, p, 0, 0, 0))] + aux_specs + rd.out_specs,
        out_shape=[jax.ShapeDtypeStruct((2, t, out_width), F32),
                   jax.ShapeDtypeStruct((2, nc, n_state, 128, 128), F32)] + aux_shape + rd.out_shape,
        scratch_shapes=[pltpu.VMEM((n_state, 128, 128), F32)] + rd.scratch,
        compiler_params=_cp(("arbitrary", "arbitrary")), name=name)(*toks, *pars, *rd.arrs)


def _scan_bwd(name, step, toks, pars, ss, dout, n_state, nctx, ride=None, aux=None):
    t = toks[0].shape[0]
    nc = t // CHUNK
    n_tok, n_par = len(toks), len(pars)
    rd = ride if ride is not None else _NoRide
    n_aux = aux.shape[2] if aux is not None else 0

    def body(*refs):
        it = iter(refs)
        tok_refs = [next(it) for _ in range(n_tok)]
        par_refs = [next(it) for _ in range(n_par)]
        ss_ref, do_ref = next(it), next(it)
        aux_ref = next(it) if n_aux else None
        ride_in = [next(it) for _ in range(rd.n)]
        dtok_refs = [next(it) for _ in range(n_tok)]
        dpar_refs = [next(it) for _ in range(n_par)]
        ride_out = [next(it) for _ in range(rd.n)]
        ds_scr = next(it)
        sems = list(it)
        d, pr = pl.program_id(0), pl.program_id(1)
        if ride is not None:
            @pl.when((d == 0) & (pr == 0))
            def _():
                ride.start(ride_in, ride_out, sems)

            @pl.when((d == 1) & (pr == nc - 1))
            def _():
                ride.wait(ride_in, ride_out, sems)

        @pl.when(pr == 0)
        def _():
            ds_scr[...] = jnp.zeros(ds_scr.shape, F32)

        kw = dict(aux=[aux_ref[0, 0, i] for i in range(n_aux)]) if n_aux else {}
        _, pull = jax.vjp(functools.partial(step, d, **kw), *[r[...] for r in tok_refs], *[r[...] for r in par_refs],
                          *[ss_ref[0, 0, h] for h in range(n_state)])
        grads = pull((do_ref[...], *[ds_scr[h] for h in range(n_state)]))
        for r, g in zip(dtok_refs, grads[:n_tok]):
            r[0] = g
        for h in range(n_state):
            ds_scr[h] = grads[n_tok + n_par + h]
        first = (d == 0) & (pr == 0)
        for r, g in zip(dpar_refs, grads[n_tok:n_tok + n_par]):
            @pl.when(first)
            def _(r=r, g=g):
                r[...] = g

            @pl.when(jnp.logical_not(first))
            def _(r=r, g=g):
                r[...] += g

    ch = lambda d, pr: _chunk_of(d, nc - 1 - pr, nctx, nc)
    in_specs = [pl.BlockSpec((CHUNK, a.shape[1]), lambda d, pr: (ch(d, pr), 0)) for a in toks]
    in_specs += [pl.BlockSpec(a.shape, lambda d, pr: (0, 0)) for a in pars]
    in_specs += [pl.BlockSpec((1, 1, n_state, 128, 128), lambda d, pr: (d, nc - 1 - pr, 0, 0, 0)),
                 pl.BlockSpec((CHUNK, dout.shape[1]), lambda d, pr: (ch(d, pr), 0))]
    if n_aux:
        in_specs += [pl.BlockSpec((1, 1, n_aux, CHUNK, CHUNK), lambda d, pr: (d, nc - 1 - pr, 0, 0, 0))]
    out_specs = [pl.BlockSpec((1, CHUNK, a.shape[1]), lambda d, pr: (d, ch(d, pr), 0)) for a in toks]
    out_specs += [pl.BlockSpec(a.shape, lambda d, pr: (0, 0)) for a in pars]
    out_shape = [jax.ShapeDtypeStruct((2, t, a.shape[1]), F32) for a in toks]
    out_shape += [jax.ShapeDtypeStruct(a.shape, F32) for a in pars]
    return pl.pallas_call(
        body, grid=(2, nc), in_specs=in_specs + rd.in_specs, out_specs=out_specs + rd.out_specs,
        out_shape=out_shape + rd.out_shape, scratch_shapes=[pltpu.VMEM((n_state, 128, 128), F32)] + rd.scratch,
        compiler_params=_cp(("arbitrary", "arbitrary")), name=name)(
            *toks, *pars, ss, dout, *([aux] if n_aux else []), *rd.arrs)


ADA_TN = 512


def _ada_fwd(cc, ada_w4, ada_b):
    per = ada_w4.shape[2] // ADA_TN
    n = N_PLANE * ada_w4.shape[2]

    def body(c_ref, w_ref, b_ref, o_ref):
        s = _silu(c_ref[...]).astype(_MXU_DTYPE)
        o_ref[...] = jnp.dot(s, w_ref[0].astype(_MXU_DTYPE), preferred_element_type=F32) + b_ref[...]

    return pl.pallas_call(
        body, grid=(n // ADA_TN,),
        in_specs=[pl.BlockSpec((8, D_MODEL), lambda j: (0, 0)),
                  pl.BlockSpec((1, D_MODEL, ADA_TN), lambda j: (j // per, 0, j % per)),
                  pl.BlockSpec((1, ADA_TN), lambda j: (0, j))],
        out_specs=pl.BlockSpec((8, ADA_TN), lambda j: (0, j)), out_shape=jax.ShapeDtypeStruct((8, n), F32),
        compiler_params=_cp(("parallel",)), name="ada_fwd")(cc, ada_w4, ada_b)


def _ada_bwd(cc, ada_w4, dmods):
    per = ada_w4.shape[2] // ADA_TN
    n = N_PLANE * ada_w4.shape[2]
    nj = n // ADA_TN

    def body(c_ref, w_ref, g_ref, dw_ref, db_ref, dc_ref):
        j = pl.program_id(0)
        g = g_ref[...]
        row = lax.broadcasted_iota(jnp.int32, g.shape, 0)
        g = jnp.where(row < 2, g, 0.0)
        s, pull = jax.vjp(_silu, c_ref[...])
        dw_ref[0] = lax.dot_general(s.astype(_MXU_DTYPE), g.astype(_MXU_DTYPE), _TN, preferred_element_type=F32)
        db_ref[...] = jnp.sum(g, axis=0, keepdims=True)
        ds = lax.dot_general(g.astype(_MXU_DTYPE), w_ref[0].astype(_MXU_DTYPE), _NT, preferred_element_type=F32)

        @pl.when(j == 0)
        def _():
            dc_ref[...] = ds

        @pl.when(j > 0)
        def _():
            dc_ref[...] += ds

        @pl.when(j == nj - 1)
        def _():
            dc_ref[...] = pull(dc_ref[...])[0]

    wspec = pl.BlockSpec((1, D_MODEL, ADA_TN), lambda j: (j // per, 0, j % per))
    return pl.pallas_call(
        body, grid=(nj,),
        in_specs=[pl.BlockSpec((8, D_MODEL), lambda j: (0, 0)), wspec, pl.BlockSpec((8, ADA_TN), lambda j: (0, j))],
        out_specs=[wspec, pl.BlockSpec((1, ADA_TN), lambda j: (0, j)), pl.BlockSpec((8, D_MODEL), lambda j: (0, 0))],
        out_shape=[jax.ShapeDtypeStruct(ada_w4.shape, F32), jax.ShapeDtypeStruct((1, n), F32),
                   jax.ShapeDtypeStruct((8, D_MODEL), F32)],
        compiler_params=_cp(("arbitrary",)), name="ada_bwd")(cc, ada_w4, dmods)


def _tail(h1, ff, mods, wf, tgt, nctx_t, tl):
    t = h1.shape[0]
    nt = t // tl

    def loss_fn(valid, h1v, ffv, g8, w, tg):
        h2 = h1v + g8[0:1] * ffv
        y = _rms(h2) * w
        err = (y - tg) ** 2
        return 0.5 * jnp.sum(jnp.mean(err, axis=-1, keepdims=True), axis=0, keepdims=True) * valid

    def body(h1_ref, ff_ref, g_ref, w_ref, t_ref, loss_ref, dh_ref, dff_ref, dg_ref, dw_ref):
        i = pl.program_id(0)
        valid = jnp.where(i < nctx_t, 0.0, 1.0)
        lv, pull = jax.vjp(functools.partial(loss_fn, valid), h1_ref[...], ff_ref[...].astype(F32), g_ref[...],
                           w_ref[...], t_ref[...])
        dh, dff, dg, dw, _ = pull(jnp.ones((1, 1), F32))
        dh_ref[...] = dh
        dff_ref[...] = dff.astype(dff_ref.dtype)
        lb = jnp.broadcast_to(lv, loss_ref.shape)

        @pl.when(i == 0)
        def _():
            loss_ref[...] = lb
            dg_ref[...] = dg
            dw_ref[...] = dw

        @pl.when(i > 0)
        def _():
            loss_ref[...] += lb
            dg_ref[...] += dg
            dw_ref[...] += dw

    tok = pl.BlockSpec((tl, D_MODEL), lambda i: (i, 0))
    return pl.pallas_call(
        body, grid=(nt,),
        in_specs=[tok, tok, pl.BlockSpec((8, D_MODEL), lambda i: (0, 5)), pl.BlockSpec((1, D_MODEL), lambda i: (0, 0)),
                  pl.BlockSpec((tl, D_MODEL), lambda i: (jnp.maximum(i - nctx_t, 0), 0))],
        out_specs=[pl.BlockSpec((8, 128), lambda i: (0, 0)), tok, tok, pl.BlockSpec((8, D_MODEL), lambda i: (0, 0)),
                   pl.BlockSpec((1, D_MODEL), lambda i: (0, 0))],
        out_shape=[jax.ShapeDtypeStruct((8, 128), F32), jax.ShapeDtypeStruct((t, D_MODEL), F32),
                   jax.ShapeDtypeStruct((t, D_MODEL), _MXU_DTYPE), jax.ShapeDtypeStruct((8, D_MODEL), F32),
                   jax.ShapeDtypeStruct((1, D_MODEL), F32)],
        compiler_params=_cp(("arbitrary",)), name="tail_loss")(h1, ff, mods, wf, tgt)


def _pack_w_in(w4):
    ns = w4.shape[2]
    placed = []
    for s0, s1, p0 in IN_SEGMENTS:
        for j in range(N_PLANE):
            lo, hi = max(s0, j * ns), min(s1, (j + 1) * ns)
            if lo < hi:
                placed.append((p0 + lo - s0, w4[j][:, lo - j * ns:hi - j * ns]))
    placed.sort(key=lambda e: e[0])
    pieces, end = [], 0
    for pos, piece in placed:
        assert pos == end, (pos, end)
        pieces.append(piece)
        end = pos + piece.shape[1]
    pieces.append(jnp.zeros((w4.shape[1], P_TOTAL - end), w4.dtype))
    return jnp.concatenate(pieces, axis=1)


def _unpack_w_in(g):
    ns = D_IN_PROJ // N_PLANE
    shards = []
    for j in range(N_PLANE):
        pieces = []
        for s0, s1, p0 in IN_SEGMENTS:
            lo, hi = max(s0, j * ns), min(s1, (j + 1) * ns)
            if lo < hi:
                pieces.append(g[:, p0 + lo - s0:p0 + hi - s0])
        shards.append(jnp.concatenate(pieces, axis=1))
    return jnp.stack(shards)


LATE_WEIGHTS = ['w_br_gdn', 'w_br_ssm', 'w_out', 'w_ffn_in', 'w_ffn_out']
COL_STACKED = ('ada_w', 'w_in', 'w_ffn_in')


def _from_shards(n, g):
    if n in COL_STACKED:
        return g
    if SHARD_AXIS[n] == 0:
        return g.reshape(N_PLANE * g.shape[1], g.shape[2])
    return jnp.concatenate([g[j] for j in range(N_PLANE)], axis=1)


def _to_dest_blocks(n, g):
    if g.ndim == 3:
        return g
    if SHARD_AXIS[n] == 0:
        return g.reshape(N_PLANE, g.shape[0] // N_PLANE, g.shape[1])
    sz = g.shape[1] // N_PLANE
    return jnp.stack([g[:, j * sz:(j + 1) * sz] for j in range(N_PLANE)])


def _local_step(x, c, ctx, tgt, W, late_shards=None):
    distributed = late_shards is not None
    lc, l = ctx.shape[0], x.shape[0]
    t = lc + l
    tl = 256
    assert lc == tl and l % tl == 0 and lc % CHUNK == 0
    nt, nctx_t, nctx = t // tl, lc // tl, lc // CHUNK
    act = _MXU_DTYPE
    r1 = lambda v: v.reshape(1, -1)

    xt = jnp.concatenate([ctx, x], axis=0)
    cc = jnp.concatenate([c, r1(W['c_ctx']), jnp.zeros((6, D_MODEL), F32)], axis=0)
    mods = _ada_fwd(cc, W['ada_w'], r1(W['ada_b']))

    def mod(kk):
        return Op(mods, (8, D_MODEL), lambda j, i, kk=kk: (0, kk))

    def tokop(arr, w=D_MODEL, off=0, tl_=tl):
        if arr.ndim == 3:
            return Op(arr, (2, tl_, w), lambda j, i: (0, i, off + j), load=_sum_dirs)
        return Op(arr, (tl_, w), lambda j, i: (i, off + j))

    def outop(n, dtype, w=D_MODEL, tl_=tl):
        return Op(((t, n), dtype), (tl_, w), lambda j, i: (i, j))

    def parop(arr, w, off=0):
        return Op(arr, (arr.shape[0], w), lambda j, i: (0, off + j))

    def parout(rows, n, w):
        return Op(((rows, n), F32), (rows, w), lambda j, i: (0, j))

    n1w = r1(W['norm1_w'])
    pre_fn = functools.partial(_prenorm_fn, nctx_t)
    pre_tok, pre_par = [tokop(xt)], [parop(n1w, D_MODEL), mod(1), mod(0)]
    (a,) = _tw_fwd("prenorm_fwd", pre_fn, (1, nt), pre_tok + pre_par, [outop(D_MODEL, act)])
    wp = _pack_w_in(W['w_in'])
    proj = _matmul(a, wp, 'nn', "in_proj")

    gcw, gcb = W['gdn_conv_w'], r1(W['gdn_conv_b'])
    scw, scb = W['ssm_conv_w'], r1(W['ssm_conv_b'])
    conv_parts = {}

    def conv_part(name, mode, poff, cw, cb, woff, width):
        fn = functools.partial(_conv_fn, nctx_t, mode)
        tok_ = [tokop(proj, 512, poff // 512)]
        par_ = [parop(cw, 512, woff // 512), parop(cb, 512, woff // 512)]
        conv_parts[name] = (fn, tok_, par_, width)
        (o,) = _tw_fwd("conv_" + name, fn, (width // 512, nt), tok_ + par_, [outop(width, F32, 512)])
        return o

    q = conv_part('q', 'q', P_QKV, gcw, gcb, 0, 1024)
    k = conv_part('k', 'k', P_QKV + 1024, gcw, gcb, 1024, 1024)
    v = conv_part('v', 'none', P_QKV + 2048, gcw, gcb, 2048, 1024)
    xs = conv_part('xs', 'none', P_XBC, scw, scb, 0, 2048)
    bm = conv_part('bm', 'none', P_XBC + 2048, scw, scb, 2048, 512)
    cm = conv_part('cm', 'none', P_XBC + 2560, scw, scb, 2560, 512)

    z16, z32 = jnp.zeros((16,), F32), jnp.zeros((32,), F32)
    p0 = jnp.concatenate([W['gdn_a_log'].reshape(-1), jnp.zeros((112,), F32)]).reshape(1, 128)
    p1 = jnp.concatenate([W['gdn_dt_bias'].reshape(-1), z16, W['ssm_dt_bias'].reshape(-1), z32]).reshape(1, 128)
    alog = jnp.concatenate([z32, W['ssm_a_log'].reshape(-1), z32]).reshape(1, 128)
    act_tok, act_par = [tokop(proj, 128, P_SMALL // 128)], [parop(p0, 128), parop(p1, 128)]
    (sp,) = _tw_fwd("small_act", _act_fn, (1, nt), act_tok + act_par, [outop(128, F32, 128)])

    gather_late = PlaneExchange([late_shards[n] for n in LATE_WEIGHTS], gather=True) if distributed else None
    o2, ss, tri, *late = _scan_fwd("gdn_scan_fwd", _gdn_step, [q, k, v, sp], [], 1024, GDN_HEADS, nctx,
                                   ride=gather_late, n_aux=GDN_HEADS)
    if distributed:
        W = dict(W, **{n: _from_shards(n, g) for n, g in zip(LATE_WEIGHTS, late)})
    y2, hs = _scan_fwd("ssd_scan_fwd", _ssd_step, [xs, bm, cm, sp], [alog], 2048, SSM_PAIRS, nctx)

    tlm = 128
    ntm = t // tlm
    gnw = r1(W['gdn_norm_w'])
    dl = jnp.repeat(W['ssm_d'], 64).reshape(1, 2048)
    snw = r1(W['ssm_norm_w'])
    mg_tok = [tokop(o2, 1024, 0, tlm), tokop(proj, 1024, P_ZG // 1024, tlm)]
    mg_par = [parop(gnw, 128)]
    (og,) = _tw_fwd("mix_gdn", _mixg_fn, (1, ntm), mg_tok + mg_par, [outop(1024, act, 1024, tlm)])
    ms_tok = [tokop(y2, 2048, 0, tlm), tokop(xs, 2048, 0, tlm), tokop(proj, 2048, P_ZS // 2048, tlm)]
    ms_par = [parop(dl, 2048), parop(snw, 2048)]
    (yg,) = _tw_fwd("mix_ssm", _mixs_fn, (1, ntm), ms_tok + ms_par, [outop(2048, act, 2048, tlm)])

    pg = _matmul(og, W['w_br_gdn'], 'nn', "br_gdn")
    ps = _matmul(yg, W['w_br_ssm'], 'nn', "br_ssm")
    mr_tok = [tokop(proj, 1024, P_GATE // 1024), tokop(proj, 1024, P_GATE // 1024 + 1), tokop(pg), tokop(ps)]
    (mrg,) = _tw_fwd("merge", _merge_fn, (1, nt), mr_tok, [outop(1024, act)])
    mo = _matmul(mrg, W['w_out'], 'nn', "w_out")

    n2w = r1(W['norm2_w'])
    n2_tok, n2_par = [tokop(xt), tokop(mo)], [mod(2), parop(n2w, D_MODEL), mod(4), mod(3)]
    h1, f = _tw_fwd("norm2_fwd", _norm2_fn, (1, nt), n2_tok + n2_par, [outop(1024, F32), outop(1024, act)])
    u = _matmul(f, W['w_ffn_in'], 'nn', "ffn_in")
    swb = D_FF // 2
    sw_tok = [tokop(u, swb, 0), tokop(u, swb, D_FF // swb)]
    (sw,) = _tw_fwd("swiglu", _swiglu_fn, (D_FF // swb, nt), sw_tok, [outop(D_FF, act, swb)])
    ff = _matmul(sw, W['w_ffn_out'], 'nn', "ffn_out")

    loss8, dh1, dff, dg2, dnf = _tail(h1, ff, mods, r1(W['norm_f_w']), tgt, nctx_t, tl)
    loss = loss8[0, 0]

    G = {}
    G['norm_f_w'] = dnf.reshape(-1)
    G['w_ffn_out'] = _matmul(sw, dff, 'tn', "d_ffn_out")
    dsw = _matmul(dff, W['w_ffn_out'], 'nt', "d_sw")
    dug, duu = _tw_bwd("swiglu_bwd", _swiglu_fn, (D_FF // swb, nt), sw_tok, [], [[tokop(dsw, swb)]],
                       [outop(D_FF, act, swb), outop(D_FF, act, swb)], [])
    du = jnp.concatenate([dug, duu], axis=1)
    G['w_ffn_in'] = _matmul(f, du, 'tn', "d_ffn_in", stacked_out=True)
    df = _matmul(du, W['w_ffn_in'], 'nt', "d_f")
    dxt1, dmo, dg1, dn2, dsc2, dsh2 = _tw_bwd(
        "norm2_bwd", _norm2_fn, (1, nt), n2_tok, n2_par, [[tokop(dh1)], [tokop(df)]],
        [outop(1024, F32), outop(1024, act)],
        [parout(8, 1024, 1024), parout(1, 1024, 1024), parout(8, 1024, 1024), parout(8, 1024, 1024)])
    G['norm2_w'] = dn2.reshape(-1)
    G['w_out'] = _matmul(mrg, dmo, 'tn', "d_w_out")
    dmrg = _matmul(dmo, W['w_out'], 'nt', "d_mrg")
    dgg, dgs, dpg, dps = _tw_bwd("merge_bwd", _merge_fn, (1, nt), mr_tok, [], [[tokop(dmrg)]],
                                 [outop(1024, act), outop(1024, act), outop(1024, act), outop(1024, act)], [])
    G['w_br_gdn'] = _matmul(og, dpg, 'tn', "d_br_gdn")
    G['w_br_ssm'] = _matmul(yg, dps, 'tn', "d_br_ssm")
    dog = _matmul(dpg, W['w_br_gdn'], 'nt', "d_og")
    dyg = _matmul(dps, W['w_br_ssm'], 'nt', "d_yg")

    do, dzg, dgnw = _tw_bwd("mix_gdn_bwd", _mixg_fn, (1, ntm), mg_tok, mg_par, [[tokop(dog, 1024, 0, tlm)]],
                            [outop(1024, F32, 1024, tlm), outop(1024, act, 1024, tlm)], [parout(1, 128, 128)])
    G['gdn_norm_w'] = dgnw.reshape(-1)
    dy, dxs_a, dzs, ddl, dsnw = _tw_bwd(
        "mix_ssm_bwd", _mixs_fn, (1, ntm), ms_tok, ms_par, [[tokop(dyg, 2048, 0, tlm)]],
        [outop(2048, F32, 2048, tlm), outop(2048, F32, 2048, tlm), outop(2048, act, 2048, tlm)],
        [parout(1, 2048, 2048), parout(1, 2048, 2048)])
    G['ssm_d'] = ddl.reshape(SSM_HEADS, 64).sum(axis=1)
    G['ssm_norm_w'] = dsnw.reshape(-1)

    dq2, dk2, dv2, dsp_g = _scan_bwd("gdn_scan_bwd", _gdn_step, [q, k, v, sp], [], ss, do, GDN_HEADS, nctx, aux=tri)
    received = {}
    scatter_late = (PlaneExchange([_to_dest_blocks(n, G.pop(n)) for n in LATE_WEIGHTS], gather=False)
                    if distributed else None)
    dxs2, dbm2, dcm2, dsp_s, dalog, *got = _scan_bwd("ssd_scan_bwd", _ssd_step, [xs, bm, cm, sp], [alog], hs, dy,
                                                     SSM_PAIRS, nctx, ride=scatter_late)
    received.update(zip(LATE_WEIGHTS, got))
    G['ssm_a_log'] = dalog[0, 32:96].reshape(2, SSM_HEADS)

    dconv_w, dconv_b, dpre = {}, {}, {}

    def conv_bwd(name, cot_ops):
        fn, tok_, par_, width = conv_parts[name]
        dpre[name], dconv_w[name], dconv_b[name] = _tw_bwd(
            "conv_" + name + "_bwd", fn, (width // 512, nt), tok_, par_, [cot_ops], [outop(width, act, 512)],
            [parout(3, width, 512), parout(1, width, 512)])

    conv_bwd('q', [tokop(dq2, 512)])
    conv_bwd('k', [tokop(dk2, 512)])
    conv_bwd('v', [tokop(dv2, 512)])
    conv_bwd('xs', [tokop(dxs2, 512), tokop(dxs_a, 512)])
    conv_bwd('bm', [tokop(dbm2, 512)])
    conv_bwd('cm', [tokop(dcm2, 512)])
    G['gdn_conv_w'] = jnp.concatenate([dconv_w['q'], dconv_w['k'], dconv_w['v']], axis=1)
    G['gdn_conv_b'] = jnp.concatenate([dconv_b['q'], dconv_b['k'], dconv_b['v']], axis=1).reshape(-1)
    G['ssm_conv_w'] = jnp.concatenate([dconv_w['xs'], dconv_w['bm'], dconv_w['cm']], axis=1)
    G['ssm_conv_b'] = jnp.concatenate([dconv_b['xs'], dconv_b['bm'], dconv_b['cm']], axis=1).reshape(-1)

    dsmall, dp0, dp1 = _tw_bwd("small_act_bwd", _act_fn, (1, nt), act_tok, act_par,
                               [[tokop(dsp_g, 128), tokop(dsp_s, 128)]], [outop(128, act, 128)],
                               [parout(1, 128, 128), parout(1, 128, 128)])
    G['gdn_a_log'] = dp0[0, 0:16].reshape(2, GDN_HEADS)
    G['gdn_dt_bias'] = dp1[0, 0:16].reshape(2, GDN_HEADS)
    G['ssm_dt_bias'] = dp1[0, 32:96].reshape(2, SSM_HEADS)

    zpad = jnp.zeros((t, P_TOTAL - P_SMALL - 128), act)
    dproj = jnp.concatenate([dpre['q'], dpre['k'], dpre['v'], dzg, dzs, dpre['xs'], dpre['bm'], dpre['cm'],
                             dgg, dgs, dsmall, zpad], axis=1)
    G['w_in'] = _unpack_w_in(_matmul(a, dproj, 'tn', "d_w_in"))
    if distributed:
        da, (received['w_in'],) = _matmul(dproj, wp, 'nt', "d_a", ride=PlaneExchange([G.pop('w_in')], gather=False))
    else:
        da = _matmul(dproj, wp, 'nt', "d_a")

    gx_out = Op(((l, D_MODEL), F32), (tl, D_MODEL), lambda j, i: (jnp.maximum(i - nctx_t, 0), 0))
    grad_x, dn1, dsc1, dsh1 = _tw_bwd(
        "prenorm_bwd", pre_fn, (1, nt), pre_tok, pre_par, [[tokop(da)]], [gx_out],
        [parout(1, 1024, 1024), parout(8, 1024, 1024), parout(8, 1024, 1024)], tok_add=tokop(dxt1),
        sem=("arbitrary", "arbitrary"))
    G['norm1_w'] = dn1.reshape(-1)
    dmods = jnp.concatenate([dsh1, dsc1, dg1, dsh2, dsc2, dg2], axis=1)
    G['ada_w'], dab, dcc = _ada_bwd(cc, W['ada_w'], dmods)
    G['ada_b'] = dab.reshape(-1)
    G['c_ctx'] = dcc[1]
    return loss, grad_x, G, received


def _row_tile(r, c):
    for tr in (512, 256, 128, 64, 32, 16, 8):
        if r % tr == 0 and tr * c * 4 <= (1 << 20):
            return tr
    return r


def _sum4(name, rv):
    _, r, c = rv.shape
    tr = _row_tile(r, c)

    def body(r_ref, o_ref):
        o_ref[...] = ((r_ref[0] + r_ref[1]) + r_ref[2]) + r_ref[3]

    return pl.pallas_call(
        body, grid=(r // tr,), in_specs=[pl.BlockSpec((N_PLANE, tr, c), lambda i: (0, i, 0))],
        out_specs=pl.BlockSpec((tr, c), lambda i: (i, 0)), out_shape=jax.ShapeDtypeStruct((r, c), rv.dtype),
        compiler_params=_cp(("parallel",)), name=name)(rv)


def _adamw(name, w, m, v, p, q):
    r, c = w.shape
    tr = _row_tile(r, c)

    def body(w_ref, m_ref, v_ref, p_ref, q_ref, g_ref, d_ref, mo_ref, vo_ref):
        g = p_ref[...] + q_ref[...]
        mn = ADAM_B1 * m_ref[...] + (1.0 - ADAM_B1) * g
        vn = ADAM_B2 * v_ref[...] + (1.0 - ADAM_B2) * jnp.square(g)
        m_hat = mn / (1.0 - ADAM_B1 ** ADAM_STEP)
        v_hat = vn / (1.0 - ADAM_B2 ** ADAM_STEP)
        g_ref[...] = g
        d_ref[...] = -ADAM_LR * (m_hat / (jnp.sqrt(v_hat) + ADAM_EPS) + ADAM_WD * w_ref[...])
        mo_ref[...] = mn
        vo_ref[...] = vn

    spec = pl.BlockSpec((tr, c), lambda i: (i, 0))
    return pl.pallas_call(
        body, grid=(r // tr,), in_specs=[spec] * 5, out_specs=[spec] * 4,
        out_shape=[jax.ShapeDtypeStruct((r, c), F32)] * 4, compiler_params=_cp(("parallel",)), name=name)(w, m, v, p, q)


SMALL_ROWS = 24


def _pack_small(d):
    v = jnp.concatenate([d[n].reshape(-1) for n in SMALL])
    return jnp.pad(v, (0, SMALL_ROWS * 1024 - v.shape[0])).reshape(SMALL_ROWS, 1024)


def _unpack_small(buf, like):
    v = buf.reshape(-1)
    out, off = {}, 0
    for n in SMALL:
        sz = like[n].size
        out[n] = v[off:off + sz].reshape(like[n].shape)
        off += sz
    return out


def kernel(x, c, ctx, c_ctx, ada_w, ada_b, norm1_w, w_in, gdn_conv_w, gdn_conv_b, gdn_a_log, gdn_dt_bias, gdn_norm_w, ssm_conv_w, ssm_conv_b, ssm_a_log, ssm_dt_bias, ssm_d, ssm_norm_w, w_br_gdn, w_br_ssm, w_out, norm2_w, w_ffn_in, w_ffn_out, norm_f_w, loss_target, m_c_ctx, m_ada_w, m_ada_b, m_norm1_w, m_w_in, m_gdn_conv_w, m_gdn_conv_b, m_gdn_a_log, m_gdn_dt_bias, m_gdn_norm_w, m_ssm_conv_w, m_ssm_conv_b, m_ssm_a_log, m_ssm_dt_bias, m_ssm_d, m_ssm_norm_w, m_w_br_gdn, m_w_br_ssm, m_w_out, m_norm2_w, m_w_ffn_in, m_w_ffn_out, m_norm_f_w, v_c_ctx, v_ada_w, v_ada_b, v_norm1_w, v_w_in, v_gdn_conv_w, v_gdn_conv_b, v_gdn_a_log, v_gdn_dt_bias, v_gdn_norm_w, v_ssm_conv_w, v_ssm_conv_b, v_ssm_a_log, v_ssm_dt_bias, v_ssm_d, v_ssm_norm_w, v_w_br_gdn, v_w_br_ssm, v_w_out, v_norm2_w, v_w_ffn_in, v_w_ffn_out, v_norm_f_w):
    args = dict(locals())
    wl = {n: args[n] for n in WEIGHTS}
    ml = {n: args['m_' + n] for n in WEIGHTS}
    vl = {n: args['v_' + n] for n in WEIGHTS}

    def nodepth(n, a):
        return a if n in ('c_ctx', 'norm_f_w') else a[0]

    shard = {n: nodepth(n, wl[n]).astype(_MXU_DTYPE if n in MXU_WEIGHTS else F32) for n in SHARDED}
    first = [n for n in SHARDED if n not in LATE_WEIGHTS]
    W = {n: nodepth(n, wl[n]) for n in SMALL}
    for n, g in zip(first, _plane_exchange("all_gather_plane", [shard[n] for n in first], gather=True)):
        W[n] = _from_shards(n, g)

    loss_local, grad_x, G, received = _local_step(x[0], c, ctx[0], loss_target[0], W,
                                                  late_shards={n: shard[n] for n in LATE_WEIGHTS})
    loss = lax.psum(loss_local, ("x", "y", "c"))

    small_g = _pack_small(G)
    last = [n for n in SHARDED if n not in received]
    send = [_to_dest_blocks(n, G[n]) for n in last] + [jnp.broadcast_to(small_g[None], (N_PLANE,) + small_g.shape)]
    received.update(zip(last + ['small'], _plane_exchange("scatter_plane", send, gather=False)))
    names = SHARDED + ['small']
    plane_sum = [_sum4("sum4_" + n, received[n]) for n in names]
    other = _swap_sibling(plane_sum)

    wd = {n: nodepth(n, wl[n]) for n in WEIGHTS}
    md = {n: nodepth(n, ml[n]) for n in WEIGHTS}
    vd = {n: nodepth(n, vl[n]) for n in WEIGHTS}
    res = {}
    for n, p, q in zip(names, plane_sum, other):
        if n == 'small':
            outs = _adamw("adamw_small", _pack_small(wd), _pack_small(md), _pack_small(vd), p, q)
            unpacked = [_unpack_small(o, wd) for o in outs]
            for sn in SMALL:
                res[sn] = [u[sn].reshape(wl[sn].shape) for u in unpacked]
        else:
            outs = _adamw("adamw_" + n, wd[n], md[n], vd[n], p, q)
            res[n] = [o.reshape(wl[n].shape) for o in outs]
    flat = [res[n][kind] for kind in range(4) for n in WEIGHTS]
    return (loss, grad_x[None], *flat)
```

```python
import functools

import jax
import jax.numpy as jnp
from jax import lax
from jax.experimental import pallas as pl
from jax.experimental.pallas import tpu as pltpu

F32 = jnp.float32
HI = lax.Precision.HIGHEST
_MXU_DTYPE = jnp.bfloat16
_SCAN_DTYPE = jnp.bfloat16
_GRAD_DTYPE = jnp.bfloat16

D_MODEL = 1024
EPS = 1e-6
CHUNK = 64
GRID_W = 64
GDN_HEADS = 8
GDN_DK = 128
SSM_HEADS = 32
SSM_PAIRS = 16
D_FF = 2816
D_IN_PROJ = 11360
N_PLANE = 4
P_QKV, P_ZG, P_ZS, P_XBC, P_GATE, P_SMALL, P_TOTAL = 0, 3072, 4096, 6144, 9216, 11264, 11776
IN_SEGMENTS = [(0, 3072, P_QKV), (3072, 4096, P_ZG), (4096, 4112, P_SMALL), (4112, 4128, P_SMALL + 16),
               (4128, 6176, P_ZS), (6176, 9248, P_XBC), (9248, 9312, P_SMALL + 32), (9312, 11360, P_GATE)]
ADAM_LR, ADAM_B1, ADAM_B2, ADAM_EPS, ADAM_WD, ADAM_STEP = 0.001, 0.9, 0.999, 1e-08, 0.01, 10

VMEM_LIMIT = 48 * 1024 * 1024

WEIGHTS = ['c_ctx', 'ada_w', 'ada_b', 'norm1_w', 'w_in', 'gdn_conv_w', 'gdn_conv_b', 'gdn_a_log', 'gdn_dt_bias',
           'gdn_norm_w', 'ssm_conv_w', 'ssm_conv_b', 'ssm_a_log', 'ssm_dt_bias', 'ssm_d', 'ssm_norm_w', 'w_br_gdn',
           'w_br_ssm', 'w_out', 'norm2_w', 'w_ffn_in', 'w_ffn_out', 'norm_f_w']
SHARD_AXIS = {'ada_w': 1, 'w_in': 1, 'gdn_conv_w': 1, 'ssm_conv_w': 1, 'w_br_gdn': 0, 'w_br_ssm': 0, 'w_out': 0,
              'w_ffn_in': 1, 'w_ffn_out': 0}
SHARDED = [n for n in WEIGHTS if n in SHARD_AXIS]
SMALL = [n for n in WEIGHTS if n not in SHARD_AXIS]
MXU_WEIGHTS = ['ada_w', 'w_in', 'w_br_gdn', 'w_br_ssm', 'w_out', 'w_ffn_in', 'w_ffn_out']


def _cp(sem):
    return pltpu.CompilerParams(dimension_semantics=sem, vmem_limit_bytes=VMEM_LIMIT)


def _pick(n, cands):
    for c in cands:
        if n % c == 0:
            return c
    raise ValueError(f"no tile for {n}")


HBM = pl.BlockSpec(memory_space=pl.ANY)
MESH = pl.DeviceIdType.MESH


def _plane_peers():
    x, y, c = lax.axis_index("x"), lax.axis_index("y"), lax.axis_index("c")
    return (x, y, c), [(1 - x, y, c), (x, 1 - y, c), (1 - x, 1 - y, c)]


class PlaneExchange:
    def __init__(self, arrs, gather):
        self.arrs, self.gather, self.n = list(arrs), gather, len(arrs)
        self.in_specs = [HBM] * self.n
        self.out_specs = [HBM] * self.n
        self.out_shape = [jax.ShapeDtypeStruct((N_PLANE,) + a.shape if gather else a.shape, a.dtype) for a in self.arrs]
        self.scratch = [pltpu.SemaphoreType.DMA((3 * self.n,)), pltpu.SemaphoreType.DMA((3 * self.n,)),
                        pltpu.SemaphoreType.DMA((self.n,))]

    def _copies(self, ins, outs, sems):
        send_sems, recv_sems, local_sems = sems
        (x, y, c), peers = _plane_peers()
        me = 2 * x + y
        copies = []
        for ti in range(self.n):
            src = ins[ti] if self.gather else ins[ti].at[me]
            copies.append(pltpu.make_async_copy(src, outs[ti].at[me], local_sems.at[ti]))
            for kk, (px, py, pc) in enumerate(peers):
                src = ins[ti] if self.gather else ins[ti].at[2 * px + py]
                copies.append(pltpu.make_async_remote_copy(
                    src_ref=src, dst_ref=outs[ti].at[me], send_sem=send_sems.at[3 * ti + kk],
                    recv_sem=recv_sems.at[3 * ti + kk], device_id=(px, py, pc), device_id_type=MESH))
        return copies

    def start(self, ins, outs, sems):
        for cp in self._copies(ins, outs, sems):
            cp.start()

    def wait(self, ins, outs, sems):
        for cp in self._copies(ins, outs, sems):
            cp.wait()


def _plane_exchange(name, arrs, gather):
    ex = PlaneExchange(arrs, gather)
    n = ex.n

    def body(*refs):
        ins, outs, sems = refs[:n], refs[n:2 * n], refs[2 * n:]
        ex.start(ins, outs, sems)
        ex.wait(ins, outs, sems)

    return pl.pallas_call(body, in_specs=ex.in_specs, out_specs=ex.out_specs, out_shape=ex.out_shape,
                          scratch_shapes=ex.scratch, name=name)(*arrs)


def _gather_two_level(name, arrs):
    n = len(arrs)
    split = [a.shape[0] % 32 == 0 for a in arrs]

    def body(*refs):
        ins, outs = refs[:n], refs[n:2 * n]
        ici_send, ici_recv, d2d_send, d2d_recv, local_sems = refs[2 * n:]
        (x, y, c), peers = _plane_peers()
        me = 2 * x + y

        def part(ref, ti):
            if not split[ti]:
                return ref
            rows = arrs[ti].shape[0] // 2
            return ref.at[pl.ds(c * rows, rows)]

        local = [pltpu.make_async_copy(ins[ti], outs[ti].at[me], local_sems.at[ti]) for ti in range(n)]
        ici = [pltpu.make_async_remote_copy(
            src_ref=part(ins[ti], ti), dst_ref=part(outs[ti].at[me], ti), send_sem=ici_send.at[3 * ti + kk],
            recv_sem=ici_recv.at[3 * ti + kk], device_id=peer, device_id_type=MESH)
            for ti in range(n) for kk, peer in enumerate(peers)]
        for cp in local + ici:
            cp.start()
        d2d = []
        for ti in range(n):
            for kk, (px, py, pc) in enumerate(peers):
                ici[3 * ti + kk].wait_recv()
                if split[ti]:
                    piece = part(outs[ti].at[2 * px + py], ti)
                    cp = pltpu.make_async_remote_copy(
                        src_ref=piece, dst_ref=piece, send_sem=d2d_send.at[3 * ti + kk],
                        recv_sem=d2d_recv.at[3 * ti + kk], device_id=(x, y, 1 - c), device_id_type=MESH)
                    cp.start()
                    d2d.append(cp)
        for cp in ici:
            cp.wait_send()
        for cp in d2d:
            cp.wait()
        for cp in local:
            cp.wait()

    return pl.pallas_call(
        body, in_specs=[HBM] * n, out_specs=[HBM] * n,
        out_shape=[jax.ShapeDtypeStruct((N_PLANE,) + a.shape, a.dtype) for a in arrs],
        scratch_shapes=[pltpu.SemaphoreType.DMA((3 * n,))] * 4 + [pltpu.SemaphoreType.DMA((n,))], name=name)(*arrs)


def _swap_sibling(arrs):
    n = len(arrs)

    def body(*refs):
        ins, outs, send_sems, recv_sems = refs[:n], refs[n:2 * n], refs[2 * n], refs[2 * n + 1]
        x, y, c = lax.axis_index("x"), lax.axis_index("y"), lax.axis_index("c")
        copies = [pltpu.make_async_remote_copy(src_ref=ins[ti], dst_ref=outs[ti], send_sem=send_sems.at[ti],
                                               recv_sem=recv_sems.at[ti], device_id=(x, y, 1 - c), device_id_type=MESH)
                  for ti in range(n)]
        for cp in copies:
            cp.start()
        for cp in copies:
            cp.wait()

    return pl.pallas_call(
        body, in_specs=[HBM] * n, out_specs=[HBM] * n, out_shape=[jax.ShapeDtypeStruct(a.shape, a.dtype) for a in arrs],
        scratch_shapes=[pltpu.SemaphoreType.DMA((n,)), pltpu.SemaphoreType.DMA((n,))], name="swap_sibling")(*arrs)


MATMUL_VMEM_BUDGET = 36 * 1024 * 1024
TILE_CANDIDATES = (2944, 2816, 1408, 1024, 768, 512, 256, 128)
STEP_COST_BYTES = 1 << 20


def _matmul_tiles(m, n, k, ab, bb, ob, tn_fixed=None, tk_fixed=None):
    def cands(dim, whole_up_to):
        out = [c for c in TILE_CANDIDATES if dim % c == 0]
        if dim <= whole_up_to and dim not in out:
            out.append(dim)
        return out

    best = None
    for tm in cands(m, 3072):
        for tn in ([tn_fixed] if tn_fixed else cands(n, 3072)):
            for tk in ([tk_fixed] if tk_fixed else cands(k, 2048)):
                gi, gj, gk = m // tm, n // tn, k // tk
                vmem = 2 * (tm * tk * ab + tk * tn * bb + tm * tn * ob) + (tm * tn * 4 if gk > 1 else 0)
                if vmem > MATMUL_VMEM_BUDGET:
                    continue
                a_reads = 1 if gk == 1 else gj
                b_reads = 1 if (gk == 1 and gj == 1) else gi
                cost = (m * k * ab * a_reads + k * n * bb * b_reads + m * n * ob + gi * gj * gk * STEP_COST_BYTES)
                if best is None or cost < best[0]:
                    best = (cost, tm, tn, tk)
    assert best is not None, (m, n, k)
    return best[1:]


def _matmul(a, b, form, name, out_dtype=F32, stacked_out=False, ride=None):
    dims = {'nn': (((1,), (0,)), ((), ())), 'nt': (((1,), (1,)), ((), ())), 'tn': (((0,), (0,)), ((), ()))}[form]
    stacked_b = b.ndim == 3
    ns = b.shape[2] if stacked_b else None
    if form == 'nn':
        m, k = a.shape
        n = b.shape[0] * ns if stacked_b else b.shape[1]
    elif form == 'nt':
        m, k = a.shape
        n = b.shape[1] if stacked_b else b.shape[0]
    else:
        k, m = a.shape
        n = b.shape[1]
    ob = jnp.dtype(out_dtype).itemsize
    tm, tn, tk = _matmul_tiles(m, n, k, a.dtype.itemsize, b.dtype.itemsize, ob,
                               tn_fixed=(n // N_PLANE if stacked_out else ns if (stacked_b and form == 'nn') else None),
                               tk_fixed=(ns if (stacked_b and form == 'nt') else None))
    nk = k // tk
    grid = (m // tm, n // tn, nk)
    n_ride = ride.n if ride is not None else 0

    def body(*refs):
        a_ref, b_ref = refs[0], refs[1]
        ride_in = refs[2:2 + n_ride]
        o_ref = refs[2 + n_ride]
        ride_out = refs[3 + n_ride:3 + 2 * n_ride]
        acc_ref = refs[3 + 2 * n_ride]
        sems = refs[4 + 2 * n_ride:]
        i, j, kk = pl.program_id(0), pl.program_id(1), pl.program_id(2)
        if ride is not None:
            @pl.when((i == 0) & (j == 0) & (kk == 0))
            def _():
                ride.start(ride_in, ride_out, sems)

        def put(val):
            if stacked_out:
                o_ref[0] = val.astype(o_ref.dtype)
            else:
                o_ref[...] = val.astype(o_ref.dtype)

        bv = b_ref[0] if stacked_b else b_ref[...]
        part = lax.dot_general(a_ref[...].astype(_MXU_DTYPE), bv.astype(_MXU_DTYPE), dims, preferred_element_type=F32)
        if nk == 1:
            put(part)
        else:
            @pl.when(kk == 0)
            def _():
                acc_ref[...] = part

            @pl.when(kk > 0)
            def _():
                acc_ref[...] += part

            @pl.when(kk == nk - 1)
            def _():
                put(acc_ref[...])

        if ride is not None:
            @pl.when((i == grid[0] - 1) & (j == grid[1] - 1) & (kk == nk - 1))
            def _():
                ride.wait(ride_in, ride_out, sems)

    if form == 'nn':
        a_spec = pl.BlockSpec((tm, tk), lambda i, j, kk: (i, kk))
        b_spec = (pl.BlockSpec((1, tk, tn), lambda i, j, kk: (j, kk, 0)) if stacked_b
                  else pl.BlockSpec((tk, tn), lambda i, j, kk: (kk, j)))
    elif form == 'nt':
        a_spec = pl.BlockSpec((tm, tk), lambda i, j, kk: (i, kk))
        b_spec = (pl.BlockSpec((1, tn, tk), lambda i, j, kk: (kk, j, 0)) if stacked_b
                  else pl.BlockSpec((tn, tk), lambda i, j, kk: (j, kk)))
    else:
        a_spec = pl.BlockSpec((tk, tm), lambda i, j, kk: (kk, i))
        b_spec = pl.BlockSpec((tk, tn), lambda i, j, kk: (kk, j))
    if stacked_out:
        o_spec = pl.BlockSpec((1, tm, tn), lambda i, j, kk: (j, i, 0))
        o_shape = jax.ShapeDtypeStruct((N_PLANE, m, tn), out_dtype)
    else:
        o_spec = pl.BlockSpec((tm, tn), lambda i, j, kk: (i, j))
        o_shape = jax.ShapeDtypeStruct((m, n), out_dtype)
    acc_shape = (tm, tn) if nk > 1 else (8, 128)
    if ride is None:
        return pl.pallas_call(
            body, grid=grid, in_specs=[a_spec, b_spec], out_specs=o_spec, out_shape=o_shape,
            scratch_shapes=[pltpu.VMEM(acc_shape, F32)],
            compiler_params=_cp(("parallel", "parallel", "arbitrary")), name=name)(a, b)
    res = pl.pallas_call(
        body, grid=grid, in_specs=[a_spec, b_spec] + ride.in_specs, out_specs=[o_spec] + ride.out_specs,
        out_shape=[o_shape] + ride.out_shape, scratch_shapes=[pltpu.VMEM(acc_shape, F32)] + ride.scratch,
        compiler_params=_cp(("arbitrary", "arbitrary", "arbitrary")), name=name)(a, b, *ride.arrs)
    return res[0], res[1:]


class Op:
    def __init__(self, arr, bs, im, load=None):
        self.arr, self.bs, self.im = arr, bs, im
        self.load = load or (lambda r: r[...].astype(F32))

    def spec(self):
        return pl.BlockSpec(self.bs, self.im)


def _sum_dirs(r):
    return r[0].astype(F32) + r[1].astype(F32)


def _tw_fwd(name, fn, grid, ins, outs):
    n_in = len(ins)

    def body(*refs):
        info = (pl.program_id(0), pl.program_id(1))
        vals = [op.load(r) for op, r in zip(ins, refs[:n_in])]
        res = fn(info, *vals)
        for r, v in zip(refs[n_in:], res):
            r[...] = v.astype(r.dtype)

    return pl.pallas_call(
        body, grid=grid, in_specs=[op.spec() for op in ins], out_specs=[op.spec() for op in outs],
        out_shape=[jax.ShapeDtypeStruct(*op.arr) for op in outs],
        compiler_params=_cp(("parallel", "arbitrary")), name=name)(*[op.arr for op in ins])


def _tw_bwd(name, fn, grid, tok, par, cots, tok_out, par_out, tok_add=None, sem=("parallel", "arbitrary")):
    n_tok = len(tok)
    flat_cots = [op for group in cots for op in group]
    extra = [tok_add] if tok_add is not None else []
    out_ops = list(tok_out) + list(par_out)

    def body(*refs):
        info = (pl.program_id(0), pl.program_id(1))
        it = iter(refs)
        tok_v = [op.load(next(it)) for op in tok]
        par_v = [op.load(next(it)) for op in par]
        cot_v = []
        for group in cots:
            vals = [op.load(next(it)) for op in group]
            cot_v.append(functools.reduce(lambda u, w: u + w, vals))
        add_v = [op.load(next(it)) for op in extra]
        _, pull = jax.vjp(lambda *a: fn(info, *a), *tok_v, *par_v)
        grads = pull(tuple(cot_v))
        for i in range(n_tok):
            r = next(it)
            g = grads[i] + add_v[0] if (i == 0 and add_v) else grads[i]
            r[...] = g.astype(r.dtype)
        first = pl.program_id(1) == 0
        for i in range(len(par)):
            r = next(it)
            g = grads[n_tok + i]

            @pl.when(first)
            def _(r=r, g=g):
                r[...] = g

            @pl.when(jnp.logical_not(first))
            def _(r=r, g=g):
                r[...] += g

    ops = tok + par + flat_cots + extra
    return pl.pallas_call(
        body, grid=grid, in_specs=[op.spec() for op in ops], out_specs=[op.spec() for op in out_ops],
        out_shape=[jax.ShapeDtypeStruct(*op.arr) for op in out_ops],
        compiler_params=_cp(sem), name=name)(*[op.arr for op in ops])


def _silu(x):
    return x * jax.nn.sigmoid(x)


def _rms(x):
    return x * lax.rsqrt(jnp.mean(x * x, axis=-1, keepdims=True) + EPS)


@functools.partial(jax.custom_vjp, nondiff_argnums=(1,))
def _shift_rows(x, k):
    return pltpu.roll(x, k % x.shape[0], 0)


def _shift_rows_fwd(x, k):
    return _shift_rows(x, k), None


def _shift_rows_bwd(k, _, g):
    return (_shift_rows(g, -k),)


_shift_rows.defvjp(_shift_rows_fwd, _shift_rows_bwd)


def _prenorm_fn(nctx_t, info, x, w, sc8, sh8):
    is_ctx = info[1] < nctx_t
    sc = jnp.where(is_ctx, sc8[1:2], sc8[0:1])
    sh = jnp.where(is_ctx, sh8[1:2], sh8[0:1])
    return (_rms(x) * w * (1.0 + sc) + sh,)


def _conv_fn(nctx_t, mode, info, x, w, b):
    n, c = x.shape
    is_ctx = info[1] < nctx_t
    idx = lax.broadcasted_iota(jnp.int32, (n, 1), 0)
    rr = jnp.where(is_ctx, idx, idx % GRID_W)
    first = rr == 0
    last = rr == jnp.where(is_ctx, n - 1, GRID_W - 1)
    prev = jnp.where(first, 0.0, _shift_rows(x, 1))
    nxt = jnp.where(last, 0.0, _shift_rows(x, -1))
    y = b + prev * w[0:1] + x * w[1:2] + nxt * w[2:3]
    y = _silu(y)
    if mode == 'none':
        return (y,)
    scale = GDN_DK ** -0.5 if mode == 'q' else 1.0
    outs = []
    for h in range(c // 128):
        yh = y[:, h * 128:(h + 1) * 128]
        outs.append(yh * lax.rsqrt(jnp.sum(yh * yh, axis=-1, keepdims=True) + EPS) * scale)
    return (jnp.concatenate(outs, axis=1),)


def _act_fn(info, x, p0, p1):
    lane = lax.broadcasted_iota(jnp.int32, x.shape, 1)
    sp = jax.nn.softplus(x + p1)
    g = -jnp.exp(p0) * sp
    bt = jax.nn.sigmoid(x)
    return (jnp.where(lane < 16, g, jnp.where(lane < 32, bt, jnp.where(lane < 96, sp, 0.0))),)


def _mixg_fn(info, o, zg, gnw):
    outs = []
    for h in range(GDN_HEADS):
        outs.append(_rms(o[:, h * 128:(h + 1) * 128]) * gnw)
    return (jnp.concatenate(outs, axis=1) * _silu(zg),)


def _mixs_fn(info, y, xs, zs, dl, snw):
    yy = (y + dl * xs) * _silu(zs)
    outs = []
    for g in range(4):
        outs.append(_rms(yy[:, g * 512:(g + 1) * 512]))
    return (jnp.concatenate(outs, axis=1) * snw,)


def _merge_fn(info, gg, gs, pg, ps):
    return (jax.nn.sigmoid(gg) * pg + jax.nn.sigmoid(gs) * ps,)


def _norm2_fn(info, xt, mo, g8, w, sc8, sh8):
    h1 = xt + g8[0:1] * mo
    return (h1, _rms(h1) * w * (1.0 + sc8[0:1]) + sh8[0:1])


def _swiglu_fn(info, ug, uu):
    return (_silu(ug) * uu,)


_NN = (((1,), (0,)), ((), ()))
_NT = (((1,), (1,)), ((), ()))
_TN = (((0,), (0,)), ((), ()))


def _mmh(a, b):
    return lax.dot_general(a, b, _NN, precision=HI, preferred_element_type=F32)


def _dot1(a, b, dims):
    return lax.dot_general(a.astype(_SCAN_DTYPE), b.astype(_SCAN_DTYPE), dims, preferred_element_type=F32)


def _mm(a, b):
    return _dot1(a, b, _NN)


def _mm_nt(a, b):
    return _dot1(a, b, _NT)


def _mm_tn(a, b):
    return _dot1(a, b, _TN)


def _split2(a):
    hi = a.astype(_SCAN_DTYPE)
    return hi, (a - hi.astype(F32)).astype(_SCAN_DTYPE)


def _dot3(a, b, dims):
    ah, al = _split2(a)
    bh, bl = _split2(b)
    d = lambda u, w: lax.dot_general(u, w, dims, preferred_element_type=F32)
    return d(ah, bh) + (d(ah, bl) + d(al, bh))


def _order_masks(d):
    i = lax.broadcasted_iota(jnp.int32, (CHUNK, CHUNK), 0)
    j = lax.broadcasted_iota(jnp.int32, (CHUNK, CHUNK), 1)
    s = jnp.where(d == 0, 1, -1) * (i - j)
    return (s >= 0).astype(F32), (s > 0).astype(F32)


@jax.custom_vjp
def _unit_tri_inv(mats):
    i = lax.broadcasted_iota(jnp.int32, (CHUNK, CHUNK), 0)
    j = lax.broadcasted_iota(jnp.int32, (CHUNK, CHUNK), 1)
    eye = (i == j).astype(F32)
    ps = [-a for a in mats]
    ts = [eye + p for p in ps]
    for _ in range(5):
        ps = [_dot3(p, p, _NN) for p in ps]
        ts = [t + _dot3(t, p, _NN) for t, p in zip(ts, ps)]
    return tuple(ts)


def _uti_fwd(mats):
    ts = _unit_tri_inv(mats)
    return ts, ts


def _uti_bwd(ts, gs):
    inner = [_dot3(g, t, _NT) for g, t in zip(gs, ts)]
    return (tuple(-_dot3(t, u, _TN) for t, u in zip(ts, inner)),)


_unit_tri_inv.defvjp(_uti_fwd, _uti_bwd)


@jax.custom_vjp
def _unit_tri_inv_given(mats, ts):
    return ts


def _utig_fwd(mats, ts):
    return ts, ts


def _utig_bwd(ts, gs):
    return _uti_bwd(ts, gs)[0], tuple(jnp.zeros_like(t) for t in ts)


_unit_tri_inv_given.defvjp(_utig_fwd, _utig_bwd)


def _lane_col(blk, lane_idx):
    lane = lax.broadcasted_iota(jnp.int32, blk.shape, 1)
    return jnp.sum(jnp.where(lane == lane_idx, blk, 0.0), axis=1, keepdims=True)


def _decay_mat(cum, incl):
    cb = jnp.broadcast_to(cum, (CHUNK, CHUNK))
    return jnp.exp(jnp.minimum(cb - cb.T, 0.0)) * incl


def _gdn_step(d, q, k, v, spb, *states, aux=None, want_aux=False):
    incl, strict = _order_masks(d)
    cum = _mmh(incl, spb)
    tot = jnp.sum(spb, axis=0, keepdims=True)
    hs = range(GDN_HEADS)
    cat = jnp.concatenate
    qs = [q[:, h * 128:(h + 1) * 128] for h in hs]
    ks = [k[:, h * 128:(h + 1) * 128] for h in hs]
    vs = [v[:, h * 128:(h + 1) * 128] for h in hs]
    gcum = [_lane_col(cum, d * GDN_HEADS + h) for h in hs]
    glast = [_lane_col(tot, d * GDN_HEADS + h) for h in hs]
    beta = [_lane_col(spb, 16 + d * GDN_HEADS + h) for h in hs]
    decay = [_decay_mat(gcum[h], incl) for h in hs]
    egc = [jnp.exp(gcum[h]) for h in hs]
    kb = [ks[h] * beta[h] for h in hs]
    kq = [_mm_nt(cat([kb[h], qs[h]], axis=0), ks[h]) for h in hs]
    mats = tuple(kq[h][:CHUNK] * decay[h] * strict for h in hs)
    ts = _unit_tri_inv(mats) if aux is None else _unit_tri_inv_given(mats, tuple(aux))
    wu = [_mm(ts[h], cat([kb[h] * egc[h], vs[h] * beta[h]], axis=1)) for h in hs]
    ws = [_mm(cat([wu[h][:, :128], qs[h] * egc[h]], axis=0), states[h]) for h in hs]
    vn = [wu[h][:, 128:] - ws[h][:CHUNK] for h in hs]
    outs = [ws[h][CHUNK:] + _mm(kq[h][CHUNK:] * decay[h], vn[h]) for h in hs]
    new_states = [states[h] * jnp.exp(glast[h]) + _mm_tn(ks[h] * jnp.exp(glast[h] - gcum[h]), vn[h]) for h in hs]
    return (cat(outs, axis=1), *new_states, *(ts if want_aux else ()))


def _split3(a):
    a1 = a.astype(_SCAN_DTYPE)
    r = a - a1.astype(F32)
    a2 = r.astype(_SCAN_DTYPE)
    return a1, a2, (r - a2.astype(F32)).astype(_SCAN_DTYPE)


def _exact_dot(a, e, dims, split_lhs):
    parts = _split3(a if split_lhs else e)
    d = (lambda u: lax.dot_general(u, e, dims, preferred_element_type=F32)) if split_lhs else \
        (lambda u: lax.dot_general(a, u, dims, preferred_element_type=F32))
    return d(parts[0]) + (d(parts[1]) + d(parts[2]))


@jax.custom_vjp
def _spread(a, e):
    return _exact_dot(a, e, _NN, True)


def _spread_fwd(a, e):
    return _spread(a, e), e


def _spread_bwd(e, g):
    return _exact_dot(g, e, _NT, True), jnp.zeros_like(e)


_spread.defvjp(_spread_fwd, _spread_bwd)


@jax.custom_vjp
def _colsum_bcast(z):
    return _exact_dot(jnp.ones((z.shape[0], z.shape[0]), _SCAN_DTYPE), z, _NN, False)


def _colsum_fwd(z):
    return _colsum_bcast(z), None


def _colsum_bwd(_, g):
    return (_colsum_bcast(g),)


_colsum_bcast.defvjp(_colsum_fwd, _colsum_bwd)


def _ssd_step(d, x, bm, cm, spb, alog, *states):
    wdt = x.shape[1]
    incl, _ = _order_masks(d)
    lane1 = lax.broadcasted_iota(jnp.int32, (1, 128), 1)
    lo_lane = 32 + d * SSM_HEADS
    a_vec = jnp.where(lane1 >= lo_lane, jnp.where(lane1 < lo_lane + SSM_HEADS, -jnp.exp(alog), 0.0), 0.0)
    adt = spb * a_vec
    acum = _mmh(incl, adt)
    alast = jnp.sum(adt, axis=0, keepdims=True)
    e = (lax.broadcasted_iota(jnp.int32, (128, wdt), 0)
         == lo_lane + lax.broadcasted_iota(jnp.int32, (128, wdt), 1) // 64).astype(_SCAN_DTYPE)
    dt2 = _spread(spb, e)
    ac2 = _spread(acum, e)
    al2 = _spread(jnp.broadcast_to(alast, (8, 128)), e)[0:1]
    ci = lax.broadcasted_iota(jnp.int32, (CHUNK, wdt), 0)
    pos = lax.broadcasted_iota(jnp.int32, (CHUNK, wdt), 1) % 64
    row = _colsum_bcast(jnp.where(ci == pos, ac2, 0.0))
    incl_t = (jnp.where(d == 0, 1, -1) * (ci - pos) >= 0).astype(F32)
    seg = jnp.exp(jnp.minimum(ac2 - row, 0.0)) * incl_t
    xdt = x * dt2
    gam = jnp.exp(ac2)
    xe = xdt * jnp.exp(al2 - ac2)
    low = lax.broadcasted_iota(jnp.int32, (CHUNK, 128), 1) < 64
    row_low = lax.broadcasted_iota(jnp.int32, (128, 1), 0) < 64
    ps = range(SSM_PAIRS)
    sl = [slice(p * 128, (p + 1) * 128) for p in ps]
    bg = [bm[:, g * 128:(g + 1) * 128] for g in range(4)]
    cg = [cm[:, g * 128:(g + 1) * 128] for g in range(4)]
    cb2 = [_mm_nt(cg[g], jnp.concatenate([bg[g], bg[g]], axis=0)) for g in range(4)]
    xd = [jnp.concatenate([jnp.where(low, xdt[:, sl[p]], 0.0), jnp.where(low, 0.0, xdt[:, sl[p]])], axis=0) for p in ps]
    yd = [_mm(cb2[p // 4] * seg[:, sl[p]], xd[p]) for p in ps]
    yo = [_mm_nt(cg[p // 4], states[p]) for p in ps]
    ys = [yd[p] + gam[:, sl[p]] * yo[p] for p in ps]
    new = [_mm_tn(xe[:, sl[p]], bg[p // 4]) for p in ps]
    al0 = [_lane_col(alast, lo_lane + 2 * p) for p in ps]
    al1 = [_lane_col(alast, lo_lane + 2 * p + 1) for p in ps]
    new_states = [states[p] * jnp.exp(jnp.where(row_low, al0[p], al1[p])) + new[p] for p in ps]
    return (jnp.concatenate(ys, axis=1), *new_states)


def _chunk_of(d, p, nctx, nc):
    return jnp.where(d == 0, p, jnp.where(p < nctx, nctx - 1 - p, nctx + nc - 1 - p))


class _NoRide:
    n, arrs, in_specs, out_specs, out_shape, scratch = 0, [], [], [], [], []


def _scan_fwd(name, step, toks, pars, out_width, n_state, nctx, ride=None, n_aux=0):
    t = toks[0].shape[0]
    nc = t // CHUNK
    n_tok, n_par = len(toks), len(pars)
    rd = ride if ride is not None else _NoRide
    kw = dict(want_aux=True) if n_aux else {}

    def body(*refs):
        it = iter(refs)
        tok_refs = [next(it) for _ in range(n_tok)]
        par_refs = [next(it) for _ in range(n_par)]
        ride_in = [next(it) for _ in range(rd.n)]
        o_ref, ss_ref = next(it), next(it)
        aux_ref = next(it) if n_aux else None
        ride_out = [next(it) for _ in range(rd.n)]
        s_scr = next(it)
        sems = list(it)
        d, p = pl.program_id(0), pl.program_id(1)
        if ride is not None:
            @pl.when((d == 0) & (p == 0))
            def _():
                ride.start(ride_in, ride_out, sems)

        @pl.when(p == 0)
        def _():
            s_scr[...] = jnp.zeros(s_scr.shape, F32)

        ss_ref[0, 0] = s_scr[...]
        res = step(d, *[r[...] for r in tok_refs], *[r[...] for r in par_refs], *[s_scr[h] for h in range(n_state)],
                   **kw)
        o_ref[0] = res[0]
        for h in range(n_state):
            s_scr[h] = res[1 + h]
        for i in range(n_aux):
            aux_ref[0, 0, i] = res[1 + n_state + i]
        if ride is not None:
            @pl.when((d == 1) & (p == nc - 1))
            def _():
                ride.wait(ride_in, ride_out, sems)

    ch = lambda d, p: _chunk_of(d, p, nctx, nc)
    in_specs = [pl.BlockSpec((CHUNK, a.shape[1]), lambda d, p: (ch(d, p), 0)) for a in toks]
    in_specs += [pl.BlockSpec(a.shape, lambda d, p: (0, 0)) for a in pars]
    aux_specs = [pl.BlockSpec((1, 1, n_aux, CHUNK, CHUNK), lambda d, p: (d, p, 0, 0, 0))] if n_aux else []
    aux_shape = [jax.ShapeDtypeStruct((2, nc, n_aux, CHUNK, CHUNK), F32)] if n_aux else []
    return pl.pallas_call(
        body, grid=(2, nc), in_specs=in_specs + rd.in_specs,
        out_specs=[pl.BlockSpec((1, CHUNK, out_width), lambda d, p: (d, ch(d, p), 0)),
                   pl.BlockSpec((1, 1, n_state, 128, 128), lambda d, p: (d, p, 0, 0, 0))] + aux_specs + rd.out_specs,
        out_shape=[jax.ShapeDtypeStruct((2, t, out_width), F32),
                   jax.ShapeDtypeStruct((2, nc, n_state, 128, 128), F32)] + aux_shape + rd.out_shape,
        scratch_shapes=[pltpu.VMEM((n_state, 128, 128), F32)] + rd.scratch,
        compiler_params=_cp(("arbitrary", "arbitrary")), name=name)(*toks, *pars, *rd.arrs)


def _scan_bwd(name, step, toks, pars, ss, dout, n_state, nctx, ride=None, aux=None):
    t = toks[0].shape[0]
    nc = t // CHUNK
    n_tok, n_par = len(toks), len(pars)
    rd = ride if ride is not None else _NoRide
    n_aux = aux.shape[2] if aux is not None else 0

    def body(*refs):
        it = iter(refs)
        tok_refs = [next(it) for _ in range(n_tok)]
        par_refs = [next(it) for _ in range(n_par)]
        ss_ref, do_ref = next(it), next(it)
        aux_ref = next(it) if n_aux else None
        ride_in = [next(it) for _ in range(rd.n)]
        dtok_refs = [next(it) for _ in range(n_tok)]
        dpar_refs = [next(it) for _ in range(n_par)]
        ride_out = [next(it) for _ in range(rd.n)]
        ds_scr = next(it)
        sems = list(it)
        d, pr = pl.program_id(0), pl.program_id(1)
        if ride is not None:
            @pl.when((d == 0) & (pr == 0))
            def _():
                ride.start(ride_in, ride_out, sems)

            @pl.when((d == 1) & (pr == nc - 1))
            def _():
                ride.wait(ride_in, ride_out, sems)

        @pl.when(pr == 0)
        def _():
            ds_scr[...] = jnp.zeros(ds_scr.shape, F32)

        kw = dict(aux=[aux_ref[0, 0, i] for i in range(n_aux)]) if n_aux else {}
        _, pull = jax.vjp(functools.partial(step, d, **kw), *[r[...] for r in tok_refs], *[r[...] for r in par_refs],
                          *[ss_ref[0, 0, h] for h in range(n_state)])
        grads = pull((do_ref[...], *[ds_scr[h] for h in range(n_state)]))
        for r, g in zip(dtok_refs, grads[:n_tok]):
            r[0] = g
        for h in range(n_state):
            ds_scr[h] = grads[n_tok + n_par + h]
        first = (d == 0) & (pr == 0)
        for r, g in zip(dpar_refs, grads[n_tok:n_tok + n_par]):
            @pl.when(first)
            def _(r=r, g=g):
                r[...] = g

            @pl.when(jnp.logical_not(first))
            def _(r=r, g=g):
                r[...] += g

    ch = lambda d, pr: _chunk_of(d, nc - 1 - pr, nctx, nc)
    in_specs = [pl.BlockSpec((CHUNK, a.shape[1]), lambda d, pr: (ch(d, pr), 0)) for a in toks]
    in_specs += [pl.BlockSpec(a.shape, lambda d, pr: (0, 0)) for a in pars]
    in_specs += [pl.BlockSpec((1, 1, n_state, 128, 128), lambda d, pr: (d, nc - 1 - pr, 0, 0, 0)),
                 pl.BlockSpec((CHUNK, dout.shape[1]), lambda d, pr: (ch(d, pr), 0))]
    if n_aux:
        in_specs += [pl.BlockSpec((1, 1, n_aux, CHUNK, CHUNK), lambda d, pr: (d, nc - 1 - pr, 0, 0, 0))]
    out_specs = [pl.BlockSpec((1, CHUNK, a.shape[1]), lambda d, pr: (d, ch(d, pr), 0)) for a in toks]
    out_specs += [pl.BlockSpec(a.shape, lambda d, pr: (0, 0)) for a in pars]
    out_shape = [jax.ShapeDtypeStruct((2, t, a.shape[1]), F32) for a in toks]
    out_shape += [jax.ShapeDtypeStruct(a.shape, F32) for a in pars]
    return pl.pallas_call(
        body, grid=(2, nc), in_specs=in_specs + rd.in_specs, out_specs=out_specs + rd.out_specs,
        out_shape=out_shape + rd.out_shape, scratch_shapes=[pltpu.VMEM((n_state, 128, 128), F32)] + rd.scratch,
        compiler_params=_cp(("arbitrary", "arbitrary")), name=name)(
            *toks, *pars, ss, dout, *([aux] if n_aux else []), *rd.arrs)


ADA_TN = 512


def _ada_fwd(cc, ada_w4, ada_b):
    per = ada_w4.shape[2] // ADA_TN
    n = N_PLANE * ada_w4.shape[2]

    def body(c_ref, w_ref, b_ref, o_ref):
        s = _silu(c_ref[...]).astype(_MXU_DTYPE)
        o_ref[...] = jnp.dot(s, w_ref[0].astype(_MXU_DTYPE), preferred_element_type=F32) + b_ref[...]

    return pl.pallas_call(
        body, grid=(n // ADA_TN,),
        in_specs=[pl.BlockSpec((8, D_MODEL), lambda j: (0, 0)),
                  pl.BlockSpec((1, D_MODEL, ADA_TN), lambda j: (j // per, 0, j % per)),
                  pl.BlockSpec((1, ADA_TN), lambda j: (0, j))],
        out_specs=pl.BlockSpec((8, ADA_TN), lambda j: (0, j)), out_shape=jax.ShapeDtypeStruct((8, n), F32),
        compiler_params=_cp(("parallel",)), name="ada_fwd")(cc, ada_w4, ada_b)


def _ada_bwd(cc, ada_w4, dmods):
    per = ada_w4.shape[2] // ADA_TN
    n = N_PLANE * ada_w4.shape[2]
    nj = n // ADA_TN

    def body(c_ref, w_ref, g_ref, dw_ref, db_ref, dc_ref):
        j = pl.program_id(0)
        g = g_ref[...]
        row = lax.broadcasted_iota(jnp.int32, g.shape, 0)
        g = jnp.where(row < 2, g, 0.0)
        s, pull = jax.vjp(_silu, c_ref[...])
        dw_ref[0] = lax.dot_general(s.astype(_MXU_DTYPE), g.astype(_MXU_DTYPE), _TN,
                                    preferred_element_type=F32).astype(dw_ref.dtype)
        db_ref[...] = jnp.sum(g, axis=0, keepdims=True)
        ds = lax.dot_general(g.astype(_MXU_DTYPE), w_ref[0].astype(_MXU_DTYPE), _NT, preferred_element_type=F32)

        @pl.when(j == 0)
        def _():
            dc_ref[...] = ds

        @pl.when(j > 0)
        def _():
            dc_ref[...] += ds

        @pl.when(j == nj - 1)
        def _():
            dc_ref[...] = pull(dc_ref[...])[0]

    wspec = pl.BlockSpec((1, D_MODEL, ADA_TN), lambda j: (j // per, 0, j % per))
    return pl.pallas_call(
        body, grid=(nj,),
        in_specs=[pl.BlockSpec((8, D_MODEL), lambda j: (0, 0)), wspec, pl.BlockSpec((8, ADA_TN), lambda j: (0, j))],
        out_specs=[wspec, pl.BlockSpec((1, ADA_TN), lambda j: (0, j)), pl.BlockSpec((8, D_MODEL), lambda j: (0, 0))],
        out_shape=[jax.ShapeDtypeStruct(ada_w4.shape, _GRAD_DTYPE), jax.ShapeDtypeStruct((1, n), F32),
                   jax.ShapeDtypeStruct((8, D_MODEL), F32)],
        compiler_params=_cp(("arbitrary",)), name="ada_bwd")(cc, ada_w4, dmods)


def _tail(h1, ff, mods, wf, tgt, nctx_t, tl):
    t = h1.shape[0]
    nt = t // tl

    def loss_fn(valid, h1v, ffv, g8, w, tg):
        h2 = h1v + g8[0:1] * ffv
        y = _rms(h2) * w
        err = (y - tg) ** 2
        return 0.5 * jnp.sum(jnp.mean(err, axis=-1, keepdims=True), axis=0, keepdims=True) * valid

    def body(h1_ref, ff_ref, g_ref, w_ref, t_ref, loss_ref, dh_ref, dff_ref, dg_ref, dw_ref):
        i = pl.program_id(0)
        valid = jnp.where(i < nctx_t, 0.0, 1.0)
        lv, pull = jax.vjp(functools.partial(loss_fn, valid), h1_ref[...], ff_ref[...].astype(F32), g_ref[...],
                           w_ref[...], t_ref[...])
        dh, dff, dg, dw, _ = pull(jnp.ones((1, 1), F32))
        dh_ref[...] = dh
        dff_ref[...] = dff.astype(dff_ref.dtype)
        lb = jnp.broadcast_to(lv, loss_ref.shape)

        @pl.when(i == 0)
        def _():
            loss_ref[...] = lb
            dg_ref[...] = dg
            dw_ref[...] = dw

        @pl.when(i > 0)
        def _():
            loss_ref[...] += lb
            dg_ref[...] += dg
            dw_ref[...] += dw

    tok = pl.BlockSpec((tl, D_MODEL), lambda i: (i, 0))
    return pl.pallas_call(
        body, grid=(nt,),
        in_specs=[tok, tok, pl.BlockSpec((8, D_MODEL), lambda i: (0, 5)), pl.BlockSpec((1, D_MODEL), lambda i: (0, 0)),
                  pl.BlockSpec((tl, D_MODEL), lambda i: (jnp.maximum(i - nctx_t, 0), 0))],
        out_specs=[pl.BlockSpec((8, 128), lambda i: (0, 0)), tok, tok, pl.BlockSpec((8, D_MODEL), lambda i: (0, 0)),
                   pl.BlockSpec((1, D_MODEL), lambda i: (0, 0))],
        out_shape=[jax.ShapeDtypeStruct((8, 128), F32), jax.ShapeDtypeStruct((t, D_MODEL), F32),
                   jax.ShapeDtypeStruct((t, D_MODEL), _MXU_DTYPE), jax.ShapeDtypeStruct((8, D_MODEL), F32),
                   jax.ShapeDtypeStruct((1, D_MODEL), F32)],
        compiler_params=_cp(("arbitrary",)), name="tail_loss")(h1, ff, mods, wf, tgt)


def _pack_w_in(w4):
    ns = w4.shape[2]
    placed = []
    for s0, s1, p0 in IN_SEGMENTS:
        for j in range(N_PLANE):
            lo, hi = max(s0, j * ns), min(s1, (j + 1) * ns)
            if lo < hi:
                placed.append((p0 + lo - s0, w4[j][:, lo - j * ns:hi - j * ns]))
    placed.sort(key=lambda e: e[0])
    pieces, end = [], 0
    for pos, piece in placed:
        assert pos == end, (pos, end)
        pieces.append(piece)
        end = pos + piece.shape[1]
    pieces.append(jnp.zeros((w4.shape[1], P_TOTAL - end), w4.dtype))
    return jnp.concatenate(pieces, axis=1)


def _unpack_w_in(g):
    ns = D_IN_PROJ // N_PLANE
    shards = []
    for j in range(N_PLANE):
        pieces = []
        for s0, s1, p0 in IN_SEGMENTS:
            lo, hi = max(s0, j * ns), min(s1, (j + 1) * ns)
            if lo < hi:
                pieces.append(g[:, p0 + lo - s0:p0 + hi - s0])
        shards.append(jnp.concatenate(pieces, axis=1))
    return jnp.stack(shards)


LATE_WEIGHTS = ['w_br_gdn', 'w_br_ssm', 'w_out', 'w_ffn_in', 'w_ffn_out']
COL_STACKED = ('ada_w', 'w_in', 'w_ffn_in')


def _from_shards(n, g):
    if n in COL_STACKED:
        return g
    if SHARD_AXIS[n] == 0:
        return g.reshape(N_PLANE * g.shape[1], g.shape[2])
    return jnp.concatenate([g[j] for j in range(N_PLANE)], axis=1)


def _to_dest_blocks(n, g):
    if g.ndim == 3:
        return g
    if SHARD_AXIS[n] == 0:
        return g.reshape(N_PLANE, g.shape[0] // N_PLANE, g.shape[1])
    sz = g.shape[1] // N_PLANE
    return jnp.stack([g[:, j * sz:(j + 1) * sz] for j in range(N_PLANE)])


def _local_step(x, c, ctx, tgt, W, late_shards=None):
    distributed = late_shards is not None
    lc, l = ctx.shape[0], x.shape[0]
    t = lc + l
    tl = 256
    assert lc == tl and l % tl == 0 and lc % CHUNK == 0
    nt, nctx_t, nctx = t // tl, lc // tl, lc // CHUNK
    act = _MXU_DTYPE
    r1 = lambda v: v.reshape(1, -1)

    xt = jnp.concatenate([ctx, x], axis=0)
    cc = jnp.concatenate([c, r1(W['c_ctx']), jnp.zeros((6, D_MODEL), F32)], axis=0)
    mods = _ada_fwd(cc, W['ada_w'], r1(W['ada_b']))

    def mod(kk):
        return Op(mods, (8, D_MODEL), lambda j, i, kk=kk: (0, kk))

    def tokop(arr, w=D_MODEL, off=0, tl_=tl):
        if arr.ndim == 3:
            return Op(arr, (2, tl_, w), lambda j, i: (0, i, off + j), load=_sum_dirs)
        return Op(arr, (tl_, w), lambda j, i: (i, off + j))

    def outop(n, dtype, w=D_MODEL, tl_=tl):
        return Op(((t, n), dtype), (tl_, w), lambda j, i: (i, j))

    def parop(arr, w, off=0):
        return Op(arr, (arr.shape[0], w), lambda j, i: (0, off + j))

    def parout(rows, n, w):
        return Op(((rows, n), F32), (rows, w), lambda j, i: (0, j))

    n1w = r1(W['norm1_w'])
    pre_fn = functools.partial(_prenorm_fn, nctx_t)
    pre_tok, pre_par = [tokop(xt)], [parop(n1w, D_MODEL), mod(1), mod(0)]
    (a,) = _tw_fwd("prenorm_fwd", pre_fn, (1, nt), pre_tok + pre_par, [outop(D_MODEL, act)])
    wp = _pack_w_in(W['w_in'])
    proj = _matmul(a, wp, 'nn', "in_proj")

    gcw, gcb = W['gdn_conv_w'], r1(W['gdn_conv_b'])
    scw, scb = W['ssm_conv_w'], r1(W['ssm_conv_b'])
    conv_parts = {}

    def conv_part(name, mode, poff, cw, cb, woff, width):
        fn = functools.partial(_conv_fn, nctx_t, mode)
        tok_ = [tokop(proj, 512, poff // 512)]
        par_ = [parop(cw, 512, woff // 512), parop(cb, 512, woff // 512)]
        conv_parts[name] = (fn, tok_, par_, width)
        (o,) = _tw_fwd("conv_" + name, fn, (width // 512, nt), tok_ + par_, [outop(width, F32, 512)])
        return o

    q = conv_part('q', 'q', P_QKV, gcw, gcb, 0, 1024)
    k = conv_part('k', 'k', P_QKV + 1024, gcw, gcb, 1024, 1024)
    v = conv_part('v', 'none', P_QKV + 2048, gcw, gcb, 2048, 1024)
    xs = conv_part('xs', 'none', P_XBC, scw, scb, 0, 2048)
    bm = conv_part('bm', 'none', P_XBC + 2048, scw, scb, 2048, 512)
    cm = conv_part('cm', 'none', P_XBC + 2560, scw, scb, 2560, 512)

    z16, z32 = jnp.zeros((16,), F32), jnp.zeros((32,), F32)
    p0 = jnp.concatenate([W['gdn_a_log'].reshape(-1), jnp.zeros((112,), F32)]).reshape(1, 128)
    p1 = jnp.concatenate([W['gdn_dt_bias'].reshape(-1), z16, W['ssm_dt_bias'].reshape(-1), z32]).reshape(1, 128)
    alog = jnp.concatenate([z32, W['ssm_a_log'].reshape(-1), z32]).reshape(1, 128)
    act_tok, act_par = [tokop(proj, 128, P_SMALL // 128)], [parop(p0, 128), parop(p1, 128)]
    (sp,) = _tw_fwd("small_act", _act_fn, (1, nt), act_tok + act_par, [outop(128, F32, 128)])

    gather_late = PlaneExchange([late_shards[n] for n in LATE_WEIGHTS], gather=True) if distributed else None
    o2, ss, tri, *late = _scan_fwd("gdn_scan_fwd", _gdn_step, [q, k, v, sp], [], 1024, GDN_HEADS, nctx,
                                   ride=gather_late, n_aux=GDN_HEADS)
    if distributed:
        W = dict(W, **{n: _from_shards(n, g) for n, g in zip(LATE_WEIGHTS, late)})
    y2, hs = _scan_fwd("ssd_scan_fwd", _ssd_step, [xs, bm, cm, sp], [alog], 2048, SSM_PAIRS, nctx)

    tlm = 128
    ntm = t // tlm
    gnw = r1(W['gdn_norm_w'])
    dl = jnp.repeat(W['ssm_d'], 64).reshape(1, 2048)
    snw = r1(W['ssm_norm_w'])
    mg_tok = [tokop(o2, 1024, 0, tlm), tokop(proj, 1024, P_ZG // 1024, tlm)]
    mg_par = [parop(gnw, 128)]
    (og,) = _tw_fwd("mix_gdn", _mixg_fn, (1, ntm), mg_tok + mg_par, [outop(1024, act, 1024, tlm)])
    ms_tok = [tokop(y2, 2048, 0, tlm), tokop(xs, 2048, 0, tlm), tokop(proj, 2048, P_ZS // 2048, tlm)]
    ms_par = [parop(dl, 2048), parop(snw, 2048)]
    (yg,) = _tw_fwd("mix_ssm", _mixs_fn, (1, ntm), ms_tok + ms_par, [outop(2048, act, 2048, tlm)])

    pg = _matmul(og, W['w_br_gdn'], 'nn', "br_gdn")
    ps = _matmul(yg, W['w_br_ssm'], 'nn', "br_ssm")
    mr_tok = [tokop(proj, 1024, P_GATE // 1024), tokop(proj, 1024, P_GATE // 1024 + 1), tokop(pg), tokop(ps)]
    (mrg,) = _tw_fwd("merge", _merge_fn, (1, nt), mr_tok, [outop(1024, act)])
    mo = _matmul(mrg, W['w_out'], 'nn', "w_out")

    n2w = r1(W['norm2_w'])
    n2_tok, n2_par = [tokop(xt), tokop(mo)], [mod(2), parop(n2w, D_MODEL), mod(4), mod(3)]
    h1, f = _tw_fwd("norm2_fwd", _norm2_fn, (1, nt), n2_tok + n2_par, [outop(1024, F32), outop(1024, act)])
    u = _matmul(f, W['w_ffn_in'], 'nn', "ffn_in")
    swb = D_FF // 2
    sw_tok = [tokop(u, swb, 0), tokop(u, swb, D_FF // swb)]
    (sw,) = _tw_fwd("swiglu", _swiglu_fn, (D_FF // swb, nt), sw_tok, [outop(D_FF, act, swb)])
    ff = _matmul(sw, W['w_ffn_out'], 'nn', "ffn_out")

    loss8, dh1, dff, dg2, dnf = _tail(h1, ff, mods, r1(W['norm_f_w']), tgt, nctx_t, tl)
    loss = loss8[0, 0]

    G = {}
    G['norm_f_w'] = dnf.reshape(-1)
    G['w_ffn_out'] = _matmul(sw, dff, 'tn', "d_ffn_out", out_dtype=_GRAD_DTYPE)
    dsw = _matmul(dff, W['w_ffn_out'], 'nt', "d_sw")
    dug, duu = _tw_bwd("swiglu_bwd", _swiglu_fn, (D_FF // swb, nt), sw_tok, [], [[tokop(dsw, swb)]],
                       [outop(D_FF, act, swb), outop(D_FF, act, swb)], [])
    du = jnp.concatenate([dug, duu], axis=1)
    G['w_ffn_in'] = _matmul(f, du, 'tn', "d_ffn_in", out_dtype=_GRAD_DTYPE, stacked_out=True)
    df = _matmul(du, W['w_ffn_in'], 'nt', "d_f")
    dxt1, dmo, dg1, dn2, dsc2, dsh2 = _tw_bwd(
        "norm2_bwd", _norm2_fn, (1, nt), n2_tok, n2_par, [[tokop(dh1)], [tokop(df)]],
        [outop(1024, F32), outop(1024, act)],
        [parout(8, 1024, 1024), parout(1, 1024, 1024), parout(8, 1024, 1024), parout(8, 1024, 1024)])
    G['norm2_w'] = dn2.reshape(-1)
    G['w_out'] = _matmul(mrg, dmo, 'tn', "d_w_out", out_dtype=_GRAD_DTYPE)
    dmrg = _matmul(dmo, W['w_out'], 'nt', "d_mrg")
    dgg, dgs, dpg, dps = _tw_bwd("merge_bwd", _merge_fn, (1, nt), mr_tok, [], [[tokop(dmrg)]],
                                 [outop(1024, act), outop(1024, act), outop(1024, act), outop(1024, act)], [])
    G['w_br_gdn'] = _matmul(og, dpg, 'tn', "d_br_gdn", out_dtype=_GRAD_DTYPE)
    G['w_br_ssm'] = _matmul(yg, dps, 'tn', "d_br_ssm", out_dtype=_GRAD_DTYPE)
    dog = _matmul(dpg, W['w_br_gdn'], 'nt', "d_og")
    dyg = _matmul(dps, W['w_br_ssm'], 'nt', "d_yg")

    do, dzg, dgnw = _tw_bwd("mix_gdn_bwd", _mixg_fn, (1, ntm), mg_tok, mg_par, [[tokop(dog, 1024, 0, tlm)]],
                            [outop(1024, F32, 1024, tlm), outop(1024, act, 1024, tlm)], [parout(1, 128, 128)])
    G['gdn_norm_w'] = dgnw.reshape(-1)
    dy, dxs_a, dzs, ddl, dsnw = _tw_bwd(
        "mix_ssm_bwd", _mixs_fn, (1, ntm), ms_tok, ms_par, [[tokop(dyg, 2048, 0, tlm)]],
        [outop(2048, F32, 2048, tlm), outop(2048, F32, 2048, tlm), outop(2048, act, 2048, tlm)],
        [parout(1, 2048, 2048), parout(1, 2048, 2048)])
    G['ssm_d'] = ddl.reshape(SSM_HEADS, 64).sum(axis=1)
    G['ssm_norm_w'] = dsnw.reshape(-1)

    dq2, dk2, dv2, dsp_g = _scan_bwd("gdn_scan_bwd", _gdn_step, [q, k, v, sp], [], ss, do, GDN_HEADS, nctx, aux=tri)
    received = {}
    scatter_late = (PlaneExchange([_to_dest_blocks(n, G.pop(n)) for n in LATE_WEIGHTS], gather=False)
                    if distributed else None)
    dxs2, dbm2, dcm2, dsp_s, dalog, *got = _scan_bwd("ssd_scan_bwd", _ssd_step, [xs, bm, cm, sp], [alog], hs, dy,
                                                     SSM_PAIRS, nctx, ride=scatter_late)
    received.update(zip(LATE_WEIGHTS, got))
    G['ssm_a_log'] = dalog[0, 32:96].reshape(2, SSM_HEADS)

    dconv_w, dconv_b, dpre = {}, {}, {}

    def conv_bwd(name, cot_ops):
        fn, tok_, par_, width = conv_parts[name]
        dpre[name], dconv_w[name], dconv_b[name] = _tw_bwd(
            "conv_" + name + "_bwd", fn, (width // 512, nt), tok_, par_, [cot_ops], [outop(width, act, 512)],
            [parout(3, width, 512), parout(1, width, 512)])

    conv_bwd('q', [tokop(dq2, 512)])
    conv_bwd('k', [tokop(dk2, 512)])
    conv_bwd('v', [tokop(dv2, 512)])
    conv_bwd('xs', [tokop(dxs2, 512), tokop(dxs_a, 512)])
    conv_bwd('bm', [tokop(dbm2, 512)])
    conv_bwd('cm', [tokop(dcm2, 512)])
    G['gdn_conv_w'] = jnp.concatenate([dconv_w['q'], dconv_w['k'], dconv_w['v']], axis=1)
    G['gdn_conv_b'] = jnp.concatenate([dconv_b['q'], dconv_b['k'], dconv_b['v']], axis=1).reshape(-1)
    G['ssm_conv_w'] = jnp.concatenate([dconv_w['xs'], dconv_w['bm'], dconv_w['cm']], axis=1)
    G['ssm_conv_b'] = jnp.concatenate([dconv_b['xs'], dconv_b['bm'], dconv_b['cm']], axis=1).reshape(-1)

    dsmall, dp0, dp1 = _tw_bwd("small_act_bwd", _act_fn, (1, nt), act_tok, act_par,
                               [[tokop(dsp_g, 128), tokop(dsp_s, 128)]], [outop(128, act, 128)],
                               [parout(1, 128, 128), parout(1, 128, 128)])
    G['gdn_a_log'] = dp0[0, 0:16].reshape(2, GDN_HEADS)
    G['gdn_dt_bias'] = dp1[0, 0:16].reshape(2, GDN_HEADS)
    G['ssm_dt_bias'] = dp1[0, 32:96].reshape(2, SSM_HEADS)

    zpad = jnp.zeros((t, P_TOTAL - P_SMALL - 128), act)
    dproj = jnp.concatenate([dpre['q'], dpre['k'], dpre['v'], dzg, dzs, dpre['xs'], dpre['bm'], dpre['cm'],
                             dgg, dgs, dsmall, zpad], axis=1)
    G['w_in'] = _unpack_w_in(_matmul(a, dproj, 'tn', "d_w_in", out_dtype=_GRAD_DTYPE))
    if distributed:
        da, (received['w_in'],) = _matmul(dproj, wp, 'nt', "d_a", ride=PlaneExchange([G.pop('w_in')], gather=False))
    else:
        da = _matmul(dproj, wp, 'nt', "d_a")

    gx_out = Op(((l, D_MODEL), F32), (tl, D_MODEL), lambda j, i: (jnp.maximum(i - nctx_t, 0), 0))
    grad_x, dn1, dsc1, dsh1 = _tw_bwd(
        "prenorm_bwd", pre_fn, (1, nt), pre_tok, pre_par, [[tokop(da)]], [gx_out],
        [parout(1, 1024, 1024), parout(8, 1024, 1024), parout(8, 1024, 1024)], tok_add=tokop(dxt1),
        sem=("arbitrary", "arbitrary"))
    G['norm1_w'] = dn1.reshape(-1)
    dmods = jnp.concatenate([dsh1, dsc1, dg1, dsh2, dsc2, dg2], axis=1)
    G['ada_w'], dab, dcc = _ada_bwd(cc, W['ada_w'], dmods)
    G['ada_b'] = dab.reshape(-1)
    G['c_ctx'] = dcc[1]
    return loss, grad_x, G, received


def _row_tile(r, c):
    for tr in (512, 256, 128, 64, 32, 16, 8):
        if r % tr == 0 and tr * c * 4 <= (1 << 20):
            return tr
    return r


def _sum4(name, rv):
    _, r, c = rv.shape
    tr = _row_tile(r, c)

    def body(r_ref, o_ref):
        o_ref[...] = ((r_ref[0].astype(F32) + r_ref[1].astype(F32)) + r_ref[2].astype(F32)) + r_ref[3].astype(F32)

    return pl.pallas_call(
        body, grid=(r // tr,), in_specs=[pl.BlockSpec((N_PLANE, tr, c), lambda i: (0, i, 0))],
        out_specs=pl.BlockSpec((tr, c), lambda i: (i, 0)), out_shape=jax.ShapeDtypeStruct((r, c), F32),
        compiler_params=_cp(("parallel",)), name=name)(rv)


def _adamw(name, w, m, v, p, q):
    r, c = w.shape
    tr = _row_tile(r, c)

    def body(w_ref, m_ref, v_ref, p_ref, q_ref, g_ref, d_ref, mo_ref, vo_ref):
        g = p_ref[...] + q_ref[...]
        mn = ADAM_B1 * m_ref[...] + (1.0 - ADAM_B1) * g
        vn = ADAM_B2 * v_ref[...] + (1.0 - ADAM_B2) * jnp.square(g)
        m_hat = mn / (1.0 - ADAM_B1 ** ADAM_STEP)
        v_hat = vn / (1.0 - ADAM_B2 ** ADAM_STEP)
        g_ref[...] = g
        d_ref[...] = -ADAM_LR * (m_hat / (jnp.sqrt(v_hat) + ADAM_EPS) + ADAM_WD * w_ref[...])
        mo_ref[...] = mn
        vo_ref[...] = vn

    spec = pl.BlockSpec((tr, c), lambda i: (i, 0))
    return pl.pallas_call(
        body, grid=(r // tr,), in_specs=[spec] * 5, out_specs=[spec] * 4,
        out_shape=[jax.ShapeDtypeStruct((r, c), F32)] * 4, compiler_params=_cp(("parallel",)), name=name)(w, m, v, p, q)


SMALL_ROWS = 24


def _pack_small(d):
    v = jnp.concatenate([d[n].reshape(-1) for n in SMALL])
    return jnp.pad(v, (0, SMALL_ROWS * 1024 - v.shape[0])).reshape(SMALL_ROWS, 1024)


def _unpack_small(buf, like):
    v = buf.reshape(-1)
    out, off = {}, 0
    for n in SMALL:
        sz = like[n].size
        out[n] = v[off:off + sz].reshape(like[n].shape)
        off += sz
    return out


def kernel(x, c, ctx, c_ctx, ada_w, ada_b, norm1_w, w_in, gdn_conv_w, gdn_conv_b, gdn_a_log, gdn_dt_bias, gdn_norm_w, ssm_conv_w, ssm_conv_b, ssm_a_log, ssm_dt_bias, ssm_d, ssm_norm_w, w_br_gdn, w_br_ssm, w_out, norm2_w, w_ffn_in, w_ffn_out, norm_f_w, loss_target, m_c_ctx, m_ada_w, m_ada_b, m_norm1_w, m_w_in, m_gdn_conv_w, m_gdn_conv_b, m_gdn_a_log, m_gdn_dt_bias, m_gdn_norm_w, m_ssm_conv_w, m_ssm_conv_b, m_ssm_a_log, m_ssm_dt_bias, m_ssm_d, m_ssm_norm_w, m_w_br_gdn, m_w_br_ssm, m_w_out, m_norm2_w, m_w_ffn_in, m_w_ffn_out, m_norm_f_w, v_c_ctx, v_ada_w, v_ada_b, v_norm1_w, v_w_in, v_gdn_conv_w, v_gdn_conv_b, v_gdn_a_log, v_gdn_dt_bias, v_gdn_norm_w, v_ssm_conv_w, v_ssm_conv_b, v_ssm_a_log, v_ssm_dt_bias, v_ssm_d, v_ssm_norm_w, v_w_br_gdn, v_w_br_ssm, v_w_out, v_norm2_w, v_w_ffn_in, v_w_ffn_out, v_norm_f_w):
    args = dict(locals())
    wl = {n: args[n] for n in WEIGHTS}
    ml = {n: args['m_' + n] for n in WEIGHTS}
    vl = {n: args['v_' + n] for n in WEIGHTS}

    def nodepth(n, a):
        return a if n in ('c_ctx', 'norm_f_w') else a[0]

    shard = {n: nodepth(n, wl[n]).astype(_MXU_DTYPE if n in MXU_WEIGHTS else F32) for n in SHARDED}
    first = [n for n in SHARDED if n not in LATE_WEIGHTS]
    W = {n: nodepth(n, wl[n]) for n in SMALL}
    for n, g in zip(first, _gather_two_level("all_gather_plane", [shard[n] for n in first])):
        W[n] = _from_shards(n, g)

    loss_local, grad_x, G, received = _local_step(x[0], c, ctx[0], loss_target[0], W,
                                                  late_shards={n: shard[n] for n in LATE_WEIGHTS})
    loss = lax.psum(loss_local, ("x", "y", "c"))

    small_g = _pack_small(G)
    last = [n for n in SHARDED if n not in received]
    send = [_to_dest_blocks(n, G[n]) for n in last] + [jnp.broadcast_to(small_g[None], (N_PLANE,) + small_g.shape)]
    received.update(zip(last + ['small'], _plane_exchange("scatter_plane", send, gather=False)))
    names = SHARDED + ['small']
    plane_sum = [_sum4("sum4_" + n, received[n]) for n in names]
    other = _swap_sibling(plane_sum)

    wd = {n: nodepth(n, wl[n]) for n in WEIGHTS}
    md = {n: nodepth(n, ml[n]) for n in WEIGHTS}
    vd = {n: nodepth(n, vl[n]) for n in WEIGHTS}
    res = {}
    for n, p, q in zip(names, plane_sum, other):
        if n == 'small':
            outs = _adamw("adamw_small", _pack_small(wd), _pack_small(md), _pack_small(vd), p, q)
            unpacked = [_unpack_small(o, wd) for o in outs]
            for sn in SMALL:
                res[sn] = [u[sn].reshape(wl[sn].shape) for u in unpacked]
        else:
            outs = _adamw("adamw_" + n, wd[n], md[n], vd[n], p, q)
            res[n] = [o.reshape(wl[n].shape) for o in outs]
    flat = [res[n][kind] for kind in range(4) for n in WEIGHTS]
    return (loss, grad_x[None], *flat)
```

```python
import functools

import jax
import jax.numpy as jnp
from jax import lax
from jax.experimental import pallas as pl
from jax.experimental.pallas import tpu as pltpu

F32 = jnp.float32
HI = lax.Precision.HIGHEST
_MXU_DTYPE = jnp.bfloat16
_SCAN_DTYPE = jnp.bfloat16
_GRAD_DTYPE = jnp.bfloat16

D_MODEL = 1024
EPS = 1e-6
CHUNK = 64
GRID_W = 64
GDN_HEADS = 8
GDN_DK = 128
SSM_HEADS = 32
SSM_PAIRS = 16
D_FF = 2816
D_IN_PROJ = 11360
N_PLANE = 4
P_QKV, P_ZG, P_ZS, P_XBC, P_GATE, P_SMALL, P_TOTAL = 0, 3072, 4096, 6144, 9216, 11264, 11776
IN_SEGMENTS = [(0, 3072, P_QKV), (3072, 4096, P_ZG), (4096, 4112, P_SMALL), (4112, 4128, P_SMALL + 16),
               (4128, 6176, P_ZS), (6176, 9248, P_XBC), (9248, 9312, P_SMALL + 32), (9312, 11360, P_GATE)]
ADAM_LR, ADAM_B1, ADAM_B2, ADAM_EPS, ADAM_WD, ADAM_STEP = 0.001, 0.9, 0.999, 1e-08, 0.01, 10

VMEM_LIMIT = 48 * 1024 * 1024

WEIGHTS = ['c_ctx', 'ada_w', 'ada_b', 'norm1_w', 'w_in', 'gdn_conv_w', 'gdn_conv_b', 'gdn_a_log', 'gdn_dt_bias',
           'gdn_norm_w', 'ssm_conv_w', 'ssm_conv_b', 'ssm_a_log', 'ssm_dt_bias', 'ssm_d', 'ssm_norm_w', 'w_br_gdn',
           'w_br_ssm', 'w_out', 'norm2_w', 'w_ffn_in', 'w_ffn_out', 'norm_f_w']
SHARD_AXIS = {'ada_w': 1, 'w_in': 1, 'gdn_conv_w': 1, 'ssm_conv_w': 1, 'w_br_gdn': 0, 'w_br_ssm': 0, 'w_out': 0,
              'w_ffn_in': 1, 'w_ffn_out': 0}
SHARDED = [n for n in WEIGHTS if n in SHARD_AXIS]
SMALL = [n for n in WEIGHTS if n not in SHARD_AXIS]
MXU_WEIGHTS = ['ada_w', 'w_in', 'w_br_gdn', 'w_br_ssm', 'w_out', 'w_ffn_in', 'w_ffn_out']


def _cp(sem):
    return pltpu.CompilerParams(dimension_semantics=sem, vmem_limit_bytes=VMEM_LIMIT)


def _pick(n, cands):
    for c in cands:
        if n % c == 0:
            return c
    raise ValueError(f"no tile for {n}")


HBM = pl.BlockSpec(memory_space=pl.ANY)
MESH = pl.DeviceIdType.MESH


def _plane_peers():
    x, y, c = lax.axis_index("x"), lax.axis_index("y"), lax.axis_index("c")
    return (x, y, c), [(1 - x, y, c), (x, 1 - y, c), (1 - x, 1 - y, c)]


class PlaneExchange:
    def __init__(self, arrs, gather):
        self.arrs, self.gather, self.n = list(arrs), gather, len(arrs)
        self.in_specs = [HBM] * self.n
        self.out_specs = [HBM] * self.n
        self.out_shape = [jax.ShapeDtypeStruct((N_PLANE,) + a.shape if gather else a.shape, a.dtype) for a in self.arrs]
        self.scratch = [pltpu.SemaphoreType.DMA((3 * self.n,)), pltpu.SemaphoreType.DMA((3 * self.n,)),
                        pltpu.SemaphoreType.DMA((self.n,))]

    def _copies(self, ins, outs, sems):
        send_sems, recv_sems, local_sems = sems
        (x, y, c), peers = _plane_peers()
        me = 2 * x + y
        copies = []
        for ti in range(self.n):
            src = ins[ti] if self.gather else ins[ti].at[me]
            copies.append(pltpu.make_async_copy(src, outs[ti].at[me], local_sems.at[ti]))
            for kk, (px, py, pc) in enumerate(peers):
                src = ins[ti] if self.gather else ins[ti].at[2 * px + py]
                copies.append(pltpu.make_async_remote_copy(
                    src_ref=src, dst_ref=outs[ti].at[me], send_sem=send_sems.at[3 * ti + kk],
                    recv_sem=recv_sems.at[3 * ti + kk], device_id=(px, py, pc), device_id_type=MESH))
        return copies

    def start(self, ins, outs, sems):
        for cp in self._copies(ins, outs, sems):
            cp.start()

    def wait(self, ins, outs, sems):
        for cp in self._copies(ins, outs, sems):
            cp.wait()


def _plane_exchange(name, arrs, gather):
    ex = PlaneExchange(arrs, gather)
    n = ex.n

    def body(*refs):
        ins, outs, sems = refs[:n], refs[n:2 * n], refs[2 * n:]
        ex.start(ins, outs, sems)
        ex.wait(ins, outs, sems)

    return pl.pallas_call(body, in_specs=ex.in_specs, out_specs=ex.out_specs, out_shape=ex.out_shape,
                          scratch_shapes=ex.scratch, name=name)(*arrs)


def _gather_two_level(name, arrs):
    n = len(arrs)
    split = [a.shape[0] % 32 == 0 for a in arrs]

    def body(*refs):
        ins, outs = refs[:n], refs[n:2 * n]
        ici_send, ici_recv, d2d_send, d2d_recv, local_sems = refs[2 * n:]
        (x, y, c), peers = _plane_peers()
        me = 2 * x + y

        def part(ref, ti):
            if not split[ti]:
                return ref
            rows = arrs[ti].shape[0] // 2
            return ref.at[pl.ds(c * rows, rows)]

        local = [pltpu.make_async_copy(ins[ti], outs[ti].at[me], local_sems.at[ti]) for ti in range(n)]
        ici = [pltpu.make_async_remote_copy(
            src_ref=part(ins[ti], ti), dst_ref=part(outs[ti].at[me], ti), send_sem=ici_send.at[3 * ti + kk],
            recv_sem=ici_recv.at[3 * ti + kk], device_id=peer, device_id_type=MESH)
            for ti in range(n) for kk, peer in enumerate(peers)]
        for cp in local + ici:
            cp.start()
        d2d = []
        for ti in range(n):
            for kk, (px, py, pc) in enumerate(peers):
                ici[3 * ti + kk].wait_recv()
                if split[ti]:
                    piece = part(outs[ti].at[2 * px + py], ti)
                    cp = pltpu.make_async_remote_copy(
                        src_ref=piece, dst_ref=piece, send_sem=d2d_send.at[3 * ti + kk],
                        recv_sem=d2d_recv.at[3 * ti + kk], device_id=(x, y, 1 - c), device_id_type=MESH)
                    cp.start()
                    d2d.append(cp)
        for cp in ici:
            cp.wait_send()
        for cp in d2d:
            cp.wait()
        for cp in local:
            cp.wait()

    return pl.pallas_call(
        body, in_specs=[HBM] * n, out_specs=[HBM] * n,
        out_shape=[jax.ShapeDtypeStruct((N_PLANE,) + a.shape, a.dtype) for a in arrs],
        scratch_shapes=[pltpu.SemaphoreType.DMA((3 * n,))] * 4 + [pltpu.SemaphoreType.DMA((n,))], name=name)(*arrs)


def _swap_sibling(arrs):
    n = len(arrs)

    def body(*refs):
        ins, outs, send_sems, recv_sems = refs[:n], refs[n:2 * n], refs[2 * n], refs[2 * n + 1]
        x, y, c = lax.axis_index("x"), lax.axis_index("y"), lax.axis_index("c")
        copies = [pltpu.make_async_remote_copy(src_ref=ins[ti], dst_ref=outs[ti], send_sem=send_sems.at[ti],
                                               recv_sem=recv_sems.at[ti], device_id=(x, y, 1 - c), device_id_type=MESH)
                  for ti in range(n)]
        for cp in copies:
            cp.start()
        for cp in copies:
            cp.wait()

    return pl.pallas_call(
        body, in_specs=[HBM] * n, out_specs=[HBM] * n, out_shape=[jax.ShapeDtypeStruct(a.shape, a.dtype) for a in arrs],
        scratch_shapes=[pltpu.SemaphoreType.DMA((n,)), pltpu.SemaphoreType.DMA((n,))], name="swap_sibling")(*arrs)


MATMUL_VMEM_BUDGET = 36 * 1024 * 1024
TILE_CANDIDATES = (2944, 2816, 1408, 1024, 768, 512, 256, 128)
STEP_COST_BYTES = 1 << 20


def _matmul_tiles(m, n, k, ab, bb, ob, tn_fixed=None, tk_fixed=None):
    def cands(dim, whole_up_to):
        out = [c for c in TILE_CANDIDATES if dim % c == 0]
        if dim <= whole_up_to and dim not in out:
            out.append(dim)
        return out

    best = None
    for tm in cands(m, 3072):
        for tn in ([tn_fixed] if tn_fixed else cands(n, 3072)):
            for tk in ([tk_fixed] if tk_fixed else cands(k, 2048)):
                gi, gj, gk = m // tm, n // tn, k // tk
                vmem = 2 * (tm * tk * ab + tk * tn * bb + tm * tn * ob) + (tm * tn * 4 if gk > 1 else 0)
                if vmem > MATMUL_VMEM_BUDGET:
                    continue
                a_reads = 1 if gk == 1 else gj
                b_reads = 1 if (gk == 1 and gj == 1) else gi
                cost = (m * k * ab * a_reads + k * n * bb * b_reads + m * n * ob + gi * gj * gk * STEP_COST_BYTES)
                if best is None or cost < best[0]:
                    best = (cost, tm, tn, tk)
    assert best is not None, (m, n, k)
    return best[1:]


def _matmul(a, b, form, name, out_dtype=F32, stacked_out=False, ride=None):
    dims = {'nn': (((1,), (0,)), ((), ())), 'nt': (((1,), (1,)), ((), ())), 'tn': (((0,), (0,)), ((), ()))}[form]
    stacked_b = b.ndim == 3
    ns = b.shape[2] if stacked_b else None
    if form == 'nn':
        m, k = a.shape
        n = b.shape[0] * ns if stacked_b else b.shape[1]
    elif form == 'nt':
        m, k = a.shape
        n = b.shape[1] if stacked_b else b.shape[0]
    else:
        k, m = a.shape
        n = b.shape[1]
    ob = jnp.dtype(out_dtype).itemsize
    tm, tn, tk = _matmul_tiles(m, n, k, a.dtype.itemsize, b.dtype.itemsize, ob,
                               tn_fixed=(n // N_PLANE if stacked_out else ns if (stacked_b and form == 'nn') else None),
                               tk_fixed=(ns if (stacked_b and form == 'nt') else None))
    nk = k // tk
    grid = (m // tm, n // tn, nk)
    n_ride = ride.n if ride is not None else 0

    def body(*refs):
        a_ref, b_ref = refs[0], refs[1]
        ride_in = refs[2:2 + n_ride]
        o_ref = refs[2 + n_ride]
        ride_out = refs[3 + n_ride:3 + 2 * n_ride]
        acc_ref = refs[3 + 2 * n_ride]
        sems = refs[4 + 2 * n_ride:]
        i, j, kk = pl.program_id(0), pl.program_id(1), pl.program_id(2)
        if ride is not None:
            @pl.when((i == 0) & (j == 0) & (kk == 0))
            def _():
                ride.start(ride_in, ride_out, sems)

        def put(val):
            if stacked_out:
                o_ref[0] = val.astype(o_ref.dtype)
            else:
                o_ref[...] = val.astype(o_ref.dtype)

        bv = b_ref[0] if stacked_b else b_ref[...]
        part = lax.dot_general(a_ref[...].astype(_MXU_DTYPE), bv.astype(_MXU_DTYPE), dims, preferred_element_type=F32)
        if nk == 1:
            put(part)
        else:
            @pl.when(kk == 0)
            def _():
                acc_ref[...] = part

            @pl.when(kk > 0)
            def _():
                acc_ref[...] += part

            @pl.when(kk == nk - 1)
            def _():
                put(acc_ref[...])

        if ride is not None:
            @pl.when((i == grid[0] - 1) & (j == grid[1] - 1) & (kk == nk - 1))
            def _():
                ride.wait(ride_in, ride_out, sems)

    if form == 'nn':
        a_spec = pl.BlockSpec((tm, tk), lambda i, j, kk: (i, kk))
        b_spec = (pl.BlockSpec((1, tk, tn), lambda i, j, kk: (j, kk, 0)) if stacked_b
                  else pl.BlockSpec((tk, tn), lambda i, j, kk: (kk, j)))
    elif form == 'nt':
        a_spec = pl.BlockSpec((tm, tk), lambda i, j, kk: (i, kk))
        b_spec = (pl.BlockSpec((1, tn, tk), lambda i, j, kk: (kk, j, 0)) if stacked_b
                  else pl.BlockSpec((tn, tk), lambda i, j, kk: (j, kk)))
    else:
        a_spec = pl.BlockSpec((tk, tm), lambda i, j, kk: (kk, i))
        b_spec = pl.BlockSpec((tk, tn), lambda i, j, kk: (kk, j))
    if stacked_out:
        o_spec = pl.BlockSpec((1, tm, tn), lambda i, j, kk: (j, i, 0))
        o_shape = jax.ShapeDtypeStruct((N_PLANE, m, tn), out_dtype)
    else:
        o_spec = pl.BlockSpec((tm, tn), lambda i, j, kk: (i, j))
        o_shape = jax.ShapeDtypeStruct((m, n), out_dtype)
    acc_shape = (tm, tn) if nk > 1 else (8, 128)
    if ride is None:
        return pl.pallas_call(
            body, grid=grid, in_specs=[a_spec, b_spec], out_specs=o_spec, out_shape=o_shape,
            scratch_shapes=[pltpu.VMEM(acc_shape, F32)],
            compiler_params=_cp(("parallel", "parallel", "arbitrary")), name=name)(a, b)
    res = pl.pallas_call(
        body, grid=grid, in_specs=[a_spec, b_spec] + ride.in_specs, out_specs=[o_spec] + ride.out_specs,
        out_shape=[o_shape] + ride.out_shape, scratch_shapes=[pltpu.VMEM(acc_shape, F32)] + ride.scratch,
        compiler_params=_cp(("arbitrary", "arbitrary", "arbitrary")), name=name)(a, b, *ride.arrs)
    return res[0], res[1:]


class Op:
    def __init__(self, arr, bs, im, load=None):
        self.arr, self.bs, self.im = arr, bs, im
        self.arrs = list(arr) if isinstance(arr, list) else [arr]
        self.load = load or (lambda r: r[...].astype(F32))

    def spec(self):
        return pl.BlockSpec(self.bs, self.im)

    def value(self, it):
        return functools.reduce(lambda u, w: u + w, [self.load(next(it)) for _ in self.arrs])


def _op_specs(ops):
    return [op.spec() for op in ops for _ in op.arrs]


def _op_arrays(ops):
    return [a for op in ops for a in op.arrs]


def _sum_dirs(r):
    return r[0].astype(F32) + r[1].astype(F32)


def _tw_fwd(name, fn, grid, ins, outs):
    def body(*refs):
        info = (pl.program_id(0), pl.program_id(1))
        it = iter(refs)
        vals = [op.value(it) for op in ins]
        res = fn(info, *vals)
        for r, v in zip(it, res):
            r[...] = v.astype(r.dtype)

    return pl.pallas_call(
        body, grid=grid, in_specs=_op_specs(ins), out_specs=[op.spec() for op in outs],
        out_shape=[jax.ShapeDtypeStruct(*op.arr) for op in outs],
        compiler_params=_cp(("parallel", "arbitrary")), name=name)(*_op_arrays(ins))


def _tw_bwd(name, fn, grid, tok, par, cots, tok_out, par_out, tok_add=None, sem=("parallel", "arbitrary")):
    n_tok = len(tok)
    flat_cots = [op for group in cots for op in group]
    extra = [tok_add] if tok_add is not None else []
    out_ops = list(tok_out) + list(par_out)

    def body(*refs):
        info = (pl.program_id(0), pl.program_id(1))
        it = iter(refs)
        tok_v = [op.value(it) for op in tok]
        par_v = [op.value(it) for op in par]
        cot_v = [functools.reduce(lambda u, w: u + w, [op.value(it) for op in group]) for group in cots]
        add_v = [op.value(it) for op in extra]
        _, pull = jax.vjp(lambda *a: fn(info, *a), *tok_v, *par_v)
        grads = pull(tuple(cot_v))
        for i in range(n_tok):
            r = next(it)
            g = grads[i] + add_v[0] if (i == 0 and add_v) else grads[i]
            r[...] = g.astype(r.dtype)
        first = pl.program_id(1) == 0
        for i in range(len(par)):
            r = next(it)
            g = grads[n_tok + i]

            @pl.when(first)
            def _(r=r, g=g):
                r[...] = g

            @pl.when(jnp.logical_not(first))
            def _(r=r, g=g):
                r[...] += g

    ops = tok + par + flat_cots + extra
    return pl.pallas_call(
        body, grid=grid, in_specs=_op_specs(ops), out_specs=[op.spec() for op in out_ops],
        out_shape=[jax.ShapeDtypeStruct(*op.arr) for op in out_ops],
        compiler_params=_cp(sem), name=name)(*_op_arrays(ops))


def _silu(x):
    return x * jax.nn.sigmoid(x)


def _rms(x):
    return x * lax.rsqrt(jnp.mean(x * x, axis=-1, keepdims=True) + EPS)


@functools.partial(jax.custom_vjp, nondiff_argnums=(1,))
def _shift_rows(x, k):
    return pltpu.roll(x, k % x.shape[0], 0)


def _shift_rows_fwd(x, k):
    return _shift_rows(x, k), None


def _shift_rows_bwd(k, _, g):
    return (_shift_rows(g, -k),)


_shift_rows.defvjp(_shift_rows_fwd, _shift_rows_bwd)


def _prenorm_fn(nctx_t, info, x, w, sc8, sh8):
    is_ctx = info[1] < nctx_t
    sc = jnp.where(is_ctx, sc8[1:2], sc8[0:1])
    sh = jnp.where(is_ctx, sh8[1:2], sh8[0:1])
    return (_rms(x) * w * (1.0 + sc) + sh,)


def _conv_fn(nctx_t, mode, info, x, w, b):
    n, c = x.shape
    is_ctx = info[1] < nctx_t
    idx = lax.broadcasted_iota(jnp.int32, (n, 1), 0)
    rr = jnp.where(is_ctx, idx, idx % GRID_W)
    first = rr == 0
    last = rr == jnp.where(is_ctx, n - 1, GRID_W - 1)
    prev = jnp.where(first, 0.0, _shift_rows(x, 1))
    nxt = jnp.where(last, 0.0, _shift_rows(x, -1))
    y = b + prev * w[0:1] + x * w[1:2] + nxt * w[2:3]
    y = _silu(y)
    if mode == 'none':
        return (y,)
    scale = GDN_DK ** -0.5 if mode == 'q' else 1.0
    outs = []
    for h in range(c // 128):
        yh = y[:, h * 128:(h + 1) * 128]
        outs.append(yh * lax.rsqrt(jnp.sum(yh * yh, axis=-1, keepdims=True) + EPS) * scale)
    return (jnp.concatenate(outs, axis=1),)


def _act_fn(info, x, p0, p1):
    lane = lax.broadcasted_iota(jnp.int32, x.shape, 1)
    sp = jax.nn.softplus(x + p1)
    g = -jnp.exp(p0) * sp
    bt = jax.nn.sigmoid(x)
    return (jnp.where(lane < 16, g, jnp.where(lane < 32, bt, jnp.where(lane < 96, sp, 0.0))),)


def _mixg_fn(info, o, zg, gnw):
    outs = []
    for h in range(GDN_HEADS):
        outs.append(_rms(o[:, h * 128:(h + 1) * 128]) * gnw)
    return (jnp.concatenate(outs, axis=1) * _silu(zg),)


def _mixs_fn(info, y, xs, zs, dl, snw):
    yy = (y + dl * xs) * _silu(zs)
    outs = []
    for g in range(4):
        outs.append(_rms(yy[:, g * 512:(g + 1) * 512]))
    return (jnp.concatenate(outs, axis=1) * snw,)


def _merge_fn(info, gg, gs, pg, ps):
    return (jax.nn.sigmoid(gg) * pg + jax.nn.sigmoid(gs) * ps,)


def _norm2_fn(info, xt, mo, g8, w, sc8, sh8):
    h1 = xt + g8[0:1] * mo
    return (h1, _rms(h1) * w * (1.0 + sc8[0:1]) + sh8[0:1])


def _swiglu_fn(info, ug, uu):
    return (_silu(ug) * uu,)


_NN = (((1,), (0,)), ((), ()))
_NT = (((1,), (1,)), ((), ()))
_TN = (((0,), (0,)), ((), ()))


def _mmh(a, b):
    return lax.dot_general(a, b, _NN, precision=HI, preferred_element_type=F32)


def _dot1(a, b, dims):
    return lax.dot_general(a.astype(_SCAN_DTYPE), b.astype(_SCAN_DTYPE), dims, preferred_element_type=F32)


def _mm(a, b):
    return _dot1(a, b, _NN)


def _mm_nt(a, b):
    return _dot1(a, b, _NT)


def _mm_tn(a, b):
    return _dot1(a, b, _TN)


def _split2(a):
    hi = a.astype(_SCAN_DTYPE)
    return hi, (a - hi.astype(F32)).astype(_SCAN_DTYPE)


def _dot3(a, b, dims):
    ah, al = _split2(a)
    bh, bl = _split2(b)
    d = lambda u, w: lax.dot_general(u, w, dims, preferred_element_type=F32)
    return d(ah, bh) + (d(ah, bl) + d(al, bh))


def _order_masks(d):
    i = lax.broadcasted_iota(jnp.int32, (CHUNK, CHUNK), 0)
    j = lax.broadcasted_iota(jnp.int32, (CHUNK, CHUNK), 1)
    s = jnp.where(d == 0, 1, -1) * (i - j)
    return (s >= 0).astype(F32), (s > 0).astype(F32)


@jax.custom_vjp
def _unit_tri_inv(mats):
    i = lax.broadcasted_iota(jnp.int32, (CHUNK, CHUNK), 0)
    j = lax.broadcasted_iota(jnp.int32, (CHUNK, CHUNK), 1)
    eye = (i == j).astype(F32)
    ps = [-a for a in mats]
    ts = [eye + p for p in ps]
    for _ in range(5):
        ps = [_dot3(p, p, _NN) for p in ps]
        ts = [t + _dot3(t, p, _NN) for t, p in zip(ts, ps)]
    return tuple(ts)


def _uti_fwd(mats):
    ts = _unit_tri_inv(mats)
    return ts, ts


def _uti_bwd(ts, gs):
    inner = [_dot3(g, t, _NT) for g, t in zip(gs, ts)]
    return (tuple(-_dot3(t, u, _TN) for t, u in zip(ts, inner)),)


_unit_tri_inv.defvjp(_uti_fwd, _uti_bwd)


@jax.custom_vjp
def _unit_tri_inv_given(mats, ts):
    return ts


def _utig_fwd(mats, ts):
    return ts, ts


def _utig_bwd(ts, gs):
    return _uti_bwd(ts, gs)[0], tuple(jnp.zeros_like(t) for t in ts)


_unit_tri_inv_given.defvjp(_utig_fwd, _utig_bwd)


def _lane_col(blk, lane_idx):
    lane = lax.broadcasted_iota(jnp.int32, blk.shape, 1)
    return jnp.sum(jnp.where(lane == lane_idx, blk, 0.0), axis=1, keepdims=True)


def _decay_mat(cum, incl):
    cb = jnp.broadcast_to(cum, (CHUNK, CHUNK))
    return jnp.exp(jnp.minimum(cb - cb.T, 0.0)) * incl


def _gdn_chunks(dirs, streams, states, aux=None, want_aux=False):
    ns = len(dirs)
    masks = [_order_masks(d) for d in dirs]
    cum = [_mmh(masks[i][0], streams[i][3]) for i in range(ns)]
    tot = [jnp.sum(streams[i][3], axis=0, keepdims=True) for i in range(ns)]
    units = [(i, h) for i in range(ns) for h in range(GDN_HEADS)]
    us = range(len(units))
    cat = jnp.concatenate
    qs = [streams[i][0][:, h * 128:(h + 1) * 128] for i, h in units]
    ks = [streams[i][1][:, h * 128:(h + 1) * 128] for i, h in units]
    vs = [streams[i][2][:, h * 128:(h + 1) * 128] for i, h in units]
    gcum = [_lane_col(cum[i], dirs[i] * GDN_HEADS + h) for i, h in units]
    glast = [_lane_col(tot[i], dirs[i] * GDN_HEADS + h) for i, h in units]
    beta = [_lane_col(streams[i][3], 16 + dirs[i] * GDN_HEADS + h) for i, h in units]
    decay = [_decay_mat(gcum[u], masks[units[u][0]][0]) for u in us]
    egc = [jnp.exp(gcum[u]) for u in us]
    kb = [ks[u] * beta[u] for u in us]
    kq = [_mm_nt(cat([kb[u], qs[u]], axis=0), ks[u]) for u in us]
    mats = tuple(kq[u][:CHUNK] * decay[u] * masks[units[u][0]][1] for u in us)
    ts = _unit_tri_inv(mats) if aux is None else _unit_tri_inv_given(mats, tuple(aux))
    wu = [_mm(ts[u], cat([kb[u] * egc[u], vs[u] * beta[u]], axis=1)) for u in us]
    ws = [_mm(cat([wu[u][:, :128], qs[u] * egc[u]], axis=0), states[u]) for u in us]
    vn = [wu[u][:, 128:] - ws[u][:CHUNK] for u in us]
    outs = [ws[u][CHUNK:] + _mm(kq[u][CHUNK:] * decay[u], vn[u]) for u in us]
    new_states = [states[u] * jnp.exp(glast[u]) + _mm_tn(ks[u] * jnp.exp(glast[u] - gcum[u]), vn[u]) for u in us]
    per_stream = [cat(outs[i * GDN_HEADS:(i + 1) * GDN_HEADS], axis=1) for i in range(ns)]
    return (*per_stream, *new_states, *(ts if want_aux else ()))


def _gdn_step(d, q, k, v, spb, *states, aux=None, want_aux=False):
    return _gdn_chunks([d], [(q, k, v, spb)], states, aux, want_aux)


def _gdn_step2(q0, k0, v0, sp0, q1, k1, v1, sp1, *states, aux=None, want_aux=False):
    return _gdn_chunks([0, 1], [(q0, k0, v0, sp0), (q1, k1, v1, sp1)], states, aux, want_aux)


def _split3(a):
    a1 = a.astype(_SCAN_DTYPE)
    r = a - a1.astype(F32)
    a2 = r.astype(_SCAN_DTYPE)
    return a1, a2, (r - a2.astype(F32)).astype(_SCAN_DTYPE)


def _exact_dot(a, e, dims, split_lhs):
    parts = _split3(a if split_lhs else e)
    d = (lambda u: lax.dot_general(u, e, dims, preferred_element_type=F32)) if split_lhs else \
        (lambda u: lax.dot_general(a, u, dims, preferred_element_type=F32))
    return d(parts[0]) + (d(parts[1]) + d(parts[2]))


@jax.custom_vjp
def _spread(a, e):
    return _exact_dot(a, e, _NN, True)


def _spread_fwd(a, e):
    return _spread(a, e), e


def _spread_bwd(e, g):
    return _exact_dot(g, e, _NT, True), jnp.zeros_like(e)


_spread.defvjp(_spread_fwd, _spread_bwd)


@jax.custom_vjp
def _colsum_bcast(z):
    return _exact_dot(jnp.ones((z.shape[0], z.shape[0]), _SCAN_DTYPE), z, _NN, False)


def _colsum_fwd(z):
    return _colsum_bcast(z), None


def _colsum_bwd(_, g):
    return (_colsum_bcast(g),)


_colsum_bcast.defvjp(_colsum_fwd, _colsum_bwd)


def _ssd_step(d, x, bm, cm, spb, alog, *states):
    wdt = x.shape[1]
    incl, _ = _order_masks(d)
    lane1 = lax.broadcasted_iota(jnp.int32, (1, 128), 1)
    lo_lane = 32 + d * SSM_HEADS
    a_vec = jnp.where(lane1 >= lo_lane, jnp.where(lane1 < lo_lane + SSM_HEADS, -jnp.exp(alog), 0.0), 0.0)
    adt = spb * a_vec
    acum = _mmh(incl, adt)
    alast = jnp.sum(adt, axis=0, keepdims=True)
    e = (lax.broadcasted_iota(jnp.int32, (128, wdt), 0)
         == lo_lane + lax.broadcasted_iota(jnp.int32, (128, wdt), 1) // 64).astype(_SCAN_DTYPE)
    dt2 = _spread(spb, e)
    ac2 = _spread(acum, e)
    al2 = _spread(jnp.broadcast_to(alast, (8, 128)), e)[0:1]
    ci = lax.broadcasted_iota(jnp.int32, (CHUNK, wdt), 0)
    pos = lax.broadcasted_iota(jnp.int32, (CHUNK, wdt), 1) % 64
    row = _colsum_bcast(jnp.where(ci == pos, ac2, 0.0))
    incl_t = (jnp.where(d == 0, 1, -1) * (ci - pos) >= 0).astype(F32)
    seg = jnp.exp(jnp.minimum(ac2 - row, 0.0)) * incl_t
    xdt = x * dt2
    gam = jnp.exp(ac2)
    xe = xdt * jnp.exp(al2 - ac2)
    low = lax.broadcasted_iota(jnp.int32, (CHUNK, 128), 1) < 64
    row_low = lax.broadcasted_iota(jnp.int32, (128, 1), 0) < 64
    ps = range(SSM_PAIRS)
    sl = [slice(p * 128, (p + 1) * 128) for p in ps]
    bg = [bm[:, g * 128:(g + 1) * 128] for g in range(4)]
    cg = [cm[:, g * 128:(g + 1) * 128] for g in range(4)]
    cb2 = [_mm_nt(cg[g], jnp.concatenate([bg[g], bg[g]], axis=0)) for g in range(4)]
    xd = [jnp.concatenate([jnp.where(low, xdt[:, sl[p]], 0.0), jnp.where(low, 0.0, xdt[:, sl[p]])], axis=0) for p in ps]
    yd = [_mm(cb2[p // 4] * seg[:, sl[p]], xd[p]) for p in ps]
    yo = [_mm_nt(cg[p // 4], states[p]) for p in ps]
    ys = [yd[p] + gam[:, sl[p]] * yo[p] for p in ps]
    new = [_mm_tn(xe[:, sl[p]], bg[p // 4]) for p in ps]
    al0 = [_lane_col(alast, lo_lane + 2 * p) for p in ps]
    al1 = [_lane_col(alast, lo_lane + 2 * p + 1) for p in ps]
    new_states = [states[p] * jnp.exp(jnp.where(row_low, al0[p], al1[p])) + new[p] for p in ps]
    return (jnp.concatenate(ys, axis=1), *new_states)


def _chunk_of(d, p, nctx, nc):
    return jnp.where(d == 0, p, jnp.where(p < nctx, nctx - 1 - p, nctx + nc - 1 - p))


class _NoRide:
    n, arrs, in_specs, out_specs, out_shape, scratch = 0, [], [], [], [], []


def _scan_fwd(name, step, toks, pars, out_width, n_state, nctx, ride=None, n_aux=0):
    t = toks[0].shape[0]
    nc = t // CHUNK
    n_tok, n_par = len(toks), len(pars)
    rd = ride if ride is not None else _NoRide
    kw = dict(want_aux=True) if n_aux else {}

    def body(*refs):
        it = iter(refs)
        tok_refs = [next(it) for _ in range(n_tok)]
        par_refs = [next(it) for _ in range(n_par)]
        ride_in = [next(it) for _ in range(rd.n)]
        o_ref, ss_ref = next(it), next(it)
        aux_ref = next(it) if n_aux else None
        ride_out = [next(it) for _ in range(rd.n)]
        s_scr = next(it)
        sems = list(it)
        d, p = pl.program_id(0), pl.program_id(1)
        if ride is not None:
            @pl.when((d == 0) & (p == 0))
            def _():
                ride.start(ride_in, ride_out, sems)

        @pl.when(p == 0)
        def _():
            s_scr[...] = jnp.zeros(s_scr.shape, F32)

        ss_ref[0, 0] = s_scr[...]
        res = step(d, *[r[...] for r in tok_refs], *[r[...] for r in par_refs], *[s_scr[h] for h in range(n_state)],
                   **kw)
        o_ref[0] = res[0]
        for h in range(n_state):
            s_scr[h] = res[1 + h]
        for i in range(n_aux):
            aux_ref[0, 0, i] = res[1 + n_state + i]
        if ride is not None:
            @pl.when((d == 1) & (p == nc - 1))
            def _():
                ride.wait(ride_in, ride_out, sems)

    ch = lambda d, p: _chunk_of(d, p, nctx, nc)
    in_specs = [pl.BlockSpec((CHUNK, a.shape[1]), lambda d, p: (ch(d, p), 0)) for a in toks]
    in_specs += [pl.BlockSpec(a.shape, lambda d, p: (0, 0)) for a in pars]
    aux_specs = [pl.BlockSpec((1, 1, n_aux, CHUNK, CHUNK), lambda d, p: (d, p, 0, 0, 0))] if n_aux else []
    aux_shape = [jax.ShapeDtypeStruct((2, nc, n_aux, CHUNK, CHUNK), F32)] if n_aux else []
    return pl.pallas_call(
        body, grid=(2, nc), in_specs=in_specs + rd.in_specs,
        out_specs=[pl.BlockSpec((1, CHUNK, out_width), lambda d, p: (d, ch(d, p), 0)),
                   pl.BlockSpec((1, 1, n_state, 128, 128), lambda d, p: (d, p, 0, 0, 0))] + aux_specs + rd.out_specs,
        out_shape=[jax.ShapeDtypeStruct((2, t, out_width), F32),
                   jax.ShapeDtypeStruct((2, nc, n_state, 128, 128), F32)] + aux_shape + rd.out_shape,
        scratch_shapes=[pltpu.VMEM((n_state, 128, 128), F32)] + rd.scratch,
        compiler_params=_cp(("arbitrary", "arbitrary")), name=name)(*toks, *pars, *rd.arrs)


def _scan_bwd(name, step, toks, pars, ss, dout, n_state, nctx, ride=None, aux=None):
    t = toks[0].shape[0]
    nc = t // CHUNK
    n_tok, n_par = len(toks), len(pars)
    rd = ride if ride is not None else _NoRide
    n_aux = aux.shape[2] if aux is not None else 0

    def body(*refs):
        it = iter(refs)
        tok_refs = [next(it) for _ in range(n_tok)]
        par_refs = [next(it) for _ in range(n_par)]
        ss_ref, do_ref = next(it), next(it)
        aux_ref = next(it) if n_aux else None
        ride_in = [next(it) for _ in range(rd.n)]
        dtok_refs = [next(it) for _ in range(n_tok)]
        dpar_refs = [next(it) for _ in range(n_par)]
        ride_out = [next(it) for _ in range(rd.n)]
        ds_scr = next(it)
        sems = list(it)
        d, pr = pl.program_id(0), pl.program_id(1)
        if ride is not None:
            @pl.when((d == 0) & (pr == 0))
            def _():
                ride.start(ride_in, ride_out, sems)

            @pl.when((d == 1) & (pr == nc - 1))
            def _():
                ride.wait(ride_in, ride_out, sems)

        @pl.when(pr == 0)
        def _():
            ds_scr[...] = jnp.zeros(ds_scr.shape, F32)

        kw = dict(aux=[aux_ref[0, 0, i] for i in range(n_aux)]) if n_aux else {}
        _, pull = jax.vjp(functools.partial(step, d, **kw), *[r[...] for r in tok_refs], *[r[...] for r in par_refs],
                          *[ss_ref[0, 0, h] for h in range(n_state)])
        grads = pull((do_ref[...], *[ds_scr[h] for h in range(n_state)]))
        for r, g in zip(dtok_refs, grads[:n_tok]):
            r[0] = g
        for h in range(n_state):
            ds_scr[h] = grads[n_tok + n_par + h]
        first = (d == 0) & (pr == 0)
        for r, g in zip(dpar_refs, grads[n_tok:n_tok + n_par]):
            @pl.when(first)
            def _(r=r, g=g):
                r[...] = g

            @pl.when(jnp.logical_not(first))
            def _(r=r, g=g):
                r[...] += g

    ch = lambda d, pr: _chunk_of(d, nc - 1 - pr, nctx, nc)
    in_specs = [pl.BlockSpec((CHUNK, a.shape[1]), lambda d, pr: (ch(d, pr), 0)) for a in toks]
    in_specs += [pl.BlockSpec(a.shape, lambda d, pr: (0, 0)) for a in pars]
    in_specs += [pl.BlockSpec((1, 1, n_state, 128, 128), lambda d, pr: (d, nc - 1 - pr, 0, 0, 0)),
                 pl.BlockSpec((CHUNK, dout.shape[1]), lambda d, pr: (ch(d, pr), 0))]
    if n_aux:
        in_specs += [pl.BlockSpec((1, 1, n_aux, CHUNK, CHUNK), lambda d, pr: (d, nc - 1 - pr, 0, 0, 0))]
    out_specs = [pl.BlockSpec((1, CHUNK, a.shape[1]), lambda d, pr: (d, ch(d, pr), 0)) for a in toks]
    out_specs += [pl.BlockSpec(a.shape, lambda d, pr: (0, 0)) for a in pars]
    out_shape = [jax.ShapeDtypeStruct((2, t, a.shape[1]), F32) for a in toks]
    out_shape += [jax.ShapeDtypeStruct(a.shape, F32) for a in pars]
    return pl.pallas_call(
        body, grid=(2, nc), in_specs=in_specs + rd.in_specs, out_specs=out_specs + rd.out_specs,
        out_shape=out_shape + rd.out_shape, scratch_shapes=[pltpu.VMEM((n_state, 128, 128), F32)] + rd.scratch,
        compiler_params=_cp(("arbitrary", "arbitrary")), name=name)(
            *toks, *pars, ss, dout, *([aux] if n_aux else []), *rd.arrs)


def _scan2_fwd(name, step2, toks, out_width, n_state, nctx, n_aux, ride=None):
    t = toks[0].shape[0]
    nc = t // CHUNK
    n_tok = len(toks)
    rd = ride if ride is not None else _NoRide

    def body(*refs):
        it = iter(refs)
        tok_refs = [next(it) for _ in range(2 * n_tok)]
        ride_in = [next(it) for _ in range(rd.n)]
        o_refs = [next(it), next(it)]
        ss_ref, aux_ref = next(it), next(it)
        ride_out = [next(it) for _ in range(rd.n)]
        s_scr = next(it)
        sems = list(it)
        p = pl.program_id(0)
        if ride is not None:
            @pl.when(p == 0)
            def _():
                ride.start(ride_in, ride_out, sems)

        @pl.when(p == 0)
        def _():
            s_scr[...] = jnp.zeros(s_scr.shape, F32)

        for d in range(2):
            ss_ref[d, 0] = s_scr[d * n_state:(d + 1) * n_state]
        res = step2(*[r[...] for r in tok_refs], *[s_scr[u] for u in range(2 * n_state)], want_aux=True)
        for d in range(2):
            o_refs[d][...] = res[d]
            for i in range(n_aux):
                aux_ref[d, 0, i] = res[2 + 2 * n_state + d * n_aux + i]
        for u in range(2 * n_state):
            s_scr[u] = res[2 + u]
        if ride is not None:
            @pl.when(p == nc - 1)
            def _():
                ride.wait(ride_in, ride_out, sems)

    def tok_spec(a, d):
        return pl.BlockSpec((CHUNK, a.shape[1]), lambda p: (_chunk_of(d, p, nctx, nc), 0))

    return pl.pallas_call(
        body, grid=(nc,), in_specs=[tok_spec(a, d) for d in range(2) for a in toks] + rd.in_specs,
        out_specs=[pl.BlockSpec((CHUNK, out_width), lambda p: (_chunk_of(0, p, nctx, nc), 0)),
                   pl.BlockSpec((CHUNK, out_width), lambda p: (_chunk_of(1, p, nctx, nc), 0)),
                   pl.BlockSpec((2, 1, n_state, 128, 128), lambda p: (0, p, 0, 0, 0)),
                   pl.BlockSpec((2, 1, n_aux, CHUNK, CHUNK), lambda p: (0, p, 0, 0, 0))] + rd.out_specs,
        out_shape=[jax.ShapeDtypeStruct((t, out_width), F32), jax.ShapeDtypeStruct((t, out_width), F32),
                   jax.ShapeDtypeStruct((2, nc, n_state, 128, 128), F32),
                   jax.ShapeDtypeStruct((2, nc, n_aux, CHUNK, CHUNK), F32)] + rd.out_shape,
        scratch_shapes=[pltpu.VMEM((2 * n_state, 128, 128), F32)] + rd.scratch,
        compiler_params=_cp(("arbitrary",)), name=name)(*toks, *toks, *rd.arrs)


ADA_TN = 512


def _ada_fwd(cc, ada_w4, ada_b):
    per = ada_w4.shape[2] // ADA_TN
    n = N_PLANE * ada_w4.shape[2]

    def body(c_ref, w_ref, b_ref, o_ref):
        s = _silu(c_ref[...]).astype(_MXU_DTYPE)
        o_ref[...] = jnp.dot(s, w_ref[0].astype(_MXU_DTYPE), preferred_element_type=F32) + b_ref[...]

    return pl.pallas_call(
        body, grid=(n // ADA_TN,),
        in_specs=[pl.BlockSpec((8, D_MODEL), lambda j: (0, 0)),
                  pl.BlockSpec((1, D_MODEL, ADA_TN), lambda j: (j // per, 0, j % per)),
                  pl.BlockSpec((1, ADA_TN), lambda j: (0, j))],
        out_specs=pl.BlockSpec((8, ADA_TN), lambda j: (0, j)), out_shape=jax.ShapeDtypeStruct((8, n), F32),
        compiler_params=_cp(("parallel",)), name="ada_fwd")(cc, ada_w4, ada_b)


def _ada_bwd(cc, ada_w4, dmods):
    per = ada_w4.shape[2] // ADA_TN
    n = N_PLANE * ada_w4.shape[2]
    nj = n // ADA_TN

    def body(c_ref, w_ref, g_ref, dw_ref, db_ref, dc_ref):
        j = pl.program_id(0)
        g = g_ref[...]
        row = lax.broadcasted_iota(jnp.int32, g.shape, 0)
        g = jnp.where(row < 2, g, 0.0)
        s, pull = jax.vjp(_silu, c_ref[...])
        dw_ref[0] = lax.dot_general(s.astype(_MXU_DTYPE), g.astype(_MXU_DTYPE), _TN,
                                    preferred_element_type=F32).astype(dw_ref.dtype)
        db_ref[...] = jnp.sum(g, axis=0, keepdims=True)
        ds = lax.dot_general(g.astype(_MXU_DTYPE), w_ref[0].astype(_MXU_DTYPE), _NT, preferred_element_type=F32)

        @pl.when(j == 0)
        def _():
            dc_ref[...] = ds

        @pl.when(j > 0)
        def _():
            dc_ref[...] += ds

        @pl.when(j == nj - 1)
        def _():
            dc_ref[...] = pull(dc_ref[...])[0]

    wspec = pl.BlockSpec((1, D_MODEL, ADA_TN), lambda j: (j // per, 0, j % per))
    return pl.pallas_call(
        body, grid=(nj,),
        in_specs=[pl.BlockSpec((8, D_MODEL), lambda j: (0, 0)), wspec, pl.BlockSpec((8, ADA_TN), lambda j: (0, j))],
        out_specs=[wspec, pl.BlockSpec((1, ADA_TN), lambda j: (0, j)), pl.BlockSpec((8, D_MODEL), lambda j: (0, 0))],
        out_shape=[jax.ShapeDtypeStruct(ada_w4.shape, _GRAD_DTYPE), jax.ShapeDtypeStruct((1, n), F32),
                   jax.ShapeDtypeStruct((8, D_MODEL), F32)],
        compiler_params=_cp(("arbitrary",)), name="ada_bwd")(cc, ada_w4, dmods)


def _tail(h1, ff, mods, wf, tgt, nctx_t, tl):
    t = h1.shape[0]
    nt = t // tl

    def loss_fn(valid, h1v, ffv, g8, w, tg):
        h2 = h1v + g8[0:1] * ffv
        y = _rms(h2) * w
        err = (y - tg) ** 2
        return 0.5 * jnp.sum(jnp.mean(err, axis=-1, keepdims=True), axis=0, keepdims=True) * valid

    def body(h1_ref, ff_ref, g_ref, w_ref, t_ref, loss_ref, dh_ref, dff_ref, dg_ref, dw_ref):
        i = pl.program_id(0)
        valid = jnp.where(i < nctx_t, 0.0, 1.0)
        lv, pull = jax.vjp(functools.partial(loss_fn, valid), h1_ref[...], ff_ref[...].astype(F32), g_ref[...],
                           w_ref[...], t_ref[...])
        dh, dff, dg, dw, _ = pull(jnp.ones((1, 1), F32))
        dh_ref[...] = dh
        dff_ref[...] = dff.astype(dff_ref.dtype)
        lb = jnp.broadcast_to(lv, loss_ref.shape)

        @pl.when(i == 0)
        def _():
            loss_ref[...] = lb
            dg_ref[...] = dg
            dw_ref[...] = dw

        @pl.when(i > 0)
        def _():
            loss_ref[...] += lb
            dg_ref[...] += dg
            dw_ref[...] += dw

    tok = pl.BlockSpec((tl, D_MODEL), lambda i: (i, 0))
    return pl.pallas_call(
        body, grid=(nt,),
        in_specs=[tok, tok, pl.BlockSpec((8, D_MODEL), lambda i: (0, 5)), pl.BlockSpec((1, D_MODEL), lambda i: (0, 0)),
                  pl.BlockSpec((tl, D_MODEL), lambda i: (jnp.maximum(i - nctx_t, 0), 0))],
        out_specs=[pl.BlockSpec((8, 128), lambda i: (0, 0)), tok, tok, pl.BlockSpec((8, D_MODEL), lambda i: (0, 0)),
                   pl.BlockSpec((1, D_MODEL), lambda i: (0, 0))],
        out_shape=[jax.ShapeDtypeStruct((8, 128), F32), jax.ShapeDtypeStruct((t, D_MODEL), F32),
                   jax.ShapeDtypeStruct((t, D_MODEL), _MXU_DTYPE), jax.ShapeDtypeStruct((8, D_MODEL), F32),
                   jax.ShapeDtypeStruct((1, D_MODEL), F32)],
        compiler_params=_cp(("arbitrary",)), name="tail_loss")(h1, ff, mods, wf, tgt)


def _pack_w_in(w4):
    ns = w4.shape[2]
    placed = []
    for s0, s1, p0 in IN_SEGMENTS:
        for j in range(N_PLANE):
            lo, hi = max(s0, j * ns), min(s1, (j + 1) * ns)
            if lo < hi:
                placed.append((p0 + lo - s0, w4[j][:, lo - j * ns:hi - j * ns]))
    placed.sort(key=lambda e: e[0])
    pieces, end = [], 0
    for pos, piece in placed:
        assert pos == end, (pos, end)
        pieces.append(piece)
        end = pos + piece.shape[1]
    pieces.append(jnp.zeros((w4.shape[1], P_TOTAL - end), w4.dtype))
    return jnp.concatenate(pieces, axis=1)


def _unpack_w_in(g):
    ns = D_IN_PROJ // N_PLANE
    shards = []
    for j in range(N_PLANE):
        pieces = []
        for s0, s1, p0 in IN_SEGMENTS:
            lo, hi = max(s0, j * ns), min(s1, (j + 1) * ns)
            if lo < hi:
                pieces.append(g[:, p0 + lo - s0:p0 + hi - s0])
        shards.append(jnp.concatenate(pieces, axis=1))
    return jnp.stack(shards)


LATE_WEIGHTS = ['w_br_gdn', 'w_br_ssm', 'w_out', 'w_ffn_in', 'w_ffn_out']
COL_STACKED = ('ada_w', 'w_in', 'w_ffn_in')


def _from_shards(n, g):
    if n in COL_STACKED:
        return g
    if SHARD_AXIS[n] == 0:
        return g.reshape(N_PLANE * g.shape[1], g.shape[2])
    return jnp.concatenate([g[j] for j in range(N_PLANE)], axis=1)


def _to_dest_blocks(n, g):
    if g.ndim == 3:
        return g
    if SHARD_AXIS[n] == 0:
        return g.reshape(N_PLANE, g.shape[0] // N_PLANE, g.shape[1])
    sz = g.shape[1] // N_PLANE
    return jnp.stack([g[:, j * sz:(j + 1) * sz] for j in range(N_PLANE)])


def _local_step(x, c, ctx, tgt, W, late_shards=None):
    distributed = late_shards is not None
    lc, l = ctx.shape[0], x.shape[0]
    t = lc + l
    tl = 256
    assert lc == tl and l % tl == 0 and lc % CHUNK == 0
    nt, nctx_t, nctx = t // tl, lc // tl, lc // CHUNK
    act = _MXU_DTYPE
    r1 = lambda v: v.reshape(1, -1)

    xt = jnp.concatenate([ctx, x], axis=0)
    cc = jnp.concatenate([c, r1(W['c_ctx']), jnp.zeros((6, D_MODEL), F32)], axis=0)
    mods = _ada_fwd(cc, W['ada_w'], r1(W['ada_b']))

    def mod(kk):
        return Op(mods, (8, D_MODEL), lambda j, i, kk=kk: (0, kk))

    def tokop(arr, w=D_MODEL, off=0, tl_=tl):
        if not isinstance(arr, list) and arr.ndim == 3:
            return Op(arr, (2, tl_, w), lambda j, i: (0, i, off + j), load=_sum_dirs)
        return Op(arr, (tl_, w), lambda j, i: (i, off + j))

    def outop(n, dtype, w=D_MODEL, tl_=tl):
        return Op(((t, n), dtype), (tl_, w), lambda j, i: (i, j))

    def parop(arr, w, off=0):
        return Op(arr, (arr.shape[0], w), lambda j, i: (0, off + j))

    def parout(rows, n, w):
        return Op(((rows, n), F32), (rows, w), lambda j, i: (0, j))

    n1w = r1(W['norm1_w'])
    pre_fn = functools.partial(_prenorm_fn, nctx_t)
    pre_tok, pre_par = [tokop(xt)], [parop(n1w, D_MODEL), mod(1), mod(0)]
    (a,) = _tw_fwd("prenorm_fwd", pre_fn, (1, nt), pre_tok + pre_par, [outop(D_MODEL, act)])
    wp = _pack_w_in(W['w_in'])
    proj = _matmul(a, wp, 'nn', "in_proj")

    gcw, gcb = W['gdn_conv_w'], r1(W['gdn_conv_b'])
    scw, scb = W['ssm_conv_w'], r1(W['ssm_conv_b'])
    conv_parts = {}

    def conv_part(name, mode, poff, cw, cb, woff, width):
        fn = functools.partial(_conv_fn, nctx_t, mode)
        tok_ = [tokop(proj, 512, poff // 512)]
        par_ = [parop(cw, 512, woff // 512), parop(cb, 512, woff // 512)]
        conv_parts[name] = (fn, tok_, par_, width)
        (o,) = _tw_fwd("conv_" + name, fn, (width // 512, nt), tok_ + par_, [outop(width, F32, 512)])
        return o

    q = conv_part('q', 'q', P_QKV, gcw, gcb, 0, 1024)
    k = conv_part('k', 'k', P_QKV + 1024, gcw, gcb, 1024, 1024)
    v = conv_part('v', 'none', P_QKV + 2048, gcw, gcb, 2048, 1024)
    xs = conv_part('xs', 'none', P_XBC, scw, scb, 0, 2048)
    bm = conv_part('bm', 'none', P_XBC + 2048, scw, scb, 2048, 512)
    cm = conv_part('cm', 'none', P_XBC + 2560, scw, scb, 2560, 512)

    z16, z32 = jnp.zeros((16,), F32), jnp.zeros((32,), F32)
    p0 = jnp.concatenate([W['gdn_a_log'].reshape(-1), jnp.zeros((112,), F32)]).reshape(1, 128)
    p1 = jnp.concatenate([W['gdn_dt_bias'].reshape(-1), z16, W['ssm_dt_bias'].reshape(-1), z32]).reshape(1, 128)
    alog = jnp.concatenate([z32, W['ssm_a_log'].reshape(-1), z32]).reshape(1, 128)
    act_tok, act_par = [tokop(proj, 128, P_SMALL // 128)], [parop(p0, 128), parop(p1, 128)]
    (sp,) = _tw_fwd("small_act", _act_fn, (1, nt), act_tok + act_par, [outop(128, F32, 128)])

    gather_late = PlaneExchange([late_shards[n] for n in LATE_WEIGHTS], gather=True) if distributed else None
    o_f, o_b, ss, tri, *late = _scan2_fwd("gdn_scan_fwd", _gdn_step2, [q, k, v, sp], 1024, GDN_HEADS, nctx,
                                          GDN_HEADS, ride=gather_late)
    if distributed:
        W = dict(W, **{n: _from_shards(n, g) for n, g in zip(LATE_WEIGHTS, late)})
    y2, hs = _scan_fwd("ssd_scan_fwd", _ssd_step, [xs, bm, cm, sp], [alog], 2048, SSM_PAIRS, nctx)

    tlm = 128
    ntm = t // tlm
    gnw = r1(W['gdn_norm_w'])
    dl = jnp.repeat(W['ssm_d'], 64).reshape(1, 2048)
    snw = r1(W['ssm_norm_w'])
    mg_tok = [tokop([o_f, o_b], 1024, 0, tlm), tokop(proj, 1024, P_ZG // 1024, tlm)]
    mg_par = [parop(gnw, 128)]
    (og,) = _tw_fwd("mix_gdn", _mixg_fn, (1, ntm), mg_tok + mg_par, [outop(1024, act, 1024, tlm)])
    ms_tok = [tokop(y2, 2048, 0, tlm), tokop(xs, 2048, 0, tlm), tokop(proj, 2048, P_ZS // 2048, tlm)]
    ms_par = [parop(dl, 2048), parop(snw, 2048)]
    (yg,) = _tw_fwd("mix_ssm", _mixs_fn, (1, ntm), ms_tok + ms_par, [outop(2048, act, 2048, tlm)])

    pg = _matmul(og, W['w_br_gdn'], 'nn', "br_gdn")
    ps = _matmul(yg, W['w_br_ssm'], 'nn', "br_ssm")
    mr_tok = [tokop(proj, 1024, P_GATE // 1024), tokop(proj, 1024, P_GATE // 1024 + 1), tokop(pg), tokop(ps)]
    (mrg,) = _tw_fwd("merge", _merge_fn, (1, nt), mr_tok, [outop(1024, act)])
    mo = _matmul(mrg, W['w_out'], 'nn', "w_out")

    n2w = r1(W['norm2_w'])
    n2_tok, n2_par = [tokop(xt), tokop(mo)], [mod(2), parop(n2w, D_MODEL), mod(4), mod(3)]
    h1, f = _tw_fwd("norm2_fwd", _norm2_fn, (1, nt), n2_tok + n2_par, [outop(1024, F32), outop(1024, act)])
    u = _matmul(f, W['w_ffn_in'], 'nn', "ffn_in")
    swb = D_FF // 2
    sw_tok = [tokop(u, swb, 0), tokop(u, swb, D_FF // swb)]
    (sw,) = _tw_fwd("swiglu", _swiglu_fn, (D_FF // swb, nt), sw_tok, [outop(D_FF, act, swb)])
    ff = _matmul(sw, W['w_ffn_out'], 'nn', "ffn_out")

    loss8, dh1, dff, dg2, dnf = _tail(h1, ff, mods, r1(W['norm_f_w']), tgt, nctx_t, tl)
    loss = loss8[0, 0]

    G = {}
    G['norm_f_w'] = dnf.reshape(-1)
    G['w_ffn_out'] = _matmul(sw, dff, 'tn', "d_ffn_out", out_dtype=_GRAD_DTYPE)
    dsw = _matmul(dff, W['w_ffn_out'], 'nt', "d_sw")
    dug, duu = _tw_bwd("swiglu_bwd", _swiglu_fn, (D_FF // swb, nt), sw_tok, [], [[tokop(dsw, swb)]],
                       [outop(D_FF, act, swb), outop(D_FF, act, swb)], [])
    du = jnp.concatenate([dug, duu], axis=1)
    G['w_ffn_in'] = _matmul(f, du, 'tn', "d_ffn_in", out_dtype=_GRAD_DTYPE, stacked_out=True)
    df = _matmul(du, W['w_ffn_in'], 'nt', "d_f")
    dxt1, dmo, dg1, dn2, dsc2, dsh2 = _tw_bwd(
        "norm2_bwd", _norm2_fn, (1, nt), n2_tok, n2_par, [[tokop(dh1)], [tokop(df)]],
        [outop(1024, F32), outop(1024, act)],
        [parout(8, 1024, 1024), parout(1, 1024, 1024), parout(8, 1024, 1024), parout(8, 1024, 1024)])
    G['norm2_w'] = dn2.reshape(-1)
    G['w_out'] = _matmul(mrg, dmo, 'tn', "d_w_out", out_dtype=_GRAD_DTYPE)
    dmrg = _matmul(dmo, W['w_out'], 'nt', "d_mrg")
    dgg, dgs, dpg, dps = _tw_bwd("merge_bwd", _merge_fn, (1, nt), mr_tok, [], [[tokop(dmrg)]],
                                 [outop(1024, act), outop(1024, act), outop(1024, act), outop(1024, act)], [])
    G['w_br_gdn'] = _matmul(og, dpg, 'tn', "d_br_gdn", out_dtype=_GRAD_DTYPE)
    G['w_br_ssm'] = _matmul(yg, dps, 'tn', "d_br_ssm", out_dtype=_GRAD_DTYPE)
    dog = _matmul(dpg, W['w_br_gdn'], 'nt', "d_og")
    dyg = _matmul(dps, W['w_br_ssm'], 'nt', "d_yg")

    do, dzg, dgnw = _tw_bwd("mix_gdn_bwd", _mixg_fn, (1, ntm), mg_tok, mg_par, [[tokop(dog, 1024, 0, tlm)]],
                            [outop(1024, F32, 1024, tlm), outop(1024, act, 1024, tlm)], [parout(1, 128, 128)])
    G['gdn_norm_w'] = dgnw.reshape(-1)
    dy, dxs_a, dzs, ddl, dsnw = _tw_bwd(
        "mix_ssm_bwd", _mixs_fn, (1, ntm), ms_tok, ms_par, [[tokop(dyg, 2048, 0, tlm)]],
        [outop(2048, F32, 2048, tlm), outop(2048, F32, 2048, tlm), outop(2048, act, 2048, tlm)],
        [parout(1, 2048, 2048), parout(1, 2048, 2048)])
    G['ssm_d'] = ddl.reshape(SSM_HEADS, 64).sum(axis=1)
    G['ssm_norm_w'] = dsnw.reshape(-1)

    dq2, dk2, dv2, dsp_g = _scan_bwd("gdn_scan_bwd", _gdn_step, [q, k, v, sp], [], ss, do, GDN_HEADS, nctx, aux=tri)
    received = {}
    scatter_late = (PlaneExchange([_to_dest_blocks(n, G.pop(n)) for n in LATE_WEIGHTS], gather=False)
                    if distributed else None)
    dxs2, dbm2, dcm2, dsp_s, dalog, *got = _scan_bwd("ssd_scan_bwd", _ssd_step, [xs, bm, cm, sp], [alog], hs, dy,
                                                     SSM_PAIRS, nctx, ride=scatter_late)
    received.update(zip(LATE_WEIGHTS, got))
    G['ssm_a_log'] = dalog[0, 32:96].reshape(2, SSM_HEADS)

    dconv_w, dconv_b, dpre = {}, {}, {}

    def conv_bwd(name, cot_ops):
        fn, tok_, par_, width = conv_parts[name]
        dpre[name], dconv_w[name], dconv_b[name] = _tw_bwd(
            "conv_" + name + "_bwd", fn, (width // 512, nt), tok_, par_, [cot_ops], [outop(width, act, 512)],
            [parout(3, width, 512), parout(1, width, 512)])

    conv_bwd('q', [tokop(dq2, 512)])
    conv_bwd('k', [tokop(dk2, 512)])
    conv_bwd('v', [tokop(dv2, 512)])
    conv_bwd('xs', [tokop(dxs2, 512), tokop(dxs_a, 512)])
    conv_bwd('bm', [tokop(dbm2, 512)])
    conv_bwd('cm', [tokop(dcm2, 512)])
    G['gdn_conv_w'] = jnp.concatenate([dconv_w['q'], dconv_w['k'], dconv_w['v']], axis=1)
    G['gdn_conv_b'] = jnp.concatenate([dconv_b['q'], dconv_b['k'], dconv_b['v']], axis=1).reshape(-1)
    G['ssm_conv_w'] = jnp.concatenate([dconv_w['xs'], dconv_w['bm'], dconv_w['cm']], axis=1)
    G['ssm_conv_b'] = jnp.concatenate([dconv_b['xs'], dconv_b['bm'], dconv_b['cm']], axis=1).reshape(-1)

    dsmall, dp0, dp1 = _tw_bwd("small_act_bwd", _act_fn, (1, nt), act_tok, act_par,
                               [[tokop(dsp_g, 128), tokop(dsp_s, 128)]], [outop(128, act, 128)],
                               [parout(1, 128, 128), parout(1, 128, 128)])
    G['gdn_a_log'] = dp0[0, 0:16].reshape(2, GDN_HEADS)
    G['gdn_dt_bias'] = dp1[0, 0:16].reshape(2, GDN_HEADS)
    G['ssm_dt_bias'] = dp1[0, 32:96].reshape(2, SSM_HEADS)

    zpad = jnp.zeros((t, P_TOTAL - P_SMALL - 128), act)
    dproj = jnp.concatenate([dpre['q'], dpre['k'], dpre['v'], dzg, dzs, dpre['xs'], dpre['bm'], dpre['cm'],
                             dgg, dgs, dsmall, zpad], axis=1)
    G['w_in'] = _unpack_w_in(_matmul(a, dproj, 'tn', "d_w_in", out_dtype=_GRAD_DTYPE))
    if distributed:
        da, (received['w_in'],) = _matmul(dproj, wp, 'nt', "d_a", ride=PlaneExchange([G.pop('w_in')], gather=False))
    else:
        da = _matmul(dproj, wp, 'nt', "d_a")

    gx_out = Op(((l, D_MODEL), F32), (tl, D_MODEL), lambda j, i: (jnp.maximum(i - nctx_t, 0), 0))
    grad_x, dn1, dsc1, dsh1 = _tw_bwd(
        "prenorm_bwd", pre_fn, (1, nt), pre_tok, pre_par, [[tokop(da)]], [gx_out],
        [parout(1, 1024, 1024), parout(8, 1024, 1024), parout(8, 1024, 1024)], tok_add=tokop(dxt1),
        sem=("arbitrary", "arbitrary"))
    G['norm1_w'] = dn1.reshape(-1)
    dmods = jnp.concatenate([dsh1, dsc1, dg1, dsh2, dsc2, dg2], axis=1)
    G['ada_w'], dab, dcc = _ada_bwd(cc, W['ada_w'], dmods)
    G['ada_b'] = dab.reshape(-1)
    G['c_ctx'] = dcc[1]
    return loss, grad_x, G, received


def _row_tile(r, c):
    for tr in (512, 256, 128, 64, 32, 16, 8):
        if r % tr == 0 and tr * c * 4 <= (1 << 20):
            return tr
    return r


def _sum4(name, rv):
    _, r, c = rv.shape
    tr = _row_tile(r, c)

    def body(r_ref, o_ref):
        o_ref[...] = ((r_ref[0].astype(F32) + r_ref[1].astype(F32)) + r_ref[2].astype(F32)) + r_ref[3].astype(F32)

    return pl.pallas_call(
        body, grid=(r // tr,), in_specs=[pl.BlockSpec((N_PLANE, tr, c), lambda i: (0, i, 0))],
        out_specs=pl.BlockSpec((tr, c), lambda i: (i, 0)), out_shape=jax.ShapeDtypeStruct((r, c), F32),
        compiler_params=_cp(("parallel",)), name=name)(rv)


def _adamw(name, w, m, v, p, q):
    r, c = w.shape
    tr = _row_tile(r, c)

    def body(w_ref, m_ref, v_ref, p_ref, q_ref, g_ref, d_ref, mo_ref, vo_ref):
        g = p_ref[...] + q_ref[...]
        mn = ADAM_B1 * m_ref[...] + (1.0 - ADAM_B1) * g
        vn = ADAM_B2 * v_ref[...] + (1.0 - ADAM_B2) * jnp.square(g)
        m_hat = mn / (1.0 - ADAM_B1 ** ADAM_STEP)
        v_hat = vn / (1.0 - ADAM_B2 ** ADAM_STEP)
        g_ref[...] = g
        d_ref[...] = -ADAM_LR * (m_hat / (jnp.sqrt(v_hat) + ADAM_EPS) + ADAM_WD * w_ref[...])
        mo_ref[...] = mn
        vo_ref[...] = vn

    spec = pl.BlockSpec((tr, c), lambda i: (i, 0))
    return pl.pallas_call(
        body, grid=(r // tr,), in_specs=[spec] * 5, out_specs=[spec] * 4,
        out_shape=[jax.ShapeDtypeStruct((r, c), F32)] * 4, compiler_params=_cp(("parallel",)), name=name)(w, m, v, p, q)


SMALL_ROWS = 24


def _pack_small(d):
    v = jnp.concatenate([d[n].reshape(-1) for n in SMALL])
    return jnp.pad(v, (0, SMALL_ROWS * 1024 - v.shape[0])).reshape(SMALL_ROWS, 1024)


def _unpack_small(buf, like):
    v = buf.reshape(-1)
    out, off = {}, 0
    for n in SMALL:
        sz = like[n].size
        out[n] = v[off:off + sz].reshape(like[n].shape)
        off += sz
    return out


def kernel(x, c, ctx, c_ctx, ada_w, ada_b, norm1_w, w_in, gdn_conv_w, gdn_conv_b, gdn_a_log, gdn_dt_bias, gdn_norm_w, ssm_conv_w, ssm_conv_b, ssm_a_log, ssm_dt_bias, ssm_d, ssm_norm_w, w_br_gdn, w_br_ssm, w_out, norm2_w, w_ffn_in, w_ffn_out, norm_f_w, loss_target, m_c_ctx, m_ada_w, m_ada_b, m_norm1_w, m_w_in, m_gdn_conv_w, m_gdn_conv_b, m_gdn_a_log, m_gdn_dt_bias, m_gdn_norm_w, m_ssm_conv_w, m_ssm_conv_b, m_ssm_a_log, m_ssm_dt_bias, m_ssm_d, m_ssm_norm_w, m_w_br_gdn, m_w_br_ssm, m_w_out, m_norm2_w, m_w_ffn_in, m_w_ffn_out, m_norm_f_w, v_c_ctx, v_ada_w, v_ada_b, v_norm1_w, v_w_in, v_gdn_conv_w, v_gdn_conv_b, v_gdn_a_log, v_gdn_dt_bias, v_gdn_norm_w, v_ssm_conv_w, v_ssm_conv_b, v_ssm_a_log, v_ssm_dt_bias, v_ssm_d, v_ssm_norm_w, v_w_br_gdn, v_w_br_ssm, v_w_out, v_norm2_w, v_w_ffn_in, v_w_ffn_out, v_norm_f_w):
    args = dict(locals())
    wl = {n: args[n] for n in WEIGHTS}
    ml = {n: args['m_' + n] for n in WEIGHTS}
    vl = {n: args['v_' + n] for n in WEIGHTS}

    def nodepth(n, a):
        return a if n in ('c_ctx', 'norm_f_w') else a[0]

    shard = {n: nodepth(n, wl[n]).astype(_MXU_DTYPE if n in MXU_WEIGHTS else F32) for n in SHARDED}
    first = [n for n in SHARDED if n not in LATE_WEIGHTS]
    W = {n: nodepth(n, wl[n]) for n in SMALL}
    for n, g in zip(first, _gather_two_level("all_gather_plane", [shard[n] for n in first])):
        W[n] = _from_shards(n, g)

    loss_local, grad_x, G, received = _local_step(x[0], c, ctx[0], loss_target[0], W,
                                                  late_shards={n: shard[n] for n in LATE_WEIGHTS})
    loss = lax.psum(loss_local, ("x", "y", "c"))

    small_g = _pack_small(G)
    last = [n for n in SHARDED if n not in received]
    send = [_to_dest_blocks(n, G[n]) for n in last] + [jnp.broadcast_to(small_g[None], (N_PLANE,) + small_g.shape)]
    received.update(zip(last + ['small'], _plane_exchange("scatter_plane", send, gather=False)))
    names = SHARDED + ['small']
    plane_sum = [_sum4("sum4_" + n, received[n]) for n in names]
    other = _swap_sibling(plane_sum)

    wd = {n: nodepth(n, wl[n]) for n in WEIGHTS}
    md = {n: nodepth(n, ml[n]) for n in WEIGHTS}
    vd = {n: nodepth(n, vl[n]) for n in WEIGHTS}
    res = {}
    for n, p, q in zip(names, plane_sum, other):
        if n == 'small':
            outs = _adamw("adamw_small", _pack_small(wd), _pack_small(md), _pack_small(vd), p, q)
            unpacked = [_unpack_small(o, wd) for o in outs]
            for sn in SMALL:
                res[sn] = [u[sn].reshape(wl[sn].shape) for u in unpacked]
        else:
            outs = _adamw("adamw_" + n, wd[n], md[n], vd[n], p, q)
            res[n] = [o.reshape(wl[n].shape) for o in outs]
    flat = [res[n][kind] for kind in range(4) for n in WEIGHTS]
    return (loss, grad_x[None], *flat)
```

```python
import functools

import jax
import jax.numpy as jnp
from jax import lax
from jax.experimental import pallas as pl
from jax.experimental.pallas import tpu as pltpu

F32 = jnp.float32
HI = lax.Precision.HIGHEST
_MXU_DTYPE = jnp.bfloat16
_SCAN_DTYPE = jnp.bfloat16
_GRAD_DTYPE = jnp.bfloat16

D_MODEL = 1024
EPS = 1e-6
CHUNK = 64
GRID_W = 64
GDN_HEADS = 8
GDN_DK = 128
SSM_HEADS = 32
SSM_PAIRS = 16
SSD_GROUP = 4
D_FF = 2816
D_IN_PROJ = 11360
N_PLANE = 4
P_QKV, P_ZG, P_ZS, P_XBC, P_GATE, P_SMALL, P_TOTAL = 0, 3072, 4096, 6144, 9216, 11264, 11776
IN_SEGMENTS = [(0, 3072, P_QKV), (3072, 4096, P_ZG), (4096, 4112, P_SMALL), (4112, 4128, P_SMALL + 16),
               (4128, 6176, P_ZS), (6176, 9248, P_XBC), (9248, 9312, P_SMALL + 32), (9312, 11360, P_GATE)]
ADAM_LR, ADAM_B1, ADAM_B2, ADAM_EPS, ADAM_WD, ADAM_STEP = 0.001, 0.9, 0.999, 1e-08, 0.01, 10

VMEM_LIMIT = 48 * 1024 * 1024

WEIGHTS = ['c_ctx', 'ada_w', 'ada_b', 'norm1_w', 'w_in', 'gdn_conv_w', 'gdn_conv_b', 'gdn_a_log', 'gdn_dt_bias',
           'gdn_norm_w', 'ssm_conv_w', 'ssm_conv_b', 'ssm_a_log', 'ssm_dt_bias', 'ssm_d', 'ssm_norm_w', 'w_br_gdn',
           'w_br_ssm', 'w_out', 'norm2_w', 'w_ffn_in', 'w_ffn_out', 'norm_f_w']
SHARD_AXIS = {'ada_w': 1, 'w_in': 1, 'gdn_conv_w': 1, 'ssm_conv_w': 1, 'w_br_gdn': 0, 'w_br_ssm': 0, 'w_out': 0,
              'w_ffn_in': 1, 'w_ffn_out': 0}
SHARDED = [n for n in WEIGHTS if n in SHARD_AXIS]
SMALL = [n for n in WEIGHTS if n not in SHARD_AXIS]
MXU_WEIGHTS = ['ada_w', 'w_in', 'w_br_gdn', 'w_br_ssm', 'w_out', 'w_ffn_in', 'w_ffn_out']


def _cp(sem):
    return pltpu.CompilerParams(dimension_semantics=sem, vmem_limit_bytes=VMEM_LIMIT)


def _pick(n, cands):
    for c in cands:
        if n % c == 0:
            return c
    raise ValueError(f"no tile for {n}")


HBM = pl.BlockSpec(memory_space=pl.ANY)
MESH = pl.DeviceIdType.MESH


def _plane_peers():
    x, y, c = lax.axis_index("x"), lax.axis_index("y"), lax.axis_index("c")
    return (x, y, c), [(1 - x, y, c), (x, 1 - y, c), (1 - x, 1 - y, c)]


class PlaneExchange:
    def __init__(self, arrs, gather):
        self.arrs, self.gather, self.n = list(arrs), gather, len(arrs)
        self.in_specs = [HBM] * self.n
        self.out_specs = [HBM] * self.n
        self.out_shape = [jax.ShapeDtypeStruct((N_PLANE,) + a.shape if gather else a.shape, a.dtype) for a in self.arrs]
        self.scratch = [pltpu.SemaphoreType.DMA((3 * self.n,)), pltpu.SemaphoreType.DMA((3 * self.n,)),
                        pltpu.SemaphoreType.DMA((self.n,))]

    def _copies(self, ins, outs, sems):
        send_sems, recv_sems, local_sems = sems
        (x, y, c), peers = _plane_peers()
        me = 2 * x + y
        copies = []
        for ti in range(self.n):
            src = ins[ti] if self.gather else ins[ti].at[me]
            copies.append(pltpu.make_async_copy(src, outs[ti].at[me], local_sems.at[ti]))
            for kk, (px, py, pc) in enumerate(peers):
                src = ins[ti] if self.gather else ins[ti].at[2 * px + py]
                copies.append(pltpu.make_async_remote_copy(
                    src_ref=src, dst_ref=outs[ti].at[me], send_sem=send_sems.at[3 * ti + kk],
                    recv_sem=recv_sems.at[3 * ti + kk], device_id=(px, py, pc), device_id_type=MESH))
        return copies

    def start(self, ins, outs, sems):
        for cp in self._copies(ins, outs, sems):
            cp.start()

    def wait(self, ins, outs, sems):
        for cp in self._copies(ins, outs, sems):
            cp.wait()


def _plane_exchange(name, arrs, gather):
    ex = PlaneExchange(arrs, gather)
    n = ex.n

    def body(*refs):
        ins, outs, sems = refs[:n], refs[n:2 * n], refs[2 * n:]
        ex.start(ins, outs, sems)
        ex.wait(ins, outs, sems)

    return pl.pallas_call(body, in_specs=ex.in_specs, out_specs=ex.out_specs, out_shape=ex.out_shape,
                          scratch_shapes=ex.scratch, name=name)(*arrs)


def _gather_two_level(name, arrs):
    n = len(arrs)
    split = [a.shape[0] % 32 == 0 for a in arrs]

    def body(*refs):
        ins, outs = refs[:n], refs[n:2 * n]
        ici_send, ici_recv, d2d_send, d2d_recv, local_sems = refs[2 * n:]
        (x, y, c), peers = _plane_peers()
        me = 2 * x + y

        def part(ref, ti):
            if not split[ti]:
                return ref
            rows = arrs[ti].shape[0] // 2
            return ref.at[pl.ds(c * rows, rows)]

        local = [pltpu.make_async_copy(ins[ti], outs[ti].at[me], local_sems.at[ti]) for ti in range(n)]
        ici = [pltpu.make_async_remote_copy(
            src_ref=part(ins[ti], ti), dst_ref=part(outs[ti].at[me], ti), send_sem=ici_send.at[3 * ti + kk],
            recv_sem=ici_recv.at[3 * ti + kk], device_id=peer, device_id_type=MESH)
            for ti in range(n) for kk, peer in enumerate(peers)]
        for cp in local + ici:
            cp.start()
        d2d = []
        for ti in range(n):
            for kk, (px, py, pc) in enumerate(peers):
                ici[3 * ti + kk].wait_recv()
                if split[ti]:
                    piece = part(outs[ti].at[2 * px + py], ti)
                    cp = pltpu.make_async_remote_copy(
                        src_ref=piece, dst_ref=piece, send_sem=d2d_send.at[3 * ti + kk],
                        recv_sem=d2d_recv.at[3 * ti + kk], device_id=(x, y, 1 - c), device_id_type=MESH)
                    cp.start()
                    d2d.append(cp)
        for cp in ici:
            cp.wait_send()
        for cp in d2d:
            cp.wait()
        for cp in local:
            cp.wait()

    return pl.pallas_call(
        body, in_specs=[HBM] * n, out_specs=[HBM] * n,
        out_shape=[jax.ShapeDtypeStruct((N_PLANE,) + a.shape, a.dtype) for a in arrs],
        scratch_shapes=[pltpu.SemaphoreType.DMA((3 * n,))] * 4 + [pltpu.SemaphoreType.DMA((n,))], name=name)(*arrs)


def _swap_sibling(arrs):
    n = len(arrs)

    def body(*refs):
        ins, outs, send_sems, recv_sems = refs[:n], refs[n:2 * n], refs[2 * n], refs[2 * n + 1]
        x, y, c = lax.axis_index("x"), lax.axis_index("y"), lax.axis_index("c")
        copies = [pltpu.make_async_remote_copy(src_ref=ins[ti], dst_ref=outs[ti], send_sem=send_sems.at[ti],
                                               recv_sem=recv_sems.at[ti], device_id=(x, y, 1 - c), device_id_type=MESH)
                  for ti in range(n)]
        for cp in copies:
            cp.start()
        for cp in copies:
            cp.wait()

    return pl.pallas_call(
        body, in_specs=[HBM] * n, out_specs=[HBM] * n, out_shape=[jax.ShapeDtypeStruct(a.shape, a.dtype) for a in arrs],
        scratch_shapes=[pltpu.SemaphoreType.DMA((n,)), pltpu.SemaphoreType.DMA((n,))], name="swap_sibling")(*arrs)


MATMUL_VMEM_BUDGET = 36 * 1024 * 1024
TILE_CANDIDATES = (2944, 2816, 1408, 1024, 768, 512, 256, 128)
STEP_COST_BYTES = 1 << 20


def _matmul_tiles(m, n, k, ab, bb, ob, tn_fixed=None, tk_fixed=None):
    def cands(dim, whole_up_to):
        out = [c for c in TILE_CANDIDATES if dim % c == 0]
        if dim <= whole_up_to and dim not in out:
            out.append(dim)
        return out

    best = None
    for tm in cands(m, 3072):
        for tn in ([tn_fixed] if tn_fixed else cands(n, 3072)):
            for tk in ([tk_fixed] if tk_fixed else cands(k, 2048)):
                gi, gj, gk = m // tm, n // tn, k // tk
                vmem = 2 * (tm * tk * ab + tk * tn * bb + tm * tn * ob) + (tm * tn * 4 if gk > 1 else 0)
                if vmem > MATMUL_VMEM_BUDGET:
                    continue
                a_reads = 1 if gk == 1 else gj
                b_reads = 1 if (gk == 1 and gj == 1) else gi
                cost = (m * k * ab * a_reads + k * n * bb * b_reads + m * n * ob + gi * gj * gk * STEP_COST_BYTES)
                if best is None or cost < best[0]:
                    best = (cost, tm, tn, tk)
    assert best is not None, (m, n, k)
    return best[1:]


def _matmul(a, b, form, name, out_dtype=F32, stacked_out=False, ride=None):
    dims = {'nn': (((1,), (0,)), ((), ())), 'nt': (((1,), (1,)), ((), ())), 'tn': (((0,), (0,)), ((), ()))}[form]
    stacked_b = b.ndim == 3
    ns = b.shape[2] if stacked_b else None
    if form == 'nn':
        m, k = a.shape
        n = b.shape[0] * ns if stacked_b else b.shape[1]
    elif form == 'nt':
        m, k = a.shape
        n = b.shape[1] if stacked_b else b.shape[0]
    else:
        k, m = a.shape
        n = b.shape[1]
    ob = jnp.dtype(out_dtype).itemsize
    tm, tn, tk = _matmul_tiles(m, n, k, a.dtype.itemsize, b.dtype.itemsize, ob,
                               tn_fixed=(n // N_PLANE if stacked_out else ns if (stacked_b and form == 'nn') else None),
                               tk_fixed=(ns if (stacked_b and form == 'nt') else None))
    nk = k // tk
    grid = (m // tm, n // tn, nk)
    n_ride = ride.n if ride is not None else 0

    def body(*refs):
        a_ref, b_ref = refs[0], refs[1]
        ride_in = refs[2:2 + n_ride]
        o_ref = refs[2 + n_ride]
        ride_out = refs[3 + n_ride:3 + 2 * n_ride]
        acc_ref = refs[3 + 2 * n_ride]
        sems = refs[4 + 2 * n_ride:]
        i, j, kk = pl.program_id(0), pl.program_id(1), pl.program_id(2)
        if ride is not None:
            @pl.when((i == 0) & (j == 0) & (kk == 0))
            def _():
                ride.start(ride_in, ride_out, sems)

        def put(val):
            if stacked_out:
                o_ref[0] = val.astype(o_ref.dtype)
            else:
                o_ref[...] = val.astype(o_ref.dtype)

        bv = b_ref[0] if stacked_b else b_ref[...]
        part = lax.dot_general(a_ref[...].astype(_MXU_DTYPE), bv.astype(_MXU_DTYPE), dims, preferred_element_type=F32)
        if nk == 1:
            put(part)
        else:
            @pl.when(kk == 0)
            def _():
                acc_ref[...] = part

            @pl.when(kk > 0)
            def _():
                acc_ref[...] += part

            @pl.when(kk == nk - 1)
            def _():
                put(acc_ref[...])

        if ride is not None:
            @pl.when((i == grid[0] - 1) & (j == grid[1] - 1) & (kk == nk - 1))
            def _():
                ride.wait(ride_in, ride_out, sems)

    if form == 'nn':
        a_spec = pl.BlockSpec((tm, tk), lambda i, j, kk: (i, kk))
        b_spec = (pl.BlockSpec((1, tk, tn), lambda i, j, kk: (j, kk, 0)) if stacked_b
                  else pl.BlockSpec((tk, tn), lambda i, j, kk: (kk, j)))
    elif form == 'nt':
        a_spec = pl.BlockSpec((tm, tk), lambda i, j, kk: (i, kk))
        b_spec = (pl.BlockSpec((1, tn, tk), lambda i, j, kk: (kk, j, 0)) if stacked_b
                  else pl.BlockSpec((tn, tk), lambda i, j, kk: (j, kk)))
    else:
        a_spec = pl.BlockSpec((tk, tm), lambda i, j, kk: (kk, i))
        b_spec = pl.BlockSpec((tk, tn), lambda i, j, kk: (kk, j))
    if stacked_out:
        o_spec = pl.BlockSpec((1, tm, tn), lambda i, j, kk: (j, i, 0))
        o_shape = jax.ShapeDtypeStruct((N_PLANE, m, tn), out_dtype)
    else:
        o_spec = pl.BlockSpec((tm, tn), lambda i, j, kk: (i, j))
        o_shape = jax.ShapeDtypeStruct((m, n), out_dtype)
    acc_shape = (tm, tn) if nk > 1 else (8, 128)
    if ride is None:
        return pl.pallas_call(
            body, grid=grid, in_specs=[a_spec, b_spec], out_specs=o_spec, out_shape=o_shape,
            scratch_shapes=[pltpu.VMEM(acc_shape, F32)],
            compiler_params=_cp(("parallel", "parallel", "arbitrary")), name=name)(a, b)
    res = pl.pallas_call(
        body, grid=grid, in_specs=[a_spec, b_spec] + ride.in_specs, out_specs=[o_spec] + ride.out_specs,
        out_shape=[o_shape] + ride.out_shape, scratch_shapes=[pltpu.VMEM(acc_shape, F32)] + ride.scratch,
        compiler_params=_cp(("arbitrary", "arbitrary", "arbitrary")), name=name)(a, b, *ride.arrs)
    return res[0], res[1:]


class Op:
    def __init__(self, arr, bs, im, load=None):
        self.arr, self.bs, self.im = arr, bs, im
        self.arrs = list(arr) if isinstance(arr, list) else [arr]
        self.load = load or (lambda r: r[...].astype(F32))

    def spec(self):
        return pl.BlockSpec(self.bs, self.im)

    def value(self, it):
        return functools.reduce(lambda u, w: u + w, [self.load(next(it)) for _ in self.arrs])


def _op_specs(ops):
    return [op.spec() for op in ops for _ in op.arrs]


def _op_arrays(ops):
    return [a for op in ops for a in op.arrs]


def _sum_dirs(r):
    return r[0].astype(F32) + r[1].astype(F32)


def _tw_fwd(name, fn, grid, ins, outs):
    def body(*refs):
        info = (pl.program_id(0), pl.program_id(1))
        it = iter(refs)
        vals = [op.value(it) for op in ins]
        res = fn(info, *vals)
        for r, v in zip(it, res):
            r[...] = v.astype(r.dtype)

    return pl.pallas_call(
        body, grid=grid, in_specs=_op_specs(ins), out_specs=[op.spec() for op in outs],
        out_shape=[jax.ShapeDtypeStruct(*op.arr) for op in outs],
        compiler_params=_cp(("parallel", "arbitrary")), name=name)(*_op_arrays(ins))


def _tw_bwd(name, fn, grid, tok, par, cots, tok_out, par_out, tok_add=None, sem=("parallel", "arbitrary")):
    n_tok = len(tok)
    flat_cots = [op for group in cots for op in group]
    extra = [tok_add] if tok_add is not None else []
    out_ops = list(tok_out) + list(par_out)

    def body(*refs):
        info = (pl.program_id(0), pl.program_id(1))
        it = iter(refs)
        tok_v = [op.value(it) for op in tok]
        par_v = [op.value(it) for op in par]
        cot_v = [functools.reduce(lambda u, w: u + w, [op.value(it) for op in group]) for group in cots]
        add_v = [op.value(it) for op in extra]
        _, pull = jax.vjp(lambda *a: fn(info, *a), *tok_v, *par_v)
        grads = pull(tuple(cot_v))
        for i in range(n_tok):
            r = next(it)
            g = grads[i] + add_v[0] if (i == 0 and add_v) else grads[i]
            r[...] = g.astype(r.dtype)
        first = pl.program_id(1) == 0
        for i in range(len(par)):
            r = next(it)
            g = grads[n_tok + i]

            @pl.when(first)
            def _(r=r, g=g):
                r[...] = g

            @pl.when(jnp.logical_not(first))
            def _(r=r, g=g):
                r[...] += g

    ops = tok + par + flat_cots + extra
    return pl.pallas_call(
        body, grid=grid, in_specs=_op_specs(ops), out_specs=[op.spec() for op in out_ops],
        out_shape=[jax.ShapeDtypeStruct(*op.arr) for op in out_ops],
        compiler_params=_cp(sem), name=name)(*_op_arrays(ops))


def _silu(x):
    return x * jax.nn.sigmoid(x)


def _rms(x):
    return x * lax.rsqrt(jnp.mean(x * x, axis=-1, keepdims=True) + EPS)


@functools.partial(jax.custom_vjp, nondiff_argnums=(1,))
def _shift_rows(x, k):
    return pltpu.roll(x, k % x.shape[0], 0)


def _shift_rows_fwd(x, k):
    return _shift_rows(x, k), None


def _shift_rows_bwd(k, _, g):
    return (_shift_rows(g, -k),)


_shift_rows.defvjp(_shift_rows_fwd, _shift_rows_bwd)


def _prenorm_fn(nctx_t, info, x, w, sc8, sh8):
    is_ctx = info[1] < nctx_t
    sc = jnp.where(is_ctx, sc8[1:2], sc8[0:1])
    sh = jnp.where(is_ctx, sh8[1:2], sh8[0:1])
    return (_rms(x) * w * (1.0 + sc) + sh,)


def _conv_fn(nctx_t, mode, info, x, w, b):
    n, c = x.shape
    is_ctx = info[1] < nctx_t
    idx = lax.broadcasted_iota(jnp.int32, (n, 1), 0)
    rr = jnp.where(is_ctx, idx, idx % GRID_W)
    first = rr == 0
    last = rr == jnp.where(is_ctx, n - 1, GRID_W - 1)
    prev = jnp.where(first, 0.0, _shift_rows(x, 1))
    nxt = jnp.where(last, 0.0, _shift_rows(x, -1))
    y = b + prev * w[0:1] + x * w[1:2] + nxt * w[2:3]
    y = _silu(y)
    if mode == 'none':
        return (y,)
    scale = GDN_DK ** -0.5 if mode == 'q' else 1.0
    outs = []
    for h in range(c // 128):
        yh = y[:, h * 128:(h + 1) * 128]
        outs.append(yh * lax.rsqrt(jnp.sum(yh * yh, axis=-1, keepdims=True) + EPS) * scale)
    return (jnp.concatenate(outs, axis=1),)


def _act_fn(info, x, p0, p1):
    lane = lax.broadcasted_iota(jnp.int32, x.shape, 1)
    sp = jax.nn.softplus(x + p1)
    g = -jnp.exp(p0) * sp
    bt = jax.nn.sigmoid(x)
    return (jnp.where(lane < 16, g, jnp.where(lane < 32, bt, jnp.where(lane < 96, sp, 0.0))),)


def _mixg_fn(info, o, zg, gnw):
    outs = []
    for h in range(GDN_HEADS):
        outs.append(_rms(o[:, h * 128:(h + 1) * 128]) * gnw)
    return (jnp.concatenate(outs, axis=1) * _silu(zg),)


def _mixs_fn(info, y, xs, zs, dl, snw):
    yy = (y + dl * xs) * _silu(zs)
    outs = []
    for g in range(4):
        outs.append(_rms(yy[:, g * 512:(g + 1) * 512]))
    return (jnp.concatenate(outs, axis=1) * snw,)


def _merge_fn(info, gg, gs, pg, ps):
    return (jax.nn.sigmoid(gg) * pg + jax.nn.sigmoid(gs) * ps,)


def _norm2_fn(info, xt, mo, g8, w, sc8, sh8):
    h1 = xt + g8[0:1] * mo
    return (h1, _rms(h1) * w * (1.0 + sc8[0:1]) + sh8[0:1])


def _swiglu_fn(info, ug, uu):
    return (_silu(ug) * uu,)


_NN = (((1,), (0,)), ((), ()))
_NT = (((1,), (1,)), ((), ()))
_TN = (((0,), (0,)), ((), ()))


def _mmh(a, b):
    return lax.dot_general(a, b, _NN, precision=HI, preferred_element_type=F32)


def _dot1(a, b, dims):
    return lax.dot_general(a.astype(_SCAN_DTYPE), b.astype(_SCAN_DTYPE), dims, preferred_element_type=F32)


def _mm(a, b):
    return _dot1(a, b, _NN)


def _mm_nt(a, b):
    return _dot1(a, b, _NT)


def _mm_tn(a, b):
    return _dot1(a, b, _TN)


def _split2(a):
    hi = a.astype(_SCAN_DTYPE)
    return hi, (a - hi.astype(F32)).astype(_SCAN_DTYPE)


def _dot3(a, b, dims):
    ah, al = _split2(a)
    bh, bl = _split2(b)
    d = lambda u, w: lax.dot_general(u, w, dims, preferred_element_type=F32)
    return d(ah, bh) + (d(ah, bl) + d(al, bh))


def _order_masks(d):
    i = lax.broadcasted_iota(jnp.int32, (CHUNK, CHUNK), 0)
    j = lax.broadcasted_iota(jnp.int32, (CHUNK, CHUNK), 1)
    s = jnp.where(d == 0, 1, -1) * (i - j)
    return (s >= 0).astype(F32), (s > 0).astype(F32)


@jax.custom_vjp
def _unit_tri_inv(mats):
    i = lax.broadcasted_iota(jnp.int32, (CHUNK, CHUNK), 0)
    j = lax.broadcasted_iota(jnp.int32, (CHUNK, CHUNK), 1)
    eye = (i == j).astype(F32)
    ps = [-a for a in mats]
    ts = [eye + p for p in ps]
    for _ in range(5):
        ps = [_dot3(p, p, _NN) for p in ps]
        ts = [t + _dot3(t, p, _NN) for t, p in zip(ts, ps)]
    return tuple(ts)


def _uti_fwd(mats):
    ts = _unit_tri_inv(mats)
    return ts, ts


def _uti_bwd(ts, gs):
    inner = [_dot3(g, t, _NT) for g, t in zip(gs, ts)]
    return (tuple(-_dot3(t, u, _TN) for t, u in zip(ts, inner)),)


_unit_tri_inv.defvjp(_uti_fwd, _uti_bwd)


@jax.custom_vjp
def _unit_tri_inv_given(mats, ts):
    return ts


def _utig_fwd(mats, ts):
    return ts, ts


def _utig_bwd(ts, gs):
    return _uti_bwd(ts, gs)[0], tuple(jnp.zeros_like(t) for t in ts)


_unit_tri_inv_given.defvjp(_utig_fwd, _utig_bwd)


def _lane_col(blk, lane_idx):
    lane = lax.broadcasted_iota(jnp.int32, blk.shape, 1)
    return jnp.sum(jnp.where(lane == lane_idx, blk, 0.0), axis=1, keepdims=True)


def _decay_mat(cum, incl):
    cb = jnp.broadcast_to(cum, (CHUNK, CHUNK))
    return jnp.exp(jnp.minimum(cb - cb.T, 0.0)) * incl


def _gdn_chunks(dirs, streams, states, aux=None, want_aux=False, group=None):
    ns = len(dirs)
    masks = [_order_masks(d) for d in dirs]
    cum = [_mmh(masks[i][0], streams[i][3]) for i in range(ns)]
    tot = [jnp.sum(streams[i][3], axis=0, keepdims=True) for i in range(ns)]
    all_units = [(i, h) for i in range(ns) for h in range(GDN_HEADS)]
    group = group or len(all_units)
    cat = jnp.concatenate
    outs, new_states, ts_all = [], [], []
    for g0 in range(0, len(all_units), group):
        units = all_units[g0:g0 + group]
        us = range(len(units))
        st = states[g0:g0 + group]
        qs = [streams[i][0][:, h * 128:(h + 1) * 128] for i, h in units]
        ks = [streams[i][1][:, h * 128:(h + 1) * 128] for i, h in units]
        vs = [streams[i][2][:, h * 128:(h + 1) * 128] for i, h in units]
        gcum = [_lane_col(cum[i], dirs[i] * GDN_HEADS + h) for i, h in units]
        glast = [_lane_col(tot[i], dirs[i] * GDN_HEADS + h) for i, h in units]
        beta = [_lane_col(streams[i][3], 16 + dirs[i] * GDN_HEADS + h) for i, h in units]
        decay = [_decay_mat(gcum[u], masks[units[u][0]][0]) for u in us]
        egc = [jnp.exp(gcum[u]) for u in us]
        kb = [ks[u] * beta[u] for u in us]
        kq = [_mm_nt(cat([kb[u], qs[u]], axis=0), ks[u]) for u in us]
        mats = tuple(kq[u][:CHUNK] * decay[u] * masks[units[u][0]][1] for u in us)
        ts = _unit_tri_inv(mats) if aux is None else _unit_tri_inv_given(mats, tuple(aux[g0:g0 + group]))
        wu = [_mm(ts[u], cat([kb[u] * egc[u], vs[u] * beta[u]], axis=1)) for u in us]
        ws = [_mm(cat([wu[u][:, :128], qs[u] * egc[u]], axis=0), st[u]) for u in us]
        vn = [wu[u][:, 128:] - ws[u][:CHUNK] for u in us]
        outs += [ws[u][CHUNK:] + _mm(kq[u][CHUNK:] * decay[u], vn[u]) for u in us]
        new_states += [st[u] * jnp.exp(glast[u]) + _mm_tn(ks[u] * jnp.exp(glast[u] - gcum[u]), vn[u]) for u in us]
        ts_all += list(ts)
    per_stream = [cat(outs[i * GDN_HEADS:(i + 1) * GDN_HEADS], axis=1) for i in range(ns)]
    return (*per_stream, *new_states, *(ts_all if want_aux else ()))


def _gdn_step(d, q, k, v, spb, *states, aux=None, want_aux=False):
    return _gdn_chunks([d], [(q, k, v, spb)], states, aux, want_aux, group=4)


def _gdn_step2(q0, k0, v0, sp0, q1, k1, v1, sp1, *states, aux=None, want_aux=False):
    return _gdn_chunks([0, 1], [(q0, k0, v0, sp0), (q1, k1, v1, sp1)], states, aux, want_aux)


def _split3(a):
    a1 = a.astype(_SCAN_DTYPE)
    r = a - a1.astype(F32)
    a2 = r.astype(_SCAN_DTYPE)
    return a1, a2, (r - a2.astype(F32)).astype(_SCAN_DTYPE)


def _exact_dot(a, e, dims, split_lhs):
    parts = _split3(a if split_lhs else e)
    d = (lambda u: lax.dot_general(u, e, dims, preferred_element_type=F32)) if split_lhs else \
        (lambda u: lax.dot_general(a, u, dims, preferred_element_type=F32))
    return d(parts[0]) + (d(parts[1]) + d(parts[2]))


@jax.custom_vjp
def _spread(a, e):
    return _exact_dot(a, e, _NN, True)


def _spread_fwd(a, e):
    return _spread(a, e), e


def _spread_bwd(e, g):
    return _exact_dot(g, e, _NT, True), jnp.zeros_like(e)


_spread.defvjp(_spread_fwd, _spread_bwd)


@jax.custom_vjp
def _colsum_bcast(z):
    return _exact_dot(jnp.ones((z.shape[0], z.shape[0]), _SCAN_DTYPE), z, _NN, False)


def _colsum_fwd(z):
    return _colsum_bcast(z), None


def _colsum_bwd(_, g):
    return (_colsum_bcast(g),)


_colsum_bcast.defvjp(_colsum_fwd, _colsum_bwd)


def _ssd_consts():
    wdt = SSM_HEADS * 64
    d = lax.broadcasted_iota(jnp.int32, (2, 128, wdt), 0)
    e = (lax.broadcasted_iota(jnp.int32, (2, 128, wdt), 1)
         == 32 + d * SSM_HEADS + lax.broadcasted_iota(jnp.int32, (2, 128, wdt), 2) // 64).astype(_SCAN_DTYPE)
    dd = lax.broadcasted_iota(jnp.int32, (2, CHUNK, wdt), 0)
    ci = lax.broadcasted_iota(jnp.int32, (2, CHUNK, wdt), 1)
    pos = lax.broadcasted_iota(jnp.int32, (2, CHUNK, wdt), 2) % 64
    incl_t = (jnp.where(dd == 0, 1, -1) * (ci - pos) >= 0).astype(F32)
    diag = (ci == pos).astype(F32)
    return [e, incl_t, diag]


def _ssd_step(d, x, bm, cm, spb, alog, *states, consts):
    e, incl_t, diag = consts
    incl, _ = _order_masks(d)
    lane1 = lax.broadcasted_iota(jnp.int32, (1, 128), 1)
    lo_lane = 32 + d * SSM_HEADS
    a_vec = jnp.where(lane1 >= lo_lane, jnp.where(lane1 < lo_lane + SSM_HEADS, -jnp.exp(alog), 0.0), 0.0)
    adt = spb * a_vec
    acum = _mmh(incl, adt)
    alast = jnp.sum(adt, axis=0, keepdims=True)
    dt2 = _spread(spb, e)
    ac2 = _spread(acum, e)
    al2 = _spread(jnp.broadcast_to(alast, (8, 128)), e)[0:1]
    row = _colsum_bcast(ac2 * diag)
    seg = jnp.exp(jnp.minimum(ac2 - row, 0.0)) * incl_t
    xdt = x * dt2
    gam = jnp.exp(ac2)
    xe = xdt * jnp.exp(al2 - ac2)
    low = lax.broadcasted_iota(jnp.int32, (CHUNK, 128), 1) < 64
    row_low = lax.broadcasted_iota(jnp.int32, (128, 1), 0) < 64
    ps = range(SSM_PAIRS)
    sl = [slice(p * 128, (p + 1) * 128) for p in ps]
    bg = [bm[:, g * 128:(g + 1) * 128] for g in range(4)]
    cg = [cm[:, g * 128:(g + 1) * 128] for g in range(4)]
    cb2 = [_mm_nt(cg[g], jnp.concatenate([bg[g], bg[g]], axis=0)) for g in range(4)]
    ys, new_states = [], []
    for p0 in range(0, SSM_PAIRS, SSD_GROUP):
        pg = range(p0, p0 + SSD_GROUP)
        xd = {p: jnp.concatenate([jnp.where(low, xdt[:, sl[p]], 0.0), jnp.where(low, 0.0, xdt[:, sl[p]])], axis=0)
              for p in pg}
        yd = {p: _mm(cb2[p // 4] * seg[:, sl[p]], xd[p]) for p in pg}
        yo = {p: _mm_nt(cg[p // 4], states[p]) for p in pg}
        ys += [yd[p] + gam[:, sl[p]] * yo[p] for p in pg]
        new = {p: _mm_tn(xe[:, sl[p]], bg[p // 4]) for p in pg}
        al0 = {p: _lane_col(alast, lo_lane + 2 * p) for p in pg}
        al1 = {p: _lane_col(alast, lo_lane + 2 * p + 1) for p in pg}
        new_states += [states[p] * jnp.exp(jnp.where(row_low, al0[p], al1[p])) + new[p] for p in pg]
    return (jnp.concatenate(ys, axis=1), *new_states)


def _chunk_of(d, p, nctx, nc):
    return jnp.where(d == 0, p, jnp.where(p < nctx, nctx - 1 - p, nctx + nc - 1 - p))


class _NoRide:
    n, arrs, in_specs, out_specs, out_shape, scratch = 0, [], [], [], [], []


def _const_specs(consts):
    return [pl.BlockSpec((1,) + a.shape[1:], lambda d, p: (d,) + (0,) * (a.ndim - 1)) for a in consts]


def _scan_fwd(name, step, toks, pars, consts, out_width, n_state, nctx):
    t = toks[0].shape[0]
    nc = t // CHUNK
    n_tok, n_par, n_const = len(toks), len(pars), len(consts)

    def body(*refs):
        it = iter(refs)
        tok_refs = [next(it) for _ in range(n_tok)]
        par_refs = [next(it) for _ in range(n_par)]
        const_refs = [next(it) for _ in range(n_const)]
        o_ref, ss_ref, s_scr = next(it), next(it), next(it)
        d, p = pl.program_id(0), pl.program_id(1)

        @pl.when(p == 0)
        def _():
            s_scr[...] = jnp.zeros(s_scr.shape, F32)

        ss_ref[0, 0] = s_scr[...]
        res = step(d, *[r[...] for r in tok_refs], *[r[...] for r in par_refs], *[s_scr[h] for h in range(n_state)],
                   consts=[r[0] for r in const_refs])
        o_ref[0] = res[0]
        for h in range(n_state):
            s_scr[h] = res[1 + h]

    ch = lambda d, p: _chunk_of(d, p, nctx, nc)
    in_specs = [pl.BlockSpec((CHUNK, a.shape[1]), lambda d, p: (ch(d, p), 0)) for a in toks]
    in_specs += [pl.BlockSpec(a.shape, lambda d, p: (0, 0)) for a in pars]
    return pl.pallas_call(
        body, grid=(2, nc), in_specs=in_specs + _const_specs(consts),
        out_specs=[pl.BlockSpec((1, CHUNK, out_width), lambda d, p: (d, ch(d, p), 0)),
                   pl.BlockSpec((1, 1, n_state, 128, 128), lambda d, p: (d, p, 0, 0, 0))],
        out_shape=[jax.ShapeDtypeStruct((2, t, out_width), F32),
                   jax.ShapeDtypeStruct((2, nc, n_state, 128, 128), F32)],
        scratch_shapes=[pltpu.VMEM((n_state, 128, 128), F32)],
        compiler_params=_cp(("arbitrary", "arbitrary")), name=name)(*toks, *pars, *consts)


def _scan_bwd(name, step, toks, pars, ss, dout, n_state, nctx, ride=None, aux=None, consts=()):
    t = toks[0].shape[0]
    nc = t // CHUNK
    n_tok, n_par, n_const = len(toks), len(pars), len(consts)
    rd = ride if ride is not None else _NoRide
    n_aux = aux.shape[2] if aux is not None else 0

    def body(*refs):
        it = iter(refs)
        tok_refs = [next(it) for _ in range(n_tok)]
        par_refs = [next(it) for _ in range(n_par)]
        ss_ref, do_ref = next(it), next(it)
        aux_ref = next(it) if n_aux else None
        const_refs = [next(it) for _ in range(n_const)]
        ride_in = [next(it) for _ in range(rd.n)]
        dtok_refs = [next(it) for _ in range(n_tok)]
        dpar_refs = [next(it) for _ in range(n_par)]
        ride_out = [next(it) for _ in range(rd.n)]
        ds_scr = next(it)
        sems = list(it)
        d, pr = pl.program_id(0), pl.program_id(1)
        if ride is not None:
            @pl.when((d == 0) & (pr == 0))
            def _():
                ride.start(ride_in, ride_out, sems)

            @pl.when((d == 1) & (pr == nc - 1))
            def _():
                ride.wait(ride_in, ride_out, sems)

        @pl.when(pr == 0)
        def _():
            ds_scr[...] = jnp.zeros(ds_scr.shape, F32)

        kw = dict(aux=[aux_ref[0, 0, i] for i in range(n_aux)]) if n_aux else {}
        if n_const:
            kw['consts'] = [r[0] for r in const_refs]
        _, pull = jax.vjp(functools.partial(step, d, **kw), *[r[...] for r in tok_refs], *[r[...] for r in par_refs],
                          *[ss_ref[0, 0, h] for h in range(n_state)])
        grads = pull((do_ref[...], *[ds_scr[h] for h in range(n_state)]))
        for r, g in zip(dtok_refs, grads[:n_tok]):
            r[0] = g
        for h in range(n_state):
            ds_scr[h] = grads[n_tok + n_par + h]
        first = (d == 0) & (pr == 0)
        for r, g in zip(dpar_refs, grads[n_tok:n_tok + n_par]):
            @pl.when(first)
            def _(r=r, g=g):
                r[...] = g

            @pl.when(jnp.logical_not(first))
            def _(r=r, g=g):
                r[...] += g

    ch = lambda d, pr: _chunk_of(d, nc - 1 - pr, nctx, nc)
    in_specs = [pl.BlockSpec((CHUNK, a.shape[1]), lambda d, pr: (ch(d, pr), 0)) for a in toks]
    in_specs += [pl.BlockSpec(a.shape, lambda d, pr: (0, 0)) for a in pars]
    in_specs += [pl.BlockSpec((1, 1, n_state, 128, 128), lambda d, pr: (d, nc - 1 - pr, 0, 0, 0)),
                 pl.BlockSpec((CHUNK, dout.shape[1]), lambda d, pr: (ch(d, pr), 0))]
    if n_aux:
        in_specs += [pl.BlockSpec((1, 1, n_aux, CHUNK, CHUNK), lambda d, pr: (d, nc - 1 - pr, 0, 0, 0))]
    in_specs += _const_specs(consts)
    out_specs = [pl.BlockSpec((1, CHUNK, a.shape[1]), lambda d, pr: (d, ch(d, pr), 0)) for a in toks]
    out_specs += [pl.BlockSpec(a.shape, lambda d, pr: (0, 0)) for a in pars]
    out_shape = [jax.ShapeDtypeStruct((2, t, a.shape[1]), F32) for a in toks]
    out_shape += [jax.ShapeDtypeStruct(a.shape, F32) for a in pars]
    return pl.pallas_call(
        body, grid=(2, nc), in_specs=in_specs + rd.in_specs, out_specs=out_specs + rd.out_specs,
        out_shape=out_shape + rd.out_shape, scratch_shapes=[pltpu.VMEM((n_state, 128, 128), F32)] + rd.scratch,
        compiler_params=_cp(("arbitrary", "arbitrary")), name=name)(
            *toks, *pars, ss, dout, *([aux] if n_aux else []), *consts, *rd.arrs)


def _scan2_fwd(name, step2, toks, out_width, n_state, nctx, n_aux, ride=None):
    t = toks[0].shape[0]
    nc = t // CHUNK
    n_tok = len(toks)
    rd = ride if ride is not None else _NoRide

    def body(*refs):
        it = iter(refs)
        tok_refs = [next(it) for _ in range(2 * n_tok)]
        ride_in = [next(it) for _ in range(rd.n)]
        o_refs = [next(it), next(it)]
        ss_ref, aux_ref = next(it), next(it)
        ride_out = [next(it) for _ in range(rd.n)]
        s_scr = next(it)
        sems = list(it)
        p = pl.program_id(0)
        if ride is not None:
            @pl.when(p == 0)
            def _():
                ride.start(ride_in, ride_out, sems)

        @pl.when(p == 0)
        def _():
            s_scr[...] = jnp.zeros(s_scr.shape, F32)

        for d in range(2):
            ss_ref[d, 0] = s_scr[d * n_state:(d + 1) * n_state]
        res = step2(*[r[...] for r in tok_refs], *[s_scr[u] for u in range(2 * n_state)], want_aux=True)
        for d in range(2):
            o_refs[d][...] = res[d]
            for i in range(n_aux):
                aux_ref[d, 0, i] = res[2 + 2 * n_state + d * n_aux + i]
        for u in range(2 * n_state):
            s_scr[u] = res[2 + u]
        if ride is not None:
            @pl.when(p == nc - 1)
            def _():
                ride.wait(ride_in, ride_out, sems)

    def tok_spec(a, d):
        return pl.BlockSpec((CHUNK, a.shape[1]), lambda p: (_chunk_of(d, p, nctx, nc), 0))

    return pl.pallas_call(
        body, grid=(nc,), in_specs=[tok_spec(a, d) for d in range(2) for a in toks] + rd.in_specs,
        out_specs=[pl.BlockSpec((CHUNK, out_width), lambda p: (_chunk_of(0, p, nctx, nc), 0)),
                   pl.BlockSpec((CHUNK, out_width), lambda p: (_chunk_of(1, p, nctx, nc), 0)),
                   pl.BlockSpec((2, 1, n_state, 128, 128), lambda p: (0, p, 0, 0, 0)),
                   pl.BlockSpec((2, 1, n_aux, CHUNK, CHUNK), lambda p: (0, p, 0, 0, 0))] + rd.out_specs,
        out_shape=[jax.ShapeDtypeStruct((t, out_width), F32), jax.ShapeDtypeStruct((t, out_width), F32),
                   jax.ShapeDtypeStruct((2, nc, n_state, 128, 128), F32),
                   jax.ShapeDtypeStruct((2, nc, n_aux, CHUNK, CHUNK), F32)] + rd.out_shape,
        scratch_shapes=[pltpu.VMEM((2 * n_state, 128, 128), F32)] + rd.scratch,
        compiler_params=_cp(("arbitrary",)), name=name)(*toks, *toks, *rd.arrs)


ADA_TN = 512


def _ada_fwd(cc, ada_w4, ada_b):
    per = ada_w4.shape[2] // ADA_TN
    n = N_PLANE * ada_w4.shape[2]

    def body(c_ref, w_ref, b_ref, o_ref):
        s = _silu(c_ref[...]).astype(_MXU_DTYPE)
        o_ref[...] = jnp.dot(s, w_ref[0].astype(_MXU_DTYPE), preferred_element_type=F32) + b_ref[...]

    return pl.pallas_call(
        body, grid=(n // ADA_TN,),
        in_specs=[pl.BlockSpec((8, D_MODEL), lambda j: (0, 0)),
                  pl.BlockSpec((1, D_MODEL, ADA_TN), lambda j: (j // per, 0, j % per)),
                  pl.BlockSpec((1, ADA_TN), lambda j: (0, j))],
        out_specs=pl.BlockSpec((8, ADA_TN), lambda j: (0, j)), out_shape=jax.ShapeDtypeStruct((8, n), F32),
        compiler_params=_cp(("parallel",)), name="ada_fwd")(cc, ada_w4, ada_b)


def _ada_bwd(cc, ada_w4, dmods):
    per = ada_w4.shape[2] // ADA_TN
    n = N_PLANE * ada_w4.shape[2]
    nj = n // ADA_TN

    def body(c_ref, w_ref, g_ref, dw_ref, db_ref, dc_ref):
        j = pl.program_id(0)
        g = g_ref[...]
        row = lax.broadcasted_iota(jnp.int32, g.shape, 0)
        g = jnp.where(row < 2, g, 0.0)
        s, pull = jax.vjp(_silu, c_ref[...])
        dw_ref[0] = lax.dot_general(s.astype(_MXU_DTYPE), g.astype(_MXU_DTYPE), _TN,
                                    preferred_element_type=F32).astype(dw_ref.dtype)
        db_ref[...] = jnp.sum(g, axis=0, keepdims=True)
        ds = lax.dot_general(g.astype(_MXU_DTYPE), w_ref[0].astype(_MXU_DTYPE), _NT, preferred_element_type=F32)

        @pl.when(j == 0)
        def _():
            dc_ref[...] = ds

        @pl.when(j > 0)
        def _():
            dc_ref[...] += ds

        @pl.when(j == nj - 1)
        def _():
            dc_ref[...] = pull(dc_ref[...])[0]

    wspec = pl.BlockSpec((1, D_MODEL, ADA_TN), lambda j: (j // per, 0, j % per))
    return pl.pallas_call(
        body, grid=(nj,),
        in_specs=[pl.BlockSpec((8, D_MODEL), lambda j: (0, 0)), wspec, pl.BlockSpec((8, ADA_TN), lambda j: (0, j))],
        out_specs=[wspec, pl.BlockSpec((1, ADA_TN), lambda j: (0, j)), pl.BlockSpec((8, D_MODEL), lambda j: (0, 0))],
        out_shape=[jax.ShapeDtypeStruct(ada_w4.shape, _GRAD_DTYPE), jax.ShapeDtypeStruct((1, n), F32),
                   jax.ShapeDtypeStruct((8, D_MODEL), F32)],
        compiler_params=_cp(("arbitrary",)), name="ada_bwd")(cc, ada_w4, dmods)


def _tail(h1, ff, mods, wf, tgt, nctx_t, tl):
    t = h1.shape[0]
    nt = t // tl

    def loss_fn(valid, h1v, ffv, g8, w, tg):
        h2 = h1v + g8[0:1] * ffv
        y = _rms(h2) * w
        err = (y - tg) ** 2
        return 0.5 * jnp.sum(jnp.mean(err, axis=-1, keepdims=True), axis=0, keepdims=True) * valid

    def body(h1_ref, ff_ref, g_ref, w_ref, t_ref, loss_ref, dh_ref, dff_ref, dg_ref, dw_ref):
        i = pl.program_id(0)
        valid = jnp.where(i < nctx_t, 0.0, 1.0)
        lv, pull = jax.vjp(functools.partial(loss_fn, valid), h1_ref[...], ff_ref[...].astype(F32), g_ref[...],
                           w_ref[...], t_ref[...])
        dh, dff, dg, dw, _ = pull(jnp.ones((1, 1), F32))
        dh_ref[...] = dh
        dff_ref[...] = dff.astype(dff_ref.dtype)
        lb = jnp.broadcast_to(lv, loss_ref.shape)

        @pl.when(i == 0)
        def _():
            loss_ref[...] = lb
            dg_ref[...] = dg
            dw_ref[...] = dw

        @pl.when(i > 0)
        def _():
            loss_ref[...] += lb
            dg_ref[...] += dg
            dw_ref[...] += dw

    tok = pl.BlockSpec((tl, D_MODEL), lambda i: (i, 0))
    return pl.pallas_call(
        body, grid=(nt,),
        in_specs=[tok, tok, pl.BlockSpec((8, D_MODEL), lambda i: (0, 5)), pl.BlockSpec((1, D_MODEL), lambda i: (0, 0)),
                  pl.BlockSpec((tl, D_MODEL), lambda i: (jnp.maximum(i - nctx_t, 0), 0))],
        out_specs=[pl.BlockSpec((8, 128), lambda i: (0, 0)), tok, tok, pl.BlockSpec((8, D_MODEL), lambda i: (0, 0)),
                   pl.BlockSpec((1, D_MODEL), lambda i: (0, 0))],
        out_shape=[jax.ShapeDtypeStruct((8, 128), F32), jax.ShapeDtypeStruct((t, D_MODEL), F32),
                   jax.ShapeDtypeStruct((t, D_MODEL), _MXU_DTYPE), jax.ShapeDtypeStruct((8, D_MODEL), F32),
                   jax.ShapeDtypeStruct((1, D_MODEL), F32)],
        compiler_params=_cp(("arbitrary",)), name="tail_loss")(h1, ff, mods, wf, tgt)


def _pack_w_in(w4):
    ns = w4.shape[2]
    placed = []
    for s0, s1, p0 in IN_SEGMENTS:
        for j in range(N_PLANE):
            lo, hi = max(s0, j * ns), min(s1, (j + 1) * ns)
            if lo < hi:
                placed.append((p0 + lo - s0, w4[j][:, lo - j * ns:hi - j * ns]))
    placed.sort(key=lambda e: e[0])
    pieces, end = [], 0
    for pos, piece in placed:
        assert pos == end, (pos, end)
        pieces.append(piece)
        end = pos + piece.shape[1]
    pieces.append(jnp.zeros((w4.shape[1], P_TOTAL - end), w4.dtype))
    return jnp.concatenate(pieces, axis=1)


def _unpack_w_in(g):
    ns = D_IN_PROJ // N_PLANE
    shards = []
    for j in range(N_PLANE):
        pieces = []
        for s0, s1, p0 in IN_SEGMENTS:
            lo, hi = max(s0, j * ns), min(s1, (j + 1) * ns)
            if lo < hi:
                pieces.append(g[:, p0 + lo - s0:p0 + hi - s0])
        shards.append(jnp.concatenate(pieces, axis=1))
    return jnp.stack(shards)


LATE_WEIGHTS = ['w_br_gdn', 'w_br_ssm', 'w_out', 'w_ffn_in', 'w_ffn_out']
COL_STACKED = ('ada_w', 'w_in', 'w_ffn_in')


def _from_shards(n, g):
    if n in COL_STACKED:
        return g
    if SHARD_AXIS[n] == 0:
        return g.reshape(N_PLANE * g.shape[1], g.shape[2])
    return jnp.concatenate([g[j] for j in range(N_PLANE)], axis=1)


def _to_dest_blocks(n, g):
    if g.ndim == 3:
        return g
    if SHARD_AXIS[n] == 0:
        return g.reshape(N_PLANE, g.shape[0] // N_PLANE, g.shape[1])
    sz = g.shape[1] // N_PLANE
    return jnp.stack([g[:, j * sz:(j + 1) * sz] for j in range(N_PLANE)])


def _local_step(x, c, ctx, tgt, W, late_shards=None):
    distributed = late_shards is not None
    lc, l = ctx.shape[0], x.shape[0]
    t = lc + l
    tl = 256
    assert lc == tl and l % tl == 0 and lc % CHUNK == 0
    nt, nctx_t, nctx = t // tl, lc // tl, lc // CHUNK
    act = _MXU_DTYPE
    r1 = lambda v: v.reshape(1, -1)

    xt = jnp.concatenate([ctx, x], axis=0)
    cc = jnp.concatenate([c, r1(W['c_ctx']), jnp.zeros((6, D_MODEL), F32)], axis=0)
    mods = _ada_fwd(cc, W['ada_w'], r1(W['ada_b']))

    def mod(kk):
        return Op(mods, (8, D_MODEL), lambda j, i, kk=kk: (0, kk))

    def tokop(arr, w=D_MODEL, off=0, tl_=tl):
        if not isinstance(arr, list) and arr.ndim == 3:
            return Op(arr, (2, tl_, w), lambda j, i: (0, i, off + j), load=_sum_dirs)
        return Op(arr, (tl_, w), lambda j, i: (i, off + j))

    def outop(n, dtype, w=D_MODEL, tl_=tl):
        return Op(((t, n), dtype), (tl_, w), lambda j, i: (i, j))

    def parop(arr, w, off=0):
        return Op(arr, (arr.shape[0], w), lambda j, i: (0, off + j))

    def parout(rows, n, w):
        return Op(((rows, n), F32), (rows, w), lambda j, i: (0, j))

    n1w = r1(W['norm1_w'])
    pre_fn = functools.partial(_prenorm_fn, nctx_t)
    pre_tok, pre_par = [tokop(xt)], [parop(n1w, D_MODEL), mod(1), mod(0)]
    (a,) = _tw_fwd("prenorm_fwd", pre_fn, (1, nt), pre_tok + pre_par, [outop(D_MODEL, act)])
    wp = _pack_w_in(W['w_in'])
    proj = _matmul(a, wp, 'nn', "in_proj")

    gcw, gcb = W['gdn_conv_w'], r1(W['gdn_conv_b'])
    scw, scb = W['ssm_conv_w'], r1(W['ssm_conv_b'])
    conv_parts = {}

    def conv_part(name, mode, poff, cw, cb, woff, width):
        fn = functools.partial(_conv_fn, nctx_t, mode)
        tok_ = [tokop(proj, 512, poff // 512)]
        par_ = [parop(cw, 512, woff // 512), parop(cb, 512, woff // 512)]
        conv_parts[name] = (fn, tok_, par_, width)
        (o,) = _tw_fwd("conv_" + name, fn, (width // 512, nt), tok_ + par_, [outop(width, F32, 512)])
        return o

    q = conv_part('q', 'q', P_QKV, gcw, gcb, 0, 1024)
    k = conv_part('k', 'k', P_QKV + 1024, gcw, gcb, 1024, 1024)
    v = conv_part('v', 'none', P_QKV + 2048, gcw, gcb, 2048, 1024)
    xs = conv_part('xs', 'none', P_XBC, scw, scb, 0, 2048)
    bm = conv_part('bm', 'none', P_XBC + 2048, scw, scb, 2048, 512)
    cm = conv_part('cm', 'none', P_XBC + 2560, scw, scb, 2560, 512)

    z16, z32 = jnp.zeros((16,), F32), jnp.zeros((32,), F32)
    p0 = jnp.concatenate([W['gdn_a_log'].reshape(-1), jnp.zeros((112,), F32)]).reshape(1, 128)
    p1 = jnp.concatenate([W['gdn_dt_bias'].reshape(-1), z16, W['ssm_dt_bias'].reshape(-1), z32]).reshape(1, 128)
    alog = jnp.concatenate([z32, W['ssm_a_log'].reshape(-1), z32]).reshape(1, 128)
    act_tok, act_par = [tokop(proj, 128, P_SMALL // 128)], [parop(p0, 128), parop(p1, 128)]
    (sp,) = _tw_fwd("small_act", _act_fn, (1, nt), act_tok + act_par, [outop(128, F32, 128)])

    gather_late = PlaneExchange([late_shards[n] for n in LATE_WEIGHTS], gather=True) if distributed else None
    o_f, o_b, ss, tri, *late = _scan2_fwd("gdn_scan_fwd", _gdn_step2, [q, k, v, sp], 1024, GDN_HEADS, nctx,
                                          GDN_HEADS, ride=gather_late)
    if distributed:
        W = dict(W, **{n: _from_shards(n, g) for n, g in zip(LATE_WEIGHTS, late)})
    ssd_consts = _ssd_consts()
    y2, hs = _scan_fwd("ssd_scan_fwd", _ssd_step, [xs, bm, cm, sp], [alog], ssd_consts, 2048, SSM_PAIRS, nctx)

    tlm = 128
    ntm = t // tlm
    gnw = r1(W['gdn_norm_w'])
    dl = jnp.repeat(W['ssm_d'], 64).reshape(1, 2048)
    snw = r1(W['ssm_norm_w'])
    mg_tok = [tokop([o_f, o_b], 1024, 0, tlm), tokop(proj, 1024, P_ZG // 1024, tlm)]
    mg_par = [parop(gnw, 128)]
    (og,) = _tw_fwd("mix_gdn", _mixg_fn, (1, ntm), mg_tok + mg_par, [outop(1024, act, 1024, tlm)])
    ms_tok = [tokop(y2, 2048, 0, tlm), tokop(xs, 2048, 0, tlm), tokop(proj, 2048, P_ZS // 2048, tlm)]
    ms_par = [parop(dl, 2048), parop(snw, 2048)]
    (yg,) = _tw_fwd("mix_ssm", _mixs_fn, (1, ntm), ms_tok + ms_par, [outop(2048, act, 2048, tlm)])

    pg = _matmul(og, W['w_br_gdn'], 'nn', "br_gdn")
    ps = _matmul(yg, W['w_br_ssm'], 'nn', "br_ssm")
    mr_tok = [tokop(proj, 1024, P_GATE // 1024), tokop(proj, 1024, P_GATE // 1024 + 1), tokop(pg), tokop(ps)]
    (mrg,) = _tw_fwd("merge", _merge_fn, (1, nt), mr_tok, [outop(1024, act)])
    mo = _matmul(mrg, W['w_out'], 'nn', "w_out")

    n2w = r1(W['norm2_w'])
    n2_tok, n2_par = [tokop(xt), tokop(mo)], [mod(2), parop(n2w, D_MODEL), mod(4), mod(3)]
    h1, f = _tw_fwd("norm2_fwd", _norm2_fn, (1, nt), n2_tok + n2_par, [outop(1024, F32), outop(1024, act)])
    u = _matmul(f, W['w_ffn_in'], 'nn', "ffn_in")
    swb = D_FF // 2
    sw_tok = [tokop(u, swb, 0), tokop(u, swb, D_FF // swb)]
    (sw,) = _tw_fwd("swiglu", _swiglu_fn, (D_FF // swb, nt), sw_tok, [outop(D_FF, act, swb)])
    ff = _matmul(sw, W['w_ffn_out'], 'nn', "ffn_out")

    loss8, dh1, dff, dg2, dnf = _tail(h1, ff, mods, r1(W['norm_f_w']), tgt, nctx_t, tl)
    loss = loss8[0, 0]

    G = {}
    G['norm_f_w'] = dnf.reshape(-1)
    G['w_ffn_out'] = _matmul(sw, dff, 'tn', "d_ffn_out", out_dtype=_GRAD_DTYPE)
    dsw = _matmul(dff, W['w_ffn_out'], 'nt', "d_sw")
    dug, duu = _tw_bwd("swiglu_bwd", _swiglu_fn, (D_FF // swb, nt), sw_tok, [], [[tokop(dsw, swb)]],
                       [outop(D_FF, act, swb), outop(D_FF, act, swb)], [])
    du = jnp.concatenate([dug, duu], axis=1)
    G['w_ffn_in'] = _matmul(f, du, 'tn', "d_ffn_in", out_dtype=_GRAD_DTYPE, stacked_out=True)
    df = _matmul(du, W['w_ffn_in'], 'nt', "d_f")
    dxt1, dmo, dg1, dn2, dsc2, dsh2 = _tw_bwd(
        "norm2_bwd", _norm2_fn, (1, nt), n2_tok, n2_par, [[tokop(dh1)], [tokop(df)]],
        [outop(1024, F32), outop(1024, act)],
        [parout(8, 1024, 1024), parout(1, 1024, 1024), parout(8, 1024, 1024), parout(8, 1024, 1024)])
    G['norm2_w'] = dn2.reshape(-1)
    G['w_out'] = _matmul(mrg, dmo, 'tn', "d_w_out", out_dtype=_GRAD_DTYPE)
    dmrg = _matmul(dmo, W['w_out'], 'nt', "d_mrg")
    dgg, dgs, dpg, dps = _tw_bwd("merge_bwd", _merge_fn, (1, nt), mr_tok, [], [[tokop(dmrg)]],
                                 [outop(1024, act), outop(1024, act), outop(1024, act), outop(1024, act)], [])
    G['w_br_gdn'] = _matmul(og, dpg, 'tn', "d_br_gdn", out_dtype=_GRAD_DTYPE)
    G['w_br_ssm'] = _matmul(yg, dps, 'tn', "d_br_ssm", out_dtype=_GRAD_DTYPE)
    dog = _matmul(dpg, W['w_br_gdn'], 'nt', "d_og")
    dyg = _matmul(dps, W['w_br_ssm'], 'nt', "d_yg")

    do, dzg, dgnw = _tw_bwd("mix_gdn_bwd", _mixg_fn, (1, ntm), mg_tok, mg_par, [[tokop(dog, 1024, 0, tlm)]],
                            [outop(1024, F32, 1024, tlm), outop(1024, act, 1024, tlm)], [parout(1, 128, 128)])
    G['gdn_norm_w'] = dgnw.reshape(-1)
    dy, dxs_a, dzs, ddl, dsnw = _tw_bwd(
        "mix_ssm_bwd", _mixs_fn, (1, ntm), ms_tok, ms_par, [[tokop(dyg, 2048, 0, tlm)]],
        [outop(2048, F32, 2048, tlm), outop(2048, F32, 2048, tlm), outop(2048, act, 2048, tlm)],
        [parout(1, 2048, 2048), parout(1, 2048, 2048)])
    G['ssm_d'] = ddl.reshape(SSM_HEADS, 64).sum(axis=1)
    G['ssm_norm_w'] = dsnw.reshape(-1)

    dq2, dk2, dv2, dsp_g = _scan_bwd("gdn_scan_bwd", _gdn_step, [q, k, v, sp], [], ss, do, GDN_HEADS, nctx, aux=tri)
    received = {}
    scatter_late = (PlaneExchange([_to_dest_blocks(n, G.pop(n)) for n in LATE_WEIGHTS], gather=False)
                    if distributed else None)
    dxs2, dbm2, dcm2, dsp_s, dalog, *got = _scan_bwd("ssd_scan_bwd", _ssd_step, [xs, bm, cm, sp], [alog], hs, dy,
                                                     SSM_PAIRS, nctx, ride=scatter_late, consts=ssd_consts)
    received.update(zip(LATE_WEIGHTS, got))
    G['ssm_a_log'] = dalog[0, 32:96].reshape(2, SSM_HEADS)

    dconv_w, dconv_b, dpre = {}, {}, {}

    def conv_bwd(name, cot_ops):
        fn, tok_, par_, width = conv_parts[name]
        dpre[name], dconv_w[name], dconv_b[name] = _tw_bwd(
            "conv_" + name + "_bwd", fn, (width // 512, nt), tok_, par_, [cot_ops], [outop(width, act, 512)],
            [parout(3, width, 512), parout(1, width, 512)])

    conv_bwd('q', [tokop(dq2, 512)])
    conv_bwd('k', [tokop(dk2, 512)])
    conv_bwd('v', [tokop(dv2, 512)])
    conv_bwd('xs', [tokop(dxs2, 512), tokop(dxs_a, 512)])
    conv_bwd('bm', [tokop(dbm2, 512)])
    conv_bwd('cm', [tokop(dcm2, 512)])
    G['gdn_conv_w'] = jnp.concatenate([dconv_w['q'], dconv_w['k'], dconv_w['v']], axis=1)
    G['gdn_conv_b'] = jnp.concatenate([dconv_b['q'], dconv_b['k'], dconv_b['v']], axis=1).reshape(-1)
    G['ssm_conv_w'] = jnp.concatenate([dconv_w['xs'], dconv_w['bm'], dconv_w['cm']], axis=1)
    G['ssm_conv_b'] = jnp.concatenate([dconv_b['xs'], dconv_b['bm'], dconv_b['cm']], axis=1).reshape(-1)

    dsmall, dp0, dp1 = _tw_bwd("small_act_bwd", _act_fn, (1, nt), act_tok, act_par,
                               [[tokop(dsp_g, 128), tokop(dsp_s, 128)]], [outop(128, act, 128)],
                               [parout(1, 128, 128), parout(1, 128, 128)])
    G['gdn_a_log'] = dp0[0, 0:16].reshape(2, GDN_HEADS)
    G['gdn_dt_bias'] = dp1[0, 0:16].reshape(2, GDN_HEADS)
    G['ssm_dt_bias'] = dp1[0, 32:96].reshape(2, SSM_HEADS)

    zpad = jnp.zeros((t, P_TOTAL - P_SMALL - 128), act)
    dproj = jnp.concatenate([dpre['q'], dpre['k'], dpre['v'], dzg, dzs, dpre['xs'], dpre['bm'], dpre['cm'],
                             dgg, dgs, dsmall, zpad], axis=1)
    G['w_in'] = _unpack_w_in(_matmul(a, dproj, 'tn', "d_w_in", out_dtype=_GRAD_DTYPE))
    if distributed:
        da, (received['w_in'],) = _matmul(dproj, wp, 'nt', "d_a", ride=PlaneExchange([G.pop('w_in')], gather=False))
    else:
        da = _matmul(dproj, wp, 'nt', "d_a")

    gx_out = Op(((l, D_MODEL), F32), (tl, D_MODEL), lambda j, i: (jnp.maximum(i - nctx_t, 0), 0))
    grad_x, dn1, dsc1, dsh1 = _tw_bwd(
        "prenorm_bwd", pre_fn, (1, nt), pre_tok, pre_par, [[tokop(da)]], [gx_out],
        [parout(1, 1024, 1024), parout(8, 1024, 1024), parout(8, 1024, 1024)], tok_add=tokop(dxt1),
        sem=("arbitrary", "arbitrary"))
    G['norm1_w'] = dn1.reshape(-1)
    dmods = jnp.concatenate([dsh1, dsc1, dg1, dsh2, dsc2, dg2], axis=1)
    G['ada_w'], dab, dcc = _ada_bwd(cc, W['ada_w'], dmods)
    G['ada_b'] = dab.reshape(-1)
    G['c_ctx'] = dcc[1]
    return loss, grad_x, G, received


def _row_tile(r, c):
    for tr in (512, 256, 128, 64, 32, 16, 8):
        if r % tr == 0 and tr * c * 4 <= (1 << 20):
            return tr
    return r


def _sum4(name, rv):
    _, r, c = rv.shape
    tr = _row_tile(r, c)

    def body(r_ref, o_ref):
        o_ref[...] = ((r_ref[0].astype(F32) + r_ref[1].astype(F32)) + r_ref[2].astype(F32)) + r_ref[3].astype(F32)

    return pl.pallas_call(
        body, grid=(r // tr,), in_specs=[pl.BlockSpec((N_PLANE, tr, c), lambda i: (0, i, 0))],
        out_specs=pl.BlockSpec((tr, c), lambda i: (i, 0)), out_shape=jax.ShapeDtypeStruct((r, c), F32),
        compiler_params=_cp(("parallel",)), name=name)(rv)


def _adamw(name, w, m, v, p, q):
    r, c = w.shape
    tr = _row_tile(r, c)

    def body(w_ref, m_ref, v_ref, p_ref, q_ref, g_ref, d_ref, mo_ref, vo_ref):
        g = p_ref[...] + q_ref[...]
        mn = ADAM_B1 * m_ref[...] + (1.0 - ADAM_B1) * g
        vn = ADAM_B2 * v_ref[...] + (1.0 - ADAM_B2) * jnp.square(g)
        m_hat = mn / (1.0 - ADAM_B1 ** ADAM_STEP)
        v_hat = vn / (1.0 - ADAM_B2 ** ADAM_STEP)
        g_ref[...] = g
        d_ref[...] = -ADAM_LR * (m_hat / (jnp.sqrt(v_hat) + ADAM_EPS) + ADAM_WD * w_ref[...])
        mo_ref[...] = mn
        vo_ref[...] = vn

    spec = pl.BlockSpec((tr, c), lambda i: (i, 0))
    return pl.pallas_call(
        body, grid=(r // tr,), in_specs=[spec] * 5, out_specs=[spec] * 4,
        out_shape=[jax.ShapeDtypeStruct((r, c), F32)] * 4, compiler_params=_cp(("parallel",)), name=name)(w, m, v, p, q)


SMALL_ROWS = 24


def _pack_small(d):
    v = jnp.concatenate([d[n].reshape(-1) for n in SMALL])
    return jnp.pad(v, (0, SMALL_ROWS * 1024 - v.shape[0])).reshape(SMALL_ROWS, 1024)


def _unpack_small(buf, like):
    v = buf.reshape(-1)
    out, off = {}, 0
    for n in SMALL:
        sz = like[n].size
        out[n] = v[off:off + sz].reshape(like[n].shape)
        off += sz
    return out


def kernel(x, c, ctx, c_ctx, ada_w, ada_b, norm1_w, w_in, gdn_conv_w, gdn_conv_b, gdn_a_log, gdn_dt_bias, gdn_norm_w, ssm_conv_w, ssm_conv_b, ssm_a_log, ssm_dt_bias, ssm_d, ssm_norm_w, w_br_gdn, w_br_ssm, w_out, norm2_w, w_ffn_in, w_ffn_out, norm_f_w, loss_target, m_c_ctx, m_ada_w, m_ada_b, m_norm1_w, m_w_in, m_gdn_conv_w, m_gdn_conv_b, m_gdn_a_log, m_gdn_dt_bias, m_gdn_norm_w, m_ssm_conv_w, m_ssm_conv_b, m_ssm_a_log, m_ssm_dt_bias, m_ssm_d, m_ssm_norm_w, m_w_br_gdn, m_w_br_ssm, m_w_out, m_norm2_w, m_w_ffn_in, m_w_ffn_out, m_norm_f_w, v_c_ctx, v_ada_w, v_ada_b, v_norm1_w, v_w_in, v_gdn_conv_w, v_gdn_conv_b, v_gdn_a_log, v_gdn_dt_bias, v_gdn_norm_w, v_ssm_conv_w, v_ssm_conv_b, v_ssm_a_log, v_ssm_dt_bias, v_ssm_d, v_ssm_norm_w, v_w_br_gdn, v_w_br_ssm, v_w_out, v_norm2_w, v_w_ffn_in, v_w_ffn_out, v_norm_f_w):
    args = dict(locals())
    wl = {n: args[n] for n in WEIGHTS}
    ml = {n: args['m_' + n] for n in WEIGHTS}
    vl = {n: args['v_' + n] for n in WEIGHTS}

    def nodepth(n, a):
        return a if n in ('c_ctx', 'norm_f_w') else a[0]

    shard = {n: nodepth(n, wl[n]).astype(_MXU_DTYPE if n in MXU_WEIGHTS else F32) for n in SHARDED}
    first = [n for n in SHARDED if n not in LATE_WEIGHTS]
    W = {n: nodepth(n, wl[n]) for n in SMALL}
    for n, g in zip(first, _gather_two_level("all_gather_plane", [shard[n] for n in first])):
        W[n] = _from_shards(n, g)

    loss_local, grad_x, G, received = _local_step(x[0], c, ctx[0], loss_target[0], W,
                                                  late_shards={n: shard[n] for n in LATE_WEIGHTS})
    loss = lax.psum(loss_local, ("x", "y", "c"))

    small_g = _pack_small(G)
    last = [n for n in SHARDED if n not in received]
    send = [_to_dest_blocks(n, G[n]) for n in last] + [jnp.broadcast_to(small_g[None], (N_PLANE,) + small_g.shape)]
    received.update(zip(last + ['small'], _plane_exchange("scatter_plane", send, gather=False)))
    names = SHARDED + ['small']
    plane_sum = [_sum4("sum4_" + n, received[n]) for n in names]
    other = _swap_sibling(plane_sum)

    wd = {n: nodepth(n, wl[n]) for n in WEIGHTS}
    md = {n: nodepth(n, ml[n]) for n in WEIGHTS}
    vd = {n: nodepth(n, vl[n]) for n in WEIGHTS}
    res = {}
    for n, p, q in zip(names, plane_sum, other):
        if n == 'small':
            outs = _adamw("adamw_small", _pack_small(wd), _pack_small(md), _pack_small(vd), p, q)
            unpacked = [_unpack_small(o, wd) for o in outs]
            for sn in SMALL:
                res[sn] = [u[sn].reshape(wl[sn].shape) for u in unpacked]
        else:
            outs = _adamw("adamw_" + n, wd[n], md[n], vd[n], p, q)
            res[n] = [o.reshape(wl[n].shape) for o in outs]
    flat = [res[n][kind] for kind in range(4) for n in WEIGHTS]
    return (loss, grad_x[None], *flat)
```

```python
import functools

import jax
import jax.numpy as jnp
from jax import lax
from jax.experimental import pallas as pl
from jax.experimental.pallas import tpu as pltpu

F32 = jnp.float32
HI = lax.Precision.HIGHEST
_MXU_DTYPE = jnp.bfloat16
_SCAN_DTYPE = jnp.bfloat16
_GRAD_DTYPE = jnp.bfloat16

D_MODEL = 1024
EPS = 1e-6
CHUNK = 64
GRID_W = 64
GDN_HEADS = 8
GDN_DK = 128
SSM_HEADS = 32
SSM_PAIRS = 16
SSD_GROUP = 4
D_FF = 2816
D_IN_PROJ = 11360
N_PLANE = 4
P_QKV, P_ZG, P_ZS, P_XBC, P_GATE, P_SMALL, P_TOTAL = 0, 3072, 4096, 6144, 9216, 11264, 11776
IN_SEGMENTS = [(0, 3072, P_QKV), (3072, 4096, P_ZG), (4096, 4112, P_SMALL), (4112, 4128, P_SMALL + 16),
               (4128, 6176, P_ZS), (6176, 9248, P_XBC), (9248, 9312, P_SMALL + 32), (9312, 11360, P_GATE)]
ADAM_LR, ADAM_B1, ADAM_B2, ADAM_EPS, ADAM_WD, ADAM_STEP = 0.001, 0.9, 0.999, 1e-08, 0.01, 10

VMEM_LIMIT = 48 * 1024 * 1024

WEIGHTS = ['c_ctx', 'ada_w', 'ada_b', 'norm1_w', 'w_in', 'gdn_conv_w', 'gdn_conv_b', 'gdn_a_log', 'gdn_dt_bias',
           'gdn_norm_w', 'ssm_conv_w', 'ssm_conv_b', 'ssm_a_log', 'ssm_dt_bias', 'ssm_d', 'ssm_norm_w', 'w_br_gdn',
           'w_br_ssm', 'w_out', 'norm2_w', 'w_ffn_in', 'w_ffn_out', 'norm_f_w']
SHARD_AXIS = {'ada_w': 1, 'w_in': 1, 'gdn_conv_w': 1, 'ssm_conv_w': 1, 'w_br_gdn': 0, 'w_br_ssm': 0, 'w_out': 0,
              'w_ffn_in': 1, 'w_ffn_out': 0}
SHARDED = [n for n in WEIGHTS if n in SHARD_AXIS]
SMALL = [n for n in WEIGHTS if n not in SHARD_AXIS]
MXU_WEIGHTS = ['ada_w', 'w_in', 'w_br_gdn', 'w_br_ssm', 'w_out', 'w_ffn_in', 'w_ffn_out']


def _cp(sem):
    return pltpu.CompilerParams(dimension_semantics=sem, vmem_limit_bytes=VMEM_LIMIT)


def _pick(n, cands):
    for c in cands:
        if n % c == 0:
            return c
    raise ValueError(f"no tile for {n}")


HBM = pl.BlockSpec(memory_space=pl.ANY)
MESH = pl.DeviceIdType.MESH


def _plane_peers():
    x, y, c = lax.axis_index("x"), lax.axis_index("y"), lax.axis_index("c")
    return (x, y, c), [(1 - x, y, c), (x, 1 - y, c), (1 - x, 1 - y, c)]


class PlaneExchange:
    def __init__(self, arrs, gather):
        self.arrs, self.gather, self.n = list(arrs), gather, len(arrs)
        self.in_specs = [HBM] * self.n
        self.out_specs = [HBM] * self.n
        self.out_shape = [jax.ShapeDtypeStruct((N_PLANE,) + a.shape if gather else a.shape, a.dtype) for a in self.arrs]
        self.scratch = [pltpu.SemaphoreType.DMA((3 * self.n,)), pltpu.SemaphoreType.DMA((3 * self.n,)),
                        pltpu.SemaphoreType.DMA((self.n,))]

    def _copies(self, ins, outs, sems):
        send_sems, recv_sems, local_sems = sems
        (x, y, c), peers = _plane_peers()
        me = 2 * x + y
        copies = []
        for ti in range(self.n):
            src = ins[ti] if self.gather else ins[ti].at[me]
            copies.append(pltpu.make_async_copy(src, outs[ti].at[me], local_sems.at[ti]))
            for kk, (px, py, pc) in enumerate(peers):
                src = ins[ti] if self.gather else ins[ti].at[2 * px + py]
                copies.append(pltpu.make_async_remote_copy(
                    src_ref=src, dst_ref=outs[ti].at[me], send_sem=send_sems.at[3 * ti + kk],
                    recv_sem=recv_sems.at[3 * ti + kk], device_id=(px, py, pc), device_id_type=MESH))
        return copies

    def start(self, ins, outs, sems):
        for cp in self._copies(ins, outs, sems):
            cp.start()

    def wait(self, ins, outs, sems):
        for cp in self._copies(ins, outs, sems):
            cp.wait()


def _plane_exchange(name, arrs, gather):
    ex = PlaneExchange(arrs, gather)
    n = ex.n

    def body(*refs):
        ins, outs, sems = refs[:n], refs[n:2 * n], refs[2 * n:]
        ex.start(ins, outs, sems)
        ex.wait(ins, outs, sems)

    return pl.pallas_call(body, in_specs=ex.in_specs, out_specs=ex.out_specs, out_shape=ex.out_shape,
                          scratch_shapes=ex.scratch, name=name)(*arrs)


def _gather_two_level(name, arrs):
    n = len(arrs)
    split = [a.shape[0] % 32 == 0 for a in arrs]

    def body(*refs):
        ins, outs = refs[:n], refs[n:2 * n]
        ici_send, ici_recv, d2d_send, d2d_recv, local_sems = refs[2 * n:]
        (x, y, c), peers = _plane_peers()
        me = 2 * x + y

        def part(ref, ti):
            if not split[ti]:
                return ref
            rows = arrs[ti].shape[0] // 2
            return ref.at[pl.ds(c * rows, rows)]

        local = [pltpu.make_async_copy(ins[ti], outs[ti].at[me], local_sems.at[ti]) for ti in range(n)]
        ici = [pltpu.make_async_remote_copy(
            src_ref=part(ins[ti], ti), dst_ref=part(outs[ti].at[me], ti), send_sem=ici_send.at[3 * ti + kk],
            recv_sem=ici_recv.at[3 * ti + kk], device_id=peer, device_id_type=MESH)
            for ti in range(n) for kk, peer in enumerate(peers)]
        for cp in local + ici:
            cp.start()
        d2d = []
        for ti in range(n):
            for kk, (px, py, pc) in enumerate(peers):
                ici[3 * ti + kk].wait_recv()
                if split[ti]:
                    piece = part(outs[ti].at[2 * px + py], ti)
                    cp = pltpu.make_async_remote_copy(
                        src_ref=piece, dst_ref=piece, send_sem=d2d_send.at[3 * ti + kk],
                        recv_sem=d2d_recv.at[3 * ti + kk], device_id=(x, y, 1 - c), device_id_type=MESH)
                    cp.start()
                    d2d.append(cp)
        for cp in ici:
            cp.wait_send()
        for cp in d2d:
            cp.wait()
        for cp in local:
            cp.wait()

    return pl.pallas_call(
        body, in_specs=[HBM] * n, out_specs=[HBM] * n,
        out_shape=[jax.ShapeDtypeStruct((N_PLANE,) + a.shape, a.dtype) for a in arrs],
        scratch_shapes=[pltpu.SemaphoreType.DMA((3 * n,))] * 4 + [pltpu.SemaphoreType.DMA((n,))], name=name)(*arrs)


def _swap_sibling(arrs):
    n = len(arrs)

    def body(*refs):
        ins, outs, send_sems, recv_sems = refs[:n], refs[n:2 * n], refs[2 * n], refs[2 * n + 1]
        x, y, c = lax.axis_index("x"), lax.axis_index("y"), lax.axis_index("c")
        copies = [pltpu.make_async_remote_copy(src_ref=ins[ti], dst_ref=outs[ti], send_sem=send_sems.at[ti],
                                               recv_sem=recv_sems.at[ti], device_id=(x, y, 1 - c), device_id_type=MESH)
                  for ti in range(n)]
        for cp in copies:
            cp.start()
        for cp in copies:
            cp.wait()

    return pl.pallas_call(
        body, in_specs=[HBM] * n, out_specs=[HBM] * n, out_shape=[jax.ShapeDtypeStruct(a.shape, a.dtype) for a in arrs],
        scratch_shapes=[pltpu.SemaphoreType.DMA((n,)), pltpu.SemaphoreType.DMA((n,))], name="swap_sibling")(*arrs)


MATMUL_VMEM_BUDGET = 36 * 1024 * 1024
TILE_CANDIDATES = (2944, 2816, 1408, 1024, 768, 512, 256, 128)
STEP_COST_BYTES = 1 << 20


def _matmul_tiles(m, n, k, ab, bb, ob, tn_fixed=None, tk_fixed=None):
    def cands(dim, whole_up_to):
        out = [c for c in TILE_CANDIDATES if dim % c == 0]
        if dim <= whole_up_to and dim not in out:
            out.append(dim)
        return out

    best = None
    for tm in cands(m, 3072):
        for tn in ([tn_fixed] if tn_fixed else cands(n, 3072)):
            for tk in ([tk_fixed] if tk_fixed else cands(k, 2048)):
                gi, gj, gk = m // tm, n // tn, k // tk
                vmem = 2 * (tm * tk * ab + tk * tn * bb + tm * tn * ob) + (tm * tn * 4 if gk > 1 else 0)
                if vmem > MATMUL_VMEM_BUDGET:
                    continue
                a_reads = 1 if gk == 1 else gj
                b_reads = 1 if (gk == 1 and gj == 1) else gi
                cost = (m * k * ab * a_reads + k * n * bb * b_reads + m * n * ob + gi * gj * gk * STEP_COST_BYTES)
                if best is None or cost < best[0]:
                    best = (cost, tm, tn, tk)
    assert best is not None, (m, n, k)
    return best[1:]


def _matmul(a, b, form, name, out_dtype=F32, stacked_out=False, ride=None):
    dims = {'nn': (((1,), (0,)), ((), ())), 'nt': (((1,), (1,)), ((), ())), 'tn': (((0,), (0,)), ((), ()))}[form]
    stacked_b = b.ndim == 3
    ns = b.shape[2] if stacked_b else None
    if form == 'nn':
        m, k = a.shape
        n = b.shape[0] * ns if stacked_b else b.shape[1]
    elif form == 'nt':
        m, k = a.shape
        n = b.shape[1] if stacked_b else b.shape[0]
    else:
        k, m = a.shape
        n = b.shape[1]
    ob = jnp.dtype(out_dtype).itemsize
    tm, tn, tk = _matmul_tiles(m, n, k, a.dtype.itemsize, b.dtype.itemsize, ob,
                               tn_fixed=(n // N_PLANE if stacked_out else ns if (stacked_b and form == 'nn') else None),
                               tk_fixed=(ns if (stacked_b and form == 'nt') else None))
    nk = k // tk
    grid = (m // tm, n // tn, nk)
    n_ride = ride.n if ride is not None else 0

    def body(*refs):
        a_ref, b_ref = refs[0], refs[1]
        ride_in = refs[2:2 + n_ride]
        o_ref = refs[2 + n_ride]
        ride_out = refs[3 + n_ride:3 + 2 * n_ride]
        acc_ref = refs[3 + 2 * n_ride]
        sems = refs[4 + 2 * n_ride:]
        i, j, kk = pl.program_id(0), pl.program_id(1), pl.program_id(2)
        if ride is not None:
            @pl.when((i == 0) & (j == 0) & (kk == 0))
            def _():
                ride.start(ride_in, ride_out, sems)

        def put(val):
            if stacked_out:
                o_ref[0] = val.astype(o_ref.dtype)
            else:
                o_ref[...] = val.astype(o_ref.dtype)

        bv = b_ref[0] if stacked_b else b_ref[...]
        part = lax.dot_general(a_ref[...].astype(_MXU_DTYPE), bv.astype(_MXU_DTYPE), dims, preferred_element_type=F32)
        if nk == 1:
            put(part)
        else:
            @pl.when(kk == 0)
            def _():
                acc_ref[...] = part

            @pl.when(kk > 0)
            def _():
                acc_ref[...] += part

            @pl.when(kk == nk - 1)
            def _():
                put(acc_ref[...])

        if ride is not None:
            @pl.when((i == grid[0] - 1) & (j == grid[1] - 1) & (kk == nk - 1))
            def _():
                ride.wait(ride_in, ride_out, sems)

    if form == 'nn':
        a_spec = pl.BlockSpec((tm, tk), lambda i, j, kk: (i, kk))
        b_spec = (pl.BlockSpec((1, tk, tn), lambda i, j, kk: (j, kk, 0)) if stacked_b
                  else pl.BlockSpec((tk, tn), lambda i, j, kk: (kk, j)))
    elif form == 'nt':
        a_spec = pl.BlockSpec((tm, tk), lambda i, j, kk: (i, kk))
        b_spec = (pl.BlockSpec((1, tn, tk), lambda i, j, kk: (kk, j, 0)) if stacked_b
                  else pl.BlockSpec((tn, tk), lambda i, j, kk: (j, kk)))
    else:
        a_spec = pl.BlockSpec((tk, tm), lambda i, j, kk: (kk, i))
        b_spec = pl.BlockSpec((tk, tn), lambda i, j, kk: (kk, j))
    if stacked_out:
        o_spec = pl.BlockSpec((1, tm, tn), lambda i, j, kk: (j, i, 0))
        o_shape = jax.ShapeDtypeStruct((N_PLANE, m, tn), out_dtype)
    else:
        o_spec = pl.BlockSpec((tm, tn), lambda i, j, kk: (i, j))
        o_shape = jax.ShapeDtypeStruct((m, n), out_dtype)
    acc_shape = (tm, tn) if nk > 1 else (8, 128)
    if ride is None:
        return pl.pallas_call(
            body, grid=grid, in_specs=[a_spec, b_spec], out_specs=o_spec, out_shape=o_shape,
            scratch_shapes=[pltpu.VMEM(acc_shape, F32)],
            compiler_params=_cp(("parallel", "parallel", "arbitrary")), name=name)(a, b)
    res = pl.pallas_call(
        body, grid=grid, in_specs=[a_spec, b_spec] + ride.in_specs, out_specs=[o_spec] + ride.out_specs,
        out_shape=[o_shape] + ride.out_shape, scratch_shapes=[pltpu.VMEM(acc_shape, F32)] + ride.scratch,
        compiler_params=_cp(("arbitrary", "arbitrary", "arbitrary")), name=name)(a, b, *ride.arrs)
    return res[0], res[1:]


class Op:
    def __init__(self, arr, bs, im, load=None):
        self.arr, self.bs, self.im = arr, bs, im
        self.arrs = list(arr) if isinstance(arr, list) else [arr]
        self.load = load or (lambda r: r[...].astype(F32))

    def spec(self):
        return pl.BlockSpec(self.bs, self.im)

    def value(self, it):
        return functools.reduce(lambda u, w: u + w, [self.load(next(it)) for _ in self.arrs])


def _op_specs(ops):
    return [op.spec() for op in ops for _ in op.arrs]


def _op_arrays(ops):
    return [a for op in ops for a in op.arrs]


def _sum_dirs(r):
    return r[0].astype(F32) + r[1].astype(F32)


def _tw_fwd(name, fn, grid, ins, outs):
    def body(*refs):
        info = (pl.program_id(0), pl.program_id(1))
        it = iter(refs)
        vals = [op.value(it) for op in ins]
        res = fn(info, *vals)
        for r, v in zip(it, res):
            r[...] = v.astype(r.dtype)

    return pl.pallas_call(
        body, grid=grid, in_specs=_op_specs(ins), out_specs=[op.spec() for op in outs],
        out_shape=[jax.ShapeDtypeStruct(*op.arr) for op in outs],
        compiler_params=_cp(("parallel", "arbitrary")), name=name)(*_op_arrays(ins))


def _tw_bwd(name, fn, grid, tok, par, cots, tok_out, par_out, tok_add=None, sem=("parallel", "arbitrary")):
    n_tok = len(tok)
    flat_cots = [op for group in cots for op in group]
    extra = [tok_add] if tok_add is not None else []
    out_ops = list(tok_out) + list(par_out)

    def body(*refs):
        info = (pl.program_id(0), pl.program_id(1))
        it = iter(refs)
        tok_v = [op.value(it) for op in tok]
        par_v = [op.value(it) for op in par]
        cot_v = [functools.reduce(lambda u, w: u + w, [op.value(it) for op in group]) for group in cots]
        add_v = [op.value(it) for op in extra]
        _, pull = jax.vjp(lambda *a: fn(info, *a), *tok_v, *par_v)
        grads = pull(tuple(cot_v))
        for i in range(n_tok):
            r = next(it)
            g = grads[i] + add_v[0] if (i == 0 and add_v) else grads[i]
            r[...] = g.astype(r.dtype)
        first = pl.program_id(1) == 0
        for i in range(len(par)):
            r = next(it)
            g = grads[n_tok + i]

            @pl.when(first)
            def _(r=r, g=g):
                r[...] = g

            @pl.when(jnp.logical_not(first))
            def _(r=r, g=g):
                r[...] += g

    ops = tok + par + flat_cots + extra
    return pl.pallas_call(
        body, grid=grid, in_specs=_op_specs(ops), out_specs=[op.spec() for op in out_ops],
        out_shape=[jax.ShapeDtypeStruct(*op.arr) for op in out_ops],
        compiler_params=_cp(sem), name=name)(*_op_arrays(ops))


def _silu(x):
    return x * jax.nn.sigmoid(x)


def _rms(x):
    return x * lax.rsqrt(jnp.mean(x * x, axis=-1, keepdims=True) + EPS)


@functools.partial(jax.custom_vjp, nondiff_argnums=(1,))
def _shift_rows(x, k):
    return pltpu.roll(x, k % x.shape[0], 0)


def _shift_rows_fwd(x, k):
    return _shift_rows(x, k), None


def _shift_rows_bwd(k, _, g):
    return (_shift_rows(g, -k),)


_shift_rows.defvjp(_shift_rows_fwd, _shift_rows_bwd)


def _prenorm_fn(nctx_t, info, x, w, sc8, sh8):
    is_ctx = info[1] < nctx_t
    sc = jnp.where(is_ctx, sc8[1:2], sc8[0:1])
    sh = jnp.where(is_ctx, sh8[1:2], sh8[0:1])
    return (_rms(x) * w * (1.0 + sc) + sh,)


def _conv_fn(nctx_t, mode, info, x, w, b):
    n, c = x.shape
    is_ctx = info[1] < nctx_t
    idx = lax.broadcasted_iota(jnp.int32, (n, 1), 0)
    rr = jnp.where(is_ctx, idx, idx % GRID_W)
    first = rr == 0
    last = rr == jnp.where(is_ctx, n - 1, GRID_W - 1)
    prev = jnp.where(first, 0.0, _shift_rows(x, 1))
    nxt = jnp.where(last, 0.0, _shift_rows(x, -1))
    y = b + prev * w[0:1] + x * w[1:2] + nxt * w[2:3]
    y = _silu(y)
    if mode == 'none':
        return (y,)
    scale = GDN_DK ** -0.5 if mode == 'q' else 1.0
    outs = []
    for h in range(c // 128):
        yh = y[:, h * 128:(h + 1) * 128]
        outs.append(yh * lax.rsqrt(jnp.sum(yh * yh, axis=-1, keepdims=True) + EPS) * scale)
    return (jnp.concatenate(outs, axis=1),)


def _act_fn(info, x, p0, p1):
    lane = lax.broadcasted_iota(jnp.int32, x.shape, 1)
    sp = jax.nn.softplus(x + p1)
    g = -jnp.exp(p0) * sp
    bt = jax.nn.sigmoid(x)
    return (jnp.where(lane < 16, g, jnp.where(lane < 32, bt, jnp.where(lane < 96, sp, 0.0))),)


def _mixg_fn(info, o, zg, gnw):
    outs = []
    for h in range(GDN_HEADS):
        outs.append(_rms(o[:, h * 128:(h + 1) * 128]) * gnw)
    return (jnp.concatenate(outs, axis=1) * _silu(zg),)


def _mixs_fn(info, y, xs, zs, dl, snw):
    yy = (y + dl * xs) * _silu(zs)
    outs = []
    for g in range(4):
        outs.append(_rms(yy[:, g * 512:(g + 1) * 512]))
    return (jnp.concatenate(outs, axis=1) * snw,)


def _merge_fn(info, gg, gs, pg, ps):
    return (jax.nn.sigmoid(gg) * pg + jax.nn.sigmoid(gs) * ps,)


def _norm2_fn(info, xt, mo, g8, w, sc8, sh8):
    h1 = xt + g8[0:1] * mo
    return (h1, _rms(h1) * w * (1.0 + sc8[0:1]) + sh8[0:1])


def _swiglu_fn(info, ug, uu):
    return (_silu(ug) * uu,)


_NN = (((1,), (0,)), ((), ()))
_NT = (((1,), (1,)), ((), ()))
_TN = (((0,), (0,)), ((), ()))


def _mmh(a, b):
    return lax.dot_general(a, b, _NN, precision=HI, preferred_element_type=F32)


def _dot1(a, b, dims):
    return lax.dot_general(a.astype(_SCAN_DTYPE), b.astype(_SCAN_DTYPE), dims, preferred_element_type=F32)


def _mm(a, b):
    return _dot1(a, b, _NN)


def _mm_nt(a, b):
    return _dot1(a, b, _NT)


def _mm_tn(a, b):
    return _dot1(a, b, _TN)


def _split2(a):
    hi = a.astype(_SCAN_DTYPE)
    return hi, (a - hi.astype(F32)).astype(_SCAN_DTYPE)


def _dot3(a, b, dims):
    ah, al = _split2(a)
    bh, bl = _split2(b)
    d = lambda u, w: lax.dot_general(u, w, dims, preferred_element_type=F32)
    return d(ah, bh) + (d(ah, bl) + d(al, bh))


def _order_masks(d):
    i = lax.broadcasted_iota(jnp.int32, (CHUNK, CHUNK), 0)
    j = lax.broadcasted_iota(jnp.int32, (CHUNK, CHUNK), 1)
    s = jnp.where(d == 0, 1, -1) * (i - j)
    return (s >= 0).astype(F32), (s > 0).astype(F32)


@jax.custom_vjp
def _unit_tri_inv(mats):
    i = lax.broadcasted_iota(jnp.int32, (CHUNK, CHUNK), 0)
    j = lax.broadcasted_iota(jnp.int32, (CHUNK, CHUNK), 1)
    eye = (i == j).astype(F32)
    ps = [-a for a in mats]
    ts = [eye + p for p in ps]
    for _ in range(5):
        ps = [_dot3(p, p, _NN) for p in ps]
        ts = [t + _dot3(t, p, _NN) for t, p in zip(ts, ps)]
    return tuple(ts)


def _uti_fwd(mats):
    ts = _unit_tri_inv(mats)
    return ts, ts


def _uti_bwd(ts, gs):
    inner = [_dot3(g, t, _NT) for g, t in zip(gs, ts)]
    return (tuple(-_dot3(t, u, _TN) for t, u in zip(ts, inner)),)


_unit_tri_inv.defvjp(_uti_fwd, _uti_bwd)


@jax.custom_vjp
def _unit_tri_inv_given(mats, ts):
    return ts


def _utig_fwd(mats, ts):
    return ts, ts


def _utig_bwd(ts, gs):
    return _uti_bwd(ts, gs)[0], tuple(jnp.zeros_like(t) for t in ts)


_unit_tri_inv_given.defvjp(_utig_fwd, _utig_bwd)


def _lane_col(blk, lane_idx):
    lane = lax.broadcasted_iota(jnp.int32, blk.shape, 1)
    return jnp.sum(jnp.where(lane == lane_idx, blk, 0.0), axis=1, keepdims=True)


def _decay_mat(cum, incl):
    cb = jnp.broadcast_to(cum, (CHUNK, CHUNK))
    return jnp.exp(jnp.minimum(cb - cb.T, 0.0)) * incl


def _gdn_chunks(dirs, streams, states, aux=None, want_aux=False, group=None):
    ns = len(dirs)
    masks = [_order_masks(d) for d in dirs]
    cum = [_mmh(masks[i][0], streams[i][3]) for i in range(ns)]
    tot = [jnp.sum(streams[i][3], axis=0, keepdims=True) for i in range(ns)]
    all_units = [(i, h) for i in range(ns) for h in range(GDN_HEADS)]
    group = group or len(all_units)
    cat = jnp.concatenate
    outs, new_states, ts_all = [], [], []
    for g0 in range(0, len(all_units), group):
        units = all_units[g0:g0 + group]
        us = range(len(units))
        st = states[g0:g0 + group]
        qs = [streams[i][0][:, h * 128:(h + 1) * 128] for i, h in units]
        ks = [streams[i][1][:, h * 128:(h + 1) * 128] for i, h in units]
        vs = [streams[i][2][:, h * 128:(h + 1) * 128] for i, h in units]
        gcum = [_lane_col(cum[i], dirs[i] * GDN_HEADS + h) for i, h in units]
        glast = [_lane_col(tot[i], dirs[i] * GDN_HEADS + h) for i, h in units]
        beta = [_lane_col(streams[i][3], 16 + dirs[i] * GDN_HEADS + h) for i, h in units]
        decay = [_decay_mat(gcum[u], masks[units[u][0]][0]) for u in us]
        egc = [jnp.exp(gcum[u]) for u in us]
        kb = [ks[u] * beta[u] for u in us]
        kq = [_mm_nt(cat([kb[u], qs[u]], axis=0), ks[u]) for u in us]
        mats = tuple(kq[u][:CHUNK] * decay[u] * masks[units[u][0]][1] for u in us)
        ts = _unit_tri_inv(mats) if aux is None else _unit_tri_inv_given(mats, tuple(aux[g0:g0 + group]))
        wu = [_mm(ts[u], cat([kb[u] * egc[u], vs[u] * beta[u]], axis=1)) for u in us]
        ws = [_mm(cat([wu[u][:, :128], qs[u] * egc[u]], axis=0), st[u]) for u in us]
        vn = [wu[u][:, 128:] - ws[u][:CHUNK] for u in us]
        outs += [ws[u][CHUNK:] + _mm(kq[u][CHUNK:] * decay[u], vn[u]) for u in us]
        new_states += [st[u] * jnp.exp(glast[u]) + _mm_tn(ks[u] * jnp.exp(glast[u] - gcum[u]), vn[u]) for u in us]
        ts_all += list(ts)
    per_stream = [cat(outs[i * GDN_HEADS:(i + 1) * GDN_HEADS], axis=1) for i in range(ns)]
    return (*per_stream, *new_states, *(ts_all if want_aux else ()))


def _gdn_step(d, q, k, v, spb, *states, aux=None, want_aux=False):
    return _gdn_chunks([d], [(q, k, v, spb)], states, aux, want_aux, group=4)


def _gdn_step2(q0, k0, v0, sp0, q1, k1, v1, sp1, *states, aux=None, want_aux=False):
    return _gdn_chunks([0, 1], [(q0, k0, v0, sp0), (q1, k1, v1, sp1)], states, aux, want_aux)


def _split3(a):
    a1 = a.astype(_SCAN_DTYPE)
    r = a - a1.astype(F32)
    a2 = r.astype(_SCAN_DTYPE)
    return a1, a2, (r - a2.astype(F32)).astype(_SCAN_DTYPE)


def _exact_dot(a, e, dims, split_lhs):
    parts = _split3(a if split_lhs else e)
    d = (lambda u: lax.dot_general(u, e, dims, preferred_element_type=F32)) if split_lhs else \
        (lambda u: lax.dot_general(a, u, dims, preferred_element_type=F32))
    return d(parts[0]) + (d(parts[1]) + d(parts[2]))


@jax.custom_vjp
def _spread(a, e):
    return _exact_dot(a, e, _NN, True)


def _spread_fwd(a, e):
    return _spread(a, e), e


def _spread_bwd(e, g):
    return _exact_dot(g, e, _NT, True), jnp.zeros_like(e)


_spread.defvjp(_spread_fwd, _spread_bwd)


@jax.custom_vjp
def _colsum_bcast(z):
    return _exact_dot(jnp.ones((z.shape[0], z.shape[0]), _SCAN_DTYPE), z, _NN, False)


def _colsum_fwd(z):
    return _colsum_bcast(z), None


def _colsum_bwd(_, g):
    return (_colsum_bcast(g),)


_colsum_bcast.defvjp(_colsum_fwd, _colsum_bwd)


def _ssd_consts():
    wdt = SSM_HEADS * 64
    d = lax.broadcasted_iota(jnp.int32, (2, 128, wdt), 0)
    e = (lax.broadcasted_iota(jnp.int32, (2, 128, wdt), 1)
         == 32 + d * SSM_HEADS + lax.broadcasted_iota(jnp.int32, (2, 128, wdt), 2) // 64).astype(_SCAN_DTYPE)
    dd = lax.broadcasted_iota(jnp.int32, (2, CHUNK, wdt), 0)
    ci = lax.broadcasted_iota(jnp.int32, (2, CHUNK, wdt), 1)
    pos = lax.broadcasted_iota(jnp.int32, (2, CHUNK, wdt), 2) % 64
    incl_t = (jnp.where(dd == 0, 1, -1) * (ci - pos) >= 0).astype(F32)
    diag = (ci == pos).astype(F32)
    return [e, incl_t, diag]


def _ssd_step(d, x, bm, cm, spb, alog, *states, consts):
    e, incl_t, diag = consts
    incl, _ = _order_masks(d)
    lane1 = lax.broadcasted_iota(jnp.int32, (1, 128), 1)
    lo_lane = 32 + d * SSM_HEADS
    a_vec = jnp.where(lane1 >= lo_lane, jnp.where(lane1 < lo_lane + SSM_HEADS, -jnp.exp(alog), 0.0), 0.0)
    adt = spb * a_vec
    acum = _mmh(incl, adt)
    alast = jnp.sum(adt, axis=0, keepdims=True)
    dt2 = _spread(spb, e)
    ac2 = _spread(acum, e)
    al2 = _spread(jnp.broadcast_to(alast, (8, 128)), e)[0:1]
    row = _colsum_bcast(ac2 * diag)
    seg = jnp.exp(jnp.minimum(ac2 - row, 0.0)) * incl_t
    xdt = x * dt2
    gam = jnp.exp(ac2)
    xe = xdt * jnp.exp(al2 - ac2)
    low = lax.broadcasted_iota(jnp.int32, (CHUNK, 128), 1) < 64
    row_low = lax.broadcasted_iota(jnp.int32, (128, 1), 0) < 64
    ps = range(SSM_PAIRS)
    sl = [slice(p * 128, (p + 1) * 128) for p in ps]
    bg = [bm[:, g * 128:(g + 1) * 128] for g in range(4)]
    cg = [cm[:, g * 128:(g + 1) * 128] for g in range(4)]
    cb2 = [_mm_nt(cg[g], jnp.concatenate([bg[g], bg[g]], axis=0)) for g in range(4)]
    ys, new_states = [], []
    for p0 in range(0, SSM_PAIRS, SSD_GROUP):
        pg = range(p0, p0 + SSD_GROUP)
        xd = {p: jnp.concatenate([jnp.where(low, xdt[:, sl[p]], 0.0), jnp.where(low, 0.0, xdt[:, sl[p]])], axis=0)
              for p in pg}
        yd = {p: _mm(cb2[p // 4] * seg[:, sl[p]], xd[p]) for p in pg}
        yo = {p: _mm_nt(cg[p // 4], states[p]) for p in pg}
        ys += [yd[p] + gam[:, sl[p]] * yo[p] for p in pg]
        new = {p: _mm_tn(xe[:, sl[p]], bg[p // 4]) for p in pg}
        al0 = {p: _lane_col(alast, lo_lane + 2 * p) for p in pg}
        al1 = {p: _lane_col(alast, lo_lane + 2 * p + 1) for p in pg}
        new_states += [states[p] * jnp.exp(jnp.where(row_low, al0[p], al1[p])) + new[p] for p in pg]
    return (jnp.concatenate(ys, axis=1), *new_states)


def _chunk_of(d, p, nctx, nc):
    return jnp.where(d == 0, p, jnp.where(p < nctx, nctx - 1 - p, nctx + nc - 1 - p))


class _NoRide:
    n, arrs, in_specs, out_specs, out_shape, scratch = 0, [], [], [], [], []


def _const_specs(consts):
    return [pl.BlockSpec((1,) + a.shape[1:], lambda d, p: (d,) + (0,) * (a.ndim - 1)) for a in consts]


def _scan_fwd(name, step, toks, pars, consts, out_width, n_state, nctx):
    t = toks[0].shape[0]
    nc = t // CHUNK
    n_tok, n_par, n_const = len(toks), len(pars), len(consts)

    def body(*refs):
        it = iter(refs)
        tok_refs = [next(it) for _ in range(n_tok)]
        par_refs = [next(it) for _ in range(n_par)]
        const_refs = [next(it) for _ in range(n_const)]
        o_ref, ss_ref, s_scr = next(it), next(it), next(it)
        d, p = pl.program_id(0), pl.program_id(1)

        @pl.when(p == 0)
        def _():
            s_scr[...] = jnp.zeros(s_scr.shape, F32)

        ss_ref[0, 0] = s_scr[...]
        res = step(d, *[r[...] for r in tok_refs], *[r[...] for r in par_refs], *[s_scr[h] for h in range(n_state)],
                   consts=[r[0] for r in const_refs])
        o_ref[0] = res[0]
        for h in range(n_state):
            s_scr[h] = res[1 + h]

    ch = lambda d, p: _chunk_of(d, p, nctx, nc)
    in_specs = [pl.BlockSpec((CHUNK, a.shape[1]), lambda d, p: (ch(d, p), 0)) for a in toks]
    in_specs += [pl.BlockSpec(a.shape, lambda d, p: (0, 0)) for a in pars]
    return pl.pallas_call(
        body, grid=(2, nc), in_specs=in_specs + _const_specs(consts),
        out_specs=[pl.BlockSpec((1, CHUNK, out_width), lambda d, p: (d, ch(d, p), 0)),
                   pl.BlockSpec((1, 1, n_state, 128, 128), lambda d, p: (d, p, 0, 0, 0))],
        out_shape=[jax.ShapeDtypeStruct((2, t, out_width), F32),
                   jax.ShapeDtypeStruct((2, nc, n_state, 128, 128), F32)],
        scratch_shapes=[pltpu.VMEM((n_state, 128, 128), F32)],
        compiler_params=_cp(("arbitrary", "arbitrary")), name=name)(*toks, *pars, *consts)


def _scan_bwd(name, step, toks, pars, ss, dout, n_state, nctx, ride=None, aux=None, consts=()):
    t = toks[0].shape[0]
    nc = t // CHUNK
    n_tok, n_par, n_const = len(toks), len(pars), len(consts)
    rd = ride if ride is not None else _NoRide
    n_aux = aux.shape[2] if aux is not None else 0

    def body(*refs):
        it = iter(refs)
        tok_refs = [next(it) for _ in range(n_tok)]
        par_refs = [next(it) for _ in range(n_par)]
        ss_ref, do_ref = next(it), next(it)
        aux_ref = next(it) if n_aux else None
        const_refs = [next(it) for _ in range(n_const)]
        ride_in = [next(it) for _ in range(rd.n)]
        dtok_refs = [next(it) for _ in range(n_tok)]
        dpar_refs = [next(it) for _ in range(n_par)]
        ride_out = [next(it) for _ in range(rd.n)]
        ds_scr = next(it)
        sems = list(it)
        d, pr = pl.program_id(0), pl.program_id(1)
        if ride is not None:
            @pl.when((d == 0) & (pr == 0))
            def _():
                ride.start(ride_in, ride_out, sems)

            @pl.when((d == 1) & (pr == nc - 1))
            def _():
                ride.wait(ride_in, ride_out, sems)

        @pl.when(pr == 0)
        def _():
            ds_scr[...] = jnp.zeros(ds_scr.shape, F32)

        kw = dict(aux=[aux_ref[0, 0, i] for i in range(n_aux)]) if n_aux else {}
        if n_const:
            kw['consts'] = [r[0] for r in const_refs]
        _, pull = jax.vjp(functools.partial(step, d, **kw), *[r[...] for r in tok_refs], *[r[...] for r in par_refs],
                          *[ss_ref[0, 0, h] for h in range(n_state)])
        grads = pull((do_ref[...], *[ds_scr[h] for h in range(n_state)]))
        for r, g in zip(dtok_refs, grads[:n_tok]):
            r[0] = g
        for h in range(n_state):
            ds_scr[h] = grads[n_tok + n_par + h]
        first = (d == 0) & (pr == 0)
        for r, g in zip(dpar_refs, grads[n_tok:n_tok + n_par]):
            @pl.when(first)
            def _(r=r, g=g):
                r[...] = g

            @pl.when(jnp.logical_not(first))
            def _(r=r, g=g):
                r[...] += g

    ch = lambda d, pr: _chunk_of(d, nc - 1 - pr, nctx, nc)
    in_specs = [pl.BlockSpec((CHUNK, a.shape[1]), lambda d, pr: (ch(d, pr), 0)) for a in toks]
    in_specs += [pl.BlockSpec(a.shape, lambda d, pr: (0, 0)) for a in pars]
    in_specs += [pl.BlockSpec((1, 1, n_state, 128, 128), lambda d, pr: (d, nc - 1 - pr, 0, 0, 0)),
                 pl.BlockSpec((CHUNK, dout.shape[1]), lambda d, pr: (ch(d, pr), 0))]
    if n_aux:
        in_specs += [pl.BlockSpec((1, 1, n_aux, CHUNK, CHUNK), lambda d, pr: (d, nc - 1 - pr, 0, 0, 0))]
    in_specs += _const_specs(consts)
    out_specs = [pl.BlockSpec((1, CHUNK, a.shape[1]), lambda d, pr: (d, ch(d, pr), 0)) for a in toks]
    out_specs += [pl.BlockSpec(a.shape, lambda d, pr: (0, 0)) for a in pars]
    out_shape = [jax.ShapeDtypeStruct((2, t, a.shape[1]), F32) for a in toks]
    out_shape += [jax.ShapeDtypeStruct(a.shape, F32) for a in pars]
    return pl.pallas_call(
        body, grid=(2, nc), in_specs=in_specs + rd.in_specs, out_specs=out_specs + rd.out_specs,
        out_shape=out_shape + rd.out_shape, scratch_shapes=[pltpu.VMEM((n_state, 128, 128), F32)] + rd.scratch,
        compiler_params=_cp(("arbitrary", "arbitrary")), name=name)(
            *toks, *pars, ss, dout, *([aux] if n_aux else []), *consts, *rd.arrs)


def _scan2_fwd(name, step2, toks, out_width, n_state, nctx, n_aux, ride=None):
    t = toks[0].shape[0]
    nc = t // CHUNK
    n_tok = len(toks)
    rd = ride if ride is not None else _NoRide

    def body(*refs):
        it = iter(refs)
        tok_refs = [next(it) for _ in range(2 * n_tok)]
        ride_in = [next(it) for _ in range(rd.n)]
        o_refs = [next(it), next(it)]
        ss_ref, aux_ref = next(it), next(it)
        ride_out = [next(it) for _ in range(rd.n)]
        s_scr = next(it)
        sems = list(it)
        p = pl.program_id(0)
        if ride is not None:
            @pl.when(p == 0)
            def _():
                ride.start(ride_in, ride_out, sems)

        @pl.when(p == 0)
        def _():
            s_scr[...] = jnp.zeros(s_scr.shape, F32)

        for d in range(2):
            ss_ref[d, 0] = s_scr[d * n_state:(d + 1) * n_state]
        res = step2(*[r[...] for r in tok_refs], *[s_scr[u] for u in range(2 * n_state)], want_aux=True)
        for d in range(2):
            o_refs[d][...] = res[d]
            for i in range(n_aux):
                aux_ref[d, 0, i] = res[2 + 2 * n_state + d * n_aux + i]
        for u in range(2 * n_state):
            s_scr[u] = res[2 + u]
        if ride is not None:
            @pl.when(p == nc - 1)
            def _():
                ride.wait(ride_in, ride_out, sems)

    def tok_spec(a, d):
        return pl.BlockSpec((CHUNK, a.shape[1]), lambda p: (_chunk_of(d, p, nctx, nc), 0))

    return pl.pallas_call(
        body, grid=(nc,), in_specs=[tok_spec(a, d) for d in range(2) for a in toks] + rd.in_specs,
        out_specs=[pl.BlockSpec((CHUNK, out_width), lambda p: (_chunk_of(0, p, nctx, nc), 0)),
                   pl.BlockSpec((CHUNK, out_width), lambda p: (_chunk_of(1, p, nctx, nc), 0)),
                   pl.BlockSpec((2, 1, n_state, 128, 128), lambda p: (0, p, 0, 0, 0)),
                   pl.BlockSpec((2, 1, n_aux, CHUNK, CHUNK), lambda p: (0, p, 0, 0, 0))] + rd.out_specs,
        out_shape=[jax.ShapeDtypeStruct((t, out_width), F32), jax.ShapeDtypeStruct((t, out_width), F32),
                   jax.ShapeDtypeStruct((2, nc, n_state, 128, 128), F32),
                   jax.ShapeDtypeStruct((2, nc, n_aux, CHUNK, CHUNK), F32)] + rd.out_shape,
        scratch_shapes=[pltpu.VMEM((2 * n_state, 128, 128), F32)] + rd.scratch,
        compiler_params=_cp(("arbitrary",)), name=name)(*toks, *toks, *rd.arrs)


ADA_TN = 512


def _ada_fwd(cc, ada_w4, ada_b):
    per = ada_w4.shape[2] // ADA_TN
    n = N_PLANE * ada_w4.shape[2]

    def body(c_ref, w_ref, b_ref, o_ref):
        s = _silu(c_ref[...]).astype(_MXU_DTYPE)
        o_ref[...] = jnp.dot(s, w_ref[0].astype(_MXU_DTYPE), preferred_element_type=F32) + b_ref[...]

    return pl.pallas_call(
        body, grid=(n // ADA_TN,),
        in_specs=[pl.BlockSpec((8, D_MODEL), lambda j: (0, 0)),
                  pl.BlockSpec((1, D_MODEL, ADA_TN), lambda j: (j // per, 0, j % per)),
                  pl.BlockSpec((1, ADA_TN), lambda j: (0, j))],
        out_specs=pl.BlockSpec((8, ADA_TN), lambda j: (0, j)), out_shape=jax.ShapeDtypeStruct((8, n), F32),
        compiler_params=_cp(("parallel",)), name="ada_fwd")(cc, ada_w4, ada_b)


def _ada_bwd(cc, ada_w4, dmods):
    per = ada_w4.shape[2] // ADA_TN
    n = N_PLANE * ada_w4.shape[2]
    nj = n // ADA_TN

    def body(c_ref, w_ref, g_ref, dw_ref, db_ref, dc_ref):
        j = pl.program_id(0)
        g = g_ref[...]
        row = lax.broadcasted_iota(jnp.int32, g.shape, 0)
        g = jnp.where(row < 2, g, 0.0)
        s, pull = jax.vjp(_silu, c_ref[...])
        dw_ref[0] = lax.dot_general(s.astype(_MXU_DTYPE), g.astype(_MXU_DTYPE), _TN,
                                    preferred_element_type=F32).astype(dw_ref.dtype)
        db_ref[...] = jnp.sum(g, axis=0, keepdims=True)
        ds = lax.dot_general(g.astype(_MXU_DTYPE), w_ref[0].astype(_MXU_DTYPE), _NT, preferred_element_type=F32)

        @pl.when(j == 0)
        def _():
            dc_ref[...] = ds

        @pl.when(j > 0)
        def _():
            dc_ref[...] += ds

        @pl.when(j == nj - 1)
        def _():
            dc_ref[...] = pull(dc_ref[...])[0]

    wspec = pl.BlockSpec((1, D_MODEL, ADA_TN), lambda j: (j // per, 0, j % per))
    return pl.pallas_call(
        body, grid=(nj,),
        in_specs=[pl.BlockSpec((8, D_MODEL), lambda j: (0, 0)), wspec, pl.BlockSpec((8, ADA_TN), lambda j: (0, j))],
        out_specs=[wspec, pl.BlockSpec((1, ADA_TN), lambda j: (0, j)), pl.BlockSpec((8, D_MODEL), lambda j: (0, 0))],
        out_shape=[jax.ShapeDtypeStruct(ada_w4.shape, _GRAD_DTYPE), jax.ShapeDtypeStruct((1, n), F32),
                   jax.ShapeDtypeStruct((8, D_MODEL), F32)],
        compiler_params=_cp(("arbitrary",)), name="ada_bwd")(cc, ada_w4, dmods)


def _tail(h1, ff, mods, wf, tgt, nctx_t, tl):
    t = h1.shape[0]
    nt = t // tl

    def loss_fn(valid, h1v, ffv, g8, w, tg):
        h2 = h1v + g8[0:1] * ffv
        y = _rms(h2) * w
        err = (y - tg) ** 2
        return 0.5 * jnp.sum(jnp.mean(err, axis=-1, keepdims=True), axis=0, keepdims=True) * valid

    def body(h1_ref, ff_ref, g_ref, w_ref, t_ref, loss_ref, dh_ref, dff_ref, dg_ref, dw_ref):
        i = pl.program_id(0)
        valid = jnp.where(i < nctx_t, 0.0, 1.0)
        lv, pull = jax.vjp(functools.partial(loss_fn, valid), h1_ref[...], ff_ref[...].astype(F32), g_ref[...],
                           w_ref[...], t_ref[...])
        dh, dff, dg, dw, _ = pull(jnp.ones((1, 1), F32))
        dh_ref[...] = dh
        dff_ref[...] = dff.astype(dff_ref.dtype)
        lb = jnp.broadcast_to(lv, loss_ref.shape)

        @pl.when(i == 0)
        def _():
            loss_ref[...] = lb
            dg_ref[...] = dg
            dw_ref[...] = dw

        @pl.when(i > 0)
        def _():
            loss_ref[...] += lb
            dg_ref[...] += dg
            dw_ref[...] += dw

    tok = pl.BlockSpec((tl, D_MODEL), lambda i: (i, 0))
    return pl.pallas_call(
        body, grid=(nt,),
        in_specs=[tok, tok, pl.BlockSpec((8, D_MODEL), lambda i: (0, 5)), pl.BlockSpec((1, D_MODEL), lambda i: (0, 0)),
                  pl.BlockSpec((tl, D_MODEL), lambda i: (jnp.maximum(i - nctx_t, 0), 0))],
        out_specs=[pl.BlockSpec((8, 128), lambda i: (0, 0)), tok, tok, pl.BlockSpec((8, D_MODEL), lambda i: (0, 0)),
                   pl.BlockSpec((1, D_MODEL), lambda i: (0, 0))],
        out_shape=[jax.ShapeDtypeStruct((8, 128), F32), jax.ShapeDtypeStruct((t, D_MODEL), F32),
                   jax.ShapeDtypeStruct((t, D_MODEL), _MXU_DTYPE), jax.ShapeDtypeStruct((8, D_MODEL), F32),
                   jax.ShapeDtypeStruct((1, D_MODEL), F32)],
        compiler_params=_cp(("arbitrary",)), name="tail_loss")(h1, ff, mods, wf, tgt)


def _pack_w_in(w4):
    ns = w4.shape[2]
    placed = []
    for s0, s1, p0 in IN_SEGMENTS:
        for j in range(N_PLANE):
            lo, hi = max(s0, j * ns), min(s1, (j + 1) * ns)
            if lo < hi:
                placed.append((p0 + lo - s0, w4[j][:, lo - j * ns:hi - j * ns]))
    placed.sort(key=lambda e: e[0])
    pieces, end = [], 0
    for pos, piece in placed:
        assert pos == end, (pos, end)
        pieces.append(piece)
        end = pos + piece.shape[1]
    pieces.append(jnp.zeros((w4.shape[1], P_TOTAL - end), w4.dtype))
    return jnp.concatenate(pieces, axis=1)


def _unpack_w_in(g):
    ns = D_IN_PROJ // N_PLANE
    shards = []
    for j in range(N_PLANE):
        pieces = []
        for s0, s1, p0 in IN_SEGMENTS:
            lo, hi = max(s0, j * ns), min(s1, (j + 1) * ns)
            if lo < hi:
                pieces.append(g[:, p0 + lo - s0:p0 + hi - s0])
        shards.append(jnp.concatenate(pieces, axis=1))
    return jnp.stack(shards)


LATE_WEIGHTS = ['w_br_gdn', 'w_br_ssm', 'w_out', 'w_ffn_in', 'w_ffn_out']
COL_STACKED = ('ada_w', 'w_in', 'w_ffn_in')


def _from_shards(n, g):
    if n in COL_STACKED:
        return g
    if SHARD_AXIS[n] == 0:
        return g.reshape(N_PLANE * g.shape[1], g.shape[2])
    return jnp.concatenate([g[j] for j in range(N_PLANE)], axis=1)


def _to_dest_blocks(n, g):
    if g.ndim == 3:
        return g
    if SHARD_AXIS[n] == 0:
        return g.reshape(N_PLANE, g.shape[0] // N_PLANE, g.shape[1])
    sz = g.shape[1] // N_PLANE
    return jnp.stack([g[:, j * sz:(j + 1) * sz] for j in range(N_PLANE)])


def _local_step(x, c, ctx, tgt, W, late_shards=None):
    distributed = late_shards is not None
    lc, l = ctx.shape[0], x.shape[0]
    t = lc + l
    tl = 256
    assert lc == tl and l % tl == 0 and lc % CHUNK == 0
    nt, nctx_t, nctx = t // tl, lc // tl, lc // CHUNK
    act = _MXU_DTYPE
    r1 = lambda v: v.reshape(1, -1)

    xt = jnp.concatenate([ctx, x], axis=0)
    cc = jnp.concatenate([c, r1(W['c_ctx']), jnp.zeros((6, D_MODEL), F32)], axis=0)
    mods = _ada_fwd(cc, W['ada_w'], r1(W['ada_b']))

    def mod(kk):
        return Op(mods, (8, D_MODEL), lambda j, i, kk=kk: (0, kk))

    def tokop(arr, w=D_MODEL, off=0, tl_=tl):
        if not isinstance(arr, list) and arr.ndim == 3:
            return Op(arr, (2, tl_, w), lambda j, i: (0, i, off + j), load=_sum_dirs)
        return Op(arr, (tl_, w), lambda j, i: (i, off + j))

    def outop(n, dtype, w=D_MODEL, tl_=tl):
        return Op(((t, n), dtype), (tl_, w), lambda j, i: (i, j))

    def parop(arr, w, off=0):
        return Op(arr, (arr.shape[0], w), lambda j, i: (0, off + j))

    def parout(rows, n, w):
        return Op(((rows, n), F32), (rows, w), lambda j, i: (0, j))

    n1w = r1(W['norm1_w'])
    pre_fn = functools.partial(_prenorm_fn, nctx_t)
    pre_tok, pre_par = [tokop(xt)], [parop(n1w, D_MODEL), mod(1), mod(0)]
    (a,) = _tw_fwd("prenorm_fwd", pre_fn, (1, nt), pre_tok + pre_par, [outop(D_MODEL, act)])
    wp = _pack_w_in(W['w_in'])
    proj = _matmul(a, wp, 'nn', "in_proj")

    gcw, gcb = W['gdn_conv_w'], r1(W['gdn_conv_b'])
    scw, scb = W['ssm_conv_w'], r1(W['ssm_conv_b'])
    conv_parts = {}

    def conv_part(name, mode, poff, cw, cb, woff, width):
        fn = functools.partial(_conv_fn, nctx_t, mode)
        bw = min(width, 1024)
        tok_ = [tokop(proj, bw, poff // bw)]
        par_ = [parop(cw, bw, woff // bw), parop(cb, bw, woff // bw)]
        conv_parts[name] = (fn, tok_, par_, width, bw)
        (o,) = _tw_fwd("conv_" + name, fn, (width // bw, nt), tok_ + par_, [outop(width, F32, bw)])
        return o

    q = conv_part('q', 'q', P_QKV, gcw, gcb, 0, 1024)
    k = conv_part('k', 'k', P_QKV + 1024, gcw, gcb, 1024, 1024)
    v = conv_part('v', 'none', P_QKV + 2048, gcw, gcb, 2048, 1024)
    xs = conv_part('xs', 'none', P_XBC, scw, scb, 0, 2048)
    bm = conv_part('bm', 'none', P_XBC + 2048, scw, scb, 2048, 512)
    cm = conv_part('cm', 'none', P_XBC + 2560, scw, scb, 2560, 512)

    z16, z32 = jnp.zeros((16,), F32), jnp.zeros((32,), F32)
    p0 = jnp.concatenate([W['gdn_a_log'].reshape(-1), jnp.zeros((112,), F32)]).reshape(1, 128)
    p1 = jnp.concatenate([W['gdn_dt_bias'].reshape(-1), z16, W['ssm_dt_bias'].reshape(-1), z32]).reshape(1, 128)
    alog = jnp.concatenate([z32, W['ssm_a_log'].reshape(-1), z32]).reshape(1, 128)
    act_tok, act_par = [tokop(proj, 128, P_SMALL // 128)], [parop(p0, 128), parop(p1, 128)]
    (sp,) = _tw_fwd("small_act", _act_fn, (1, nt), act_tok + act_par, [outop(128, F32, 128)])

    gather_late = PlaneExchange([late_shards[n] for n in LATE_WEIGHTS], gather=True) if distributed else None
    o_f, o_b, ss, tri, *late = _scan2_fwd("gdn_scan_fwd", _gdn_step2, [q, k, v, sp], 1024, GDN_HEADS, nctx,
                                          GDN_HEADS, ride=gather_late)
    if distributed:
        W = dict(W, **{n: _from_shards(n, g) for n, g in zip(LATE_WEIGHTS, late)})
    ssd_consts = _ssd_consts()
    y2, hs = _scan_fwd("ssd_scan_fwd", _ssd_step, [xs, bm, cm, sp], [alog], ssd_consts, 2048, SSM_PAIRS, nctx)

    tlm = 128
    ntm = t // tlm
    gnw = r1(W['gdn_norm_w'])
    dl = jnp.repeat(W['ssm_d'], 64).reshape(1, 2048)
    snw = r1(W['ssm_norm_w'])
    mg_tok = [tokop([o_f, o_b], 1024, 0, tlm), tokop(proj, 1024, P_ZG // 1024, tlm)]
    mg_par = [parop(gnw, 128)]
    (og,) = _tw_fwd("mix_gdn", _mixg_fn, (1, ntm), mg_tok + mg_par, [outop(1024, act, 1024, tlm)])
    ms_tok = [tokop(y2, 2048, 0, tlm), tokop(xs, 2048, 0, tlm), tokop(proj, 2048, P_ZS // 2048, tlm)]
    ms_par = [parop(dl, 2048), parop(snw, 2048)]
    (yg,) = _tw_fwd("mix_ssm", _mixs_fn, (1, ntm), ms_tok + ms_par, [outop(2048, act, 2048, tlm)])

    pg = _matmul(og, W['w_br_gdn'], 'nn', "br_gdn", out_dtype=act)
    ps = _matmul(yg, W['w_br_ssm'], 'nn', "br_ssm", out_dtype=act)
    mr_tok = [tokop(proj, 1024, P_GATE // 1024), tokop(proj, 1024, P_GATE // 1024 + 1), tokop(pg), tokop(ps)]
    (mrg,) = _tw_fwd("merge", _merge_fn, (1, nt), mr_tok, [outop(1024, act)])
    mo = _matmul(mrg, W['w_out'], 'nn', "w_out")

    n2w = r1(W['norm2_w'])
    n2_tok, n2_par = [tokop(xt), tokop(mo)], [mod(2), parop(n2w, D_MODEL), mod(4), mod(3)]
    h1, f = _tw_fwd("norm2_fwd", _norm2_fn, (1, nt), n2_tok + n2_par, [outop(1024, F32), outop(1024, act)])
    u = _matmul(f, W['w_ffn_in'], 'nn', "ffn_in", out_dtype=act)
    swb = D_FF // 2
    sw_tok = [tokop(u, swb, 0), tokop(u, swb, D_FF // swb)]
    (sw,) = _tw_fwd("swiglu", _swiglu_fn, (D_FF // swb, nt), sw_tok, [outop(D_FF, act, swb)])
    ff = _matmul(sw, W['w_ffn_out'], 'nn', "ffn_out")

    loss8, dh1, dff, dg2, dnf = _tail(h1, ff, mods, r1(W['norm_f_w']), tgt, nctx_t, tl)
    loss = loss8[0, 0]

    G = {}
    G['norm_f_w'] = dnf.reshape(-1)
    G['w_ffn_out'] = _matmul(sw, dff, 'tn', "d_ffn_out", out_dtype=_GRAD_DTYPE)
    dsw = _matmul(dff, W['w_ffn_out'], 'nt', "d_sw", out_dtype=act)
    dug, duu = _tw_bwd("swiglu_bwd", _swiglu_fn, (D_FF // swb, nt), sw_tok, [], [[tokop(dsw, swb)]],
                       [outop(D_FF, act, swb), outop(D_FF, act, swb)], [])
    du = jnp.concatenate([dug, duu], axis=1)
    G['w_ffn_in'] = _matmul(f, du, 'tn', "d_ffn_in", out_dtype=_GRAD_DTYPE, stacked_out=True)
    df = _matmul(du, W['w_ffn_in'], 'nt', "d_f")
    dxt1, dmo, dg1, dn2, dsc2, dsh2 = _tw_bwd(
        "norm2_bwd", _norm2_fn, (1, nt), n2_tok, n2_par, [[tokop(dh1)], [tokop(df)]],
        [outop(1024, F32), outop(1024, act)],
        [parout(8, 1024, 1024), parout(1, 1024, 1024), parout(8, 1024, 1024), parout(8, 1024, 1024)])
    G['norm2_w'] = dn2.reshape(-1)
    G['w_out'] = _matmul(mrg, dmo, 'tn', "d_w_out", out_dtype=_GRAD_DTYPE)
    dmrg = _matmul(dmo, W['w_out'], 'nt', "d_mrg", out_dtype=act)
    dgg, dgs, dpg, dps = _tw_bwd("merge_bwd", _merge_fn, (1, nt), mr_tok, [], [[tokop(dmrg)]],
                                 [outop(1024, act), outop(1024, act), outop(1024, act), outop(1024, act)], [])
    G['w_br_gdn'] = _matmul(og, dpg, 'tn', "d_br_gdn", out_dtype=_GRAD_DTYPE)
    G['w_br_ssm'] = _matmul(yg, dps, 'tn', "d_br_ssm", out_dtype=_GRAD_DTYPE)
    dog = _matmul(dpg, W['w_br_gdn'], 'nt', "d_og")
    dyg = _matmul(dps, W['w_br_ssm'], 'nt', "d_yg")

    do, dzg, dgnw = _tw_bwd("mix_gdn_bwd", _mixg_fn, (1, ntm), mg_tok, mg_par, [[tokop(dog, 1024, 0, tlm)]],
                            [outop(1024, F32, 1024, tlm), outop(1024, act, 1024, tlm)], [parout(1, 128, 128)])
    G['gdn_norm_w'] = dgnw.reshape(-1)
    dy, dxs_a, dzs, ddl, dsnw = _tw_bwd(
        "mix_ssm_bwd", _mixs_fn, (1, ntm), ms_tok, ms_par, [[tokop(dyg, 2048, 0, tlm)]],
        [outop(2048, F32, 2048, tlm), outop(2048, F32, 2048, tlm), outop(2048, act, 2048, tlm)],
        [parout(1, 2048, 2048), parout(1, 2048, 2048)])
    G['ssm_d'] = ddl.reshape(SSM_HEADS, 64).sum(axis=1)
    G['ssm_norm_w'] = dsnw.reshape(-1)

    dq2, dk2, dv2, dsp_g = _scan_bwd("gdn_scan_bwd", _gdn_step, [q, k, v, sp], [], ss, do, GDN_HEADS, nctx, aux=tri)
    received = {}
    scatter_late = (PlaneExchange([_to_dest_blocks(n, G.pop(n)) for n in LATE_WEIGHTS], gather=False)
                    if distributed else None)
    dxs2, dbm2, dcm2, dsp_s, dalog, *got = _scan_bwd("ssd_scan_bwd", _ssd_step, [xs, bm, cm, sp], [alog], hs, dy,
                                                     SSM_PAIRS, nctx, ride=scatter_late, consts=ssd_consts)
    received.update(zip(LATE_WEIGHTS, got))
    G['ssm_a_log'] = dalog[0, 32:96].reshape(2, SSM_HEADS)

    dconv_w, dconv_b, dpre = {}, {}, {}

    def conv_bwd(name, cots_):
        fn, tok_, par_, width, bw = conv_parts[name]
        dpre[name], dconv_w[name], dconv_b[name] = _tw_bwd(
            "conv_" + name + "_bwd", fn, (width // bw, nt), tok_, par_, [[tokop(c_, bw) for c_ in cots_]],
            [outop(width, act, bw)], [parout(3, width, bw), parout(1, width, bw)])

    conv_bwd('q', [dq2])
    conv_bwd('k', [dk2])
    conv_bwd('v', [dv2])
    conv_bwd('xs', [dxs2, dxs_a])
    conv_bwd('bm', [dbm2])
    conv_bwd('cm', [dcm2])
    G['gdn_conv_w'] = jnp.concatenate([dconv_w['q'], dconv_w['k'], dconv_w['v']], axis=1)
    G['gdn_conv_b'] = jnp.concatenate([dconv_b['q'], dconv_b['k'], dconv_b['v']], axis=1).reshape(-1)
    G['ssm_conv_w'] = jnp.concatenate([dconv_w['xs'], dconv_w['bm'], dconv_w['cm']], axis=1)
    G['ssm_conv_b'] = jnp.concatenate([dconv_b['xs'], dconv_b['bm'], dconv_b['cm']], axis=1).reshape(-1)

    dsmall, dp0, dp1 = _tw_bwd("small_act_bwd", _act_fn, (1, nt), act_tok, act_par,
                               [[tokop(dsp_g, 128), tokop(dsp_s, 128)]], [outop(128, act, 128)],
                               [parout(1, 128, 128), parout(1, 128, 128)])
    G['gdn_a_log'] = dp0[0, 0:16].reshape(2, GDN_HEADS)
    G['gdn_dt_bias'] = dp1[0, 0:16].reshape(2, GDN_HEADS)
    G['ssm_dt_bias'] = dp1[0, 32:96].reshape(2, SSM_HEADS)

    zpad = jnp.zeros((t, P_TOTAL - P_SMALL - 128), act)
    dproj = jnp.concatenate([dpre['q'], dpre['k'], dpre['v'], dzg, dzs, dpre['xs'], dpre['bm'], dpre['cm'],
                             dgg, dgs, dsmall, zpad], axis=1)
    G['w_in'] = _unpack_w_in(_matmul(a, dproj, 'tn', "d_w_in", out_dtype=_GRAD_DTYPE))
    if distributed:
        da, (received['w_in'],) = _matmul(dproj, wp, 'nt', "d_a", ride=PlaneExchange([G.pop('w_in')], gather=False))
    else:
        da = _matmul(dproj, wp, 'nt', "d_a")

    gx_out = Op(((l, D_MODEL), F32), (tl, D_MODEL), lambda j, i: (jnp.maximum(i - nctx_t, 0), 0))
    grad_x, dn1, dsc1, dsh1 = _tw_bwd(
        "prenorm_bwd", pre_fn, (1, nt), pre_tok, pre_par, [[tokop(da)]], [gx_out],
        [parout(1, 1024, 1024), parout(8, 1024, 1024), parout(8, 1024, 1024)], tok_add=tokop(dxt1),
        sem=("arbitrary", "arbitrary"))
    G['norm1_w'] = dn1.reshape(-1)
    dmods = jnp.concatenate([dsh1, dsc1, dg1, dsh2, dsc2, dg2], axis=1)
    G['ada_w'], dab, dcc = _ada_bwd(cc, W['ada_w'], dmods)
    G['ada_b'] = dab.reshape(-1)
    G['c_ctx'] = dcc[1]
    return loss, grad_x, G, received


def _row_tile(r, c):
    for tr in (512, 256, 128, 64, 32, 16, 8):
        if r % tr == 0 and tr * c * 4 <= (1 << 20):
            return tr
    return r


def _sum4(name, rv):
    _, r, c = rv.shape
    tr = _row_tile(r, c)

    def body(r_ref, o_ref):
        o_ref[...] = ((r_ref[0].astype(F32) + r_ref[1].astype(F32)) + r_ref[2].astype(F32)) + r_ref[3].astype(F32)

    return pl.pallas_call(
        body, grid=(r // tr,), in_specs=[pl.BlockSpec((N_PLANE, tr, c), lambda i: (0, i, 0))],
        out_specs=pl.BlockSpec((tr, c), lambda i: (i, 0)), out_shape=jax.ShapeDtypeStruct((r, c), F32),
        compiler_params=_cp(("parallel",)), name=name)(rv)


def _adamw(name, w, m, v, p, q):
    r, c = w.shape
    tr = _row_tile(r, c)

    def body(w_ref, m_ref, v_ref, p_ref, q_ref, g_ref, d_ref, mo_ref, vo_ref):
        g = p_ref[...] + q_ref[...]
        mn = ADAM_B1 * m_ref[...] + (1.0 - ADAM_B1) * g
        vn = ADAM_B2 * v_ref[...] + (1.0 - ADAM_B2) * jnp.square(g)
        m_hat = mn / (1.0 - ADAM_B1 ** ADAM_STEP)
        v_hat = vn / (1.0 - ADAM_B2 ** ADAM_STEP)
        g_ref[...] = g
        d_ref[...] = -ADAM_LR * (m_hat / (jnp.sqrt(v_hat) + ADAM_EPS) + ADAM_WD * w_ref[...])
        mo_ref[...] = mn
        vo_ref[...] = vn

    spec = pl.BlockSpec((tr, c), lambda i: (i, 0))
    return pl.pallas_call(
        body, grid=(r // tr,), in_specs=[spec] * 5, out_specs=[spec] * 4,
        out_shape=[jax.ShapeDtypeStruct((r, c), F32)] * 4, compiler_params=_cp(("parallel",)), name=name)(w, m, v, p, q)


SMALL_ROWS = 24


def _pack_small(d):
    v = jnp.concatenate([d[n].reshape(-1) for n in SMALL])
    return jnp.pad(v, (0, SMALL_ROWS * 1024 - v.shape[0])).reshape(SMALL_ROWS, 1024)


def _unpack_small(buf, like):
    v = buf.reshape(-1)
    out, off = {}, 0
    for n in SMALL:
        sz = like[n].size
        out[n] = v[off:off + sz].reshape(like[n].shape)
        off += sz
    return out


def kernel(x, c, ctx, c_ctx, ada_w, ada_b, norm1_w, w_in, gdn_conv_w, gdn_conv_b, gdn_a_log, gdn_dt_bias, gdn_norm_w, ssm_conv_w, ssm_conv_b, ssm_a_log, ssm_dt_bias, ssm_d, ssm_norm_w, w_br_gdn, w_br_ssm, w_out, norm2_w, w_ffn_in, w_ffn_out, norm_f_w, loss_target, m_c_ctx, m_ada_w, m_ada_b, m_norm1_w, m_w_in, m_gdn_conv_w, m_gdn_conv_b, m_gdn_a_log, m_gdn_dt_bias, m_gdn_norm_w, m_ssm_conv_w, m_ssm_conv_b, m_ssm_a_log, m_ssm_dt_bias, m_ssm_d, m_ssm_norm_w, m_w_br_gdn, m_w_br_ssm, m_w_out, m_norm2_w, m_w_ffn_in, m_w_ffn_out, m_norm_f_w, v_c_ctx, v_ada_w, v_ada_b, v_norm1_w, v_w_in, v_gdn_conv_w, v_gdn_conv_b, v_gdn_a_log, v_gdn_dt_bias, v_gdn_norm_w, v_ssm_conv_w, v_ssm_conv_b, v_ssm_a_log, v_ssm_dt_bias, v_ssm_d, v_ssm_norm_w, v_w_br_gdn, v_w_br_ssm, v_w_out, v_norm2_w, v_w_ffn_in, v_w_ffn_out, v_norm_f_w):
    args = dict(locals())
    wl = {n: args[n] for n in WEIGHTS}
    ml = {n: args['m_' + n] for n in WEIGHTS}
    vl = {n: args['v_' + n] for n in WEIGHTS}

    def nodepth(n, a):
        return a if n in ('c_ctx', 'norm_f_w') else a[0]

    shard = {n: nodepth(n, wl[n]).astype(_MXU_DTYPE if n in MXU_WEIGHTS else F32) for n in SHARDED}
    first = [n for n in SHARDED if n not in LATE_WEIGHTS]
    W = {n: nodepth(n, wl[n]) for n in SMALL}
    for n, g in zip(first, _gather_two_level("all_gather_plane", [shard[n] for n in first])):
        W[n] = _from_shards(n, g)

    loss_local, grad_x, G, received = _local_step(x[0], c, ctx[0], loss_target[0], W,
                                                  late_shards={n: shard[n] for n in LATE_WEIGHTS})
    loss = lax.psum(loss_local, ("x", "y", "c"))

    small_g = _pack_small(G)
    last = [n for n in SHARDED if n not in received]
    send = [_to_dest_blocks(n, G[n]) for n in last] + [jnp.broadcast_to(small_g[None], (N_PLANE,) + small_g.shape)]
    received.update(zip(last + ['small'], _plane_exchange("scatter_plane", send, gather=False)))
    names = SHARDED + ['small']
    plane_sum = [_sum4("sum4_" + n, received[n]) for n in names]
    other = _swap_sibling(plane_sum)

    wd = {n: nodepth(n, wl[n]) for n in WEIGHTS}
    md = {n: nodepth(n, ml[n]) for n in WEIGHTS}
    vd = {n: nodepth(n, vl[n]) for n in WEIGHTS}
    res = {}
    for n, p, q in zip(names, plane_sum, other):
        if n == 'small':
            outs = _adamw("adamw_small", _pack_small(wd), _pack_small(md), _pack_small(vd), p, q)
            unpacked = [_unpack_small(o, wd) for o in outs]
            for sn in SMALL:
                res[sn] = [u[sn].reshape(wl[sn].shape) for u in unpacked]
        else:
            outs = _adamw("adamw_" + n, wd[n], md[n], vd[n], p, q)
            res[n] = [o.reshape(wl[n].shape) for o in outs]
    flat = [res[n][kind] for kind in range(4) for n in WEIGHTS]
    return (loss, grad_x[None], *flat)
```

```python
import functools

import jax
import jax.numpy as jnp
from jax import lax
from jax.experimental import pallas as pl
from jax.experimental.pallas import tpu as pltpu

F32 = jnp.float32
HI = lax.Precision.HIGHEST
_MXU_DTYPE = jnp.bfloat16
_SCAN_DTYPE = jnp.bfloat16
_GRAD_DTYPE = jnp.bfloat16

D_MODEL = 1024
EPS = 1e-6
CHUNK = 64
GRID_W = 64
GDN_HEADS = 8
GDN_DK = 128
SSM_HEADS = 32
SSM_PAIRS = 16
SSD_GROUP = 4
D_FF = 2816
D_IN_PROJ = 11360
N_PLANE = 4
P_QKV, P_ZG, P_ZS, P_XBC, P_GATE, P_SMALL, P_TOTAL = 0, 3072, 4096, 6144, 9216, 11264, 11776
IN_SEGMENTS = [(0, 3072, P_QKV), (3072, 4096, P_ZG), (4096, 4112, P_SMALL), (4112, 4128, P_SMALL + 16),
               (4128, 6176, P_ZS), (6176, 9248, P_XBC), (9248, 9312, P_SMALL + 32), (9312, 11360, P_GATE)]
ADAM_LR, ADAM_B1, ADAM_B2, ADAM_EPS, ADAM_WD, ADAM_STEP = 0.001, 0.9, 0.999, 1e-08, 0.01, 10

VMEM_LIMIT = 48 * 1024 * 1024

WEIGHTS = ['c_ctx', 'ada_w', 'ada_b', 'norm1_w', 'w_in', 'gdn_conv_w', 'gdn_conv_b', 'gdn_a_log', 'gdn_dt_bias',
           'gdn_norm_w', 'ssm_conv_w', 'ssm_conv_b', 'ssm_a_log', 'ssm_dt_bias', 'ssm_d', 'ssm_norm_w', 'w_br_gdn',
           'w_br_ssm', 'w_out', 'norm2_w', 'w_ffn_in', 'w_ffn_out', 'norm_f_w']
SHARD_AXIS = {'ada_w': 1, 'w_in': 1, 'gdn_conv_w': 1, 'ssm_conv_w': 1, 'w_br_gdn': 0, 'w_br_ssm': 0, 'w_out': 0,
              'w_ffn_in': 1, 'w_ffn_out': 0}
SHARDED = [n for n in WEIGHTS if n in SHARD_AXIS]
SMALL = [n for n in WEIGHTS if n not in SHARD_AXIS]
MXU_WEIGHTS = ['ada_w', 'w_in', 'w_br_gdn', 'w_br_ssm', 'w_out', 'w_ffn_in', 'w_ffn_out']


def _cp(sem):
    return pltpu.CompilerParams(dimension_semantics=sem, vmem_limit_bytes=VMEM_LIMIT)


def _pick(n, cands):
    for c in cands:
        if n % c == 0:
            return c
    raise ValueError(f"no tile for {n}")


HBM = pl.BlockSpec(memory_space=pl.ANY)
MESH = pl.DeviceIdType.MESH


def _plane_peers():
    x, y, c = lax.axis_index("x"), lax.axis_index("y"), lax.axis_index("c")
    return (x, y, c), [(1 - x, y, c), (x, 1 - y, c), (1 - x, 1 - y, c)]


class PlaneExchange:
    def __init__(self, arrs, gather):
        self.arrs, self.gather, self.n = list(arrs), gather, len(arrs)
        self.in_specs = [HBM] * self.n
        self.out_specs = [HBM] * self.n
        self.out_shape = [jax.ShapeDtypeStruct((N_PLANE,) + a.shape if gather else a.shape, a.dtype) for a in self.arrs]
        self.scratch = [pltpu.SemaphoreType.DMA((3 * self.n,)), pltpu.SemaphoreType.DMA((3 * self.n,)),
                        pltpu.SemaphoreType.DMA((self.n,))]

    def _copies(self, ins, outs, sems):
        send_sems, recv_sems, local_sems = sems
        (x, y, c), peers = _plane_peers()
        me = 2 * x + y
        copies = []
        for ti in range(self.n):
            src = ins[ti] if self.gather else ins[ti].at[me]
            copies.append(pltpu.make_async_copy(src, outs[ti].at[me], local_sems.at[ti]))
            for kk, (px, py, pc) in enumerate(peers):
                src = ins[ti] if self.gather else ins[ti].at[2 * px + py]
                copies.append(pltpu.make_async_remote_copy(
                    src_ref=src, dst_ref=outs[ti].at[me], send_sem=send_sems.at[3 * ti + kk],
                    recv_sem=recv_sems.at[3 * ti + kk], device_id=(px, py, pc), device_id_type=MESH))
        return copies

    def start(self, ins, outs, sems):
        for cp in self._copies(ins, outs, sems):
            cp.start()

    def wait(self, ins, outs, sems):
        for cp in self._copies(ins, outs, sems):
            cp.wait()


def _plane_exchange(name, arrs, gather):
    ex = PlaneExchange(arrs, gather)
    n = ex.n

    def body(*refs):
        ins, outs, sems = refs[:n], refs[n:2 * n], refs[2 * n:]
        ex.start(ins, outs, sems)
        ex.wait(ins, outs, sems)

    return pl.pallas_call(body, in_specs=ex.in_specs, out_specs=ex.out_specs, out_shape=ex.out_shape,
                          scratch_shapes=ex.scratch, name=name)(*arrs)


def _gather_two_level(name, arrs):
    n = len(arrs)
    split = [a.shape[0] % 32 == 0 for a in arrs]

    def body(*refs):
        ins, outs = refs[:n], refs[n:2 * n]
        ici_send, ici_recv, d2d_send, d2d_recv, local_sems = refs[2 * n:]
        (x, y, c), peers = _plane_peers()
        me = 2 * x + y

        def part(ref, ti):
            if not split[ti]:
                return ref
            rows = arrs[ti].shape[0] // 2
            return ref.at[pl.ds(c * rows, rows)]

        local = [pltpu.make_async_copy(ins[ti], outs[ti].at[me], local_sems.at[ti]) for ti in range(n)]
        ici = [pltpu.make_async_remote_copy(
            src_ref=part(ins[ti], ti), dst_ref=part(outs[ti].at[me], ti), send_sem=ici_send.at[3 * ti + kk],
            recv_sem=ici_recv.at[3 * ti + kk], device_id=peer, device_id_type=MESH)
            for ti in range(n) for kk, peer in enumerate(peers)]
        for cp in local + ici:
            cp.start()
        d2d = []
        for ti in range(n):
            for kk, (px, py, pc) in enumerate(peers):
                ici[3 * ti + kk].wait_recv()
                if split[ti]:
                    piece = part(outs[ti].at[2 * px + py], ti)
                    cp = pltpu.make_async_remote_copy(
                        src_ref=piece, dst_ref=piece, send_sem=d2d_send.at[3 * ti + kk],
                        recv_sem=d2d_recv.at[3 * ti + kk], device_id=(x, y, 1 - c), device_id_type=MESH)
                    cp.start()
                    d2d.append(cp)
        for cp in ici:
            cp.wait_send()
        for cp in d2d:
            cp.wait()
        for cp in local:
            cp.wait()

    return pl.pallas_call(
        body, in_specs=[HBM] * n, out_specs=[HBM] * n,
        out_shape=[jax.ShapeDtypeStruct((N_PLANE,) + a.shape, a.dtype) for a in arrs],
        scratch_shapes=[pltpu.SemaphoreType.DMA((3 * n,))] * 4 + [pltpu.SemaphoreType.DMA((n,))], name=name)(*arrs)


def _gather_all(name, arr):
    flips = [(fx, fy, fc) for fx in (0, 1) for fy in (0, 1) for fc in (0, 1)][1:]

    def body(a_ref, o_ref, send_sems, recv_sems, local_sem):
        x, y, c = lax.axis_index("x"), lax.axis_index("y"), lax.axis_index("c")
        me = 4 * x + 2 * y + c
        copies = [pltpu.make_async_copy(a_ref, o_ref.at[me], local_sem)]
        for kk, (fx, fy, fc) in enumerate(flips):
            peer = (1 - x if fx else x, 1 - y if fy else y, 1 - c if fc else c)
            copies.append(pltpu.make_async_remote_copy(
                src_ref=a_ref, dst_ref=o_ref.at[me], send_sem=send_sems.at[kk], recv_sem=recv_sems.at[kk],
                device_id=peer, device_id_type=MESH))
        for cp in copies:
            cp.start()
        for cp in copies:
            cp.wait()

    return pl.pallas_call(
        body, in_specs=[HBM], out_specs=HBM, out_shape=jax.ShapeDtypeStruct((8,) + arr.shape, arr.dtype),
        scratch_shapes=[pltpu.SemaphoreType.DMA((7,)), pltpu.SemaphoreType.DMA((7,)), pltpu.SemaphoreType.DMA],
        name=name)(arr)


def _swap_sibling(arrs):
    n = len(arrs)

    def body(*refs):
        ins, outs, send_sems, recv_sems = refs[:n], refs[n:2 * n], refs[2 * n], refs[2 * n + 1]
        x, y, c = lax.axis_index("x"), lax.axis_index("y"), lax.axis_index("c")
        copies = [pltpu.make_async_remote_copy(src_ref=ins[ti], dst_ref=outs[ti], send_sem=send_sems.at[ti],
                                               recv_sem=recv_sems.at[ti], device_id=(x, y, 1 - c), device_id_type=MESH)
                  for ti in range(n)]
        for cp in copies:
            cp.start()
        for cp in copies:
            cp.wait()

    return pl.pallas_call(
        body, in_specs=[HBM] * n, out_specs=[HBM] * n, out_shape=[jax.ShapeDtypeStruct(a.shape, a.dtype) for a in arrs],
        scratch_shapes=[pltpu.SemaphoreType.DMA((n,)), pltpu.SemaphoreType.DMA((n,))], name="swap_sibling")(*arrs)


MATMUL_VMEM_BUDGET = 36 * 1024 * 1024
TILE_CANDIDATES = (2944, 2816, 1408, 1024, 768, 512, 256, 128)
STEP_COST_BYTES = 1 << 20


def _matmul_tiles(m, n, k, ab, bb, ob, tn_fixed=None, tk_fixed=None):
    def cands(dim, whole_up_to):
        out = [c for c in TILE_CANDIDATES if dim % c == 0]
        if dim <= whole_up_to and dim not in out:
            out.append(dim)
        return out

    best = None
    for tm in cands(m, 3072):
        for tn in ([tn_fixed] if tn_fixed else cands(n, 3072)):
            for tk in ([tk_fixed] if tk_fixed else cands(k, 2048)):
                gi, gj, gk = m // tm, n // tn, k // tk
                vmem = 2 * (tm * tk * ab + tk * tn * bb + tm * tn * ob) + (tm * tn * 4 if gk > 1 else 0)
                if vmem > MATMUL_VMEM_BUDGET:
                    continue
                a_reads = 1 if gk == 1 else gj
                b_reads = 1 if (gk == 1 and gj == 1) else gi
                cost = (m * k * ab * a_reads + k * n * bb * b_reads + m * n * ob + gi * gj * gk * STEP_COST_BYTES)
                if best is None or cost < best[0]:
                    best = (cost, tm, tn, tk)
    assert best is not None, (m, n, k)
    return best[1:]


def _matmul(a, b, form, name, out_dtype=F32, stacked_out=False, ride=None):
    dims = {'nn': (((1,), (0,)), ((), ())), 'nt': (((1,), (1,)), ((), ())), 'tn': (((0,), (0,)), ((), ()))}[form]
    stacked_b = b.ndim == 3
    ns = b.shape[2] if stacked_b else None
    if form == 'nn':
        m, k = a.shape
        n = b.shape[0] * ns if stacked_b else b.shape[1]
    elif form == 'nt':
        m, k = a.shape
        n = b.shape[1] if stacked_b else b.shape[0]
    else:
        k, m = a.shape
        n = b.shape[1]
    ob = jnp.dtype(out_dtype).itemsize
    tm, tn, tk = _matmul_tiles(m, n, k, a.dtype.itemsize, b.dtype.itemsize, ob,
                               tn_fixed=(n // N_PLANE if stacked_out else ns if (stacked_b and form == 'nn') else None),
                               tk_fixed=(ns if (stacked_b and form == 'nt') else None))
    nk = k // tk
    grid = (m // tm, n // tn, nk)
    n_ride = ride.n if ride is not None else 0

    def body(*refs):
        a_ref, b_ref = refs[0], refs[1]
        ride_in = refs[2:2 + n_ride]
        o_ref = refs[2 + n_ride]
        ride_out = refs[3 + n_ride:3 + 2 * n_ride]
        acc_ref = refs[3 + 2 * n_ride]
        sems = refs[4 + 2 * n_ride:]
        i, j, kk = pl.program_id(0), pl.program_id(1), pl.program_id(2)
        if ride is not None:
            @pl.when((i == 0) & (j == 0) & (kk == 0))
            def _():
                ride.start(ride_in, ride_out, sems)

        def put(val):
            if stacked_out:
                o_ref[0] = val.astype(o_ref.dtype)
            else:
                o_ref[...] = val.astype(o_ref.dtype)

        bv = b_ref[0] if stacked_b else b_ref[...]
        part = lax.dot_general(a_ref[...].astype(_MXU_DTYPE), bv.astype(_MXU_DTYPE), dims, preferred_element_type=F32)
        if nk == 1:
            put(part)
        else:
            @pl.when(kk == 0)
            def _():
                acc_ref[...] = part

            @pl.when(kk > 0)
            def _():
                acc_ref[...] += part

            @pl.when(kk == nk - 1)
            def _():
                put(acc_ref[...])

        if ride is not None:
            @pl.when((i == grid[0] - 1) & (j == grid[1] - 1) & (kk == nk - 1))
            def _():
                ride.wait(ride_in, ride_out, sems)

    if form == 'nn':
        a_spec = pl.BlockSpec((tm, tk), lambda i, j, kk: (i, kk))
        b_spec = (pl.BlockSpec((1, tk, tn), lambda i, j, kk: (j, kk, 0)) if stacked_b
                  else pl.BlockSpec((tk, tn), lambda i, j, kk: (kk, j)))
    elif form == 'nt':
        a_spec = pl.BlockSpec((tm, tk), lambda i, j, kk: (i, kk))
        b_spec = (pl.BlockSpec((1, tn, tk), lambda i, j, kk: (kk, j, 0)) if stacked_b
                  else pl.BlockSpec((tn, tk), lambda i, j, kk: (j, kk)))
    else:
        a_spec = pl.BlockSpec((tk, tm), lambda i, j, kk: (kk, i))
        b_spec = pl.BlockSpec((tk, tn), lambda i, j, kk: (kk, j))
    if stacked_out:
        o_spec = pl.BlockSpec((1, tm, tn), lambda i, j, kk: (j, i, 0))
        o_shape = jax.ShapeDtypeStruct((N_PLANE, m, tn), out_dtype)
    else:
        o_spec = pl.BlockSpec((tm, tn), lambda i, j, kk: (i, j))
        o_shape = jax.ShapeDtypeStruct((m, n), out_dtype)
    acc_shape = (tm, tn) if nk > 1 else (8, 128)
    if ride is None:
        return pl.pallas_call(
            body, grid=grid, in_specs=[a_spec, b_spec], out_specs=o_spec, out_shape=o_shape,
            scratch_shapes=[pltpu.VMEM(acc_shape, F32)],
            compiler_params=_cp(("parallel", "parallel", "arbitrary")), name=name)(a, b)
    res = pl.pallas_call(
        body, grid=grid, in_specs=[a_spec, b_spec] + ride.in_specs, out_specs=[o_spec] + ride.out_specs,
        out_shape=[o_shape] + ride.out_shape, scratch_shapes=[pltpu.VMEM(acc_shape, F32)] + ride.scratch,
        compiler_params=_cp(("arbitrary", "arbitrary", "arbitrary")), name=name)(a, b, *ride.arrs)
    return res[0], res[1:]


class Op:
    def __init__(self, arr, bs, im, load=None):
        self.arr, self.bs, self.im = arr, bs, im
        self.arrs = list(arr) if isinstance(arr, list) else [arr]
        self.load = load or (lambda r: r[...].astype(F32))

    def spec(self):
        return pl.BlockSpec(self.bs, self.im)

    def value(self, it):
        return functools.reduce(lambda u, w: u + w, [self.load(next(it)) for _ in self.arrs])


def _op_specs(ops):
    return [op.spec() for op in ops for _ in op.arrs]


def _op_arrays(ops):
    return [a for op in ops for a in op.arrs]


def _sum_dirs(r):
    return r[0].astype(F32) + r[1].astype(F32)


def _tw_fwd(name, fn, grid, ins, outs):
    def body(*refs):
        info = (pl.program_id(0), pl.program_id(1))
        it = iter(refs)
        vals = [op.value(it) for op in ins]
        res = fn(info, *vals)
        for r, v in zip(it, res):
            r[...] = v.astype(r.dtype)

    return pl.pallas_call(
        body, grid=grid, in_specs=_op_specs(ins), out_specs=[op.spec() for op in outs],
        out_shape=[jax.ShapeDtypeStruct(*op.arr) for op in outs],
        compiler_params=_cp(("parallel", "arbitrary")), name=name)(*_op_arrays(ins))


def _tw_bwd(name, fn, grid, tok, par, cots, tok_out, par_out, tok_add=None, sem=("parallel", "arbitrary")):
    n_tok = len(tok)
    flat_cots = [op for group in cots for op in group]
    extra = [tok_add] if tok_add is not None else []
    out_ops = list(tok_out) + list(par_out)

    def body(*refs):
        info = (pl.program_id(0), pl.program_id(1))
        it = iter(refs)
        tok_v = [op.value(it) for op in tok]
        par_v = [op.value(it) for op in par]
        cot_v = [functools.reduce(lambda u, w: u + w, [op.value(it) for op in group]) for group in cots]
        add_v = [op.value(it) for op in extra]
        _, pull = jax.vjp(lambda *a: fn(info, *a), *tok_v, *par_v)
        grads = pull(tuple(cot_v))
        for i in range(n_tok):
            r = next(it)
            g = grads[i] + add_v[0] if (i == 0 and add_v) else grads[i]
            r[...] = g.astype(r.dtype)
        first = pl.program_id(1) == 0
        for i in range(len(par)):
            r = next(it)
            g = grads[n_tok + i]

            @pl.when(first)
            def _(r=r, g=g):
                r[...] = g

            @pl.when(jnp.logical_not(first))
            def _(r=r, g=g):
                r[...] += g

    ops = tok + par + flat_cots + extra
    return pl.pallas_call(
        body, grid=grid, in_specs=_op_specs(ops), out_specs=[op.spec() for op in out_ops],
        out_shape=[jax.ShapeDtypeStruct(*op.arr) for op in out_ops],
        compiler_params=_cp(sem), name=name)(*_op_arrays(ops))


def _silu(x):
    return x * jax.nn.sigmoid(x)


def _rms(x):
    return x * lax.rsqrt(jnp.mean(x * x, axis=-1, keepdims=True) + EPS)


@functools.partial(jax.custom_vjp, nondiff_argnums=(1,))
def _shift_rows(x, k):
    return pltpu.roll(x, k % x.shape[0], 0)


def _shift_rows_fwd(x, k):
    return _shift_rows(x, k), None


def _shift_rows_bwd(k, _, g):
    return (_shift_rows(g, -k),)


_shift_rows.defvjp(_shift_rows_fwd, _shift_rows_bwd)


def _prenorm_fn(nctx_t, info, x, w, sc8, sh8):
    is_ctx = info[1] < nctx_t
    sc = jnp.where(is_ctx, sc8[1:2], sc8[0:1])
    sh = jnp.where(is_ctx, sh8[1:2], sh8[0:1])
    return (_rms(x) * w * (1.0 + sc) + sh,)


def _conv_fn(nctx_t, mode, info, x, w, b):
    n, c = x.shape
    is_ctx = info[1] < nctx_t
    idx = lax.broadcasted_iota(jnp.int32, (n, 1), 0)
    rr = jnp.where(is_ctx, idx, idx % GRID_W)
    first = rr == 0
    last = rr == jnp.where(is_ctx, n - 1, GRID_W - 1)
    prev = jnp.where(first, 0.0, _shift_rows(x, 1))
    nxt = jnp.where(last, 0.0, _shift_rows(x, -1))
    y = b + prev * w[0:1] + x * w[1:2] + nxt * w[2:3]
    y = _silu(y)
    if mode == 'none':
        return (y,)
    scale = GDN_DK ** -0.5 if mode == 'q' else 1.0
    outs = []
    for h in range(c // 128):
        yh = y[:, h * 128:(h + 1) * 128]
        outs.append(yh * lax.rsqrt(jnp.sum(yh * yh, axis=-1, keepdims=True) + EPS) * scale)
    return (jnp.concatenate(outs, axis=1),)


def _act_fn(info, x, p0, p1):
    lane = lax.broadcasted_iota(jnp.int32, x.shape, 1)
    sp = jax.nn.softplus(x + p1)
    g = -jnp.exp(p0) * sp
    bt = jax.nn.sigmoid(x)
    return (jnp.where(lane < 16, g, jnp.where(lane < 32, bt, jnp.where(lane < 96, sp, 0.0))),)


def _mixg_fn(info, o, zg, gnw):
    outs = []
    for h in range(GDN_HEADS):
        outs.append(_rms(o[:, h * 128:(h + 1) * 128]) * gnw)
    return (jnp.concatenate(outs, axis=1) * _silu(zg),)


def _mixs_fn(info, y, xs, zs, dl, snw):
    yy = (y + dl * xs) * _silu(zs)
    outs = []
    for g in range(4):
        outs.append(_rms(yy[:, g * 512:(g + 1) * 512]))
    return (jnp.concatenate(outs, axis=1) * snw,)


def _merge_fn(info, gg, gs, pg, ps):
    return (jax.nn.sigmoid(gg) * pg + jax.nn.sigmoid(gs) * ps,)


def _norm2_fn(info, xt, mo, g8, w, sc8, sh8):
    h1 = xt + g8[0:1] * mo
    return (h1, _rms(h1) * w * (1.0 + sc8[0:1]) + sh8[0:1])


def _swiglu_fn(info, ug, uu):
    return (_silu(ug) * uu,)


_NN = (((1,), (0,)), ((), ()))
_NT = (((1,), (1,)), ((), ()))
_TN = (((0,), (0,)), ((), ()))


def _mmh(a, b):
    return lax.dot_general(a, b, _NN, precision=HI, preferred_element_type=F32)


def _dot1(a, b, dims):
    return lax.dot_general(a.astype(_SCAN_DTYPE), b.astype(_SCAN_DTYPE), dims, preferred_element_type=F32)


def _mm(a, b):
    return _dot1(a, b, _NN)


def _mm_nt(a, b):
    return _dot1(a, b, _NT)


def _mm_tn(a, b):
    return _dot1(a, b, _TN)


def _split2(a):
    hi = a.astype(_SCAN_DTYPE)
    return hi, (a - hi.astype(F32)).astype(_SCAN_DTYPE)


def _dot3(a, b, dims):
    ah, al = _split2(a)
    bh, bl = _split2(b)
    d = lambda u, w: lax.dot_general(u, w, dims, preferred_element_type=F32)
    return d(ah, bh) + (d(ah, bl) + d(al, bh))


def _order_masks(d):
    i = lax.broadcasted_iota(jnp.int32, (CHUNK, CHUNK), 0)
    j = lax.broadcasted_iota(jnp.int32, (CHUNK, CHUNK), 1)
    s = jnp.where(d == 0, 1, -1) * (i - j)
    return (s >= 0).astype(F32), (s > 0).astype(F32)


@jax.custom_vjp
def _unit_tri_inv(mats):
    i = lax.broadcasted_iota(jnp.int32, (CHUNK, CHUNK), 0)
    j = lax.broadcasted_iota(jnp.int32, (CHUNK, CHUNK), 1)
    eye = (i == j).astype(F32)
    ps = [-a for a in mats]
    ts = [eye + p for p in ps]
    for _ in range(5):
        ps = [_dot3(p, p, _NN) for p in ps]
        ts = [t + _dot3(t, p, _NN) for t, p in zip(ts, ps)]
    return tuple(ts)


def _uti_fwd(mats):
    ts = _unit_tri_inv(mats)
    return ts, ts


def _uti_bwd(ts, gs):
    inner = [_dot3(g, t, _NT) for g, t in zip(gs, ts)]
    return (tuple(-_dot3(t, u, _TN) for t, u in zip(ts, inner)),)


_unit_tri_inv.defvjp(_uti_fwd, _uti_bwd)


@jax.custom_vjp
def _unit_tri_inv_given(mats, ts):
    return ts


def _utig_fwd(mats, ts):
    return ts, ts


def _utig_bwd(ts, gs):
    return _uti_bwd(ts, gs)[0], tuple(jnp.zeros_like(t) for t in ts)


_unit_tri_inv_given.defvjp(_utig_fwd, _utig_bwd)


def _lane_col(blk, lane_idx):
    lane = lax.broadcasted_iota(jnp.int32, blk.shape, 1)
    return jnp.sum(jnp.where(lane == lane_idx, blk, 0.0), axis=1, keepdims=True)


def _decay_mat(cum, incl):
    cb = jnp.broadcast_to(cum, (CHUNK, CHUNK))
    return jnp.exp(jnp.minimum(cb - cb.T, 0.0)) * incl


def _gdn_chunks(dirs, streams, states, aux=None, want_aux=False, group=None):
    ns = len(dirs)
    masks = [_order_masks(d) for d in dirs]
    cum = [_mmh(masks[i][0], streams[i][3]) for i in range(ns)]
    tot = [jnp.sum(streams[i][3], axis=0, keepdims=True) for i in range(ns)]
    all_units = [(i, h) for i in range(ns) for h in range(GDN_HEADS)]
    group = group or len(all_units)
    cat = jnp.concatenate
    outs, new_states, ts_all = [], [], []
    for g0 in range(0, len(all_units), group):
        units = all_units[g0:g0 + group]
        us = range(len(units))
        st = states[g0:g0 + group]
        qs = [streams[i][0][:, h * 128:(h + 1) * 128] for i, h in units]
        ks = [streams[i][1][:, h * 128:(h + 1) * 128] for i, h in units]
        vs = [streams[i][2][:, h * 128:(h + 1) * 128] for i, h in units]
        gcum = [_lane_col(cum[i], dirs[i] * GDN_HEADS + h) for i, h in units]
        glast = [_lane_col(tot[i], dirs[i] * GDN_HEADS + h) for i, h in units]
        beta = [_lane_col(streams[i][3], 16 + dirs[i] * GDN_HEADS + h) for i, h in units]
        decay = [_decay_mat(gcum[u], masks[units[u][0]][0]) for u in us]
        egc = [jnp.exp(gcum[u]) for u in us]
        kb = [ks[u] * beta[u] for u in us]
        kq = [_mm_nt(cat([kb[u], qs[u]], axis=0), ks[u]) for u in us]
        mats = tuple(kq[u][:CHUNK] * decay[u] * masks[units[u][0]][1] for u in us)
        ts = _unit_tri_inv(mats) if aux is None else _unit_tri_inv_given(mats, tuple(aux[g0:g0 + group]))
        wu = [_mm(ts[u], cat([kb[u] * egc[u], vs[u] * beta[u]], axis=1)) for u in us]
        ws = [_mm(cat([wu[u][:, :128], qs[u] * egc[u]], axis=0), st[u]) for u in us]
        vn = [wu[u][:, 128:] - ws[u][:CHUNK] for u in us]
        outs += [ws[u][CHUNK:] + _mm(kq[u][CHUNK:] * decay[u], vn[u]) for u in us]
        new_states += [st[u] * jnp.exp(glast[u]) + _mm_tn(ks[u] * jnp.exp(glast[u] - gcum[u]), vn[u]) for u in us]
        ts_all += list(ts)
    per_stream = [cat(outs[i * GDN_HEADS:(i + 1) * GDN_HEADS], axis=1) for i in range(ns)]
    return (*per_stream, *new_states, *(ts_all if want_aux else ()))


def _gdn_step(d, q, k, v, spb, *states, aux=None, want_aux=False):
    return _gdn_chunks([d], [(q, k, v, spb)], states, aux, want_aux, group=4)


def _gdn_step2(q0, k0, v0, sp0, q1, k1, v1, sp1, *states, aux=None, want_aux=False):
    return _gdn_chunks([0, 1], [(q0, k0, v0, sp0), (q1, k1, v1, sp1)], states, aux, want_aux)


def _split3(a):
    a1 = a.astype(_SCAN_DTYPE)
    r = a - a1.astype(F32)
    a2 = r.astype(_SCAN_DTYPE)
    return a1, a2, (r - a2.astype(F32)).astype(_SCAN_DTYPE)


def _exact_dot(a, e, dims, split_lhs, passes=3):
    parts = _split3(a if split_lhs else e)[:passes]
    d = (lambda u: lax.dot_general(u, e, dims, preferred_element_type=F32)) if split_lhs else \
        (lambda u: lax.dot_general(a, u, dims, preferred_element_type=F32))
    return functools.reduce(lambda u, w: u + w, [d(p) for p in reversed(parts)])


@jax.custom_vjp
def _spread(a, e):
    return _exact_dot(a, e, _NN, True)


def _spread_fwd(a, e):
    return _spread(a, e), e


def _spread_bwd(e, g):
    return _exact_dot(g, e, _NT, True, passes=2), jnp.zeros_like(e)


_spread.defvjp(_spread_fwd, _spread_bwd)


@jax.custom_vjp
def _colsum_bcast(z):
    return _exact_dot(jnp.ones((z.shape[0], z.shape[0]), _SCAN_DTYPE), z, _NN, False)


def _colsum_fwd(z):
    return _colsum_bcast(z), None


def _colsum_bwd(_, g):
    return (_exact_dot(jnp.ones((g.shape[0], g.shape[0]), _SCAN_DTYPE), g, _NN, False, passes=2),)


_colsum_bcast.defvjp(_colsum_fwd, _colsum_bwd)


def _ssd_consts():
    wdt = SSM_HEADS * 64
    d = lax.broadcasted_iota(jnp.int32, (2, 128, wdt), 0)
    e = (lax.broadcasted_iota(jnp.int32, (2, 128, wdt), 1)
         == 32 + d * SSM_HEADS + lax.broadcasted_iota(jnp.int32, (2, 128, wdt), 2) // 64).astype(_SCAN_DTYPE)
    dd = lax.broadcasted_iota(jnp.int32, (2, CHUNK, wdt), 0)
    ci = lax.broadcasted_iota(jnp.int32, (2, CHUNK, wdt), 1)
    pos = lax.broadcasted_iota(jnp.int32, (2, CHUNK, wdt), 2) % 64
    incl_t = (jnp.where(dd == 0, 1, -1) * (ci - pos) >= 0).astype(F32)
    diag = (ci == pos).astype(F32)
    return [e, incl_t, diag]


def _ssd_step(d, x, bm, cm, spb, alog, *states, consts):
    e, incl_t, diag = consts
    incl, _ = _order_masks(d)
    lane1 = lax.broadcasted_iota(jnp.int32, (1, 128), 1)
    lo_lane = 32 + d * SSM_HEADS
    a_vec = jnp.where(lane1 >= lo_lane, jnp.where(lane1 < lo_lane + SSM_HEADS, -jnp.exp(alog), 0.0), 0.0)
    adt = spb * a_vec
    acum = _mmh(incl, adt)
    alast = jnp.sum(adt, axis=0, keepdims=True)
    dt2 = _spread(spb, e)
    ac2 = _spread(acum, e)
    al2 = _spread(jnp.broadcast_to(alast, (8, 128)), e)[0:1]
    row = _colsum_bcast(ac2 * diag)
    seg = jnp.exp(jnp.minimum(ac2 - row, 0.0)) * incl_t
    xdt = x * dt2
    gam = jnp.exp(ac2)
    xe = xdt * jnp.exp(al2 - ac2)
    low = lax.broadcasted_iota(jnp.int32, (CHUNK, 128), 1) < 64
    row_low = lax.broadcasted_iota(jnp.int32, (128, 1), 0) < 64
    ps = range(SSM_PAIRS)
    sl = [slice(p * 128, (p + 1) * 128) for p in ps]
    bg = [bm[:, g * 128:(g + 1) * 128] for g in range(4)]
    cg = [cm[:, g * 128:(g + 1) * 128] for g in range(4)]
    cb2 = [_mm_nt(cg[g], jnp.concatenate([bg[g], bg[g]], axis=0)) for g in range(4)]
    ys, new_states = [], []
    for p0 in range(0, SSM_PAIRS, SSD_GROUP):
        pg = range(p0, p0 + SSD_GROUP)
        xd = {p: jnp.concatenate([jnp.where(low, xdt[:, sl[p]], 0.0), jnp.where(low, 0.0, xdt[:, sl[p]])], axis=0)
              for p in pg}
        yd = {p: _mm(cb2[p // 4] * seg[:, sl[p]], xd[p]) for p in pg}
        yo = {p: _mm_nt(cg[p // 4], states[p]) for p in pg}
        ys += [yd[p] + gam[:, sl[p]] * yo[p] for p in pg]
        new = {p: _mm_tn(xe[:, sl[p]], bg[p // 4]) for p in pg}
        al0 = {p: _lane_col(alast, lo_lane + 2 * p) for p in pg}
        al1 = {p: _lane_col(alast, lo_lane + 2 * p + 1) for p in pg}
        new_states += [states[p] * jnp.exp(jnp.where(row_low, al0[p], al1[p])) + new[p] for p in pg]
    return (jnp.concatenate(ys, axis=1), *new_states)


def _chunk_of(d, p, nctx, nc):
    return jnp.where(d == 0, p, jnp.where(p < nctx, nctx - 1 - p, nctx + nc - 1 - p))


class _NoRide:
    n, arrs, in_specs, out_specs, out_shape, scratch = 0, [], [], [], [], []


def _const_specs(consts):
    return [pl.BlockSpec((1,) + a.shape[1:], lambda d, p: (d,) + (0,) * (a.ndim - 1)) for a in consts]


def _scan_fwd(name, step, toks, pars, consts, out_width, n_state, nctx):
    t = toks[0].shape[0]
    nc = t // CHUNK
    n_tok, n_par, n_const = len(toks), len(pars), len(consts)

    def body(*refs):
        it = iter(refs)
        tok_refs = [next(it) for _ in range(n_tok)]
        par_refs = [next(it) for _ in range(n_par)]
        const_refs = [next(it) for _ in range(n_const)]
        o_ref, ss_ref, s_scr = next(it), next(it), next(it)
        d, p = pl.program_id(0), pl.program_id(1)

        @pl.when(p == 0)
        def _():
            s_scr[...] = jnp.zeros(s_scr.shape, F32)

        ss_ref[0, 0] = s_scr[...]
        res = step(d, *[r[...] for r in tok_refs], *[r[...] for r in par_refs], *[s_scr[h] for h in range(n_state)],
                   consts=[r[0] for r in const_refs])
        o_ref[0] = res[0]
        for h in range(n_state):
            s_scr[h] = res[1 + h]

    ch = lambda d, p: _chunk_of(d, p, nctx, nc)
    in_specs = [pl.BlockSpec((CHUNK, a.shape[1]), lambda d, p: (ch(d, p), 0)) for a in toks]
    in_specs += [pl.BlockSpec(a.shape, lambda d, p: (0, 0)) for a in pars]
    return pl.pallas_call(
        body, grid=(2, nc), in_specs=in_specs + _const_specs(consts),
        out_specs=[pl.BlockSpec((1, CHUNK, out_width), lambda d, p: (d, ch(d, p), 0)),
                   pl.BlockSpec((1, 1, n_state, 128, 128), lambda d, p: (d, p, 0, 0, 0))],
        out_shape=[jax.ShapeDtypeStruct((2, t, out_width), F32),
                   jax.ShapeDtypeStruct((2, nc, n_state, 128, 128), F32)],
        scratch_shapes=[pltpu.VMEM((n_state, 128, 128), F32)],
        compiler_params=_cp(("arbitrary", "arbitrary")), name=name)(*toks, *pars, *consts)


def _scan_bwd(name, step, toks, pars, ss, dout, n_state, nctx, ride=None, aux=None, consts=()):
    t = toks[0].shape[0]
    nc = t // CHUNK
    n_tok, n_par, n_const = len(toks), len(pars), len(consts)
    rd = ride if ride is not None else _NoRide
    n_aux = aux.shape[2] if aux is not None else 0

    def body(*refs):
        it = iter(refs)
        tok_refs = [next(it) for _ in range(n_tok)]
        par_refs = [next(it) for _ in range(n_par)]
        ss_ref, do_ref = next(it), next(it)
        aux_ref = next(it) if n_aux else None
        const_refs = [next(it) for _ in range(n_const)]
        ride_in = [next(it) for _ in range(rd.n)]
        dtok_refs = [next(it) for _ in range(n_tok)]
        dpar_refs = [next(it) for _ in range(n_par)]
        ride_out = [next(it) for _ in range(rd.n)]
        ds_scr = next(it)
        sems = list(it)
        d, pr = pl.program_id(0), pl.program_id(1)
        if ride is not None:
            @pl.when((d == 0) & (pr == 0))
            def _():
                ride.start(ride_in, ride_out, sems)

            @pl.when((d == 1) & (pr == nc - 1))
            def _():
                ride.wait(ride_in, ride_out, sems)

        @pl.when(pr == 0)
        def _():
            ds_scr[...] = jnp.zeros(ds_scr.shape, F32)

        kw = dict(aux=[aux_ref[0, 0, i] for i in range(n_aux)]) if n_aux else {}
        if n_const:
            kw['consts'] = [r[0] for r in const_refs]
        _, pull = jax.vjp(functools.partial(step, d, **kw), *[r[...] for r in tok_refs], *[r[...] for r in par_refs],
                          *[ss_ref[0, 0, h] for h in range(n_state)])
        grads = pull((do_ref[...], *[ds_scr[h] for h in range(n_state)]))
        for r, g in zip(dtok_refs, grads[:n_tok]):
            r[0] = g
        for h in range(n_state):
            ds_scr[h] = grads[n_tok + n_par + h]
        first = (d == 0) & (pr == 0)
        for r, g in zip(dpar_refs, grads[n_tok:n_tok + n_par]):
            @pl.when(first)
            def _(r=r, g=g):
                r[...] = g

            @pl.when(jnp.logical_not(first))
            def _(r=r, g=g):
                r[...] += g

    ch = lambda d, pr: _chunk_of(d, nc - 1 - pr, nctx, nc)
    in_specs = [pl.BlockSpec((CHUNK, a.shape[1]), lambda d, pr: (ch(d, pr), 0)) for a in toks]
    in_specs += [pl.BlockSpec(a.shape, lambda d, pr: (0, 0)) for a in pars]
    in_specs += [pl.BlockSpec((1, 1, n_state, 128, 128), lambda d, pr: (d, nc - 1 - pr, 0, 0, 0)),
                 pl.BlockSpec((CHUNK, dout.shape[1]), lambda d, pr: (ch(d, pr), 0))]
    if n_aux:
        in_specs += [pl.BlockSpec((1, 1, n_aux, CHUNK, CHUNK), lambda d, pr: (d, nc - 1 - pr, 0, 0, 0))]
    in_specs += _const_specs(consts)
    out_specs = [pl.BlockSpec((1, CHUNK, a.shape[1]), lambda d, pr: (d, ch(d, pr), 0)) for a in toks]
    out_specs += [pl.BlockSpec(a.shape, lambda d, pr: (0, 0)) for a in pars]
    out_shape = [jax.ShapeDtypeStruct((2, t, a.shape[1]), F32) for a in toks]
    out_shape += [jax.ShapeDtypeStruct(a.shape, F32) for a in pars]
    return pl.pallas_call(
        body, grid=(2, nc), in_specs=in_specs + rd.in_specs, out_specs=out_specs + rd.out_specs,
        out_shape=out_shape + rd.out_shape, scratch_shapes=[pltpu.VMEM((n_state, 128, 128), F32)] + rd.scratch,
        compiler_params=_cp(("arbitrary", "arbitrary")), name=name)(
            *toks, *pars, ss, dout, *([aux] if n_aux else []), *consts, *rd.arrs)


def _scan2_fwd(name, step2, toks, out_width, n_state, nctx, n_aux, ride=None):
    t = toks[0].shape[0]
    nc = t // CHUNK
    n_tok = len(toks)
    rd = ride if ride is not None else _NoRide

    def body(*refs):
        it = iter(refs)
        tok_refs = [next(it) for _ in range(2 * n_tok)]
        ride_in = [next(it) for _ in range(rd.n)]
        o_refs = [next(it), next(it)]
        ss_ref, aux_ref = next(it), next(it)
        ride_out = [next(it) for _ in range(rd.n)]
        s_scr = next(it)
        sems = list(it)
        p = pl.program_id(0)
        if ride is not None:
            @pl.when(p == 0)
            def _():
                ride.start(ride_in, ride_out, sems)

        @pl.when(p == 0)
        def _():
            s_scr[...] = jnp.zeros(s_scr.shape, F32)

        for d in range(2):
            ss_ref[d, 0] = s_scr[d * n_state:(d + 1) * n_state]
        res = step2(*[r[...] for r in tok_refs], *[s_scr[u] for u in range(2 * n_state)], want_aux=True)
        for d in range(2):
            o_refs[d][...] = res[d]
            for i in range(n_aux):
                aux_ref[d, 0, i] = res[2 + 2 * n_state + d * n_aux + i]
        for u in range(2 * n_state):
            s_scr[u] = res[2 + u]
        if ride is not None:
            @pl.when(p == nc - 1)
            def _():
                ride.wait(ride_in, ride_out, sems)

    def tok_spec(a, d):
        return pl.BlockSpec((CHUNK, a.shape[1]), lambda p: (_chunk_of(d, p, nctx, nc), 0))

    return pl.pallas_call(
        body, grid=(nc,), in_specs=[tok_spec(a, d) for d in range(2) for a in toks] + rd.in_specs,
        out_specs=[pl.BlockSpec((CHUNK, out_width), lambda p: (_chunk_of(0, p, nctx, nc), 0)),
                   pl.BlockSpec((CHUNK, out_width), lambda p: (_chunk_of(1, p, nctx, nc), 0)),
                   pl.BlockSpec((2, 1, n_state, 128, 128), lambda p: (0, p, 0, 0, 0)),
                   pl.BlockSpec((2, 1, n_aux, CHUNK, CHUNK), lambda p: (0, p, 0, 0, 0))] + rd.out_specs,
        out_shape=[jax.ShapeDtypeStruct((t, out_width), F32), jax.ShapeDtypeStruct((t, out_width), F32),
                   jax.ShapeDtypeStruct((2, nc, n_state, 128, 128), F32),
                   jax.ShapeDtypeStruct((2, nc, n_aux, CHUNK, CHUNK), F32)] + rd.out_shape,
        scratch_shapes=[pltpu.VMEM((2 * n_state, 128, 128), F32)] + rd.scratch,
        compiler_params=_cp(("arbitrary",)), name=name)(*toks, *toks, *rd.arrs)


ADA_TN = 512


def _ada_fwd(cc, ada_w4, ada_b):
    per = ada_w4.shape[2] // ADA_TN
    n = N_PLANE * ada_w4.shape[2]

    def body(c_ref, w_ref, b_ref, o_ref):
        s = _silu(c_ref[...]).astype(_MXU_DTYPE)
        o_ref[...] = jnp.dot(s, w_ref[0].astype(_MXU_DTYPE), preferred_element_type=F32) + b_ref[...]

    return pl.pallas_call(
        body, grid=(n // ADA_TN,),
        in_specs=[pl.BlockSpec((8, D_MODEL), lambda j: (0, 0)),
                  pl.BlockSpec((1, D_MODEL, ADA_TN), lambda j: (j // per, 0, j % per)),
                  pl.BlockSpec((1, ADA_TN), lambda j: (0, j))],
        out_specs=pl.BlockSpec((8, ADA_TN), lambda j: (0, j)), out_shape=jax.ShapeDtypeStruct((8, n), F32),
        compiler_params=_cp(("parallel",)), name="ada_fwd")(cc, ada_w4, ada_b)


def _ada_bwd(cc, ada_w4, dmods):
    per = ada_w4.shape[2] // ADA_TN
    n = N_PLANE * ada_w4.shape[2]
    nj = n // ADA_TN

    def body(c_ref, w_ref, g_ref, dw_ref, db_ref, dc_ref):
        j = pl.program_id(0)
        g = g_ref[...]
        row = lax.broadcasted_iota(jnp.int32, g.shape, 0)
        g = jnp.where(row < 2, g, 0.0)
        s, pull = jax.vjp(_silu, c_ref[...])
        dw_ref[0] = lax.dot_general(s.astype(_MXU_DTYPE), g.astype(_MXU_DTYPE), _TN,
                                    preferred_element_type=F32).astype(dw_ref.dtype)
        db_ref[...] = jnp.sum(g, axis=0, keepdims=True)
        ds = lax.dot_general(g.astype(_MXU_DTYPE), w_ref[0].astype(_MXU_DTYPE), _NT, preferred_element_type=F32)

        @pl.when(j == 0)
        def _():
            dc_ref[...] = ds

        @pl.when(j > 0)
        def _():
            dc_ref[...] += ds

        @pl.when(j == nj - 1)
        def _():
            dc_ref[...] = pull(dc_ref[...])[0]

    wspec = pl.BlockSpec((1, D_MODEL, ADA_TN), lambda j: (j // per, 0, j % per))
    return pl.pallas_call(
        body, grid=(nj,),
        in_specs=[pl.BlockSpec((8, D_MODEL), lambda j: (0, 0)), wspec, pl.BlockSpec((8, ADA_TN), lambda j: (0, j))],
        out_specs=[wspec, pl.BlockSpec((1, ADA_TN), lambda j: (0, j)), pl.BlockSpec((8, D_MODEL), lambda j: (0, 0))],
        out_shape=[jax.ShapeDtypeStruct(ada_w4.shape, _GRAD_DTYPE), jax.ShapeDtypeStruct((1, n), F32),
                   jax.ShapeDtypeStruct((8, D_MODEL), F32)],
        compiler_params=_cp(("arbitrary",)), name="ada_bwd")(cc, ada_w4, dmods)


def _ada_part(cc16, w_shard):
    n = w_shard.shape[1]

    def body(c_ref, w_ref, o_ref):
        s = _silu(c_ref[...]).astype(_MXU_DTYPE)
        o_ref[...] = jnp.dot(s, w_ref[...].astype(_MXU_DTYPE), preferred_element_type=F32)

    return pl.pallas_call(
        body, grid=(n // ADA_TN,),
        in_specs=[pl.BlockSpec((16, D_MODEL), lambda j: (0, 0)), pl.BlockSpec((D_MODEL, ADA_TN), lambda j: (0, j))],
        out_specs=pl.BlockSpec((16, ADA_TN), lambda j: (0, j)), out_shape=jax.ShapeDtypeStruct((16, n), F32),
        compiler_params=_cp(("parallel",)), name="ada_part")(cc16, w_shard)


def _ada_bwd_shard(cc16, w_shard, d_lat, d_ctx):
    n = w_shard.shape[1]
    nj = n // ADA_TN

    def body(c_ref, w_ref, gl_ref, gc_ref, dw_ref, dc_ref):
        j = pl.program_id(0)
        s, pull = jax.vjp(_silu, c_ref[...])
        row = lax.broadcasted_iota(jnp.int32, (8, 1), 0)
        dctx = jnp.where(row == 0, jnp.sum(gc_ref[...], axis=0, keepdims=True), 0.0)
        mxu = lambda v: v.astype(_MXU_DTYPE)
        dw_ref[...] = (lax.dot_general(mxu(s[0:8]), mxu(gl_ref[...]), _TN, preferred_element_type=F32)
                       + lax.dot_general(mxu(s[8:16]), mxu(dctx), _TN, preferred_element_type=F32))
        ds = lax.dot_general(mxu(dctx), mxu(w_ref[...]), _NT, preferred_element_type=F32)

        @pl.when(j == 0)
        def _():
            dc_ref[...] = ds

        @pl.when(j > 0)
        def _():
            dc_ref[...] += ds

        @pl.when(j == nj - 1)
        def _():
            ct = jnp.concatenate([jnp.zeros((8, D_MODEL), F32), dc_ref[...]], axis=0)
            dc_ref[...] = 0.5 * pull(ct)[0][8:16]

    tile = pl.BlockSpec((8, ADA_TN), lambda j: (0, j))
    wspec = pl.BlockSpec((D_MODEL, ADA_TN), lambda j: (0, j))
    return pl.pallas_call(
        body, grid=(nj,), in_specs=[pl.BlockSpec((16, D_MODEL), lambda j: (0, 0)), wspec, tile, tile],
        out_specs=[wspec, pl.BlockSpec((8, D_MODEL), lambda j: (0, 0))],
        out_shape=[jax.ShapeDtypeStruct((D_MODEL, n), F32), jax.ShapeDtypeStruct((8, D_MODEL), F32)],
        compiler_params=_cp(("arbitrary",)), name="ada_bwd_shard")(cc16, w_shard, d_lat, d_ctx)


def _rowsum2(a, b):
    def body(a_ref, b_ref, o_ref):
        o_ref[...] = jnp.sum(a_ref[...], axis=0, keepdims=True) + jnp.sum(b_ref[...], axis=0, keepdims=True)

    return pl.pallas_call(body, out_shape=jax.ShapeDtypeStruct((1, a.shape[1]), F32), name="ada_b_grad")(a, b)


def _tail(h1, ff, mods, wf, tgt, nctx_t, tl):
    t = h1.shape[0]
    nt = t // tl

    def loss_fn(valid, h1v, ffv, g8, w, tg):
        h2 = h1v + g8[0:1] * ffv
        y = _rms(h2) * w
        err = (y - tg) ** 2
        return 0.5 * jnp.sum(jnp.mean(err, axis=-1, keepdims=True), axis=0, keepdims=True) * valid

    def body(h1_ref, ff_ref, g_ref, w_ref, t_ref, loss_ref, dh_ref, dff_ref, dg_ref, dw_ref):
        i = pl.program_id(0)
        valid = jnp.where(i < nctx_t, 0.0, 1.0)
        lv, pull = jax.vjp(functools.partial(loss_fn, valid), h1_ref[...], ff_ref[...].astype(F32), g_ref[...],
                           w_ref[...], t_ref[...])
        dh, dff, dg, dw, _ = pull(jnp.ones((1, 1), F32))
        dh_ref[...] = dh
        dff_ref[...] = dff.astype(dff_ref.dtype)
        lb = jnp.broadcast_to(lv, loss_ref.shape)

        @pl.when(i == 0)
        def _():
            loss_ref[...] = lb
            dg_ref[...] = dg
            dw_ref[...] = dw

        @pl.when(i > 0)
        def _():
            loss_ref[...] += lb
            dg_ref[...] += dg
            dw_ref[...] += dw

    tok = pl.BlockSpec((tl, D_MODEL), lambda i: (i, 0))
    return pl.pallas_call(
        body, grid=(nt,),
        in_specs=[tok, tok, pl.BlockSpec((8, D_MODEL), lambda i: (0, 5)), pl.BlockSpec((1, D_MODEL), lambda i: (0, 0)),
                  pl.BlockSpec((tl, D_MODEL), lambda i: (jnp.maximum(i - nctx_t, 0), 0))],
        out_specs=[pl.BlockSpec((8, 128), lambda i: (0, 0)), tok, tok, pl.BlockSpec((8, D_MODEL), lambda i: (0, 0)),
                   pl.BlockSpec((1, D_MODEL), lambda i: (0, 0))],
        out_shape=[jax.ShapeDtypeStruct((8, 128), F32), jax.ShapeDtypeStruct((t, D_MODEL), F32),
                   jax.ShapeDtypeStruct((t, D_MODEL), _MXU_DTYPE), jax.ShapeDtypeStruct((8, D_MODEL), F32),
                   jax.ShapeDtypeStruct((1, D_MODEL), F32)],
        compiler_params=_cp(("arbitrary",)), name="tail_loss")(h1, ff, mods, wf, tgt)


def _pack_w_in(w4):
    ns = w4.shape[2]
    placed = []
    for s0, s1, p0 in IN_SEGMENTS:
        for j in range(N_PLANE):
            lo, hi = max(s0, j * ns), min(s1, (j + 1) * ns)
            if lo < hi:
                placed.append((p0 + lo - s0, w4[j][:, lo - j * ns:hi - j * ns]))
    placed.sort(key=lambda e: e[0])
    pieces, end = [], 0
    for pos, piece in placed:
        assert pos == end, (pos, end)
        pieces.append(piece)
        end = pos + piece.shape[1]
    pieces.append(jnp.zeros((w4.shape[1], P_TOTAL - end), w4.dtype))
    return jnp.concatenate(pieces, axis=1)


def _unpack_w_in(g):
    ns = D_IN_PROJ // N_PLANE
    shards = []
    for j in range(N_PLANE):
        pieces = []
        for s0, s1, p0 in IN_SEGMENTS:
            lo, hi = max(s0, j * ns), min(s1, (j + 1) * ns)
            if lo < hi:
                pieces.append(g[:, p0 + lo - s0:p0 + hi - s0])
        shards.append(jnp.concatenate(pieces, axis=1))
    return jnp.stack(shards)


LATE_WEIGHTS = ['w_br_gdn', 'w_br_ssm', 'w_out', 'w_ffn_in', 'w_ffn_out']
COL_STACKED = ('ada_w', 'w_in', 'w_ffn_in')


def _from_shards(n, g):
    if n in COL_STACKED:
        return g
    if SHARD_AXIS[n] == 0:
        return g.reshape(N_PLANE * g.shape[1], g.shape[2])
    return jnp.concatenate([g[j] for j in range(N_PLANE)], axis=1)


def _to_dest_blocks(n, g):
    if g.ndim == 3:
        return g
    if SHARD_AXIS[n] == 0:
        return g.reshape(N_PLANE, g.shape[0] // N_PLANE, g.shape[1])
    sz = g.shape[1] // N_PLANE
    return jnp.stack([g[:, j * sz:(j + 1) * sz] for j in range(N_PLANE)])


def _local_step(x, c, ctx, tgt, W, late_shards=None):
    distributed = late_shards is not None
    lc, l = ctx.shape[0], x.shape[0]
    t = lc + l
    tl = 256
    assert lc == tl and l % tl == 0 and lc % CHUNK == 0
    nt, nctx_t, nctx = t // tl, lc // tl, lc // CHUNK
    act = _MXU_DTYPE
    r1 = lambda v: v.reshape(1, -1)

    xt = jnp.concatenate([ctx, x], axis=0)
    if distributed:
        lin = 4 * lax.axis_index("x") + 2 * lax.axis_index("y") + lax.axis_index("c")
        c_all = _gather_all("gather_c", jnp.broadcast_to(c, (8, D_MODEL)))[:, 0]
        cc16 = jnp.concatenate([c_all, r1(W['c_ctx']), jnp.zeros((7, D_MODEL), F32)], axis=0)
        (parts,) = _plane_exchange("gather_mods", [_ada_part(cc16, W['ada_w'])], gather=True)
        mods_all = jnp.transpose(parts, (1, 0, 2)).reshape(16, -1) + r1(W['ada_b'])
        mods = jnp.concatenate([lax.dynamic_slice_in_dim(mods_all, lin, 1, axis=0), mods_all[8:9],
                                jnp.zeros((6, mods_all.shape[1]), F32)], axis=0)
    else:
        cc = jnp.concatenate([c, r1(W['c_ctx']), jnp.zeros((6, D_MODEL), F32)], axis=0)
        mods = _ada_fwd(cc, W['ada_w'], r1(W['ada_b']))

    def mod(kk):
        return Op(mods, (8, D_MODEL), lambda j, i, kk=kk: (0, kk))

    def tokop(arr, w=D_MODEL, off=0, tl_=tl):
        if not isinstance(arr, list) and arr.ndim == 3:
            return Op(arr, (2, tl_, w), lambda j, i: (0, i, off + j), load=_sum_dirs)
        return Op(arr, (tl_, w), lambda j, i: (i, off + j))

    def outop(n, dtype, w=D_MODEL, tl_=tl):
        return Op(((t, n), dtype), (tl_, w), lambda j, i: (i, j))

    def parop(arr, w, off=0):
        return Op(arr, (arr.shape[0], w), lambda j, i: (0, off + j))

    def parout(rows, n, w):
        return Op(((rows, n), F32), (rows, w), lambda j, i: (0, j))

    n1w = r1(W['norm1_w'])
    pre_fn = functools.partial(_prenorm_fn, nctx_t)
    pre_tok, pre_par = [tokop(xt)], [parop(n1w, D_MODEL), mod(1), mod(0)]
    (a,) = _tw_fwd("prenorm_fwd", pre_fn, (1, nt), pre_tok + pre_par, [outop(D_MODEL, act)])
    wp = _pack_w_in(W['w_in'])
    proj = _matmul(a, wp, 'nn', "in_proj")

    gcw, gcb = W['gdn_conv_w'], r1(W['gdn_conv_b'])
    scw, scb = W['ssm_conv_w'], r1(W['ssm_conv_b'])
    conv_parts = {}

    def conv_part(name, mode, poff, cw, cb, woff, width):
        fn = functools.partial(_conv_fn, nctx_t, mode)
        bw = min(width, 1024)
        tok_ = [tokop(proj, bw, poff // bw)]
        par_ = [parop(cw, bw, woff // bw), parop(cb, bw, woff // bw)]
        conv_parts[name] = (fn, tok_, par_, width, bw)
        (o,) = _tw_fwd("conv_" + name, fn, (width // bw, nt), tok_ + par_, [outop(width, F32, bw)])
        return o

    q = conv_part('q', 'q', P_QKV, gcw, gcb, 0, 1024)
    k = conv_part('k', 'k', P_QKV + 1024, gcw, gcb, 1024, 1024)
    v = conv_part('v', 'none', P_QKV + 2048, gcw, gcb, 2048, 1024)
    xs = conv_part('xs', 'none', P_XBC, scw, scb, 0, 2048)
    bm = conv_part('bm', 'none', P_XBC + 2048, scw, scb, 2048, 512)
    cm = conv_part('cm', 'none', P_XBC + 2560, scw, scb, 2560, 512)

    z16, z32 = jnp.zeros((16,), F32), jnp.zeros((32,), F32)
    p0 = jnp.concatenate([W['gdn_a_log'].reshape(-1), jnp.zeros((112,), F32)]).reshape(1, 128)
    p1 = jnp.concatenate([W['gdn_dt_bias'].reshape(-1), z16, W['ssm_dt_bias'].reshape(-1), z32]).reshape(1, 128)
    alog = jnp.concatenate([z32, W['ssm_a_log'].reshape(-1), z32]).reshape(1, 128)
    act_tok, act_par = [tokop(proj, 128, P_SMALL // 128)], [parop(p0, 128), parop(p1, 128)]
    (sp,) = _tw_fwd("small_act", _act_fn, (1, nt), act_tok + act_par, [outop(128, F32, 128)])

    gather_late = PlaneExchange([late_shards[n] for n in LATE_WEIGHTS], gather=True) if distributed else None
    o_f, o_b, ss, tri, *late = _scan2_fwd("gdn_scan_fwd", _gdn_step2, [q, k, v, sp], 1024, GDN_HEADS, nctx,
                                          GDN_HEADS, ride=gather_late)
    if distributed:
        W = dict(W, **{n: _from_shards(n, g) for n, g in zip(LATE_WEIGHTS, late)})
    ssd_consts = _ssd_consts()
    y2, hs = _scan_fwd("ssd_scan_fwd", _ssd_step, [xs, bm, cm, sp], [alog], ssd_consts, 2048, SSM_PAIRS, nctx)

    tlm = 128
    ntm = t // tlm
    gnw = r1(W['gdn_norm_w'])
    dl = jnp.repeat(W['ssm_d'], 64).reshape(1, 2048)
    snw = r1(W['ssm_norm_w'])
    mg_tok = [tokop([o_f, o_b], 1024, 0, tlm), tokop(proj, 1024, P_ZG // 1024, tlm)]
    mg_par = [parop(gnw, 128)]
    (og,) = _tw_fwd("mix_gdn", _mixg_fn, (1, ntm), mg_tok + mg_par, [outop(1024, act, 1024, tlm)])
    ms_tok = [tokop(y2, 2048, 0, tlm), tokop(xs, 2048, 0, tlm), tokop(proj, 2048, P_ZS // 2048, tlm)]
    ms_par = [parop(dl, 2048), parop(snw, 2048)]
    (yg,) = _tw_fwd("mix_ssm", _mixs_fn, (1, ntm), ms_tok + ms_par, [outop(2048, act, 2048, tlm)])

    pg = _matmul(og, W['w_br_gdn'], 'nn', "br_gdn", out_dtype=act)
    ps = _matmul(yg, W['w_br_ssm'], 'nn', "br_ssm", out_dtype=act)
    mr_tok = [tokop(proj, 1024, P_GATE // 1024), tokop(proj, 1024, P_GATE // 1024 + 1), tokop(pg), tokop(ps)]
    (mrg,) = _tw_fwd("merge", _merge_fn, (1, nt), mr_tok, [outop(1024, act)])
    mo = _matmul(mrg, W['w_out'], 'nn', "w_out")

    n2w = r1(W['norm2_w'])
    n2_tok, n2_par = [tokop(xt), tokop(mo)], [mod(2), parop(n2w, D_MODEL), mod(4), mod(3)]
    h1, f = _tw_fwd("norm2_fwd", _norm2_fn, (1, nt), n2_tok + n2_par, [outop(1024, F32), outop(1024, act)])
    u = _matmul(f, W['w_ffn_in'], 'nn', "ffn_in", out_dtype=act)
    swb = D_FF // 2
    sw_tok = [tokop(u, swb, 0), tokop(u, swb, D_FF // swb)]
    (sw,) = _tw_fwd("swiglu", _swiglu_fn, (D_FF // swb, nt), sw_tok, [outop(D_FF, act, swb)])
    ff = _matmul(sw, W['w_ffn_out'], 'nn', "ffn_out")

    loss8, dh1, dff, dg2, dnf = _tail(h1, ff, mods, r1(W['norm_f_w']), tgt, nctx_t, tl)
    loss = loss8[0, 0]

    G = {}
    G['norm_f_w'] = dnf.reshape(-1)
    G['w_ffn_out'] = _matmul(sw, dff, 'tn', "d_ffn_out", out_dtype=_GRAD_DTYPE)
    dsw = _matmul(dff, W['w_ffn_out'], 'nt', "d_sw", out_dtype=act)
    dug, duu = _tw_bwd("swiglu_bwd", _swiglu_fn, (D_FF // swb, nt), sw_tok, [], [[tokop(dsw, swb)]],
                       [outop(D_FF, act, swb), outop(D_FF, act, swb)], [])
    du = jnp.concatenate([dug, duu], axis=1)
    G['w_ffn_in'] = _matmul(f, du, 'tn', "d_ffn_in", out_dtype=_GRAD_DTYPE, stacked_out=True)
    df = _matmul(du, W['w_ffn_in'], 'nt', "d_f")
    dxt1, dmo, dg1, dn2, dsc2, dsh2 = _tw_bwd(
        "norm2_bwd", _norm2_fn, (1, nt), n2_tok, n2_par, [[tokop(dh1)], [tokop(df)]],
        [outop(1024, F32), outop(1024, act)],
        [parout(8, 1024, 1024), parout(1, 1024, 1024), parout(8, 1024, 1024), parout(8, 1024, 1024)])
    G['norm2_w'] = dn2.reshape(-1)
    G['w_out'] = _matmul(mrg, dmo, 'tn', "d_w_out", out_dtype=_GRAD_DTYPE)
    dmrg = _matmul(dmo, W['w_out'], 'nt', "d_mrg", out_dtype=act)
    dgg, dgs, dpg, dps = _tw_bwd("merge_bwd", _merge_fn, (1, nt), mr_tok, [], [[tokop(dmrg)]],
                                 [outop(1024, act), outop(1024, act), outop(1024, act), outop(1024, act)], [])
    G['w_br_gdn'] = _matmul(og, dpg, 'tn', "d_br_gdn", out_dtype=_GRAD_DTYPE)
    G['w_br_ssm'] = _matmul(yg, dps, 'tn', "d_br_ssm", out_dtype=_GRAD_DTYPE)
    dog = _matmul(dpg, W['w_br_gdn'], 'nt', "d_og")
    dyg = _matmul(dps, W['w_br_ssm'], 'nt', "d_yg")

    do, dzg, dgnw = _tw_bwd("mix_gdn_bwd", _mixg_fn, (1, ntm), mg_tok, mg_par, [[tokop(dog, 1024, 0, tlm)]],
                            [outop(1024, F32, 1024, tlm), outop(1024, act, 1024, tlm)], [parout(1, 128, 128)])
    G['gdn_norm_w'] = dgnw.reshape(-1)
    dy, dxs_a, dzs, ddl, dsnw = _tw_bwd(
        "mix_ssm_bwd", _mixs_fn, (1, ntm), ms_tok, ms_par, [[tokop(dyg, 2048, 0, tlm)]],
        [outop(2048, F32, 2048, tlm), outop(2048, F32, 2048, tlm), outop(2048, act, 2048, tlm)],
        [parout(1, 2048, 2048), parout(1, 2048, 2048)])
    G['ssm_d'] = ddl.reshape(SSM_HEADS, 64).sum(axis=1)
    G['ssm_norm_w'] = dsnw.reshape(-1)

    dq2, dk2, dv2, dsp_g = _scan_bwd("gdn_scan_bwd", _gdn_step, [q, k, v, sp], [], ss, do, GDN_HEADS, nctx, aux=tri)
    received = {}
    scatter_late = (PlaneExchange([_to_dest_blocks(n, G.pop(n)) for n in LATE_WEIGHTS], gather=False)
                    if distributed else None)
    dxs2, dbm2, dcm2, dsp_s, dalog, *got = _scan_bwd("ssd_scan_bwd", _ssd_step, [xs, bm, cm, sp], [alog], hs, dy,
                                                     SSM_PAIRS, nctx, ride=scatter_late, consts=ssd_consts)
    received.update(zip(LATE_WEIGHTS, got))
    G['ssm_a_log'] = dalog[0, 32:96].reshape(2, SSM_HEADS)

    dconv_w, dconv_b, dpre = {}, {}, {}

    def conv_bwd(name, cots_):
        fn, tok_, par_, width, bw = conv_parts[name]
        dpre[name], dconv_w[name], dconv_b[name] = _tw_bwd(
            "conv_" + name + "_bwd", fn, (width // bw, nt), tok_, par_, [[tokop(c_, bw) for c_ in cots_]],
            [outop(width, act, bw)], [parout(3, width, bw), parout(1, width, bw)])

    conv_bwd('q', [dq2])
    conv_bwd('k', [dk2])
    conv_bwd('v', [dv2])
    conv_bwd('xs', [dxs2, dxs_a])
    conv_bwd('bm', [dbm2])
    conv_bwd('cm', [dcm2])
    G['gdn_conv_w'] = jnp.concatenate([dconv_w['q'], dconv_w['k'], dconv_w['v']], axis=1)
    G['gdn_conv_b'] = jnp.concatenate([dconv_b['q'], dconv_b['k'], dconv_b['v']], axis=1).reshape(-1)
    G['ssm_conv_w'] = jnp.concatenate([dconv_w['xs'], dconv_w['bm'], dconv_w['cm']], axis=1)
    G['ssm_conv_b'] = jnp.concatenate([dconv_b['xs'], dconv_b['bm'], dconv_b['cm']], axis=1).reshape(-1)

    dsmall, dp0, dp1 = _tw_bwd("small_act_bwd", _act_fn, (1, nt), act_tok, act_par,
                               [[tokop(dsp_g, 128), tokop(dsp_s, 128)]], [outop(128, act, 128)],
                               [parout(1, 128, 128), parout(1, 128, 128)])
    G['gdn_a_log'] = dp0[0, 0:16].reshape(2, GDN_HEADS)
    G['gdn_dt_bias'] = dp1[0, 0:16].reshape(2, GDN_HEADS)
    G['ssm_dt_bias'] = dp1[0, 32:96].reshape(2, SSM_HEADS)

    zpad = jnp.zeros((t, P_TOTAL - P_SMALL - 128), act)
    dproj = jnp.concatenate([dpre['q'], dpre['k'], dpre['v'], dzg, dzs, dpre['xs'], dpre['bm'], dpre['cm'],
                             dgg, dgs, dsmall, zpad], axis=1)
    G['w_in'] = _unpack_w_in(_matmul(a, dproj, 'tn', "d_w_in", out_dtype=_GRAD_DTYPE))
    if distributed:
        da, (received['w_in'],) = _matmul(dproj, wp, 'nt', "d_a", ride=PlaneExchange([G.pop('w_in')], gather=False))
    else:
        da = _matmul(dproj, wp, 'nt', "d_a")

    gx_out = Op(((l, D_MODEL), F32), (tl, D_MODEL), lambda j, i: (jnp.maximum(i - nctx_t, 0), 0))
    grad_x, dn1, dsc1, dsh1 = _tw_bwd(
        "prenorm_bwd", pre_fn, (1, nt), pre_tok, pre_par, [[tokop(da)]], [gx_out],
        [parout(1, 1024, 1024), parout(8, 1024, 1024), parout(8, 1024, 1024)], tok_add=tokop(dxt1),
        sem=("arbitrary", "arbitrary"))
    G['norm1_w'] = dn1.reshape(-1)
    dmods = jnp.concatenate([dsh1, dsc1, dg1, dsh2, dsc2, dg2], axis=1)
    if distributed:
        dm_all = _gather_all("gather_dmods", dmods)
        ns = W['ada_w'].shape[1]
        mine = lax.dynamic_slice_in_dim(dm_all, (2 * lax.axis_index("x") + lax.axis_index("y")) * ns, ns, axis=2)
        G['ada_w'], dcc = _ada_bwd_shard(cc16, W['ada_w'], mine[:, 0], mine[:, 1])
        G['ada_b'] = jnp.where(lin == 0, _rowsum2(dm_all[:, 0], dm_all[:, 1]).reshape(-1), 0.0)
        G['c_ctx'] = dcc[0]
    else:
        G['ada_w'], dab, dcc = _ada_bwd(cc, W['ada_w'], dmods)
        G['ada_b'] = dab.reshape(-1)
        G['c_ctx'] = dcc[1]
    return loss, grad_x, G, received


def _row_tile(r, c):
    for tr in (512, 256, 128, 64, 32, 16, 8):
        if r % tr == 0 and tr * c * 4 <= (1 << 20):
            return tr
    return r


def _sum4(name, rv):
    _, r, c = rv.shape
    tr = _row_tile(r, c)

    def body(r_ref, o_ref):
        o_ref[...] = ((r_ref[0].astype(F32) + r_ref[1].astype(F32)) + r_ref[2].astype(F32)) + r_ref[3].astype(F32)

    return pl.pallas_call(
        body, grid=(r // tr,), in_specs=[pl.BlockSpec((N_PLANE, tr, c), lambda i: (0, i, 0))],
        out_specs=pl.BlockSpec((tr, c), lambda i: (i, 0)), out_shape=jax.ShapeDtypeStruct((r, c), F32),
        compiler_params=_cp(("parallel",)), name=name)(rv)


def _adamw(name, w, m, v, p, q=None):
    r, c = w.shape
    tr = _row_tile(r, c)
    grads = [p] if q is None else [p, q]

    def body(w_ref, m_ref, v_ref, *refs):
        g_ref, d_ref, mo_ref, vo_ref = refs[len(grads):]
        g = refs[0][...] if q is None else refs[0][...] + refs[1][...]
        mn = ADAM_B1 * m_ref[...] + (1.0 - ADAM_B1) * g
        vn = ADAM_B2 * v_ref[...] + (1.0 - ADAM_B2) * jnp.square(g)
        m_hat = mn / (1.0 - ADAM_B1 ** ADAM_STEP)
        v_hat = vn / (1.0 - ADAM_B2 ** ADAM_STEP)
        g_ref[...] = g
        d_ref[...] = -ADAM_LR * (m_hat / (jnp.sqrt(v_hat) + ADAM_EPS) + ADAM_WD * w_ref[...])
        mo_ref[...] = mn
        vo_ref[...] = vn

    spec = pl.BlockSpec((tr, c), lambda i: (i, 0))
    return pl.pallas_call(
        body, grid=(r // tr,), in_specs=[spec] * (3 + len(grads)), out_specs=[spec] * 4,
        out_shape=[jax.ShapeDtypeStruct((r, c), F32)] * 4, compiler_params=_cp(("parallel",)), name=name)(w, m, v, *grads)


SMALL_ROWS = 24


def _pack_small(d):
    v = jnp.concatenate([d[n].reshape(-1) for n in SMALL])
    return jnp.pad(v, (0, SMALL_ROWS * 1024 - v.shape[0])).reshape(SMALL_ROWS, 1024)


def _unpack_small(buf, like):
    v = buf.reshape(-1)
    out, off = {}, 0
    for n in SMALL:
        sz = like[n].size
        out[n] = v[off:off + sz].reshape(like[n].shape)
        off += sz
    return out


def kernel(x, c, ctx, c_ctx, ada_w, ada_b, norm1_w, w_in, gdn_conv_w, gdn_conv_b, gdn_a_log, gdn_dt_bias, gdn_norm_w, ssm_conv_w, ssm_conv_b, ssm_a_log, ssm_dt_bias, ssm_d, ssm_norm_w, w_br_gdn, w_br_ssm, w_out, norm2_w, w_ffn_in, w_ffn_out, norm_f_w, loss_target, m_c_ctx, m_ada_w, m_ada_b, m_norm1_w, m_w_in, m_gdn_conv_w, m_gdn_conv_b, m_gdn_a_log, m_gdn_dt_bias, m_gdn_norm_w, m_ssm_conv_w, m_ssm_conv_b, m_ssm_a_log, m_ssm_dt_bias, m_ssm_d, m_ssm_norm_w, m_w_br_gdn, m_w_br_ssm, m_w_out, m_norm2_w, m_w_ffn_in, m_w_ffn_out, m_norm_f_w, v_c_ctx, v_ada_w, v_ada_b, v_norm1_w, v_w_in, v_gdn_conv_w, v_gdn_conv_b, v_gdn_a_log, v_gdn_dt_bias, v_gdn_norm_w, v_ssm_conv_w, v_ssm_conv_b, v_ssm_a_log, v_ssm_dt_bias, v_ssm_d, v_ssm_norm_w, v_w_br_gdn, v_w_br_ssm, v_w_out, v_norm2_w, v_w_ffn_in, v_w_ffn_out, v_norm_f_w):
    args = dict(locals())
    wl = {n: args[n] for n in WEIGHTS}
    ml = {n: args['m_' + n] for n in WEIGHTS}
    vl = {n: args['v_' + n] for n in WEIGHTS}

    def nodepth(n, a):
        return a if n in ('c_ctx', 'norm_f_w') else a[0]

    shard = {n: nodepth(n, wl[n]).astype(_MXU_DTYPE if n in MXU_WEIGHTS else F32) for n in SHARDED}
    first = [n for n in SHARDED if n not in LATE_WEIGHTS and n != 'ada_w']
    W = {n: nodepth(n, wl[n]) for n in SMALL}
    W['ada_w'] = shard['ada_w']
    for n, g in zip(first, _gather_two_level("all_gather_plane", [shard[n] for n in first])):
        W[n] = _from_shards(n, g)

    loss_local, grad_x, G, received = _local_step(x[0], c, ctx[0], loss_target[0], W,
                                                  late_shards={n: shard[n] for n in LATE_WEIGHTS})
    loss = lax.psum(loss_local, ("x", "y", "c"))

    small_g = _pack_small(G)
    last = [n for n in SHARDED if n not in received and n != 'ada_w']
    send = [_to_dest_blocks(n, G[n]) for n in last] + [jnp.broadcast_to(small_g[None], (N_PLANE,) + small_g.shape)]
    received.update(zip(last + ['small'], _plane_exchange("scatter_plane", send, gather=False)))
    names = [n for n in SHARDED if n != 'ada_w'] + ['small']
    plane_sum = [_sum4("sum4_" + n, received[n]) for n in names]
    other = _swap_sibling(plane_sum)

    wd = {n: nodepth(n, wl[n]) for n in WEIGHTS}
    md = {n: nodepth(n, ml[n]) for n in WEIGHTS}
    vd = {n: nodepth(n, vl[n]) for n in WEIGHTS}
    res = {'ada_w': [o.reshape(wl['ada_w'].shape)
                     for o in _adamw("adamw_ada_w", wd['ada_w'], md['ada_w'], vd['ada_w'], G['ada_w'])]}
    for n, p, q in zip(names, plane_sum, other):
        if n == 'small':
            outs = _adamw("adamw_small", _pack_small(wd), _pack_small(md), _pack_small(vd), p, q)
            unpacked = [_unpack_small(o, wd) for o in outs]
            for sn in SMALL:
                res[sn] = [u[sn].reshape(wl[sn].shape) for u in unpacked]
        else:
            outs = _adamw("adamw_" + n, wd[n], md[n], vd[n], p, q)
            res[n] = [o.reshape(wl[n].shape) for o in outs]
    flat = [res[n][kind] for kind in range(4) for n in WEIGHTS]
    return (loss, grad_x[None], *flat)
```

```python
import functools

import jax
import jax.numpy as jnp
from jax import lax
from jax.experimental import pallas as pl
from jax.experimental.pallas import tpu as pltpu

F32 = jnp.float32
HI = lax.Precision.HIGHEST
_MXU_DTYPE = jnp.bfloat16
_SCAN_DTYPE = jnp.bfloat16
_GRAD_DTYPE = jnp.bfloat16

D_MODEL = 1024
EPS = 1e-6
CHUNK = 64
GRID_W = 64
GDN_HEADS = 8
GDN_DK = 128
SSM_HEADS = 32
SSM_PAIRS = 16
SSD_GROUP = 4
D_FF = 2816
D_IN_PROJ = 11360
N_PLANE = 4
P_QKV, P_ZG, P_GATE, P_ZS, P_XBC, P_SMALL, P_TOTAL = 0, 3072, 4096, 6144, 8192, 11264, 11776
IN_SEGMENTS = [(0, 3072, P_QKV), (3072, 4096, P_ZG), (4096, 4112, P_SMALL), (4112, 4128, P_SMALL + 16),
               (4128, 6176, P_ZS), (6176, 9248, P_XBC), (9248, 9312, P_SMALL + 32), (9312, 11360, P_GATE)]
ADAM_LR, ADAM_B1, ADAM_B2, ADAM_EPS, ADAM_WD, ADAM_STEP = 0.001, 0.9, 0.999, 1e-08, 0.01, 10

VMEM_LIMIT = 48 * 1024 * 1024

WEIGHTS = ['c_ctx', 'ada_w', 'ada_b', 'norm1_w', 'w_in', 'gdn_conv_w', 'gdn_conv_b', 'gdn_a_log', 'gdn_dt_bias',
           'gdn_norm_w', 'ssm_conv_w', 'ssm_conv_b', 'ssm_a_log', 'ssm_dt_bias', 'ssm_d', 'ssm_norm_w', 'w_br_gdn',
           'w_br_ssm', 'w_out', 'norm2_w', 'w_ffn_in', 'w_ffn_out', 'norm_f_w']
SHARD_AXIS = {'ada_w': 1, 'w_in': 1, 'gdn_conv_w': 1, 'ssm_conv_w': 1, 'w_br_gdn': 0, 'w_br_ssm': 0, 'w_out': 0,
              'w_ffn_in': 1, 'w_ffn_out': 0}
SHARDED = [n for n in WEIGHTS if n in SHARD_AXIS]
SMALL = [n for n in WEIGHTS if n not in SHARD_AXIS]
MXU_WEIGHTS = ['ada_w', 'w_in', 'w_br_gdn', 'w_br_ssm', 'w_out', 'w_ffn_in', 'w_ffn_out']


def _cp(sem):
    return pltpu.CompilerParams(dimension_semantics=sem, vmem_limit_bytes=VMEM_LIMIT)


def _pick(n, cands):
    for c in cands:
        if n % c == 0:
            return c
    raise ValueError(f"no tile for {n}")


HBM = pl.BlockSpec(memory_space=pl.ANY)
MESH = pl.DeviceIdType.MESH


def _plane_peers():
    x, y, c = lax.axis_index("x"), lax.axis_index("y"), lax.axis_index("c")
    return (x, y, c), [(1 - x, y, c), (x, 1 - y, c), (1 - x, 1 - y, c)]


class PlaneExchange:
    def __init__(self, arrs, gather):
        self.arrs, self.gather, self.n = list(arrs), gather, len(arrs)
        self.in_specs = [HBM] * self.n
        self.out_specs = [HBM] * self.n
        self.out_shape = [jax.ShapeDtypeStruct((N_PLANE,) + a.shape if gather else a.shape, a.dtype) for a in self.arrs]
        self.scratch = [pltpu.SemaphoreType.DMA((3 * self.n,)), pltpu.SemaphoreType.DMA((3 * self.n,)),
                        pltpu.SemaphoreType.DMA((self.n,))]

    def _copies(self, ins, outs, sems):
        send_sems, recv_sems, local_sems = sems
        (x, y, c), peers = _plane_peers()
        me = 2 * x + y
        copies = []
        for ti in range(self.n):
            src = ins[ti] if self.gather else ins[ti].at[me]
            copies.append(pltpu.make_async_copy(src, outs[ti].at[me], local_sems.at[ti]))
            for kk, (px, py, pc) in enumerate(peers):
                src = ins[ti] if self.gather else ins[ti].at[2 * px + py]
                copies.append(pltpu.make_async_remote_copy(
                    src_ref=src, dst_ref=outs[ti].at[me], send_sem=send_sems.at[3 * ti + kk],
                    recv_sem=recv_sems.at[3 * ti + kk], device_id=(px, py, pc), device_id_type=MESH))
        return copies

    def start(self, ins, outs, sems):
        for cp in self._copies(ins, outs, sems):
            cp.start()

    def wait(self, ins, outs, sems):
        for cp in self._copies(ins, outs, sems):
            cp.wait()


def _plane_exchange(name, arrs, gather):
    ex = PlaneExchange(arrs, gather)
    n = ex.n

    def body(*refs):
        ins, outs, sems = refs[:n], refs[n:2 * n], refs[2 * n:]
        ex.start(ins, outs, sems)
        ex.wait(ins, outs, sems)

    return pl.pallas_call(body, in_specs=ex.in_specs, out_specs=ex.out_specs, out_shape=ex.out_shape,
                          scratch_shapes=ex.scratch, name=name)(*arrs)


def _gather_two_level(name, arrs):
    n = len(arrs)
    split = [a.shape[0] % 32 == 0 for a in arrs]

    def body(*refs):
        ins, outs = refs[:n], refs[n:2 * n]
        ici_send, ici_recv, d2d_send, d2d_recv, local_sems = refs[2 * n:]
        (x, y, c), peers = _plane_peers()
        me = 2 * x + y

        def part(ref, ti):
            if not split[ti]:
                return ref
            rows = arrs[ti].shape[0] // 2
            return ref.at[pl.ds(c * rows, rows)]

        local = [pltpu.make_async_copy(ins[ti], outs[ti].at[me], local_sems.at[ti]) for ti in range(n)]
        ici = [pltpu.make_async_remote_copy(
            src_ref=part(ins[ti], ti), dst_ref=part(outs[ti].at[me], ti), send_sem=ici_send.at[3 * ti + kk],
            recv_sem=ici_recv.at[3 * ti + kk], device_id=peer, device_id_type=MESH)
            for ti in range(n) for kk, peer in enumerate(peers)]
        for cp in local + ici:
            cp.start()
        d2d = []
        for ti in range(n):
            for kk, (px, py, pc) in enumerate(peers):
                ici[3 * ti + kk].wait_recv()
                if split[ti]:
                    piece = part(outs[ti].at[2 * px + py], ti)
                    cp = pltpu.make_async_remote_copy(
                        src_ref=piece, dst_ref=piece, send_sem=d2d_send.at[3 * ti + kk],
                        recv_sem=d2d_recv.at[3 * ti + kk], device_id=(x, y, 1 - c), device_id_type=MESH)
                    cp.start()
                    d2d.append(cp)
        for cp in ici:
            cp.wait_send()
        for cp in d2d:
            cp.wait()
        for cp in local:
            cp.wait()

    return pl.pallas_call(
        body, in_specs=[HBM] * n, out_specs=[HBM] * n,
        out_shape=[jax.ShapeDtypeStruct((N_PLANE,) + a.shape, a.dtype) for a in arrs],
        scratch_shapes=[pltpu.SemaphoreType.DMA((3 * n,))] * 4 + [pltpu.SemaphoreType.DMA((n,))], name=name)(*arrs)


def _gather_all(name, arr):
    flips = [(fx, fy, fc) for fx in (0, 1) for fy in (0, 1) for fc in (0, 1)][1:]

    def body(a_ref, o_ref, send_sems, recv_sems, local_sem):
        x, y, c = lax.axis_index("x"), lax.axis_index("y"), lax.axis_index("c")
        me = 4 * x + 2 * y + c
        copies = [pltpu.make_async_copy(a_ref, o_ref.at[me], local_sem)]
        for kk, (fx, fy, fc) in enumerate(flips):
            peer = (1 - x if fx else x, 1 - y if fy else y, 1 - c if fc else c)
            copies.append(pltpu.make_async_remote_copy(
                src_ref=a_ref, dst_ref=o_ref.at[me], send_sem=send_sems.at[kk], recv_sem=recv_sems.at[kk],
                device_id=peer, device_id_type=MESH))
        for cp in copies:
            cp.start()
        for cp in copies:
            cp.wait()

    return pl.pallas_call(
        body, in_specs=[HBM], out_specs=HBM, out_shape=jax.ShapeDtypeStruct((8,) + arr.shape, arr.dtype),
        scratch_shapes=[pltpu.SemaphoreType.DMA((7,)), pltpu.SemaphoreType.DMA((7,)), pltpu.SemaphoreType.DMA],
        name=name)(arr)


def _swap_sibling(arrs):
    n = len(arrs)

    def body(*refs):
        ins, outs, send_sems, recv_sems = refs[:n], refs[n:2 * n], refs[2 * n], refs[2 * n + 1]
        x, y, c = lax.axis_index("x"), lax.axis_index("y"), lax.axis_index("c")
        copies = [pltpu.make_async_remote_copy(src_ref=ins[ti], dst_ref=outs[ti], send_sem=send_sems.at[ti],
                                               recv_sem=recv_sems.at[ti], device_id=(x, y, 1 - c), device_id_type=MESH)
                  for ti in range(n)]
        for cp in copies:
            cp.start()
        for cp in copies:
            cp.wait()

    return pl.pallas_call(
        body, in_specs=[HBM] * n, out_specs=[HBM] * n, out_shape=[jax.ShapeDtypeStruct(a.shape, a.dtype) for a in arrs],
        scratch_shapes=[pltpu.SemaphoreType.DMA((n,)), pltpu.SemaphoreType.DMA((n,))], name="swap_sibling")(*arrs)


MATMUL_VMEM_BUDGET = 36 * 1024 * 1024
TILE_CANDIDATES = (2944, 2816, 1408, 1024, 768, 512, 256, 128)
STEP_COST_BYTES = 1 << 20


def _matmul_tiles(m, n, k, ab, bb, ob, tn_fixed=None, tk_fixed=None):
    def cands(dim, whole_up_to):
        out = [c for c in TILE_CANDIDATES if dim % c == 0]
        if dim <= whole_up_to and dim not in out:
            out.append(dim)
        return out

    best = None
    for tm in cands(m, 3072):
        for tn in ([tn_fixed] if tn_fixed else cands(n, 3072)):
            for tk in ([tk_fixed] if tk_fixed else cands(k, 2048)):
                gi, gj, gk = m // tm, n // tn, k // tk
                vmem = 2 * (tm * tk * ab + tk * tn * bb + tm * tn * ob) + (tm * tn * 4 if gk > 1 else 0)
                if vmem > MATMUL_VMEM_BUDGET:
                    continue
                a_reads = 1 if gk == 1 else gj
                b_reads = 1 if (gk == 1 and gj == 1) else gi
                cost = (m * k * ab * a_reads + k * n * bb * b_reads + m * n * ob + gi * gj * gk * STEP_COST_BYTES)
                if best is None or cost < best[0]:
                    best = (cost, tm, tn, tk)
    assert best is not None, (m, n, k)
    return best[1:]


def _matmul(a, b, form, name, out_dtype=F32, stacked_out=False, ride=None):
    dims = {'nn': (((1,), (0,)), ((), ())), 'nt': (((1,), (1,)), ((), ())), 'tn': (((0,), (0,)), ((), ()))}[form]
    stacked_b = b.ndim == 3
    ns = b.shape[2] if stacked_b else None
    if form == 'nn':
        m, k = a.shape
        n = b.shape[0] * ns if stacked_b else b.shape[1]
    elif form == 'nt':
        m, k = a.shape
        n = b.shape[1] if stacked_b else b.shape[0]
    else:
        k, m = a.shape
        n = b.shape[1]
    ob = jnp.dtype(out_dtype).itemsize
    tm, tn, tk = _matmul_tiles(m, n, k, a.dtype.itemsize, b.dtype.itemsize, ob,
                               tn_fixed=(n // N_PLANE if stacked_out else ns if (stacked_b and form == 'nn') else None),
                               tk_fixed=(ns if (stacked_b and form == 'nt') else None))
    nk = k // tk
    grid = (m // tm, n // tn, nk)
    n_ride = ride.n if ride is not None else 0

    def body(*refs):
        a_ref, b_ref = refs[0], refs[1]
        ride_in = refs[2:2 + n_ride]
        o_ref = refs[2 + n_ride]
        ride_out = refs[3 + n_ride:3 + 2 * n_ride]
        acc_ref = refs[3 + 2 * n_ride]
        sems = refs[4 + 2 * n_ride:]
        i, j, kk = pl.program_id(0), pl.program_id(1), pl.program_id(2)
        if ride is not None:
            @pl.when((i == 0) & (j == 0) & (kk == 0))
            def _():
                ride.start(ride_in, ride_out, sems)

        def put(val):
            if stacked_out:
                o_ref[0] = val.astype(o_ref.dtype)
            else:
                o_ref[...] = val.astype(o_ref.dtype)

        bv = b_ref[0] if stacked_b else b_ref[...]
        part = lax.dot_general(a_ref[...].astype(_MXU_DTYPE), bv.astype(_MXU_DTYPE), dims, preferred_element_type=F32)
        if nk == 1:
            put(part)
        else:
            @pl.when(kk == 0)
            def _():
                acc_ref[...] = part

            @pl.when(kk > 0)
            def _():
                acc_ref[...] += part

            @pl.when(kk == nk - 1)
            def _():
                put(acc_ref[...])

        if ride is not None:
            @pl.when((i == grid[0] - 1) & (j == grid[1] - 1) & (kk == nk - 1))
            def _():
                ride.wait(ride_in, ride_out, sems)

    if form == 'nn':
        a_spec = pl.BlockSpec((tm, tk), lambda i, j, kk: (i, kk))
        b_spec = (pl.BlockSpec((1, tk, tn), lambda i, j, kk: (j, kk, 0)) if stacked_b
                  else pl.BlockSpec((tk, tn), lambda i, j, kk: (kk, j)))
    elif form == 'nt':
        a_spec = pl.BlockSpec((tm, tk), lambda i, j, kk: (i, kk))
        b_spec = (pl.BlockSpec((1, tn, tk), lambda i, j, kk: (kk, j, 0)) if stacked_b
                  else pl.BlockSpec((tn, tk), lambda i, j, kk: (j, kk)))
    else:
        a_spec = pl.BlockSpec((tk, tm), lambda i, j, kk: (kk, i))
        b_spec = pl.BlockSpec((tk, tn), lambda i, j, kk: (kk, j))
    if stacked_out:
        o_spec = pl.BlockSpec((1, tm, tn), lambda i, j, kk: (j, i, 0))
        o_shape = jax.ShapeDtypeStruct((N_PLANE, m, tn), out_dtype)
    else:
        o_spec = pl.BlockSpec((tm, tn), lambda i, j, kk: (i, j))
        o_shape = jax.ShapeDtypeStruct((m, n), out_dtype)
    acc_shape = (tm, tn) if nk > 1 else (8, 128)
    if ride is None:
        return pl.pallas_call(
            body, grid=grid, in_specs=[a_spec, b_spec], out_specs=o_spec, out_shape=o_shape,
            scratch_shapes=[pltpu.VMEM(acc_shape, F32)],
            compiler_params=_cp(("parallel", "parallel", "arbitrary")), name=name)(a, b)
    res = pl.pallas_call(
        body, grid=grid, in_specs=[a_spec, b_spec] + ride.in_specs, out_specs=[o_spec] + ride.out_specs,
        out_shape=[o_shape] + ride.out_shape, scratch_shapes=[pltpu.VMEM(acc_shape, F32)] + ride.scratch,
        compiler_params=_cp(("arbitrary", "arbitrary", "arbitrary")), name=name)(a, b, *ride.arrs)
    return res[0], res[1:]


class Op:
    def __init__(self, arr, bs, im, load=None):
        self.arr, self.bs, self.im = arr, bs, im
        self.arrs = list(arr) if isinstance(arr, list) else [arr]
        self.load = load or (lambda r: r[...].astype(F32))

    def spec(self):
        return pl.BlockSpec(self.bs, self.im)

    def value(self, it):
        return functools.reduce(lambda u, w: u + w, [self.load(next(it)) for _ in self.arrs])


def _op_specs(ops):
    return [op.spec() for op in ops for _ in op.arrs]


def _op_arrays(ops):
    return [a for op in ops for a in op.arrs]


def _sum_dirs(r):
    return r[0].astype(F32) + r[1].astype(F32)


def _tw_fwd(name, fn, grid, ins, outs):
    def body(*refs):
        info = (pl.program_id(0), pl.program_id(1))
        it = iter(refs)
        vals = [op.value(it) for op in ins]
        res = fn(info, *vals)
        for r, v in zip(it, res):
            r[...] = v.astype(r.dtype)

    return pl.pallas_call(
        body, grid=grid, in_specs=_op_specs(ins), out_specs=[op.spec() for op in outs],
        out_shape=[jax.ShapeDtypeStruct(*op.arr) for op in outs],
        compiler_params=_cp(("parallel", "arbitrary")), name=name)(*_op_arrays(ins))


def _tw_bwd(name, fn, grid, tok, par, cots, tok_out, par_out, tok_add=None, sem=("parallel", "arbitrary"), into=None):
    n_tok = len(tok)
    flat_cots = [op for group in cots for op in group]
    extra = [tok_add] if tok_add is not None else []
    out_ops = list(tok_out) + list(par_out)
    into = into or {}

    def body(*refs):
        info = (pl.program_id(0), pl.program_id(1))
        it = iter(refs)
        tok_v = [op.value(it) for op in tok]
        par_v = [op.value(it) for op in par]
        cot_v = [functools.reduce(lambda u, w: u + w, [op.value(it) for op in group]) for group in cots]
        add_v = [op.value(it) for op in extra]
        for _ in into:
            next(it)
        _, pull = jax.vjp(lambda *a: fn(info, *a), *tok_v, *par_v)
        grads = pull(tuple(cot_v))
        for i in range(n_tok):
            r = next(it)
            g = grads[i] + add_v[0] if (i == 0 and add_v) else grads[i]
            if r.shape[-1] > g.shape[-1]:
                g = jnp.concatenate([g, jnp.zeros((g.shape[0], r.shape[-1] - g.shape[-1]), g.dtype)], axis=1)
            r[...] = g.astype(r.dtype)
        first = pl.program_id(1) == 0
        for i in range(len(par)):
            r = next(it)
            g = grads[n_tok + i]

            @pl.when(first)
            def _(r=r, g=g):
                r[...] = g

            @pl.when(jnp.logical_not(first))
            def _(r=r, g=g):
                r[...] += g

    ops = tok + par + flat_cots + extra
    n_in = len(_op_arrays(ops))
    return pl.pallas_call(
        body, grid=grid, in_specs=_op_specs(ops) + [HBM] * len(into), out_specs=[op.spec() for op in out_ops],
        out_shape=[jax.ShapeDtypeStruct(*op.arr) for op in out_ops],
        input_output_aliases={n_in + k: i for k, i in enumerate(into)},
        compiler_params=_cp(sem), name=name)(*_op_arrays(ops), *into.values())


def _silu(x):
    return x * jax.nn.sigmoid(x)


def _rms(x):
    return x * lax.rsqrt(jnp.mean(x * x, axis=-1, keepdims=True) + EPS)


@functools.partial(jax.custom_vjp, nondiff_argnums=(1,))
def _shift_rows(x, k):
    return pltpu.roll(x, k % x.shape[0], 0)


def _shift_rows_fwd(x, k):
    return _shift_rows(x, k), None


def _shift_rows_bwd(k, _, g):
    return (_shift_rows(g, -k),)


_shift_rows.defvjp(_shift_rows_fwd, _shift_rows_bwd)


def _prenorm_fn(nctx_t, info, x, w, sc8, sh8):
    is_ctx = info[1] < nctx_t
    sc = jnp.where(is_ctx, sc8[1:2], sc8[0:1])
    sh = jnp.where(is_ctx, sh8[1:2], sh8[0:1])
    return (_rms(x) * w * (1.0 + sc) + sh,)


def _conv_fn(nctx_t, mode, info, x, w, b):
    n, c = x.shape
    is_ctx = info[1] < nctx_t
    idx = lax.broadcasted_iota(jnp.int32, (n, 1), 0)
    rr = jnp.where(is_ctx, idx, idx % GRID_W)
    first = rr == 0
    last = rr == jnp.where(is_ctx, n - 1, GRID_W - 1)
    prev = jnp.where(first, 0.0, _shift_rows(x, 1))
    nxt = jnp.where(last, 0.0, _shift_rows(x, -1))
    y = b + prev * w[0:1] + x * w[1:2] + nxt * w[2:3]
    y = _silu(y)
    if mode == 'none':
        return (y,)
    scale = GDN_DK ** -0.5 if mode == 'q' else 1.0
    outs = []
    for h in range(c // 128):
        yh = y[:, h * 128:(h + 1) * 128]
        outs.append(yh * lax.rsqrt(jnp.sum(yh * yh, axis=-1, keepdims=True) + EPS) * scale)
    return (jnp.concatenate(outs, axis=1),)


def _act_fn(info, x, p0, p1):
    lane = lax.broadcasted_iota(jnp.int32, x.shape, 1)
    sp = jax.nn.softplus(x + p1)
    g = -jnp.exp(p0) * sp
    bt = jax.nn.sigmoid(x)
    return (jnp.where(lane < 16, g, jnp.where(lane < 32, bt, jnp.where(lane < 96, sp, 0.0))),)


def _mixg_fn(info, o, zg, gnw):
    outs = []
    for h in range(GDN_HEADS):
        outs.append(_rms(o[:, h * 128:(h + 1) * 128]) * gnw)
    return (jnp.concatenate(outs, axis=1) * _silu(zg),)


def _mixs_fn(info, y, xs, zs, dl, snw):
    yy = (y + dl * xs) * _silu(zs)
    outs = []
    for g in range(4):
        outs.append(_rms(yy[:, g * 512:(g + 1) * 512]))
    return (jnp.concatenate(outs, axis=1) * snw,)


def _merge_fn(info, gates, pg, ps):
    half = gates.shape[1] // 2
    return (jax.nn.sigmoid(gates[:, :half]) * pg + jax.nn.sigmoid(gates[:, half:]) * ps,)


def _norm2_fn(info, xt, mo, g8, w, sc8, sh8):
    h1 = xt + g8[0:1] * mo
    return (h1, _rms(h1) * w * (1.0 + sc8[0:1]) + sh8[0:1])


def _swiglu_fn(info, ug, uu):
    return (_silu(ug) * uu,)


_NN = (((1,), (0,)), ((), ()))
_NT = (((1,), (1,)), ((), ()))
_TN = (((0,), (0,)), ((), ()))


def _mmh(a, b):
    return lax.dot_general(a, b, _NN, precision=HI, preferred_element_type=F32)


def _dot1(a, b, dims):
    return lax.dot_general(a.astype(_SCAN_DTYPE), b.astype(_SCAN_DTYPE), dims, preferred_element_type=F32)


def _mm(a, b):
    return _dot1(a, b, _NN)


def _mm_nt(a, b):
    return _dot1(a, b, _NT)


def _mm_tn(a, b):
    return _dot1(a, b, _TN)


def _split2(a):
    hi = a.astype(_SCAN_DTYPE)
    return hi, (a - hi.astype(F32)).astype(_SCAN_DTYPE)


def _dot3(a, b, dims):
    ah, al = _split2(a)
    bh, bl = _split2(b)
    d = lambda u, w: lax.dot_general(u, w, dims, preferred_element_type=F32)
    return d(ah, bh) + (d(ah, bl) + d(al, bh))


def _order_masks(d):
    i = lax.broadcasted_iota(jnp.int32, (CHUNK, CHUNK), 0)
    j = lax.broadcasted_iota(jnp.int32, (CHUNK, CHUNK), 1)
    s = jnp.where(d == 0, 1, -1) * (i - j)
    return (s >= 0).astype(F32), (s > 0).astype(F32)


@jax.custom_vjp
def _unit_tri_inv(mats):
    i = lax.broadcasted_iota(jnp.int32, (CHUNK, CHUNK), 0)
    j = lax.broadcasted_iota(jnp.int32, (CHUNK, CHUNK), 1)
    eye = (i == j).astype(F32)
    ps = [-a for a in mats]
    ts = [eye + p for p in ps]
    for _ in range(5):
        ps = [_dot3(p, p, _NN) for p in ps]
        ts = [t + _dot3(t, p, _NN) for t, p in zip(ts, ps)]
    return tuple(ts)


def _uti_fwd(mats):
    ts = _unit_tri_inv(mats)
    return ts, ts


def _uti_bwd(ts, gs):
    inner = [_dot3(g, t, _NT) for g, t in zip(gs, ts)]
    return (tuple(-_dot3(t, u, _TN) for t, u in zip(ts, inner)),)


_unit_tri_inv.defvjp(_uti_fwd, _uti_bwd)


@jax.custom_vjp
def _unit_tri_inv_given(mats, ts):
    return ts


def _utig_fwd(mats, ts):
    return ts, ts


def _utig_bwd(ts, gs):
    return _uti_bwd(ts, gs)[0], tuple(jnp.zeros_like(t) for t in ts)


_unit_tri_inv_given.defvjp(_utig_fwd, _utig_bwd)


def _lane_col(blk, lane_idx):
    lane = lax.broadcasted_iota(jnp.int32, blk.shape, 1)
    return jnp.sum(jnp.where(lane == lane_idx, blk, 0.0), axis=1, keepdims=True)


def _decay_mat(cum, incl):
    cb = jnp.broadcast_to(cum, (CHUNK, CHUNK))
    return jnp.exp(jnp.minimum(cb - cb.T, 0.0)) * incl


def _gdn_chunks(dirs, streams, states, aux=None, want_aux=False, group=None):
    ns = len(dirs)
    masks = [_order_masks(d) for d in dirs]
    cum = [_mmh(masks[i][0], streams[i][3]) for i in range(ns)]
    tot = [jnp.sum(streams[i][3], axis=0, keepdims=True) for i in range(ns)]
    all_units = [(i, h) for i in range(ns) for h in range(GDN_HEADS)]
    group = group or len(all_units)
    cat = jnp.concatenate
    outs, new_states, ts_all = [], [], []
    for g0 in range(0, len(all_units), group):
        units = all_units[g0:g0 + group]
        us = range(len(units))
        st = states[g0:g0 + group]
        qs = [streams[i][0][:, h * 128:(h + 1) * 128] for i, h in units]
        ks = [streams[i][1][:, h * 128:(h + 1) * 128] for i, h in units]
        vs = [streams[i][2][:, h * 128:(h + 1) * 128] for i, h in units]
        gcum = [_lane_col(cum[i], dirs[i] * GDN_HEADS + h) for i, h in units]
        glast = [_lane_col(tot[i], dirs[i] * GDN_HEADS + h) for i, h in units]
        beta = [_lane_col(streams[i][3], 16 + dirs[i] * GDN_HEADS + h) for i, h in units]
        decay = [_decay_mat(gcum[u], masks[units[u][0]][0]) for u in us]
        egc = [jnp.exp(gcum[u]) for u in us]
        kb = [ks[u] * beta[u] for u in us]
        kq = [_mm_nt(cat([kb[u], qs[u]], axis=0), ks[u]) for u in us]
        mats = tuple(kq[u][:CHUNK] * decay[u] * masks[units[u][0]][1] for u in us)
        ts = _unit_tri_inv(mats) if aux is None else _unit_tri_inv_given(mats, tuple(aux[g0:g0 + group]))
        wu = [_mm(ts[u], cat([kb[u] * egc[u], vs[u] * beta[u]], axis=1)) for u in us]
        ws = [_mm(cat([wu[u][:, :128], qs[u] * egc[u]], axis=0), st[u]) for u in us]
        vn = [wu[u][:, 128:] - ws[u][:CHUNK] for u in us]
        outs += [ws[u][CHUNK:] + _mm(kq[u][CHUNK:] * decay[u], vn[u]) for u in us]
        new_states += [st[u] * jnp.exp(glast[u]) + _mm_tn(ks[u] * jnp.exp(glast[u] - gcum[u]), vn[u]) for u in us]
        ts_all += list(ts)
    per_stream = [cat(outs[i * GDN_HEADS:(i + 1) * GDN_HEADS], axis=1) for i in range(ns)]
    return (*per_stream, *new_states, *(ts_all if want_aux else ()))


def _gdn_step(d, q, k, v, spb, *states, aux=None, want_aux=False):
    return _gdn_chunks([d], [(q, k, v, spb)], states, aux, want_aux, group=4)


def _gdn_step2(q0, k0, v0, sp0, q1, k1, v1, sp1, *states, aux=None, want_aux=False):
    return _gdn_chunks([0, 1], [(q0, k0, v0, sp0), (q1, k1, v1, sp1)], states, aux, want_aux)


def _split3(a):
    a1 = a.astype(_SCAN_DTYPE)
    r = a - a1.astype(F32)
    a2 = r.astype(_SCAN_DTYPE)
    return a1, a2, (r - a2.astype(F32)).astype(_SCAN_DTYPE)


def _exact_dot(a, e, dims, split_lhs, passes=3):
    parts = _split3(a if split_lhs else e)[:passes]
    d = (lambda u: lax.dot_general(u, e, dims, preferred_element_type=F32)) if split_lhs else \
        (lambda u: lax.dot_general(a, u, dims, preferred_element_type=F32))
    return functools.reduce(lambda u, w: u + w, [d(p) for p in reversed(parts)])


@jax.custom_vjp
def _spread(a, e):
    return _exact_dot(a, e, _NN, True)


def _spread_fwd(a, e):
    return _spread(a, e), e


def _spread_bwd(e, g):
    return _exact_dot(g, e, _NT, True, passes=2), jnp.zeros_like(e)


_spread.defvjp(_spread_fwd, _spread_bwd)


@jax.custom_vjp
def _colsum_bcast(z):
    return _exact_dot(jnp.ones((z.shape[0], z.shape[0]), _SCAN_DTYPE), z, _NN, False)


def _colsum_fwd(z):
    return _colsum_bcast(z), None


def _colsum_bwd(_, g):
    return (_exact_dot(jnp.ones((g.shape[0], g.shape[0]), _SCAN_DTYPE), g, _NN, False, passes=2),)


_colsum_bcast.defvjp(_colsum_fwd, _colsum_bwd)


def _ssd_consts():
    wdt = SSM_HEADS * 64
    d = lax.broadcasted_iota(jnp.int32, (2, 128, wdt), 0)
    e = (lax.broadcasted_iota(jnp.int32, (2, 128, wdt), 1)
         == 32 + d * SSM_HEADS + lax.broadcasted_iota(jnp.int32, (2, 128, wdt), 2) // 64).astype(_SCAN_DTYPE)
    dd = lax.broadcasted_iota(jnp.int32, (2, CHUNK, wdt), 0)
    ci = lax.broadcasted_iota(jnp.int32, (2, CHUNK, wdt), 1)
    pos = lax.broadcasted_iota(jnp.int32, (2, CHUNK, wdt), 2) % 64
    incl_t = (jnp.where(dd == 0, 1, -1) * (ci - pos) >= 0).astype(F32)
    diag = (ci == pos).astype(F32)
    return [e, incl_t, diag]


def _ssd_step(d, x, bm, cm, spb, alog, *states, consts):
    e, incl_t, diag = consts
    incl, _ = _order_masks(d)
    lane1 = lax.broadcasted_iota(jnp.int32, (1, 128), 1)
    lo_lane = 32 + d * SSM_HEADS
    a_vec = jnp.where(lane1 >= lo_lane, jnp.where(lane1 < lo_lane + SSM_HEADS, -jnp.exp(alog), 0.0), 0.0)
    adt = spb * a_vec
    acum = _mmh(incl, adt)
    alast = jnp.sum(adt, axis=0, keepdims=True)
    dt2 = _spread(spb, e)
    ac2 = _spread(acum, e)
    al2 = _spread(jnp.broadcast_to(alast, (8, 128)), e)[0:1]
    row = _colsum_bcast(ac2 * diag)
    seg = jnp.exp(jnp.minimum(ac2 - row, 0.0)) * incl_t
    xdt = x * dt2
    gam = jnp.exp(ac2)
    xe = xdt * jnp.exp(al2 - ac2)
    low = lax.broadcasted_iota(jnp.int32, (CHUNK, 128), 1) < 64
    row_low = lax.broadcasted_iota(jnp.int32, (128, 1), 0) < 64
    ps = range(SSM_PAIRS)
    sl = [slice(p * 128, (p + 1) * 128) for p in ps]
    bg = [bm[:, g * 128:(g + 1) * 128] for g in range(4)]
    cg = [cm[:, g * 128:(g + 1) * 128] for g in range(4)]
    cb2 = [_mm_nt(cg[g], jnp.concatenate([bg[g], bg[g]], axis=0)) for g in range(4)]
    ys, new_states = [], []
    for p0 in range(0, SSM_PAIRS, SSD_GROUP):
        pg = range(p0, p0 + SSD_GROUP)
        xd = {p: jnp.concatenate([jnp.where(low, xdt[:, sl[p]], 0.0), jnp.where(low, 0.0, xdt[:, sl[p]])], axis=0)
              for p in pg}
        yd = {p: _mm(cb2[p // 4] * seg[:, sl[p]], xd[p]) for p in pg}
        yo = {p: _mm_nt(cg[p // 4], states[p]) for p in pg}
        ys += [yd[p] + gam[:, sl[p]] * yo[p] for p in pg]
        new = {p: _mm_tn(xe[:, sl[p]], bg[p // 4]) for p in pg}
        al0 = {p: _lane_col(alast, lo_lane + 2 * p) for p in pg}
        al1 = {p: _lane_col(alast, lo_lane + 2 * p + 1) for p in pg}
        new_states += [states[p] * jnp.exp(jnp.where(row_low, al0[p], al1[p])) + new[p] for p in pg]
    return (jnp.concatenate(ys, axis=1), *new_states)


def _chunk_of(d, p, nctx, nc):
    return jnp.where(d == 0, p, jnp.where(p < nctx, nctx - 1 - p, nctx + nc - 1 - p))


class _NoRide:
    n, arrs, in_specs, out_specs, out_shape, scratch = 0, [], [], [], [], []


def _const_specs(consts):
    return [pl.BlockSpec((1,) + a.shape[1:], lambda d, p: (d,) + (0,) * (a.ndim - 1)) for a in consts]


def _scan_fwd(name, step, toks, pars, consts, out_width, n_state, nctx):
    t = toks[0].shape[0]
    nc = t // CHUNK
    n_tok, n_par, n_const = len(toks), len(pars), len(consts)

    def body(*refs):
        it = iter(refs)
        tok_refs = [next(it) for _ in range(n_tok)]
        par_refs = [next(it) for _ in range(n_par)]
        const_refs = [next(it) for _ in range(n_const)]
        o_ref, ss_ref, s_scr = next(it), next(it), next(it)
        d, p = pl.program_id(0), pl.program_id(1)

        @pl.when(p == 0)
        def _():
            s_scr[...] = jnp.zeros(s_scr.shape, F32)

        ss_ref[0, 0] = s_scr[...]
        res = step(d, *[r[...] for r in tok_refs], *[r[...] for r in par_refs], *[s_scr[h] for h in range(n_state)],
                   consts=[r[0] for r in const_refs])
        o_ref[0] = res[0]
        for h in range(n_state):
            s_scr[h] = res[1 + h]

    ch = lambda d, p: _chunk_of(d, p, nctx, nc)
    in_specs = [pl.BlockSpec((CHUNK, a.shape[1]), lambda d, p: (ch(d, p), 0)) for a in toks]
    in_specs += [pl.BlockSpec(a.shape, lambda d, p: (0, 0)) for a in pars]
    return pl.pallas_call(
        body, grid=(2, nc), in_specs=in_specs + _const_specs(consts),
        out_specs=[pl.BlockSpec((1, CHUNK, out_width), lambda d, p: (d, ch(d, p), 0)),
                   pl.BlockSpec((1, 1, n_state, 128, 128), lambda d, p: (d, p, 0, 0, 0))],
        out_shape=[jax.ShapeDtypeStruct((2, t, out_width), F32),
                   jax.ShapeDtypeStruct((2, nc, n_state, 128, 128), F32)],
        scratch_shapes=[pltpu.VMEM((n_state, 128, 128), F32)],
        compiler_params=_cp(("arbitrary", "arbitrary")), name=name)(*toks, *pars, *consts)


def _scan_bwd(name, step, toks, pars, ss, dout, n_state, nctx, ride=None, aux=None, consts=()):
    t = toks[0].shape[0]
    nc = t // CHUNK
    n_tok, n_par, n_const = len(toks), len(pars), len(consts)
    rd = ride if ride is not None else _NoRide
    n_aux = aux.shape[2] if aux is not None else 0

    def body(*refs):
        it = iter(refs)
        tok_refs = [next(it) for _ in range(n_tok)]
        par_refs = [next(it) for _ in range(n_par)]
        ss_ref, do_ref = next(it), next(it)
        aux_ref = next(it) if n_aux else None
        const_refs = [next(it) for _ in range(n_const)]
        ride_in = [next(it) for _ in range(rd.n)]
        dtok_refs = [next(it) for _ in range(n_tok)]
        dpar_refs = [next(it) for _ in range(n_par)]
        ride_out = [next(it) for _ in range(rd.n)]
        ds_scr = next(it)
        sems = list(it)
        d, pr = pl.program_id(0), pl.program_id(1)
        if ride is not None:
            @pl.when((d == 0) & (pr == 0))
            def _():
                ride.start(ride_in, ride_out, sems)

            @pl.when((d == 1) & (pr == nc - 1))
            def _():
                ride.wait(ride_in, ride_out, sems)

        @pl.when(pr == 0)
        def _():
            ds_scr[...] = jnp.zeros(ds_scr.shape, F32)

        kw = dict(aux=[aux_ref[0, 0, i] for i in range(n_aux)]) if n_aux else {}
        if n_const:
            kw['consts'] = [r[0] for r in const_refs]
        _, pull = jax.vjp(functools.partial(step, d, **kw), *[r[...] for r in tok_refs], *[r[...] for r in par_refs],
                          *[ss_ref[0, 0, h] for h in range(n_state)])
        grads = pull((do_ref[...], *[ds_scr[h] for h in range(n_state)]))
        for r, g in zip(dtok_refs, grads[:n_tok]):
            r[0] = g
        for h in range(n_state):
            ds_scr[h] = grads[n_tok + n_par + h]
        first = (d == 0) & (pr == 0)
        for r, g in zip(dpar_refs, grads[n_tok:n_tok + n_par]):
            @pl.when(first)
            def _(r=r, g=g):
                r[...] = g

            @pl.when(jnp.logical_not(first))
            def _(r=r, g=g):
                r[...] += g

    ch = lambda d, pr: _chunk_of(d, nc - 1 - pr, nctx, nc)
    in_specs = [pl.BlockSpec((CHUNK, a.shape[1]), lambda d, pr: (ch(d, pr), 0)) for a in toks]
    in_specs += [pl.BlockSpec(a.shape, lambda d, pr: (0, 0)) for a in pars]
    in_specs += [pl.BlockSpec((1, 1, n_state, 128, 128), lambda d, pr: (d, nc - 1 - pr, 0, 0, 0)),
                 pl.BlockSpec((CHUNK, dout.shape[1]), lambda d, pr: (ch(d, pr), 0))]
    if n_aux:
        in_specs += [pl.BlockSpec((1, 1, n_aux, CHUNK, CHUNK), lambda d, pr: (d, nc - 1 - pr, 0, 0, 0))]
    in_specs += _const_specs(consts)
    out_specs = [pl.BlockSpec((1, CHUNK, a.shape[1]), lambda d, pr: (d, ch(d, pr), 0)) for a in toks]
    out_specs += [pl.BlockSpec(a.shape, lambda d, pr: (0, 0)) for a in pars]
    out_shape = [jax.ShapeDtypeStruct((2, t, a.shape[1]), F32) for a in toks]
    out_shape += [jax.ShapeDtypeStruct(a.shape, F32) for a in pars]
    return pl.pallas_call(
        body, grid=(2, nc), in_specs=in_specs + rd.in_specs, out_specs=out_specs + rd.out_specs,
        out_shape=out_shape + rd.out_shape, scratch_shapes=[pltpu.VMEM((n_state, 128, 128), F32)] + rd.scratch,
        compiler_params=_cp(("arbitrary", "arbitrary")), name=name)(
            *toks, *pars, ss, dout, *([aux] if n_aux else []), *consts, *rd.arrs)


def _scan2_fwd(name, step2, toks, out_width, n_state, nctx, n_aux, ride=None):
    t = toks[0].shape[0]
    nc = t // CHUNK
    n_tok = len(toks)
    rd = ride if ride is not None else _NoRide

    def body(*refs):
        it = iter(refs)
        tok_refs = [next(it) for _ in range(2 * n_tok)]
        ride_in = [next(it) for _ in range(rd.n)]
        o_refs = [next(it), next(it)]
        ss_ref, aux_ref = next(it), next(it)
        ride_out = [next(it) for _ in range(rd.n)]
        s_scr = next(it)
        sems = list(it)
        p = pl.program_id(0)
        if ride is not None:
            @pl.when(p == 0)
            def _():
                ride.start(ride_in, ride_out, sems)

        @pl.when(p == 0)
        def _():
            s_scr[...] = jnp.zeros(s_scr.shape, F32)

        for d in range(2):
            ss_ref[d, 0] = s_scr[d * n_state:(d + 1) * n_state]
        res = step2(*[r[...] for r in tok_refs], *[s_scr[u] for u in range(2 * n_state)], want_aux=True)
        for d in range(2):
            o_refs[d][...] = res[d]
            for i in range(n_aux):
                aux_ref[d, 0, i] = res[2 + 2 * n_state + d * n_aux + i]
        for u in range(2 * n_state):
            s_scr[u] = res[2 + u]
        if ride is not None:
            @pl.when(p == nc - 1)
            def _():
                ride.wait(ride_in, ride_out, sems)

    def tok_spec(a, d):
        return pl.BlockSpec((CHUNK, a.shape[1]), lambda p: (_chunk_of(d, p, nctx, nc), 0))

    return pl.pallas_call(
        body, grid=(nc,), in_specs=[tok_spec(a, d) for d in range(2) for a in toks] + rd.in_specs,
        out_specs=[pl.BlockSpec((CHUNK, out_width), lambda p: (_chunk_of(0, p, nctx, nc), 0)),
                   pl.BlockSpec((CHUNK, out_width), lambda p: (_chunk_of(1, p, nctx, nc), 0)),
                   pl.BlockSpec((2, 1, n_state, 128, 128), lambda p: (0, p, 0, 0, 0)),
                   pl.BlockSpec((2, 1, n_aux, CHUNK, CHUNK), lambda p: (0, p, 0, 0, 0))] + rd.out_specs,
        out_shape=[jax.ShapeDtypeStruct((t, out_width), F32), jax.ShapeDtypeStruct((t, out_width), F32),
                   jax.ShapeDtypeStruct((2, nc, n_state, 128, 128), F32),
                   jax.ShapeDtypeStruct((2, nc, n_aux, CHUNK, CHUNK), F32)] + rd.out_shape,
        scratch_shapes=[pltpu.VMEM((2 * n_state, 128, 128), F32)] + rd.scratch,
        compiler_params=_cp(("arbitrary",)), name=name)(*toks, *toks, *rd.arrs)


ADA_TN = 512


def _ada_fwd(cc, ada_w4, ada_b):
    per = ada_w4.shape[2] // ADA_TN
    n = N_PLANE * ada_w4.shape[2]

    def body(c_ref, w_ref, b_ref, o_ref):
        s = _silu(c_ref[...]).astype(_MXU_DTYPE)
        o_ref[...] = jnp.dot(s, w_ref[0].astype(_MXU_DTYPE), preferred_element_type=F32) + b_ref[...]

    return pl.pallas_call(
        body, grid=(n // ADA_TN,),
        in_specs=[pl.BlockSpec((8, D_MODEL), lambda j: (0, 0)),
                  pl.BlockSpec((1, D_MODEL, ADA_TN), lambda j: (j // per, 0, j % per)),
                  pl.BlockSpec((1, ADA_TN), lambda j: (0, j))],
        out_specs=pl.BlockSpec((8, ADA_TN), lambda j: (0, j)), out_shape=jax.ShapeDtypeStruct((8, n), F32),
        compiler_params=_cp(("parallel",)), name="ada_fwd")(cc, ada_w4, ada_b)


def _ada_bwd(cc, ada_w4, dmods):
    per = ada_w4.shape[2] // ADA_TN
    n = N_PLANE * ada_w4.shape[2]
    nj = n // ADA_TN

    def body(c_ref, w_ref, g_ref, dw_ref, db_ref, dc_ref):
        j = pl.program_id(0)
        g = g_ref[...]
        row = lax.broadcasted_iota(jnp.int32, g.shape, 0)
        g = jnp.where(row < 2, g, 0.0)
        s, pull = jax.vjp(_silu, c_ref[...])
        dw_ref[0] = lax.dot_general(s.astype(_MXU_DTYPE), g.astype(_MXU_DTYPE), _TN,
                                    preferred_element_type=F32).astype(dw_ref.dtype)
        db_ref[...] = jnp.sum(g, axis=0, keepdims=True)
        ds = lax.dot_general(g.astype(_MXU_DTYPE), w_ref[0].astype(_MXU_DTYPE), _NT, preferred_element_type=F32)

        @pl.when(j == 0)
        def _():
            dc_ref[...] = ds

        @pl.when(j > 0)
        def _():
            dc_ref[...] += ds

        @pl.when(j == nj - 1)
        def _():
            dc_ref[...] = pull(dc_ref[...])[0]

    wspec = pl.BlockSpec((1, D_MODEL, ADA_TN), lambda j: (j // per, 0, j % per))
    return pl.pallas_call(
        body, grid=(nj,),
        in_specs=[pl.BlockSpec((8, D_MODEL), lambda j: (0, 0)), wspec, pl.BlockSpec((8, ADA_TN), lambda j: (0, j))],
        out_specs=[wspec, pl.BlockSpec((1, ADA_TN), lambda j: (0, j)), pl.BlockSpec((8, D_MODEL), lambda j: (0, 0))],
        out_shape=[jax.ShapeDtypeStruct(ada_w4.shape, _GRAD_DTYPE), jax.ShapeDtypeStruct((1, n), F32),
                   jax.ShapeDtypeStruct((8, D_MODEL), F32)],
        compiler_params=_cp(("arbitrary",)), name="ada_bwd")(cc, ada_w4, dmods)


def _ada_part(cc16, w_shard):
    n = w_shard.shape[1]

    def body(c_ref, w_ref, o_ref):
        s = _silu(c_ref[...]).astype(_MXU_DTYPE)
        o_ref[...] = jnp.dot(s, w_ref[...].astype(_MXU_DTYPE), preferred_element_type=F32)

    return pl.pallas_call(
        body, grid=(n // ADA_TN,),
        in_specs=[pl.BlockSpec((16, D_MODEL), lambda j: (0, 0)), pl.BlockSpec((D_MODEL, ADA_TN), lambda j: (0, j))],
        out_specs=pl.BlockSpec((16, ADA_TN), lambda j: (0, j)), out_shape=jax.ShapeDtypeStruct((16, n), F32),
        compiler_params=_cp(("parallel",)), name="ada_part")(cc16, w_shard)


def _ada_bwd_shard(cc16, w_shard, d_lat, d_ctx):
    n = w_shard.shape[1]
    nj = n // ADA_TN

    def body(c_ref, w_ref, gl_ref, gc_ref, dw_ref, dc_ref):
        j = pl.program_id(0)
        s, pull = jax.vjp(_silu, c_ref[...])
        row = lax.broadcasted_iota(jnp.int32, (8, 1), 0)
        dctx = jnp.where(row == 0, jnp.sum(gc_ref[...], axis=0, keepdims=True), 0.0)
        mxu = lambda v: v.astype(_MXU_DTYPE)
        dw_ref[...] = (lax.dot_general(mxu(s[0:8]), mxu(gl_ref[...]), _TN, preferred_element_type=F32)
                       + lax.dot_general(mxu(s[8:16]), mxu(dctx), _TN, preferred_element_type=F32))
        ds = lax.dot_general(mxu(dctx), mxu(w_ref[...]), _NT, preferred_element_type=F32)

        @pl.when(j == 0)
        def _():
            dc_ref[...] = ds

        @pl.when(j > 0)
        def _():
            dc_ref[...] += ds

        @pl.when(j == nj - 1)
        def _():
            ct = jnp.concatenate([jnp.zeros((8, D_MODEL), F32), dc_ref[...]], axis=0)
            dc_ref[...] = 0.5 * pull(ct)[0][8:16]

    tile = pl.BlockSpec((8, ADA_TN), lambda j: (0, j))
    wspec = pl.BlockSpec((D_MODEL, ADA_TN), lambda j: (0, j))
    return pl.pallas_call(
        body, grid=(nj,), in_specs=[pl.BlockSpec((16, D_MODEL), lambda j: (0, 0)), wspec, tile, tile],
        out_specs=[wspec, pl.BlockSpec((8, D_MODEL), lambda j: (0, 0))],
        out_shape=[jax.ShapeDtypeStruct((D_MODEL, n), F32), jax.ShapeDtypeStruct((8, D_MODEL), F32)],
        compiler_params=_cp(("arbitrary",)), name="ada_bwd_shard")(cc16, w_shard, d_lat, d_ctx)


def _rowsum2(a, b):
    def body(a_ref, b_ref, o_ref):
        o_ref[...] = jnp.sum(a_ref[...], axis=0, keepdims=True) + jnp.sum(b_ref[...], axis=0, keepdims=True)

    return pl.pallas_call(body, out_shape=jax.ShapeDtypeStruct((1, a.shape[1]), F32), name="ada_b_grad")(a, b)


def _tail(h1, ff, mods, wf, tgt, nctx_t, tl):
    t = h1.shape[0]
    nt = t // tl

    def loss_fn(valid, h1v, ffv, g8, w, tg):
        h2 = h1v + g8[0:1] * ffv
        y = _rms(h2) * w
        err = (y - tg) ** 2
        return 0.5 * jnp.sum(jnp.mean(err, axis=-1, keepdims=True), axis=0, keepdims=True) * valid

    def body(h1_ref, ff_ref, g_ref, w_ref, t_ref, loss_ref, dh_ref, dff_ref, dg_ref, dw_ref):
        i = pl.program_id(0)
        valid = jnp.where(i < nctx_t, 0.0, 1.0)
        lv, pull = jax.vjp(functools.partial(loss_fn, valid), h1_ref[...], ff_ref[...].astype(F32), g_ref[...],
                           w_ref[...], t_ref[...])
        dh, dff, dg, dw, _ = pull(jnp.ones((1, 1), F32))
        dh_ref[...] = dh
        dff_ref[...] = dff.astype(dff_ref.dtype)
        lb = jnp.broadcast_to(lv, loss_ref.shape)

        @pl.when(i == 0)
        def _():
            loss_ref[...] = lb
            dg_ref[...] = dg
            dw_ref[...] = dw

        @pl.when(i > 0)
        def _():
            loss_ref[...] += lb
            dg_ref[...] += dg
            dw_ref[...] += dw

    tok = pl.BlockSpec((tl, D_MODEL), lambda i: (i, 0))
    return pl.pallas_call(
        body, grid=(nt,),
        in_specs=[tok, tok, pl.BlockSpec((8, D_MODEL), lambda i: (0, 5)), pl.BlockSpec((1, D_MODEL), lambda i: (0, 0)),
                  pl.BlockSpec((tl, D_MODEL), lambda i: (jnp.maximum(i - nctx_t, 0), 0))],
        out_specs=[pl.BlockSpec((8, 128), lambda i: (0, 0)), tok, tok, pl.BlockSpec((8, D_MODEL), lambda i: (0, 0)),
                   pl.BlockSpec((1, D_MODEL), lambda i: (0, 0))],
        out_shape=[jax.ShapeDtypeStruct((8, 128), F32), jax.ShapeDtypeStruct((t, D_MODEL), F32),
                   jax.ShapeDtypeStruct((t, D_MODEL), _MXU_DTYPE), jax.ShapeDtypeStruct((8, D_MODEL), F32),
                   jax.ShapeDtypeStruct((1, D_MODEL), F32)],
        compiler_params=_cp(("arbitrary",)), name="tail_loss")(h1, ff, mods, wf, tgt)


def _pack_w_in(w4):
    ns = w4.shape[2]
    placed = []
    for s0, s1, p0 in IN_SEGMENTS:
        for j in range(N_PLANE):
            lo, hi = max(s0, j * ns), min(s1, (j + 1) * ns)
            if lo < hi:
                placed.append((p0 + lo - s0, w4[j][:, lo - j * ns:hi - j * ns]))
    placed.sort(key=lambda e: e[0])
    pieces, end = [], 0
    for pos, piece in placed:
        assert pos == end, (pos, end)
        pieces.append(piece)
        end = pos + piece.shape[1]
    pieces.append(jnp.zeros((w4.shape[1], P_TOTAL - end), w4.dtype))
    return jnp.concatenate(pieces, axis=1)


def _unpack_w_in(g):
    ns = D_IN_PROJ // N_PLANE
    shards = []
    for j in range(N_PLANE):
        pieces = []
        for s0, s1, p0 in IN_SEGMENTS:
            lo, hi = max(s0, j * ns), min(s1, (j + 1) * ns)
            if lo < hi:
                pieces.append(g[:, p0 + lo - s0:p0 + hi - s0])
        shards.append(jnp.concatenate(pieces, axis=1))
    return jnp.stack(shards)


LATE_WEIGHTS = ['w_br_gdn', 'w_br_ssm', 'w_out', 'w_ffn_in', 'w_ffn_out']
COL_STACKED = ('ada_w', 'w_in', 'w_ffn_in')


def _from_shards(n, g):
    if n in COL_STACKED:
        return g
    if SHARD_AXIS[n] == 0:
        return g.reshape(N_PLANE * g.shape[1], g.shape[2])
    return jnp.concatenate([g[j] for j in range(N_PLANE)], axis=1)


def _to_dest_blocks(n, g):
    if g.ndim == 3:
        return g
    if SHARD_AXIS[n] == 0:
        return g.reshape(N_PLANE, g.shape[0] // N_PLANE, g.shape[1])
    sz = g.shape[1] // N_PLANE
    return jnp.stack([g[:, j * sz:(j + 1) * sz] for j in range(N_PLANE)])


def _local_step(x, c, ctx, tgt, W, late_shards=None):
    distributed = late_shards is not None
    lc, l = ctx.shape[0], x.shape[0]
    t = lc + l
    tl = 256
    assert lc == tl and l % tl == 0 and lc % CHUNK == 0
    nt, nctx_t, nctx = t // tl, lc // tl, lc // CHUNK
    act = _MXU_DTYPE
    r1 = lambda v: v.reshape(1, -1)

    xt = jnp.concatenate([ctx, x], axis=0)
    if distributed:
        lin = 4 * lax.axis_index("x") + 2 * lax.axis_index("y") + lax.axis_index("c")
        c_all = _gather_all("gather_c", jnp.broadcast_to(c, (8, D_MODEL)))[:, 0]
        cc16 = jnp.concatenate([c_all, r1(W['c_ctx']), jnp.zeros((7, D_MODEL), F32)], axis=0)
        (parts,) = _plane_exchange("gather_mods", [_ada_part(cc16, W['ada_w'])], gather=True)
        mods_all = jnp.transpose(parts, (1, 0, 2)).reshape(16, -1) + r1(W['ada_b'])
        mods = jnp.concatenate([lax.dynamic_slice_in_dim(mods_all, lin, 1, axis=0), mods_all[8:9],
                                jnp.zeros((6, mods_all.shape[1]), F32)], axis=0)
    else:
        cc = jnp.concatenate([c, r1(W['c_ctx']), jnp.zeros((6, D_MODEL), F32)], axis=0)
        mods = _ada_fwd(cc, W['ada_w'], r1(W['ada_b']))

    def mod(kk):
        return Op(mods, (8, D_MODEL), lambda j, i, kk=kk: (0, kk))

    def tokop(arr, w=D_MODEL, off=0, tl_=tl):
        if not isinstance(arr, list) and arr.ndim == 3:
            return Op(arr, (2, tl_, w), lambda j, i: (0, i, off + j), load=_sum_dirs)
        return Op(arr, (tl_, w), lambda j, i: (i, off + j))

    def outop(n, dtype, w=D_MODEL, tl_=tl):
        return Op(((t, n), dtype), (tl_, w), lambda j, i: (i, j))

    def parop(arr, w, off=0):
        return Op(arr, (arr.shape[0], w), lambda j, i: (0, off + j))

    def parout(rows, n, w):
        return Op(((rows, n), F32), (rows, w), lambda j, i: (0, j))

    n1w = r1(W['norm1_w'])
    pre_fn = functools.partial(_prenorm_fn, nctx_t)
    pre_tok, pre_par = [tokop(xt)], [parop(n1w, D_MODEL), mod(1), mod(0)]
    (a,) = _tw_fwd("prenorm_fwd", pre_fn, (1, nt), pre_tok + pre_par, [outop(D_MODEL, act)])
    wp = _pack_w_in(W['w_in'])
    proj = _matmul(a, wp, 'nn', "in_proj")

    gcw, gcb = W['gdn_conv_w'], r1(W['gdn_conv_b'])
    scw, scb = W['ssm_conv_w'], r1(W['ssm_conv_b'])
    conv_parts = {}

    def conv_part(name, mode, poff, cw, cb, woff, width):
        fn = functools.partial(_conv_fn, nctx_t, mode)
        bw = min(width, 1024)
        tok_ = [tokop(proj, bw, poff // bw)]
        par_ = [parop(cw, bw, woff // bw), parop(cb, bw, woff // bw)]
        conv_parts[name] = (fn, tok_, par_, width, bw, poff)
        (o,) = _tw_fwd("conv_" + name, fn, (width // bw, nt), tok_ + par_, [outop(width, F32, bw)])
        return o

    q = conv_part('q', 'q', P_QKV, gcw, gcb, 0, 1024)
    k = conv_part('k', 'k', P_QKV + 1024, gcw, gcb, 1024, 1024)
    v = conv_part('v', 'none', P_QKV + 2048, gcw, gcb, 2048, 1024)
    xs = conv_part('xs', 'none', P_XBC, scw, scb, 0, 2048)
    bm = conv_part('bm', 'none', P_XBC + 2048, scw, scb, 2048, 512)
    cm = conv_part('cm', 'none', P_XBC + 2560, scw, scb, 2560, 512)

    z16, z32 = jnp.zeros((16,), F32), jnp.zeros((32,), F32)
    p0 = jnp.concatenate([W['gdn_a_log'].reshape(-1), jnp.zeros((112,), F32)]).reshape(1, 128)
    p1 = jnp.concatenate([W['gdn_dt_bias'].reshape(-1), z16, W['ssm_dt_bias'].reshape(-1), z32]).reshape(1, 128)
    alog = jnp.concatenate([z32, W['ssm_a_log'].reshape(-1), z32]).reshape(1, 128)
    act_tok, act_par = [tokop(proj, 128, P_SMALL // 128)], [parop(p0, 128), parop(p1, 128)]
    (sp,) = _tw_fwd("small_act", _act_fn, (1, nt), act_tok + act_par, [outop(128, F32, 128)])

    gather_late = PlaneExchange([late_shards[n] for n in LATE_WEIGHTS], gather=True) if distributed else None
    o_f, o_b, ss, tri, *late = _scan2_fwd("gdn_scan_fwd", _gdn_step2, [q, k, v, sp], 1024, GDN_HEADS, nctx,
                                          GDN_HEADS, ride=gather_late)
    if distributed:
        W = dict(W, **{n: _from_shards(n, g) for n, g in zip(LATE_WEIGHTS, late)})
    ssd_consts = _ssd_consts()
    y2, hs = _scan_fwd("ssd_scan_fwd", _ssd_step, [xs, bm, cm, sp], [alog], ssd_consts, 2048, SSM_PAIRS, nctx)

    tlm = 128
    ntm = t // tlm
    gnw = r1(W['gdn_norm_w'])
    dl = jnp.repeat(W['ssm_d'], 64).reshape(1, 2048)
    snw = r1(W['ssm_norm_w'])
    mg_tok = [tokop([o_f, o_b], 1024, 0, tlm), tokop(proj, 1024, P_ZG // 1024, tlm)]
    mg_par = [parop(gnw, 128)]
    (og,) = _tw_fwd("mix_gdn", _mixg_fn, (1, ntm), mg_tok + mg_par, [outop(1024, act, 1024, tlm)])
    ms_tok = [tokop(y2, 2048, 0, tlm), tokop(xs, 2048, 0, tlm), tokop(proj, 2048, P_ZS // 2048, tlm)]
    ms_par = [parop(dl, 2048), parop(snw, 2048)]
    (yg,) = _tw_fwd("mix_ssm", _mixs_fn, (1, ntm), ms_tok + ms_par, [outop(2048, act, 2048, tlm)])

    pg = _matmul(og, W['w_br_gdn'], 'nn', "br_gdn", out_dtype=act)
    ps = _matmul(yg, W['w_br_ssm'], 'nn', "br_ssm", out_dtype=act)
    mr_tok = [tokop(proj, 2048, P_GATE // 2048), tokop(pg), tokop(ps)]
    (mrg,) = _tw_fwd("merge", _merge_fn, (1, nt), mr_tok, [outop(1024, act)])
    mo = _matmul(mrg, W['w_out'], 'nn', "w_out")

    n2w = r1(W['norm2_w'])
    n2_tok, n2_par = [tokop(xt), tokop(mo)], [mod(2), parop(n2w, D_MODEL), mod(4), mod(3)]
    h1, f = _tw_fwd("norm2_fwd", _norm2_fn, (1, nt), n2_tok + n2_par, [outop(1024, F32), outop(1024, act)])
    u = _matmul(f, W['w_ffn_in'], 'nn', "ffn_in", out_dtype=act)
    swb = D_FF // 2
    sw_tok = [tokop(u, swb, 0), tokop(u, swb, D_FF // swb)]
    (sw,) = _tw_fwd("swiglu", _swiglu_fn, (D_FF // swb, nt), sw_tok, [outop(D_FF, act, swb)])
    ff = _matmul(sw, W['w_ffn_out'], 'nn', "ffn_out")

    loss8, dh1, dff, dg2, dnf = _tail(h1, ff, mods, r1(W['norm_f_w']), tgt, nctx_t, tl)
    loss = loss8[0, 0]

    G = {}
    G['norm_f_w'] = dnf.reshape(-1)
    G['w_ffn_out'] = _matmul(sw, dff, 'tn', "d_ffn_out", out_dtype=_GRAD_DTYPE)
    dsw = _matmul(dff, W['w_ffn_out'], 'nt', "d_sw", out_dtype=act)
    dug, duu = _tw_bwd("swiglu_bwd", _swiglu_fn, (D_FF // swb, nt), sw_tok, [], [[tokop(dsw, swb)]],
                       [outop(D_FF, act, swb), outop(D_FF, act, swb)], [])
    du = jnp.concatenate([dug, duu], axis=1)
    G['w_ffn_in'] = _matmul(f, du, 'tn', "d_ffn_in", out_dtype=_GRAD_DTYPE, stacked_out=True)
    df = _matmul(du, W['w_ffn_in'], 'nt', "d_f")
    dxt1, dmo, dg1, dn2, dsc2, dsh2 = _tw_bwd(
        "norm2_bwd", _norm2_fn, (1, nt), n2_tok, n2_par, [[tokop(dh1)], [tokop(df)]],
        [outop(1024, F32), outop(1024, act)],
        [parout(8, 1024, 1024), parout(1, 1024, 1024), parout(8, 1024, 1024), parout(8, 1024, 1024)])
    G['norm2_w'] = dn2.reshape(-1)
    G['w_out'] = _matmul(mrg, dmo, 'tn', "d_w_out", out_dtype=_GRAD_DTYPE)
    dmrg = _matmul(dmo, W['w_out'], 'nt', "d_mrg", out_dtype=act)
    def win(off, w, tl_=tl):
        return Op(((t, P_TOTAL), act), (tl_, w), lambda j, i: (i, off // w + j))

    dproj = lax.empty((t, P_TOTAL), act)
    dproj, dpg, dps = _tw_bwd("merge_bwd", _merge_fn, (1, nt), mr_tok, [], [[tokop(dmrg)]],
                              [win(P_GATE, 2048), outop(1024, act), outop(1024, act)], [], into={0: dproj})
    G['w_br_gdn'] = _matmul(og, dpg, 'tn', "d_br_gdn", out_dtype=_GRAD_DTYPE)
    G['w_br_ssm'] = _matmul(yg, dps, 'tn', "d_br_ssm", out_dtype=_GRAD_DTYPE)
    dog = _matmul(dpg, W['w_br_gdn'], 'nt', "d_og")
    dyg = _matmul(dps, W['w_br_ssm'], 'nt', "d_yg")

    do, dproj, dgnw = _tw_bwd("mix_gdn_bwd", _mixg_fn, (1, ntm), mg_tok, mg_par, [[tokop(dog, 1024, 0, tlm)]],
                              [outop(1024, F32, 1024, tlm), win(P_ZG, 1024, tlm)], [parout(1, 128, 128)],
                              into={1: dproj})
    G['gdn_norm_w'] = dgnw.reshape(-1)
    dy, dxs_a, dproj, ddl, dsnw = _tw_bwd(
        "mix_ssm_bwd", _mixs_fn, (1, ntm), ms_tok, ms_par, [[tokop(dyg, 2048, 0, tlm)]],
        [outop(2048, F32, 2048, tlm), outop(2048, F32, 2048, tlm), win(P_ZS, 2048, tlm)],
        [parout(1, 2048, 2048), parout(1, 2048, 2048)], into={2: dproj})
    G['ssm_d'] = ddl.reshape(SSM_HEADS, 64).sum(axis=1)
    G['ssm_norm_w'] = dsnw.reshape(-1)

    dq2, dk2, dv2, dsp_g = _scan_bwd("gdn_scan_bwd", _gdn_step, [q, k, v, sp], [], ss, do, GDN_HEADS, nctx, aux=tri)
    received = {}
    scatter_late = (PlaneExchange([_to_dest_blocks(n, G.pop(n)) for n in LATE_WEIGHTS], gather=False)
                    if distributed else None)
    dxs2, dbm2, dcm2, dsp_s, dalog, *got = _scan_bwd("ssd_scan_bwd", _ssd_step, [xs, bm, cm, sp], [alog], hs, dy,
                                                     SSM_PAIRS, nctx, ride=scatter_late, consts=ssd_consts)
    received.update(zip(LATE_WEIGHTS, got))
    G['ssm_a_log'] = dalog[0, 32:96].reshape(2, SSM_HEADS)

    dconv_w, dconv_b = {}, {}

    def conv_bwd(name, cots_, buf):
        fn, tok_, par_, width, bw, poff = conv_parts[name]
        buf, dconv_w[name], dconv_b[name] = _tw_bwd(
            "conv_" + name + "_bwd", fn, (width // bw, nt), tok_, par_, [[tokop(c_, bw) for c_ in cots_]],
            [win(poff, bw)], [parout(3, width, bw), parout(1, width, bw)], into={0: buf})
        return buf

    dproj = conv_bwd('q', [dq2], dproj)
    dproj = conv_bwd('k', [dk2], dproj)
    dproj = conv_bwd('v', [dv2], dproj)
    dproj = conv_bwd('xs', [dxs2, dxs_a], dproj)
    dproj = conv_bwd('bm', [dbm2], dproj)
    dproj = conv_bwd('cm', [dcm2], dproj)
    G['gdn_conv_w'] = jnp.concatenate([dconv_w['q'], dconv_w['k'], dconv_w['v']], axis=1)
    G['gdn_conv_b'] = jnp.concatenate([dconv_b['q'], dconv_b['k'], dconv_b['v']], axis=1).reshape(-1)
    G['ssm_conv_w'] = jnp.concatenate([dconv_w['xs'], dconv_w['bm'], dconv_w['cm']], axis=1)
    G['ssm_conv_b'] = jnp.concatenate([dconv_b['xs'], dconv_b['bm'], dconv_b['cm']], axis=1).reshape(-1)

    dproj, dp0, dp1 = _tw_bwd("small_act_bwd", _act_fn, (1, nt), act_tok, act_par,
                              [[tokop(dsp_g, 128), tokop(dsp_s, 128)]], [win(P_SMALL, P_TOTAL - P_SMALL)],
                              [parout(1, 128, 128), parout(1, 128, 128)], into={0: dproj})
    G['gdn_a_log'] = dp0[0, 0:16].reshape(2, GDN_HEADS)
    G['gdn_dt_bias'] = dp1[0, 0:16].reshape(2, GDN_HEADS)
    G['ssm_dt_bias'] = dp1[0, 32:96].reshape(2, SSM_HEADS)

    G['w_in'] = _unpack_w_in(_matmul(a, dproj, 'tn', "d_w_in", out_dtype=_GRAD_DTYPE))
    if distributed:
        da, (received['w_in'],) = _matmul(dproj, wp, 'nt', "d_a", ride=PlaneExchange([G.pop('w_in')], gather=False))
    else:
        da = _matmul(dproj, wp, 'nt', "d_a")

    gx_out = Op(((l, D_MODEL), F32), (tl, D_MODEL), lambda j, i: (jnp.maximum(i - nctx_t, 0), 0))
    grad_x, dn1, dsc1, dsh1 = _tw_bwd(
        "prenorm_bwd", pre_fn, (1, nt), pre_tok, pre_par, [[tokop(da)]], [gx_out],
        [parout(1, 1024, 1024), parout(8, 1024, 1024), parout(8, 1024, 1024)], tok_add=tokop(dxt1),
        sem=("arbitrary", "arbitrary"))
    G['norm1_w'] = dn1.reshape(-1)
    dmods = jnp.concatenate([dsh1, dsc1, dg1, dsh2, dsc2, dg2], axis=1)
    if distributed:
        dm_all = _gather_all("gather_dmods", dmods)
        ns = W['ada_w'].shape[1]
        mine = lax.dynamic_slice_in_dim(dm_all, (2 * lax.axis_index("x") + lax.axis_index("y")) * ns, ns, axis=2)
        G['ada_w'], dcc = _ada_bwd_shard(cc16, W['ada_w'], mine[:, 0], mine[:, 1])
        G['ada_b'] = jnp.where(lin == 0, _rowsum2(dm_all[:, 0], dm_all[:, 1]).reshape(-1), 0.0)
        G['c_ctx'] = dcc[0]
    else:
        G['ada_w'], dab, dcc = _ada_bwd(cc, W['ada_w'], dmods)
        G['ada_b'] = dab.reshape(-1)
        G['c_ctx'] = dcc[1]
    return loss, grad_x, G, received


def _row_tile(r, c):
    for tr in (512, 256, 128, 64, 32, 16, 8):
        if r % tr == 0 and tr * c * 4 <= (1 << 20):
            return tr
    return r


def _sum4(name, rv):
    _, r, c = rv.shape
    tr = _row_tile(r, c)

    def body(r_ref, o_ref):
        o_ref[...] = ((r_ref[0].astype(F32) + r_ref[1].astype(F32)) + r_ref[2].astype(F32)) + r_ref[3].astype(F32)

    return pl.pallas_call(
        body, grid=(r // tr,), in_specs=[pl.BlockSpec((N_PLANE, tr, c), lambda i: (0, i, 0))],
        out_specs=pl.BlockSpec((tr, c), lambda i: (i, 0)), out_shape=jax.ShapeDtypeStruct((r, c), F32),
        compiler_params=_cp(("parallel",)), name=name)(rv)


def _adamw(name, w, m, v, p, q=None):
    r, c = w.shape
    tr = _row_tile(r, c)
    grads = [p] if q is None else [p, q]

    def body(w_ref, m_ref, v_ref, *refs):
        g_ref, d_ref, mo_ref, vo_ref = refs[len(grads):]
        g = refs[0][...] if q is None else refs[0][...] + refs[1][...]
        mn = ADAM_B1 * m_ref[...] + (1.0 - ADAM_B1) * g
        vn = ADAM_B2 * v_ref[...] + (1.0 - ADAM_B2) * jnp.square(g)
        m_hat = mn / (1.0 - ADAM_B1 ** ADAM_STEP)
        v_hat = vn / (1.0 - ADAM_B2 ** ADAM_STEP)
        g_ref[...] = g
        d_ref[...] = -ADAM_LR * (m_hat / (jnp.sqrt(v_hat) + ADAM_EPS) + ADAM_WD * w_ref[...])
        mo_ref[...] = mn
        vo_ref[...] = vn

    spec = pl.BlockSpec((tr, c), lambda i: (i, 0))
    return pl.pallas_call(
        body, grid=(r // tr,), in_specs=[spec] * (3 + len(grads)), out_specs=[spec] * 4,
        out_shape=[jax.ShapeDtypeStruct((r, c), F32)] * 4, compiler_params=_cp(("parallel",)), name=name)(w, m, v, *grads)


SMALL_ROWS = 24


def _pack_small(d):
    v = jnp.concatenate([d[n].reshape(-1) for n in SMALL])
    return jnp.pad(v, (0, SMALL_ROWS * 1024 - v.shape[0])).reshape(SMALL_ROWS, 1024)


def _unpack_small(buf, like):
    v = buf.reshape(-1)
    out, off = {}, 0
    for n in SMALL:
        sz = like[n].size
        out[n] = v[off:off + sz].reshape(like[n].shape)
        off += sz
    return out


def kernel(x, c, ctx, c_ctx, ada_w, ada_b, norm1_w, w_in, gdn_conv_w, gdn_conv_b, gdn_a_log, gdn_dt_bias, gdn_norm_w, ssm_conv_w, ssm_conv_b, ssm_a_log, ssm_dt_bias, ssm_d, ssm_norm_w, w_br_gdn, w_br_ssm, w_out, norm2_w, w_ffn_in, w_ffn_out, norm_f_w, loss_target, m_c_ctx, m_ada_w, m_ada_b, m_norm1_w, m_w_in, m_gdn_conv_w, m_gdn_conv_b, m_gdn_a_log, m_gdn_dt_bias, m_gdn_norm_w, m_ssm_conv_w, m_ssm_conv_b, m_ssm_a_log, m_ssm_dt_bias, m_ssm_d, m_ssm_norm_w, m_w_br_gdn, m_w_br_ssm, m_w_out, m_norm2_w, m_w_ffn_in, m_w_ffn_out, m_norm_f_w, v_c_ctx, v_ada_w, v_ada_b, v_norm1_w, v_w_in, v_gdn_conv_w, v_gdn_conv_b, v_gdn_a_log, v_gdn_dt_bias, v_gdn_norm_w, v_ssm_conv_w, v_ssm_conv_b, v_ssm_a_log, v_ssm_dt_bias, v_ssm_d, v_ssm_norm_w, v_w_br_gdn, v_w_br_ssm, v_w_out, v_norm2_w, v_w_ffn_in, v_w_ffn_out, v_norm_f_w):
    args = dict(locals())
    wl = {n: args[n] for n in WEIGHTS}
    ml = {n: args['m_' + n] for n in WEIGHTS}
    vl = {n: args['v_' + n] for n in WEIGHTS}

    def nodepth(n, a):
        return a if n in ('c_ctx', 'norm_f_w') else a[0]

    shard = {n: nodepth(n, wl[n]).astype(_MXU_DTYPE if n in MXU_WEIGHTS else F32) for n in SHARDED}
    first = [n for n in SHARDED if n not in LATE_WEIGHTS and n != 'ada_w']
    W = {n: nodepth(n, wl[n]) for n in SMALL}
    W['ada_w'] = shard['ada_w']
    for n, g in zip(first, _gather_two_level("all_gather_plane", [shard[n] for n in first])):
        W[n] = _from_shards(n, g)

    loss_local, grad_x, G, received = _local_step(x[0], c, ctx[0], loss_target[0], W,
                                                  late_shards={n: shard[n] for n in LATE_WEIGHTS})
    loss = lax.psum(loss_local, ("x", "y", "c"))

    small_g = _pack_small(G)
    last = [n for n in SHARDED if n not in received and n != 'ada_w']
    send = [_to_dest_blocks(n, G[n]) for n in last] + [jnp.broadcast_to(small_g[None], (N_PLANE,) + small_g.shape)]
    received.update(zip(last + ['small'], _plane_exchange("scatter_plane", send, gather=False)))
    names = [n for n in SHARDED if n != 'ada_w'] + ['small']
    plane_sum = [_sum4("sum4_" + n, received[n]) for n in names]
    other = _swap_sibling(plane_sum)

    wd = {n: nodepth(n, wl[n]) for n in WEIGHTS}
    md = {n: nodepth(n, ml[n]) for n in WEIGHTS}
    vd = {n: nodepth(n, vl[n]) for n in WEIGHTS}
    res = {'ada_w': [o.reshape(wl['ada_w'].shape)
                     for o in _adamw("adamw_ada_w", wd['ada_w'], md['ada_w'], vd['ada_w'], G['ada_w'])]}
    for n, p, q in zip(names, plane_sum, other):
        if n == 'small':
            outs = _adamw("adamw_small", _pack_small(wd), _pack_small(md), _pack_small(vd), p, q)
            unpacked = [_unpack_small(o, wd) for o in outs]
            for sn in SMALL:
                res[sn] = [u[sn].reshape(wl[sn].shape) for u in unpacked]
        else:
            outs = _adamw("adamw_" + n, wd[n], md[n], vd[n], p, q)
            res[n] = [o.reshape(wl[n].shape) for o in outs]
    flat = [res[n][kind] for kind in range(4) for n in WEIGHTS]
    return (loss, grad_x[None], *flat)
```

```python
import functools

import jax
import jax.numpy as jnp
from jax import lax
from jax.experimental import pallas as pl
from jax.experimental.pallas import tpu as pltpu

F32 = jnp.float32
HI = lax.Precision.HIGHEST
_MXU_DTYPE = jnp.bfloat16
_SCAN_DTYPE = jnp.bfloat16
_GRAD_DTYPE = jnp.bfloat16

D_MODEL = 1024
EPS = 1e-6
CHUNK = 64
GRID_W = 64
GDN_HEADS = 8
GDN_DK = 128
SSM_HEADS = 32
SSM_PAIRS = 16
SSD_GROUP = 4
D_FF = 2816
D_IN_PROJ = 11360
N_PLANE = 4
P_QKV, P_ZG, P_GATE, P_ZS, P_XBC, P_SMALL, P_TOTAL = 0, 3072, 4096, 6144, 8192, 11264, 11776
IN_SEGMENTS = [(0, 3072, P_QKV), (3072, 4096, P_ZG), (4096, 4112, P_SMALL), (4112, 4128, P_SMALL + 16),
               (4128, 6176, P_ZS), (6176, 9248, P_XBC), (9248, 9312, P_SMALL + 32), (9312, 11360, P_GATE)]
ADAM_LR, ADAM_B1, ADAM_B2, ADAM_EPS, ADAM_WD, ADAM_STEP = 0.001, 0.9, 0.999, 1e-08, 0.01, 10

VMEM_LIMIT = 48 * 1024 * 1024

WEIGHTS = ['c_ctx', 'ada_w', 'ada_b', 'norm1_w', 'w_in', 'gdn_conv_w', 'gdn_conv_b', 'gdn_a_log', 'gdn_dt_bias',
           'gdn_norm_w', 'ssm_conv_w', 'ssm_conv_b', 'ssm_a_log', 'ssm_dt_bias', 'ssm_d', 'ssm_norm_w', 'w_br_gdn',
           'w_br_ssm', 'w_out', 'norm2_w', 'w_ffn_in', 'w_ffn_out', 'norm_f_w']
SHARD_AXIS = {'ada_w': 1, 'w_in': 1, 'gdn_conv_w': 1, 'ssm_conv_w': 1, 'w_br_gdn': 0, 'w_br_ssm': 0, 'w_out': 0,
              'w_ffn_in': 1, 'w_ffn_out': 0}
SHARDED = [n for n in WEIGHTS if n in SHARD_AXIS]
SMALL = [n for n in WEIGHTS if n not in SHARD_AXIS]
MXU_WEIGHTS = ['ada_w', 'w_in', 'w_br_gdn', 'w_br_ssm', 'w_out', 'w_ffn_in', 'w_ffn_out']


def _cp(sem):
    return pltpu.CompilerParams(dimension_semantics=sem, vmem_limit_bytes=VMEM_LIMIT)


def _pick(n, cands):
    for c in cands:
        if n % c == 0:
            return c
    raise ValueError(f"no tile for {n}")


HBM = pl.BlockSpec(memory_space=pl.ANY)
MESH = pl.DeviceIdType.MESH


def _plane_peers():
    x, y, c = lax.axis_index("x"), lax.axis_index("y"), lax.axis_index("c")
    return (x, y, c), [(1 - x, y, c), (x, 1 - y, c), (1 - x, 1 - y, c)]


class PlaneExchange:
    def __init__(self, arrs, gather):
        self.arrs, self.gather, self.n = list(arrs), gather, len(arrs)
        self.in_specs = [HBM] * self.n
        self.out_specs = [HBM] * self.n
        self.out_shape = [jax.ShapeDtypeStruct((N_PLANE,) + a.shape if gather else a.shape, a.dtype) for a in self.arrs]
        self.scratch = [pltpu.SemaphoreType.DMA((3 * self.n,)), pltpu.SemaphoreType.DMA((3 * self.n,)),
                        pltpu.SemaphoreType.DMA((self.n,))]

    def _copies(self, ins, outs, sems):
        send_sems, recv_sems, local_sems = sems
        (x, y, c), peers = _plane_peers()
        me = 2 * x + y
        copies = []
        for ti in range(self.n):
            src = ins[ti] if self.gather else ins[ti].at[me]
            copies.append(pltpu.make_async_copy(src, outs[ti].at[me], local_sems.at[ti]))
            for kk, (px, py, pc) in enumerate(peers):
                src = ins[ti] if self.gather else ins[ti].at[2 * px + py]
                copies.append(pltpu.make_async_remote_copy(
                    src_ref=src, dst_ref=outs[ti].at[me], send_sem=send_sems.at[3 * ti + kk],
                    recv_sem=recv_sems.at[3 * ti + kk], device_id=(px, py, pc), device_id_type=MESH))
        return copies

    def start(self, ins, outs, sems):
        for cp in self._copies(ins, outs, sems):
            cp.start()

    def wait(self, ins, outs, sems):
        for cp in self._copies(ins, outs, sems):
            cp.wait()


def _plane_exchange(name, arrs, gather):
    ex = PlaneExchange(arrs, gather)
    n = ex.n

    def body(*refs):
        ins, outs, sems = refs[:n], refs[n:2 * n], refs[2 * n:]
        ex.start(ins, outs, sems)
        ex.wait(ins, outs, sems)

    return pl.pallas_call(body, in_specs=ex.in_specs, out_specs=ex.out_specs, out_shape=ex.out_shape,
                          scratch_shapes=ex.scratch, name=name)(*arrs)


def _gather_two_level(name, arrs):
    n = len(arrs)
    split = [a.shape[0] % 32 == 0 for a in arrs]

    def body(*refs):
        ins, outs = refs[:n], refs[n:2 * n]
        ici_send, ici_recv, d2d_send, d2d_recv, local_sems = refs[2 * n:]
        (x, y, c), peers = _plane_peers()
        me = 2 * x + y

        def part(ref, ti):
            if not split[ti]:
                return ref
            rows = arrs[ti].shape[0] // 2
            return ref.at[pl.ds(c * rows, rows)]

        local = [pltpu.make_async_copy(ins[ti], outs[ti].at[me], local_sems.at[ti]) for ti in range(n)]
        ici = [pltpu.make_async_remote_copy(
            src_ref=part(ins[ti], ti), dst_ref=part(outs[ti].at[me], ti), send_sem=ici_send.at[3 * ti + kk],
            recv_sem=ici_recv.at[3 * ti + kk], device_id=peer, device_id_type=MESH)
            for ti in range(n) for kk, peer in enumerate(peers)]
        for cp in local + ici:
            cp.start()
        d2d = []
        for ti in range(n):
            for kk, (px, py, pc) in enumerate(peers):
                ici[3 * ti + kk].wait_recv()
                if split[ti]:
                    piece = part(outs[ti].at[2 * px + py], ti)
                    cp = pltpu.make_async_remote_copy(
                        src_ref=piece, dst_ref=piece, send_sem=d2d_send.at[3 * ti + kk],
                        recv_sem=d2d_recv.at[3 * ti + kk], device_id=(x, y, 1 - c), device_id_type=MESH)
                    cp.start()
                    d2d.append(cp)
        for cp in ici:
            cp.wait_send()
        for cp in d2d:
            cp.wait()
        for cp in local:
            cp.wait()

    return pl.pallas_call(
        body, in_specs=[HBM] * n, out_specs=[HBM] * n,
        out_shape=[jax.ShapeDtypeStruct((N_PLANE,) + a.shape, a.dtype) for a in arrs],
        scratch_shapes=[pltpu.SemaphoreType.DMA((3 * n,))] * 4 + [pltpu.SemaphoreType.DMA((n,))], name=name)(*arrs)


def _gather_all(name, arr):
    flips = [(fx, fy, fc) for fx in (0, 1) for fy in (0, 1) for fc in (0, 1)][1:]

    def body(a_ref, o_ref, send_sems, recv_sems, local_sem):
        x, y, c = lax.axis_index("x"), lax.axis_index("y"), lax.axis_index("c")
        me = 4 * x + 2 * y + c
        copies = [pltpu.make_async_copy(a_ref, o_ref.at[me], local_sem)]
        for kk, (fx, fy, fc) in enumerate(flips):
            peer = (1 - x if fx else x, 1 - y if fy else y, 1 - c if fc else c)
            copies.append(pltpu.make_async_remote_copy(
                src_ref=a_ref, dst_ref=o_ref.at[me], send_sem=send_sems.at[kk], recv_sem=recv_sems.at[kk],
                device_id=peer, device_id_type=MESH))
        for cp in copies:
            cp.start()
        for cp in copies:
            cp.wait()

    return pl.pallas_call(
        body, in_specs=[HBM], out_specs=HBM, out_shape=jax.ShapeDtypeStruct((8,) + arr.shape, arr.dtype),
        scratch_shapes=[pltpu.SemaphoreType.DMA((7,)), pltpu.SemaphoreType.DMA((7,)), pltpu.SemaphoreType.DMA],
        name=name)(arr)


def _swap_sibling(arrs):
    n = len(arrs)

    def body(*refs):
        ins, outs, send_sems, recv_sems = refs[:n], refs[n:2 * n], refs[2 * n], refs[2 * n + 1]
        x, y, c = lax.axis_index("x"), lax.axis_index("y"), lax.axis_index("c")
        copies = [pltpu.make_async_remote_copy(src_ref=ins[ti], dst_ref=outs[ti], send_sem=send_sems.at[ti],
                                               recv_sem=recv_sems.at[ti], device_id=(x, y, 1 - c), device_id_type=MESH)
                  for ti in range(n)]
        for cp in copies:
            cp.start()
        for cp in copies:
            cp.wait()

    return pl.pallas_call(
        body, in_specs=[HBM] * n, out_specs=[HBM] * n, out_shape=[jax.ShapeDtypeStruct(a.shape, a.dtype) for a in arrs],
        scratch_shapes=[pltpu.SemaphoreType.DMA((n,)), pltpu.SemaphoreType.DMA((n,))], name="swap_sibling")(*arrs)


MATMUL_VMEM_BUDGET = 36 * 1024 * 1024
TILE_CANDIDATES = (2944, 2816, 1408, 1024, 768, 512, 256, 128)
STEP_COST_BYTES = 1 << 20


def _matmul_tiles(m, n, k, ab, bb, ob, tn_fixed=None, tk_fixed=None):
    def cands(dim, whole_up_to):
        out = [c for c in TILE_CANDIDATES if dim % c == 0]
        if dim <= whole_up_to and dim not in out:
            out.append(dim)
        return out

    best = None
    for tm in cands(m, 3072):
        for tn in ([tn_fixed] if tn_fixed else cands(n, 3072)):
            for tk in ([tk_fixed] if tk_fixed else cands(k, 2048)):
                gi, gj, gk = m // tm, n // tn, k // tk
                vmem = 2 * (tm * tk * ab + tk * tn * bb + tm * tn * ob) + (tm * tn * 4 if gk > 1 else 0)
                if vmem > MATMUL_VMEM_BUDGET:
                    continue
                a_reads = 1 if gk == 1 else gj
                b_reads = 1 if (gk == 1 and gj == 1) else gi
                cost = (m * k * ab * a_reads + k * n * bb * b_reads + m * n * ob + gi * gj * gk * STEP_COST_BYTES)
                if best is None or cost < best[0]:
                    best = (cost, tm, tn, tk)
    assert best is not None, (m, n, k)
    return best[1:]


def _matmul(a, b, form, name, out_dtype=F32, stacked_out=False, ride=None, halves=None):
    dims = {'nn': (((1,), (0,)), ((), ())), 'nt': (((1,), (1,)), ((), ())), 'tn': (((0,), (0,)), ((), ()))}[form]
    stacked_b = b.ndim == 3 and halves != 'b'
    ns = b.shape[2] if stacked_b else None
    if form == 'nn':
        m, k = a.shape
        n = b.shape[0] * ns if stacked_b else b.shape[1]
    elif form == 'nt':
        m, k = (a.shape[1], 2 * a.shape[2]) if halves == 'a' else a.shape
        n = b.shape[1] if stacked_b else b.shape[0]
    else:
        k, m = a.shape
        n = 2 * b.shape[2] if halves == 'b' else b.shape[1]
    ob = jnp.dtype(out_dtype).itemsize
    tm, tn, tk = _matmul_tiles(m, n, k, a.dtype.itemsize, b.dtype.itemsize, ob,
                               tn_fixed=(n // N_PLANE if stacked_out else ns if (stacked_b and form == 'nn') else None),
                               tk_fixed=(ns if (stacked_b and form == 'nt') else None))
    nk = k // tk
    grid = (m // tm, n // tn, nk)
    n_ride = ride.n if ride is not None else 0

    def body(*refs):
        a_ref, b_ref = refs[0], refs[1]
        ride_in = refs[2:2 + n_ride]
        o_ref = refs[2 + n_ride]
        ride_out = refs[3 + n_ride:3 + 2 * n_ride]
        acc_ref = refs[3 + 2 * n_ride]
        sems = refs[4 + 2 * n_ride:]
        i, j, kk = pl.program_id(0), pl.program_id(1), pl.program_id(2)
        if ride is not None:
            @pl.when((i == 0) & (j == 0) & (kk == 0))
            def _():
                ride.start(ride_in, ride_out, sems)

        def put(val):
            if stacked_out or halves == 'out':
                o_ref[0] = val.astype(o_ref.dtype)
            else:
                o_ref[...] = val.astype(o_ref.dtype)

        av = a_ref[0] if halves == 'a' else a_ref[...]
        bv = b_ref[0] if (stacked_b or halves == 'b') else b_ref[...]
        part = lax.dot_general(av.astype(_MXU_DTYPE), bv.astype(_MXU_DTYPE), dims, preferred_element_type=F32)
        if nk == 1:
            put(part)
        else:
            @pl.when(kk == 0)
            def _():
                acc_ref[...] = part

            @pl.when(kk > 0)
            def _():
                acc_ref[...] += part

            @pl.when(kk == nk - 1)
            def _():
                put(acc_ref[...])

        if ride is not None:
            @pl.when((i == grid[0] - 1) & (j == grid[1] - 1) & (kk == nk - 1))
            def _():
                ride.wait(ride_in, ride_out, sems)

    if form == 'nn':
        a_spec = pl.BlockSpec((tm, tk), lambda i, j, kk: (i, kk))
        b_spec = (pl.BlockSpec((1, tk, tn), lambda i, j, kk: (j, kk, 0)) if stacked_b
                  else pl.BlockSpec((tk, tn), lambda i, j, kk: (kk, j)))
    elif form == 'nt':
        a_spec = pl.BlockSpec((tm, tk), lambda i, j, kk: (i, kk))
        b_spec = (pl.BlockSpec((1, tn, tk), lambda i, j, kk: (kk, j, 0)) if stacked_b
                  else pl.BlockSpec((tn, tk), lambda i, j, kk: (j, kk)))
    else:
        a_spec = pl.BlockSpec((tk, tm), lambda i, j, kk: (kk, i))
        b_spec = pl.BlockSpec((tk, tn), lambda i, j, kk: (kk, j))
    hj, hk = grid[1] // 2, nk // 2
    if halves == 'a':
        a_spec = pl.BlockSpec((1, tm, tk), lambda i, j, kk: (kk // hk, i, kk % hk))
    if halves == 'b':
        b_spec = pl.BlockSpec((1, tk, tn), lambda i, j, kk: (j // hj, kk, j % hj))
    if stacked_out:
        o_spec = pl.BlockSpec((1, tm, tn), lambda i, j, kk: (j, i, 0))
        o_shape = jax.ShapeDtypeStruct((N_PLANE, m, tn), out_dtype)
    elif halves == 'out':
        o_spec = pl.BlockSpec((1, tm, tn), lambda i, j, kk: (j // hj, i, j % hj))
        o_shape = jax.ShapeDtypeStruct((2, m, n // 2), out_dtype)
    else:
        o_spec = pl.BlockSpec((tm, tn), lambda i, j, kk: (i, j))
        o_shape = jax.ShapeDtypeStruct((m, n), out_dtype)
    acc_shape = (tm, tn) if nk > 1 else (8, 128)
    if ride is None:
        return pl.pallas_call(
            body, grid=grid, in_specs=[a_spec, b_spec], out_specs=o_spec, out_shape=o_shape,
            scratch_shapes=[pltpu.VMEM(acc_shape, F32)],
            compiler_params=_cp(("parallel", "parallel", "arbitrary")), name=name)(a, b)
    res = pl.pallas_call(
        body, grid=grid, in_specs=[a_spec, b_spec] + ride.in_specs, out_specs=[o_spec] + ride.out_specs,
        out_shape=[o_shape] + ride.out_shape, scratch_shapes=[pltpu.VMEM(acc_shape, F32)] + ride.scratch,
        compiler_params=_cp(("arbitrary", "arbitrary", "arbitrary")), name=name)(a, b, *ride.arrs)
    return res[0], res[1:]


class Op:
    def __init__(self, arr, bs, im, load=None):
        self.arr, self.bs, self.im = arr, bs, im
        self.arrs = list(arr) if isinstance(arr, list) else [arr]
        self.load = load or (lambda r: r[...].astype(F32))

    def spec(self):
        return pl.BlockSpec(self.bs, self.im)

    def value(self, it):
        return functools.reduce(lambda u, w: u + w, [self.load(next(it)) for _ in self.arrs])


def _op_specs(ops):
    return [op.spec() for op in ops for _ in op.arrs]


def _op_arrays(ops):
    return [a for op in ops for a in op.arrs]


def _sum_dirs(r):
    return r[0].astype(F32) + r[1].astype(F32)


def _tw_fwd(name, fn, grid, ins, outs):
    def body(*refs):
        info = (pl.program_id(0), pl.program_id(1))
        it = iter(refs)
        vals = [op.value(it) for op in ins]
        res = fn(info, *vals)
        for r, v in zip(it, res):
            r[...] = v.astype(r.dtype)

    return pl.pallas_call(
        body, grid=grid, in_specs=_op_specs(ins), out_specs=[op.spec() for op in outs],
        out_shape=[jax.ShapeDtypeStruct(*op.arr) for op in outs],
        compiler_params=_cp(("parallel", "arbitrary")), name=name)(*_op_arrays(ins))


def _tw_bwd(name, fn, grid, tok, par, cots, tok_out, par_out, tok_add=None, sem=("parallel", "arbitrary"), into=None):
    n_tok = len(tok)
    flat_cots = [op for group in cots for op in group]
    extra = [tok_add] if tok_add is not None else []
    out_ops = list(tok_out) + list(par_out)
    into = into or {}

    def body(*refs):
        info = (pl.program_id(0), pl.program_id(1))
        it = iter(refs)
        tok_v = [op.value(it) for op in tok]
        par_v = [op.value(it) for op in par]
        cot_v = [functools.reduce(lambda u, w: u + w, [op.value(it) for op in group]) for group in cots]
        add_v = [op.value(it) for op in extra]
        for _ in into:
            next(it)
        _, pull = jax.vjp(lambda *a: fn(info, *a), *tok_v, *par_v)
        grads = pull(tuple(cot_v))
        for i in range(n_tok):
            r = next(it)
            g = grads[i] + add_v[0] if (i == 0 and add_v) else grads[i]
            if r.shape[-1] > g.shape[-1]:
                g = jnp.concatenate([g, jnp.zeros((g.shape[0], r.shape[-1] - g.shape[-1]), g.dtype)], axis=1)
            r[...] = g.astype(r.dtype)
        first = pl.program_id(1) == 0
        for i in range(len(par)):
            r = next(it)
            g = grads[n_tok + i]

            @pl.when(first)
            def _(r=r, g=g):
                r[...] = g

            @pl.when(jnp.logical_not(first))
            def _(r=r, g=g):
                r[...] += g

    ops = tok + par + flat_cots + extra
    n_in = len(_op_arrays(ops))
    return pl.pallas_call(
        body, grid=grid, in_specs=_op_specs(ops) + [HBM] * len(into), out_specs=[op.spec() for op in out_ops],
        out_shape=[jax.ShapeDtypeStruct(*op.arr) for op in out_ops],
        input_output_aliases={n_in + k: i for k, i in enumerate(into)},
        compiler_params=_cp(sem), name=name)(*_op_arrays(ops), *into.values())


def _silu(x):
    return x * jax.nn.sigmoid(x)


def _rms(x):
    return x * lax.rsqrt(jnp.mean(x * x, axis=-1, keepdims=True) + EPS)


@functools.partial(jax.custom_vjp, nondiff_argnums=(1,))
def _shift_rows(x, k):
    return pltpu.roll(x, k % x.shape[0], 0)


def _shift_rows_fwd(x, k):
    return _shift_rows(x, k), None


def _shift_rows_bwd(k, _, g):
    return (_shift_rows(g, -k),)


_shift_rows.defvjp(_shift_rows_fwd, _shift_rows_bwd)


def _prenorm_fn(nctx_t, info, x, w, sc8, sh8):
    is_ctx = info[1] < nctx_t
    sc = jnp.where(is_ctx, sc8[1:2], sc8[0:1])
    sh = jnp.where(is_ctx, sh8[1:2], sh8[0:1])
    return (_rms(x) * w * (1.0 + sc) + sh,)


def _conv_fn(nctx_t, mode, info, x, w, b):
    n, c = x.shape
    is_ctx = info[1] < nctx_t
    idx = lax.broadcasted_iota(jnp.int32, (n, 1), 0)
    rr = jnp.where(is_ctx, idx, idx % GRID_W)
    first = rr == 0
    last = rr == jnp.where(is_ctx, n - 1, GRID_W - 1)
    prev = jnp.where(first, 0.0, _shift_rows(x, 1))
    nxt = jnp.where(last, 0.0, _shift_rows(x, -1))
    y = b + prev * w[0:1] + x * w[1:2] + nxt * w[2:3]
    y = _silu(y)
    if mode == 'none':
        return (y,)
    scale = GDN_DK ** -0.5 if mode == 'q' else 1.0
    outs = []
    for h in range(c // 128):
        yh = y[:, h * 128:(h + 1) * 128]
        outs.append(yh * lax.rsqrt(jnp.sum(yh * yh, axis=-1, keepdims=True) + EPS) * scale)
    return (jnp.concatenate(outs, axis=1),)


def _act_fn(info, x, p0, p1):
    lane = lax.broadcasted_iota(jnp.int32, x.shape, 1)
    sp = jax.nn.softplus(x + p1)
    g = -jnp.exp(p0) * sp
    bt = jax.nn.sigmoid(x)
    return (jnp.where(lane < 16, g, jnp.where(lane < 32, bt, jnp.where(lane < 96, sp, 0.0))),)


def _mixg_fn(info, o, zg, gnw):
    outs = []
    for h in range(GDN_HEADS):
        outs.append(_rms(o[:, h * 128:(h + 1) * 128]) * gnw)
    return (jnp.concatenate(outs, axis=1) * _silu(zg),)


def _mixs_fn(info, y, xs, zs, dl, snw):
    yy = (y + dl * xs) * _silu(zs)
    outs = []
    for g in range(4):
        outs.append(_rms(yy[:, g * 512:(g + 1) * 512]))
    return (jnp.concatenate(outs, axis=1) * snw,)


def _merge_fn(info, gates, pg, ps):
    half = gates.shape[1] // 2
    return (jax.nn.sigmoid(gates[:, :half]) * pg + jax.nn.sigmoid(gates[:, half:]) * ps,)


def _norm2_fn(info, xt, mo, g8, w, sc8, sh8):
    h1 = xt + g8[0:1] * mo
    return (h1, _rms(h1) * w * (1.0 + sc8[0:1]) + sh8[0:1])


def _swiglu_fn(info, u2):
    return (_silu(u2[0]) * u2[1],)


_NN = (((1,), (0,)), ((), ()))
_NT = (((1,), (1,)), ((), ()))
_TN = (((0,), (0,)), ((), ()))


def _mmh(a, b):
    return lax.dot_general(a, b, _NN, precision=HI, preferred_element_type=F32)


def _dot1(a, b, dims):
    return lax.dot_general(a.astype(_SCAN_DTYPE), b.astype(_SCAN_DTYPE), dims, preferred_element_type=F32)


def _mm(a, b):
    return _dot1(a, b, _NN)


def _mm_nt(a, b):
    return _dot1(a, b, _NT)


def _mm_tn(a, b):
    return _dot1(a, b, _TN)


def _split2(a):
    hi = a.astype(_SCAN_DTYPE)
    return hi, (a - hi.astype(F32)).astype(_SCAN_DTYPE)


def _dot3(a, b, dims):
    ah, al = _split2(a)
    bh, bl = _split2(b)
    d = lambda u, w: lax.dot_general(u, w, dims, preferred_element_type=F32)
    return d(ah, bh) + (d(ah, bl) + d(al, bh))


def _order_masks(d):
    i = lax.broadcasted_iota(jnp.int32, (CHUNK, CHUNK), 0)
    j = lax.broadcasted_iota(jnp.int32, (CHUNK, CHUNK), 1)
    s = jnp.where(d == 0, 1, -1) * (i - j)
    return (s >= 0).astype(F32), (s > 0).astype(F32)


@jax.custom_vjp
def _unit_tri_inv(mats):
    i = lax.broadcasted_iota(jnp.int32, (CHUNK, CHUNK), 0)
    j = lax.broadcasted_iota(jnp.int32, (CHUNK, CHUNK), 1)
    eye = (i == j).astype(F32)
    ps = [-a for a in mats]
    ts = [eye + p for p in ps]
    for _ in range(5):
        ps = [_dot3(p, p, _NN) for p in ps]
        ts = [t + _dot3(t, p, _NN) for t, p in zip(ts, ps)]
    return tuple(ts)


def _uti_fwd(mats):
    ts = _unit_tri_inv(mats)
    return ts, ts


def _uti_bwd(ts, gs):
    inner = [_dot3(g, t, _NT) for g, t in zip(gs, ts)]
    return (tuple(-_dot3(t, u, _TN) for t, u in zip(ts, inner)),)


_unit_tri_inv.defvjp(_uti_fwd, _uti_bwd)


@jax.custom_vjp
def _unit_tri_inv_given(mats, ts):
    return ts


def _utig_fwd(mats, ts):
    return ts, ts


def _utig_bwd(ts, gs):
    return _uti_bwd(ts, gs)[0], tuple(jnp.zeros_like(t) for t in ts)


_unit_tri_inv_given.defvjp(_utig_fwd, _utig_bwd)


def _lane_col(blk, lane_idx):
    lane = lax.broadcasted_iota(jnp.int32, blk.shape, 1)
    return jnp.sum(jnp.where(lane == lane_idx, blk, 0.0), axis=1, keepdims=True)


def _decay_mat(cum, incl):
    cb = jnp.broadcast_to(cum, (CHUNK, CHUNK))
    return jnp.exp(jnp.minimum(cb - cb.T, 0.0)) * incl


def _gdn_chunks(dirs, streams, states, aux=None, want_aux=False, group=None):
    ns = len(dirs)
    masks = [_order_masks(d) for d in dirs]
    cum = [_mmh(masks[i][0], streams[i][3]) for i in range(ns)]
    tot = [jnp.sum(streams[i][3], axis=0, keepdims=True) for i in range(ns)]
    all_units = [(i, h) for i in range(ns) for h in range(GDN_HEADS)]
    group = group or len(all_units)
    cat = jnp.concatenate
    outs, new_states, ts_all = [], [], []
    for g0 in range(0, len(all_units), group):
        units = all_units[g0:g0 + group]
        us = range(len(units))
        st = states[g0:g0 + group]
        qs = [streams[i][0][:, h * 128:(h + 1) * 128] for i, h in units]
        ks = [streams[i][1][:, h * 128:(h + 1) * 128] for i, h in units]
        vs = [streams[i][2][:, h * 128:(h + 1) * 128] for i, h in units]
        gcum = [_lane_col(cum[i], dirs[i] * GDN_HEADS + h) for i, h in units]
        glast = [_lane_col(tot[i], dirs[i] * GDN_HEADS + h) for i, h in units]
        beta = [_lane_col(streams[i][3], 16 + dirs[i] * GDN_HEADS + h) for i, h in units]
        decay = [_decay_mat(gcum[u], masks[units[u][0]][0]) for u in us]
        egc = [jnp.exp(gcum[u]) for u in us]
        kb = [ks[u] * beta[u] for u in us]
        kq = [_mm_nt(cat([kb[u], qs[u]], axis=0), ks[u]) for u in us]
        mats = tuple(kq[u][:CHUNK] * decay[u] * masks[units[u][0]][1] for u in us)
        ts = _unit_tri_inv(mats) if aux is None else _unit_tri_inv_given(mats, tuple(aux[g0:g0 + group]))
        wu = [_mm(ts[u], cat([kb[u] * egc[u], vs[u] * beta[u]], axis=1)) for u in us]
        ws = [_mm(cat([wu[u][:, :128], qs[u] * egc[u]], axis=0), st[u]) for u in us]
        vn = [wu[u][:, 128:] - ws[u][:CHUNK] for u in us]
        outs += [ws[u][CHUNK:] + _mm(kq[u][CHUNK:] * decay[u], vn[u]) for u in us]
        new_states += [st[u] * jnp.exp(glast[u]) + _mm_tn(ks[u] * jnp.exp(glast[u] - gcum[u]), vn[u]) for u in us]
        ts_all += list(ts)
    per_stream = [cat(outs[i * GDN_HEADS:(i + 1) * GDN_HEADS], axis=1) for i in range(ns)]
    return (*per_stream, *new_states, *(ts_all if want_aux else ()))


def _gdn_step(d, q, k, v, spb, *states, aux=None, want_aux=False):
    return _gdn_chunks([d], [(q, k, v, spb)], states, aux, want_aux, group=4)


def _gdn_step2(q0, k0, v0, sp0, q1, k1, v1, sp1, *states, aux=None, want_aux=False):
    return _gdn_chunks([0, 1], [(q0, k0, v0, sp0), (q1, k1, v1, sp1)], states, aux, want_aux)


def _split3(a):
    a1 = a.astype(_SCAN_DTYPE)
    r = a - a1.astype(F32)
    a2 = r.astype(_SCAN_DTYPE)
    return a1, a2, (r - a2.astype(F32)).astype(_SCAN_DTYPE)


def _exact_dot(a, e, dims, split_lhs, passes=3):
    parts = _split3(a if split_lhs else e)[:passes]
    d = (lambda u: lax.dot_general(u, e, dims, preferred_element_type=F32)) if split_lhs else \
        (lambda u: lax.dot_general(a, u, dims, preferred_element_type=F32))
    return functools.reduce(lambda u, w: u + w, [d(p) for p in reversed(parts)])


@jax.custom_vjp
def _spread(a, e):
    return _exact_dot(a, e, _NN, True)


def _spread_fwd(a, e):
    return _spread(a, e), e


def _spread_bwd(e, g):
    return _exact_dot(g, e, _NT, True, passes=2), jnp.zeros_like(e)


_spread.defvjp(_spread_fwd, _spread_bwd)


@jax.custom_vjp
def _colsum_bcast(z):
    return _exact_dot(jnp.ones((z.shape[0], z.shape[0]), _SCAN_DTYPE), z, _NN, False)


def _colsum_fwd(z):
    return _colsum_bcast(z), None


def _colsum_bwd(_, g):
    return (_exact_dot(jnp.ones((g.shape[0], g.shape[0]), _SCAN_DTYPE), g, _NN, False, passes=2),)


_colsum_bcast.defvjp(_colsum_fwd, _colsum_bwd)


def _ssd_consts():
    wdt = SSM_HEADS * 64
    d = lax.broadcasted_iota(jnp.int32, (2, 128, wdt), 0)
    e = (lax.broadcasted_iota(jnp.int32, (2, 128, wdt), 1)
         == 32 + d * SSM_HEADS + lax.broadcasted_iota(jnp.int32, (2, 128, wdt), 2) // 64).astype(_SCAN_DTYPE)
    dd = lax.broadcasted_iota(jnp.int32, (2, CHUNK, wdt), 0)
    ci = lax.broadcasted_iota(jnp.int32, (2, CHUNK, wdt), 1)
    pos = lax.broadcasted_iota(jnp.int32, (2, CHUNK, wdt), 2) % 64
    incl_t = (jnp.where(dd == 0, 1, -1) * (ci - pos) >= 0).astype(F32)
    diag = (ci == pos).astype(F32)
    return [e, incl_t, diag]


def _ssd_step(d, x, bm, cm, spb, alog, *states, consts):
    e, incl_t, diag = consts
    incl, _ = _order_masks(d)
    lane1 = lax.broadcasted_iota(jnp.int32, (1, 128), 1)
    lo_lane = 32 + d * SSM_HEADS
    a_vec = jnp.where(lane1 >= lo_lane, jnp.where(lane1 < lo_lane + SSM_HEADS, -jnp.exp(alog), 0.0), 0.0)
    adt = spb * a_vec
    acum = _mmh(incl, adt)
    alast = jnp.sum(adt, axis=0, keepdims=True)
    dt2 = _spread(spb, e)
    ac2 = _spread(acum, e)
    al2 = _spread(jnp.broadcast_to(alast, (8, 128)), e)[0:1]
    row = _colsum_bcast(ac2 * diag)
    seg = jnp.exp(jnp.minimum(ac2 - row, 0.0)) * incl_t
    xdt = x * dt2
    gam = jnp.exp(ac2)
    xe = xdt * jnp.exp(al2 - ac2)
    low = lax.broadcasted_iota(jnp.int32, (CHUNK, 128), 1) < 64
    row_low = lax.broadcasted_iota(jnp.int32, (128, 1), 0) < 64
    ps = range(SSM_PAIRS)
    sl = [slice(p * 128, (p + 1) * 128) for p in ps]
    bg = [bm[:, g * 128:(g + 1) * 128] for g in range(4)]
    cg = [cm[:, g * 128:(g + 1) * 128] for g in range(4)]
    cb2 = [_mm_nt(cg[g], jnp.concatenate([bg[g], bg[g]], axis=0)) for g in range(4)]
    ys, new_states = [], []
    for p0 in range(0, SSM_PAIRS, SSD_GROUP):
        pg = range(p0, p0 + SSD_GROUP)
        xd = {p: jnp.concatenate([jnp.where(low, xdt[:, sl[p]], 0.0), jnp.where(low, 0.0, xdt[:, sl[p]])], axis=0)
              for p in pg}
        yd = {p: _mm(cb2[p // 4] * seg[:, sl[p]], xd[p]) for p in pg}
        yo = {p: _mm_nt(cg[p // 4], states[p]) for p in pg}
        ys += [yd[p] + gam[:, sl[p]] * yo[p] for p in pg]
        new = {p: _mm_tn(xe[:, sl[p]], bg[p // 4]) for p in pg}
        al0 = {p: _lane_col(alast, lo_lane + 2 * p) for p in pg}
        al1 = {p: _lane_col(alast, lo_lane + 2 * p + 1) for p in pg}
        new_states += [states[p] * jnp.exp(jnp.where(row_low, al0[p], al1[p])) + new[p] for p in pg]
    return (jnp.concatenate(ys, axis=1), *new_states)


def _chunk_of(d, p, nctx, nc):
    return jnp.where(d == 0, p, jnp.where(p < nctx, nctx - 1 - p, nctx + nc - 1 - p))


class _NoRide:
    n, arrs, in_specs, out_specs, out_shape, scratch = 0, [], [], [], [], []


def _const_specs(consts):
    return [pl.BlockSpec((1,) + a.shape[1:], lambda d, p: (d,) + (0,) * (a.ndim - 1)) for a in consts]


def _scan_fwd(name, step, toks, pars, consts, out_width, n_state, nctx):
    t = toks[0].shape[0]
    nc = t // CHUNK
    n_tok, n_par, n_const = len(toks), len(pars), len(consts)

    def body(*refs):
        it = iter(refs)
        tok_refs = [next(it) for _ in range(n_tok)]
        par_refs = [next(it) for _ in range(n_par)]
        const_refs = [next(it) for _ in range(n_const)]
        o_ref, ss_ref, s_scr = next(it), next(it), next(it)
        d, p = pl.program_id(0), pl.program_id(1)

        @pl.when(p == 0)
        def _():
            s_scr[...] = jnp.zeros(s_scr.shape, F32)

        ss_ref[0, 0] = s_scr[...]
        res = step(d, *[r[...] for r in tok_refs], *[r[...] for r in par_refs], *[s_scr[h] for h in range(n_state)],
                   consts=[r[0] for r in const_refs])
        o_ref[0] = res[0]
        for h in range(n_state):
            s_scr[h] = res[1 + h]

    ch = lambda d, p: _chunk_of(d, p, nctx, nc)
    in_specs = [pl.BlockSpec((CHUNK, a.shape[1]), lambda d, p: (ch(d, p), 0)) for a in toks]
    in_specs += [pl.BlockSpec(a.shape, lambda d, p: (0, 0)) for a in pars]
    return pl.pallas_call(
        body, grid=(2, nc), in_specs=in_specs + _const_specs(consts),
        out_specs=[pl.BlockSpec((1, CHUNK, out_width), lambda d, p: (d, ch(d, p), 0)),
                   pl.BlockSpec((1, 1, n_state, 128, 128), lambda d, p: (d, p, 0, 0, 0))],
        out_shape=[jax.ShapeDtypeStruct((2, t, out_width), F32),
                   jax.ShapeDtypeStruct((2, nc, n_state, 128, 128), F32)],
        scratch_shapes=[pltpu.VMEM((n_state, 128, 128), F32)],
        compiler_params=_cp(("arbitrary", "arbitrary")), name=name)(*toks, *pars, *consts)


def _scan_bwd(name, step, toks, pars, ss, dout, n_state, nctx, ride=None, aux=None, consts=()):
    t = toks[0].shape[0]
    nc = t // CHUNK
    n_tok, n_par, n_const = len(toks), len(pars), len(consts)
    rd = ride if ride is not None else _NoRide
    n_aux = aux.shape[2] if aux is not None else 0

    def body(*refs):
        it = iter(refs)
        tok_refs = [next(it) for _ in range(n_tok)]
        par_refs = [next(it) for _ in range(n_par)]
        ss_ref, do_ref = next(it), next(it)
        aux_ref = next(it) if n_aux else None
        const_refs = [next(it) for _ in range(n_const)]
        ride_in = [next(it) for _ in range(rd.n)]
        dtok_refs = [next(it) for _ in range(n_tok)]
        dpar_refs = [next(it) for _ in range(n_par)]
        ride_out = [next(it) for _ in range(rd.n)]
        ds_scr = next(it)
        sems = list(it)
        d, pr = pl.program_id(0), pl.program_id(1)
        if ride is not None:
            @pl.when((d == 0) & (pr == 0))
            def _():
                ride.start(ride_in, ride_out, sems)

            @pl.when((d == 1) & (pr == nc - 1))
            def _():
                ride.wait(ride_in, ride_out, sems)

        @pl.when(pr == 0)
        def _():
            ds_scr[...] = jnp.zeros(ds_scr.shape, F32)

        kw = dict(aux=[aux_ref[0, 0, i] for i in range(n_aux)]) if n_aux else {}
        if n_const:
            kw['consts'] = [r[0] for r in const_refs]
        _, pull = jax.vjp(functools.partial(step, d, **kw), *[r[...] for r in tok_refs], *[r[...] for r in par_refs],
                          *[ss_ref[0, 0, h] for h in range(n_state)])
        grads = pull((do_ref[...], *[ds_scr[h] for h in range(n_state)]))
        for r, g in zip(dtok_refs, grads[:n_tok]):
            r[0] = g
        for h in range(n_state):
            ds_scr[h] = grads[n_tok + n_par + h]
        first = (d == 0) & (pr == 0)
        for r, g in zip(dpar_refs, grads[n_tok:n_tok + n_par]):
            @pl.when(first)
            def _(r=r, g=g):
                r[...] = g

            @pl.when(jnp.logical_not(first))
            def _(r=r, g=g):
                r[...] += g

    ch = lambda d, pr: _chunk_of(d, nc - 1 - pr, nctx, nc)
    in_specs = [pl.BlockSpec((CHUNK, a.shape[1]), lambda d, pr: (ch(d, pr), 0)) for a in toks]
    in_specs += [pl.BlockSpec(a.shape, lambda d, pr: (0, 0)) for a in pars]
    in_specs += [pl.BlockSpec((1, 1, n_state, 128, 128), lambda d, pr: (d, nc - 1 - pr, 0, 0, 0)),
                 pl.BlockSpec((CHUNK, dout.shape[1]), lambda d, pr: (ch(d, pr), 0))]
    if n_aux:
        in_specs += [pl.BlockSpec((1, 1, n_aux, CHUNK, CHUNK), lambda d, pr: (d, nc - 1 - pr, 0, 0, 0))]
    in_specs += _const_specs(consts)
    out_specs = [pl.BlockSpec((1, CHUNK, a.shape[1]), lambda d, pr: (d, ch(d, pr), 0)) for a in toks]
    out_specs += [pl.BlockSpec(a.shape, lambda d, pr: (0, 0)) for a in pars]
    out_shape = [jax.ShapeDtypeStruct((2, t, a.shape[1]), F32) for a in toks]
    out_shape += [jax.ShapeDtypeStruct(a.shape, F32) for a in pars]
    return pl.pallas_call(
        body, grid=(2, nc), in_specs=in_specs + rd.in_specs, out_specs=out_specs + rd.out_specs,
        out_shape=out_shape + rd.out_shape, scratch_shapes=[pltpu.VMEM((n_state, 128, 128), F32)] + rd.scratch,
        compiler_params=_cp(("arbitrary", "arbitrary")), name=name)(
            *toks, *pars, ss, dout, *([aux] if n_aux else []), *consts, *rd.arrs)


def _scan2_fwd(name, step2, toks, out_width, n_state, nctx, n_aux, ride=None):
    t = toks[0].shape[0]
    nc = t // CHUNK
    n_tok = len(toks)
    rd = ride if ride is not None else _NoRide

    def body(*refs):
        it = iter(refs)
        tok_refs = [next(it) for _ in range(2 * n_tok)]
        ride_in = [next(it) for _ in range(rd.n)]
        o_refs = [next(it), next(it)]
        ss_ref, aux_ref = next(it), next(it)
        ride_out = [next(it) for _ in range(rd.n)]
        s_scr = next(it)
        sems = list(it)
        p = pl.program_id(0)
        if ride is not None:
            @pl.when(p == 0)
            def _():
                ride.start(ride_in, ride_out, sems)

        @pl.when(p == 0)
        def _():
            s_scr[...] = jnp.zeros(s_scr.shape, F32)

        for d in range(2):
            ss_ref[d, 0] = s_scr[d * n_state:(d + 1) * n_state]
        res = step2(*[r[...] for r in tok_refs], *[s_scr[u] for u in range(2 * n_state)], want_aux=True)
        for d in range(2):
            o_refs[d][...] = res[d]
            for i in range(n_aux):
                aux_ref[d, 0, i] = res[2 + 2 * n_state + d * n_aux + i]
        for u in range(2 * n_state):
            s_scr[u] = res[2 + u]
        if ride is not None:
            @pl.when(p == nc - 1)
            def _():
                ride.wait(ride_in, ride_out, sems)

    def tok_spec(a, d):
        return pl.BlockSpec((CHUNK, a.shape[1]), lambda p: (_chunk_of(d, p, nctx, nc), 0))

    return pl.pallas_call(
        body, grid=(nc,), in_specs=[tok_spec(a, d) for d in range(2) for a in toks] + rd.in_specs,
        out_specs=[pl.BlockSpec((CHUNK, out_width), lambda p: (_chunk_of(0, p, nctx, nc), 0)),
                   pl.BlockSpec((CHUNK, out_width), lambda p: (_chunk_of(1, p, nctx, nc), 0)),
                   pl.BlockSpec((2, 1, n_state, 128, 128), lambda p: (0, p, 0, 0, 0)),
                   pl.BlockSpec((2, 1, n_aux, CHUNK, CHUNK), lambda p: (0, p, 0, 0, 0))] + rd.out_specs,
        out_shape=[jax.ShapeDtypeStruct((t, out_width), F32), jax.ShapeDtypeStruct((t, out_width), F32),
                   jax.ShapeDtypeStruct((2, nc, n_state, 128, 128), F32),
                   jax.ShapeDtypeStruct((2, nc, n_aux, CHUNK, CHUNK), F32)] + rd.out_shape,
        scratch_shapes=[pltpu.VMEM((2 * n_state, 128, 128), F32)] + rd.scratch,
        compiler_params=_cp(("arbitrary",)), name=name)(*toks, *toks, *rd.arrs)


ADA_TN = 512


def _ada_fwd(cc, ada_w4, ada_b):
    per = ada_w4.shape[2] // ADA_TN
    n = N_PLANE * ada_w4.shape[2]

    def body(c_ref, w_ref, b_ref, o_ref):
        s = _silu(c_ref[...]).astype(_MXU_DTYPE)
        o_ref[...] = jnp.dot(s, w_ref[0].astype(_MXU_DTYPE), preferred_element_type=F32) + b_ref[...]

    return pl.pallas_call(
        body, grid=(n // ADA_TN,),
        in_specs=[pl.BlockSpec((8, D_MODEL), lambda j: (0, 0)),
                  pl.BlockSpec((1, D_MODEL, ADA_TN), lambda j: (j // per, 0, j % per)),
                  pl.BlockSpec((1, ADA_TN), lambda j: (0, j))],
        out_specs=pl.BlockSpec((8, ADA_TN), lambda j: (0, j)), out_shape=jax.ShapeDtypeStruct((8, n), F32),
        compiler_params=_cp(("parallel",)), name="ada_fwd")(cc, ada_w4, ada_b)


def _ada_bwd(cc, ada_w4, dmods):
    per = ada_w4.shape[2] // ADA_TN
    n = N_PLANE * ada_w4.shape[2]
    nj = n // ADA_TN

    def body(c_ref, w_ref, g_ref, dw_ref, db_ref, dc_ref):
        j = pl.program_id(0)
        g = g_ref[...]
        row = lax.broadcasted_iota(jnp.int32, g.shape, 0)
        g = jnp.where(row < 2, g, 0.0)
        s, pull = jax.vjp(_silu, c_ref[...])
        dw_ref[0] = lax.dot_general(s.astype(_MXU_DTYPE), g.astype(_MXU_DTYPE), _TN,
                                    preferred_element_type=F32).astype(dw_ref.dtype)
        db_ref[...] = jnp.sum(g, axis=0, keepdims=True)
        ds = lax.dot_general(g.astype(_MXU_DTYPE), w_ref[0].astype(_MXU_DTYPE), _NT, preferred_element_type=F32)

        @pl.when(j == 0)
        def _():
            dc_ref[...] = ds

        @pl.when(j > 0)
        def _():
            dc_ref[...] += ds

        @pl.when(j == nj - 1)
        def _():
            dc_ref[...] = pull(dc_ref[...])[0]

    wspec = pl.BlockSpec((1, D_MODEL, ADA_TN), lambda j: (j // per, 0, j % per))
    return pl.pallas_call(
        body, grid=(nj,),
        in_specs=[pl.BlockSpec((8, D_MODEL), lambda j: (0, 0)), wspec, pl.BlockSpec((8, ADA_TN), lambda j: (0, j))],
        out_specs=[wspec, pl.BlockSpec((1, ADA_TN), lambda j: (0, j)), pl.BlockSpec((8, D_MODEL), lambda j: (0, 0))],
        out_shape=[jax.ShapeDtypeStruct(ada_w4.shape, _GRAD_DTYPE), jax.ShapeDtypeStruct((1, n), F32),
                   jax.ShapeDtypeStruct((8, D_MODEL), F32)],
        compiler_params=_cp(("arbitrary",)), name="ada_bwd")(cc, ada_w4, dmods)


def _ada_part(cc16, w_shard):
    n = w_shard.shape[1]

    def body(c_ref, w_ref, o_ref):
        s = _silu(c_ref[...]).astype(_MXU_DTYPE)
        o_ref[...] = jnp.dot(s, w_ref[...].astype(_MXU_DTYPE), preferred_element_type=F32)

    return pl.pallas_call(
        body, grid=(n // ADA_TN,),
        in_specs=[pl.BlockSpec((16, D_MODEL), lambda j: (0, 0)), pl.BlockSpec((D_MODEL, ADA_TN), lambda j: (0, j))],
        out_specs=pl.BlockSpec((16, ADA_TN), lambda j: (0, j)), out_shape=jax.ShapeDtypeStruct((16, n), F32),
        compiler_params=_cp(("parallel",)), name="ada_part")(cc16, w_shard)


def _ada_bwd_shard(cc16, w_shard, d_lat, d_ctx):
    n = w_shard.shape[1]
    nj = n // ADA_TN

    def body(c_ref, w_ref, gl_ref, gc_ref, dw_ref, dc_ref):
        j = pl.program_id(0)
        s, pull = jax.vjp(_silu, c_ref[...])
        row = lax.broadcasted_iota(jnp.int32, (8, 1), 0)
        dctx = jnp.where(row == 0, jnp.sum(gc_ref[...], axis=0, keepdims=True), 0.0)
        mxu = lambda v: v.astype(_MXU_DTYPE)
        dw_ref[...] = (lax.dot_general(mxu(s[0:8]), mxu(gl_ref[...]), _TN, preferred_element_type=F32)
                       + lax.dot_general(mxu(s[8:16]), mxu(dctx), _TN, preferred_element_type=F32))
        ds = lax.dot_general(mxu(dctx), mxu(w_ref[...]), _NT, preferred_element_type=F32)

        @pl.when(j == 0)
        def _():
            dc_ref[...] = ds

        @pl.when(j > 0)
        def _():
            dc_ref[...] += ds

        @pl.when(j == nj - 1)
        def _():
            ct = jnp.concatenate([jnp.zeros((8, D_MODEL), F32), dc_ref[...]], axis=0)
            dc_ref[...] = 0.5 * pull(ct)[0][8:16]

    tile = pl.BlockSpec((8, ADA_TN), lambda j: (0, j))
    wspec = pl.BlockSpec((D_MODEL, ADA_TN), lambda j: (0, j))
    return pl.pallas_call(
        body, grid=(nj,), in_specs=[pl.BlockSpec((16, D_MODEL), lambda j: (0, 0)), wspec, tile, tile],
        out_specs=[wspec, pl.BlockSpec((8, D_MODEL), lambda j: (0, 0))],
        out_shape=[jax.ShapeDtypeStruct((D_MODEL, n), F32), jax.ShapeDtypeStruct((8, D_MODEL), F32)],
        compiler_params=_cp(("arbitrary",)), name="ada_bwd_shard")(cc16, w_shard, d_lat, d_ctx)


def _rowsum2(a, b):
    def body(a_ref, b_ref, o_ref):
        o_ref[...] = jnp.sum(a_ref[...], axis=0, keepdims=True) + jnp.sum(b_ref[...], axis=0, keepdims=True)

    return pl.pallas_call(body, out_shape=jax.ShapeDtypeStruct((1, a.shape[1]), F32), name="ada_b_grad")(a, b)


def _tail(h1, ff, mods, wf, tgt, nctx_t, tl):
    t = h1.shape[0]
    nt = t // tl

    def loss_fn(valid, h1v, ffv, g8, w, tg):
        h2 = h1v + g8[0:1] * ffv
        y = _rms(h2) * w
        err = (y - tg) ** 2
        return 0.5 * jnp.sum(jnp.mean(err, axis=-1, keepdims=True), axis=0, keepdims=True) * valid

    def body(h1_ref, ff_ref, g_ref, w_ref, t_ref, loss_ref, dh_ref, dff_ref, dg_ref, dw_ref):
        i = pl.program_id(0)
        valid = jnp.where(i < nctx_t, 0.0, 1.0)
        lv, pull = jax.vjp(functools.partial(loss_fn, valid), h1_ref[...], ff_ref[...].astype(F32), g_ref[...],
                           w_ref[...], t_ref[...])
        dh, dff, dg, dw, _ = pull(jnp.ones((1, 1), F32))
        dh_ref[...] = dh
        dff_ref[...] = dff.astype(dff_ref.dtype)
        lb = jnp.broadcast_to(lv, loss_ref.shape)

        @pl.when(i == 0)
        def _():
            loss_ref[...] = lb
            dg_ref[...] = dg
            dw_ref[...] = dw

        @pl.when(i > 0)
        def _():
            loss_ref[...] += lb
            dg_ref[...] += dg
            dw_ref[...] += dw

    tok = pl.BlockSpec((tl, D_MODEL), lambda i: (i, 0))
    return pl.pallas_call(
        body, grid=(nt,),
        in_specs=[tok, tok, pl.BlockSpec((8, D_MODEL), lambda i: (0, 5)), pl.BlockSpec((1, D_MODEL), lambda i: (0, 0)),
                  pl.BlockSpec((tl, D_MODEL), lambda i: (jnp.maximum(i - nctx_t, 0), 0))],
        out_specs=[pl.BlockSpec((8, 128), lambda i: (0, 0)), tok, tok, pl.BlockSpec((8, D_MODEL), lambda i: (0, 0)),
                   pl.BlockSpec((1, D_MODEL), lambda i: (0, 0))],
        out_shape=[jax.ShapeDtypeStruct((8, 128), F32), jax.ShapeDtypeStruct((t, D_MODEL), F32),
                   jax.ShapeDtypeStruct((t, D_MODEL), _MXU_DTYPE), jax.ShapeDtypeStruct((8, D_MODEL), F32),
                   jax.ShapeDtypeStruct((1, D_MODEL), F32)],
        compiler_params=_cp(("arbitrary",)), name="tail_loss")(h1, ff, mods, wf, tgt)


def _pack_w_in(w4):
    ns = w4.shape[2]
    placed = []
    for s0, s1, p0 in IN_SEGMENTS:
        for j in range(N_PLANE):
            lo, hi = max(s0, j * ns), min(s1, (j + 1) * ns)
            if lo < hi:
                placed.append((p0 + lo - s0, w4[j][:, lo - j * ns:hi - j * ns]))
    placed.sort(key=lambda e: e[0])
    pieces, end = [], 0
    for pos, piece in placed:
        assert pos == end, (pos, end)
        pieces.append(piece)
        end = pos + piece.shape[1]
    pieces.append(jnp.zeros((w4.shape[1], P_TOTAL - end), w4.dtype))
    return jnp.concatenate(pieces, axis=1)


def _unpack_w_in(g):
    ns = D_IN_PROJ // N_PLANE
    shards = []
    for j in range(N_PLANE):
        pieces = []
        for s0, s1, p0 in IN_SEGMENTS:
            lo, hi = max(s0, j * ns), min(s1, (j + 1) * ns)
            if lo < hi:
                pieces.append(g[:, p0 + lo - s0:p0 + hi - s0])
        shards.append(jnp.concatenate(pieces, axis=1))
    return jnp.stack(shards)


LATE_WEIGHTS = ['w_br_gdn', 'w_br_ssm', 'w_out', 'w_ffn_in', 'w_ffn_out']
COL_STACKED = ('ada_w', 'w_in', 'w_ffn_in')


def _from_shards(n, g):
    if n in COL_STACKED:
        return g
    if SHARD_AXIS[n] == 0:
        return g.reshape(N_PLANE * g.shape[1], g.shape[2])
    return jnp.concatenate([g[j] for j in range(N_PLANE)], axis=1)


def _to_dest_blocks(n, g):
    if g.ndim == 3:
        return g
    if SHARD_AXIS[n] == 0:
        return g.reshape(N_PLANE, g.shape[0] // N_PLANE, g.shape[1])
    sz = g.shape[1] // N_PLANE
    return jnp.stack([g[:, j * sz:(j + 1) * sz] for j in range(N_PLANE)])


def _local_step(x, c, ctx, tgt, W, late_shards=None):
    distributed = late_shards is not None
    lc, l = ctx.shape[0], x.shape[0]
    t = lc + l
    tl = 256
    assert lc == tl and l % tl == 0 and lc % CHUNK == 0
    nt, nctx_t, nctx = t // tl, lc // tl, lc // CHUNK
    act = _MXU_DTYPE
    r1 = lambda v: v.reshape(1, -1)

    xt = jnp.concatenate([ctx, x], axis=0)
    if distributed:
        lin = 4 * lax.axis_index("x") + 2 * lax.axis_index("y") + lax.axis_index("c")
        c_all = _gather_all("gather_c", jnp.broadcast_to(c, (8, D_MODEL)))[:, 0]
        cc16 = jnp.concatenate([c_all, r1(W['c_ctx']), jnp.zeros((7, D_MODEL), F32)], axis=0)
        (parts,) = _plane_exchange("gather_mods", [_ada_part(cc16, W['ada_w'])], gather=True)
        mods_all = jnp.transpose(parts, (1, 0, 2)).reshape(16, -1) + r1(W['ada_b'])
        mods = jnp.concatenate([lax.dynamic_slice_in_dim(mods_all, lin, 1, axis=0), mods_all[8:9],
                                jnp.zeros((6, mods_all.shape[1]), F32)], axis=0)
    else:
        cc = jnp.concatenate([c, r1(W['c_ctx']), jnp.zeros((6, D_MODEL), F32)], axis=0)
        mods = _ada_fwd(cc, W['ada_w'], r1(W['ada_b']))

    def mod(kk):
        return Op(mods, (8, D_MODEL), lambda j, i, kk=kk: (0, kk))

    def tokop(arr, w=D_MODEL, off=0, tl_=tl):
        if not isinstance(arr, list) and arr.ndim == 3:
            return Op(arr, (2, tl_, w), lambda j, i: (0, i, off + j), load=_sum_dirs)
        return Op(arr, (tl_, w), lambda j, i: (i, off + j))

    def outop(n, dtype, w=D_MODEL, tl_=tl):
        return Op(((t, n), dtype), (tl_, w), lambda j, i: (i, j))

    def parop(arr, w, off=0):
        return Op(arr, (arr.shape[0], w), lambda j, i: (0, off + j))

    def parout(rows, n, w):
        return Op(((rows, n), F32), (rows, w), lambda j, i: (0, j))

    n1w = r1(W['norm1_w'])
    pre_fn = functools.partial(_prenorm_fn, nctx_t)
    pre_tok, pre_par = [tokop(xt)], [parop(n1w, D_MODEL), mod(1), mod(0)]
    (a,) = _tw_fwd("prenorm_fwd", pre_fn, (1, nt), pre_tok + pre_par, [outop(D_MODEL, act)])
    wp = _pack_w_in(W['w_in'])
    proj = _matmul(a, wp, 'nn', "in_proj")

    gcw, gcb = W['gdn_conv_w'], r1(W['gdn_conv_b'])
    scw, scb = W['ssm_conv_w'], r1(W['ssm_conv_b'])
    conv_parts = {}

    def conv_part(name, mode, poff, cw, cb, woff, width):
        fn = functools.partial(_conv_fn, nctx_t, mode)
        bw = min(width, 1024)
        tok_ = [tokop(proj, bw, poff // bw)]
        par_ = [parop(cw, bw, woff // bw), parop(cb, bw, woff // bw)]
        conv_parts[name] = (fn, tok_, par_, width, bw, poff)
        (o,) = _tw_fwd("conv_" + name, fn, (width // bw, nt), tok_ + par_, [outop(width, F32, bw)])
        return o

    q = conv_part('q', 'q', P_QKV, gcw, gcb, 0, 1024)
    k = conv_part('k', 'k', P_QKV + 1024, gcw, gcb, 1024, 1024)
    v = conv_part('v', 'none', P_QKV + 2048, gcw, gcb, 2048, 1024)
    xs = conv_part('xs', 'none', P_XBC, scw, scb, 0, 2048)
    bm = conv_part('bm', 'none', P_XBC + 2048, scw, scb, 2048, 512)
    cm = conv_part('cm', 'none', P_XBC + 2560, scw, scb, 2560, 512)

    z16, z32 = jnp.zeros((16,), F32), jnp.zeros((32,), F32)
    p0 = jnp.concatenate([W['gdn_a_log'].reshape(-1), jnp.zeros((112,), F32)]).reshape(1, 128)
    p1 = jnp.concatenate([W['gdn_dt_bias'].reshape(-1), z16, W['ssm_dt_bias'].reshape(-1), z32]).reshape(1, 128)
    alog = jnp.concatenate([z32, W['ssm_a_log'].reshape(-1), z32]).reshape(1, 128)
    act_tok, act_par = [tokop(proj, 128, P_SMALL // 128)], [parop(p0, 128), parop(p1, 128)]
    (sp,) = _tw_fwd("small_act", _act_fn, (1, nt), act_tok + act_par, [outop(128, F32, 128)])

    gather_late = PlaneExchange([late_shards[n] for n in LATE_WEIGHTS], gather=True) if distributed else None
    o_f, o_b, ss, tri, *late = _scan2_fwd("gdn_scan_fwd", _gdn_step2, [q, k, v, sp], 1024, GDN_HEADS, nctx,
                                          GDN_HEADS, ride=gather_late)
    if distributed:
        W = dict(W, **{n: _from_shards(n, g) for n, g in zip(LATE_WEIGHTS, late)})
    ssd_consts = _ssd_consts()
    y2, hs = _scan_fwd("ssd_scan_fwd", _ssd_step, [xs, bm, cm, sp], [alog], ssd_consts, 2048, SSM_PAIRS, nctx)

    tlm = 128
    ntm = t // tlm
    gnw = r1(W['gdn_norm_w'])
    dl = jnp.repeat(W['ssm_d'], 64).reshape(1, 2048)
    snw = r1(W['ssm_norm_w'])
    mg_tok = [tokop([o_f, o_b], 1024, 0, tlm), tokop(proj, 1024, P_ZG // 1024, tlm)]
    mg_par = [parop(gnw, 128)]
    (og,) = _tw_fwd("mix_gdn", _mixg_fn, (1, ntm), mg_tok + mg_par, [outop(1024, act, 1024, tlm)])
    ms_tok = [tokop(y2, 2048, 0, tlm), tokop(xs, 2048, 0, tlm), tokop(proj, 2048, P_ZS // 2048, tlm)]
    ms_par = [parop(dl, 2048), parop(snw, 2048)]
    (yg,) = _tw_fwd("mix_ssm", _mixs_fn, (1, ntm), ms_tok + ms_par, [outop(2048, act, 2048, tlm)])

    pg = _matmul(og, W['w_br_gdn'], 'nn', "br_gdn", out_dtype=act)
    ps = _matmul(yg, W['w_br_ssm'], 'nn', "br_ssm", out_dtype=act)
    mr_tok = [tokop(proj, 2048, P_GATE // 2048), tokop(pg), tokop(ps)]
    (mrg,) = _tw_fwd("merge", _merge_fn, (1, nt), mr_tok, [outop(1024, act)])
    mo = _matmul(mrg, W['w_out'], 'nn', "w_out")

    n2w = r1(W['norm2_w'])
    n2_tok, n2_par = [tokop(xt), tokop(mo)], [mod(2), parop(n2w, D_MODEL), mod(4), mod(3)]
    h1, f = _tw_fwd("norm2_fwd", _norm2_fn, (1, nt), n2_tok + n2_par, [outop(1024, F32), outop(1024, act)])
    u2 = _matmul(f, W['w_ffn_in'], 'nn', "ffn_in", out_dtype=act, halves='out')
    swb = D_FF // 2
    both = lambda arr: Op(arr, (2, tl, swb), lambda j, i: (0, i, j))
    sw_tok = [both(u2)]
    (sw,) = _tw_fwd("swiglu", _swiglu_fn, (D_FF // swb, nt), sw_tok, [outop(D_FF, act, swb)])
    ff = _matmul(sw, W['w_ffn_out'], 'nn', "ffn_out")

    loss8, dh1, dff, dg2, dnf = _tail(h1, ff, mods, r1(W['norm_f_w']), tgt, nctx_t, tl)
    loss = loss8[0, 0]

    G = {}
    G['norm_f_w'] = dnf.reshape(-1)
    G['w_ffn_out'] = _matmul(sw, dff, 'tn', "d_ffn_out", out_dtype=_GRAD_DTYPE)
    dsw = _matmul(dff, W['w_ffn_out'], 'nt', "d_sw", out_dtype=act)
    (du2,) = _tw_bwd("swiglu_bwd", _swiglu_fn, (D_FF // swb, nt), sw_tok, [], [[tokop(dsw, swb)]],
                     [both(((2, t, D_FF), act))], [])
    G['w_ffn_in'] = _matmul(f, du2, 'tn', "d_ffn_in", out_dtype=_GRAD_DTYPE, stacked_out=True, halves='b')
    df = _matmul(du2, W['w_ffn_in'], 'nt', "d_f", halves='a')
    dxt1, dmo, dg1, dn2, dsc2, dsh2 = _tw_bwd(
        "norm2_bwd", _norm2_fn, (1, nt), n2_tok, n2_par, [[tokop(dh1)], [tokop(df)]],
        [outop(1024, F32), outop(1024, act)],
        [parout(8, 1024, 1024), parout(1, 1024, 1024), parout(8, 1024, 1024), parout(8, 1024, 1024)])
    G['norm2_w'] = dn2.reshape(-1)
    G['w_out'] = _matmul(mrg, dmo, 'tn', "d_w_out", out_dtype=_GRAD_DTYPE)
    dmrg = _matmul(dmo, W['w_out'], 'nt', "d_mrg", out_dtype=act)
    def win(off, w, tl_=tl):
        return Op(((t, P_TOTAL), act), (tl_, w), lambda j, i: (i, off // w + j))

    dproj = lax.empty((t, P_TOTAL), act)
    dproj, dpg, dps = _tw_bwd("merge_bwd", _merge_fn, (1, nt), mr_tok, [], [[tokop(dmrg)]],
                              [win(P_GATE, 2048), outop(1024, act), outop(1024, act)], [], into={0: dproj})
    G['w_br_gdn'] = _matmul(og, dpg, 'tn', "d_br_gdn", out_dtype=_GRAD_DTYPE)
    G['w_br_ssm'] = _matmul(yg, dps, 'tn', "d_br_ssm", out_dtype=_GRAD_DTYPE)
    dog = _matmul(dpg, W['w_br_gdn'], 'nt', "d_og")
    dyg = _matmul(dps, W['w_br_ssm'], 'nt', "d_yg")

    do, dproj, dgnw = _tw_bwd("mix_gdn_bwd", _mixg_fn, (1, ntm), mg_tok, mg_par, [[tokop(dog, 1024, 0, tlm)]],
                              [outop(1024, F32, 1024, tlm), win(P_ZG, 1024, tlm)], [parout(1, 128, 128)],
                              into={1: dproj})
    G['gdn_norm_w'] = dgnw.reshape(-1)
    dy, dxs_a, dproj, ddl, dsnw = _tw_bwd(
        "mix_ssm_bwd", _mixs_fn, (1, ntm), ms_tok, ms_par, [[tokop(dyg, 2048, 0, tlm)]],
        [outop(2048, F32, 2048, tlm), outop(2048, F32, 2048, tlm), win(P_ZS, 2048, tlm)],
        [parout(1, 2048, 2048), parout(1, 2048, 2048)], into={2: dproj})
    G['ssm_d'] = ddl.reshape(SSM_HEADS, 64).sum(axis=1)
    G['ssm_norm_w'] = dsnw.reshape(-1)

    dq2, dk2, dv2, dsp_g = _scan_bwd("gdn_scan_bwd", _gdn_step, [q, k, v, sp], [], ss, do, GDN_HEADS, nctx, aux=tri)
    received = {}
    scatter_late = (PlaneExchange([_to_dest_blocks(n, G.pop(n)) for n in LATE_WEIGHTS], gather=False)
                    if distributed else None)
    dxs2, dbm2, dcm2, dsp_s, dalog, *got = _scan_bwd("ssd_scan_bwd", _ssd_step, [xs, bm, cm, sp], [alog], hs, dy,
                                                     SSM_PAIRS, nctx, ride=scatter_late, consts=ssd_consts)
    received.update(zip(LATE_WEIGHTS, got))
    G['ssm_a_log'] = dalog[0, 32:96].reshape(2, SSM_HEADS)

    dconv_w, dconv_b = {}, {}

    def conv_bwd(name, cots_, buf):
        fn, tok_, par_, width, bw, poff = conv_parts[name]
        buf, dconv_w[name], dconv_b[name] = _tw_bwd(
            "conv_" + name + "_bwd", fn, (width // bw, nt), tok_, par_, [[tokop(c_, bw) for c_ in cots_]],
            [win(poff, bw)], [parout(3, width, bw), parout(1, width, bw)], into={0: buf})
        return buf

    dproj = conv_bwd('q', [dq2], dproj)
    dproj = conv_bwd('k', [dk2], dproj)
    dproj = conv_bwd('v', [dv2], dproj)
    dproj = conv_bwd('xs', [dxs2, dxs_a], dproj)
    dproj = conv_bwd('bm', [dbm2], dproj)
    dproj = conv_bwd('cm', [dcm2], dproj)
    G['gdn_conv_w'] = jnp.concatenate([dconv_w['q'], dconv_w['k'], dconv_w['v']], axis=1)
    G['gdn_conv_b'] = jnp.concatenate([dconv_b['q'], dconv_b['k'], dconv_b['v']], axis=1).reshape(-1)
    G['ssm_conv_w'] = jnp.concatenate([dconv_w['xs'], dconv_w['bm'], dconv_w['cm']], axis=1)
    G['ssm_conv_b'] = jnp.concatenate([dconv_b['xs'], dconv_b['bm'], dconv_b['cm']], axis=1).reshape(-1)

    dproj, dp0, dp1 = _tw_bwd("small_act_bwd", _act_fn, (1, nt), act_tok, act_par,
                              [[tokop(dsp_g, 128), tokop(dsp_s, 128)]], [win(P_SMALL, P_TOTAL - P_SMALL)],
                              [parout(1, 128, 128), parout(1, 128, 128)], into={0: dproj})
    G['gdn_a_log'] = dp0[0, 0:16].reshape(2, GDN_HEADS)
    G['gdn_dt_bias'] = dp1[0, 0:16].reshape(2, GDN_HEADS)
    G['ssm_dt_bias'] = dp1[0, 32:96].reshape(2, SSM_HEADS)

    G['w_in'] = _unpack_w_in(_matmul(a, dproj, 'tn', "d_w_in", out_dtype=_GRAD_DTYPE))
    if distributed:
        da, (received['w_in'],) = _matmul(dproj, wp, 'nt', "d_a", ride=PlaneExchange([G.pop('w_in')], gather=False))
    else:
        da = _matmul(dproj, wp, 'nt', "d_a")

    gx_out = Op(((l, D_MODEL), F32), (tl, D_MODEL), lambda j, i: (jnp.maximum(i - nctx_t, 0), 0))
    grad_x, dn1, dsc1, dsh1 = _tw_bwd(
        "prenorm_bwd", pre_fn, (1, nt), pre_tok, pre_par, [[tokop(da)]], [gx_out],
        [parout(1, 1024, 1024), parout(8, 1024, 1024), parout(8, 1024, 1024)], tok_add=tokop(dxt1),
        sem=("arbitrary", "arbitrary"))
    G['norm1_w'] = dn1.reshape(-1)
    dmods = jnp.concatenate([dsh1, dsc1, dg1, dsh2, dsc2, dg2], axis=1)
    if distributed:
        dm_all = _gather_all("gather_dmods", dmods)
        ns = W['ada_w'].shape[1]
        mine = lax.dynamic_slice_in_dim(dm_all, (2 * lax.axis_index("x") + lax.axis_index("y")) * ns, ns, axis=2)
        G['ada_w'], dcc = _ada_bwd_shard(cc16, W['ada_w'], mine[:, 0], mine[:, 1])
        G['ada_b'] = jnp.where(lin == 0, _rowsum2(dm_all[:, 0], dm_all[:, 1]).reshape(-1), 0.0)
        G['c_ctx'] = dcc[0]
    else:
        G['ada_w'], dab, dcc = _ada_bwd(cc, W['ada_w'], dmods)
        G['ada_b'] = dab.reshape(-1)
        G['c_ctx'] = dcc[1]
    return loss, grad_x, G, received


def _row_tile(r, c):
    for tr in (512, 256, 128, 64, 32, 16, 8):
        if r % tr == 0 and tr * c * 4 <= (1 << 20):
            return tr
    return r


def _sum4(name, rv):
    _, r, c = rv.shape
    tr = _row_tile(r, c)

    def body(r_ref, o_ref):
        o_ref[...] = ((r_ref[0].astype(F32) + r_ref[1].astype(F32)) + r_ref[2].astype(F32)) + r_ref[3].astype(F32)

    return pl.pallas_call(
        body, grid=(r // tr,), in_specs=[pl.BlockSpec((N_PLANE, tr, c), lambda i: (0, i, 0))],
        out_specs=pl.BlockSpec((tr, c), lambda i: (i, 0)), out_shape=jax.ShapeDtypeStruct((r, c), F32),
        compiler_params=_cp(("parallel",)), name=name)(rv)


def _adamw(name, w, m, v, p, q=None):
    r, c = w.shape
    tr = _row_tile(r, c)
    grads = [p] if q is None else [p, q]

    def body(w_ref, m_ref, v_ref, *refs):
        g_ref, d_ref, mo_ref, vo_ref = refs[len(grads):]
        g = refs[0][...] if q is None else refs[0][...] + refs[1][...]
        mn = ADAM_B1 * m_ref[...] + (1.0 - ADAM_B1) * g
        vn = ADAM_B2 * v_ref[...] + (1.0 - ADAM_B2) * jnp.square(g)
        m_hat = mn / (1.0 - ADAM_B1 ** ADAM_STEP)
        v_hat = vn / (1.0 - ADAM_B2 ** ADAM_STEP)
        g_ref[...] = g
        d_ref[...] = -ADAM_LR * (m_hat / (jnp.sqrt(v_hat) + ADAM_EPS) + ADAM_WD * w_ref[...])
        mo_ref[...] = mn
        vo_ref[...] = vn

    spec = pl.BlockSpec((tr, c), lambda i: (i, 0))
    return pl.pallas_call(
        body, grid=(r // tr,), in_specs=[spec] * (3 + len(grads)), out_specs=[spec] * 4,
        out_shape=[jax.ShapeDtypeStruct((r, c), F32)] * 4, compiler_params=_cp(("parallel",)), name=name)(w, m, v, *grads)


SMALL_ROWS = 24


def _pack_small(d):
    v = jnp.concatenate([d[n].reshape(-1) for n in SMALL])
    return jnp.pad(v, (0, SMALL_ROWS * 1024 - v.shape[0])).reshape(SMALL_ROWS, 1024)


def _unpack_small(buf, like):
    v = buf.reshape(-1)
    out, off = {}, 0
    for n in SMALL:
        sz = like[n].size
        out[n] = v[off:off + sz].reshape(like[n].shape)
        off += sz
    return out


def kernel(x, c, ctx, c_ctx, ada_w, ada_b, norm1_w, w_in, gdn_conv_w, gdn_conv_b, gdn_a_log, gdn_dt_bias, gdn_norm_w, ssm_conv_w, ssm_conv_b, ssm_a_log, ssm_dt_bias, ssm_d, ssm_norm_w, w_br_gdn, w_br_ssm, w_out, norm2_w, w_ffn_in, w_ffn_out, norm_f_w, loss_target, m_c_ctx, m_ada_w, m_ada_b, m_norm1_w, m_w_in, m_gdn_conv_w, m_gdn_conv_b, m_gdn_a_log, m_gdn_dt_bias, m_gdn_norm_w, m_ssm_conv_w, m_ssm_conv_b, m_ssm_a_log, m_ssm_dt_bias, m_ssm_d, m_ssm_norm_w, m_w_br_gdn, m_w_br_ssm, m_w_out, m_norm2_w, m_w_ffn_in, m_w_ffn_out, m_norm_f_w, v_c_ctx, v_ada_w, v_ada_b, v_norm1_w, v_w_in, v_gdn_conv_w, v_gdn_conv_b, v_gdn_a_log, v_gdn_dt_bias, v_gdn_norm_w, v_ssm_conv_w, v_ssm_conv_b, v_ssm_a_log, v_ssm_dt_bias, v_ssm_d, v_ssm_norm_w, v_w_br_gdn, v_w_br_ssm, v_w_out, v_norm2_w, v_w_ffn_in, v_w_ffn_out, v_norm_f_w):
    args = dict(locals())
    wl = {n: args[n] for n in WEIGHTS}
    ml = {n: args['m_' + n] for n in WEIGHTS}
    vl = {n: args['v_' + n] for n in WEIGHTS}

    def nodepth(n, a):
        return a if n in ('c_ctx', 'norm_f_w') else a[0]

    shard = {n: nodepth(n, wl[n]).astype(_MXU_DTYPE if n in MXU_WEIGHTS else F32) for n in SHARDED}
    first = [n for n in SHARDED if n not in LATE_WEIGHTS and n != 'ada_w']
    W = {n: nodepth(n, wl[n]) for n in SMALL}
    W['ada_w'] = shard['ada_w']
    for n, g in zip(first, _gather_two_level("all_gather_plane", [shard[n] for n in first])):
        W[n] = _from_shards(n, g)

    loss_local, grad_x, G, received = _local_step(x[0], c, ctx[0], loss_target[0], W,
                                                  late_shards={n: shard[n] for n in LATE_WEIGHTS})
    loss = lax.psum(loss_local, ("x", "y", "c"))

    small_g = _pack_small(G)
    last = [n for n in SHARDED if n not in received and n != 'ada_w']
    send = [_to_dest_blocks(n, G[n]) for n in last] + [jnp.broadcast_to(small_g[None], (N_PLANE,) + small_g.shape)]
    received.update(zip(last + ['small'], _plane_exchange("scatter_plane", send, gather=False)))
    names = [n for n in SHARDED if n != 'ada_w'] + ['small']
    plane_sum = [_sum4("sum4_" + n, received[n]) for n in names]
    other = _swap_sibling(plane_sum)

    wd = {n: nodepth(n, wl[n]) for n in WEIGHTS}
    md = {n: nodepth(n, ml[n]) for n in WEIGHTS}
    vd = {n: nodepth(n, vl[n]) for n in WEIGHTS}
    res = {'ada_w': [o.reshape(wl['ada_w'].shape)
                     for o in _adamw("adamw_ada_w", wd['ada_w'], md['ada_w'], vd['ada_w'], G['ada_w'])]}
    for n, p, q in zip(names, plane_sum, other):
        if n == 'small':
            outs = _adamw("adamw_small", _pack_small(wd), _pack_small(md), _pack_small(vd), p, q)
            unpacked = [_unpack_small(o, wd) for o in outs]
            for sn in SMALL:
                res[sn] = [u[sn].reshape(wl[sn].shape) for u in unpacked]
        else:
            outs = _adamw("adamw_" + n, wd[n], md[n], vd[n], p, q)
            res[n] = [o.reshape(wl[n].shape) for o in outs]
    flat = [res[n][kind] for kind in range(4) for n in WEIGHTS]
    return (loss, grad_x[None], *flat)
```

```python
import functools

import jax
import jax.numpy as jnp
from jax import lax
from jax.experimental import pallas as pl
from jax.experimental.pallas import tpu as pltpu

F32 = jnp.float32
HI = lax.Precision.HIGHEST
_MXU_DTYPE = jnp.bfloat16
_SCAN_DTYPE = jnp.bfloat16
_GRAD_DTYPE = jnp.bfloat16

D_MODEL = 1024
EPS = 1e-6
CHUNK = 64
GRID_W = 64
GDN_HEADS = 8
GDN_DK = 128
SSM_HEADS = 32
SSM_PAIRS = 16
SSD_GROUP = 4
D_FF = 2816
D_IN_PROJ = 11360
N_PLANE = 4
P_QKV, P_ZG, P_GATE, P_ZS, P_XBC, P_SMALL, P_TOTAL = 0, 3072, 4096, 6144, 8192, 11264, 11776
IN_SEGMENTS = [(0, 3072, P_QKV), (3072, 4096, P_ZG), (4096, 4112, P_SMALL), (4112, 4128, P_SMALL + 16),
               (4128, 6176, P_ZS), (6176, 9248, P_XBC), (9248, 9312, P_SMALL + 32), (9312, 11360, P_GATE)]
ADAM_LR, ADAM_B1, ADAM_B2, ADAM_EPS, ADAM_WD, ADAM_STEP = 0.001, 0.9, 0.999, 1e-08, 0.01, 10

VMEM_LIMIT = 48 * 1024 * 1024

WEIGHTS = ['c_ctx', 'ada_w', 'ada_b', 'norm1_w', 'w_in', 'gdn_conv_w', 'gdn_conv_b', 'gdn_a_log', 'gdn_dt_bias',
           'gdn_norm_w', 'ssm_conv_w', 'ssm_conv_b', 'ssm_a_log', 'ssm_dt_bias', 'ssm_d', 'ssm_norm_w', 'w_br_gdn',
           'w_br_ssm', 'w_out', 'norm2_w', 'w_ffn_in', 'w_ffn_out', 'norm_f_w']
SHARD_AXIS = {'ada_w': 1, 'w_in': 1, 'gdn_conv_w': 1, 'ssm_conv_w': 1, 'w_br_gdn': 0, 'w_br_ssm': 0, 'w_out': 0,
              'w_ffn_in': 1, 'w_ffn_out': 0}
SHARDED = [n for n in WEIGHTS if n in SHARD_AXIS]
SMALL = [n for n in WEIGHTS if n not in SHARD_AXIS]
MXU_WEIGHTS = ['ada_w', 'w_in', 'w_br_gdn', 'w_br_ssm', 'w_out', 'w_ffn_in', 'w_ffn_out']


def _cp(sem):
    return pltpu.CompilerParams(dimension_semantics=sem, vmem_limit_bytes=VMEM_LIMIT)


def _pick(n, cands):
    for c in cands:
        if n % c == 0:
            return c
    raise ValueError(f"no tile for {n}")


HBM = pl.BlockSpec(memory_space=pl.ANY)
MESH = pl.DeviceIdType.MESH


def _plane_peers():
    x, y, c = lax.axis_index("x"), lax.axis_index("y"), lax.axis_index("c")
    return (x, y, c), [(1 - x, y, c), (x, 1 - y, c), (1 - x, 1 - y, c)]


class PlaneExchange:
    def __init__(self, arrs, gather):
        self.arrs, self.gather, self.n = list(arrs), gather, len(arrs)
        self.in_specs = [HBM] * self.n
        self.out_specs = [HBM] * self.n
        self.out_shape = [jax.ShapeDtypeStruct((N_PLANE,) + a.shape if gather else a.shape, a.dtype) for a in self.arrs]
        self.scratch = [pltpu.SemaphoreType.DMA((3 * self.n,)), pltpu.SemaphoreType.DMA((3 * self.n,)),
                        pltpu.SemaphoreType.DMA((self.n,))]

    def _copies(self, ins, outs, sems):
        send_sems, recv_sems, local_sems = sems
        (x, y, c), peers = _plane_peers()
        me = 2 * x + y
        copies = []
        for ti in range(self.n):
            src = ins[ti] if self.gather else ins[ti].at[me]
            copies.append(pltpu.make_async_copy(src, outs[ti].at[me], local_sems.at[ti]))
            for kk, (px, py, pc) in enumerate(peers):
                src = ins[ti] if self.gather else ins[ti].at[2 * px + py]
                copies.append(pltpu.make_async_remote_copy(
                    src_ref=src, dst_ref=outs[ti].at[me], send_sem=send_sems.at[3 * ti + kk],
                    recv_sem=recv_sems.at[3 * ti + kk], device_id=(px, py, pc), device_id_type=MESH))
        return copies

    def start(self, ins, outs, sems):
        for cp in self._copies(ins, outs, sems):
            cp.start()

    def wait(self, ins, outs, sems):
        for cp in self._copies(ins, outs, sems):
            cp.wait()


def _plane_exchange(name, arrs, gather):
    ex = PlaneExchange(arrs, gather)
    n = ex.n

    def body(*refs):
        ins, outs, sems = refs[:n], refs[n:2 * n], refs[2 * n:]
        ex.start(ins, outs, sems)
        ex.wait(ins, outs, sems)

    return pl.pallas_call(body, in_specs=ex.in_specs, out_specs=ex.out_specs, out_shape=ex.out_shape,
                          scratch_shapes=ex.scratch, name=name)(*arrs)


def _gather_two_level(name, arrs):
    n = len(arrs)
    split = [a.shape[0] % 32 == 0 for a in arrs]

    def body(*refs):
        ins, outs = refs[:n], refs[n:2 * n]
        ici_send, ici_recv, d2d_send, d2d_recv, local_sems = refs[2 * n:]
        (x, y, c), peers = _plane_peers()
        me = 2 * x + y

        def part(ref, ti):
            if not split[ti]:
                return ref
            rows = arrs[ti].shape[0] // 2
            return ref.at[pl.ds(c * rows, rows)]

        local = [pltpu.make_async_copy(ins[ti], outs[ti].at[me], local_sems.at[ti]) for ti in range(n)]
        ici = [pltpu.make_async_remote_copy(
            src_ref=part(ins[ti], ti), dst_ref=part(outs[ti].at[me], ti), send_sem=ici_send.at[3 * ti + kk],
            recv_sem=ici_recv.at[3 * ti + kk], device_id=peer, device_id_type=MESH)
            for ti in range(n) for kk, peer in enumerate(peers)]
        for cp in local + ici:
            cp.start()
        d2d = []
        for ti in range(n):
            for kk, (px, py, pc) in enumerate(peers):
                ici[3 * ti + kk].wait_recv()
                if split[ti]:
                    piece = part(outs[ti].at[2 * px + py], ti)
                    cp = pltpu.make_async_remote_copy(
                        src_ref=piece, dst_ref=piece, send_sem=d2d_send.at[3 * ti + kk],
                        recv_sem=d2d_recv.at[3 * ti + kk], device_id=(x, y, 1 - c), device_id_type=MESH)
                    cp.start()
                    d2d.append(cp)
        for cp in ici:
            cp.wait_send()
        for cp in d2d:
            cp.wait()
        for cp in local:
            cp.wait()

    return pl.pallas_call(
        body, in_specs=[HBM] * n, out_specs=[HBM] * n,
        out_shape=[jax.ShapeDtypeStruct((N_PLANE,) + a.shape, a.dtype) for a in arrs],
        scratch_shapes=[pltpu.SemaphoreType.DMA((3 * n,))] * 4 + [pltpu.SemaphoreType.DMA((n,))], name=name)(*arrs)


def _gather_all(name, arr):
    flips = [(fx, fy, fc) for fx in (0, 1) for fy in (0, 1) for fc in (0, 1)][1:]

    def body(a_ref, o_ref, send_sems, recv_sems, local_sem):
        x, y, c = lax.axis_index("x"), lax.axis_index("y"), lax.axis_index("c")
        me = 4 * x + 2 * y + c
        copies = [pltpu.make_async_copy(a_ref, o_ref.at[me], local_sem)]
        for kk, (fx, fy, fc) in enumerate(flips):
            peer = (1 - x if fx else x, 1 - y if fy else y, 1 - c if fc else c)
            copies.append(pltpu.make_async_remote_copy(
                src_ref=a_ref, dst_ref=o_ref.at[me], send_sem=send_sems.at[kk], recv_sem=recv_sems.at[kk],
                device_id=peer, device_id_type=MESH))
        for cp in copies:
            cp.start()
        for cp in copies:
            cp.wait()

    return pl.pallas_call(
        body, in_specs=[HBM], out_specs=HBM, out_shape=jax.ShapeDtypeStruct((8,) + arr.shape, arr.dtype),
        scratch_shapes=[pltpu.SemaphoreType.DMA((7,)), pltpu.SemaphoreType.DMA((7,)), pltpu.SemaphoreType.DMA],
        name=name)(arr)


def _swap_sibling(arrs):
    n = len(arrs)

    def body(*refs):
        ins, outs, send_sems, recv_sems = refs[:n], refs[n:2 * n], refs[2 * n], refs[2 * n + 1]
        x, y, c = lax.axis_index("x"), lax.axis_index("y"), lax.axis_index("c")
        copies = [pltpu.make_async_remote_copy(src_ref=ins[ti], dst_ref=outs[ti], send_sem=send_sems.at[ti],
                                               recv_sem=recv_sems.at[ti], device_id=(x, y, 1 - c), device_id_type=MESH)
                  for ti in range(n)]
        for cp in copies:
            cp.start()
        for cp in copies:
            cp.wait()

    return pl.pallas_call(
        body, in_specs=[HBM] * n, out_specs=[HBM] * n, out_shape=[jax.ShapeDtypeStruct(a.shape, a.dtype) for a in arrs],
        scratch_shapes=[pltpu.SemaphoreType.DMA((n,)), pltpu.SemaphoreType.DMA((n,))], name="swap_sibling")(*arrs)


MATMUL_VMEM_BUDGET = 36 * 1024 * 1024
TILE_CANDIDATES = (2944, 2816, 1408, 1024, 768, 512, 256, 128)
STEP_COST_BYTES = 1 << 20


def _matmul_tiles(m, n, k, ab, bb, ob, tn_fixed=None, tk_fixed=None):
    def cands(dim, whole_up_to):
        out = [c for c in TILE_CANDIDATES if dim % c == 0]
        if dim <= whole_up_to and dim not in out:
            out.append(dim)
        return out

    best = None
    for tm in cands(m, 3072):
        for tn in ([tn_fixed] if tn_fixed else cands(n, 3072)):
            for tk in ([tk_fixed] if tk_fixed else cands(k, 2048)):
                gi, gj, gk = m // tm, n // tn, k // tk
                vmem = 2 * (tm * tk * ab + tk * tn * bb + tm * tn * ob) + (tm * tn * 4 if gk > 1 else 0)
                if vmem > MATMUL_VMEM_BUDGET:
                    continue
                a_reads = 1 if gk == 1 else gj
                b_reads = 1 if (gk == 1 and gj == 1) else gi
                cost = (m * k * ab * a_reads + k * n * bb * b_reads + m * n * ob + gi * gj * gk * STEP_COST_BYTES)
                if best is None or cost < best[0]:
                    best = (cost, tm, tn, tk)
    assert best is not None, (m, n, k)
    return best[1:]


def _matmul(a, b, form, name, out_dtype=F32, stacked_out=False, ride=None, halves=None):
    dims = {'nn': (((1,), (0,)), ((), ())), 'nt': (((1,), (1,)), ((), ())), 'tn': (((0,), (0,)), ((), ()))}[form]
    stacked_b = b.ndim == 3 and halves != 'b'
    ns = b.shape[2] if stacked_b else None
    if form == 'nn':
        m, k = a.shape
        n = b.shape[0] * ns if stacked_b else b.shape[1]
    elif form == 'nt':
        m, k = (a.shape[1], 2 * a.shape[2]) if halves == 'a' else a.shape
        n = b.shape[1] if stacked_b else b.shape[0]
    else:
        k, m = a.shape
        n = 2 * b.shape[2] if halves == 'b' else b.shape[1]
    ob = jnp.dtype(out_dtype).itemsize
    tm, tn, tk = _matmul_tiles(m, n, k, a.dtype.itemsize, b.dtype.itemsize, ob,
                               tn_fixed=(n // N_PLANE if stacked_out else ns if (stacked_b and form == 'nn') else None),
                               tk_fixed=(ns if (stacked_b and form == 'nt') else None))
    nk = k // tk
    grid = (m // tm, n // tn, nk)
    n_ride = ride.n if ride is not None else 0

    def body(*refs):
        a_ref, b_ref = refs[0], refs[1]
        ride_in = refs[2:2 + n_ride]
        o_ref = refs[2 + n_ride]
        ride_out = refs[3 + n_ride:3 + 2 * n_ride]
        acc_ref = refs[3 + 2 * n_ride]
        sems = refs[4 + 2 * n_ride:]
        i, j, kk = pl.program_id(0), pl.program_id(1), pl.program_id(2)
        if ride is not None:
            @pl.when((i == 0) & (j == 0) & (kk == 0))
            def _():
                ride.start(ride_in, ride_out, sems)

        def put(val):
            if stacked_out or halves == 'out':
                o_ref[0] = val.astype(o_ref.dtype)
            else:
                o_ref[...] = val.astype(o_ref.dtype)

        av = a_ref[0] if halves == 'a' else a_ref[...]
        bv = b_ref[0] if (stacked_b or halves == 'b') else b_ref[...]
        part = lax.dot_general(av.astype(_MXU_DTYPE), bv.astype(_MXU_DTYPE), dims, preferred_element_type=F32)
        if nk == 1:
            put(part)
        else:
            @pl.when(kk == 0)
            def _():
                acc_ref[...] = part

            @pl.when(kk > 0)
            def _():
                acc_ref[...] += part

            @pl.when(kk == nk - 1)
            def _():
                put(acc_ref[...])

        if ride is not None:
            @pl.when((i == grid[0] - 1) & (j == grid[1] - 1) & (kk == nk - 1))
            def _():
                ride.wait(ride_in, ride_out, sems)

    if form == 'nn':
        a_spec = pl.BlockSpec((tm, tk), lambda i, j, kk: (i, kk))
        b_spec = (pl.BlockSpec((1, tk, tn), lambda i, j, kk: (j, kk, 0)) if stacked_b
                  else pl.BlockSpec((tk, tn), lambda i, j, kk: (kk, j)))
    elif form == 'nt':
        a_spec = pl.BlockSpec((tm, tk), lambda i, j, kk: (i, kk))
        b_spec = (pl.BlockSpec((1, tn, tk), lambda i, j, kk: (kk, j, 0)) if stacked_b
                  else pl.BlockSpec((tn, tk), lambda i, j, kk: (j, kk)))
    else:
        a_spec = pl.BlockSpec((tk, tm), lambda i, j, kk: (kk, i))
        b_spec = pl.BlockSpec((tk, tn), lambda i, j, kk: (kk, j))
    hj, hk = grid[1] // 2, nk // 2
    if halves == 'a':
        a_spec = pl.BlockSpec((1, tm, tk), lambda i, j, kk: (kk // hk, i, kk % hk))
    if halves == 'b':
        b_spec = pl.BlockSpec((1, tk, tn), lambda i, j, kk: (j // hj, kk, j % hj))
    if stacked_out:
        o_spec = pl.BlockSpec((1, tm, tn), lambda i, j, kk: (j, i, 0))
        o_shape = jax.ShapeDtypeStruct((N_PLANE, m, tn), out_dtype)
    elif halves == 'out':
        o_spec = pl.BlockSpec((1, tm, tn), lambda i, j, kk: (j // hj, i, j % hj))
        o_shape = jax.ShapeDtypeStruct((2, m, n // 2), out_dtype)
    else:
        o_spec = pl.BlockSpec((tm, tn), lambda i, j, kk: (i, j))
        o_shape = jax.ShapeDtypeStruct((m, n), out_dtype)
    acc_shape = (tm, tn) if nk > 1 else (8, 128)
    if ride is None:
        return pl.pallas_call(
            body, grid=grid, in_specs=[a_spec, b_spec], out_specs=o_spec, out_shape=o_shape,
            scratch_shapes=[pltpu.VMEM(acc_shape, F32)],
            compiler_params=_cp(("parallel", "parallel", "arbitrary")), name=name)(a, b)
    res = pl.pallas_call(
        body, grid=grid, in_specs=[a_spec, b_spec] + ride.in_specs, out_specs=[o_spec] + ride.out_specs,
        out_shape=[o_shape] + ride.out_shape, scratch_shapes=[pltpu.VMEM(acc_shape, F32)] + ride.scratch,
        compiler_params=_cp(("arbitrary", "arbitrary", "arbitrary")), name=name)(a, b, *ride.arrs)
    return res[0], res[1:]


class Op:
    def __init__(self, arr, bs, im, load=None):
        self.arr, self.bs, self.im = arr, bs, im
        self.arrs = list(arr) if isinstance(arr, list) else [arr]
        self.load = load or (lambda r: r[...].astype(F32))

    def spec(self):
        return pl.BlockSpec(self.bs, self.im)

    def value(self, it):
        return functools.reduce(lambda u, w: u + w, [self.load(next(it)) for _ in self.arrs])


def _op_specs(ops):
    return [op.spec() for op in ops for _ in op.arrs]


def _op_arrays(ops):
    return [a for op in ops for a in op.arrs]


def _sum_dirs(r):
    return r[0].astype(F32) + r[1].astype(F32)


def _tw_fwd(name, fn, grid, ins, outs):
    def body(*refs):
        info = (pl.program_id(0), pl.program_id(1))
        it = iter(refs)
        vals = [op.value(it) for op in ins]
        res = fn(info, *vals)
        for r, v in zip(it, res):
            r[...] = v.astype(r.dtype)

    return pl.pallas_call(
        body, grid=grid, in_specs=_op_specs(ins), out_specs=[op.spec() for op in outs],
        out_shape=[jax.ShapeDtypeStruct(*op.arr) for op in outs],
        compiler_params=_cp(("parallel", "arbitrary")), name=name)(*_op_arrays(ins))


def _tw_bwd(name, fn, grid, tok, par, cots, tok_out, par_out, tok_add=None, sem=("parallel", "arbitrary"), into=None):
    n_tok = len(tok)
    flat_cots = [op for group in cots for op in group]
    extra = [tok_add] if tok_add is not None else []
    out_ops = list(tok_out) + list(par_out)
    into = into or {}

    def body(*refs):
        info = (pl.program_id(0), pl.program_id(1))
        it = iter(refs)
        tok_v = [op.value(it) for op in tok]
        par_v = [op.value(it) for op in par]
        cot_v = [functools.reduce(lambda u, w: u + w, [op.value(it) for op in group]) for group in cots]
        add_v = [op.value(it) for op in extra]
        for _ in into:
            next(it)
        _, pull = jax.vjp(lambda *a: fn(info, *a), *tok_v, *par_v)
        grads = pull(tuple(cot_v))
        for i in range(n_tok):
            r = next(it)
            g = grads[i] + add_v[0] if (i == 0 and add_v) else grads[i]
            if r.shape[-1] > g.shape[-1]:
                g = jnp.concatenate([g, jnp.zeros((g.shape[0], r.shape[-1] - g.shape[-1]), g.dtype)], axis=1)
            r[...] = g.astype(r.dtype)
        first = pl.program_id(1) == 0
        for i in range(len(par)):
            r = next(it)
            g = grads[n_tok + i]

            @pl.when(first)
            def _(r=r, g=g):
                r[...] = g

            @pl.when(jnp.logical_not(first))
            def _(r=r, g=g):
                r[...] += g

    ops = tok + par + flat_cots + extra
    n_in = len(_op_arrays(ops))
    return pl.pallas_call(
        body, grid=grid, in_specs=_op_specs(ops) + [HBM] * len(into), out_specs=[op.spec() for op in out_ops],
        out_shape=[jax.ShapeDtypeStruct(*op.arr) for op in out_ops],
        input_output_aliases={n_in + k: i for k, i in enumerate(into)},
        compiler_params=_cp(sem), name=name)(*_op_arrays(ops), *into.values())


def _silu(x):
    return x * jax.nn.sigmoid(x)


def _rms(x):
    return x * lax.rsqrt(jnp.mean(x * x, axis=-1, keepdims=True) + EPS)


@functools.partial(jax.custom_vjp, nondiff_argnums=(1,))
def _shift_rows(x, k):
    return pltpu.roll(x, k % x.shape[0], 0)


def _shift_rows_fwd(x, k):
    return _shift_rows(x, k), None


def _shift_rows_bwd(k, _, g):
    return (_shift_rows(g, -k),)


_shift_rows.defvjp(_shift_rows_fwd, _shift_rows_bwd)


def _prenorm_fn(nctx_t, info, x, w, sc8, sh8):
    is_ctx = info[1] < nctx_t
    sc = jnp.where(is_ctx, sc8[1:2], sc8[0:1])
    sh = jnp.where(is_ctx, sh8[1:2], sh8[0:1])
    return (_rms(x) * w * (1.0 + sc) + sh,)


def _conv_fn(nctx_t, mode, info, x, w, b):
    n, c = x.shape
    is_ctx = info[1] < nctx_t
    idx = lax.broadcasted_iota(jnp.int32, (n, 1), 0)
    rr = jnp.where(is_ctx, idx, idx % GRID_W)
    first = rr == 0
    last = rr == jnp.where(is_ctx, n - 1, GRID_W - 1)
    prev = jnp.where(first, 0.0, _shift_rows(x, 1))
    nxt = jnp.where(last, 0.0, _shift_rows(x, -1))
    y = b + prev * w[0:1] + x * w[1:2] + nxt * w[2:3]
    y = _silu(y)
    if mode == 'none':
        return (y,)
    scale = GDN_DK ** -0.5 if mode == 'q' else 1.0
    outs = []
    for h in range(c // 128):
        yh = y[:, h * 128:(h + 1) * 128]
        outs.append(yh * lax.rsqrt(jnp.sum(yh * yh, axis=-1, keepdims=True) + EPS) * scale)
    return (jnp.concatenate(outs, axis=1),)


def _act_fn(info, x, p0, p1):
    lane = lax.broadcasted_iota(jnp.int32, x.shape, 1)
    sp = jax.nn.softplus(x + p1)
    g = -jnp.exp(p0) * sp
    bt = jax.nn.sigmoid(x)
    return (jnp.where(lane < 16, g, jnp.where(lane < 32, bt, jnp.where(lane < 96, sp, 0.0))),)


def _mixg_fn(info, o, zg, gnw):
    outs = []
    for h in range(GDN_HEADS):
        outs.append(_rms(o[:, h * 128:(h + 1) * 128]) * gnw)
    return (jnp.concatenate(outs, axis=1) * _silu(zg),)


def _mixs_fn(info, y, xs, zs, dl, snw):
    yy = (y + dl * xs) * _silu(zs)
    outs = []
    for g in range(4):
        outs.append(_rms(yy[:, g * 512:(g + 1) * 512]))
    return (jnp.concatenate(outs, axis=1) * snw,)


def _merge_fn(info, gates, pg, ps):
    half = gates.shape[1] // 2
    return (jax.nn.sigmoid(gates[:, :half]) * pg + jax.nn.sigmoid(gates[:, half:]) * ps,)


def _norm2_fn(info, xt, mo, g8, w, sc8, sh8):
    h1 = xt + g8[0:1] * mo
    return (h1, _rms(h1) * w * (1.0 + sc8[0:1]) + sh8[0:1])


def _swiglu_fn(info, u2):
    return (_silu(u2[0]) * u2[1],)


_NN = (((1,), (0,)), ((), ()))
_NT = (((1,), (1,)), ((), ()))
_TN = (((0,), (0,)), ((), ()))


def _mmh(a, b):
    return lax.dot_general(a, b, _NN, precision=HI, preferred_element_type=F32)


def _dot1(a, b, dims):
    return lax.dot_general(a.astype(_SCAN_DTYPE), b.astype(_SCAN_DTYPE), dims, preferred_element_type=F32)


def _mm(a, b):
    return _dot1(a, b, _NN)


def _mm_nt(a, b):
    return _dot1(a, b, _NT)


def _mm_tn(a, b):
    return _dot1(a, b, _TN)


def _split2(a):
    hi = a.astype(_SCAN_DTYPE)
    return hi, (a - hi.astype(F32)).astype(_SCAN_DTYPE)


def _dot3(a, b, dims):
    ah, al = _split2(a)
    bh, bl = _split2(b)
    d = lambda u, w: lax.dot_general(u, w, dims, preferred_element_type=F32)
    return d(ah, bh) + (d(ah, bl) + d(al, bh))


def _order_masks(d):
    i = lax.broadcasted_iota(jnp.int32, (CHUNK, CHUNK), 0)
    j = lax.broadcasted_iota(jnp.int32, (CHUNK, CHUNK), 1)
    s = jnp.where(d == 0, 1, -1) * (i - j)
    return (s >= 0).astype(F32), (s > 0).astype(F32)


@jax.custom_vjp
def _unit_tri_inv(mats):
    i = lax.broadcasted_iota(jnp.int32, (CHUNK, CHUNK), 0)
    j = lax.broadcasted_iota(jnp.int32, (CHUNK, CHUNK), 1)
    eye = (i == j).astype(F32)
    ps = [-a for a in mats]
    ts = [eye + p for p in ps]
    for _ in range(5):
        ps = [_dot3(p, p, _NN) for p in ps]
        ts = [t + _dot3(t, p, _NN) for t, p in zip(ts, ps)]
    return tuple(ts)


def _uti_fwd(mats):
    ts = _unit_tri_inv(mats)
    return ts, ts


def _uti_bwd(ts, gs):
    inner = [_dot3(g, t, _NT) for g, t in zip(gs, ts)]
    return (tuple(-_dot3(t, u, _TN) for t, u in zip(ts, inner)),)


_unit_tri_inv.defvjp(_uti_fwd, _uti_bwd)


@jax.custom_vjp
def _unit_tri_inv_given(mats, ts):
    return ts


def _utig_fwd(mats, ts):
    return ts, ts


def _utig_bwd(ts, gs):
    return _uti_bwd(ts, gs)[0], tuple(jnp.zeros_like(t) for t in ts)


_unit_tri_inv_given.defvjp(_utig_fwd, _utig_bwd)


def _lane_col(blk, lane_idx):
    lane = lax.broadcasted_iota(jnp.int32, blk.shape, 1)
    return jnp.sum(jnp.where(lane == lane_idx, blk, 0.0), axis=1, keepdims=True)


def _decay_mat(cum, incl):
    cb = jnp.broadcast_to(cum, (CHUNK, CHUNK))
    return jnp.exp(jnp.minimum(cb - cb.T, 0.0)) * incl


def _gdn_chunks(dirs, streams, states, aux=None, want_aux=False, group=None):
    ns = len(dirs)
    masks = [_order_masks(d) for d in dirs]
    cum = [_mmh(masks[i][0], streams[i][3]) for i in range(ns)]
    tot = [jnp.sum(streams[i][3], axis=0, keepdims=True) for i in range(ns)]
    all_units = [(i, h) for i in range(ns) for h in range(GDN_HEADS)]
    group = group or len(all_units)
    cat = jnp.concatenate
    outs, new_states, ts_all = [], [], []
    for g0 in range(0, len(all_units), group):
        units = all_units[g0:g0 + group]
        us = range(len(units))
        st = states[g0:g0 + group]
        qs = [streams[i][0][:, h * 128:(h + 1) * 128] for i, h in units]
        ks = [streams[i][1][:, h * 128:(h + 1) * 128] for i, h in units]
        vs = [streams[i][2][:, h * 128:(h + 1) * 128] for i, h in units]
        gcum = [_lane_col(cum[i], dirs[i] * GDN_HEADS + h) for i, h in units]
        glast = [_lane_col(tot[i], dirs[i] * GDN_HEADS + h) for i, h in units]
        beta = [_lane_col(streams[i][3], 16 + dirs[i] * GDN_HEADS + h) for i, h in units]
        decay = [_decay_mat(gcum[u], masks[units[u][0]][0]) for u in us]
        egc = [jnp.exp(gcum[u]) for u in us]
        kb = [ks[u] * beta[u] for u in us]
        kq = [_mm_nt(cat([kb[u], qs[u]], axis=0), ks[u]) for u in us]
        mats = tuple(kq[u][:CHUNK] * decay[u] * masks[units[u][0]][1] for u in us)
        ts = _unit_tri_inv(mats) if aux is None else _unit_tri_inv_given(mats, tuple(aux[g0:g0 + group]))
        wu = [_mm(ts[u], cat([kb[u] * egc[u], vs[u] * beta[u]], axis=1)) for u in us]
        ws = [_mm(cat([wu[u][:, :128], qs[u] * egc[u]], axis=0), st[u]) for u in us]
        vn = [wu[u][:, 128:] - ws[u][:CHUNK] for u in us]
        outs += [ws[u][CHUNK:] + _mm(kq[u][CHUNK:] * decay[u], vn[u]) for u in us]
        new_states += [st[u] * jnp.exp(glast[u]) + _mm_tn(ks[u] * jnp.exp(glast[u] - gcum[u]), vn[u]) for u in us]
        ts_all += list(ts)
    per_stream = [cat(outs[i * GDN_HEADS:(i + 1) * GDN_HEADS], axis=1) for i in range(ns)]
    return (*per_stream, *new_states, *(ts_all if want_aux else ()))


def _gdn_step(d, q, k, v, spb, *states, aux=None, want_aux=False):
    return _gdn_chunks([d], [(q, k, v, spb)], states, aux, want_aux, group=4)


def _gdn_step2(q0, k0, v0, sp0, q1, k1, v1, sp1, *states, aux=None, want_aux=False):
    return _gdn_chunks([0, 1], [(q0, k0, v0, sp0), (q1, k1, v1, sp1)], states, aux, want_aux)


def _split3(a):
    a1 = a.astype(_SCAN_DTYPE)
    r = a - a1.astype(F32)
    a2 = r.astype(_SCAN_DTYPE)
    return a1, a2, (r - a2.astype(F32)).astype(_SCAN_DTYPE)


def _exact_dot(a, e, dims, split_lhs, passes=3):
    parts = _split3(a if split_lhs else e)[:passes]
    d = (lambda u: lax.dot_general(u, e, dims, preferred_element_type=F32)) if split_lhs else \
        (lambda u: lax.dot_general(a, u, dims, preferred_element_type=F32))
    return functools.reduce(lambda u, w: u + w, [d(p) for p in reversed(parts)])


@jax.custom_vjp
def _spread(a, e):
    return _exact_dot(a, e, _NN, True)


def _spread_fwd(a, e):
    return _spread(a, e), e


def _spread_bwd(e, g):
    return _exact_dot(g, e, _NT, True, passes=2), jnp.zeros_like(e)


_spread.defvjp(_spread_fwd, _spread_bwd)


@jax.custom_vjp
def _colsum_bcast(z):
    return _exact_dot(jnp.ones((z.shape[0], z.shape[0]), _SCAN_DTYPE), z, _NN, False)


def _colsum_fwd(z):
    return _colsum_bcast(z), None


def _colsum_bwd(_, g):
    return (_exact_dot(jnp.ones((g.shape[0], g.shape[0]), _SCAN_DTYPE), g, _NN, False, passes=2),)


_colsum_bcast.defvjp(_colsum_fwd, _colsum_bwd)


def _ssd_consts():
    wdt = SSM_HEADS * 64
    d = lax.broadcasted_iota(jnp.int32, (2, 128, wdt), 0)
    e = (lax.broadcasted_iota(jnp.int32, (2, 128, wdt), 1)
         == 32 + d * SSM_HEADS + lax.broadcasted_iota(jnp.int32, (2, 128, wdt), 2) // 64).astype(_SCAN_DTYPE)
    dd = lax.broadcasted_iota(jnp.int32, (2, CHUNK, wdt), 0)
    ci = lax.broadcasted_iota(jnp.int32, (2, CHUNK, wdt), 1)
    pos = lax.broadcasted_iota(jnp.int32, (2, CHUNK, wdt), 2) % 64
    incl_t = (jnp.where(dd == 0, 1, -1) * (ci - pos) >= 0).astype(F32)
    diag = (ci == pos).astype(F32)
    return [e, incl_t, diag]


def _ssd_step(d, x, bm, cm, spb, alog, *states, consts):
    e, incl_t, diag = consts
    incl, _ = _order_masks(d)
    lane1 = lax.broadcasted_iota(jnp.int32, (1, 128), 1)
    lo_lane = 32 + d * SSM_HEADS
    a_vec = jnp.where(lane1 >= lo_lane, jnp.where(lane1 < lo_lane + SSM_HEADS, -jnp.exp(alog), 0.0), 0.0)
    adt = spb * a_vec
    acum = _mmh(incl, adt)
    alast = jnp.sum(adt, axis=0, keepdims=True)
    dt2 = _spread(spb, e)
    ac2 = _spread(acum, e)
    al2 = _spread(jnp.broadcast_to(alast, (8, 128)), e)[0:1]
    row = _colsum_bcast(ac2 * diag)
    seg = jnp.exp(jnp.minimum(ac2 - row, 0.0)) * incl_t
    xdt = x * dt2
    gam = jnp.exp(ac2)
    xe = xdt * jnp.exp(al2 - ac2)
    low = lax.broadcasted_iota(jnp.int32, (CHUNK, 128), 1) < 64
    row_low = lax.broadcasted_iota(jnp.int32, (128, 1), 0) < 64
    ps = range(SSM_PAIRS)
    sl = [slice(p * 128, (p + 1) * 128) for p in ps]
    bg = [bm[:, g * 128:(g + 1) * 128] for g in range(4)]
    cg = [cm[:, g * 128:(g + 1) * 128] for g in range(4)]
    cb2 = [_mm_nt(cg[g], jnp.concatenate([bg[g], bg[g]], axis=0)) for g in range(4)]
    ys, new_states = [], []
    for p0 in range(0, SSM_PAIRS, SSD_GROUP):
        pg = range(p0, p0 + SSD_GROUP)
        xd = {p: jnp.concatenate([jnp.where(low, xdt[:, sl[p]], 0.0), jnp.where(low, 0.0, xdt[:, sl[p]])], axis=0)
              for p in pg}
        yd = {p: _mm(cb2[p // 4] * seg[:, sl[p]], xd[p]) for p in pg}
        yo = {p: _mm_nt(cg[p // 4], states[p]) for p in pg}
        ys += [yd[p] + gam[:, sl[p]] * yo[p] for p in pg]
        new = {p: _mm_tn(xe[:, sl[p]], bg[p // 4]) for p in pg}
        al0 = {p: _lane_col(alast, lo_lane + 2 * p) for p in pg}
        al1 = {p: _lane_col(alast, lo_lane + 2 * p + 1) for p in pg}
        new_states += [states[p] * jnp.exp(jnp.where(row_low, al0[p], al1[p])) + new[p] for p in pg]
    return (jnp.concatenate(ys, axis=1), *new_states)


def _chunk_of(d, p, nctx, nc):
    return jnp.where(d == 0, p, jnp.where(p < nctx, nctx - 1 - p, nctx + nc - 1 - p))


class _NoRide:
    n, arrs, in_specs, out_specs, out_shape, scratch = 0, [], [], [], [], []


def _const_specs(consts):
    return [pl.BlockSpec((1,) + a.shape[1:], lambda d, p: (d,) + (0,) * (a.ndim - 1)) for a in consts]


def _scan_fwd(name, step, toks, pars, consts, out_width, n_state, nctx):
    t = toks[0].shape[0]
    nc = t // CHUNK
    n_tok, n_par, n_const = len(toks), len(pars), len(consts)

    def body(*refs):
        it = iter(refs)
        tok_refs = [next(it) for _ in range(n_tok)]
        par_refs = [next(it) for _ in range(n_par)]
        const_refs = [next(it) for _ in range(n_const)]
        o_ref, ss_ref, s_scr = next(it), next(it), next(it)
        d, p = pl.program_id(0), pl.program_id(1)

        @pl.when(p == 0)
        def _():
            s_scr[...] = jnp.zeros(s_scr.shape, F32)

        ss_ref[0, 0] = s_scr[...]
        res = step(d, *[r[...] for r in tok_refs], *[r[...] for r in par_refs], *[s_scr[h] for h in range(n_state)],
                   consts=[r[0] for r in const_refs])
        o_ref[0] = res[0]
        for h in range(n_state):
            s_scr[h] = res[1 + h]

    ch = lambda d, p: _chunk_of(d, p, nctx, nc)
    in_specs = [pl.BlockSpec((CHUNK, a.shape[1]), lambda d, p: (ch(d, p), 0)) for a in toks]
    in_specs += [pl.BlockSpec(a.shape, lambda d, p: (0, 0)) for a in pars]
    return pl.pallas_call(
        body, grid=(2, nc), in_specs=in_specs + _const_specs(consts),
        out_specs=[pl.BlockSpec((1, CHUNK, out_width), lambda d, p: (d, ch(d, p), 0)),
                   pl.BlockSpec((1, 1, n_state, 128, 128), lambda d, p: (d, p, 0, 0, 0))],
        out_shape=[jax.ShapeDtypeStruct((2, t, out_width), F32),
                   jax.ShapeDtypeStruct((2, nc, n_state, 128, 128), F32)],
        scratch_shapes=[pltpu.VMEM((n_state, 128, 128), F32)],
        compiler_params=_cp(("arbitrary", "arbitrary")), name=name)(*toks, *pars, *consts)


def _scan_bwd(name, step, toks, pars, ss, dout, n_state, nctx, ride=None, aux=None, consts=()):
    t = toks[0].shape[0]
    nc = t // CHUNK
    n_tok, n_par, n_const = len(toks), len(pars), len(consts)
    rd = ride if ride is not None else _NoRide
    n_aux = aux.shape[2] if aux is not None else 0

    def body(*refs):
        it = iter(refs)
        tok_refs = [next(it) for _ in range(n_tok)]
        par_refs = [next(it) for _ in range(n_par)]
        ss_ref, do_ref = next(it), next(it)
        aux_ref = next(it) if n_aux else None
        const_refs = [next(it) for _ in range(n_const)]
        ride_in = [next(it) for _ in range(rd.n)]
        dtok_refs = [next(it) for _ in range(n_tok)]
        dpar_refs = [next(it) for _ in range(n_par)]
        ride_out = [next(it) for _ in range(rd.n)]
        ds_scr = next(it)
        sems = list(it)
        d, pr = pl.program_id(0), pl.program_id(1)
        if ride is not None:
            @pl.when((d == 0) & (pr == 0))
            def _():
                ride.start(ride_in, ride_out, sems)

            @pl.when((d == 1) & (pr == nc - 1))
            def _():
                ride.wait(ride_in, ride_out, sems)

        @pl.when(pr == 0)
        def _():
            ds_scr[...] = jnp.zeros(ds_scr.shape, F32)

        kw = dict(aux=[aux_ref[0, 0, i] for i in range(n_aux)]) if n_aux else {}
        if n_const:
            kw['consts'] = [r[0] for r in const_refs]
        _, pull = jax.vjp(functools.partial(step, d, **kw), *[r[...] for r in tok_refs], *[r[...] for r in par_refs],
                          *[ss_ref[0, 0, h] for h in range(n_state)])
        grads = pull((do_ref[...], *[ds_scr[h] for h in range(n_state)]))
        for r, g in zip(dtok_refs, grads[:n_tok]):
            r[0] = g
        for h in range(n_state):
            ds_scr[h] = grads[n_tok + n_par + h]
        first = (d == 0) & (pr == 0)
        for r, g in zip(dpar_refs, grads[n_tok:n_tok + n_par]):
            @pl.when(first)
            def _(r=r, g=g):
                r[...] = g

            @pl.when(jnp.logical_not(first))
            def _(r=r, g=g):
                r[...] += g

    ch = lambda d, pr: _chunk_of(d, nc - 1 - pr, nctx, nc)
    in_specs = [pl.BlockSpec((CHUNK, a.shape[1]), lambda d, pr: (ch(d, pr), 0)) for a in toks]
    in_specs += [pl.BlockSpec(a.shape, lambda d, pr: (0, 0)) for a in pars]
    in_specs += [pl.BlockSpec((1, 1, n_state, 128, 128), lambda d, pr: (d, nc - 1 - pr, 0, 0, 0)),
                 pl.BlockSpec((CHUNK, dout.shape[1]), lambda d, pr: (ch(d, pr), 0))]
    if n_aux:
        in_specs += [pl.BlockSpec((1, 1, n_aux, CHUNK, CHUNK), lambda d, pr: (d, nc - 1 - pr, 0, 0, 0))]
    in_specs += _const_specs(consts)
    out_specs = [pl.BlockSpec((1, CHUNK, a.shape[1]), lambda d, pr: (d, ch(d, pr), 0)) for a in toks]
    out_specs += [pl.BlockSpec(a.shape, lambda d, pr: (0, 0)) for a in pars]
    out_shape = [jax.ShapeDtypeStruct((2, t, a.shape[1]), F32) for a in toks]
    out_shape += [jax.ShapeDtypeStruct(a.shape, F32) for a in pars]
    return pl.pallas_call(
        body, grid=(2, nc), in_specs=in_specs + rd.in_specs, out_specs=out_specs + rd.out_specs,
        out_shape=out_shape + rd.out_shape, scratch_shapes=[pltpu.VMEM((n_state, 128, 128), F32)] + rd.scratch,
        compiler_params=_cp(("arbitrary", "arbitrary")), name=name)(
            *toks, *pars, ss, dout, *([aux] if n_aux else []), *consts, *rd.arrs)


def _scan2_fwd(name, step2, toks, out_width, n_state, nctx, n_aux, ride=None):
    t = toks[0].shape[0]
    nc = t // CHUNK
    n_tok = len(toks)
    rd = ride if ride is not None else _NoRide

    def body(*refs):
        it = iter(refs)
        tok_refs = [next(it) for _ in range(2 * n_tok)]
        ride_in = [next(it) for _ in range(rd.n)]
        o_refs = [next(it), next(it)]
        ss_ref, aux_ref = next(it), next(it)
        ride_out = [next(it) for _ in range(rd.n)]
        s_scr = next(it)
        sems = list(it)
        p = pl.program_id(0)
        if ride is not None:
            @pl.when(p == 0)
            def _():
                ride.start(ride_in, ride_out, sems)

        @pl.when(p == 0)
        def _():
            s_scr[...] = jnp.zeros(s_scr.shape, F32)

        for d in range(2):
            ss_ref[d, 0] = s_scr[d * n_state:(d + 1) * n_state]
        res = step2(*[r[...] for r in tok_refs], *[s_scr[u] for u in range(2 * n_state)], want_aux=True)
        for d in range(2):
            o_refs[d][...] = res[d]
            for i in range(n_aux):
                aux_ref[d, 0, i] = res[2 + 2 * n_state + d * n_aux + i]
        for u in range(2 * n_state):
            s_scr[u] = res[2 + u]
        if ride is not None:
            @pl.when(p == nc - 1)
            def _():
                ride.wait(ride_in, ride_out, sems)

    def tok_spec(a, d):
        return pl.BlockSpec((CHUNK, a.shape[1]), lambda p: (_chunk_of(d, p, nctx, nc), 0))

    return pl.pallas_call(
        body, grid=(nc,), in_specs=[tok_spec(a, d) for d in range(2) for a in toks] + rd.in_specs,
        out_specs=[pl.BlockSpec((CHUNK, out_width), lambda p: (_chunk_of(0, p, nctx, nc), 0)),
                   pl.BlockSpec((CHUNK, out_width), lambda p: (_chunk_of(1, p, nctx, nc), 0)),
                   pl.BlockSpec((2, 1, n_state, 128, 128), lambda p: (0, p, 0, 0, 0)),
                   pl.BlockSpec((2, 1, n_aux, CHUNK, CHUNK), lambda p: (0, p, 0, 0, 0))] + rd.out_specs,
        out_shape=[jax.ShapeDtypeStruct((t, out_width), F32), jax.ShapeDtypeStruct((t, out_width), F32),
                   jax.ShapeDtypeStruct((2, nc, n_state, 128, 128), F32),
                   jax.ShapeDtypeStruct((2, nc, n_aux, CHUNK, CHUNK), F32)] + rd.out_shape,
        scratch_shapes=[pltpu.VMEM((2 * n_state, 128, 128), F32)] + rd.scratch,
        compiler_params=_cp(("arbitrary",)), name=name)(*toks, *toks, *rd.arrs)


ADA_TN = 512


def _ada_part(cc16, w_shard):
    n = w_shard.shape[1]

    def body(c_ref, w_ref, o_ref):
        s = _silu(c_ref[...]).astype(_MXU_DTYPE)
        o_ref[...] = jnp.dot(s, w_ref[...].astype(_MXU_DTYPE), preferred_element_type=F32)

    return pl.pallas_call(
        body, grid=(n // ADA_TN,),
        in_specs=[pl.BlockSpec((16, D_MODEL), lambda j: (0, 0)), pl.BlockSpec((D_MODEL, ADA_TN), lambda j: (0, j))],
        out_specs=pl.BlockSpec((16, ADA_TN), lambda j: (0, j)), out_shape=jax.ShapeDtypeStruct((16, n), F32),
        compiler_params=_cp(("parallel",)), name="ada_part")(cc16, w_shard)


def _ada_bwd_shard(cc16, w_shard, d_lat, d_ctx):
    n = w_shard.shape[1]
    nj = n // ADA_TN

    def body(c_ref, w_ref, gl_ref, gc_ref, dw_ref, dc_ref):
        j = pl.program_id(0)
        s, pull = jax.vjp(_silu, c_ref[...])
        row = lax.broadcasted_iota(jnp.int32, (8, 1), 0)
        dctx = jnp.where(row == 0, jnp.sum(gc_ref[...], axis=0, keepdims=True), 0.0)
        mxu = lambda v: v.astype(_MXU_DTYPE)
        dw_ref[...] = (lax.dot_general(mxu(s[0:8]), mxu(gl_ref[...]), _TN, preferred_element_type=F32)
                       + lax.dot_general(mxu(s[8:16]), mxu(dctx), _TN, preferred_element_type=F32))
        ds = lax.dot_general(mxu(dctx), mxu(w_ref[...]), _NT, preferred_element_type=F32)

        @pl.when(j == 0)
        def _():
            dc_ref[...] = ds

        @pl.when(j > 0)
        def _():
            dc_ref[...] += ds

        @pl.when(j == nj - 1)
        def _():
            ct = jnp.concatenate([jnp.zeros((8, D_MODEL), F32), dc_ref[...]], axis=0)
            dc_ref[...] = 0.5 * pull(ct)[0][8:16]

    tile = pl.BlockSpec((8, ADA_TN), lambda j: (0, j))
    wspec = pl.BlockSpec((D_MODEL, ADA_TN), lambda j: (0, j))
    return pl.pallas_call(
        body, grid=(nj,), in_specs=[pl.BlockSpec((16, D_MODEL), lambda j: (0, 0)), wspec, tile, tile],
        out_specs=[wspec, pl.BlockSpec((8, D_MODEL), lambda j: (0, 0))],
        out_shape=[jax.ShapeDtypeStruct((D_MODEL, n), F32), jax.ShapeDtypeStruct((8, D_MODEL), F32)],
        compiler_params=_cp(("arbitrary",)), name="ada_bwd_shard")(cc16, w_shard, d_lat, d_ctx)


def _rowsum2(a, b):
    def body(a_ref, b_ref, o_ref):
        o_ref[...] = jnp.sum(a_ref[...], axis=0, keepdims=True) + jnp.sum(b_ref[...], axis=0, keepdims=True)

    return pl.pallas_call(body, out_shape=jax.ShapeDtypeStruct((1, a.shape[1]), F32), name="ada_b_grad")(a, b)


def _tail(h1, ff, mods, wf, tgt, nctx_t, tl):
    t = h1.shape[0]
    nt = t // tl

    def loss_fn(valid, h1v, ffv, g8, w, tg):
        h2 = h1v + g8[0:1] * ffv
        y = _rms(h2) * w
        err = (y - tg) ** 2
        return 0.5 * jnp.sum(jnp.mean(err, axis=-1, keepdims=True), axis=0, keepdims=True) * valid

    def body(h1_ref, ff_ref, g_ref, w_ref, t_ref, loss_ref, dh_ref, dff_ref, dg_ref, dw_ref):
        i = pl.program_id(0)
        valid = jnp.where(i < nctx_t, 0.0, 1.0)
        lv, pull = jax.vjp(functools.partial(loss_fn, valid), h1_ref[...], ff_ref[...].astype(F32), g_ref[...],
                           w_ref[...], t_ref[...])
        dh, dff, dg, dw, _ = pull(jnp.ones((1, 1), F32))
        dh_ref[...] = dh
        dff_ref[...] = dff.astype(dff_ref.dtype)
        lb = jnp.broadcast_to(lv, loss_ref.shape)

        @pl.when(i == 0)
        def _():
            loss_ref[...] = lb
            dg_ref[...] = dg
            dw_ref[...] = dw

        @pl.when(i > 0)
        def _():
            loss_ref[...] += lb
            dg_ref[...] += dg
            dw_ref[...] += dw

    tok = pl.BlockSpec((tl, D_MODEL), lambda i: (i, 0))
    return pl.pallas_call(
        body, grid=(nt,),
        in_specs=[tok, tok, pl.BlockSpec((8, D_MODEL), lambda i: (0, 5)), pl.BlockSpec((1, D_MODEL), lambda i: (0, 0)),
                  pl.BlockSpec((tl, D_MODEL), lambda i: (jnp.maximum(i - nctx_t, 0), 0))],
        out_specs=[pl.BlockSpec((8, 128), lambda i: (0, 0)), tok, tok, pl.BlockSpec((8, D_MODEL), lambda i: (0, 0)),
                   pl.BlockSpec((1, D_MODEL), lambda i: (0, 0))],
        out_shape=[jax.ShapeDtypeStruct((8, 128), F32), jax.ShapeDtypeStruct((t, D_MODEL), F32),
                   jax.ShapeDtypeStruct((t, D_MODEL), _MXU_DTYPE), jax.ShapeDtypeStruct((8, D_MODEL), F32),
                   jax.ShapeDtypeStruct((1, D_MODEL), F32)],
        compiler_params=_cp(("arbitrary",)), name="tail_loss")(h1, ff, mods, wf, tgt)


def _pack_w_in(w4):
    ns = w4.shape[2]
    placed = []
    for s0, s1, p0 in IN_SEGMENTS:
        for j in range(N_PLANE):
            lo, hi = max(s0, j * ns), min(s1, (j + 1) * ns)
            if lo < hi:
                placed.append((p0 + lo - s0, w4[j][:, lo - j * ns:hi - j * ns]))
    placed.sort(key=lambda e: e[0])
    pieces, end = [], 0
    for pos, piece in placed:
        assert pos == end, (pos, end)
        pieces.append(piece)
        end = pos + piece.shape[1]
    pieces.append(jnp.zeros((w4.shape[1], P_TOTAL - end), w4.dtype))
    return jnp.concatenate(pieces, axis=1)


def _unpack_w_in(g):
    ns = D_IN_PROJ // N_PLANE
    shards = []
    for j in range(N_PLANE):
        pieces = []
        for s0, s1, p0 in IN_SEGMENTS:
            lo, hi = max(s0, j * ns), min(s1, (j + 1) * ns)
            if lo < hi:
                pieces.append(g[:, p0 + lo - s0:p0 + hi - s0])
        shards.append(jnp.concatenate(pieces, axis=1))
    return jnp.stack(shards)


LATE_WEIGHTS = ['w_br_gdn', 'w_br_ssm', 'w_out', 'w_ffn_in', 'w_ffn_out']
COL_STACKED = ('w_in', 'w_ffn_in')


def _from_shards(n, g):
    if n in COL_STACKED:
        return g
    if SHARD_AXIS[n] == 0:
        return g.reshape(N_PLANE * g.shape[1], g.shape[2])
    return jnp.concatenate([g[j] for j in range(N_PLANE)], axis=1)


def _to_dest_blocks(n, g):
    if g.ndim == 3:
        return g
    if SHARD_AXIS[n] == 0:
        return g.reshape(N_PLANE, g.shape[0] // N_PLANE, g.shape[1])
    sz = g.shape[1] // N_PLANE
    return jnp.stack([g[:, j * sz:(j + 1) * sz] for j in range(N_PLANE)])


def _local_step(x, c, ctx, tgt, W, late_shards):
    lc, l = ctx.shape[0], x.shape[0]
    t = lc + l
    tl = 256
    assert lc == tl and l % tl == 0 and lc % CHUNK == 0
    nt, nctx_t, nctx = t // tl, lc // tl, lc // CHUNK
    act = _MXU_DTYPE
    r1 = lambda v: v.reshape(1, -1)

    xt = jnp.concatenate([ctx, x], axis=0)
    lin = 4 * lax.axis_index("x") + 2 * lax.axis_index("y") + lax.axis_index("c")
    c_all = _gather_all("gather_c", jnp.broadcast_to(c, (8, D_MODEL)))[:, 0]
    cc16 = jnp.concatenate([c_all, r1(W['c_ctx']), jnp.zeros((7, D_MODEL), F32)], axis=0)
    (parts,) = _plane_exchange("gather_mods", [_ada_part(cc16, W['ada_w'])], gather=True)
    mods_all = jnp.transpose(parts, (1, 0, 2)).reshape(16, -1) + r1(W['ada_b'])
    mods = jnp.concatenate([lax.dynamic_slice_in_dim(mods_all, lin, 1, axis=0), mods_all[8:9],
                            jnp.zeros((6, mods_all.shape[1]), F32)], axis=0)

    def mod(kk):
        return Op(mods, (8, D_MODEL), lambda j, i, kk=kk: (0, kk))

    def tokop(arr, w=D_MODEL, off=0, tl_=tl):
        if not isinstance(arr, list) and arr.ndim == 3:
            return Op(arr, (2, tl_, w), lambda j, i: (0, i, off + j), load=_sum_dirs)
        return Op(arr, (tl_, w), lambda j, i: (i, off + j))

    def outop(n, dtype, w=D_MODEL, tl_=tl):
        return Op(((t, n), dtype), (tl_, w), lambda j, i: (i, j))

    def parop(arr, w, off=0):
        return Op(arr, (arr.shape[0], w), lambda j, i: (0, off + j))

    def parout(rows, n, w):
        return Op(((rows, n), F32), (rows, w), lambda j, i: (0, j))

    n1w = r1(W['norm1_w'])
    pre_fn = functools.partial(_prenorm_fn, nctx_t)
    pre_tok, pre_par = [tokop(xt)], [parop(n1w, D_MODEL), mod(1), mod(0)]
    (a,) = _tw_fwd("prenorm_fwd", pre_fn, (1, nt), pre_tok + pre_par, [outop(D_MODEL, act)])
    wp = _pack_w_in(W['w_in'])
    proj = _matmul(a, wp, 'nn', "in_proj")

    gcw, gcb = W['gdn_conv_w'], r1(W['gdn_conv_b'])
    scw, scb = W['ssm_conv_w'], r1(W['ssm_conv_b'])
    conv_parts = {}

    def conv_part(name, mode, poff, cw, cb, woff, width):
        fn = functools.partial(_conv_fn, nctx_t, mode)
        bw = min(width, 1024)
        tok_ = [tokop(proj, bw, poff // bw)]
        par_ = [parop(cw, bw, woff // bw), parop(cb, bw, woff // bw)]
        conv_parts[name] = (fn, tok_, par_, width, bw, poff)
        (o,) = _tw_fwd("conv_" + name, fn, (width // bw, nt), tok_ + par_, [outop(width, F32, bw)])
        return o

    q = conv_part('q', 'q', P_QKV, gcw, gcb, 0, 1024)
    k = conv_part('k', 'k', P_QKV + 1024, gcw, gcb, 1024, 1024)
    v = conv_part('v', 'none', P_QKV + 2048, gcw, gcb, 2048, 1024)
    xs = conv_part('xs', 'none', P_XBC, scw, scb, 0, 2048)
    bm = conv_part('bm', 'none', P_XBC + 2048, scw, scb, 2048, 512)
    cm = conv_part('cm', 'none', P_XBC + 2560, scw, scb, 2560, 512)

    z16, z32 = jnp.zeros((16,), F32), jnp.zeros((32,), F32)
    p0 = jnp.concatenate([W['gdn_a_log'].reshape(-1), jnp.zeros((112,), F32)]).reshape(1, 128)
    p1 = jnp.concatenate([W['gdn_dt_bias'].reshape(-1), z16, W['ssm_dt_bias'].reshape(-1), z32]).reshape(1, 128)
    alog = jnp.concatenate([z32, W['ssm_a_log'].reshape(-1), z32]).reshape(1, 128)
    act_tok, act_par = [tokop(proj, 128, P_SMALL // 128)], [parop(p0, 128), parop(p1, 128)]
    (sp,) = _tw_fwd("small_act", _act_fn, (1, nt), act_tok + act_par, [outop(128, F32, 128)])

    gather_late = PlaneExchange([late_shards[n] for n in LATE_WEIGHTS], gather=True)
    o_f, o_b, ss, tri, *late = _scan2_fwd("gdn_scan_fwd", _gdn_step2, [q, k, v, sp], 1024, GDN_HEADS, nctx,
                                          GDN_HEADS, ride=gather_late)
    W = dict(W, **{n: _from_shards(n, g) for n, g in zip(LATE_WEIGHTS, late)})
    ssd_consts = _ssd_consts()
    y2, hs = _scan_fwd("ssd_scan_fwd", _ssd_step, [xs, bm, cm, sp], [alog], ssd_consts, 2048, SSM_PAIRS, nctx)

    tlm = 128
    ntm = t // tlm
    gnw = r1(W['gdn_norm_w'])
    dl = jnp.repeat(W['ssm_d'], 64).reshape(1, 2048)
    snw = r1(W['ssm_norm_w'])
    mg_tok = [tokop([o_f, o_b], 1024, 0, tlm), tokop(proj, 1024, P_ZG // 1024, tlm)]
    mg_par = [parop(gnw, 128)]
    (og,) = _tw_fwd("mix_gdn", _mixg_fn, (1, ntm), mg_tok + mg_par, [outop(1024, act, 1024, tlm)])
    ms_tok = [tokop(y2, 2048, 0, tlm), tokop(xs, 2048, 0, tlm), tokop(proj, 2048, P_ZS // 2048, tlm)]
    ms_par = [parop(dl, 2048), parop(snw, 2048)]
    (yg,) = _tw_fwd("mix_ssm", _mixs_fn, (1, ntm), ms_tok + ms_par, [outop(2048, act, 2048, tlm)])

    pg = _matmul(og, W['w_br_gdn'], 'nn', "br_gdn", out_dtype=act)
    ps = _matmul(yg, W['w_br_ssm'], 'nn', "br_ssm", out_dtype=act)
    mr_tok = [tokop(proj, 2048, P_GATE // 2048), tokop(pg), tokop(ps)]
    (mrg,) = _tw_fwd("merge", _merge_fn, (1, nt), mr_tok, [outop(1024, act)])
    mo = _matmul(mrg, W['w_out'], 'nn', "w_out")

    n2w = r1(W['norm2_w'])
    n2_tok, n2_par = [tokop(xt), tokop(mo)], [mod(2), parop(n2w, D_MODEL), mod(4), mod(3)]
    h1, f = _tw_fwd("norm2_fwd", _norm2_fn, (1, nt), n2_tok + n2_par, [outop(1024, F32), outop(1024, act)])
    u2 = _matmul(f, W['w_ffn_in'], 'nn', "ffn_in", out_dtype=act, halves='out')
    swb = D_FF // 2
    both = lambda arr: Op(arr, (2, tl, swb), lambda j, i: (0, i, j))
    sw_tok = [both(u2)]
    (sw,) = _tw_fwd("swiglu", _swiglu_fn, (D_FF // swb, nt), sw_tok, [outop(D_FF, act, swb)])
    ff = _matmul(sw, W['w_ffn_out'], 'nn', "ffn_out")

    loss8, dh1, dff, dg2, dnf = _tail(h1, ff, mods, r1(W['norm_f_w']), tgt, nctx_t, tl)
    loss = loss8[0, 0]

    G = {}
    G['norm_f_w'] = dnf.reshape(-1)
    G['w_ffn_out'] = _matmul(sw, dff, 'tn', "d_ffn_out", out_dtype=_GRAD_DTYPE)
    dsw = _matmul(dff, W['w_ffn_out'], 'nt', "d_sw", out_dtype=act)
    (du2,) = _tw_bwd("swiglu_bwd", _swiglu_fn, (D_FF // swb, nt), sw_tok, [], [[tokop(dsw, swb)]],
                     [both(((2, t, D_FF), act))], [])
    G['w_ffn_in'] = _matmul(f, du2, 'tn', "d_ffn_in", out_dtype=_GRAD_DTYPE, stacked_out=True, halves='b')
    df = _matmul(du2, W['w_ffn_in'], 'nt', "d_f", halves='a')
    dxt1, dmo, dg1, dn2, dsc2, dsh2 = _tw_bwd(
        "norm2_bwd", _norm2_fn, (1, nt), n2_tok, n2_par, [[tokop(dh1)], [tokop(df)]],
        [outop(1024, F32), outop(1024, act)],
        [parout(8, 1024, 1024), parout(1, 1024, 1024), parout(8, 1024, 1024), parout(8, 1024, 1024)])
    G['norm2_w'] = dn2.reshape(-1)
    G['w_out'] = _matmul(mrg, dmo, 'tn', "d_w_out", out_dtype=_GRAD_DTYPE)
    dmrg = _matmul(dmo, W['w_out'], 'nt', "d_mrg", out_dtype=act)
    def win(off, w, tl_=tl):
        return Op(((t, P_TOTAL), act), (tl_, w), lambda j, i: (i, off // w + j))

    dproj = lax.empty((t, P_TOTAL), act)
    dproj, dpg, dps = _tw_bwd("merge_bwd", _merge_fn, (1, nt), mr_tok, [], [[tokop(dmrg)]],
                              [win(P_GATE, 2048), outop(1024, act), outop(1024, act)], [], into={0: dproj})
    G['w_br_gdn'] = _matmul(og, dpg, 'tn', "d_br_gdn", out_dtype=_GRAD_DTYPE)
    G['w_br_ssm'] = _matmul(yg, dps, 'tn', "d_br_ssm", out_dtype=_GRAD_DTYPE)
    dog = _matmul(dpg, W['w_br_gdn'], 'nt', "d_og")
    dyg = _matmul(dps, W['w_br_ssm'], 'nt', "d_yg")

    do, dproj, dgnw = _tw_bwd("mix_gdn_bwd", _mixg_fn, (1, ntm), mg_tok, mg_par, [[tokop(dog, 1024, 0, tlm)]],
                              [outop(1024, F32, 1024, tlm), win(P_ZG, 1024, tlm)], [parout(1, 128, 128)],
                              into={1: dproj})
    G['gdn_norm_w'] = dgnw.reshape(-1)
    dy, dxs_a, dproj, ddl, dsnw = _tw_bwd(
        "mix_ssm_bwd", _mixs_fn, (1, ntm), ms_tok, ms_par, [[tokop(dyg, 2048, 0, tlm)]],
        [outop(2048, F32, 2048, tlm), outop(2048, F32, 2048, tlm), win(P_ZS, 2048, tlm)],
        [parout(1, 2048, 2048), parout(1, 2048, 2048)], into={2: dproj})
    G['ssm_d'] = ddl.reshape(SSM_HEADS, 64).sum(axis=1)
    G['ssm_norm_w'] = dsnw.reshape(-1)

    dq2, dk2, dv2, dsp_g = _scan_bwd("gdn_scan_bwd", _gdn_step, [q, k, v, sp], [], ss, do, GDN_HEADS, nctx, aux=tri)
    received = {}
    scatter_late = PlaneExchange([_to_dest_blocks(n, G.pop(n)) for n in LATE_WEIGHTS], gather=False)
    dxs2, dbm2, dcm2, dsp_s, dalog, *got = _scan_bwd("ssd_scan_bwd", _ssd_step, [xs, bm, cm, sp], [alog], hs, dy,
                                                     SSM_PAIRS, nctx, ride=scatter_late, consts=ssd_consts)
    received.update(zip(LATE_WEIGHTS, got))
    G['ssm_a_log'] = dalog[0, 32:96].reshape(2, SSM_HEADS)

    dconv_w, dconv_b = {}, {}

    def conv_bwd(name, cots_, buf):
        fn, tok_, par_, width, bw, poff = conv_parts[name]
        buf, dconv_w[name], dconv_b[name] = _tw_bwd(
            "conv_" + name + "_bwd", fn, (width // bw, nt), tok_, par_, [[tokop(c_, bw) for c_ in cots_]],
            [win(poff, bw)], [parout(3, width, bw), parout(1, width, bw)], into={0: buf})
        return buf

    dproj = conv_bwd('q', [dq2], dproj)
    dproj = conv_bwd('k', [dk2], dproj)
    dproj = conv_bwd('v', [dv2], dproj)
    dproj = conv_bwd('xs', [dxs2, dxs_a], dproj)
    dproj = conv_bwd('bm', [dbm2], dproj)
    dproj = conv_bwd('cm', [dcm2], dproj)
    G['gdn_conv_w'] = jnp.concatenate([dconv_w['q'], dconv_w['k'], dconv_w['v']], axis=1)
    G['gdn_conv_b'] = [dconv_b['q'], dconv_b['k'], dconv_b['v']]
    G['ssm_conv_w'] = jnp.concatenate([dconv_w['xs'], dconv_w['bm'], dconv_w['cm']], axis=1)
    G['ssm_conv_b'] = [dconv_b['xs'], dconv_b['bm'], dconv_b['cm']]

    dproj, dp0, dp1 = _tw_bwd("small_act_bwd", _act_fn, (1, nt), act_tok, act_par,
                              [[tokop(dsp_g, 128), tokop(dsp_s, 128)]], [win(P_SMALL, P_TOTAL - P_SMALL)],
                              [parout(1, 128, 128), parout(1, 128, 128)], into={0: dproj})
    G['gdn_a_log'] = dp0[0, 0:16].reshape(2, GDN_HEADS)
    G['gdn_dt_bias'] = dp1[0, 0:16].reshape(2, GDN_HEADS)
    G['ssm_dt_bias'] = dp1[0, 32:96].reshape(2, SSM_HEADS)

    g_w_in = _unpack_w_in(_matmul(a, dproj, 'tn', "d_w_in", out_dtype=_GRAD_DTYPE))
    da, (received['w_in'],) = _matmul(dproj, wp, 'nt', "d_a", ride=PlaneExchange([g_w_in], gather=False))

    gx_out = Op(((l, D_MODEL), F32), (tl, D_MODEL), lambda j, i: (jnp.maximum(i - nctx_t, 0), 0))
    grad_x, dn1, dsc1, dsh1 = _tw_bwd(
        "prenorm_bwd", pre_fn, (1, nt), pre_tok, pre_par, [[tokop(da)]], [gx_out],
        [parout(1, 1024, 1024), parout(8, 1024, 1024), parout(8, 1024, 1024)], tok_add=tokop(dxt1),
        sem=("arbitrary", "arbitrary"))
    G['norm1_w'] = dn1.reshape(-1)
    dmods = jnp.concatenate([dsh1, dsc1, dg1, dsh2, dsc2, dg2], axis=1)
    dm_all = _gather_all("gather_dmods", dmods)
    ns = W['ada_w'].shape[1]
    mine = lax.dynamic_slice_in_dim(dm_all, (2 * lax.axis_index("x") + lax.axis_index("y")) * ns, ns, axis=2)
    G['ada_w'], dcc = _ada_bwd_shard(cc16, W['ada_w'], mine[:, 0], mine[:, 1])
    G['ada_b'] = jnp.where(lin == 0, _rowsum2(dm_all[:, 0], dm_all[:, 1]).reshape(-1), 0.0)
    G['c_ctx'] = dcc[0]
    return loss, grad_x, G, received


def _row_tile(r, c):
    for tr in (512, 256, 128, 64, 32, 16, 8):
        if r % tr == 0 and tr * c * 4 <= (1 << 20):
            return tr
    return r


def _sum4(name, rv):
    _, r, c = rv.shape
    tr = _row_tile(r, c)

    def body(r_ref, o_ref):
        o_ref[...] = ((r_ref[0].astype(F32) + r_ref[1].astype(F32)) + r_ref[2].astype(F32)) + r_ref[3].astype(F32)

    return pl.pallas_call(
        body, grid=(r // tr,), in_specs=[pl.BlockSpec((N_PLANE, tr, c), lambda i: (0, i, 0))],
        out_specs=pl.BlockSpec((tr, c), lambda i: (i, 0)), out_shape=jax.ShapeDtypeStruct((r, c), F32),
        compiler_params=_cp(("parallel",)), name=name)(rv)


def _adamw(name, w, m, v, p, q=None):
    r, c = w.shape
    tr = _row_tile(r, c)
    grads = [p] if q is None else [p, q]

    def body(w_ref, m_ref, v_ref, *refs):
        g_ref, d_ref, mo_ref, vo_ref = refs[len(grads):]
        g = refs[0][...] if q is None else refs[0][...] + refs[1][...]
        mn = ADAM_B1 * m_ref[...] + (1.0 - ADAM_B1) * g
        vn = ADAM_B2 * v_ref[...] + (1.0 - ADAM_B2) * jnp.square(g)
        m_hat = mn / (1.0 - ADAM_B1 ** ADAM_STEP)
        v_hat = vn / (1.0 - ADAM_B2 ** ADAM_STEP)
        g_ref[...] = g
        d_ref[...] = -ADAM_LR * (m_hat / (jnp.sqrt(v_hat) + ADAM_EPS) + ADAM_WD * w_ref[...])
        mo_ref[...] = mn
        vo_ref[...] = vn

    spec = pl.BlockSpec((tr, c), lambda i: (i, 0))
    return pl.pallas_call(
        body, grid=(r // tr,), in_specs=[spec] * (3 + len(grads)), out_specs=[spec] * 4,
        out_shape=[jax.ShapeDtypeStruct((r, c), F32)] * 4, compiler_params=_cp(("parallel",)), name=name)(w, m, v, *grads)


SMALL_ROWS = 24


def _pack_small(d):
    pieces = [p.reshape(-1) for n in SMALL for p in (d[n] if isinstance(d[n], list) else [d[n]])]
    v = jnp.concatenate(pieces)
    return jnp.pad(v, (0, SMALL_ROWS * 1024 - v.shape[0])).reshape(SMALL_ROWS, 1024)


def _unpack_small(buf, like):
    v = buf.reshape(-1)
    out, off = {}, 0
    for n in SMALL:
        sz = like[n].size
        out[n] = v[off:off + sz].reshape(like[n].shape)
        off += sz
    return out


def kernel(x, c, ctx, c_ctx, ada_w, ada_b, norm1_w, w_in, gdn_conv_w, gdn_conv_b, gdn_a_log, gdn_dt_bias, gdn_norm_w, ssm_conv_w, ssm_conv_b, ssm_a_log, ssm_dt_bias, ssm_d, ssm_norm_w, w_br_gdn, w_br_ssm, w_out, norm2_w, w_ffn_in, w_ffn_out, norm_f_w, loss_target, m_c_ctx, m_ada_w, m_ada_b, m_norm1_w, m_w_in, m_gdn_conv_w, m_gdn_conv_b, m_gdn_a_log, m_gdn_dt_bias, m_gdn_norm_w, m_ssm_conv_w, m_ssm_conv_b, m_ssm_a_log, m_ssm_dt_bias, m_ssm_d, m_ssm_norm_w, m_w_br_gdn, m_w_br_ssm, m_w_out, m_norm2_w, m_w_ffn_in, m_w_ffn_out, m_norm_f_w, v_c_ctx, v_ada_w, v_ada_b, v_norm1_w, v_w_in, v_gdn_conv_w, v_gdn_conv_b, v_gdn_a_log, v_gdn_dt_bias, v_gdn_norm_w, v_ssm_conv_w, v_ssm_conv_b, v_ssm_a_log, v_ssm_dt_bias, v_ssm_d, v_ssm_norm_w, v_w_br_gdn, v_w_br_ssm, v_w_out, v_norm2_w, v_w_ffn_in, v_w_ffn_out, v_norm_f_w):
    args = dict(locals())
    wl = {n: args[n] for n in WEIGHTS}
    ml = {n: args['m_' + n] for n in WEIGHTS}
    vl = {n: args['v_' + n] for n in WEIGHTS}

    def nodepth(n, a):
        return a if n in ('c_ctx', 'norm_f_w') else a[0]

    shard = {n: nodepth(n, wl[n]).astype(_MXU_DTYPE if n in MXU_WEIGHTS else F32) for n in SHARDED}
    first = [n for n in SHARDED if n not in LATE_WEIGHTS and n != 'ada_w']
    W = {n: nodepth(n, wl[n]) for n in SMALL}
    W['ada_w'] = shard['ada_w']
    for n, g in zip(first, _gather_two_level("all_gather_plane", [shard[n] for n in first])):
        W[n] = _from_shards(n, g)

    loss_local, grad_x, G, received = _local_step(x[0], c, ctx[0], loss_target[0], W,
                                                  late_shards={n: shard[n] for n in LATE_WEIGHTS})
    loss = lax.psum(loss_local, ("x", "y", "c"))

    small_g = _pack_small(G)
    last = [n for n in SHARDED if n not in received and n != 'ada_w']
    send = [_to_dest_blocks(n, G[n]) for n in last] + [jnp.broadcast_to(small_g[None], (N_PLANE,) + small_g.shape)]
    received.update(zip(last + ['small'], _plane_exchange("scatter_plane", send, gather=False)))
    names = [n for n in SHARDED if n != 'ada_w'] + ['small']
    plane_sum = [_sum4("sum4_" + n, received[n]) for n in names]
    other = _swap_sibling(plane_sum)

    wd = {n: nodepth(n, wl[n]) for n in WEIGHTS}
    md = {n: nodepth(n, ml[n]) for n in WEIGHTS}
    vd = {n: nodepth(n, vl[n]) for n in WEIGHTS}
    res = {'ada_w': [o.reshape(wl['ada_w'].shape)
                     for o in _adamw("adamw_ada_w", wd['ada_w'], md['ada_w'], vd['ada_w'], G['ada_w'])]}
    for n, p, q in zip(names, plane_sum, other):
        if n == 'small':
            outs = _adamw("adamw_small", _pack_small(wd), _pack_small(md), _pack_small(vd), p, q)
            unpacked = [_unpack_small(o, wd) for o in outs]
            for sn in SMALL:
                res[sn] = [u[sn].reshape(wl[sn].shape) for u in unpacked]
        else:
            outs = _adamw("adamw_" + n, wd[n], md[n], vd[n], p, q)
            res[n] = [o.reshape(wl[n].shape) for o in outs]
    flat = [res[n][kind] for kind in range(4) for n in WEIGHTS]
    return (loss, grad_x[None], *flat)
```

```python
import functools

import jax
import jax.numpy as jnp
from jax import lax
from jax.experimental import pallas as pl
from jax.experimental.pallas import tpu as pltpu

F32 = jnp.float32
HI = lax.Precision.HIGHEST
_MXU_DTYPE = jnp.bfloat16
_SCAN_DTYPE = jnp.bfloat16
_GRAD_DTYPE = jnp.bfloat16

D_MODEL = 1024
EPS = 1e-6
CHUNK = 64
GRID_W = 64
GDN_HEADS = 8
GDN_DK = 128
SSM_HEADS = 32
SSM_PAIRS = 16
SSD_GROUP = 1
D_FF = 2816
D_IN_PROJ = 11360
N_PLANE = 4
P_QKV, P_ZG, P_GATE, P_ZS, P_XBC, P_SMALL, P_TOTAL = 0, 3072, 4096, 6144, 8192, 11264, 11776
IN_SEGMENTS = [(0, 3072, P_QKV), (3072, 4096, P_ZG), (4096, 4112, P_SMALL), (4112, 4128, P_SMALL + 16),
               (4128, 6176, P_ZS), (6176, 9248, P_XBC), (9248, 9312, P_SMALL + 32), (9312, 11360, P_GATE)]
ADAM_LR, ADAM_B1, ADAM_B2, ADAM_EPS, ADAM_WD, ADAM_STEP = 0.001, 0.9, 0.999, 1e-08, 0.01, 10

VMEM_LIMIT = 48 * 1024 * 1024

WEIGHTS = ['c_ctx', 'ada_w', 'ada_b', 'norm1_w', 'w_in', 'gdn_conv_w', 'gdn_conv_b', 'gdn_a_log', 'gdn_dt_bias',
           'gdn_norm_w', 'ssm_conv_w', 'ssm_conv_b', 'ssm_a_log', 'ssm_dt_bias', 'ssm_d', 'ssm_norm_w', 'w_br_gdn',
           'w_br_ssm', 'w_out', 'norm2_w', 'w_ffn_in', 'w_ffn_out', 'norm_f_w']
SHARD_AXIS = {'ada_w': 1, 'w_in': 1, 'gdn_conv_w': 1, 'ssm_conv_w': 1, 'w_br_gdn': 0, 'w_br_ssm': 0, 'w_out': 0,
              'w_ffn_in': 1, 'w_ffn_out': 0}
SHARDED = [n for n in WEIGHTS if n in SHARD_AXIS]
SMALL = [n for n in WEIGHTS if n not in SHARD_AXIS]
MXU_WEIGHTS = ['ada_w', 'w_in', 'w_br_gdn', 'w_br_ssm', 'w_out', 'w_ffn_in', 'w_ffn_out']


def _cp(sem):
    return pltpu.CompilerParams(dimension_semantics=sem, vmem_limit_bytes=VMEM_LIMIT)


def _pick(n, cands):
    for c in cands:
        if n % c == 0:
            return c
    raise ValueError(f"no tile for {n}")


HBM = pl.BlockSpec(memory_space=pl.ANY)
MESH = pl.DeviceIdType.MESH


def _plane_peers():
    x, y, c = lax.axis_index("x"), lax.axis_index("y"), lax.axis_index("c")
    return (x, y, c), [(1 - x, y, c), (x, 1 - y, c), (1 - x, 1 - y, c)]


class PlaneExchange:
    def __init__(self, arrs, gather):
        self.arrs, self.gather, self.n = list(arrs), gather, len(arrs)
        self.in_specs = [HBM] * self.n
        self.out_specs = [HBM] * self.n
        self.out_shape = [jax.ShapeDtypeStruct((N_PLANE,) + a.shape if gather else a.shape, a.dtype) for a in self.arrs]
        self.scratch = [pltpu.SemaphoreType.DMA((3 * self.n,)), pltpu.SemaphoreType.DMA((3 * self.n,)),
                        pltpu.SemaphoreType.DMA((self.n,))]

    def _copies(self, ins, outs, sems):
        send_sems, recv_sems, local_sems = sems
        (x, y, c), peers = _plane_peers()
        me = 2 * x + y
        copies = []
        for ti in range(self.n):
            src = ins[ti] if self.gather else ins[ti].at[me]
            copies.append(pltpu.make_async_copy(src, outs[ti].at[me], local_sems.at[ti]))
            for kk, (px, py, pc) in enumerate(peers):
                src = ins[ti] if self.gather else ins[ti].at[2 * px + py]
                copies.append(pltpu.make_async_remote_copy(
                    src_ref=src, dst_ref=outs[ti].at[me], send_sem=send_sems.at[3 * ti + kk],
                    recv_sem=recv_sems.at[3 * ti + kk], device_id=(px, py, pc), device_id_type=MESH))
        return copies

    def start(self, ins, outs, sems):
        for cp in self._copies(ins, outs, sems):
            cp.start()

    def wait(self, ins, outs, sems):
        for cp in self._copies(ins, outs, sems):
            cp.wait()


def _plane_exchange(name, arrs, gather):
    ex = PlaneExchange(arrs, gather)
    n = ex.n

    def body(*refs):
        ins, outs, sems = refs[:n], refs[n:2 * n], refs[2 * n:]
        ex.start(ins, outs, sems)
        ex.wait(ins, outs, sems)

    return pl.pallas_call(body, in_specs=ex.in_specs, out_specs=ex.out_specs, out_shape=ex.out_shape,
                          scratch_shapes=ex.scratch, name=name)(*arrs)


def _gather_two_level(name, arrs):
    n = len(arrs)
    split = [a.shape[0] % 32 == 0 for a in arrs]

    def body(*refs):
        ins, outs = refs[:n], refs[n:2 * n]
        ici_send, ici_recv, d2d_send, d2d_recv, local_sems = refs[2 * n:]
        (x, y, c), peers = _plane_peers()
        me = 2 * x + y

        def part(ref, ti):
            if not split[ti]:
                return ref
            rows = arrs[ti].shape[0] // 2
            return ref.at[pl.ds(c * rows, rows)]

        local = [pltpu.make_async_copy(ins[ti], outs[ti].at[me], local_sems.at[ti]) for ti in range(n)]
        ici = [pltpu.make_async_remote_copy(
            src_ref=part(ins[ti], ti), dst_ref=part(outs[ti].at[me], ti), send_sem=ici_send.at[3 * ti + kk],
            recv_sem=ici_recv.at[3 * ti + kk], device_id=peer, device_id_type=MESH)
            for ti in range(n) for kk, peer in enumerate(peers)]
        for cp in local + ici:
            cp.start()
        d2d = []
        for ti in range(n):
            for kk, (px, py, pc) in enumerate(peers):
                ici[3 * ti + kk].wait_recv()
                if split[ti]:
                    piece = part(outs[ti].at[2 * px + py], ti)
                    cp = pltpu.make_async_remote_copy(
                        src_ref=piece, dst_ref=piece, send_sem=d2d_send.at[3 * ti + kk],
                        recv_sem=d2d_recv.at[3 * ti + kk], device_id=(x, y, 1 - c), device_id_type=MESH)
                    cp.start()
                    d2d.append(cp)
        for cp in ici:
            cp.wait_send()
        for cp in d2d:
            cp.wait()
        for cp in local:
            cp.wait()

    return pl.pallas_call(
        body, in_specs=[HBM] * n, out_specs=[HBM] * n,
        out_shape=[jax.ShapeDtypeStruct((N_PLANE,) + a.shape, a.dtype) for a in arrs],
        scratch_shapes=[pltpu.SemaphoreType.DMA((3 * n,))] * 4 + [pltpu.SemaphoreType.DMA((n,))], name=name)(*arrs)


def _gather_all(name, arr):
    flips = [(fx, fy, fc) for fx in (0, 1) for fy in (0, 1) for fc in (0, 1)][1:]

    def body(a_ref, o_ref, send_sems, recv_sems, local_sem):
        x, y, c = lax.axis_index("x"), lax.axis_index("y"), lax.axis_index("c")
        me = 4 * x + 2 * y + c
        copies = [pltpu.make_async_copy(a_ref, o_ref.at[me], local_sem)]
        for kk, (fx, fy, fc) in enumerate(flips):
            peer = (1 - x if fx else x, 1 - y if fy else y, 1 - c if fc else c)
            copies.append(pltpu.make_async_remote_copy(
                src_ref=a_ref, dst_ref=o_ref.at[me], send_sem=send_sems.at[kk], recv_sem=recv_sems.at[kk],
                device_id=peer, device_id_type=MESH))
        for cp in copies:
            cp.start()
        for cp in copies:
            cp.wait()

    return pl.pallas_call(
        body, in_specs=[HBM], out_specs=HBM, out_shape=jax.ShapeDtypeStruct((8,) + arr.shape, arr.dtype),
        scratch_shapes=[pltpu.SemaphoreType.DMA((7,)), pltpu.SemaphoreType.DMA((7,)), pltpu.SemaphoreType.DMA],
        name=name)(arr)


def _swap_sibling(arrs):
    n = len(arrs)

    def body(*refs):
        ins, outs, send_sems, recv_sems = refs[:n], refs[n:2 * n], refs[2 * n], refs[2 * n + 1]
        x, y, c = lax.axis_index("x"), lax.axis_index("y"), lax.axis_index("c")
        copies = [pltpu.make_async_remote_copy(src_ref=ins[ti], dst_ref=outs[ti], send_sem=send_sems.at[ti],
                                               recv_sem=recv_sems.at[ti], device_id=(x, y, 1 - c), device_id_type=MESH)
                  for ti in range(n)]
        for cp in copies:
            cp.start()
        for cp in copies:
            cp.wait()

    return pl.pallas_call(
        body, in_specs=[HBM] * n, out_specs=[HBM] * n, out_shape=[jax.ShapeDtypeStruct(a.shape, a.dtype) for a in arrs],
        scratch_shapes=[pltpu.SemaphoreType.DMA((n,)), pltpu.SemaphoreType.DMA((n,))], name="swap_sibling")(*arrs)


MATMUL_VMEM_BUDGET = 36 * 1024 * 1024
TILE_CANDIDATES = (2944, 2816, 1408, 1024, 768, 512, 256, 128)
STEP_COST_BYTES = 1 << 20


def _matmul_tiles(m, n, k, ab, bb, ob, tn_fixed=None, tk_fixed=None):
    def cands(dim, whole_up_to):
        out = [c for c in TILE_CANDIDATES if dim % c == 0]
        if dim <= whole_up_to and dim not in out:
            out.append(dim)
        return out

    best = None
    for tm in cands(m, 3072):
        for tn in ([tn_fixed] if tn_fixed else cands(n, 3072)):
            for tk in ([tk_fixed] if tk_fixed else cands(k, 2048)):
                gi, gj, gk = m // tm, n // tn, k // tk
                vmem = 2 * (tm * tk * ab + tk * tn * bb + tm * tn * ob) + (tm * tn * 4 if gk > 1 else 0)
                if vmem > MATMUL_VMEM_BUDGET:
                    continue
                a_reads = 1 if gk == 1 else gj
                b_reads = 1 if (gk == 1 and gj == 1) else gi
                cost = (m * k * ab * a_reads + k * n * bb * b_reads + m * n * ob + gi * gj * gk * STEP_COST_BYTES)
                if best is None or cost < best[0]:
                    best = (cost, tm, tn, tk)
    assert best is not None, (m, n, k)
    return best[1:]


def _matmul(a, b, form, name, out_dtype=F32, stacked_out=False, ride=None, halves=None):
    dims = {'nn': (((1,), (0,)), ((), ())), 'nt': (((1,), (1,)), ((), ())), 'tn': (((0,), (0,)), ((), ()))}[form]
    stacked_b = b.ndim == 3 and halves != 'b'
    ns = b.shape[2] if stacked_b else None
    if form == 'nn':
        m, k = a.shape
        n = b.shape[0] * ns if stacked_b else b.shape[1]
    elif form == 'nt':
        m, k = (a.shape[1], 2 * a.shape[2]) if halves == 'a' else a.shape
        n = b.shape[1] if stacked_b else b.shape[0]
    else:
        k, m = a.shape
        n = 2 * b.shape[2] if halves == 'b' else b.shape[1]
    ob = jnp.dtype(out_dtype).itemsize
    tm, tn, tk = _matmul_tiles(m, n, k, a.dtype.itemsize, b.dtype.itemsize, ob,
                               tn_fixed=(n // N_PLANE if stacked_out else ns if (stacked_b and form == 'nn') else None),
                               tk_fixed=(ns if (stacked_b and form == 'nt') else None))
    nk = k // tk
    grid = (m // tm, n // tn, nk)
    n_ride = ride.n if ride is not None else 0

    def body(*refs):
        a_ref, b_ref = refs[0], refs[1]
        ride_in = refs[2:2 + n_ride]
        o_ref = refs[2 + n_ride]
        ride_out = refs[3 + n_ride:3 + 2 * n_ride]
        acc_ref = refs[3 + 2 * n_ride]
        sems = refs[4 + 2 * n_ride:]
        i, j, kk = pl.program_id(0), pl.program_id(1), pl.program_id(2)
        if ride is not None:
            @pl.when((i == 0) & (j == 0) & (kk == 0))
            def _():
                ride.start(ride_in, ride_out, sems)

        def put(val):
            if stacked_out or halves == 'out':
                o_ref[0] = val.astype(o_ref.dtype)
            else:
                o_ref[...] = val.astype(o_ref.dtype)

        av = a_ref[0] if halves == 'a' else a_ref[...]
        bv = b_ref[0] if (stacked_b or halves == 'b') else b_ref[...]
        part = lax.dot_general(av.astype(_MXU_DTYPE), bv.astype(_MXU_DTYPE), dims, preferred_element_type=F32)
        if nk == 1:
            put(part)
        else:
            @pl.when(kk == 0)
            def _():
                acc_ref[...] = part

            @pl.when(kk > 0)
            def _():
                acc_ref[...] += part

            @pl.when(kk == nk - 1)
            def _():
                put(acc_ref[...])

        if ride is not None:
            @pl.when((i == grid[0] - 1) & (j == grid[1] - 1) & (kk == nk - 1))
            def _():
                ride.wait(ride_in, ride_out, sems)

    if form == 'nn':
        a_spec = pl.BlockSpec((tm, tk), lambda i, j, kk: (i, kk))
        b_spec = (pl.BlockSpec((1, tk, tn), lambda i, j, kk: (j, kk, 0)) if stacked_b
                  else pl.BlockSpec((tk, tn), lambda i, j, kk: (kk, j)))
    elif form == 'nt':
        a_spec = pl.BlockSpec((tm, tk), lambda i, j, kk: (i, kk))
        b_spec = (pl.BlockSpec((1, tn, tk), lambda i, j, kk: (kk, j, 0)) if stacked_b
                  else pl.BlockSpec((tn, tk), lambda i, j, kk: (j, kk)))
    else:
        a_spec = pl.BlockSpec((tk, tm), lambda i, j, kk: (kk, i))
        b_spec = pl.BlockSpec((tk, tn), lambda i, j, kk: (kk, j))
    hj, hk = grid[1] // 2, nk // 2
    if halves == 'a':
        a_spec = pl.BlockSpec((1, tm, tk), lambda i, j, kk: (kk // hk, i, kk % hk))
    if halves == 'b':
        b_spec = pl.BlockSpec((1, tk, tn), lambda i, j, kk: (j // hj, kk, j % hj))
    if stacked_out:
        o_spec = pl.BlockSpec((1, tm, tn), lambda i, j, kk: (j, i, 0))
        o_shape = jax.ShapeDtypeStruct((N_PLANE, m, tn), out_dtype)
    elif halves == 'out':
        o_spec = pl.BlockSpec((1, tm, tn), lambda i, j, kk: (j // hj, i, j % hj))
        o_shape = jax.ShapeDtypeStruct((2, m, n // 2), out_dtype)
    else:
        o_spec = pl.BlockSpec((tm, tn), lambda i, j, kk: (i, j))
        o_shape = jax.ShapeDtypeStruct((m, n), out_dtype)
    acc_shape = (tm, tn) if nk > 1 else (8, 128)
    if ride is None:
        return pl.pallas_call(
            body, grid=grid, in_specs=[a_spec, b_spec], out_specs=o_spec, out_shape=o_shape,
            scratch_shapes=[pltpu.VMEM(acc_shape, F32)],
            compiler_params=_cp(("parallel", "parallel", "arbitrary")), name=name)(a, b)
    res = pl.pallas_call(
        body, grid=grid, in_specs=[a_spec, b_spec] + ride.in_specs, out_specs=[o_spec] + ride.out_specs,
        out_shape=[o_shape] + ride.out_shape, scratch_shapes=[pltpu.VMEM(acc_shape, F32)] + ride.scratch,
        compiler_params=_cp(("arbitrary", "arbitrary", "arbitrary")), name=name)(a, b, *ride.arrs)
    return res[0], res[1:]


class Op:
    def __init__(self, arr, bs, im, load=None):
        self.arr, self.bs, self.im = arr, bs, im
        self.arrs = list(arr) if isinstance(arr, list) else [arr]
        self.load = load or (lambda r: r[...].astype(F32))

    def spec(self):
        return pl.BlockSpec(self.bs, self.im)

    def value(self, it):
        return functools.reduce(lambda u, w: u + w, [self.load(next(it)) for _ in self.arrs])


def _op_specs(ops):
    return [op.spec() for op in ops for _ in op.arrs]


def _op_arrays(ops):
    return [a for op in ops for a in op.arrs]


def _sum_dirs(r):
    return r[0].astype(F32) + r[1].astype(F32)


def _tw_fwd(name, fn, grid, ins, outs):
    def body(*refs):
        info = (pl.program_id(0), pl.program_id(1))
        it = iter(refs)
        vals = [op.value(it) for op in ins]
        res = fn(info, *vals)
        for r, v in zip(it, res):
            r[...] = v.astype(r.dtype)

    return pl.pallas_call(
        body, grid=grid, in_specs=_op_specs(ins), out_specs=[op.spec() for op in outs],
        out_shape=[jax.ShapeDtypeStruct(*op.arr) for op in outs],
        compiler_params=_cp(("parallel", "arbitrary")), name=name)(*_op_arrays(ins))


def _tw_bwd(name, fn, grid, tok, par, cots, tok_out, par_out, tok_add=None, sem=("parallel", "arbitrary"), into=None):
    n_tok = len(tok)
    flat_cots = [op for group in cots for op in group]
    extra = [tok_add] if tok_add is not None else []
    out_ops = list(tok_out) + list(par_out)
    into = into or {}

    def body(*refs):
        info = (pl.program_id(0), pl.program_id(1))
        it = iter(refs)
        tok_v = [op.value(it) for op in tok]
        par_v = [op.value(it) for op in par]
        cot_v = [functools.reduce(lambda u, w: u + w, [op.value(it) for op in group]) for group in cots]
        add_v = [op.value(it) for op in extra]
        for _ in into:
            next(it)
        _, pull = jax.vjp(lambda *a: fn(info, *a), *tok_v, *par_v)
        grads = pull(tuple(cot_v))
        for i in range(n_tok):
            r = next(it)
            g = grads[i] + add_v[0] if (i == 0 and add_v) else grads[i]
            if r.shape[-1] > g.shape[-1]:
                g = jnp.concatenate([g, jnp.zeros((g.shape[0], r.shape[-1] - g.shape[-1]), g.dtype)], axis=1)
            r[...] = g.astype(r.dtype)
        first = pl.program_id(1) == 0
        for i in range(len(par)):
            r = next(it)
            g = grads[n_tok + i]

            @pl.when(first)
            def _(r=r, g=g):
                r[...] = g

            @pl.when(jnp.logical_not(first))
            def _(r=r, g=g):
                r[...] += g

    ops = tok + par + flat_cots + extra
    n_in = len(_op_arrays(ops))
    return pl.pallas_call(
        body, grid=grid, in_specs=_op_specs(ops) + [HBM] * len(into), out_specs=[op.spec() for op in out_ops],
        out_shape=[jax.ShapeDtypeStruct(*op.arr) for op in out_ops],
        input_output_aliases={n_in + k: i for k, i in enumerate(into)},
        compiler_params=_cp(sem), name=name)(*_op_arrays(ops), *into.values())


def _silu(x):
    return x * jax.nn.sigmoid(x)


def _rms(x):
    return x * lax.rsqrt(jnp.mean(x * x, axis=-1, keepdims=True) + EPS)


@functools.partial(jax.custom_vjp, nondiff_argnums=(1,))
def _shift_rows(x, k):
    return pltpu.roll(x, k % x.shape[0], 0)


def _shift_rows_fwd(x, k):
    return _shift_rows(x, k), None


def _shift_rows_bwd(k, _, g):
    return (_shift_rows(g, -k),)


_shift_rows.defvjp(_shift_rows_fwd, _shift_rows_bwd)


def _prenorm_fn(nctx_t, info, x, w, sc8, sh8):
    is_ctx = info[1] < nctx_t
    sc = jnp.where(is_ctx, sc8[1:2], sc8[0:1])
    sh = jnp.where(is_ctx, sh8[1:2], sh8[0:1])
    return (_rms(x) * w * (1.0 + sc) + sh,)


def _conv_fn(nctx_t, mode, info, x, w, b):
    n, c = x.shape
    is_ctx = info[1] < nctx_t
    idx = lax.broadcasted_iota(jnp.int32, (n, 1), 0)
    rr = jnp.where(is_ctx, idx, idx % GRID_W)
    first = rr == 0
    last = rr == jnp.where(is_ctx, n - 1, GRID_W - 1)
    prev = jnp.where(first, 0.0, _shift_rows(x, 1))
    nxt = jnp.where(last, 0.0, _shift_rows(x, -1))
    y = b + prev * w[0:1] + x * w[1:2] + nxt * w[2:3]
    y = _silu(y)
    if mode == 'none':
        return (y,)
    scale = GDN_DK ** -0.5 if mode == 'q' else 1.0
    outs = []
    for h in range(c // 128):
        yh = y[:, h * 128:(h + 1) * 128]
        outs.append(yh * lax.rsqrt(jnp.sum(yh * yh, axis=-1, keepdims=True) + EPS) * scale)
    return (jnp.concatenate(outs, axis=1),)


def _act_fn(info, x, p0, p1):
    lane = lax.broadcasted_iota(jnp.int32, x.shape, 1)
    sp = jax.nn.softplus(x + p1)
    g = -jnp.exp(p0) * sp
    bt = jax.nn.sigmoid(x)
    return (jnp.where(lane < 16, g, jnp.where(lane < 32, bt, jnp.where(lane < 96, sp, 0.0))),)


def _mixg_fn(info, o, zg, gnw):
    outs = []
    for h in range(GDN_HEADS):
        outs.append(_rms(o[:, h * 128:(h + 1) * 128]) * gnw)
    return (jnp.concatenate(outs, axis=1) * _silu(zg),)


def _mixs_fn(info, y, xs, zs, dl, snw):
    yy = (y + dl * xs) * _silu(zs)
    outs = []
    for g in range(4):
        outs.append(_rms(yy[:, g * 512:(g + 1) * 512]))
    return (jnp.concatenate(outs, axis=1) * snw,)


def _merge_fn(info, gates, pg, ps):
    half = gates.shape[1] // 2
    return (jax.nn.sigmoid(gates[:, :half]) * pg + jax.nn.sigmoid(gates[:, half:]) * ps,)


def _norm2_fn(info, xt, mo, g8, w, sc8, sh8):
    h1 = xt + g8[0:1] * mo
    return (h1, _rms(h1) * w * (1.0 + sc8[0:1]) + sh8[0:1])


def _swiglu_fn(info, u2):
    return (_silu(u2[0]) * u2[1],)


_NN = (((1,), (0,)), ((), ()))
_NT = (((1,), (1,)), ((), ()))
_TN = (((0,), (0,)), ((), ()))


def _mmh(a, b):
    return lax.dot_general(a, b, _NN, precision=HI, preferred_element_type=F32)


def _dot1(a, b, dims):
    return lax.dot_general(a.astype(_SCAN_DTYPE), b.astype(_SCAN_DTYPE), dims, preferred_element_type=F32)


def _mm(a, b):
    return _dot1(a, b, _NN)


def _mm_nt(a, b):
    return _dot1(a, b, _NT)


def _mm_tn(a, b):
    return _dot1(a, b, _TN)


def _split2(a):
    hi = a.astype(_SCAN_DTYPE)
    return hi, (a - hi.astype(F32)).astype(_SCAN_DTYPE)


def _dot3(a, b, dims):
    ah, al = _split2(a)
    bh, bl = _split2(b)
    d = lambda u, w: lax.dot_general(u, w, dims, preferred_element_type=F32)
    return d(ah, bh) + (d(ah, bl) + d(al, bh))


def _order_masks(d):
    i = lax.broadcasted_iota(jnp.int32, (CHUNK, CHUNK), 0)
    j = lax.broadcasted_iota(jnp.int32, (CHUNK, CHUNK), 1)
    s = jnp.where(d == 0, 1, -1) * (i - j)
    return (s >= 0).astype(F32), (s > 0).astype(F32)


@jax.custom_vjp
def _unit_tri_inv(mats):
    i = lax.broadcasted_iota(jnp.int32, (CHUNK, CHUNK), 0)
    j = lax.broadcasted_iota(jnp.int32, (CHUNK, CHUNK), 1)
    eye = (i == j).astype(F32)
    ps = [-a for a in mats]
    ts = [eye + p for p in ps]
    for _ in range(5):
        ps = [_dot3(p, p, _NN) for p in ps]
        ts = [t + _dot3(t, p, _NN) for t, p in zip(ts, ps)]
    return tuple(ts)


def _uti_fwd(mats):
    ts = _unit_tri_inv(mats)
    return ts, ts


def _uti_bwd(ts, gs):
    inner = [_dot3(g, t, _NT) for g, t in zip(gs, ts)]
    return (tuple(-_dot3(t, u, _TN) for t, u in zip(ts, inner)),)


_unit_tri_inv.defvjp(_uti_fwd, _uti_bwd)


@jax.custom_vjp
def _unit_tri_inv_given(mats, ts):
    return ts


def _utig_fwd(mats, ts):
    return ts, ts


def _utig_bwd(ts, gs):
    return _uti_bwd(ts, gs)[0], tuple(jnp.zeros_like(t) for t in ts)


_unit_tri_inv_given.defvjp(_utig_fwd, _utig_bwd)


def _lane_col(blk, lane_idx):
    lane = lax.broadcasted_iota(jnp.int32, blk.shape, 1)
    return jnp.sum(jnp.where(lane == lane_idx, blk, 0.0), axis=1, keepdims=True)


def _decay_mat(cum, incl):
    cb = jnp.broadcast_to(cum, (CHUNK, CHUNK))
    return jnp.exp(jnp.minimum(cb - cb.T, 0.0)) * incl


def _gdn_chunks(dirs, streams, states, aux=None, want_aux=False, group=None):
    ns = len(dirs)
    masks = [_order_masks(d) for d in dirs]
    cum = [_mmh(masks[i][0], streams[i][3]) for i in range(ns)]
    tot = [jnp.sum(streams[i][3], axis=0, keepdims=True) for i in range(ns)]
    all_units = [(i, h) for i in range(ns) for h in range(GDN_HEADS)]
    group = group or len(all_units)
    cat = jnp.concatenate
    outs, new_states, ts_all = [], [], []
    for g0 in range(0, len(all_units), group):
        units = all_units[g0:g0 + group]
        us = range(len(units))
        st = states[g0:g0 + group]
        qs = [streams[i][0][:, h * 128:(h + 1) * 128] for i, h in units]
        ks = [streams[i][1][:, h * 128:(h + 1) * 128] for i, h in units]
        vs = [streams[i][2][:, h * 128:(h + 1) * 128] for i, h in units]
        gcum = [_lane_col(cum[i], dirs[i] * GDN_HEADS + h) for i, h in units]
        glast = [_lane_col(tot[i], dirs[i] * GDN_HEADS + h) for i, h in units]
        beta = [_lane_col(streams[i][3], 16 + dirs[i] * GDN_HEADS + h) for i, h in units]
        decay = [_decay_mat(gcum[u], masks[units[u][0]][0]) for u in us]
        egc = [jnp.exp(gcum[u]) for u in us]
        kb = [ks[u] * beta[u] for u in us]
        kq = [_mm_nt(cat([kb[u], qs[u]], axis=0), ks[u]) for u in us]
        mats = tuple(kq[u][:CHUNK] * decay[u] * masks[units[u][0]][1] for u in us)
        ts = _unit_tri_inv(mats) if aux is None else _unit_tri_inv_given(mats, tuple(aux[g0:g0 + group]))
        wu = [_mm(ts[u], cat([kb[u] * egc[u], vs[u] * beta[u]], axis=1)) for u in us]
        ws = [_mm(cat([wu[u][:, :128], qs[u] * egc[u]], axis=0), st[u]) for u in us]
        vn = [wu[u][:, 128:] - ws[u][:CHUNK] for u in us]
        outs += [ws[u][CHUNK:] + _mm(kq[u][CHUNK:] * decay[u], vn[u]) for u in us]
        new_states += [st[u] * jnp.exp(glast[u]) + _mm_tn(ks[u] * jnp.exp(glast[u] - gcum[u]), vn[u]) for u in us]
        ts_all += list(ts)
    per_stream = [cat(outs[i * GDN_HEADS:(i + 1) * GDN_HEADS], axis=1) for i in range(ns)]
    return (*per_stream, *new_states, *(ts_all if want_aux else ()))


def _gdn_step(d, q, k, v, spb, *states, aux=None, want_aux=False):
    return _gdn_chunks([d], [(q, k, v, spb)], states, aux, want_aux, group=4)


def _gdn_step2(q0, k0, v0, sp0, q1, k1, v1, sp1, *states, aux=None, want_aux=False):
    return _gdn_chunks([0, 1], [(q0, k0, v0, sp0), (q1, k1, v1, sp1)], states, aux, want_aux)


def _split3(a):
    a1 = a.astype(_SCAN_DTYPE)
    r = a - a1.astype(F32)
    a2 = r.astype(_SCAN_DTYPE)
    return a1, a2, (r - a2.astype(F32)).astype(_SCAN_DTYPE)


def _exact_dot(a, e, dims, split_lhs, passes=3):
    parts = _split3(a if split_lhs else e)[:passes]
    d = (lambda u: lax.dot_general(u, e, dims, preferred_element_type=F32)) if split_lhs else \
        (lambda u: lax.dot_general(a, u, dims, preferred_element_type=F32))
    return functools.reduce(lambda u, w: u + w, [d(p) for p in reversed(parts)])


@jax.custom_vjp
def _spread(a, e):
    return _exact_dot(a, e, _NN, True)


def _spread_fwd(a, e):
    return _spread(a, e), e


def _spread_bwd(e, g):
    return _exact_dot(g, e, _NT, True, passes=2), jnp.zeros_like(e)


_spread.defvjp(_spread_fwd, _spread_bwd)


@jax.custom_vjp
def _colsum_bcast(z):
    return _exact_dot(jnp.ones((z.shape[0], z.shape[0]), _SCAN_DTYPE), z, _NN, False)


def _colsum_fwd(z):
    return _colsum_bcast(z), None


def _colsum_bwd(_, g):
    return (_exact_dot(jnp.ones((g.shape[0], g.shape[0]), _SCAN_DTYPE), g, _NN, False, passes=2),)


_colsum_bcast.defvjp(_colsum_fwd, _colsum_bwd)


def _ssd_consts():
    wdt = SSM_HEADS * 64
    d = lax.broadcasted_iota(jnp.int32, (2, 128, wdt), 0)
    e = (lax.broadcasted_iota(jnp.int32, (2, 128, wdt), 1)
         == 32 + d * SSM_HEADS + lax.broadcasted_iota(jnp.int32, (2, 128, wdt), 2) // 64).astype(_SCAN_DTYPE)
    dd = lax.broadcasted_iota(jnp.int32, (2, CHUNK, wdt), 0)
    ci = lax.broadcasted_iota(jnp.int32, (2, CHUNK, wdt), 1)
    pos = lax.broadcasted_iota(jnp.int32, (2, CHUNK, wdt), 2) % 64
    incl_t = (jnp.where(dd == 0, 1, -1) * (ci - pos) >= 0).astype(F32)
    diag = (ci == pos).astype(F32)
    return [e, incl_t, diag]


def _ssd_step(d, x, bm, cm, spb, alog, *states, consts):
    e, incl_t, diag = consts
    incl, _ = _order_masks(d)
    lane1 = lax.broadcasted_iota(jnp.int32, (1, 128), 1)
    lo_lane = 32 + d * SSM_HEADS
    a_vec = jnp.where(lane1 >= lo_lane, jnp.where(lane1 < lo_lane + SSM_HEADS, -jnp.exp(alog), 0.0), 0.0)
    adt = spb * a_vec
    acum = _mmh(incl, adt)
    alast = jnp.sum(adt, axis=0, keepdims=True)
    dt2 = _spread(spb, e)
    ac2 = _spread(acum, e)
    al2 = _spread(jnp.broadcast_to(alast, (8, 128)), e)[0:1]
    row = _colsum_bcast(ac2 * diag)
    seg = jnp.exp(jnp.minimum(ac2 - row, 0.0)) * incl_t
    xdt = x * dt2
    gam = jnp.exp(ac2)
    xe = xdt * jnp.exp(al2 - ac2)
    low = lax.broadcasted_iota(jnp.int32, (CHUNK, 128), 1) < 64
    row_low = lax.broadcasted_iota(jnp.int32, (128, 1), 0) < 64
    ps = range(SSM_PAIRS)
    sl = [slice(p * 128, (p + 1) * 128) for p in ps]
    bg = [bm[:, g * 128:(g + 1) * 128] for g in range(4)]
    cg = [cm[:, g * 128:(g + 1) * 128] for g in range(4)]
    cb2 = [_mm_nt(cg[g], jnp.concatenate([bg[g], bg[g]], axis=0)) for g in range(4)]
    ys, new_states = [], []
    for p0 in range(0, SSM_PAIRS, SSD_GROUP):
        pg = range(p0, p0 + SSD_GROUP)
        xd = {p: jnp.concatenate([jnp.where(low, xdt[:, sl[p]], 0.0), jnp.where(low, 0.0, xdt[:, sl[p]])], axis=0)
              for p in pg}
        yd = {p: _mm(cb2[p // 4] * seg[:, sl[p]], xd[p]) for p in pg}
        yo = {p: _mm_nt(cg[p // 4], states[p]) for p in pg}
        ys += [yd[p] + gam[:, sl[p]] * yo[p] for p in pg]
        new = {p: _mm_tn(xe[:, sl[p]], bg[p // 4]) for p in pg}
        al0 = {p: _lane_col(alast, lo_lane + 2 * p) for p in pg}
        al1 = {p: _lane_col(alast, lo_lane + 2 * p + 1) for p in pg}
        new_states += [states[p] * jnp.exp(jnp.where(row_low, al0[p], al1[p])) + new[p] for p in pg]
    return (jnp.concatenate(ys, axis=1), *new_states)


def _chunk_of(d, p, nctx, nc):
    return jnp.where(d == 0, p, jnp.where(p < nctx, nctx - 1 - p, nctx + nc - 1 - p))


class _NoRide:
    n, arrs, in_specs, out_specs, out_shape, scratch = 0, [], [], [], [], []


def _const_specs(consts):
    return [pl.BlockSpec((1,) + a.shape[1:], lambda d, p: (d,) + (0,) * (a.ndim - 1)) for a in consts]


def _scan_fwd(name, step, toks, pars, consts, out_width, n_state, nctx):
    t = toks[0].shape[0]
    nc = t // CHUNK
    n_tok, n_par, n_const = len(toks), len(pars), len(consts)

    def body(*refs):
        it = iter(refs)
        tok_refs = [next(it) for _ in range(n_tok)]
        par_refs = [next(it) for _ in range(n_par)]
        const_refs = [next(it) for _ in range(n_const)]
        o_ref, ss_ref, s_scr = next(it), next(it), next(it)
        d, p = pl.program_id(0), pl.program_id(1)

        @pl.when(p == 0)
        def _():
            s_scr[...] = jnp.zeros(s_scr.shape, F32)

        ss_ref[0, 0] = s_scr[...]
        res = step(d, *[r[...] for r in tok_refs], *[r[...] for r in par_refs], *[s_scr[h] for h in range(n_state)],
                   consts=[r[0] for r in const_refs])
        o_ref[0] = res[0]
        for h in range(n_state):
            s_scr[h] = res[1 + h]

    ch = lambda d, p: _chunk_of(d, p, nctx, nc)
    in_specs = [pl.BlockSpec((CHUNK, a.shape[1]), lambda d, p: (ch(d, p), 0)) for a in toks]
    in_specs += [pl.BlockSpec(a.shape, lambda d, p: (0, 0)) for a in pars]
    return pl.pallas_call(
        body, grid=(2, nc), in_specs=in_specs + _const_specs(consts),
        out_specs=[pl.BlockSpec((1, CHUNK, out_width), lambda d, p: (d, ch(d, p), 0)),
                   pl.BlockSpec((1, 1, n_state, 128, 128), lambda d, p: (d, p, 0, 0, 0))],
        out_shape=[jax.ShapeDtypeStruct((2, t, out_width), F32),
                   jax.ShapeDtypeStruct((2, nc, n_state, 128, 128), F32)],
        scratch_shapes=[pltpu.VMEM((n_state, 128, 128), F32)],
        compiler_params=_cp(("arbitrary", "arbitrary")), name=name)(*toks, *pars, *consts)


def _scan_bwd(name, step, toks, pars, ss, dout, n_state, nctx, ride=None, aux=None, consts=()):
    t = toks[0].shape[0]
    nc = t // CHUNK
    n_tok, n_par, n_const = len(toks), len(pars), len(consts)
    rd = ride if ride is not None else _NoRide
    n_aux = aux.shape[2] if aux is not None else 0

    def body(*refs):
        it = iter(refs)
        tok_refs = [next(it) for _ in range(n_tok)]
        par_refs = [next(it) for _ in range(n_par)]
        ss_ref, do_ref = next(it), next(it)
        aux_ref = next(it) if n_aux else None
        const_refs = [next(it) for _ in range(n_const)]
        ride_in = [next(it) for _ in range(rd.n)]
        dtok_refs = [next(it) for _ in range(n_tok)]
        dpar_refs = [next(it) for _ in range(n_par)]
        ride_out = [next(it) for _ in range(rd.n)]
        ds_scr = next(it)
        sems = list(it)
        d, pr = pl.program_id(0), pl.program_id(1)
        if ride is not None:
            @pl.when((d == 0) & (pr == 0))
            def _():
                ride.start(ride_in, ride_out, sems)

            @pl.when((d == 1) & (pr == nc - 1))
            def _():
                ride.wait(ride_in, ride_out, sems)

        @pl.when(pr == 0)
        def _():
            ds_scr[...] = jnp.zeros(ds_scr.shape, F32)

        kw = dict(aux=[aux_ref[0, 0, i] for i in range(n_aux)]) if n_aux else {}
        if n_const:
            kw['consts'] = [r[0] for r in const_refs]
        _, pull = jax.vjp(functools.partial(step, d, **kw), *[r[...] for r in tok_refs], *[r[...] for r in par_refs],
                          *[ss_ref[0, 0, h] for h in range(n_state)])
        grads = pull((do_ref[...], *[ds_scr[h] for h in range(n_state)]))
        for r, g in zip(dtok_refs, grads[:n_tok]):
            r[0] = g
        for h in range(n_state):
            ds_scr[h] = grads[n_tok + n_par + h]
        first = (d == 0) & (pr == 0)
        for r, g in zip(dpar_refs, grads[n_tok:n_tok + n_par]):
            @pl.when(first)
            def _(r=r, g=g):
                r[...] = g

            @pl.when(jnp.logical_not(first))
            def _(r=r, g=g):
                r[...] += g

    ch = lambda d, pr: _chunk_of(d, nc - 1 - pr, nctx, nc)
    in_specs = [pl.BlockSpec((CHUNK, a.shape[1]), lambda d, pr: (ch(d, pr), 0)) for a in toks]
    in_specs += [pl.BlockSpec(a.shape, lambda d, pr: (0, 0)) for a in pars]
    in_specs += [pl.BlockSpec((1, 1, n_state, 128, 128), lambda d, pr: (d, nc - 1 - pr, 0, 0, 0)),
                 pl.BlockSpec((CHUNK, dout.shape[1]), lambda d, pr: (ch(d, pr), 0))]
    if n_aux:
        in_specs += [pl.BlockSpec((1, 1, n_aux, CHUNK, CHUNK), lambda d, pr: (d, nc - 1 - pr, 0, 0, 0))]
    in_specs += _const_specs(consts)
    out_specs = [pl.BlockSpec((1, CHUNK, a.shape[1]), lambda d, pr: (d, ch(d, pr), 0)) for a in toks]
    out_specs += [pl.BlockSpec(a.shape, lambda d, pr: (0, 0)) for a in pars]
    out_shape = [jax.ShapeDtypeStruct((2, t, a.shape[1]), F32) for a in toks]
    out_shape += [jax.ShapeDtypeStruct(a.shape, F32) for a in pars]
    return pl.pallas_call(
        body, grid=(2, nc), in_specs=in_specs + rd.in_specs, out_specs=out_specs + rd.out_specs,
        out_shape=out_shape + rd.out_shape, scratch_shapes=[pltpu.VMEM((n_state, 128, 128), F32)] + rd.scratch,
        compiler_params=_cp(("arbitrary", "arbitrary")), name=name)(
            *toks, *pars, ss, dout, *([aux] if n_aux else []), *consts, *rd.arrs)


def _scan2_fwd(name, step2, toks, out_width, n_state, nctx, n_aux, ride=None):
    t = toks[0].shape[0]
    nc = t // CHUNK
    n_tok = len(toks)
    rd = ride if ride is not None else _NoRide

    def body(*refs):
        it = iter(refs)
        tok_refs = [next(it) for _ in range(2 * n_tok)]
        ride_in = [next(it) for _ in range(rd.n)]
        o_refs = [next(it), next(it)]
        ss_ref, aux_ref = next(it), next(it)
        ride_out = [next(it) for _ in range(rd.n)]
        s_scr = next(it)
        sems = list(it)
        p = pl.program_id(0)
        if ride is not None:
            @pl.when(p == 0)
            def _():
                ride.start(ride_in, ride_out, sems)

        @pl.when(p == 0)
        def _():
            s_scr[...] = jnp.zeros(s_scr.shape, F32)

        for d in range(2):
            ss_ref[d, 0] = s_scr[d * n_state:(d + 1) * n_state]
        res = step2(*[r[...] for r in tok_refs], *[s_scr[u] for u in range(2 * n_state)], want_aux=True)
        for d in range(2):
            o_refs[d][...] = res[d]
            for i in range(n_aux):
                aux_ref[d, 0, i] = res[2 + 2 * n_state + d * n_aux + i]
        for u in range(2 * n_state):
            s_scr[u] = res[2 + u]
        if ride is not None:
            @pl.when(p == nc - 1)
            def _():
                ride.wait(ride_in, ride_out, sems)

    def tok_spec(a, d):
        return pl.BlockSpec((CHUNK, a.shape[1]), lambda p: (_chunk_of(d, p, nctx, nc), 0))

    return pl.pallas_call(
        body, grid=(nc,), in_specs=[tok_spec(a, d) for d in range(2) for a in toks] + rd.in_specs,
        out_specs=[pl.BlockSpec((CHUNK, out_width), lambda p: (_chunk_of(0, p, nctx, nc), 0)),
                   pl.BlockSpec((CHUNK, out_width), lambda p: (_chunk_of(1, p, nctx, nc), 0)),
                   pl.BlockSpec((2, 1, n_state, 128, 128), lambda p: (0, p, 0, 0, 0)),
                   pl.BlockSpec((2, 1, n_aux, CHUNK, CHUNK), lambda p: (0, p, 0, 0, 0))] + rd.out_specs,
        out_shape=[jax.ShapeDtypeStruct((t, out_width), F32), jax.ShapeDtypeStruct((t, out_width), F32),
                   jax.ShapeDtypeStruct((2, nc, n_state, 128, 128), F32),
                   jax.ShapeDtypeStruct((2, nc, n_aux, CHUNK, CHUNK), F32)] + rd.out_shape,
        scratch_shapes=[pltpu.VMEM((2 * n_state, 128, 128), F32)] + rd.scratch,
        compiler_params=_cp(("arbitrary",)), name=name)(*toks, *toks, *rd.arrs)


ADA_TN = 512


def _ada_part(cc16, w_shard):
    n = w_shard.shape[1]

    def body(c_ref, w_ref, o_ref):
        s = _silu(c_ref[...]).astype(_MXU_DTYPE)
        o_ref[...] = jnp.dot(s, w_ref[...].astype(_MXU_DTYPE), preferred_element_type=F32)

    return pl.pallas_call(
        body, grid=(n // ADA_TN,),
        in_specs=[pl.BlockSpec((16, D_MODEL), lambda j: (0, 0)), pl.BlockSpec((D_MODEL, ADA_TN), lambda j: (0, j))],
        out_specs=pl.BlockSpec((16, ADA_TN), lambda j: (0, j)), out_shape=jax.ShapeDtypeStruct((16, n), F32),
        compiler_params=_cp(("parallel",)), name="ada_part")(cc16, w_shard)


def _ada_bwd_shard(cc16, w_shard, d_lat, d_ctx):
    n = w_shard.shape[1]
    nj = n // ADA_TN

    def body(c_ref, w_ref, gl_ref, gc_ref, dw_ref, dc_ref):
        j = pl.program_id(0)
        s, pull = jax.vjp(_silu, c_ref[...])
        row = lax.broadcasted_iota(jnp.int32, (8, 1), 0)
        dctx = jnp.where(row == 0, jnp.sum(gc_ref[...], axis=0, keepdims=True), 0.0)
        mxu = lambda v: v.astype(_MXU_DTYPE)
        dw_ref[...] = (lax.dot_general(mxu(s[0:8]), mxu(gl_ref[...]), _TN, preferred_element_type=F32)
                       + lax.dot_general(mxu(s[8:16]), mxu(dctx), _TN, preferred_element_type=F32))
        ds = lax.dot_general(mxu(dctx), mxu(w_ref[...]), _NT, preferred_element_type=F32)

        @pl.when(j == 0)
        def _():
            dc_ref[...] = ds

        @pl.when(j > 0)
        def _():
            dc_ref[...] += ds

        @pl.when(j == nj - 1)
        def _():
            ct = jnp.concatenate([jnp.zeros((8, D_MODEL), F32), dc_ref[...]], axis=0)
            dc_ref[...] = 0.5 * pull(ct)[0][8:16]

    tile = pl.BlockSpec((8, ADA_TN), lambda j: (0, j))
    wspec = pl.BlockSpec((D_MODEL, ADA_TN), lambda j: (0, j))
    return pl.pallas_call(
        body, grid=(nj,), in_specs=[pl.BlockSpec((16, D_MODEL), lambda j: (0, 0)), wspec, tile, tile],
        out_specs=[wspec, pl.BlockSpec((8, D_MODEL), lambda j: (0, 0))],
        out_shape=[jax.ShapeDtypeStruct((D_MODEL, n), F32), jax.ShapeDtypeStruct((8, D_MODEL), F32)],
        compiler_params=_cp(("arbitrary",)), name="ada_bwd_shard")(cc16, w_shard, d_lat, d_ctx)


def _rowsum2(a, b):
    def body(a_ref, b_ref, o_ref):
        o_ref[...] = jnp.sum(a_ref[...], axis=0, keepdims=True) + jnp.sum(b_ref[...], axis=0, keepdims=True)

    return pl.pallas_call(body, out_shape=jax.ShapeDtypeStruct((1, a.shape[1]), F32), name="ada_b_grad")(a, b)


def _tail(h1, ff, mods, wf, tgt, nctx_t, tl):
    t = h1.shape[0]
    nt = t // tl

    def loss_fn(valid, h1v, ffv, g8, w, tg):
        h2 = h1v + g8[0:1] * ffv
        y = _rms(h2) * w
        err = (y - tg) ** 2
        return 0.5 * jnp.sum(jnp.mean(err, axis=-1, keepdims=True), axis=0, keepdims=True) * valid

    def body(h1_ref, ff_ref, g_ref, w_ref, t_ref, loss_ref, dh_ref, dff_ref, dg_ref, dw_ref):
        i = pl.program_id(0)
        valid = jnp.where(i < nctx_t, 0.0, 1.0)
        lv, pull = jax.vjp(functools.partial(loss_fn, valid), h1_ref[...], ff_ref[...].astype(F32), g_ref[...],
                           w_ref[...], t_ref[...])
        dh, dff, dg, dw, _ = pull(jnp.ones((1, 1), F32))
        dh_ref[...] = dh
        dff_ref[...] = dff.astype(dff_ref.dtype)
        lb = jnp.broadcast_to(lv, loss_ref.shape)

        @pl.when(i == 0)
        def _():
            loss_ref[...] = lb
            dg_ref[...] = dg
            dw_ref[...] = dw

        @pl.when(i > 0)
        def _():
            loss_ref[...] += lb
            dg_ref[...] += dg
            dw_ref[...] += dw

    tok = pl.BlockSpec((tl, D_MODEL), lambda i: (i, 0))
    return pl.pallas_call(
        body, grid=(nt,),
        in_specs=[tok, tok, pl.BlockSpec((8, D_MODEL), lambda i: (0, 5)), pl.BlockSpec((1, D_MODEL), lambda i: (0, 0)),
                  pl.BlockSpec((tl, D_MODEL), lambda i: (jnp.maximum(i - nctx_t, 0), 0))],
        out_specs=[pl.BlockSpec((8, 128), lambda i: (0, 0)), tok, tok, pl.BlockSpec((8, D_MODEL), lambda i: (0, 0)),
                   pl.BlockSpec((1, D_MODEL), lambda i: (0, 0))],
        out_shape=[jax.ShapeDtypeStruct((8, 128), F32), jax.ShapeDtypeStruct((t, D_MODEL), F32),
                   jax.ShapeDtypeStruct((t, D_MODEL), _MXU_DTYPE), jax.ShapeDtypeStruct((8, D_MODEL), F32),
                   jax.ShapeDtypeStruct((1, D_MODEL), F32)],
        compiler_params=_cp(("arbitrary",)), name="tail_loss")(h1, ff, mods, wf, tgt)


def _pack_w_in(w4):
    ns = w4.shape[2]
    placed = []
    for s0, s1, p0 in IN_SEGMENTS:
        for j in range(N_PLANE):
            lo, hi = max(s0, j * ns), min(s1, (j + 1) * ns)
            if lo < hi:
                placed.append((p0 + lo - s0, w4[j][:, lo - j * ns:hi - j * ns]))
    placed.sort(key=lambda e: e[0])
    pieces, end = [], 0
    for pos, piece in placed:
        assert pos == end, (pos, end)
        pieces.append(piece)
        end = pos + piece.shape[1]
    pieces.append(jnp.zeros((w4.shape[1], P_TOTAL - end), w4.dtype))
    return jnp.concatenate(pieces, axis=1)


def _unpack_w_in(g):
    ns = D_IN_PROJ // N_PLANE
    shards = []
    for j in range(N_PLANE):
        pieces = []
        for s0, s1, p0 in IN_SEGMENTS:
            lo, hi = max(s0, j * ns), min(s1, (j + 1) * ns)
            if lo < hi:
                pieces.append(g[:, p0 + lo - s0:p0 + hi - s0])
        shards.append(jnp.concatenate(pieces, axis=1))
    return jnp.stack(shards)


LATE_WEIGHTS = ['w_br_gdn', 'w_br_ssm', 'w_out', 'w_ffn_in', 'w_ffn_out']
COL_STACKED = ('w_in', 'w_ffn_in')


def _from_shards(n, g):
    if n in COL_STACKED:
        return g
    if SHARD_AXIS[n] == 0:
        return g.reshape(N_PLANE * g.shape[1], g.shape[2])
    return jnp.concatenate([g[j] for j in range(N_PLANE)], axis=1)


def _to_dest_blocks(n, g):
    if g.ndim == 3:
        return g
    if SHARD_AXIS[n] == 0:
        return g.reshape(N_PLANE, g.shape[0] // N_PLANE, g.shape[1])
    sz = g.shape[1] // N_PLANE
    return jnp.stack([g[:, j * sz:(j + 1) * sz] for j in range(N_PLANE)])


def _local_step(x, c, ctx, tgt, W, late_shards):
    lc, l = ctx.shape[0], x.shape[0]
    t = lc + l
    tl = 256
    assert lc == tl and l % tl == 0 and lc % CHUNK == 0
    nt, nctx_t, nctx = t // tl, lc // tl, lc // CHUNK
    act = _MXU_DTYPE
    r1 = lambda v: v.reshape(1, -1)

    xt = jnp.concatenate([ctx, x], axis=0)
    lin = 4 * lax.axis_index("x") + 2 * lax.axis_index("y") + lax.axis_index("c")
    c_all = _gather_all("gather_c", jnp.broadcast_to(c, (8, D_MODEL)))[:, 0]
    cc16 = jnp.concatenate([c_all, r1(W['c_ctx']), jnp.zeros((7, D_MODEL), F32)], axis=0)
    (parts,) = _plane_exchange("gather_mods", [_ada_part(cc16, W['ada_w'])], gather=True)
    mods_all = jnp.transpose(parts, (1, 0, 2)).reshape(16, -1) + r1(W['ada_b'])
    mods = jnp.concatenate([lax.dynamic_slice_in_dim(mods_all, lin, 1, axis=0), mods_all[8:9],
                            jnp.zeros((6, mods_all.shape[1]), F32)], axis=0)

    def mod(kk):
        return Op(mods, (8, D_MODEL), lambda j, i, kk=kk: (0, kk))

    def tokop(arr, w=D_MODEL, off=0, tl_=tl):
        if not isinstance(arr, list) and arr.ndim == 3:
            return Op(arr, (2, tl_, w), lambda j, i: (0, i, off + j), load=_sum_dirs)
        return Op(arr, (tl_, w), lambda j, i: (i, off + j))

    def outop(n, dtype, w=D_MODEL, tl_=tl):
        return Op(((t, n), dtype), (tl_, w), lambda j, i: (i, j))

    def parop(arr, w, off=0):
        return Op(arr, (arr.shape[0], w), lambda j, i: (0, off + j))

    def parout(rows, n, w):
        return Op(((rows, n), F32), (rows, w), lambda j, i: (0, j))

    n1w = r1(W['norm1_w'])
    pre_fn = functools.partial(_prenorm_fn, nctx_t)
    pre_tok, pre_par = [tokop(xt)], [parop(n1w, D_MODEL), mod(1), mod(0)]
    (a,) = _tw_fwd("prenorm_fwd", pre_fn, (1, nt), pre_tok + pre_par, [outop(D_MODEL, act)])
    wp = _pack_w_in(W['w_in'])
    proj = _matmul(a, wp, 'nn', "in_proj")

    gcw, gcb = W['gdn_conv_w'], r1(W['gdn_conv_b'])
    scw, scb = W['ssm_conv_w'], r1(W['ssm_conv_b'])
    conv_parts = {}

    def conv_part(name, mode, poff, cw, cb, woff, width):
        fn = functools.partial(_conv_fn, nctx_t, mode)
        bw = min(width, 1024)
        tok_ = [tokop(proj, bw, poff // bw)]
        par_ = [parop(cw, bw, woff // bw), parop(cb, bw, woff // bw)]
        conv_parts[name] = (fn, tok_, par_, width, bw, poff)
        (o,) = _tw_fwd("conv_" + name, fn, (width // bw, nt), tok_ + par_, [outop(width, F32, bw)])
        return o

    q = conv_part('q', 'q', P_QKV, gcw, gcb, 0, 1024)
    k = conv_part('k', 'k', P_QKV + 1024, gcw, gcb, 1024, 1024)
    v = conv_part('v', 'none', P_QKV + 2048, gcw, gcb, 2048, 1024)
    xs = conv_part('xs', 'none', P_XBC, scw, scb, 0, 2048)
    bm = conv_part('bm', 'none', P_XBC + 2048, scw, scb, 2048, 512)
    cm = conv_part('cm', 'none', P_XBC + 2560, scw, scb, 2560, 512)

    z16, z32 = jnp.zeros((16,), F32), jnp.zeros((32,), F32)
    p0 = jnp.concatenate([W['gdn_a_log'].reshape(-1), jnp.zeros((112,), F32)]).reshape(1, 128)
    p1 = jnp.concatenate([W['gdn_dt_bias'].reshape(-1), z16, W['ssm_dt_bias'].reshape(-1), z32]).reshape(1, 128)
    alog = jnp.concatenate([z32, W['ssm_a_log'].reshape(-1), z32]).reshape(1, 128)
    act_tok, act_par = [tokop(proj, 128, P_SMALL // 128)], [parop(p0, 128), parop(p1, 128)]
    (sp,) = _tw_fwd("small_act", _act_fn, (1, nt), act_tok + act_par, [outop(128, F32, 128)])

    gather_late = PlaneExchange([late_shards[n] for n in LATE_WEIGHTS], gather=True)
    o_f, o_b, ss, tri, *late = _scan2_fwd("gdn_scan_fwd", _gdn_step2, [q, k, v, sp], 1024, GDN_HEADS, nctx,
                                          GDN_HEADS, ride=gather_late)
    W = dict(W, **{n: _from_shards(n, g) for n, g in zip(LATE_WEIGHTS, late)})
    ssd_consts = _ssd_consts()
    y2, hs = _scan_fwd("ssd_scan_fwd", _ssd_step, [xs, bm, cm, sp], [alog], ssd_consts, 2048, SSM_PAIRS, nctx)

    tlm = 128
    ntm = t // tlm
    gnw = r1(W['gdn_norm_w'])
    dl = jnp.repeat(W['ssm_d'], 64).reshape(1, 2048)
    snw = r1(W['ssm_norm_w'])
    mg_tok = [tokop([o_f, o_b], 1024, 0, tlm), tokop(proj, 1024, P_ZG // 1024, tlm)]
    mg_par = [parop(gnw, 128)]
    (og,) = _tw_fwd("mix_gdn", _mixg_fn, (1, ntm), mg_tok + mg_par, [outop(1024, act, 1024, tlm)])
    ms_tok = [tokop(y2, 2048, 0, tlm), tokop(xs, 2048, 0, tlm), tokop(proj, 2048, P_ZS // 2048, tlm)]
    ms_par = [parop(dl, 2048), parop(snw, 2048)]
    (yg,) = _tw_fwd("mix_ssm", _mixs_fn, (1, ntm), ms_tok + ms_par, [outop(2048, act, 2048, tlm)])

    pg = _matmul(og, W['w_br_gdn'], 'nn', "br_gdn", out_dtype=act)
    ps = _matmul(yg, W['w_br_ssm'], 'nn', "br_ssm", out_dtype=act)
    mr_tok = [tokop(proj, 2048, P_GATE // 2048), tokop(pg), tokop(ps)]
    (mrg,) = _tw_fwd("merge", _merge_fn, (1, nt), mr_tok, [outop(1024, act)])
    mo = _matmul(mrg, W['w_out'], 'nn', "w_out")

    n2w = r1(W['norm2_w'])
    n2_tok, n2_par = [tokop(xt), tokop(mo)], [mod(2), parop(n2w, D_MODEL), mod(4), mod(3)]
    h1, f = _tw_fwd("norm2_fwd", _norm2_fn, (1, nt), n2_tok + n2_par, [outop(1024, F32), outop(1024, act)])
    u2 = _matmul(f, W['w_ffn_in'], 'nn', "ffn_in", out_dtype=act, halves='out')
    swb = D_FF // 2
    both = lambda arr: Op(arr, (2, tl, swb), lambda j, i: (0, i, j))
    sw_tok = [both(u2)]
    (sw,) = _tw_fwd("swiglu", _swiglu_fn, (D_FF // swb, nt), sw_tok, [outop(D_FF, act, swb)])
    ff = _matmul(sw, W['w_ffn_out'], 'nn', "ffn_out")

    loss8, dh1, dff, dg2, dnf = _tail(h1, ff, mods, r1(W['norm_f_w']), tgt, nctx_t, tl)
    loss = loss8[0, 0]

    G = {}
    G['norm_f_w'] = dnf.reshape(-1)
    G['w_ffn_out'] = _matmul(sw, dff, 'tn', "d_ffn_out", out_dtype=_GRAD_DTYPE)
    dsw = _matmul(dff, W['w_ffn_out'], 'nt', "d_sw", out_dtype=act)
    (du2,) = _tw_bwd("swiglu_bwd", _swiglu_fn, (D_FF // swb, nt), sw_tok, [], [[tokop(dsw, swb)]],
                     [both(((2, t, D_FF), act))], [])
    G['w_ffn_in'] = _matmul(f, du2, 'tn', "d_ffn_in", out_dtype=_GRAD_DTYPE, stacked_out=True, halves='b')
    df = _matmul(du2, W['w_ffn_in'], 'nt', "d_f", halves='a')
    dxt1, dmo, dg1, dn2, dsc2, dsh2 = _tw_bwd(
        "norm2_bwd", _norm2_fn, (1, nt), n2_tok, n2_par, [[tokop(dh1)], [tokop(df)]],
        [outop(1024, F32), outop(1024, act)],
        [parout(8, 1024, 1024), parout(1, 1024, 1024), parout(8, 1024, 1024), parout(8, 1024, 1024)])
    G['norm2_w'] = dn2.reshape(-1)
    G['w_out'] = _matmul(mrg, dmo, 'tn', "d_w_out", out_dtype=_GRAD_DTYPE)
    dmrg = _matmul(dmo, W['w_out'], 'nt', "d_mrg", out_dtype=act)
    def win(off, w, tl_=tl):
        return Op(((t, P_TOTAL), act), (tl_, w), lambda j, i: (i, off // w + j))

    dproj = lax.empty((t, P_TOTAL), act)
    dproj, dpg, dps = _tw_bwd("merge_bwd", _merge_fn, (1, nt), mr_tok, [], [[tokop(dmrg)]],
                              [win(P_GATE, 2048), outop(1024, act), outop(1024, act)], [], into={0: dproj})
    G['w_br_gdn'] = _matmul(og, dpg, 'tn', "d_br_gdn", out_dtype=_GRAD_DTYPE)
    G['w_br_ssm'] = _matmul(yg, dps, 'tn', "d_br_ssm", out_dtype=_GRAD_DTYPE)
    dog = _matmul(dpg, W['w_br_gdn'], 'nt', "d_og")
    dyg = _matmul(dps, W['w_br_ssm'], 'nt', "d_yg")

    do, dproj, dgnw = _tw_bwd("mix_gdn_bwd", _mixg_fn, (1, ntm), mg_tok, mg_par, [[tokop(dog, 1024, 0, tlm)]],
                              [outop(1024, F32, 1024, tlm), win(P_ZG, 1024, tlm)], [parout(1, 128, 128)],
                              into={1: dproj})
    G['gdn_norm_w'] = dgnw.reshape(-1)
    dy, dxs_a, dproj, ddl, dsnw = _tw_bwd(
        "mix_ssm_bwd", _mixs_fn, (1, ntm), ms_tok, ms_par, [[tokop(dyg, 2048, 0, tlm)]],
        [outop(2048, F32, 2048, tlm), outop(2048, F32, 2048, tlm), win(P_ZS, 2048, tlm)],
        [parout(1, 2048, 2048), parout(1, 2048, 2048)], into={2: dproj})
    G['ssm_d'] = ddl.reshape(SSM_HEADS, 64).sum(axis=1)
    G['ssm_norm_w'] = dsnw.reshape(-1)

    dq2, dk2, dv2, dsp_g = _scan_bwd("gdn_scan_bwd", _gdn_step, [q, k, v, sp], [], ss, do, GDN_HEADS, nctx, aux=tri)
    received = {}
    scatter_late = PlaneExchange([_to_dest_blocks(n, G.pop(n)) for n in LATE_WEIGHTS], gather=False)
    dxs2, dbm2, dcm2, dsp_s, dalog, *got = _scan_bwd("ssd_scan_bwd", _ssd_step, [xs, bm, cm, sp], [alog], hs, dy,
                                                     SSM_PAIRS, nctx, ride=scatter_late, consts=ssd_consts)
    received.update(zip(LATE_WEIGHTS, got))
    G['ssm_a_log'] = dalog[0, 32:96].reshape(2, SSM_HEADS)

    dconv_w, dconv_b = {}, {}

    def conv_bwd(name, cots_, buf):
        fn, tok_, par_, width, bw, poff = conv_parts[name]
        buf, dconv_w[name], dconv_b[name] = _tw_bwd(
            "conv_" + name + "_bwd", fn, (width // bw, nt), tok_, par_, [[tokop(c_, bw) for c_ in cots_]],
            [win(poff, bw)], [parout(3, width, bw), parout(1, width, bw)], into={0: buf})
        return buf

    dproj = conv_bwd('q', [dq2], dproj)
    dproj = conv_bwd('k', [dk2], dproj)
    dproj = conv_bwd('v', [dv2], dproj)
    dproj = conv_bwd('xs', [dxs2, dxs_a], dproj)
    dproj = conv_bwd('bm', [dbm2], dproj)
    dproj = conv_bwd('cm', [dcm2], dproj)
    G['gdn_conv_w'] = jnp.concatenate([dconv_w['q'], dconv_w['k'], dconv_w['v']], axis=1)
    G['gdn_conv_b'] = [dconv_b['q'], dconv_b['k'], dconv_b['v']]
    G['ssm_conv_w'] = jnp.concatenate([dconv_w['xs'], dconv_w['bm'], dconv_w['cm']], axis=1)
    G['ssm_conv_b'] = [dconv_b['xs'], dconv_b['bm'], dconv_b['cm']]

    dproj, dp0, dp1 = _tw_bwd("small_act_bwd", _act_fn, (1, nt), act_tok, act_par,
                              [[tokop(dsp_g, 128), tokop(dsp_s, 128)]], [win(P_SMALL, P_TOTAL - P_SMALL)],
                              [parout(1, 128, 128), parout(1, 128, 128)], into={0: dproj})
    G['gdn_a_log'] = dp0[0, 0:16].reshape(2, GDN_HEADS)
    G['gdn_dt_bias'] = dp1[0, 0:16].reshape(2, GDN_HEADS)
    G['ssm_dt_bias'] = dp1[0, 32:96].reshape(2, SSM_HEADS)

    g_w_in = _unpack_w_in(_matmul(a, dproj, 'tn', "d_w_in", out_dtype=_GRAD_DTYPE))
    da, (received['w_in'],) = _matmul(dproj, wp, 'nt', "d_a", ride=PlaneExchange([g_w_in], gather=False))

    gx_out = Op(((l, D_MODEL), F32), (tl, D_MODEL), lambda j, i: (jnp.maximum(i - nctx_t, 0), 0))
    grad_x, dn1, dsc1, dsh1 = _tw_bwd(
        "prenorm_bwd", pre_fn, (1, nt), pre_tok, pre_par, [[tokop(da)]], [gx_out],
        [parout(1, 1024, 1024), parout(8, 1024, 1024), parout(8, 1024, 1024)], tok_add=tokop(dxt1),
        sem=("arbitrary", "arbitrary"))
    G['norm1_w'] = dn1.reshape(-1)
    dmods = jnp.concatenate([dsh1, dsc1, dg1, dsh2, dsc2, dg2], axis=1)
    dm_all = _gather_all("gather_dmods", dmods)
    ns = W['ada_w'].shape[1]
    mine = lax.dynamic_slice_in_dim(dm_all, (2 * lax.axis_index("x") + lax.axis_index("y")) * ns, ns, axis=2)
    G['ada_w'], dcc = _ada_bwd_shard(cc16, W['ada_w'], mine[:, 0], mine[:, 1])
    G['ada_b'] = jnp.where(lin == 0, _rowsum2(dm_all[:, 0], dm_all[:, 1]).reshape(-1), 0.0)
    G['c_ctx'] = dcc[0]
    return loss, grad_x, G, received


def _row_tile(r, c):
    for tr in (512, 256, 128, 64, 32, 16, 8):
        if r % tr == 0 and tr * c * 4 <= (1 << 20):
            return tr
    return r


def _sum4(name, rv):
    _, r, c = rv.shape
    tr = _row_tile(r, c)

    def body(r_ref, o_ref):
        o_ref[...] = ((r_ref[0].astype(F32) + r_ref[1].astype(F32)) + r_ref[2].astype(F32)) + r_ref[3].astype(F32)

    return pl.pallas_call(
        body, grid=(r // tr,), in_specs=[pl.BlockSpec((N_PLANE, tr, c), lambda i: (0, i, 0))],
        out_specs=pl.BlockSpec((tr, c), lambda i: (i, 0)), out_shape=jax.ShapeDtypeStruct((r, c), F32),
        compiler_params=_cp(("parallel",)), name=name)(rv)


def _adamw(name, w, m, v, p, q=None):
    r, c = w.shape[-2:]
    tr = _row_tile(r, c)
    grads = [p] if q is None else [p, q]
    lead = (0,) * (w.ndim - 2)

    def body(w_ref, m_ref, v_ref, *refs):
        g_ref, d_ref, mo_ref, vo_ref = refs[len(grads):]
        g = refs[0][...] if q is None else refs[0][...] + refs[1][...]
        at = lead if lead else Ellipsis
        mn = ADAM_B1 * m_ref[at] + (1.0 - ADAM_B1) * g
        vn = ADAM_B2 * v_ref[at] + (1.0 - ADAM_B2) * jnp.square(g)
        m_hat = mn / (1.0 - ADAM_B1 ** ADAM_STEP)
        v_hat = vn / (1.0 - ADAM_B2 ** ADAM_STEP)
        g_ref[at] = g
        d_ref[at] = -ADAM_LR * (m_hat / (jnp.sqrt(v_hat) + ADAM_EPS) + ADAM_WD * w_ref[at])
        mo_ref[at] = mn
        vo_ref[at] = vn

    gspec = pl.BlockSpec((tr, c), lambda i: (i, 0))
    wspec = pl.BlockSpec((1,) * len(lead) + (tr, c), lambda i: lead + (i, 0))
    return pl.pallas_call(
        body, grid=(r // tr,), in_specs=[wspec] * 3 + [gspec] * len(grads), out_specs=[wspec] * 4,
        out_shape=[jax.ShapeDtypeStruct(w.shape, F32)] * 4, compiler_params=_cp(("parallel",)), name=name)(w, m, v, *grads)


SMALL_ROWS = 24


def _pack_small(d):
    pieces = [p.reshape(-1) for n in SMALL for p in (d[n] if isinstance(d[n], list) else [d[n]])]
    v = jnp.concatenate(pieces)
    return jnp.pad(v, (0, SMALL_ROWS * 1024 - v.shape[0])).reshape(SMALL_ROWS, 1024)


def _unpack_small(buf, like):
    v = buf.reshape(-1)
    out, off = {}, 0
    for n in SMALL:
        sz = like[n].size
        out[n] = v[off:off + sz].reshape(like[n].shape)
        off += sz
    return out


def kernel(x, c, ctx, c_ctx, ada_w, ada_b, norm1_w, w_in, gdn_conv_w, gdn_conv_b, gdn_a_log, gdn_dt_bias, gdn_norm_w, ssm_conv_w, ssm_conv_b, ssm_a_log, ssm_dt_bias, ssm_d, ssm_norm_w, w_br_gdn, w_br_ssm, w_out, norm2_w, w_ffn_in, w_ffn_out, norm_f_w, loss_target, m_c_ctx, m_ada_w, m_ada_b, m_norm1_w, m_w_in, m_gdn_conv_w, m_gdn_conv_b, m_gdn_a_log, m_gdn_dt_bias, m_gdn_norm_w, m_ssm_conv_w, m_ssm_conv_b, m_ssm_a_log, m_ssm_dt_bias, m_ssm_d, m_ssm_norm_w, m_w_br_gdn, m_w_br_ssm, m_w_out, m_norm2_w, m_w_ffn_in, m_w_ffn_out, m_norm_f_w, v_c_ctx, v_ada_w, v_ada_b, v_norm1_w, v_w_in, v_gdn_conv_w, v_gdn_conv_b, v_gdn_a_log, v_gdn_dt_bias, v_gdn_norm_w, v_ssm_conv_w, v_ssm_conv_b, v_ssm_a_log, v_ssm_dt_bias, v_ssm_d, v_ssm_norm_w, v_w_br_gdn, v_w_br_ssm, v_w_out, v_norm2_w, v_w_ffn_in, v_w_ffn_out, v_norm_f_w):
    args = dict(locals())
    wl = {n: args[n] for n in WEIGHTS}
    ml = {n: args['m_' + n] for n in WEIGHTS}
    vl = {n: args['v_' + n] for n in WEIGHTS}

    def nodepth(n, a):
        return a if n in ('c_ctx', 'norm_f_w') else a[0]

    shard = {n: nodepth(n, wl[n]).astype(_MXU_DTYPE if n in MXU_WEIGHTS else F32) for n in SHARDED}
    first = [n for n in SHARDED if n not in LATE_WEIGHTS and n != 'ada_w']
    W = {n: nodepth(n, wl[n]) for n in SMALL}
    W['ada_w'] = shard['ada_w']
    for n, g in zip(first, _gather_two_level("all_gather_plane", [shard[n] for n in first])):
        W[n] = _from_shards(n, g)

    loss_local, grad_x, G, received = _local_step(x[0], c, ctx[0], loss_target[0], W,
                                                  late_shards={n: shard[n] for n in LATE_WEIGHTS})
    loss = lax.psum(loss_local, ("x", "y", "c"))

    small_g = _pack_small(G)
    last = [n for n in SHARDED if n not in received and n != 'ada_w']
    send = [_to_dest_blocks(n, G[n]) for n in last] + [jnp.broadcast_to(small_g[None], (N_PLANE,) + small_g.shape)]
    received.update(zip(last + ['small'], _plane_exchange("scatter_plane", send, gather=False)))
    names = [n for n in SHARDED if n != 'ada_w'] + ['small']
    plane_sum = [_sum4("sum4_" + n, received[n]) for n in names]
    other = _swap_sibling(plane_sum)

    wd = {n: nodepth(n, wl[n]) for n in WEIGHTS}
    md = {n: nodepth(n, ml[n]) for n in WEIGHTS}
    vd = {n: nodepth(n, vl[n]) for n in WEIGHTS}
    res = {'ada_w': _adamw("adamw_ada_w", wl['ada_w'], ml['ada_w'], vl['ada_w'], G['ada_w'])}
    for n, p, q in zip(names, plane_sum, other):
        if n == 'small':
            outs = _adamw("adamw_small", _pack_small(wd), _pack_small(md), _pack_small(vd), p, q)
            unpacked = [_unpack_small(o, wd) for o in outs]
            for sn in SMALL:
                res[sn] = [u[sn].reshape(wl[sn].shape) for u in unpacked]
        else:
            res[n] = _adamw("adamw_" + n, wl[n], ml[n], vl[n], p, q)
    flat = [res[n][kind] for kind in range(4) for n in WEIGHTS]
    return (loss, grad_x[None], *flat)
```

```python
import functools

import jax
import jax.numpy as jnp
from jax import lax
from jax.experimental import pallas as pl
from jax.experimental.pallas import tpu as pltpu

F32 = jnp.float32
HI = lax.Precision.HIGHEST
_MXU_DTYPE = jnp.bfloat16
_SCAN_DTYPE = jnp.bfloat16
_GRAD_DTYPE = jnp.bfloat16

D_MODEL = 1024
EPS = 1e-6
CHUNK = 64
GRID_W = 64
GDN_HEADS = 8
GDN_DK = 128
SSM_HEADS = 32
SSM_PAIRS = 16
SSD_GROUP = 1
D_FF = 2816
D_IN_PROJ = 11360
N_PLANE = 4
P_QKV, P_ZG, P_GATE, P_ZS, P_XBC, P_SMALL, P_TOTAL = 0, 3072, 4096, 6144, 8192, 11264, 11776
IN_SEGMENTS = [(0, 3072, P_QKV), (3072, 4096, P_ZG), (4096, 4112, P_SMALL), (4112, 4128, P_SMALL + 16),
               (4128, 6176, P_ZS), (6176, 9248, P_XBC), (9248, 9312, P_SMALL + 32), (9312, 11360, P_GATE)]
ADAM_LR, ADAM_B1, ADAM_B2, ADAM_EPS, ADAM_WD, ADAM_STEP = 0.001, 0.9, 0.999, 1e-08, 0.01, 10

VMEM_LIMIT = 48 * 1024 * 1024

WEIGHTS = ['c_ctx', 'ada_w', 'ada_b', 'norm1_w', 'w_in', 'gdn_conv_w', 'gdn_conv_b', 'gdn_a_log', 'gdn_dt_bias',
           'gdn_norm_w', 'ssm_conv_w', 'ssm_conv_b', 'ssm_a_log', 'ssm_dt_bias', 'ssm_d', 'ssm_norm_w', 'w_br_gdn',
           'w_br_ssm', 'w_out', 'norm2_w', 'w_ffn_in', 'w_ffn_out', 'norm_f_w']
SHARD_AXIS = {'ada_w': 1, 'w_in': 1, 'gdn_conv_w': 1, 'ssm_conv_w': 1, 'w_br_gdn': 0, 'w_br_ssm': 0, 'w_out': 0,
              'w_ffn_in': 1, 'w_ffn_out': 0}
SHARDED = [n for n in WEIGHTS if n in SHARD_AXIS]
SMALL = [n for n in WEIGHTS if n not in SHARD_AXIS]
MXU_WEIGHTS = ['ada_w', 'w_in', 'w_br_gdn', 'w_br_ssm', 'w_out', 'w_ffn_in', 'w_ffn_out']


def _cp(sem):
    return pltpu.CompilerParams(dimension_semantics=sem, vmem_limit_bytes=VMEM_LIMIT)


def _pick(n, cands):
    for c in cands:
        if n % c == 0:
            return c
    raise ValueError(f"no tile for {n}")


HBM = pl.BlockSpec(memory_space=pl.ANY)
MESH = pl.DeviceIdType.MESH


def _plane_peers():
    x, y, c = lax.axis_index("x"), lax.axis_index("y"), lax.axis_index("c")
    return (x, y, c), [(1 - x, y, c), (x, 1 - y, c), (1 - x, 1 - y, c)]


class PlaneExchange:
    def __init__(self, arrs, gather):
        self.arrs, self.gather, self.n = list(arrs), gather, len(arrs)
        self.in_specs = [HBM] * self.n
        self.out_specs = [HBM] * self.n
        self.out_shape = [jax.ShapeDtypeStruct((N_PLANE,) + a.shape if gather else a.shape, a.dtype) for a in self.arrs]
        self.scratch = [pltpu.SemaphoreType.DMA((3 * self.n,)), pltpu.SemaphoreType.DMA((3 * self.n,)),
                        pltpu.SemaphoreType.DMA((self.n,))]

    def _copies(self, ins, outs, sems):
        send_sems, recv_sems, local_sems = sems
        (x, y, c), peers = _plane_peers()
        me = 2 * x + y
        copies = []
        for ti in range(self.n):
            src = ins[ti] if self.gather else ins[ti].at[me]
            copies.append(pltpu.make_async_copy(src, outs[ti].at[me], local_sems.at[ti]))
            for kk, (px, py, pc) in enumerate(peers):
                src = ins[ti] if self.gather else ins[ti].at[2 * px + py]
                copies.append(pltpu.make_async_remote_copy(
                    src_ref=src, dst_ref=outs[ti].at[me], send_sem=send_sems.at[3 * ti + kk],
                    recv_sem=recv_sems.at[3 * ti + kk], device_id=(px, py, pc), device_id_type=MESH))
        return copies

    def start(self, ins, outs, sems):
        for cp in self._copies(ins, outs, sems):
            cp.start()

    def wait(self, ins, outs, sems):
        for cp in self._copies(ins, outs, sems):
            cp.wait()


def _plane_exchange(name, arrs, gather):
    ex = PlaneExchange(arrs, gather)
    n = ex.n

    def body(*refs):
        ins, outs, sems = refs[:n], refs[n:2 * n], refs[2 * n:]
        ex.start(ins, outs, sems)
        ex.wait(ins, outs, sems)

    return pl.pallas_call(body, in_specs=ex.in_specs, out_specs=ex.out_specs, out_shape=ex.out_shape,
                          scratch_shapes=ex.scratch, name=name)(*arrs)


def _gather_two_level(name, arrs):
    n = len(arrs)
    split = [a.shape[0] % 32 == 0 for a in arrs]

    def body(*refs):
        ins, outs = refs[:n], refs[n:2 * n]
        ici_send, ici_recv, d2d_send, d2d_recv, local_sems = refs[2 * n:]
        (x, y, c), peers = _plane_peers()
        me = 2 * x + y

        def part(ref, ti):
            if not split[ti]:
                return ref
            rows = arrs[ti].shape[0] // 2
            return ref.at[pl.ds(c * rows, rows)]

        local = [pltpu.make_async_copy(ins[ti], outs[ti].at[me], local_sems.at[ti]) for ti in range(n)]
        ici = [pltpu.make_async_remote_copy(
            src_ref=part(ins[ti], ti), dst_ref=part(outs[ti].at[me], ti), send_sem=ici_send.at[3 * ti + kk],
            recv_sem=ici_recv.at[3 * ti + kk], device_id=peer, device_id_type=MESH)
            for ti in range(n) for kk, peer in enumerate(peers)]
        for cp in local + ici:
            cp.start()
        d2d = []
        for ti in range(n):
            for kk, (px, py, pc) in enumerate(peers):
                ici[3 * ti + kk].wait_recv()
                if split[ti]:
                    piece = part(outs[ti].at[2 * px + py], ti)
                    cp = pltpu.make_async_remote_copy(
                        src_ref=piece, dst_ref=piece, send_sem=d2d_send.at[3 * ti + kk],
                        recv_sem=d2d_recv.at[3 * ti + kk], device_id=(x, y, 1 - c), device_id_type=MESH)
                    cp.start()
                    d2d.append(cp)
        for cp in ici:
            cp.wait_send()
        for cp in d2d:
            cp.wait()
        for cp in local:
            cp.wait()

    return pl.pallas_call(
        body, in_specs=[HBM] * n, out_specs=[HBM] * n,
        out_shape=[jax.ShapeDtypeStruct((N_PLANE,) + a.shape, a.dtype) for a in arrs],
        scratch_shapes=[pltpu.SemaphoreType.DMA((3 * n,))] * 4 + [pltpu.SemaphoreType.DMA((n,))], name=name)(*arrs)


def _gather_all(name, arr):
    flips = [(fx, fy, fc) for fx in (0, 1) for fy in (0, 1) for fc in (0, 1)][1:]

    def body(a_ref, o_ref, send_sems, recv_sems, local_sem):
        x, y, c = lax.axis_index("x"), lax.axis_index("y"), lax.axis_index("c")
        me = 4 * x + 2 * y + c
        copies = [pltpu.make_async_copy(a_ref, o_ref.at[me], local_sem)]
        for kk, (fx, fy, fc) in enumerate(flips):
            peer = (1 - x if fx else x, 1 - y if fy else y, 1 - c if fc else c)
            copies.append(pltpu.make_async_remote_copy(
                src_ref=a_ref, dst_ref=o_ref.at[me], send_sem=send_sems.at[kk], recv_sem=recv_sems.at[kk],
                device_id=peer, device_id_type=MESH))
        for cp in copies:
            cp.start()
        for cp in copies:
            cp.wait()

    return pl.pallas_call(
        body, in_specs=[HBM], out_specs=HBM, out_shape=jax.ShapeDtypeStruct((8,) + arr.shape, arr.dtype),
        scratch_shapes=[pltpu.SemaphoreType.DMA((7,)), pltpu.SemaphoreType.DMA((7,)), pltpu.SemaphoreType.DMA],
        name=name)(arr)


def _swap_sibling(arrs):
    n = len(arrs)

    def body(*refs):
        ins, outs, send_sems, recv_sems = refs[:n], refs[n:2 * n], refs[2 * n], refs[2 * n + 1]
        x, y, c = lax.axis_index("x"), lax.axis_index("y"), lax.axis_index("c")
        copies = [pltpu.make_async_remote_copy(src_ref=ins[ti], dst_ref=outs[ti], send_sem=send_sems.at[ti],
                                               recv_sem=recv_sems.at[ti], device_id=(x, y, 1 - c), device_id_type=MESH)
                  for ti in range(n)]
        for cp in copies:
            cp.start()
        for cp in copies:
            cp.wait()

    return pl.pallas_call(
        body, in_specs=[HBM] * n, out_specs=[HBM] * n, out_shape=[jax.ShapeDtypeStruct(a.shape, a.dtype) for a in arrs],
        scratch_shapes=[pltpu.SemaphoreType.DMA((n,)), pltpu.SemaphoreType.DMA((n,))], name="swap_sibling")(*arrs)


MATMUL_VMEM_BUDGET = 36 * 1024 * 1024
TILE_CANDIDATES = (2944, 2816, 1408, 1024, 768, 512, 256, 128)
STEP_COST_BYTES = 1 << 20


def _matmul_tiles(m, n, k, ab, bb, ob, tn_fixed=None, tk_fixed=None):
    def cands(dim, whole_up_to):
        out = [c for c in TILE_CANDIDATES if dim % c == 0]
        if dim <= whole_up_to and dim not in out:
            out.append(dim)
        return out

    best = None
    for tm in cands(m, 3072):
        for tn in ([tn_fixed] if tn_fixed else cands(n, 3072)):
            for tk in ([tk_fixed] if tk_fixed else cands(k, 2048)):
                gi, gj, gk = m // tm, n // tn, k // tk
                vmem = 2 * (tm * tk * ab + tk * tn * bb + tm * tn * ob) + (tm * tn * 4 if gk > 1 else 0)
                if vmem > MATMUL_VMEM_BUDGET:
                    continue
                a_reads = 1 if gk == 1 else gj
                b_reads = 1 if (gk == 1 and gj == 1) else gi
                cost = (m * k * ab * a_reads + k * n * bb * b_reads + m * n * ob + gi * gj * gk * STEP_COST_BYTES)
                if best is None or cost < best[0]:
                    best = (cost, tm, tn, tk)
    assert best is not None, (m, n, k)
    return best[1:]


def _matmul(a, b, form, name, out_dtype=F32, stacked_out=False, ride=None, halves=None):
    dims = {'nn': (((1,), (0,)), ((), ())), 'nt': (((1,), (1,)), ((), ())), 'tn': (((0,), (0,)), ((), ()))}[form]
    stacked_b = b.ndim == 3 and halves != 'b'
    ns = b.shape[2] if stacked_b else None
    if form == 'nn':
        m, k = a.shape
        n = b.shape[0] * ns if stacked_b else b.shape[1]
    elif form == 'nt':
        m, k = (a.shape[1], 2 * a.shape[2]) if halves == 'a' else a.shape
        n = b.shape[1] if stacked_b else b.shape[0]
    else:
        k, m = a.shape
        n = 2 * b.shape[2] if halves == 'b' else b.shape[1]
    ob = jnp.dtype(out_dtype).itemsize
    tm, tn, tk = _matmul_tiles(m, n, k, a.dtype.itemsize, b.dtype.itemsize, ob,
                               tn_fixed=(n // N_PLANE if stacked_out else ns if (stacked_b and form == 'nn') else None),
                               tk_fixed=(ns if (stacked_b and form == 'nt') else None))
    nk = k // tk
    grid = (m // tm, n // tn, nk)
    n_ride = ride.n if ride is not None else 0

    def body(*refs):
        a_ref, b_ref = refs[0], refs[1]
        ride_in = refs[2:2 + n_ride]
        o_ref = refs[2 + n_ride]
        ride_out = refs[3 + n_ride:3 + 2 * n_ride]
        acc_ref = refs[3 + 2 * n_ride]
        sems = refs[4 + 2 * n_ride:]
        i, j, kk = pl.program_id(0), pl.program_id(1), pl.program_id(2)
        if ride is not None:
            @pl.when((i == 0) & (j == 0) & (kk == 0))
            def _():
                ride.start(ride_in, ride_out, sems)

        def put(val):
            if stacked_out or halves == 'out':
                o_ref[0] = val.astype(o_ref.dtype)
            else:
                o_ref[...] = val.astype(o_ref.dtype)

        av = a_ref[0] if halves == 'a' else a_ref[...]
        bv = b_ref[0] if (stacked_b or halves == 'b') else b_ref[...]
        part = lax.dot_general(av.astype(_MXU_DTYPE), bv.astype(_MXU_DTYPE), dims, preferred_element_type=F32)
        if nk == 1:
            put(part)
        else:
            @pl.when(kk == 0)
            def _():
                acc_ref[...] = part

            @pl.when(kk > 0)
            def _():
                acc_ref[...] += part

            @pl.when(kk == nk - 1)
            def _():
                put(acc_ref[...])

        if ride is not None:
            @pl.when((i == grid[0] - 1) & (j == grid[1] - 1) & (kk == nk - 1))
            def _():
                ride.wait(ride_in, ride_out, sems)

    if form == 'nn':
        a_spec = pl.BlockSpec((tm, tk), lambda i, j, kk: (i, kk))
        b_spec = (pl.BlockSpec((1, tk, tn), lambda i, j, kk: (j, kk, 0)) if stacked_b
                  else pl.BlockSpec((tk, tn), lambda i, j, kk: (kk, j)))
    elif form == 'nt':
        a_spec = pl.BlockSpec((tm, tk), lambda i, j, kk: (i, kk))
        b_spec = (pl.BlockSpec((1, tn, tk), lambda i, j, kk: (kk, j, 0)) if stacked_b
                  else pl.BlockSpec((tn, tk), lambda i, j, kk: (j, kk)))
    else:
        a_spec = pl.BlockSpec((tk, tm), lambda i, j, kk: (kk, i))
        b_spec = pl.BlockSpec((tk, tn), lambda i, j, kk: (kk, j))
    hj, hk = grid[1] // 2, nk // 2
    if halves == 'a':
        a_spec = pl.BlockSpec((1, tm, tk), lambda i, j, kk: (kk // hk, i, kk % hk))
    if halves == 'b':
        b_spec = pl.BlockSpec((1, tk, tn), lambda i, j, kk: (j // hj, kk, j % hj))
    if stacked_out:
        o_spec = pl.BlockSpec((1, tm, tn), lambda i, j, kk: (j, i, 0))
        o_shape = jax.ShapeDtypeStruct((N_PLANE, m, tn), out_dtype)
    elif halves == 'out':
        o_spec = pl.BlockSpec((1, tm, tn), lambda i, j, kk: (j // hj, i, j % hj))
        o_shape = jax.ShapeDtypeStruct((2, m, n // 2), out_dtype)
    else:
        o_spec = pl.BlockSpec((tm, tn), lambda i, j, kk: (i, j))
        o_shape = jax.ShapeDtypeStruct((m, n), out_dtype)
    acc_shape = (tm, tn) if nk > 1 else (8, 128)
    if ride is None:
        return pl.pallas_call(
            body, grid=grid, in_specs=[a_spec, b_spec], out_specs=o_spec, out_shape=o_shape,
            scratch_shapes=[pltpu.VMEM(acc_shape, F32)],
            compiler_params=_cp(("parallel", "parallel", "arbitrary")), name=name)(a, b)
    res = pl.pallas_call(
        body, grid=grid, in_specs=[a_spec, b_spec] + ride.in_specs, out_specs=[o_spec] + ride.out_specs,
        out_shape=[o_shape] + ride.out_shape, scratch_shapes=[pltpu.VMEM(acc_shape, F32)] + ride.scratch,
        compiler_params=_cp(("arbitrary", "arbitrary", "arbitrary")), name=name)(a, b, *ride.arrs)
    return res[0], res[1:]


class Op:
    def __init__(self, arr, bs, im, load=None):
        self.arr, self.bs, self.im = arr, bs, im
        self.arrs = list(arr) if isinstance(arr, list) else [arr]
        self.load = load or (lambda r: r[...].astype(F32))

    def spec(self):
        return pl.BlockSpec(self.bs, self.im)

    def value(self, it):
        return functools.reduce(lambda u, w: u + w, [self.load(next(it)) for _ in self.arrs])


def _op_specs(ops):
    return [op.spec() for op in ops for _ in op.arrs]


def _op_arrays(ops):
    return [a for op in ops for a in op.arrs]


def _sum_dirs(r):
    return r[0].astype(F32) + r[1].astype(F32)


def _tw_fwd(name, fn, grid, ins, outs):
    def body(*refs):
        info = (pl.program_id(0), pl.program_id(1))
        it = iter(refs)
        vals = [op.value(it) for op in ins]
        res = fn(info, *vals)
        for r, v in zip(it, res):
            r[...] = v.astype(r.dtype)

    return pl.pallas_call(
        body, grid=grid, in_specs=_op_specs(ins), out_specs=[op.spec() for op in outs],
        out_shape=[jax.ShapeDtypeStruct(*op.arr) for op in outs],
        compiler_params=_cp(("parallel", "arbitrary")), name=name)(*_op_arrays(ins))


def _tw_bwd(name, fn, grid, tok, par, cots, tok_out, par_out, tok_add=None, sem=("parallel", "arbitrary"), into=None):
    n_tok = len(tok)
    flat_cots = [op for group in cots for op in group]
    extra = [tok_add] if tok_add is not None else []
    out_ops = list(tok_out) + list(par_out)
    into = into or {}

    def body(*refs):
        info = (pl.program_id(0), pl.program_id(1))
        it = iter(refs)
        tok_v = [op.value(it) for op in tok]
        par_v = [op.value(it) for op in par]
        cot_v = [functools.reduce(lambda u, w: u + w, [op.value(it) for op in group]) for group in cots]
        add_v = [op.value(it) for op in extra]
        for _ in into:
            next(it)
        _, pull = jax.vjp(lambda *a: fn(info, *a), *tok_v, *par_v)
        grads = pull(tuple(cot_v))
        for i in range(n_tok):
            r = next(it)
            g = grads[i] + add_v[0] if (i == 0 and add_v) else grads[i]
            if r.shape[-1] > g.shape[-1]:
                g = jnp.concatenate([g, jnp.zeros((g.shape[0], r.shape[-1] - g.shape[-1]), g.dtype)], axis=1)
            r[...] = g.astype(r.dtype)
        first = pl.program_id(1) == 0
        for i in range(len(par)):
            r = next(it)
            g = grads[n_tok + i]

            @pl.when(first)
            def _(r=r, g=g):
                r[...] = g

            @pl.when(jnp.logical_not(first))
            def _(r=r, g=g):
                r[...] += g

    ops = tok + par + flat_cots + extra
    n_in = len(_op_arrays(ops))
    return pl.pallas_call(
        body, grid=grid, in_specs=_op_specs(ops) + [HBM] * len(into), out_specs=[op.spec() for op in out_ops],
        out_shape=[jax.ShapeDtypeStruct(*op.arr) for op in out_ops],
        input_output_aliases={n_in + k: i for k, i in enumerate(into)},
        compiler_params=_cp(sem), name=name)(*_op_arrays(ops), *into.values())


def _silu(x):
    return x * jax.nn.sigmoid(x)


def _rms(x):
    return x * lax.rsqrt(jnp.mean(x * x, axis=-1, keepdims=True) + EPS)


@functools.partial(jax.custom_vjp, nondiff_argnums=(1,))
def _shift_rows(x, k):
    return pltpu.roll(x, k % x.shape[0], 0)


def _shift_rows_fwd(x, k):
    return _shift_rows(x, k), None


def _shift_rows_bwd(k, _, g):
    return (_shift_rows(g, -k),)


_shift_rows.defvjp(_shift_rows_fwd, _shift_rows_bwd)


def _prenorm_fn(nctx_t, info, x, w, sc8, sh8):
    is_ctx = info[1] < nctx_t
    sc = jnp.where(is_ctx, sc8[1:2], sc8[0:1])
    sh = jnp.where(is_ctx, sh8[1:2], sh8[0:1])
    return (_rms(x) * w * (1.0 + sc) + sh,)


def _conv_fn(nctx_t, mode, info, x, w, b):
    n, c = x.shape
    is_ctx = info[1] < nctx_t
    idx = lax.broadcasted_iota(jnp.int32, (n, 1), 0)
    rr = jnp.where(is_ctx, idx, idx % GRID_W)
    first = rr == 0
    last = rr == jnp.where(is_ctx, n - 1, GRID_W - 1)
    prev = jnp.where(first, 0.0, _shift_rows(x, 1))
    nxt = jnp.where(last, 0.0, _shift_rows(x, -1))
    y = b + prev * w[0:1] + x * w[1:2] + nxt * w[2:3]
    y = _silu(y)
    if mode == 'none':
        return (y,)
    scale = GDN_DK ** -0.5 if mode == 'q' else 1.0
    outs = []
    for h in range(c // 128):
        yh = y[:, h * 128:(h + 1) * 128]
        outs.append(yh * lax.rsqrt(jnp.sum(yh * yh, axis=-1, keepdims=True) + EPS) * scale)
    return (jnp.concatenate(outs, axis=1),)


def _act_fn(info, x, p0, p1):
    lane = lax.broadcasted_iota(jnp.int32, x.shape, 1)
    sp = jax.nn.softplus(x + p1)
    g = -jnp.exp(p0) * sp
    bt = jax.nn.sigmoid(x)
    return (jnp.where(lane < 16, g, jnp.where(lane < 32, bt, jnp.where(lane < 96, sp, 0.0))),)


def _mixg_fn(info, o, zg, gnw):
    outs = []
    for h in range(GDN_HEADS):
        outs.append(_rms(o[:, h * 128:(h + 1) * 128]) * gnw)
    return (jnp.concatenate(outs, axis=1) * _silu(zg),)


def _mixs_fn(info, y, xs, zs, dl, snw):
    yy = (y + dl * xs) * _silu(zs)
    outs = []
    for g in range(4):
        outs.append(_rms(yy[:, g * 512:(g + 1) * 512]))
    return (jnp.concatenate(outs, axis=1) * snw,)


def _merge_fn(info, gates, pg, ps):
    half = gates.shape[1] // 2
    return (jax.nn.sigmoid(gates[:, :half]) * pg + jax.nn.sigmoid(gates[:, half:]) * ps,)


def _norm2_fn(info, xt, mo, g8, w, sc8, sh8):
    h1 = xt + g8[0:1] * mo
    return (h1, _rms(h1) * w * (1.0 + sc8[0:1]) + sh8[0:1])


def _swiglu_fn(info, u2):
    return (_silu(u2[0]) * u2[1],)


_NN = (((1,), (0,)), ((), ()))
_NT = (((1,), (1,)), ((), ()))
_TN = (((0,), (0,)), ((), ()))


def _mmh(a, b):
    return lax.dot_general(a, b, _NN, precision=HI, preferred_element_type=F32)


def _dot1(a, b, dims):
    return lax.dot_general(a.astype(_SCAN_DTYPE), b.astype(_SCAN_DTYPE), dims, preferred_element_type=F32)


def _mm(a, b):
    return _dot1(a, b, _NN)


def _mm_nt(a, b):
    return _dot1(a, b, _NT)


def _mm_tn(a, b):
    return _dot1(a, b, _TN)


def _split2(a):
    hi = a.astype(_SCAN_DTYPE)
    return hi, (a - hi.astype(F32)).astype(_SCAN_DTYPE)


def _dot3(a, b, dims):
    ah, al = _split2(a)
    bh, bl = _split2(b)
    d = lambda u, w: lax.dot_general(u, w, dims, preferred_element_type=F32)
    return d(ah, bh) + (d(ah, bl) + d(al, bh))


def _order_masks(d):
    i = lax.broadcasted_iota(jnp.int32, (CHUNK, CHUNK), 0)
    j = lax.broadcasted_iota(jnp.int32, (CHUNK, CHUNK), 1)
    s = jnp.where(d == 0, 1, -1) * (i - j)
    return (s >= 0).astype(F32), (s > 0).astype(F32)


@jax.custom_vjp
def _unit_tri_inv(mats):
    i = lax.broadcasted_iota(jnp.int32, (CHUNK, CHUNK), 0)
    j = lax.broadcasted_iota(jnp.int32, (CHUNK, CHUNK), 1)
    eye = (i == j).astype(F32)
    ps = [-a for a in mats]
    ts = [eye + p for p in ps]
    for _ in range(5):
        ps = [_dot3(p, p, _NN) for p in ps]
        ts = [t + _dot3(t, p, _NN) for t, p in zip(ts, ps)]
    return tuple(ts)


def _uti_fwd(mats):
    ts = _unit_tri_inv(mats)
    return ts, ts


def _uti_bwd(ts, gs):
    inner = [_dot3(g, t, _NT) for g, t in zip(gs, ts)]
    return (tuple(-_dot3(t, u, _TN) for t, u in zip(ts, inner)),)


_unit_tri_inv.defvjp(_uti_fwd, _uti_bwd)


@jax.custom_vjp
def _unit_tri_inv_given(mats, ts):
    return ts


def _utig_fwd(mats, ts):
    return ts, ts


def _utig_bwd(ts, gs):
    return _uti_bwd(ts, gs)[0], tuple(jnp.zeros_like(t) for t in ts)


_unit_tri_inv_given.defvjp(_utig_fwd, _utig_bwd)


def _lane_col(blk, lane_idx):
    lane = lax.broadcasted_iota(jnp.int32, blk.shape, 1)
    return jnp.sum(jnp.where(lane == lane_idx, blk, 0.0), axis=1, keepdims=True)


def _decay_mat(cum, incl):
    cb = jnp.broadcast_to(cum, (CHUNK, CHUNK))
    return jnp.exp(jnp.minimum(cb - cb.T, 0.0)) * incl


def _gdn_chunks(dirs, streams, states, aux=None, want_aux=False, group=None):
    ns = len(dirs)
    masks = [_order_masks(d) for d in dirs]
    cum = [_mmh(masks[i][0], streams[i][3]) for i in range(ns)]
    tot = [jnp.sum(streams[i][3], axis=0, keepdims=True) for i in range(ns)]
    all_units = [(i, h) for i in range(ns) for h in range(GDN_HEADS)]
    group = group or len(all_units)
    cat = jnp.concatenate
    outs, new_states, ts_all = [], [], []
    for g0 in range(0, len(all_units), group):
        units = all_units[g0:g0 + group]
        us = range(len(units))
        st = states[g0:g0 + group]
        qs = [streams[i][0][:, h * 128:(h + 1) * 128] for i, h in units]
        ks = [streams[i][1][:, h * 128:(h + 1) * 128] for i, h in units]
        vs = [streams[i][2][:, h * 128:(h + 1) * 128] for i, h in units]
        gcum = [_lane_col(cum[i], dirs[i] * GDN_HEADS + h) for i, h in units]
        glast = [_lane_col(tot[i], dirs[i] * GDN_HEADS + h) for i, h in units]
        beta = [_lane_col(streams[i][3], 16 + dirs[i] * GDN_HEADS + h) for i, h in units]
        decay = [_decay_mat(gcum[u], masks[units[u][0]][0]) for u in us]
        egc = [jnp.exp(gcum[u]) for u in us]
        kb = [ks[u] * beta[u] for u in us]
        kq = [_mm_nt(cat([kb[u], qs[u]], axis=0), ks[u]) for u in us]
        mats = tuple(kq[u][:CHUNK] * decay[u] * masks[units[u][0]][1] for u in us)
        ts = _unit_tri_inv(mats) if aux is None else _unit_tri_inv_given(mats, tuple(aux[g0:g0 + group]))
        wu = [_mm(ts[u], cat([kb[u] * egc[u], vs[u] * beta[u]], axis=1)) for u in us]
        ws = [_mm(cat([wu[u][:, :128], qs[u] * egc[u]], axis=0), st[u]) for u in us]
        vn = [wu[u][:, 128:] - ws[u][:CHUNK] for u in us]
        outs += [ws[u][CHUNK:] + _mm(kq[u][CHUNK:] * decay[u], vn[u]) for u in us]
        new_states += [st[u] * jnp.exp(glast[u]) + _mm_tn(ks[u] * jnp.exp(glast[u] - gcum[u]), vn[u]) for u in us]
        ts_all += list(ts)
    per_stream = [cat(outs[i * GDN_HEADS:(i + 1) * GDN_HEADS], axis=1) for i in range(ns)]
    return (*per_stream, *new_states, *(ts_all if want_aux else ()))


def _gdn_step(d, q, k, v, spb, *states, aux=None, want_aux=False):
    return _gdn_chunks([d], [(q, k, v, spb)], states, aux, want_aux, group=4)


def _gdn_step2(q0, k0, v0, sp0, q1, k1, v1, sp1, *states, aux=None, want_aux=False):
    return _gdn_chunks([0, 1], [(q0, k0, v0, sp0), (q1, k1, v1, sp1)], states, aux, want_aux)


def _split3(a):
    a1 = a.astype(_SCAN_DTYPE)
    r = a - a1.astype(F32)
    a2 = r.astype(_SCAN_DTYPE)
    return a1, a2, (r - a2.astype(F32)).astype(_SCAN_DTYPE)


def _exact_dot(a, e, dims, split_lhs, passes=3):
    parts = _split3(a if split_lhs else e)[:passes]
    d = (lambda u: lax.dot_general(u, e, dims, preferred_element_type=F32)) if split_lhs else \
        (lambda u: lax.dot_general(a, u, dims, preferred_element_type=F32))
    return functools.reduce(lambda u, w: u + w, [d(p) for p in reversed(parts)])


@jax.custom_vjp
def _spread(a, e):
    return _exact_dot(a, e, _NN, True)


def _spread_fwd(a, e):
    return _spread(a, e), e


def _spread_bwd(e, g):
    return _exact_dot(g, e, _NT, True, passes=2), jnp.zeros_like(e)


_spread.defvjp(_spread_fwd, _spread_bwd)


@jax.custom_vjp
def _colsum_bcast(z):
    return _exact_dot(jnp.ones((z.shape[0], z.shape[0]), _SCAN_DTYPE), z, _NN, False)


def _colsum_fwd(z):
    return _colsum_bcast(z), None


def _colsum_bwd(_, g):
    return (_exact_dot(jnp.ones((g.shape[0], g.shape[0]), _SCAN_DTYPE), g, _NN, False, passes=2),)


_colsum_bcast.defvjp(_colsum_fwd, _colsum_bwd)


def _ssd_consts():
    wdt = SSM_HEADS * 64
    d = lax.broadcasted_iota(jnp.int32, (2, 128, wdt), 0)
    e = (lax.broadcasted_iota(jnp.int32, (2, 128, wdt), 1)
         == 32 + d * SSM_HEADS + lax.broadcasted_iota(jnp.int32, (2, 128, wdt), 2) // 64).astype(_SCAN_DTYPE)
    dd = lax.broadcasted_iota(jnp.int32, (2, CHUNK, wdt), 0)
    ci = lax.broadcasted_iota(jnp.int32, (2, CHUNK, wdt), 1)
    pos = lax.broadcasted_iota(jnp.int32, (2, CHUNK, wdt), 2) % 64
    incl_t = (jnp.where(dd == 0, 1, -1) * (ci - pos) >= 0).astype(F32)
    diag = (ci == pos).astype(F32)
    return [e, incl_t, diag]


def _ssd_step(d, x, bm, cm, spb, alog, *states, consts):
    e, incl_t, diag = consts
    incl, _ = _order_masks(d)
    lane1 = lax.broadcasted_iota(jnp.int32, (1, 128), 1)
    lo_lane = 32 + d * SSM_HEADS
    a_vec = jnp.where(lane1 >= lo_lane, jnp.where(lane1 < lo_lane + SSM_HEADS, -jnp.exp(alog), 0.0), 0.0)
    adt = spb * a_vec
    acum = _mmh(incl, adt)
    alast = jnp.sum(adt, axis=0, keepdims=True)
    dt2 = _spread(spb, e)
    ac2 = _spread(acum, e)
    al2 = _spread(jnp.broadcast_to(alast, (8, 128)), e)[0:1]
    row = _colsum_bcast(ac2 * diag)
    seg = jnp.exp(jnp.minimum(ac2 - row, 0.0)) * incl_t
    xdt = x * dt2
    gam = jnp.exp(ac2)
    xe = xdt * jnp.exp(al2 - ac2)
    low = lax.broadcasted_iota(jnp.int32, (CHUNK, 128), 1) < 64
    row_low = lax.broadcasted_iota(jnp.int32, (128, 1), 0) < 64
    ps = range(SSM_PAIRS)
    sl = [slice(p * 128, (p + 1) * 128) for p in ps]
    bg = [bm[:, g * 128:(g + 1) * 128] for g in range(4)]
    cg = [cm[:, g * 128:(g + 1) * 128] for g in range(4)]
    cb2 = [_mm_nt(cg[g], jnp.concatenate([bg[g], bg[g]], axis=0)) for g in range(4)]
    ys, new_states = [], []
    for p0 in range(0, SSM_PAIRS, SSD_GROUP):
        pg = range(p0, p0 + SSD_GROUP)
        xd = {p: jnp.concatenate([jnp.where(low, xdt[:, sl[p]], 0.0), jnp.where(low, 0.0, xdt[:, sl[p]])], axis=0)
              for p in pg}
        yd = {p: _mm(cb2[p // 4] * seg[:, sl[p]], xd[p]) for p in pg}
        yo = {p: _mm_nt(cg[p // 4], states[p]) for p in pg}
        ys += [yd[p] + gam[:, sl[p]] * yo[p] for p in pg]
        new = {p: _mm_tn(xe[:, sl[p]], bg[p // 4]) for p in pg}
        al0 = {p: _lane_col(alast, lo_lane + 2 * p) for p in pg}
        al1 = {p: _lane_col(alast, lo_lane + 2 * p + 1) for p in pg}
        new_states += [states[p] * jnp.exp(jnp.where(row_low, al0[p], al1[p])) + new[p] for p in pg]
    return (jnp.concatenate(ys, axis=1), *new_states)


def _chunk_of(d, p, nctx, nc):
    return jnp.where(d == 0, p, jnp.where(p < nctx, nctx - 1 - p, nctx + nc - 1 - p))


class _NoRide:
    n, arrs, in_specs, out_specs, out_shape, scratch = 0, [], [], [], [], []


def _const_specs(consts):
    return [pl.BlockSpec((1,) + a.shape[1:], lambda d, p: (d,) + (0,) * (a.ndim - 1)) for a in consts]


def _scan_fwd(name, step, toks, pars, consts, out_width, n_state, nctx):
    t = toks[0].shape[0]
    nc = t // CHUNK
    n_tok, n_par, n_const = len(toks), len(pars), len(consts)

    def body(*refs):
        it = iter(refs)
        tok_refs = [next(it) for _ in range(n_tok)]
        par_refs = [next(it) for _ in range(n_par)]
        const_refs = [next(it) for _ in range(n_const)]
        o_ref, ss_ref, s_scr = next(it), next(it), next(it)
        d, p = pl.program_id(0), pl.program_id(1)

        @pl.when(p == 0)
        def _():
            s_scr[...] = jnp.zeros(s_scr.shape, F32)

        ss_ref[0, 0] = s_scr[...]
        res = step(d, *[r[...] for r in tok_refs], *[r[...] for r in par_refs], *[s_scr[h] for h in range(n_state)],
                   consts=[r[0] for r in const_refs])
        o_ref[0] = res[0]
        for h in range(n_state):
            s_scr[h] = res[1 + h]

    ch = lambda d, p: _chunk_of(d, p, nctx, nc)
    in_specs = [pl.BlockSpec((CHUNK, a.shape[1]), lambda d, p: (ch(d, p), 0)) for a in toks]
    in_specs += [pl.BlockSpec(a.shape, lambda d, p: (0, 0)) for a in pars]
    return pl.pallas_call(
        body, grid=(2, nc), in_specs=in_specs + _const_specs(consts),
        out_specs=[pl.BlockSpec((1, CHUNK, out_width), lambda d, p: (d, ch(d, p), 0)),
                   pl.BlockSpec((1, 1, n_state, 128, 128), lambda d, p: (d, p, 0, 0, 0))],
        out_shape=[jax.ShapeDtypeStruct((2, t, out_width), F32),
                   jax.ShapeDtypeStruct((2, nc, n_state, 128, 128), F32)],
        scratch_shapes=[pltpu.VMEM((n_state, 128, 128), F32)],
        compiler_params=_cp(("arbitrary", "arbitrary")), name=name)(*toks, *pars, *consts)


def _scan_bwd(name, step, toks, pars, ss, dout, n_state, nctx, ride=None, aux=None, consts=()):
    t = toks[0].shape[0]
    nc = t // CHUNK
    n_tok, n_par, n_const = len(toks), len(pars), len(consts)
    rd = ride if ride is not None else _NoRide
    n_aux = aux.shape[2] if aux is not None else 0

    def body(*refs):
        it = iter(refs)
        tok_refs = [next(it) for _ in range(n_tok)]
        par_refs = [next(it) for _ in range(n_par)]
        ss_ref, do_ref = next(it), next(it)
        aux_ref = next(it) if n_aux else None
        const_refs = [next(it) for _ in range(n_const)]
        ride_in = [next(it) for _ in range(rd.n)]
        dtok_refs = [next(it) for _ in range(n_tok)]
        dpar_refs = [next(it) for _ in range(n_par)]
        ride_out = [next(it) for _ in range(rd.n)]
        ds_scr = next(it)
        sems = list(it)
        d, pr = pl.program_id(0), pl.program_id(1)
        if ride is not None:
            @pl.when((d == 0) & (pr == 0))
            def _():
                ride.start(ride_in, ride_out, sems)

            @pl.when((d == 1) & (pr == nc - 1))
            def _():
                ride.wait(ride_in, ride_out, sems)

        @pl.when(pr == 0)
        def _():
            ds_scr[...] = jnp.zeros(ds_scr.shape, F32)

        kw = dict(aux=[aux_ref[0, 0, i] for i in range(n_aux)]) if n_aux else {}
        if n_const:
            kw['consts'] = [r[0] for r in const_refs]
        _, pull = jax.vjp(functools.partial(step, d, **kw), *[r[...] for r in tok_refs], *[r[...] for r in par_refs],
                          *[ss_ref[0, 0, h] for h in range(n_state)])
        grads = pull((do_ref[...], *[ds_scr[h] for h in range(n_state)]))
        for r, g in zip(dtok_refs, grads[:n_tok]):
            r[0] = g
        for h in range(n_state):
            ds_scr[h] = grads[n_tok + n_par + h]
        first = (d == 0) & (pr == 0)
        for r, g in zip(dpar_refs, grads[n_tok:n_tok + n_par]):
            @pl.when(first)
            def _(r=r, g=g):
                r[...] = g

            @pl.when(jnp.logical_not(first))
            def _(r=r, g=g):
                r[...] += g

    ch = lambda d, pr: _chunk_of(d, nc - 1 - pr, nctx, nc)
    in_specs = [pl.BlockSpec((CHUNK, a.shape[1]), lambda d, pr: (ch(d, pr), 0)) for a in toks]
    in_specs += [pl.BlockSpec(a.shape, lambda d, pr: (0, 0)) for a in pars]
    in_specs += [pl.BlockSpec((1, 1, n_state, 128, 128), lambda d, pr: (d, nc - 1 - pr, 0, 0, 0)),
                 pl.BlockSpec((CHUNK, dout.shape[1]), lambda d, pr: (ch(d, pr), 0))]
    if n_aux:
        in_specs += [pl.BlockSpec((1, 1, n_aux, CHUNK, CHUNK), lambda d, pr: (d, nc - 1 - pr, 0, 0, 0))]
    in_specs += _const_specs(consts)
    out_specs = [pl.BlockSpec((1, CHUNK, a.shape[1]), lambda d, pr: (d, ch(d, pr), 0)) for a in toks]
    out_specs += [pl.BlockSpec(a.shape, lambda d, pr: (0, 0)) for a in pars]
    out_shape = [jax.ShapeDtypeStruct((2, t, a.shape[1]), F32) for a in toks]
    out_shape += [jax.ShapeDtypeStruct(a.shape, F32) for a in pars]
    return pl.pallas_call(
        body, grid=(2, nc), in_specs=in_specs + rd.in_specs, out_specs=out_specs + rd.out_specs,
        out_shape=out_shape + rd.out_shape, scratch_shapes=[pltpu.VMEM((n_state, 128, 128), F32)] + rd.scratch,
        compiler_params=_cp(("arbitrary", "arbitrary")), name=name)(
            *toks, *pars, ss, dout, *([aux] if n_aux else []), *consts, *rd.arrs)


def _scan2_fwd(name, step2, toks, out_width, n_state, nctx, n_aux, ride=None):
    t = toks[0].shape[0]
    nc = t // CHUNK
    n_tok = len(toks)
    rd = ride if ride is not None else _NoRide

    def body(*refs):
        it = iter(refs)
        tok_refs = [next(it) for _ in range(2 * n_tok)]
        ride_in = [next(it) for _ in range(rd.n)]
        o_refs = [next(it), next(it)]
        ss_ref, aux_ref = next(it), next(it)
        ride_out = [next(it) for _ in range(rd.n)]
        s_scr = next(it)
        sems = list(it)
        p = pl.program_id(0)
        if ride is not None:
            @pl.when(p == 0)
            def _():
                ride.start(ride_in, ride_out, sems)

        @pl.when(p == 0)
        def _():
            s_scr[...] = jnp.zeros(s_scr.shape, F32)

        for d in range(2):
            ss_ref[d, 0] = s_scr[d * n_state:(d + 1) * n_state]
        res = step2(*[r[...] for r in tok_refs], *[s_scr[u] for u in range(2 * n_state)], want_aux=True)
        for d in range(2):
            o_refs[d][...] = res[d]
            for i in range(n_aux):
                aux_ref[d, 0, i] = res[2 + 2 * n_state + d * n_aux + i]
        for u in range(2 * n_state):
            s_scr[u] = res[2 + u]
        if ride is not None:
            @pl.when(p == nc - 1)
            def _():
                ride.wait(ride_in, ride_out, sems)

    def tok_spec(a, d):
        return pl.BlockSpec((CHUNK, a.shape[1]), lambda p: (_chunk_of(d, p, nctx, nc), 0))

    return pl.pallas_call(
        body, grid=(nc,), in_specs=[tok_spec(a, d) for d in range(2) for a in toks] + rd.in_specs,
        out_specs=[pl.BlockSpec((CHUNK, out_width), lambda p: (_chunk_of(0, p, nctx, nc), 0)),
                   pl.BlockSpec((CHUNK, out_width), lambda p: (_chunk_of(1, p, nctx, nc), 0)),
                   pl.BlockSpec((2, 1, n_state, 128, 128), lambda p: (0, p, 0, 0, 0)),
                   pl.BlockSpec((2, 1, n_aux, CHUNK, CHUNK), lambda p: (0, p, 0, 0, 0))] + rd.out_specs,
        out_shape=[jax.ShapeDtypeStruct((t, out_width), F32), jax.ShapeDtypeStruct((t, out_width), F32),
                   jax.ShapeDtypeStruct((2, nc, n_state, 128, 128), F32),
                   jax.ShapeDtypeStruct((2, nc, n_aux, CHUNK, CHUNK), F32)] + rd.out_shape,
        scratch_shapes=[pltpu.VMEM((2 * n_state, 128, 128), F32)] + rd.scratch,
        compiler_params=_cp(("arbitrary",)), name=name)(*toks, *toks, *rd.arrs)


ADA_TN = 512


def _ada_part(cc16, w_shard):
    n = w_shard.shape[1]

    def body(c_ref, w_ref, o_ref):
        s = _silu(c_ref[...]).astype(_MXU_DTYPE)
        o_ref[...] = jnp.dot(s, w_ref[...].astype(_MXU_DTYPE), preferred_element_type=F32)

    return pl.pallas_call(
        body, grid=(n // ADA_TN,),
        in_specs=[pl.BlockSpec((16, D_MODEL), lambda j: (0, 0)), pl.BlockSpec((D_MODEL, ADA_TN), lambda j: (0, j))],
        out_specs=pl.BlockSpec((16, ADA_TN), lambda j: (0, j)), out_shape=jax.ShapeDtypeStruct((16, n), F32),
        compiler_params=_cp(("parallel",)), name="ada_part")(cc16, w_shard)


def _ada_bwd_shard(cc16, w_shard, d_lat, d_ctx):
    n = w_shard.shape[1]
    nj = n // ADA_TN

    def body(c_ref, w_ref, gl_ref, gc_ref, dw_ref, dc_ref):
        j = pl.program_id(0)
        s, pull = jax.vjp(_silu, c_ref[...])
        row = lax.broadcasted_iota(jnp.int32, (8, 1), 0)
        dctx = jnp.where(row == 0, jnp.sum(gc_ref[...], axis=0, keepdims=True), 0.0)
        mxu = lambda v: v.astype(_MXU_DTYPE)
        dw_ref[...] = (lax.dot_general(mxu(s[0:8]), mxu(gl_ref[...]), _TN, preferred_element_type=F32)
                       + lax.dot_general(mxu(s[8:16]), mxu(dctx), _TN, preferred_element_type=F32))
        ds = lax.dot_general(mxu(dctx), mxu(w_ref[...]), _NT, preferred_element_type=F32)

        @pl.when(j == 0)
        def _():
            dc_ref[...] = ds

        @pl.when(j > 0)
        def _():
            dc_ref[...] += ds

        @pl.when(j == nj - 1)
        def _():
            ct = jnp.concatenate([jnp.zeros((8, D_MODEL), F32), dc_ref[...]], axis=0)
            dc_ref[...] = 0.5 * pull(ct)[0][8:16]

    tile = pl.BlockSpec((8, ADA_TN), lambda j: (0, j))
    wspec = pl.BlockSpec((D_MODEL, ADA_TN), lambda j: (0, j))
    return pl.pallas_call(
        body, grid=(nj,), in_specs=[pl.BlockSpec((16, D_MODEL), lambda j: (0, 0)), wspec, tile, tile],
        out_specs=[wspec, pl.BlockSpec((8, D_MODEL), lambda j: (0, 0))],
        out_shape=[jax.ShapeDtypeStruct((D_MODEL, n), F32), jax.ShapeDtypeStruct((8, D_MODEL), F32)],
        compiler_params=_cp(("arbitrary",)), name="ada_bwd_shard")(cc16, w_shard, d_lat, d_ctx)


def _rowsum2(a, b):
    def body(a_ref, b_ref, o_ref):
        o_ref[...] = jnp.sum(a_ref[...], axis=0, keepdims=True) + jnp.sum(b_ref[...], axis=0, keepdims=True)

    return pl.pallas_call(body, out_shape=jax.ShapeDtypeStruct((1, a.shape[1]), F32), name="ada_b_grad")(a, b)


def _tail(h1, ff, mods, wf, tgt, nctx_t, tl):
    t = h1.shape[0]
    nt = t // tl

    def loss_fn(valid, h1v, ffv, g8, w, tg):
        h2 = h1v + g8[0:1] * ffv
        y = _rms(h2) * w
        err = (y - tg) ** 2
        return 0.5 * jnp.sum(jnp.mean(err, axis=-1, keepdims=True), axis=0, keepdims=True) * valid

    def body(h1_ref, ff_ref, g_ref, w_ref, t_ref, loss_ref, dh_ref, dff_ref, dg_ref, dw_ref):
        i = pl.program_id(0)
        valid = jnp.where(i < nctx_t, 0.0, 1.0)
        lv, pull = jax.vjp(functools.partial(loss_fn, valid), h1_ref[...], ff_ref[...].astype(F32), g_ref[...],
                           w_ref[...], t_ref[...])
        dh, dff, dg, dw, _ = pull(jnp.ones((1, 1), F32))
        dh_ref[...] = dh
        dff_ref[...] = dff.astype(dff_ref.dtype)
        lb = jnp.broadcast_to(lv, loss_ref.shape)

        @pl.when(i == 0)
        def _():
            loss_ref[...] = lb
            dg_ref[...] = dg
            dw_ref[...] = dw

        @pl.when(i > 0)
        def _():
            loss_ref[...] += lb
            dg_ref[...] += dg
            dw_ref[...] += dw

    tok = pl.BlockSpec((tl, D_MODEL), lambda i: (i, 0))
    return pl.pallas_call(
        body, grid=(nt,),
        in_specs=[tok, tok, pl.BlockSpec((8, D_MODEL), lambda i: (0, 5)), pl.BlockSpec((1, D_MODEL), lambda i: (0, 0)),
                  pl.BlockSpec((tl, D_MODEL), lambda i: (jnp.maximum(i - nctx_t, 0), 0))],
        out_specs=[pl.BlockSpec((8, 128), lambda i: (0, 0)), tok, tok, pl.BlockSpec((8, D_MODEL), lambda i: (0, 0)),
                   pl.BlockSpec((1, D_MODEL), lambda i: (0, 0))],
        out_shape=[jax.ShapeDtypeStruct((8, 128), F32), jax.ShapeDtypeStruct((t, D_MODEL), F32),
                   jax.ShapeDtypeStruct((t, D_MODEL), _MXU_DTYPE), jax.ShapeDtypeStruct((8, D_MODEL), F32),
                   jax.ShapeDtypeStruct((1, D_MODEL), F32)],
        compiler_params=_cp(("arbitrary",)), name="tail_loss")(h1, ff, mods, wf, tgt)


def _pack_w_in(w4):
    ns = w4.shape[2]
    placed = []
    for s0, s1, p0 in IN_SEGMENTS:
        for j in range(N_PLANE):
            lo, hi = max(s0, j * ns), min(s1, (j + 1) * ns)
            if lo < hi:
                placed.append((p0 + lo - s0, w4[j][:, lo - j * ns:hi - j * ns]))
    placed.sort(key=lambda e: e[0])
    pieces, end = [], 0
    for pos, piece in placed:
        assert pos == end, (pos, end)
        pieces.append(piece)
        end = pos + piece.shape[1]
    pieces.append(jnp.zeros((w4.shape[1], P_TOTAL - end), w4.dtype))
    return jnp.concatenate(pieces, axis=1)


def _unpack_w_in(g):
    ns = D_IN_PROJ // N_PLANE
    shards = []
    for j in range(N_PLANE):
        pieces = []
        for s0, s1, p0 in IN_SEGMENTS:
            lo, hi = max(s0, j * ns), min(s1, (j + 1) * ns)
            if lo < hi:
                pieces.append(g[:, p0 + lo - s0:p0 + hi - s0])
        shards.append(jnp.concatenate(pieces, axis=1))
    return jnp.stack(shards)


LATE_WEIGHTS = ['w_br_gdn', 'w_br_ssm', 'w_out', 'w_ffn_in', 'w_ffn_out']
COL_STACKED = ('w_in', 'w_ffn_in')


def _from_shards(n, g):
    if n in COL_STACKED:
        return g
    if SHARD_AXIS[n] == 0:
        return g.reshape(N_PLANE * g.shape[1], g.shape[2])
    return jnp.concatenate([g[j] for j in range(N_PLANE)], axis=1)


def _to_dest_blocks(n, g):
    if g.ndim == 3:
        return g
    if SHARD_AXIS[n] == 0:
        return g.reshape(N_PLANE, g.shape[0] // N_PLANE, g.shape[1])
    sz = g.shape[1] // N_PLANE
    return jnp.stack([g[:, j * sz:(j + 1) * sz] for j in range(N_PLANE)])


def _local_step(x, c, ctx, tgt, W, late_shards):
    lc, l = ctx.shape[0], x.shape[0]
    t = lc + l
    tl = 256
    assert lc == tl and l % tl == 0 and lc % CHUNK == 0
    nt, nctx_t, nctx = t // tl, lc // tl, lc // CHUNK
    act = _MXU_DTYPE
    r1 = lambda v: v.reshape(1, -1)

    xt = jnp.concatenate([ctx, x], axis=0)
    lin = 4 * lax.axis_index("x") + 2 * lax.axis_index("y") + lax.axis_index("c")
    c_all = _gather_all("gather_c", jnp.broadcast_to(c, (8, D_MODEL)))[:, 0]
    cc16 = jnp.concatenate([c_all, r1(W['c_ctx']), jnp.zeros((7, D_MODEL), F32)], axis=0)
    (parts,) = _plane_exchange("gather_mods", [_ada_part(cc16, W['ada_w'])], gather=True)
    mods_all = jnp.transpose(parts, (1, 0, 2)).reshape(16, -1) + r1(W['ada_b'])
    mods = jnp.concatenate([lax.dynamic_slice_in_dim(mods_all, lin, 1, axis=0), mods_all[8:9],
                            jnp.zeros((6, mods_all.shape[1]), F32)], axis=0)

    def mod(kk):
        return Op(mods, (8, D_MODEL), lambda j, i, kk=kk: (0, kk))

    def tokop(arr, w=D_MODEL, off=0, tl_=tl):
        if not isinstance(arr, list) and arr.ndim == 3:
            return Op(arr, (2, tl_, w), lambda j, i: (0, i, off + j), load=_sum_dirs)
        return Op(arr, (tl_, w), lambda j, i: (i, off + j))

    def outop(n, dtype, w=D_MODEL, tl_=tl):
        return Op(((t, n), dtype), (tl_, w), lambda j, i: (i, j))

    def parop(arr, w, off=0):
        return Op(arr, (arr.shape[0], w), lambda j, i: (0, off + j))

    def parout(rows, n, w):
        return Op(((rows, n), F32), (rows, w), lambda j, i: (0, j))

    n1w = r1(W['norm1_w'])
    pre_fn = functools.partial(_prenorm_fn, nctx_t)
    pre_tok, pre_par = [tokop(xt)], [parop(n1w, D_MODEL), mod(1), mod(0)]
    (a,) = _tw_fwd("prenorm_fwd", pre_fn, (1, nt), pre_tok + pre_par, [outop(D_MODEL, act)])
    wp = _pack_w_in(W['w_in'])
    proj = _matmul(a, wp[:, :P_SMALL], 'nn', "in_proj", out_dtype=act)
    proj_small = _matmul(a, wp[:, P_SMALL:], 'nn', "in_proj_small")

    gcw, gcb = W['gdn_conv_w'], r1(W['gdn_conv_b'])
    scw, scb = W['ssm_conv_w'], r1(W['ssm_conv_b'])
    conv_parts = {}

    def conv_part(name, mode, poff, cw, cb, woff, width):
        fn = functools.partial(_conv_fn, nctx_t, mode)
        bw = min(width, 1024)
        tok_ = [tokop(proj, bw, poff // bw)]
        par_ = [parop(cw, bw, woff // bw), parop(cb, bw, woff // bw)]
        conv_parts[name] = (fn, tok_, par_, width, bw, poff)
        (o,) = _tw_fwd("conv_" + name, fn, (width // bw, nt), tok_ + par_, [outop(width, F32, bw)])
        return o

    q = conv_part('q', 'q', P_QKV, gcw, gcb, 0, 1024)
    k = conv_part('k', 'k', P_QKV + 1024, gcw, gcb, 1024, 1024)
    v = conv_part('v', 'none', P_QKV + 2048, gcw, gcb, 2048, 1024)
    xs = conv_part('xs', 'none', P_XBC, scw, scb, 0, 2048)
    bm = conv_part('bm', 'none', P_XBC + 2048, scw, scb, 2048, 512)
    cm = conv_part('cm', 'none', P_XBC + 2560, scw, scb, 2560, 512)

    z16, z32 = jnp.zeros((16,), F32), jnp.zeros((32,), F32)
    p0 = jnp.concatenate([W['gdn_a_log'].reshape(-1), jnp.zeros((112,), F32)]).reshape(1, 128)
    p1 = jnp.concatenate([W['gdn_dt_bias'].reshape(-1), z16, W['ssm_dt_bias'].reshape(-1), z32]).reshape(1, 128)
    alog = jnp.concatenate([z32, W['ssm_a_log'].reshape(-1), z32]).reshape(1, 128)
    act_tok, act_par = [tokop(proj_small, 128, 0)], [parop(p0, 128), parop(p1, 128)]
    (sp,) = _tw_fwd("small_act", _act_fn, (1, nt), act_tok + act_par, [outop(128, F32, 128)])

    gather_late = PlaneExchange([late_shards[n] for n in LATE_WEIGHTS], gather=True)
    o_f, o_b, ss, tri, *late = _scan2_fwd("gdn_scan_fwd", _gdn_step2, [q, k, v, sp], 1024, GDN_HEADS, nctx,
                                          GDN_HEADS, ride=gather_late)
    W = dict(W, **{n: _from_shards(n, g) for n, g in zip(LATE_WEIGHTS, late)})
    ssd_consts = _ssd_consts()
    y2, hs = _scan_fwd("ssd_scan_fwd", _ssd_step, [xs, bm, cm, sp], [alog], ssd_consts, 2048, SSM_PAIRS, nctx)

    tlm = 128
    ntm = t // tlm
    gnw = r1(W['gdn_norm_w'])
    dl = jnp.repeat(W['ssm_d'], 64).reshape(1, 2048)
    snw = r1(W['ssm_norm_w'])
    mg_tok = [tokop([o_f, o_b], 1024, 0, tlm), tokop(proj, 1024, P_ZG // 1024, tlm)]
    mg_par = [parop(gnw, 128)]
    (og,) = _tw_fwd("mix_gdn", _mixg_fn, (1, ntm), mg_tok + mg_par, [outop(1024, act, 1024, tlm)])
    ms_tok = [tokop(y2, 2048, 0, tlm), tokop(xs, 2048, 0, tlm), tokop(proj, 2048, P_ZS // 2048, tlm)]
    ms_par = [parop(dl, 2048), parop(snw, 2048)]
    (yg,) = _tw_fwd("mix_ssm", _mixs_fn, (1, ntm), ms_tok + ms_par, [outop(2048, act, 2048, tlm)])

    pg = _matmul(og, W['w_br_gdn'], 'nn', "br_gdn", out_dtype=act)
    ps = _matmul(yg, W['w_br_ssm'], 'nn', "br_ssm", out_dtype=act)
    mr_tok = [tokop(proj, 2048, P_GATE // 2048), tokop(pg), tokop(ps)]
    (mrg,) = _tw_fwd("merge", _merge_fn, (1, nt), mr_tok, [outop(1024, act)])
    mo = _matmul(mrg, W['w_out'], 'nn', "w_out")

    n2w = r1(W['norm2_w'])
    n2_tok, n2_par = [tokop(xt), tokop(mo)], [mod(2), parop(n2w, D_MODEL), mod(4), mod(3)]
    h1, f = _tw_fwd("norm2_fwd", _norm2_fn, (1, nt), n2_tok + n2_par, [outop(1024, F32), outop(1024, act)])
    u2 = _matmul(f, W['w_ffn_in'], 'nn', "ffn_in", out_dtype=act, halves='out')
    swb = D_FF // 2
    both = lambda arr: Op(arr, (2, tl, swb), lambda j, i: (0, i, j))
    sw_tok = [both(u2)]
    (sw,) = _tw_fwd("swiglu", _swiglu_fn, (D_FF // swb, nt), sw_tok, [outop(D_FF, act, swb)])
    ff = _matmul(sw, W['w_ffn_out'], 'nn', "ffn_out")

    loss8, dh1, dff, dg2, dnf = _tail(h1, ff, mods, r1(W['norm_f_w']), tgt, nctx_t, tl)
    loss = loss8[0, 0]

    G = {}
    G['norm_f_w'] = dnf.reshape(-1)
    G['w_ffn_out'] = _matmul(sw, dff, 'tn', "d_ffn_out", out_dtype=_GRAD_DTYPE)
    dsw = _matmul(dff, W['w_ffn_out'], 'nt', "d_sw", out_dtype=act)
    (du2,) = _tw_bwd("swiglu_bwd", _swiglu_fn, (D_FF // swb, nt), sw_tok, [], [[tokop(dsw, swb)]],
                     [both(((2, t, D_FF), act))], [])
    G['w_ffn_in'] = _matmul(f, du2, 'tn', "d_ffn_in", out_dtype=_GRAD_DTYPE, stacked_out=True, halves='b')
    df = _matmul(du2, W['w_ffn_in'], 'nt', "d_f", halves='a')
    dxt1, dmo, dg1, dn2, dsc2, dsh2 = _tw_bwd(
        "norm2_bwd", _norm2_fn, (1, nt), n2_tok, n2_par, [[tokop(dh1)], [tokop(df)]],
        [outop(1024, F32), outop(1024, act)],
        [parout(8, 1024, 1024), parout(1, 1024, 1024), parout(8, 1024, 1024), parout(8, 1024, 1024)])
    G['norm2_w'] = dn2.reshape(-1)
    G['w_out'] = _matmul(mrg, dmo, 'tn', "d_w_out", out_dtype=_GRAD_DTYPE)
    dmrg = _matmul(dmo, W['w_out'], 'nt', "d_mrg", out_dtype=act)
    def win(off, w, tl_=tl):
        return Op(((t, P_TOTAL), act), (tl_, w), lambda j, i: (i, off // w + j))

    dproj = lax.empty((t, P_TOTAL), act)
    dproj, dpg, dps = _tw_bwd("merge_bwd", _merge_fn, (1, nt), mr_tok, [], [[tokop(dmrg)]],
                              [win(P_GATE, 2048), outop(1024, act), outop(1024, act)], [], into={0: dproj})
    G['w_br_gdn'] = _matmul(og, dpg, 'tn', "d_br_gdn", out_dtype=_GRAD_DTYPE)
    G['w_br_ssm'] = _matmul(yg, dps, 'tn', "d_br_ssm", out_dtype=_GRAD_DTYPE)
    dog = _matmul(dpg, W['w_br_gdn'], 'nt', "d_og")
    dyg = _matmul(dps, W['w_br_ssm'], 'nt', "d_yg")

    do, dproj, dgnw = _tw_bwd("mix_gdn_bwd", _mixg_fn, (1, ntm), mg_tok, mg_par, [[tokop(dog, 1024, 0, tlm)]],
                              [outop(1024, F32, 1024, tlm), win(P_ZG, 1024, tlm)], [parout(1, 128, 128)],
                              into={1: dproj})
    G['gdn_norm_w'] = dgnw.reshape(-1)
    dy, dxs_a, dproj, ddl, dsnw = _tw_bwd(
        "mix_ssm_bwd", _mixs_fn, (1, ntm), ms_tok, ms_par, [[tokop(dyg, 2048, 0, tlm)]],
        [outop(2048, F32, 2048, tlm), outop(2048, F32, 2048, tlm), win(P_ZS, 2048, tlm)],
        [parout(1, 2048, 2048), parout(1, 2048, 2048)], into={2: dproj})
    G['ssm_d'] = ddl.reshape(SSM_HEADS, 64).sum(axis=1)
    G['ssm_norm_w'] = dsnw.reshape(-1)

    dq2, dk2, dv2, dsp_g = _scan_bwd("gdn_scan_bwd", _gdn_step, [q, k, v, sp], [], ss, do, GDN_HEADS, nctx, aux=tri)
    received = {}
    scatter_late = PlaneExchange([_to_dest_blocks(n, G.pop(n)) for n in LATE_WEIGHTS], gather=False)
    dxs2, dbm2, dcm2, dsp_s, dalog, *got = _scan_bwd("ssd_scan_bwd", _ssd_step, [xs, bm, cm, sp], [alog], hs, dy,
                                                     SSM_PAIRS, nctx, ride=scatter_late, consts=ssd_consts)
    received.update(zip(LATE_WEIGHTS, got))
    G['ssm_a_log'] = dalog[0, 32:96].reshape(2, SSM_HEADS)

    dconv_w, dconv_b = {}, {}

    def conv_bwd(name, cots_, buf):
        fn, tok_, par_, width, bw, poff = conv_parts[name]
        buf, dconv_w[name], dconv_b[name] = _tw_bwd(
            "conv_" + name + "_bwd", fn, (width // bw, nt), tok_, par_, [[tokop(c_, bw) for c_ in cots_]],
            [win(poff, bw)], [parout(3, width, bw), parout(1, width, bw)], into={0: buf})
        return buf

    dproj = conv_bwd('q', [dq2], dproj)
    dproj = conv_bwd('k', [dk2], dproj)
    dproj = conv_bwd('v', [dv2], dproj)
    dproj = conv_bwd('xs', [dxs2, dxs_a], dproj)
    dproj = conv_bwd('bm', [dbm2], dproj)
    dproj = conv_bwd('cm', [dcm2], dproj)
    G['gdn_conv_w'] = jnp.concatenate([dconv_w['q'], dconv_w['k'], dconv_w['v']], axis=1)
    G['gdn_conv_b'] = [dconv_b['q'], dconv_b['k'], dconv_b['v']]
    G['ssm_conv_w'] = jnp.concatenate([dconv_w['xs'], dconv_w['bm'], dconv_w['cm']], axis=1)
    G['ssm_conv_b'] = [dconv_b['xs'], dconv_b['bm'], dconv_b['cm']]

    dproj, dp0, dp1 = _tw_bwd("small_act_bwd", _act_fn, (1, nt), act_tok, act_par,
                              [[tokop(dsp_g, 128), tokop(dsp_s, 128)]], [win(P_SMALL, P_TOTAL - P_SMALL)],
                              [parout(1, 128, 128), parout(1, 128, 128)], into={0: dproj})
    G['gdn_a_log'] = dp0[0, 0:16].reshape(2, GDN_HEADS)
    G['gdn_dt_bias'] = dp1[0, 0:16].reshape(2, GDN_HEADS)
    G['ssm_dt_bias'] = dp1[0, 32:96].reshape(2, SSM_HEADS)

    g_w_in = _unpack_w_in(_matmul(a, dproj, 'tn', "d_w_in", out_dtype=_GRAD_DTYPE))
    da, (received['w_in'],) = _matmul(dproj, wp, 'nt', "d_a", ride=PlaneExchange([g_w_in], gather=False))

    gx_out = Op(((l, D_MODEL), F32), (tl, D_MODEL), lambda j, i: (jnp.maximum(i - nctx_t, 0), 0))
    grad_x, dn1, dsc1, dsh1 = _tw_bwd(
        "prenorm_bwd", pre_fn, (1, nt), pre_tok, pre_par, [[tokop(da)]], [gx_out],
        [parout(1, 1024, 1024), parout(8, 1024, 1024), parout(8, 1024, 1024)], tok_add=tokop(dxt1),
        sem=("arbitrary", "arbitrary"))
    G['norm1_w'] = dn1.reshape(-1)
    dmods = jnp.concatenate([dsh1, dsc1, dg1, dsh2, dsc2, dg2], axis=1)
    dm_all = _gather_all("gather_dmods", dmods)
    ns = W['ada_w'].shape[1]
    mine = lax.dynamic_slice_in_dim(dm_all, (2 * lax.axis_index("x") + lax.axis_index("y")) * ns, ns, axis=2)
    G['ada_w'], dcc = _ada_bwd_shard(cc16, W['ada_w'], mine[:, 0], mine[:, 1])
    G['ada_b'] = jnp.where(lin == 0, _rowsum2(dm_all[:, 0], dm_all[:, 1]).reshape(-1), 0.0)
    G['c_ctx'] = dcc[0]
    return loss, grad_x, G, received


def _row_tile(r, c):
    for tr in (512, 256, 128, 64, 32, 16, 8):
        if r % tr == 0 and tr * c * 4 <= (1 << 20):
            return tr
    return r


def _sum4(name, rv):
    _, r, c = rv.shape
    tr = _row_tile(r, c)

    def body(r_ref, o_ref):
        o_ref[...] = ((r_ref[0].astype(F32) + r_ref[1].astype(F32)) + r_ref[2].astype(F32)) + r_ref[3].astype(F32)

    return pl.pallas_call(
        body, grid=(r // tr,), in_specs=[pl.BlockSpec((N_PLANE, tr, c), lambda i: (0, i, 0))],
        out_specs=pl.BlockSpec((tr, c), lambda i: (i, 0)), out_shape=jax.ShapeDtypeStruct((r, c), F32),
        compiler_params=_cp(("parallel",)), name=name)(rv)


def _adamw(name, w, m, v, p, q=None):
    r, c = w.shape[-2:]
    tr = _row_tile(r, c)
    grads = [p] if q is None else [p, q]
    lead = (0,) * (w.ndim - 2)

    def body(w_ref, m_ref, v_ref, *refs):
        g_ref, d_ref, mo_ref, vo_ref = refs[len(grads):]
        g = refs[0][...] if q is None else refs[0][...] + refs[1][...]
        at = lead if lead else Ellipsis
        mn = ADAM_B1 * m_ref[at] + (1.0 - ADAM_B1) * g
        vn = ADAM_B2 * v_ref[at] + (1.0 - ADAM_B2) * jnp.square(g)
        m_hat = mn / (1.0 - ADAM_B1 ** ADAM_STEP)
        v_hat = vn / (1.0 - ADAM_B2 ** ADAM_STEP)
        g_ref[at] = g
        d_ref[at] = -ADAM_LR * (m_hat / (jnp.sqrt(v_hat) + ADAM_EPS) + ADAM_WD * w_ref[at])
        mo_ref[at] = mn
        vo_ref[at] = vn

    gspec = pl.BlockSpec((tr, c), lambda i: (i, 0))
    wspec = pl.BlockSpec((1,) * len(lead) + (tr, c), lambda i: lead + (i, 0))
    return pl.pallas_call(
        body, grid=(r // tr,), in_specs=[wspec] * 3 + [gspec] * len(grads), out_specs=[wspec] * 4,
        out_shape=[jax.ShapeDtypeStruct(w.shape, F32)] * 4, compiler_params=_cp(("parallel",)), name=name)(w, m, v, *grads)


SMALL_ROWS = 24


def _pack_small(d):
    pieces = [p.reshape(-1) for n in SMALL for p in (d[n] if isinstance(d[n], list) else [d[n]])]
    v = jnp.concatenate(pieces)
    return jnp.pad(v, (0, SMALL_ROWS * 1024 - v.shape[0])).reshape(SMALL_ROWS, 1024)


def _unpack_small(buf, like):
    v = buf.reshape(-1)
    out, off = {}, 0
    for n in SMALL:
        sz = like[n].size
        out[n] = v[off:off + sz].reshape(like[n].shape)
        off += sz
    return out


def kernel(x, c, ctx, c_ctx, ada_w, ada_b, norm1_w, w_in, gdn_conv_w, gdn_conv_b, gdn_a_log, gdn_dt_bias, gdn_norm_w, ssm_conv_w, ssm_conv_b, ssm_a_log, ssm_dt_bias, ssm_d, ssm_norm_w, w_br_gdn, w_br_ssm, w_out, norm2_w, w_ffn_in, w_ffn_out, norm_f_w, loss_target, m_c_ctx, m_ada_w, m_ada_b, m_norm1_w, m_w_in, m_gdn_conv_w, m_gdn_conv_b, m_gdn_a_log, m_gdn_dt_bias, m_gdn_norm_w, m_ssm_conv_w, m_ssm_conv_b, m_ssm_a_log, m_ssm_dt_bias, m_ssm_d, m_ssm_norm_w, m_w_br_gdn, m_w_br_ssm, m_w_out, m_norm2_w, m_w_ffn_in, m_w_ffn_out, m_norm_f_w, v_c_ctx, v_ada_w, v_ada_b, v_norm1_w, v_w_in, v_gdn_conv_w, v_gdn_conv_b, v_gdn_a_log, v_gdn_dt_bias, v_gdn_norm_w, v_ssm_conv_w, v_ssm_conv_b, v_ssm_a_log, v_ssm_dt_bias, v_ssm_d, v_ssm_norm_w, v_w_br_gdn, v_w_br_ssm, v_w_out, v_norm2_w, v_w_ffn_in, v_w_ffn_out, v_norm_f_w):
    args = dict(locals())
    wl = {n: args[n] for n in WEIGHTS}
    ml = {n: args['m_' + n] for n in WEIGHTS}
    vl = {n: args['v_' + n] for n in WEIGHTS}

    def nodepth(n, a):
        return a if n in ('c_ctx', 'norm_f_w') else a[0]

    shard = {n: nodepth(n, wl[n]).astype(_MXU_DTYPE if n in MXU_WEIGHTS else F32) for n in SHARDED}
    first = [n for n in SHARDED if n not in LATE_WEIGHTS and n != 'ada_w']
    W = {n: nodepth(n, wl[n]) for n in SMALL}
    W['ada_w'] = shard['ada_w']
    for n, g in zip(first, _gather_two_level("all_gather_plane", [shard[n] for n in first])):
        W[n] = _from_shards(n, g)

    loss_local, grad_x, G, received = _local_step(x[0], c, ctx[0], loss_target[0], W,
                                                  late_shards={n: shard[n] for n in LATE_WEIGHTS})
    loss = lax.psum(loss_local, ("x", "y", "c"))

    small_g = _pack_small(G)
    last = [n for n in SHARDED if n not in received and n != 'ada_w']
    send = [_to_dest_blocks(n, G[n]) for n in last] + [jnp.broadcast_to(small_g[None], (N_PLANE,) + small_g.shape)]
    received.update(zip(last + ['small'], _plane_exchange("scatter_plane", send, gather=False)))
    names = [n for n in SHARDED if n != 'ada_w'] + ['small']
    plane_sum = [_sum4("sum4_" + n, received[n]) for n in names]
    other = _swap_sibling(plane_sum)

    wd = {n: nodepth(n, wl[n]) for n in WEIGHTS}
    md = {n: nodepth(n, ml[n]) for n in WEIGHTS}
    vd = {n: nodepth(n, vl[n]) for n in WEIGHTS}
    res = {'ada_w': _adamw("adamw_ada_w", wl['ada_w'], ml['ada_w'], vl['ada_w'], G['ada_w'])}
    for n, p, q in zip(names, plane_sum, other):
        if n == 'small':
            outs = _adamw("adamw_small", _pack_small(wd), _pack_small(md), _pack_small(vd), p, q)
            unpacked = [_unpack_small(o, wd) for o in outs]
            for sn in SMALL:
                res[sn] = [u[sn].reshape(wl[sn].shape) for u in unpacked]
        else:
            res[n] = _adamw("adamw_" + n, wl[n], ml[n], vl[n], p, q)
    flat = [res[n][kind] for kind in range(4) for n in WEIGHTS]
    return (loss, grad_x[None], *flat)
```

```python
import functools

import jax
import jax.numpy as jnp
from jax import lax
from jax.experimental import pallas as pl
from jax.experimental.pallas import tpu as pltpu

F32 = jnp.float32
HI = lax.Precision.HIGHEST
_MXU_DTYPE = jnp.bfloat16
_SCAN_DTYPE = jnp.bfloat16
_GRAD_DTYPE = jnp.bfloat16

D_MODEL = 1024
EPS = 1e-6
CHUNK = 64
GRID_W = 64
GDN_HEADS = 8
GDN_DK = 128
SSM_HEADS = 32
SSM_PAIRS = 16
SSD_GROUP = 1
D_FF = 2816
D_IN_PROJ = 11360
N_PLANE = 4
P_QKV, P_ZG, P_GATE, P_ZS, P_XBC, P_SMALL, P_TOTAL = 0, 3072, 4096, 6144, 8192, 11264, 11776
IN_SEGMENTS = [(0, 3072, P_QKV), (3072, 4096, P_ZG), (4096, 4112, P_SMALL), (4112, 4128, P_SMALL + 16),
               (4128, 6176, P_ZS), (6176, 9248, P_XBC), (9248, 9312, P_SMALL + 32), (9312, 11360, P_GATE)]
ADAM_LR, ADAM_B1, ADAM_B2, ADAM_EPS, ADAM_WD, ADAM_STEP = 0.001, 0.9, 0.999, 1e-08, 0.01, 10

VMEM_LIMIT = 48 * 1024 * 1024

WEIGHTS = ['c_ctx', 'ada_w', 'ada_b', 'norm1_w', 'w_in', 'gdn_conv_w', 'gdn_conv_b', 'gdn_a_log', 'gdn_dt_bias',
           'gdn_norm_w', 'ssm_conv_w', 'ssm_conv_b', 'ssm_a_log', 'ssm_dt_bias', 'ssm_d', 'ssm_norm_w', 'w_br_gdn',
           'w_br_ssm', 'w_out', 'norm2_w', 'w_ffn_in', 'w_ffn_out', 'norm_f_w']
SHARD_AXIS = {'ada_w': 1, 'w_in': 1, 'gdn_conv_w': 1, 'ssm_conv_w': 1, 'w_br_gdn': 0, 'w_br_ssm': 0, 'w_out': 0,
              'w_ffn_in': 1, 'w_ffn_out': 0}
SHARDED = [n for n in WEIGHTS if n in SHARD_AXIS]
SMALL = [n for n in WEIGHTS if n not in SHARD_AXIS]
MXU_WEIGHTS = ['ada_w', 'w_in', 'w_br_gdn', 'w_br_ssm', 'w_out', 'w_ffn_in', 'w_ffn_out']


def _cp(sem):
    return pltpu.CompilerParams(dimension_semantics=sem, vmem_limit_bytes=VMEM_LIMIT)


def _pick(n, cands):
    for c in cands:
        if n % c == 0:
            return c
    raise ValueError(f"no tile for {n}")


HBM = pl.BlockSpec(memory_space=pl.ANY)
MESH = pl.DeviceIdType.MESH


def _plane_peers():
    x, y, c = lax.axis_index("x"), lax.axis_index("y"), lax.axis_index("c")
    return (x, y, c), [(1 - x, y, c), (x, 1 - y, c), (1 - x, 1 - y, c)]


class PlaneExchange:
    def __init__(self, arrs, gather):
        self.arrs, self.gather, self.n = list(arrs), gather, len(arrs)
        self.in_specs = [HBM] * self.n
        self.out_specs = [HBM] * self.n
        self.out_shape = [jax.ShapeDtypeStruct((N_PLANE,) + a.shape if gather else a.shape, a.dtype) for a in self.arrs]
        self.scratch = [pltpu.SemaphoreType.DMA((3 * self.n,)), pltpu.SemaphoreType.DMA((3 * self.n,)),
                        pltpu.SemaphoreType.DMA((self.n,))]

    def _copies(self, ins, outs, sems):
        send_sems, recv_sems, local_sems = sems
        (x, y, c), peers = _plane_peers()
        me = 2 * x + y
        copies = []
        for ti in range(self.n):
            src = ins[ti] if self.gather else ins[ti].at[me]
            copies.append(pltpu.make_async_copy(src, outs[ti].at[me], local_sems.at[ti]))
            for kk, (px, py, pc) in enumerate(peers):
                src = ins[ti] if self.gather else ins[ti].at[2 * px + py]
                copies.append(pltpu.make_async_remote_copy(
                    src_ref=src, dst_ref=outs[ti].at[me], send_sem=send_sems.at[3 * ti + kk],
                    recv_sem=recv_sems.at[3 * ti + kk], device_id=(px, py, pc), device_id_type=MESH))
        return copies

    def start(self, ins, outs, sems):
        for cp in self._copies(ins, outs, sems):
            cp.start()

    def wait(self, ins, outs, sems):
        for cp in self._copies(ins, outs, sems):
            cp.wait()


def _plane_exchange(name, arrs, gather):
    ex = PlaneExchange(arrs, gather)
    n = ex.n

    def body(*refs):
        ins, outs, sems = refs[:n], refs[n:2 * n], refs[2 * n:]
        ex.start(ins, outs, sems)
        ex.wait(ins, outs, sems)

    return pl.pallas_call(body, in_specs=ex.in_specs, out_specs=ex.out_specs, out_shape=ex.out_shape,
                          scratch_shapes=ex.scratch, name=name)(*arrs)


def _gather_two_level(name, arrs):
    n = len(arrs)
    split = [a.shape[0] % 32 == 0 for a in arrs]

    def body(*refs):
        ins, outs = refs[:n], refs[n:2 * n]
        ici_send, ici_recv, d2d_send, d2d_recv, local_sems = refs[2 * n:]
        (x, y, c), peers = _plane_peers()
        me = 2 * x + y

        def part(ref, ti):
            if not split[ti]:
                return ref
            rows = arrs[ti].shape[0] // 2
            return ref.at[pl.ds(c * rows, rows)]

        local = [pltpu.make_async_copy(ins[ti], outs[ti].at[me], local_sems.at[ti]) for ti in range(n)]
        ici = [pltpu.make_async_remote_copy(
            src_ref=part(ins[ti], ti), dst_ref=part(outs[ti].at[me], ti), send_sem=ici_send.at[3 * ti + kk],
            recv_sem=ici_recv.at[3 * ti + kk], device_id=peer, device_id_type=MESH)
            for ti in range(n) for kk, peer in enumerate(peers)]
        for cp in local + ici:
            cp.start()
        d2d = []
        for ti in range(n):
            for kk, (px, py, pc) in enumerate(peers):
                ici[3 * ti + kk].wait_recv()
                if split[ti]:
                    piece = part(outs[ti].at[2 * px + py], ti)
                    cp = pltpu.make_async_remote_copy(
                        src_ref=piece, dst_ref=piece, send_sem=d2d_send.at[3 * ti + kk],
                        recv_sem=d2d_recv.at[3 * ti + kk], device_id=(x, y, 1 - c), device_id_type=MESH)
                    cp.start()
                    d2d.append(cp)
        for cp in ici:
            cp.wait_send()
        for cp in d2d:
            cp.wait()
        for cp in local:
            cp.wait()

    return pl.pallas_call(
        body, in_specs=[HBM] * n, out_specs=[HBM] * n,
        out_shape=[jax.ShapeDtypeStruct((N_PLANE,) + a.shape, a.dtype) for a in arrs],
        scratch_shapes=[pltpu.SemaphoreType.DMA((3 * n,))] * 4 + [pltpu.SemaphoreType.DMA((n,))], name=name)(*arrs)


def _gather_all(name, arr):
    flips = [(fx, fy, fc) for fx in (0, 1) for fy in (0, 1) for fc in (0, 1)][1:]

    def body(a_ref, o_ref, send_sems, recv_sems, local_sem):
        x, y, c = lax.axis_index("x"), lax.axis_index("y"), lax.axis_index("c")
        me = 4 * x + 2 * y + c
        copies = [pltpu.make_async_copy(a_ref, o_ref.at[me], local_sem)]
        for kk, (fx, fy, fc) in enumerate(flips):
            peer = (1 - x if fx else x, 1 - y if fy else y, 1 - c if fc else c)
            copies.append(pltpu.make_async_remote_copy(
                src_ref=a_ref, dst_ref=o_ref.at[me], send_sem=send_sems.at[kk], recv_sem=recv_sems.at[kk],
                device_id=peer, device_id_type=MESH))
        for cp in copies:
            cp.start()
        for cp in copies:
            cp.wait()

    return pl.pallas_call(
        body, in_specs=[HBM], out_specs=HBM, out_shape=jax.ShapeDtypeStruct((8,) + arr.shape, arr.dtype),
        scratch_shapes=[pltpu.SemaphoreType.DMA((7,)), pltpu.SemaphoreType.DMA((7,)), pltpu.SemaphoreType.DMA],
        name=name)(arr)


def _swap_sibling(arrs):
    n = len(arrs)

    def body(*refs):
        ins, outs, send_sems, recv_sems = refs[:n], refs[n:2 * n], refs[2 * n], refs[2 * n + 1]
        x, y, c = lax.axis_index("x"), lax.axis_index("y"), lax.axis_index("c")
        copies = [pltpu.make_async_remote_copy(src_ref=ins[ti], dst_ref=outs[ti], send_sem=send_sems.at[ti],
                                               recv_sem=recv_sems.at[ti], device_id=(x, y, 1 - c), device_id_type=MESH)
                  for ti in range(n)]
        for cp in copies:
            cp.start()
        for cp in copies:
            cp.wait()

    return pl.pallas_call(
        body, in_specs=[HBM] * n, out_specs=[HBM] * n, out_shape=[jax.ShapeDtypeStruct(a.shape, a.dtype) for a in arrs],
        scratch_shapes=[pltpu.SemaphoreType.DMA((n,)), pltpu.SemaphoreType.DMA((n,))], name="swap_sibling")(*arrs)


MATMUL_VMEM_BUDGET = 36 * 1024 * 1024
TILE_CANDIDATES = (2944, 2816, 1408, 1024, 768, 512, 256, 128)
STEP_COST_BYTES = 1 << 20


def _matmul_tiles(m, n, k, ab, bb, ob, tn_fixed=None, tk_fixed=None):
    def cands(dim, whole_up_to):
        out = [c for c in TILE_CANDIDATES if dim % c == 0]
        if dim <= whole_up_to and dim not in out:
            out.append(dim)
        return out

    best = None
    for tm in cands(m, 3072):
        for tn in ([tn_fixed] if tn_fixed else cands(n, 3072)):
            for tk in ([tk_fixed] if tk_fixed else cands(k, 2048)):
                gi, gj, gk = m // tm, n // tn, k // tk
                vmem = 2 * (tm * tk * ab + tk * tn * bb + tm * tn * ob) + (tm * tn * 4 if gk > 1 else 0)
                if vmem > MATMUL_VMEM_BUDGET:
                    continue
                a_reads = 1 if gk == 1 else gj
                b_reads = 1 if (gk == 1 and gj == 1) else gi
                cost = (m * k * ab * a_reads + k * n * bb * b_reads + m * n * ob + gi * gj * gk * STEP_COST_BYTES)
                if best is None or cost < best[0]:
                    best = (cost, tm, tn, tk)
    assert best is not None, (m, n, k)
    return best[1:]


def _matmul(a, b, form, name, out_dtype=F32, stacked_out=False, ride=None, halves=None):
    dims = {'nn': (((1,), (0,)), ((), ())), 'nt': (((1,), (1,)), ((), ())), 'tn': (((0,), (0,)), ((), ()))}[form]
    stacked_b = b.ndim == 3 and halves != 'b'
    ns = b.shape[2] if stacked_b else None
    if form == 'nn':
        m, k = a.shape
        n = b.shape[0] * ns if stacked_b else b.shape[1]
    elif form == 'nt':
        m, k = (a.shape[1], 2 * a.shape[2]) if halves == 'a' else a.shape
        n = b.shape[1] if stacked_b else b.shape[0]
    else:
        k, m = a.shape
        n = 2 * b.shape[2] if halves == 'b' else b.shape[1]
    ob = jnp.dtype(out_dtype).itemsize
    tm, tn, tk = _matmul_tiles(m, n, k, a.dtype.itemsize, b.dtype.itemsize, ob,
                               tn_fixed=(n // N_PLANE if stacked_out else ns if (stacked_b and form == 'nn') else None),
                               tk_fixed=(ns if (stacked_b and form == 'nt') else None))
    nk = k // tk
    grid = (m // tm, n // tn, nk)
    n_ride = ride.n if ride is not None else 0

    def body(*refs):
        a_ref, b_ref = refs[0], refs[1]
        ride_in = refs[2:2 + n_ride]
        o_ref = refs[2 + n_ride]
        ride_out = refs[3 + n_ride:3 + 2 * n_ride]
        acc_ref = refs[3 + 2 * n_ride]
        sems = refs[4 + 2 * n_ride:]
        i, j, kk = pl.program_id(0), pl.program_id(1), pl.program_id(2)
        if ride is not None:
            @pl.when((i == 0) & (j == 0) & (kk == 0))
            def _():
                ride.start(ride_in, ride_out, sems)

        def put(val):
            if stacked_out or halves == 'out':
                o_ref[0] = val.astype(o_ref.dtype)
            else:
                o_ref[...] = val.astype(o_ref.dtype)

        av = a_ref[0] if halves == 'a' else a_ref[...]
        bv = b_ref[0] if (stacked_b or halves == 'b') else b_ref[...]
        part = lax.dot_general(av.astype(_MXU_DTYPE), bv.astype(_MXU_DTYPE), dims, preferred_element_type=F32)
        if nk == 1:
            put(part)
        else:
            @pl.when(kk == 0)
            def _():
                acc_ref[...] = part

            @pl.when(kk > 0)
            def _():
                acc_ref[...] += part

            @pl.when(kk == nk - 1)
            def _():
                put(acc_ref[...])

        if ride is not None:
            @pl.when((i == grid[0] - 1) & (j == grid[1] - 1) & (kk == nk - 1))
            def _():
                ride.wait(ride_in, ride_out, sems)

    if form == 'nn':
        a_spec = pl.BlockSpec((tm, tk), lambda i, j, kk: (i, kk))
        b_spec = (pl.BlockSpec((1, tk, tn), lambda i, j, kk: (j, kk, 0)) if stacked_b
                  else pl.BlockSpec((tk, tn), lambda i, j, kk: (kk, j)))
    elif form == 'nt':
        a_spec = pl.BlockSpec((tm, tk), lambda i, j, kk: (i, kk))
        b_spec = (pl.BlockSpec((1, tn, tk), lambda i, j, kk: (kk, j, 0)) if stacked_b
                  else pl.BlockSpec((tn, tk), lambda i, j, kk: (j, kk)))
    else:
        a_spec = pl.BlockSpec((tk, tm), lambda i, j, kk: (kk, i))
        b_spec = pl.BlockSpec((tk, tn), lambda i, j, kk: (kk, j))
    hj, hk = grid[1] // 2, nk // 2
    if halves == 'a':
        a_spec = pl.BlockSpec((1, tm, tk), lambda i, j, kk: (kk // hk, i, kk % hk))
    if halves == 'b':
        b_spec = pl.BlockSpec((1, tk, tn), lambda i, j, kk: (j // hj, kk, j % hj))
    if stacked_out:
        o_spec = pl.BlockSpec((1, tm, tn), lambda i, j, kk: (j, i, 0))
        o_shape = jax.ShapeDtypeStruct((N_PLANE, m, tn), out_dtype)
    elif halves == 'out':
        o_spec = pl.BlockSpec((1, tm, tn), lambda i, j, kk: (j // hj, i, j % hj))
        o_shape = jax.ShapeDtypeStruct((2, m, n // 2), out_dtype)
    else:
        o_spec = pl.BlockSpec((tm, tn), lambda i, j, kk: (i, j))
        o_shape = jax.ShapeDtypeStruct((m, n), out_dtype)
    acc_shape = (tm, tn) if nk > 1 else (8, 128)
    if ride is None:
        return pl.pallas_call(
            body, grid=grid, in_specs=[a_spec, b_spec], out_specs=o_spec, out_shape=o_shape,
            scratch_shapes=[pltpu.VMEM(acc_shape, F32)],
            compiler_params=_cp(("parallel", "parallel", "arbitrary")), name=name)(a, b)
    res = pl.pallas_call(
        body, grid=grid, in_specs=[a_spec, b_spec] + ride.in_specs, out_specs=[o_spec] + ride.out_specs,
        out_shape=[o_shape] + ride.out_shape, scratch_shapes=[pltpu.VMEM(acc_shape, F32)] + ride.scratch,
        compiler_params=_cp(("arbitrary", "arbitrary", "arbitrary")), name=name)(a, b, *ride.arrs)
    return res[0], res[1:]


class Op:
    def __init__(self, arr, bs, im, load=None):
        self.arr, self.bs, self.im = arr, bs, im
        self.arrs = list(arr) if isinstance(arr, list) else [arr]
        self.load = load or (lambda r: r[...].astype(F32))

    def spec(self):
        return pl.BlockSpec(self.bs, self.im)

    def value(self, it):
        return functools.reduce(lambda u, w: u + w, [self.load(next(it)) for _ in self.arrs])


def _op_specs(ops):
    return [op.spec() for op in ops for _ in op.arrs]


def _op_arrays(ops):
    return [a for op in ops for a in op.arrs]


def _sum_dirs(r):
    return r[0].astype(F32) + r[1].astype(F32)


def _tw_fwd(name, fn, grid, ins, outs):
    def body(*refs):
        info = (pl.program_id(0), pl.program_id(1))
        it = iter(refs)
        vals = [op.value(it) for op in ins]
        res = fn(info, *vals)
        for r, v in zip(it, res):
            r[...] = v.astype(r.dtype)

    return pl.pallas_call(
        body, grid=grid, in_specs=_op_specs(ins), out_specs=[op.spec() for op in outs],
        out_shape=[jax.ShapeDtypeStruct(*op.arr) for op in outs],
        compiler_params=_cp(("parallel", "arbitrary")), name=name)(*_op_arrays(ins))


def _tw_bwd(name, fn, grid, tok, par, cots, tok_out, par_out, tok_add=None, sem=("parallel", "arbitrary"), into=None):
    n_tok = len(tok)
    flat_cots = [op for group in cots for op in group]
    extra = [tok_add] if tok_add is not None else []
    out_ops = list(tok_out) + list(par_out)
    into = into or {}

    def body(*refs):
        info = (pl.program_id(0), pl.program_id(1))
        it = iter(refs)
        tok_v = [op.value(it) for op in tok]
        par_v = [op.value(it) for op in par]
        cot_v = [functools.reduce(lambda u, w: u + w, [op.value(it) for op in group]) for group in cots]
        add_v = [op.value(it) for op in extra]
        for _ in into:
            next(it)
        _, pull = jax.vjp(lambda *a: fn(info, *a), *tok_v, *par_v)
        grads = pull(tuple(cot_v))
        for i in range(n_tok):
            r = next(it)
            g = grads[i] + add_v[0] if (i == 0 and add_v) else grads[i]
            if r.shape[-1] > g.shape[-1]:
                g = jnp.concatenate([g, jnp.zeros((g.shape[0], r.shape[-1] - g.shape[-1]), g.dtype)], axis=1)
            r[...] = g.astype(r.dtype)
        first = pl.program_id(1) == 0
        for i in range(len(par)):
            r = next(it)
            g = grads[n_tok + i]

            @pl.when(first)
            def _(r=r, g=g):
                r[...] = g

            @pl.when(jnp.logical_not(first))
            def _(r=r, g=g):
                r[...] += g

    ops = tok + par + flat_cots + extra
    n_in = len(_op_arrays(ops))
    return pl.pallas_call(
        body, grid=grid, in_specs=_op_specs(ops) + [HBM] * len(into), out_specs=[op.spec() for op in out_ops],
        out_shape=[jax.ShapeDtypeStruct(*op.arr) for op in out_ops],
        input_output_aliases={n_in + k: i for k, i in enumerate(into)},
        compiler_params=_cp(sem), name=name)(*_op_arrays(ops), *into.values())


def _silu(x):
    return x * jax.nn.sigmoid(x)


def _rms(x):
    return x * lax.rsqrt(jnp.mean(x * x, axis=-1, keepdims=True) + EPS)


@functools.partial(jax.custom_vjp, nondiff_argnums=(1,))
def _shift_rows(x, k):
    return pltpu.roll(x, k % x.shape[0], 0)


def _shift_rows_fwd(x, k):
    return _shift_rows(x, k), None


def _shift_rows_bwd(k, _, g):
    return (_shift_rows(g, -k),)


_shift_rows.defvjp(_shift_rows_fwd, _shift_rows_bwd)


def _prenorm_fn(nctx_t, info, x, w, sc8, sh8):
    is_ctx = info[1] < nctx_t
    sc = jnp.where(is_ctx, sc8[1:2], sc8[0:1])
    sh = jnp.where(is_ctx, sh8[1:2], sh8[0:1])
    return (_rms(x) * w * (1.0 + sc) + sh,)


def _conv_fn(lc, mode, info, x, w, b):
    n, c = x.shape
    g = info[1] * n + lax.broadcasted_iota(jnp.int32, (n, 1), 0)
    is_ctx = g < lc
    rr = jnp.where(is_ctx, g, g % GRID_W)
    first = rr == 0
    last = rr == jnp.where(is_ctx, lc - 1, GRID_W - 1)
    prev = jnp.where(first, 0.0, _shift_rows(x, 1))
    nxt = jnp.where(last, 0.0, _shift_rows(x, -1))
    y = b + prev * w[0:1] + x * w[1:2] + nxt * w[2:3]
    y = _silu(y)
    if mode == 'none':
        return (y,)
    scale = GDN_DK ** -0.5 if mode == 'q' else 1.0
    outs = []
    for h in range(c // 128):
        yh = y[:, h * 128:(h + 1) * 128]
        outs.append(yh * lax.rsqrt(jnp.sum(yh * yh, axis=-1, keepdims=True) + EPS) * scale)
    return (jnp.concatenate(outs, axis=1),)


def _act_fn(info, x, p0, p1):
    lane = lax.broadcasted_iota(jnp.int32, x.shape, 1)
    sp = jax.nn.softplus(x + p1)
    g = -jnp.exp(p0) * sp
    bt = jax.nn.sigmoid(x)
    return (jnp.where(lane < 16, g, jnp.where(lane < 32, bt, jnp.where(lane < 96, sp, 0.0))),)


def _mixg_fn(info, o, zg, gnw):
    outs = []
    for h in range(GDN_HEADS):
        outs.append(_rms(o[:, h * 128:(h + 1) * 128]) * gnw)
    return (jnp.concatenate(outs, axis=1) * _silu(zg),)


def _mixs_fn(info, y, xs, zs, dl, snw):
    yy = (y + dl * xs) * _silu(zs)
    outs = []
    for g in range(4):
        outs.append(_rms(yy[:, g * 512:(g + 1) * 512]))
    return (jnp.concatenate(outs, axis=1) * snw,)


def _merge_fn(info, gates, pg, ps):
    half = gates.shape[1] // 2
    return (jax.nn.sigmoid(gates[:, :half]) * pg + jax.nn.sigmoid(gates[:, half:]) * ps,)


def _norm2_fn(info, xt, mo, g8, w, sc8, sh8):
    h1 = xt + g8[0:1] * mo
    return (h1, _rms(h1) * w * (1.0 + sc8[0:1]) + sh8[0:1])


def _swiglu_fn(info, u2):
    return (_silu(u2[0]) * u2[1],)


_NN = (((1,), (0,)), ((), ()))
_NT = (((1,), (1,)), ((), ()))
_TN = (((0,), (0,)), ((), ()))


def _mmh(a, b):
    return lax.dot_general(a, b, _NN, precision=HI, preferred_element_type=F32)


def _dot1(a, b, dims):
    return lax.dot_general(a.astype(_SCAN_DTYPE), b.astype(_SCAN_DTYPE), dims, preferred_element_type=F32)


def _mm(a, b):
    return _dot1(a, b, _NN)


def _mm_nt(a, b):
    return _dot1(a, b, _NT)


def _mm_tn(a, b):
    return _dot1(a, b, _TN)


def _split2(a):
    hi = a.astype(_SCAN_DTYPE)
    return hi, (a - hi.astype(F32)).astype(_SCAN_DTYPE)


def _dot3(a, b, dims):
    ah, al = _split2(a)
    bh, bl = _split2(b)
    d = lambda u, w: lax.dot_general(u, w, dims, preferred_element_type=F32)
    return d(ah, bh) + (d(ah, bl) + d(al, bh))


def _order_masks(d):
    i = lax.broadcasted_iota(jnp.int32, (CHUNK, CHUNK), 0)
    j = lax.broadcasted_iota(jnp.int32, (CHUNK, CHUNK), 1)
    s = jnp.where(d == 0, 1, -1) * (i - j)
    return (s >= 0).astype(F32), (s > 0).astype(F32)


@jax.custom_vjp
def _unit_tri_inv(mats):
    i = lax.broadcasted_iota(jnp.int32, (CHUNK, CHUNK), 0)
    j = lax.broadcasted_iota(jnp.int32, (CHUNK, CHUNK), 1)
    eye = (i == j).astype(F32)
    ps = [-a for a in mats]
    ts = [eye + p for p in ps]
    for _ in range(5):
        ps = [_dot3(p, p, _NN) for p in ps]
        ts = [t + _dot3(t, p, _NN) for t, p in zip(ts, ps)]
    return tuple(ts)


def _uti_fwd(mats):
    ts = _unit_tri_inv(mats)
    return ts, ts


def _uti_bwd(ts, gs):
    inner = [_dot3(g, t, _NT) for g, t in zip(gs, ts)]
    return (tuple(-_dot3(t, u, _TN) for t, u in zip(ts, inner)),)


_unit_tri_inv.defvjp(_uti_fwd, _uti_bwd)


@jax.custom_vjp
def _unit_tri_inv_given(mats, ts):
    return ts


def _utig_fwd(mats, ts):
    return ts, ts


def _utig_bwd(ts, gs):
    return _uti_bwd(ts, gs)[0], tuple(jnp.zeros_like(t) for t in ts)


_unit_tri_inv_given.defvjp(_utig_fwd, _utig_bwd)


def _lane_col(blk, lane_idx):
    lane = lax.broadcasted_iota(jnp.int32, blk.shape, 1)
    return jnp.sum(jnp.where(lane == lane_idx, blk, 0.0), axis=1, keepdims=True)


def _decay_mat(cum, incl):
    cb = jnp.broadcast_to(cum, (CHUNK, CHUNK))
    return jnp.exp(jnp.minimum(cb - cb.T, 0.0)) * incl


def _gdn_chunks(dirs, streams, states, aux=None, want_aux=False, group=None):
    ns = len(dirs)
    masks = [_order_masks(d) for d in dirs]
    cum = [_mmh(masks[i][0], streams[i][3]) for i in range(ns)]
    tot = [jnp.sum(streams[i][3], axis=0, keepdims=True) for i in range(ns)]
    all_units = [(i, h) for i in range(ns) for h in range(GDN_HEADS)]
    group = group or len(all_units)
    cat = jnp.concatenate
    outs, new_states, ts_all = [], [], []
    for g0 in range(0, len(all_units), group):
        units = all_units[g0:g0 + group]
        us = range(len(units))
        st = states[g0:g0 + group]
        qs = [streams[i][0][:, h * 128:(h + 1) * 128] for i, h in units]
        ks = [streams[i][1][:, h * 128:(h + 1) * 128] for i, h in units]
        vs = [streams[i][2][:, h * 128:(h + 1) * 128] for i, h in units]
        gcum = [_lane_col(cum[i], dirs[i] * GDN_HEADS + h) for i, h in units]
        glast = [_lane_col(tot[i], dirs[i] * GDN_HEADS + h) for i, h in units]
        beta = [_lane_col(streams[i][3], 16 + dirs[i] * GDN_HEADS + h) for i, h in units]
        decay = [_decay_mat(gcum[u], masks[units[u][0]][0]) for u in us]
        egc = [jnp.exp(gcum[u]) for u in us]
        kb = [ks[u] * beta[u] for u in us]
        kq = [_mm_nt(cat([kb[u], qs[u]], axis=0), ks[u]) for u in us]
        mats = tuple(kq[u][:CHUNK] * decay[u] * masks[units[u][0]][1] for u in us)
        ts = _unit_tri_inv(mats) if aux is None else _unit_tri_inv_given(mats, tuple(aux[g0:g0 + group]))
        wu = [_mm(ts[u], cat([kb[u] * egc[u], vs[u] * beta[u]], axis=1)) for u in us]
        ws = [_mm(cat([wu[u][:, :128], qs[u] * egc[u]], axis=0), st[u]) for u in us]
        vn = [wu[u][:, 128:] - ws[u][:CHUNK] for u in us]
        outs += [ws[u][CHUNK:] + _mm(kq[u][CHUNK:] * decay[u], vn[u]) for u in us]
        new_states += [st[u] * jnp.exp(glast[u]) + _mm_tn(ks[u] * jnp.exp(glast[u] - gcum[u]), vn[u]) for u in us]
        ts_all += list(ts)
    per_stream = [cat(outs[i * GDN_HEADS:(i + 1) * GDN_HEADS], axis=1) for i in range(ns)]
    return (*per_stream, *new_states, *(ts_all if want_aux else ()))


def _gdn_step(d, q, k, v, spb, *states, aux=None, want_aux=False):
    return _gdn_chunks([d], [(q, k, v, spb)], states, aux, want_aux, group=4)


def _gdn_step2(q0, k0, v0, sp0, q1, k1, v1, sp1, *states, aux=None, want_aux=False):
    return _gdn_chunks([0, 1], [(q0, k0, v0, sp0), (q1, k1, v1, sp1)], states, aux, want_aux)


def _split3(a):
    a1 = a.astype(_SCAN_DTYPE)
    r = a - a1.astype(F32)
    a2 = r.astype(_SCAN_DTYPE)
    return a1, a2, (r - a2.astype(F32)).astype(_SCAN_DTYPE)


def _exact_dot(a, e, dims, split_lhs, passes=3):
    parts = _split3(a if split_lhs else e)[:passes]
    d = (lambda u: lax.dot_general(u, e, dims, preferred_element_type=F32)) if split_lhs else \
        (lambda u: lax.dot_general(a, u, dims, preferred_element_type=F32))
    return functools.reduce(lambda u, w: u + w, [d(p) for p in reversed(parts)])


@jax.custom_vjp
def _spread(a, e):
    return _exact_dot(a, e, _NN, True)


def _spread_fwd(a, e):
    return _spread(a, e), e


def _spread_bwd(e, g):
    return _exact_dot(g, e, _NT, True, passes=2), jnp.zeros_like(e)


_spread.defvjp(_spread_fwd, _spread_bwd)


@jax.custom_vjp
def _colsum_bcast(z):
    return _exact_dot(jnp.ones((z.shape[0], z.shape[0]), _SCAN_DTYPE), z, _NN, False)


def _colsum_fwd(z):
    return _colsum_bcast(z), None


def _colsum_bwd(_, g):
    return (_exact_dot(jnp.ones((g.shape[0], g.shape[0]), _SCAN_DTYPE), g, _NN, False, passes=2),)


_colsum_bcast.defvjp(_colsum_fwd, _colsum_bwd)


def _ssd_consts():
    wdt = SSM_HEADS * 64
    d = lax.broadcasted_iota(jnp.int32, (2, 128, wdt), 0)
    e = (lax.broadcasted_iota(jnp.int32, (2, 128, wdt), 1)
         == 32 + d * SSM_HEADS + lax.broadcasted_iota(jnp.int32, (2, 128, wdt), 2) // 64).astype(_SCAN_DTYPE)
    dd = lax.broadcasted_iota(jnp.int32, (2, CHUNK, wdt), 0)
    ci = lax.broadcasted_iota(jnp.int32, (2, CHUNK, wdt), 1)
    pos = lax.broadcasted_iota(jnp.int32, (2, CHUNK, wdt), 2) % 64
    incl_t = (jnp.where(dd == 0, 1, -1) * (ci - pos) >= 0).astype(F32)
    diag = (ci == pos).astype(F32)
    return [e, incl_t, diag]


def _ssd_step(d, x, bm, cm, spb, alog, *states, consts):
    e, incl_t, diag = consts
    incl, _ = _order_masks(d)
    lane1 = lax.broadcasted_iota(jnp.int32, (1, 128), 1)
    lo_lane = 32 + d * SSM_HEADS
    a_vec = jnp.where(lane1 >= lo_lane, jnp.where(lane1 < lo_lane + SSM_HEADS, -jnp.exp(alog), 0.0), 0.0)
    adt = spb * a_vec
    acum = _mmh(incl, adt)
    alast = jnp.sum(adt, axis=0, keepdims=True)
    dt2 = _spread(spb, e)
    ac2 = _spread(acum, e)
    al2 = _spread(jnp.broadcast_to(alast, (8, 128)), e)[0:1]
    row = _colsum_bcast(ac2 * diag)
    seg = jnp.exp(jnp.minimum(ac2 - row, 0.0)) * incl_t
    xdt = x * dt2
    gam = jnp.exp(ac2)
    xe = xdt * jnp.exp(al2 - ac2)
    low = lax.broadcasted_iota(jnp.int32, (CHUNK, 128), 1) < 64
    row_low = lax.broadcasted_iota(jnp.int32, (128, 1), 0) < 64
    ps = range(SSM_PAIRS)
    sl = [slice(p * 128, (p + 1) * 128) for p in ps]
    bg = [bm[:, g * 128:(g + 1) * 128] for g in range(4)]
    cg = [cm[:, g * 128:(g + 1) * 128] for g in range(4)]
    cb2 = [_mm_nt(cg[g], jnp.concatenate([bg[g], bg[g]], axis=0)) for g in range(4)]
    ys, new_states = [], []
    for p0 in range(0, SSM_PAIRS, SSD_GROUP):
        pg = range(p0, p0 + SSD_GROUP)
        xd = {p: jnp.concatenate([jnp.where(low, xdt[:, sl[p]], 0.0), jnp.where(low, 0.0, xdt[:, sl[p]])], axis=0)
              for p in pg}
        yd = {p: _mm(cb2[p // 4] * seg[:, sl[p]], xd[p]) for p in pg}
        yo = {p: _mm_nt(cg[p // 4], states[p]) for p in pg}
        ys += [yd[p] + gam[:, sl[p]] * yo[p] for p in pg]
        new = {p: _mm_tn(xe[:, sl[p]], bg[p // 4]) for p in pg}
        al0 = {p: _lane_col(alast, lo_lane + 2 * p) for p in pg}
        al1 = {p: _lane_col(alast, lo_lane + 2 * p + 1) for p in pg}
        new_states += [states[p] * jnp.exp(jnp.where(row_low, al0[p], al1[p])) + new[p] for p in pg]
    return (jnp.concatenate(ys, axis=1), *new_states)


def _chunk_of(d, p, nctx, nc):
    return jnp.where(d == 0, p, jnp.where(p < nctx, nctx - 1 - p, nctx + nc - 1 - p))


class _NoRide:
    n, arrs, in_specs, out_specs, out_shape, scratch = 0, [], [], [], [], []


def _const_specs(consts):
    return [pl.BlockSpec((1,) + a.shape[1:], lambda d, p: (d,) + (0,) * (a.ndim - 1)) for a in consts]


def _scan_fwd(name, step, toks, pars, consts, out_width, n_state, nctx):
    t = toks[0].shape[0]
    nc = t // CHUNK
    n_tok, n_par, n_const = len(toks), len(pars), len(consts)

    def body(*refs):
        it = iter(refs)
        tok_refs = [next(it) for _ in range(n_tok)]
        par_refs = [next(it) for _ in range(n_par)]
        const_refs = [next(it) for _ in range(n_const)]
        o_ref, ss_ref, s_scr = next(it), next(it), next(it)
        d, p = pl.program_id(0), pl.program_id(1)

        @pl.when(p == 0)
        def _():
            s_scr[...] = jnp.zeros(s_scr.shape, F32)

        ss_ref[0, 0] = s_scr[...]
        res = step(d, *[r[...] for r in tok_refs], *[r[...] for r in par_refs], *[s_scr[h] for h in range(n_state)],
                   consts=[r[0] for r in const_refs])
        o_ref[0] = res[0]
        for h in range(n_state):
            s_scr[h] = res[1 + h]

    ch = lambda d, p: _chunk_of(d, p, nctx, nc)
    in_specs = [pl.BlockSpec((CHUNK, a.shape[1]), lambda d, p: (ch(d, p), 0)) for a in toks]
    in_specs += [pl.BlockSpec(a.shape, lambda d, p: (0, 0)) for a in pars]
    return pl.pallas_call(
        body, grid=(2, nc), in_specs=in_specs + _const_specs(consts),
        out_specs=[pl.BlockSpec((1, CHUNK, out_width), lambda d, p: (d, ch(d, p), 0)),
                   pl.BlockSpec((1, 1, n_state, 128, 128), lambda d, p: (d, p, 0, 0, 0))],
        out_shape=[jax.ShapeDtypeStruct((2, t, out_width), F32),
                   jax.ShapeDtypeStruct((2, nc, n_state, 128, 128), F32)],
        scratch_shapes=[pltpu.VMEM((n_state, 128, 128), F32)],
        compiler_params=_cp(("arbitrary", "arbitrary")), name=name)(*toks, *pars, *consts)


def _scan_bwd(name, step, toks, pars, ss, dout, n_state, nctx, ride=None, aux=None, consts=()):
    t = toks[0].shape[0]
    nc = t // CHUNK
    n_tok, n_par, n_const = len(toks), len(pars), len(consts)
    rd = ride if ride is not None else _NoRide
    n_aux = aux.shape[2] if aux is not None else 0

    def body(*refs):
        it = iter(refs)
        tok_refs = [next(it) for _ in range(n_tok)]
        par_refs = [next(it) for _ in range(n_par)]
        ss_ref, do_ref = next(it), next(it)
        aux_ref = next(it) if n_aux else None
        const_refs = [next(it) for _ in range(n_const)]
        ride_in = [next(it) for _ in range(rd.n)]
        dtok_refs = [next(it) for _ in range(n_tok)]
        dpar_refs = [next(it) for _ in range(n_par)]
        ride_out = [next(it) for _ in range(rd.n)]
        ds_scr = next(it)
        sems = list(it)
        d, pr = pl.program_id(0), pl.program_id(1)
        if ride is not None:
            @pl.when((d == 0) & (pr == 0))
            def _():
                ride.start(ride_in, ride_out, sems)

            @pl.when((d == 1) & (pr == nc - 1))
            def _():
                ride.wait(ride_in, ride_out, sems)

        @pl.when(pr == 0)
        def _():
            ds_scr[...] = jnp.zeros(ds_scr.shape, F32)

        kw = dict(aux=[aux_ref[0, 0, i] for i in range(n_aux)]) if n_aux else {}
        if n_const:
            kw['consts'] = [r[0] for r in const_refs]
        _, pull = jax.vjp(functools.partial(step, d, **kw), *[r[...] for r in tok_refs], *[r[...] for r in par_refs],
                          *[ss_ref[0, 0, h] for h in range(n_state)])
        grads = pull((do_ref[...], *[ds_scr[h] for h in range(n_state)]))
        for r, g in zip(dtok_refs, grads[:n_tok]):
            r[0] = g
        for h in range(n_state):
            ds_scr[h] = grads[n_tok + n_par + h]
        first = (d == 0) & (pr == 0)
        for r, g in zip(dpar_refs, grads[n_tok:n_tok + n_par]):
            @pl.when(first)
            def _(r=r, g=g):
                r[...] = g

            @pl.when(jnp.logical_not(first))
            def _(r=r, g=g):
                r[...] += g

    ch = lambda d, pr: _chunk_of(d, nc - 1 - pr, nctx, nc)
    in_specs = [pl.BlockSpec((CHUNK, a.shape[1]), lambda d, pr: (ch(d, pr), 0)) for a in toks]
    in_specs += [pl.BlockSpec(a.shape, lambda d, pr: (0, 0)) for a in pars]
    in_specs += [pl.BlockSpec((1, 1, n_state, 128, 128), lambda d, pr: (d, nc - 1 - pr, 0, 0, 0)),
                 pl.BlockSpec((CHUNK, dout.shape[1]), lambda d, pr: (ch(d, pr), 0))]
    if n_aux:
        in_specs += [pl.BlockSpec((1, 1, n_aux, CHUNK, CHUNK), lambda d, pr: (d, nc - 1 - pr, 0, 0, 0))]
    in_specs += _const_specs(consts)
    out_specs = [pl.BlockSpec((1, CHUNK, a.shape[1]), lambda d, pr: (d, ch(d, pr), 0)) for a in toks]
    out_specs += [pl.BlockSpec(a.shape, lambda d, pr: (0, 0)) for a in pars]
    out_shape = [jax.ShapeDtypeStruct((2, t, a.shape[1]), F32) for a in toks]
    out_shape += [jax.ShapeDtypeStruct(a.shape, F32) for a in pars]
    return pl.pallas_call(
        body, grid=(2, nc), in_specs=in_specs + rd.in_specs, out_specs=out_specs + rd.out_specs,
        out_shape=out_shape + rd.out_shape, scratch_shapes=[pltpu.VMEM((n_state, 128, 128), F32)] + rd.scratch,
        compiler_params=_cp(("arbitrary", "arbitrary")), name=name)(
            *toks, *pars, ss, dout, *([aux] if n_aux else []), *consts, *rd.arrs)


def _scan2_fwd(name, step2, toks, out_width, n_state, nctx, n_aux, ride=None):
    t = toks[0].shape[0]
    nc = t // CHUNK
    n_tok = len(toks)
    rd = ride if ride is not None else _NoRide

    def body(*refs):
        it = iter(refs)
        tok_refs = [next(it) for _ in range(2 * n_tok)]
        ride_in = [next(it) for _ in range(rd.n)]
        o_refs = [next(it), next(it)]
        ss_ref, aux_ref = next(it), next(it)
        ride_out = [next(it) for _ in range(rd.n)]
        s_scr = next(it)
        sems = list(it)
        p = pl.program_id(0)
        if ride is not None:
            @pl.when(p == 0)
            def _():
                ride.start(ride_in, ride_out, sems)

        @pl.when(p == 0)
        def _():
            s_scr[...] = jnp.zeros(s_scr.shape, F32)

        for d in range(2):
            ss_ref[d, 0] = s_scr[d * n_state:(d + 1) * n_state]
        res = step2(*[r[...] for r in tok_refs], *[s_scr[u] for u in range(2 * n_state)], want_aux=True)
        for d in range(2):
            o_refs[d][...] = res[d]
            for i in range(n_aux):
                aux_ref[d, 0, i] = res[2 + 2 * n_state + d * n_aux + i]
        for u in range(2 * n_state):
            s_scr[u] = res[2 + u]
        if ride is not None:
            @pl.when(p == nc - 1)
            def _():
                ride.wait(ride_in, ride_out, sems)

    def tok_spec(a, d):
        return pl.BlockSpec((CHUNK, a.shape[1]), lambda p: (_chunk_of(d, p, nctx, nc), 0))

    return pl.pallas_call(
        body, grid=(nc,), in_specs=[tok_spec(a, d) for d in range(2) for a in toks] + rd.in_specs,
        out_specs=[pl.BlockSpec((CHUNK, out_width), lambda p: (_chunk_of(0, p, nctx, nc), 0)),
                   pl.BlockSpec((CHUNK, out_width), lambda p: (_chunk_of(1, p, nctx, nc), 0)),
                   pl.BlockSpec((2, 1, n_state, 128, 128), lambda p: (0, p, 0, 0, 0)),
                   pl.BlockSpec((2, 1, n_aux, CHUNK, CHUNK), lambda p: (0, p, 0, 0, 0))] + rd.out_specs,
        out_shape=[jax.ShapeDtypeStruct((t, out_width), F32), jax.ShapeDtypeStruct((t, out_width), F32),
                   jax.ShapeDtypeStruct((2, nc, n_state, 128, 128), F32),
                   jax.ShapeDtypeStruct((2, nc, n_aux, CHUNK, CHUNK), F32)] + rd.out_shape,
        scratch_shapes=[pltpu.VMEM((2 * n_state, 128, 128), F32)] + rd.scratch,
        compiler_params=_cp(("arbitrary",)), name=name)(*toks, *toks, *rd.arrs)


ADA_TN = 512


def _ada_part(cc16, w_shard):
    n = w_shard.shape[1]

    def body(c_ref, w_ref, o_ref):
        s = _silu(c_ref[...]).astype(_MXU_DTYPE)
        o_ref[...] = jnp.dot(s, w_ref[...].astype(_MXU_DTYPE), preferred_element_type=F32)

    return pl.pallas_call(
        body, grid=(n // ADA_TN,),
        in_specs=[pl.BlockSpec((16, D_MODEL), lambda j: (0, 0)), pl.BlockSpec((D_MODEL, ADA_TN), lambda j: (0, j))],
        out_specs=pl.BlockSpec((16, ADA_TN), lambda j: (0, j)), out_shape=jax.ShapeDtypeStruct((16, n), F32),
        compiler_params=_cp(("parallel",)), name="ada_part")(cc16, w_shard)


def _ada_bwd_shard(cc16, w_shard, d_lat, d_ctx):
    n = w_shard.shape[1]
    nj = n // ADA_TN

    def body(c_ref, w_ref, gl_ref, gc_ref, dw_ref, dc_ref):
        j = pl.program_id(0)
        s, pull = jax.vjp(_silu, c_ref[...])
        row = lax.broadcasted_iota(jnp.int32, (8, 1), 0)
        dctx = jnp.where(row == 0, jnp.sum(gc_ref[...], axis=0, keepdims=True), 0.0)
        mxu = lambda v: v.astype(_MXU_DTYPE)
        dw_ref[...] = (lax.dot_general(mxu(s[0:8]), mxu(gl_ref[...]), _TN, preferred_element_type=F32)
                       + lax.dot_general(mxu(s[8:16]), mxu(dctx), _TN, preferred_element_type=F32))
        ds = lax.dot_general(mxu(dctx), mxu(w_ref[...]), _NT, preferred_element_type=F32)

        @pl.when(j == 0)
        def _():
            dc_ref[...] = ds

        @pl.when(j > 0)
        def _():
            dc_ref[...] += ds

        @pl.when(j == nj - 1)
        def _():
            ct = jnp.concatenate([jnp.zeros((8, D_MODEL), F32), dc_ref[...]], axis=0)
            dc_ref[...] = 0.5 * pull(ct)[0][8:16]

    tile = pl.BlockSpec((8, ADA_TN), lambda j: (0, j))
    wspec = pl.BlockSpec((D_MODEL, ADA_TN), lambda j: (0, j))
    return pl.pallas_call(
        body, grid=(nj,), in_specs=[pl.BlockSpec((16, D_MODEL), lambda j: (0, 0)), wspec, tile, tile],
        out_specs=[wspec, pl.BlockSpec((8, D_MODEL), lambda j: (0, 0))],
        out_shape=[jax.ShapeDtypeStruct((D_MODEL, n), F32), jax.ShapeDtypeStruct((8, D_MODEL), F32)],
        compiler_params=_cp(("arbitrary",)), name="ada_bwd_shard")(cc16, w_shard, d_lat, d_ctx)


def _rowsum2(a, b):
    def body(a_ref, b_ref, o_ref):
        o_ref[...] = jnp.sum(a_ref[...], axis=0, keepdims=True) + jnp.sum(b_ref[...], axis=0, keepdims=True)

    return pl.pallas_call(body, out_shape=jax.ShapeDtypeStruct((1, a.shape[1]), F32), name="ada_b_grad")(a, b)


def _tail(h1, ff, mods, wf, tgt, nctx_t, tl):
    t = h1.shape[0]
    nt = t // tl

    def loss_fn(valid, h1v, ffv, g8, w, tg):
        h2 = h1v + g8[0:1] * ffv
        y = _rms(h2) * w
        err = (y - tg) ** 2
        return 0.5 * jnp.sum(jnp.mean(err, axis=-1, keepdims=True), axis=0, keepdims=True) * valid

    def body(h1_ref, ff_ref, g_ref, w_ref, t_ref, loss_ref, dh_ref, dff_ref, dg_ref, dw_ref):
        i = pl.program_id(0)
        valid = jnp.where(i < nctx_t, 0.0, 1.0)
        lv, pull = jax.vjp(functools.partial(loss_fn, valid), h1_ref[...], ff_ref[...].astype(F32), g_ref[...],
                           w_ref[...], t_ref[...])
        dh, dff, dg, dw, _ = pull(jnp.ones((1, 1), F32))
        dh_ref[...] = dh
        dff_ref[...] = dff.astype(dff_ref.dtype)
        lb = jnp.broadcast_to(lv, loss_ref.shape)

        @pl.when(i == 0)
        def _():
            loss_ref[...] = lb
            dg_ref[...] = dg
            dw_ref[...] = dw

        @pl.when(i > 0)
        def _():
            loss_ref[...] += lb
            dg_ref[...] += dg
            dw_ref[...] += dw

    tok = pl.BlockSpec((tl, D_MODEL), lambda i: (i, 0))
    return pl.pallas_call(
        body, grid=(nt,),
        in_specs=[tok, tok, pl.BlockSpec((8, D_MODEL), lambda i: (0, 5)), pl.BlockSpec((1, D_MODEL), lambda i: (0, 0)),
                  pl.BlockSpec((tl, D_MODEL), lambda i: (jnp.maximum(i - nctx_t, 0), 0))],
        out_specs=[pl.BlockSpec((8, 128), lambda i: (0, 0)), tok, tok, pl.BlockSpec((8, D_MODEL), lambda i: (0, 0)),
                   pl.BlockSpec((1, D_MODEL), lambda i: (0, 0))],
        out_shape=[jax.ShapeDtypeStruct((8, 128), F32), jax.ShapeDtypeStruct((t, D_MODEL), F32),
                   jax.ShapeDtypeStruct((t, D_MODEL), _MXU_DTYPE), jax.ShapeDtypeStruct((8, D_MODEL), F32),
                   jax.ShapeDtypeStruct((1, D_MODEL), F32)],
        compiler_params=_cp(("arbitrary",)), name="tail_loss")(h1, ff, mods, wf, tgt)


def _pack_w_in(w4):
    ns = w4.shape[2]
    placed = []
    for s0, s1, p0 in IN_SEGMENTS:
        for j in range(N_PLANE):
            lo, hi = max(s0, j * ns), min(s1, (j + 1) * ns)
            if lo < hi:
                placed.append((p0 + lo - s0, w4[j][:, lo - j * ns:hi - j * ns]))
    placed.sort(key=lambda e: e[0])
    pieces, end = [], 0
    for pos, piece in placed:
        assert pos == end, (pos, end)
        pieces.append(piece)
        end = pos + piece.shape[1]
    pieces.append(jnp.zeros((w4.shape[1], P_TOTAL - end), w4.dtype))
    return jnp.concatenate(pieces, axis=1)


def _unpack_w_in(g):
    ns = D_IN_PROJ // N_PLANE
    shards = []
    for j in range(N_PLANE):
        pieces = []
        for s0, s1, p0 in IN_SEGMENTS:
            lo, hi = max(s0, j * ns), min(s1, (j + 1) * ns)
            if lo < hi:
                pieces.append(g[:, p0 + lo - s0:p0 + hi - s0])
        shards.append(jnp.concatenate(pieces, axis=1))
    return jnp.stack(shards)


LATE_WEIGHTS = ['w_br_gdn', 'w_br_ssm', 'w_out', 'w_ffn_in', 'w_ffn_out']
COL_STACKED = ('w_in', 'w_ffn_in')


def _from_shards(n, g):
    if n in COL_STACKED:
        return g
    if SHARD_AXIS[n] == 0:
        return g.reshape(N_PLANE * g.shape[1], g.shape[2])
    return jnp.concatenate([g[j] for j in range(N_PLANE)], axis=1)


def _to_dest_blocks(n, g):
    if g.ndim == 3:
        return g
    if SHARD_AXIS[n] == 0:
        return g.reshape(N_PLANE, g.shape[0] // N_PLANE, g.shape[1])
    sz = g.shape[1] // N_PLANE
    return jnp.stack([g[:, j * sz:(j + 1) * sz] for j in range(N_PLANE)])


def _local_step(x, c, ctx, tgt, W, late_shards):
    lc, l = ctx.shape[0], x.shape[0]
    t = lc + l
    tl = 256
    assert lc == tl and l % tl == 0 and lc % CHUNK == 0
    nt, nctx_t, nctx = t // tl, lc // tl, lc // CHUNK
    act = _MXU_DTYPE
    r1 = lambda v: v.reshape(1, -1)

    xt = jnp.concatenate([ctx, x], axis=0)
    lin = 4 * lax.axis_index("x") + 2 * lax.axis_index("y") + lax.axis_index("c")
    c_all = _gather_all("gather_c", jnp.broadcast_to(c, (8, D_MODEL)))[:, 0]
    cc16 = jnp.concatenate([c_all, r1(W['c_ctx']), jnp.zeros((7, D_MODEL), F32)], axis=0)
    (parts,) = _plane_exchange("gather_mods", [_ada_part(cc16, W['ada_w'])], gather=True)
    mods_all = jnp.transpose(parts, (1, 0, 2)).reshape(16, -1) + r1(W['ada_b'])
    mods = jnp.concatenate([lax.dynamic_slice_in_dim(mods_all, lin, 1, axis=0), mods_all[8:9],
                            jnp.zeros((6, mods_all.shape[1]), F32)], axis=0)

    def mod(kk):
        return Op(mods, (8, D_MODEL), lambda j, i, kk=kk: (0, kk))

    def tokop(arr, w=D_MODEL, off=0, tl_=tl):
        if not isinstance(arr, list) and arr.ndim == 3:
            return Op(arr, (2, tl_, w), lambda j, i: (0, i, off + j), load=_sum_dirs)
        return Op(arr, (tl_, w), lambda j, i: (i, off + j))

    def outop(n, dtype, w=D_MODEL, tl_=tl):
        return Op(((t, n), dtype), (tl_, w), lambda j, i: (i, j))

    def parop(arr, w, off=0):
        return Op(arr, (arr.shape[0], w), lambda j, i: (0, off + j))

    def parout(rows, n, w):
        return Op(((rows, n), F32), (rows, w), lambda j, i: (0, j))

    n1w = r1(W['norm1_w'])
    pre_fn = functools.partial(_prenorm_fn, nctx_t)
    pre_tok, pre_par = [tokop(xt)], [parop(n1w, D_MODEL), mod(1), mod(0)]
    (a,) = _tw_fwd("prenorm_fwd", pre_fn, (1, nt), pre_tok + pre_par, [outop(D_MODEL, act)])
    wp = _pack_w_in(W['w_in'])
    proj = _matmul(a, wp, 'nn', "in_proj")

    gcw, gcb = W['gdn_conv_w'], r1(W['gdn_conv_b'])
    scw, scb = W['ssm_conv_w'], r1(W['ssm_conv_b'])
    conv_parts = {}

    tlf = 768 if t % 768 == 0 else tl

    def conv_part(name, mode, poff, cw, cb, woff, width):
        fn = functools.partial(_conv_fn, lc, mode)
        bw = min(width, 1024)
        par_ = [parop(cw, bw, woff // bw), parop(cb, bw, woff // bw)]
        conv_parts[name] = (fn, [tokop(proj, bw, poff // bw)], par_, width, bw, poff)
        (o,) = _tw_fwd("conv_" + name, fn, (width // bw, t // tlf), [tokop(proj, bw, poff // bw, tlf)] + par_,
                       [outop(width, F32, bw, tlf)])
        return o

    q = conv_part('q', 'q', P_QKV, gcw, gcb, 0, 1024)
    k = conv_part('k', 'k', P_QKV + 1024, gcw, gcb, 1024, 1024)
    v = conv_part('v', 'none', P_QKV + 2048, gcw, gcb, 2048, 1024)
    xs = conv_part('xs', 'none', P_XBC, scw, scb, 0, 2048)
    bm = conv_part('bm', 'none', P_XBC + 2048, scw, scb, 2048, 512)
    cm = conv_part('cm', 'none', P_XBC + 2560, scw, scb, 2560, 512)

    z16, z32 = jnp.zeros((16,), F32), jnp.zeros((32,), F32)
    p0 = jnp.concatenate([W['gdn_a_log'].reshape(-1), jnp.zeros((112,), F32)]).reshape(1, 128)
    p1 = jnp.concatenate([W['gdn_dt_bias'].reshape(-1), z16, W['ssm_dt_bias'].reshape(-1), z32]).reshape(1, 128)
    alog = jnp.concatenate([z32, W['ssm_a_log'].reshape(-1), z32]).reshape(1, 128)
    act_tok, act_par = [tokop(proj, 128, P_SMALL // 128)], [parop(p0, 128), parop(p1, 128)]
    (sp,) = _tw_fwd("small_act", _act_fn, (1, nt), act_tok + act_par, [outop(128, F32, 128)])

    gather_late = PlaneExchange([late_shards[n] for n in LATE_WEIGHTS], gather=True)
    o_f, o_b, ss, tri, *late = _scan2_fwd("gdn_scan_fwd", _gdn_step2, [q, k, v, sp], 1024, GDN_HEADS, nctx,
                                          GDN_HEADS, ride=gather_late)
    W = dict(W, **{n: _from_shards(n, g) for n, g in zip(LATE_WEIGHTS, late)})
    ssd_consts = _ssd_consts()
    y2, hs = _scan_fwd("ssd_scan_fwd", _ssd_step, [xs, bm, cm, sp], [alog], ssd_consts, 2048, SSM_PAIRS, nctx)

    tlm = 128
    ntm = t // tlm
    gnw = r1(W['gdn_norm_w'])
    dl = jnp.repeat(W['ssm_d'], 64).reshape(1, 2048)
    snw = r1(W['ssm_norm_w'])
    mg_tok = [tokop([o_f, o_b], 1024, 0, tlm), tokop(proj, 1024, P_ZG // 1024, tlm)]
    mg_par = [parop(gnw, 128)]
    (og,) = _tw_fwd("mix_gdn", _mixg_fn, (1, ntm), mg_tok + mg_par, [outop(1024, act, 1024, tlm)])
    ms_tok = [tokop(y2, 2048, 0, tlm), tokop(xs, 2048, 0, tlm), tokop(proj, 2048, P_ZS // 2048, tlm)]
    ms_par = [parop(dl, 2048), parop(snw, 2048)]
    (yg,) = _tw_fwd("mix_ssm", _mixs_fn, (1, ntm), ms_tok + ms_par, [outop(2048, act, 2048, tlm)])

    pg = _matmul(og, W['w_br_gdn'], 'nn', "br_gdn", out_dtype=act)
    ps = _matmul(yg, W['w_br_ssm'], 'nn', "br_ssm", out_dtype=act)
    mr_tok = [tokop(proj, 2048, P_GATE // 2048), tokop(pg), tokop(ps)]
    (mrg,) = _tw_fwd("merge", _merge_fn, (1, t // tlf),
                     [tokop(proj, 2048, P_GATE // 2048, tlf), tokop(pg, tl_=tlf), tokop(ps, tl_=tlf)],
                     [outop(1024, act, tl_=tlf)])
    mo = _matmul(mrg, W['w_out'], 'nn', "w_out")

    n2w = r1(W['norm2_w'])
    n2_tok, n2_par = [tokop(xt), tokop(mo)], [mod(2), parop(n2w, D_MODEL), mod(4), mod(3)]
    h1, f = _tw_fwd("norm2_fwd", _norm2_fn, (1, t // tlf), [tokop(xt, tl_=tlf), tokop(mo, tl_=tlf)] + n2_par,
                    [outop(1024, F32, tl_=tlf), outop(1024, act, tl_=tlf)])
    u2 = _matmul(f, W['w_ffn_in'], 'nn', "ffn_in", out_dtype=act, halves='out')
    swb = D_FF // 2
    both = lambda arr, tl_=tl: Op(arr, (2, tl_, swb), lambda j, i: (0, i, j))
    sw_tok = [both(u2)]
    (sw,) = _tw_fwd("swiglu", _swiglu_fn, (D_FF // swb, t // tlf), [both(u2, tlf)], [outop(D_FF, act, swb, tlf)])
    ff = _matmul(sw, W['w_ffn_out'], 'nn', "ffn_out")

    loss8, dh1, dff, dg2, dnf = _tail(h1, ff, mods, r1(W['norm_f_w']), tgt, nctx_t, tl)
    loss = loss8[0, 0]

    G = {}
    G['norm_f_w'] = dnf.reshape(-1)
    G['w_ffn_out'] = _matmul(sw, dff, 'tn', "d_ffn_out", out_dtype=_GRAD_DTYPE)
    dsw = _matmul(dff, W['w_ffn_out'], 'nt', "d_sw", out_dtype=act)
    (du2,) = _tw_bwd("swiglu_bwd", _swiglu_fn, (D_FF // swb, nt), sw_tok, [], [[tokop(dsw, swb)]],
                     [both(((2, t, D_FF), act))], [])
    G['w_ffn_in'] = _matmul(f, du2, 'tn', "d_ffn_in", out_dtype=_GRAD_DTYPE, stacked_out=True, halves='b')
    df = _matmul(du2, W['w_ffn_in'], 'nt', "d_f", halves='a')
    dxt1, dmo, dg1, dn2, dsc2, dsh2 = _tw_bwd(
        "norm2_bwd", _norm2_fn, (1, nt), n2_tok, n2_par, [[tokop(dh1)], [tokop(df)]],
        [outop(1024, F32), outop(1024, act)],
        [parout(8, 1024, 1024), parout(1, 1024, 1024), parout(8, 1024, 1024), parout(8, 1024, 1024)])
    G['norm2_w'] = dn2.reshape(-1)
    G['w_out'] = _matmul(mrg, dmo, 'tn', "d_w_out", out_dtype=_GRAD_DTYPE)
    dmrg = _matmul(dmo, W['w_out'], 'nt', "d_mrg", out_dtype=act)
    def win(off, w, tl_=tl):
        return Op(((t, P_TOTAL), act), (tl_, w), lambda j, i: (i, off // w + j))

    dproj = lax.empty((t, P_TOTAL), act)
    dproj, dpg, dps = _tw_bwd("merge_bwd", _merge_fn, (1, nt), mr_tok, [], [[tokop(dmrg)]],
                              [win(P_GATE, 2048), outop(1024, act), outop(1024, act)], [], into={0: dproj})
    G['w_br_gdn'] = _matmul(og, dpg, 'tn', "d_br_gdn", out_dtype=_GRAD_DTYPE)
    G['w_br_ssm'] = _matmul(yg, dps, 'tn', "d_br_ssm", out_dtype=_GRAD_DTYPE)
    dog = _matmul(dpg, W['w_br_gdn'], 'nt', "d_og")
    dyg = _matmul(dps, W['w_br_ssm'], 'nt', "d_yg")

    do, dproj, dgnw = _tw_bwd("mix_gdn_bwd", _mixg_fn, (1, ntm), mg_tok, mg_par, [[tokop(dog, 1024, 0, tlm)]],
                              [outop(1024, F32, 1024, tlm), win(P_ZG, 1024, tlm)], [parout(1, 128, 128)],
                              into={1: dproj})
    G['gdn_norm_w'] = dgnw.reshape(-1)
    dy, dxs_a, dproj, ddl, dsnw = _tw_bwd(
        "mix_ssm_bwd", _mixs_fn, (1, ntm), ms_tok, ms_par, [[tokop(dyg, 2048, 0, tlm)]],
        [outop(2048, F32, 2048, tlm), outop(2048, F32, 2048, tlm), win(P_ZS, 2048, tlm)],
        [parout(1, 2048, 2048), parout(1, 2048, 2048)], into={2: dproj})
    G['ssm_d'] = ddl.reshape(SSM_HEADS, 64).sum(axis=1)
    G['ssm_norm_w'] = dsnw.reshape(-1)

    dq2, dk2, dv2, dsp_g = _scan_bwd("gdn_scan_bwd", _gdn_step, [q, k, v, sp], [], ss, do, GDN_HEADS, nctx, aux=tri)
    received = {}
    scatter_late = PlaneExchange([_to_dest_blocks(n, G.pop(n)) for n in LATE_WEIGHTS], gather=False)
    dxs2, dbm2, dcm2, dsp_s, dalog, *got = _scan_bwd("ssd_scan_bwd", _ssd_step, [xs, bm, cm, sp], [alog], hs, dy,
                                                     SSM_PAIRS, nctx, ride=scatter_late, consts=ssd_consts)
    received.update(zip(LATE_WEIGHTS, got))
    G['ssm_a_log'] = dalog[0, 32:96].reshape(2, SSM_HEADS)

    dconv_w, dconv_b = {}, {}

    def conv_bwd(name, cots_, buf):
        fn, tok_, par_, width, bw, poff = conv_parts[name]
        buf, dconv_w[name], dconv_b[name] = _tw_bwd(
            "conv_" + name + "_bwd", fn, (width // bw, nt), tok_, par_, [[tokop(c_, bw) for c_ in cots_]],
            [win(poff, bw)], [parout(3, width, bw), parout(1, width, bw)], into={0: buf})
        return buf

    dproj = conv_bwd('q', [dq2], dproj)
    dproj = conv_bwd('k', [dk2], dproj)
    dproj = conv_bwd('v', [dv2], dproj)
    dproj = conv_bwd('xs', [dxs2, dxs_a], dproj)
    dproj = conv_bwd('bm', [dbm2], dproj)
    dproj = conv_bwd('cm', [dcm2], dproj)
    G['gdn_conv_w'] = jnp.concatenate([dconv_w['q'], dconv_w['k'], dconv_w['v']], axis=1)
    G['gdn_conv_b'] = [dconv_b['q'], dconv_b['k'], dconv_b['v']]
    G['ssm_conv_w'] = jnp.concatenate([dconv_w['xs'], dconv_w['bm'], dconv_w['cm']], axis=1)
    G['ssm_conv_b'] = [dconv_b['xs'], dconv_b['bm'], dconv_b['cm']]

    dproj, dp0, dp1 = _tw_bwd("small_act_bwd", _act_fn, (1, nt), act_tok, act_par,
                              [[tokop(dsp_g, 128), tokop(dsp_s, 128)]], [win(P_SMALL, P_TOTAL - P_SMALL)],
                              [parout(1, 128, 128), parout(1, 128, 128)], into={0: dproj})
    G['gdn_a_log'] = dp0[0, 0:16].reshape(2, GDN_HEADS)
    G['gdn_dt_bias'] = dp1[0, 0:16].reshape(2, GDN_HEADS)
    G['ssm_dt_bias'] = dp1[0, 32:96].reshape(2, SSM_HEADS)

    g_w_in = _unpack_w_in(_matmul(a, dproj, 'tn', "d_w_in", out_dtype=_GRAD_DTYPE))
    da, (received['w_in'],) = _matmul(dproj, wp, 'nt', "d_a", ride=PlaneExchange([g_w_in], gather=False))

    gx_out = Op(((l, D_MODEL), F32), (tl, D_MODEL), lambda j, i: (jnp.maximum(i - nctx_t, 0), 0))
    grad_x, dn1, dsc1, dsh1 = _tw_bwd(
        "prenorm_bwd", pre_fn, (1, nt), pre_tok, pre_par, [[tokop(da)]], [gx_out],
        [parout(1, 1024, 1024), parout(8, 1024, 1024), parout(8, 1024, 1024)], tok_add=tokop(dxt1),
        sem=("arbitrary", "arbitrary"))
    G['norm1_w'] = dn1.reshape(-1)
    dmods = jnp.concatenate([dsh1, dsc1, dg1, dsh2, dsc2, dg2], axis=1)
    dm_all = _gather_all("gather_dmods", dmods)
    ns = W['ada_w'].shape[1]
    mine = lax.dynamic_slice_in_dim(dm_all, (2 * lax.axis_index("x") + lax.axis_index("y")) * ns, ns, axis=2)
    G['ada_w'], dcc = _ada_bwd_shard(cc16, W['ada_w'], mine[:, 0], mine[:, 1])
    G['ada_b'] = jnp.where(lin == 0, _rowsum2(dm_all[:, 0], dm_all[:, 1]).reshape(-1), 0.0)
    G['c_ctx'] = dcc[0]
    return loss, grad_x, G, received


def _row_tile(r, c):
    for tr in (512, 256, 128, 64, 32, 16, 8):
        if r % tr == 0 and tr * c * 4 <= (1 << 20):
            return tr
    return r


def _sum4(name, rv):
    _, r, c = rv.shape
    tr = _row_tile(r, c)

    def body(r_ref, o_ref):
        o_ref[...] = ((r_ref[0].astype(F32) + r_ref[1].astype(F32)) + r_ref[2].astype(F32)) + r_ref[3].astype(F32)

    return pl.pallas_call(
        body, grid=(r // tr,), in_specs=[pl.BlockSpec((N_PLANE, tr, c), lambda i: (0, i, 0))],
        out_specs=pl.BlockSpec((tr, c), lambda i: (i, 0)), out_shape=jax.ShapeDtypeStruct((r, c), F32),
        compiler_params=_cp(("parallel",)), name=name)(rv)


def _adamw(name, w, m, v, p, q=None):
    r, c = w.shape[-2:]
    tr = _row_tile(r, c)
    grads = [p] if q is None else [p, q]
    lead = (0,) * (w.ndim - 2)

    def body(w_ref, m_ref, v_ref, *refs):
        g_ref, d_ref, mo_ref, vo_ref = refs[len(grads):]
        g = refs[0][...] if q is None else refs[0][...] + refs[1][...]
        at = lead if lead else Ellipsis
        mn = ADAM_B1 * m_ref[at] + (1.0 - ADAM_B1) * g
        vn = ADAM_B2 * v_ref[at] + (1.0 - ADAM_B2) * jnp.square(g)
        m_hat = mn / (1.0 - ADAM_B1 ** ADAM_STEP)
        v_hat = vn / (1.0 - ADAM_B2 ** ADAM_STEP)
        g_ref[at] = g
        d_ref[at] = -ADAM_LR * (m_hat / (jnp.sqrt(v_hat) + ADAM_EPS) + ADAM_WD * w_ref[at])
        mo_ref[at] = mn
        vo_ref[at] = vn

    gspec = pl.BlockSpec((tr, c), lambda i: (i, 0))
    wspec = pl.BlockSpec((1,) * len(lead) + (tr, c), lambda i: lead + (i, 0))
    return pl.pallas_call(
        body, grid=(r // tr,), in_specs=[wspec] * 3 + [gspec] * len(grads), out_specs=[wspec] * 4,
        out_shape=[jax.ShapeDtypeStruct(w.shape, F32)] * 4, compiler_params=_cp(("parallel",)), name=name)(w, m, v, *grads)


SMALL_ROWS = 24


def _pack_small(d):
    pieces = [p.reshape(-1) for n in SMALL for p in (d[n] if isinstance(d[n], list) else [d[n]])]
    v = jnp.concatenate(pieces)
    return jnp.pad(v, (0, SMALL_ROWS * 1024 - v.shape[0])).reshape(SMALL_ROWS, 1024)


def _unpack_small(buf, like):
    v = buf.reshape(-1)
    out, off = {}, 0
    for n in SMALL:
        sz = like[n].size
        out[n] = v[off:off + sz].reshape(like[n].shape)
        off += sz
    return out


def kernel(x, c, ctx, c_ctx, ada_w, ada_b, norm1_w, w_in, gdn_conv_w, gdn_conv_b, gdn_a_log, gdn_dt_bias, gdn_norm_w, ssm_conv_w, ssm_conv_b, ssm_a_log, ssm_dt_bias, ssm_d, ssm_norm_w, w_br_gdn, w_br_ssm, w_out, norm2_w, w_ffn_in, w_ffn_out, norm_f_w, loss_target, m_c_ctx, m_ada_w, m_ada_b, m_norm1_w, m_w_in, m_gdn_conv_w, m_gdn_conv_b, m_gdn_a_log, m_gdn_dt_bias, m_gdn_norm_w, m_ssm_conv_w, m_ssm_conv_b, m_ssm_a_log, m_ssm_dt_bias, m_ssm_d, m_ssm_norm_w, m_w_br_gdn, m_w_br_ssm, m_w_out, m_norm2_w, m_w_ffn_in, m_w_ffn_out, m_norm_f_w, v_c_ctx, v_ada_w, v_ada_b, v_norm1_w, v_w_in, v_gdn_conv_w, v_gdn_conv_b, v_gdn_a_log, v_gdn_dt_bias, v_gdn_norm_w, v_ssm_conv_w, v_ssm_conv_b, v_ssm_a_log, v_ssm_dt_bias, v_ssm_d, v_ssm_norm_w, v_w_br_gdn, v_w_br_ssm, v_w_out, v_norm2_w, v_w_ffn_in, v_w_ffn_out, v_norm_f_w):
    args = dict(locals())
    wl = {n: args[n] for n in WEIGHTS}
    ml = {n: args['m_' + n] for n in WEIGHTS}
    vl = {n: args['v_' + n] for n in WEIGHTS}

    def nodepth(n, a):
        return a if n in ('c_ctx', 'norm_f_w') else a[0]

    shard = {n: nodepth(n, wl[n]).astype(_MXU_DTYPE if n in MXU_WEIGHTS else F32) for n in SHARDED}
    first = [n for n in SHARDED if n not in LATE_WEIGHTS and n != 'ada_w']
    W = {n: nodepth(n, wl[n]) for n in SMALL}
    W['ada_w'] = shard['ada_w']
    for n, g in zip(first, _gather_two_level("all_gather_plane", [shard[n] for n in first])):
        W[n] = _from_shards(n, g)

    loss_local, grad_x, G, received = _local_step(x[0], c, ctx[0], loss_target[0], W,
                                                  late_shards={n: shard[n] for n in LATE_WEIGHTS})
    loss = lax.psum(loss_local, ("x", "y", "c"))

    small_g = _pack_small(G)
    last = [n for n in SHARDED if n not in received and n != 'ada_w']
    send = [_to_dest_blocks(n, G[n]) for n in last] + [jnp.broadcast_to(small_g[None], (N_PLANE,) + small_g.shape)]
    received.update(zip(last + ['small'], _plane_exchange("scatter_plane", send, gather=False)))
    names = [n for n in SHARDED if n != 'ada_w'] + ['small']
    plane_sum = [_sum4("sum4_" + n, received[n]) for n in names]
    other = _swap_sibling(plane_sum)

    wd = {n: nodepth(n, wl[n]) for n in WEIGHTS}
    md = {n: nodepth(n, ml[n]) for n in WEIGHTS}
    vd = {n: nodepth(n, vl[n]) for n in WEIGHTS}
    res = {'ada_w': _adamw("adamw_ada_w", wl['ada_w'], ml['ada_w'], vl['ada_w'], G['ada_w'])}
    for n, p, q in zip(names, plane_sum, other):
        if n == 'small':
            outs = _adamw("adamw_small", _pack_small(wd), _pack_small(md), _pack_small(vd), p, q)
            unpacked = [_unpack_small(o, wd) for o in outs]
            for sn in SMALL:
                res[sn] = [u[sn].reshape(wl[sn].shape) for u in unpacked]
        else:
            res[n] = _adamw("adamw_" + n, wl[n], ml[n], vl[n], p, q)
    flat = [res[n][kind] for kind in range(4) for n in WEIGHTS]
    return (loss, grad_x[None], *flat)
```

```python
import functools

import jax
import jax.numpy as jnp
from jax import lax
from jax.experimental import pallas as pl
from jax.experimental.pallas import tpu as pltpu

F32 = jnp.float32
HI = lax.Precision.HIGHEST
_MXU_DTYPE = jnp.bfloat16
_SCAN_DTYPE = jnp.bfloat16
_GRAD_DTYPE = jnp.bfloat16

D_MODEL = 1024
EPS = 1e-6
CHUNK = 64
GRID_W = 64
GDN_HEADS = 8
GDN_DK = 128
SSM_HEADS = 32
SSM_PAIRS = 16
SSD_GROUP = 1
D_FF = 2816
D_IN_PROJ = 11360
N_PLANE = 4
P_QKV, P_ZG, P_GATE, P_ZS, P_XBC, P_SMALL, P_TOTAL = 0, 3072, 4096, 6144, 8192, 11264, 11776
IN_SEGMENTS = [(0, 3072, P_QKV), (3072, 4096, P_ZG), (4096, 4112, P_SMALL), (4112, 4128, P_SMALL + 16),
               (4128, 6176, P_ZS), (6176, 9248, P_XBC), (9248, 9312, P_SMALL + 32), (9312, 11360, P_GATE)]
ADAM_LR, ADAM_B1, ADAM_B2, ADAM_EPS, ADAM_WD, ADAM_STEP = 0.001, 0.9, 0.999, 1e-08, 0.01, 10

VMEM_LIMIT = 48 * 1024 * 1024

WEIGHTS = ['c_ctx', 'ada_w', 'ada_b', 'norm1_w', 'w_in', 'gdn_conv_w', 'gdn_conv_b', 'gdn_a_log', 'gdn_dt_bias',
           'gdn_norm_w', 'ssm_conv_w', 'ssm_conv_b', 'ssm_a_log', 'ssm_dt_bias', 'ssm_d', 'ssm_norm_w', 'w_br_gdn',
           'w_br_ssm', 'w_out', 'norm2_w', 'w_ffn_in', 'w_ffn_out', 'norm_f_w']
SHARD_AXIS = {'ada_w': 1, 'w_in': 1, 'gdn_conv_w': 1, 'ssm_conv_w': 1, 'w_br_gdn': 0, 'w_br_ssm': 0, 'w_out': 0,
              'w_ffn_in': 1, 'w_ffn_out': 0}
SHARDED = [n for n in WEIGHTS if n in SHARD_AXIS]
SMALL = [n for n in WEIGHTS if n not in SHARD_AXIS]
MXU_WEIGHTS = ['ada_w', 'w_in', 'w_br_gdn', 'w_br_ssm', 'w_out', 'w_ffn_in', 'w_ffn_out']


def _cp(sem):
    return pltpu.CompilerParams(dimension_semantics=sem, vmem_limit_bytes=VMEM_LIMIT)


def _pick(n, cands):
    for c in cands:
        if n % c == 0:
            return c
    raise ValueError(f"no tile for {n}")


HBM = pl.BlockSpec(memory_space=pl.ANY)
MESH = pl.DeviceIdType.MESH


def _plane_peers():
    x, y, c = lax.axis_index("x"), lax.axis_index("y"), lax.axis_index("c")
    return (x, y, c), [(1 - x, y, c), (x, 1 - y, c), (1 - x, 1 - y, c)]


class PlaneExchange:
    def __init__(self, arrs, gather):
        self.arrs, self.gather, self.n = list(arrs), gather, len(arrs)
        self.in_specs = [HBM] * self.n
        self.out_specs = [HBM] * self.n
        self.out_shape = [jax.ShapeDtypeStruct((N_PLANE,) + a.shape if gather else a.shape, a.dtype) for a in self.arrs]
        self.scratch = [pltpu.SemaphoreType.DMA((3 * self.n,)), pltpu.SemaphoreType.DMA((3 * self.n,)),
                        pltpu.SemaphoreType.DMA((self.n,))]

    def _copies(self, ins, outs, sems):
        send_sems, recv_sems, local_sems = sems
        (x, y, c), peers = _plane_peers()
        me = 2 * x + y
        copies = []
        for ti in range(self.n):
            src = ins[ti] if self.gather else ins[ti].at[me]
            copies.append(pltpu.make_async_copy(src, outs[ti].at[me], local_sems.at[ti]))
            for kk, (px, py, pc) in enumerate(peers):
                src = ins[ti] if self.gather else ins[ti].at[2 * px + py]
                copies.append(pltpu.make_async_remote_copy(
                    src_ref=src, dst_ref=outs[ti].at[me], send_sem=send_sems.at[3 * ti + kk],
                    recv_sem=recv_sems.at[3 * ti + kk], device_id=(px, py, pc), device_id_type=MESH))
        return copies

    def start(self, ins, outs, sems):
        for cp in self._copies(ins, outs, sems):
            cp.start()

    def wait(self, ins, outs, sems):
        for cp in self._copies(ins, outs, sems):
            cp.wait()


def _plane_exchange(name, arrs, gather):
    ex = PlaneExchange(arrs, gather)
    n = ex.n

    def body(*refs):
        ins, outs, sems = refs[:n], refs[n:2 * n], refs[2 * n:]
        ex.start(ins, outs, sems)
        ex.wait(ins, outs, sems)

    return pl.pallas_call(body, in_specs=ex.in_specs, out_specs=ex.out_specs, out_shape=ex.out_shape,
                          scratch_shapes=ex.scratch, name=name)(*arrs)


def _gather_two_level(name, arrs):
    n = len(arrs)
    split = [a.shape[0] % 32 == 0 for a in arrs]

    def body(*refs):
        ins, outs = refs[:n], refs[n:2 * n]
        ici_send, ici_recv, d2d_send, d2d_recv, local_sems = refs[2 * n:]
        (x, y, c), peers = _plane_peers()
        me = 2 * x + y

        def part(ref, ti):
            if not split[ti]:
                return ref
            rows = arrs[ti].shape[0] // 2
            return ref.at[pl.ds(c * rows, rows)]

        local = [pltpu.make_async_copy(ins[ti], outs[ti].at[me], local_sems.at[ti]) for ti in range(n)]
        ici = [pltpu.make_async_remote_copy(
            src_ref=part(ins[ti], ti), dst_ref=part(outs[ti].at[me], ti), send_sem=ici_send.at[3 * ti + kk],
            recv_sem=ici_recv.at[3 * ti + kk], device_id=peer, device_id_type=MESH)
            for ti in range(n) for kk, peer in enumerate(peers)]
        for cp in local + ici:
            cp.start()
        d2d = []
        for ti in range(n):
            for kk, (px, py, pc) in enumerate(peers):
                ici[3 * ti + kk].wait_recv()
                if split[ti]:
                    piece = part(outs[ti].at[2 * px + py], ti)
                    cp = pltpu.make_async_remote_copy(
                        src_ref=piece, dst_ref=piece, send_sem=d2d_send.at[3 * ti + kk],
                        recv_sem=d2d_recv.at[3 * ti + kk], device_id=(x, y, 1 - c), device_id_type=MESH)
                    cp.start()
                    d2d.append(cp)
        for cp in ici:
            cp.wait_send()
        for cp in d2d:
            cp.wait()
        for cp in local:
            cp.wait()

    return pl.pallas_call(
        body, in_specs=[HBM] * n, out_specs=[HBM] * n,
        out_shape=[jax.ShapeDtypeStruct((N_PLANE,) + a.shape, a.dtype) for a in arrs],
        scratch_shapes=[pltpu.SemaphoreType.DMA((3 * n,))] * 4 + [pltpu.SemaphoreType.DMA((n,))], name=name)(*arrs)


def _gather_all(name, arr):
    flips = [(fx, fy, fc) for fx in (0, 1) for fy in (0, 1) for fc in (0, 1)][1:]

    def body(a_ref, o_ref, send_sems, recv_sems, local_sem):
        x, y, c = lax.axis_index("x"), lax.axis_index("y"), lax.axis_index("c")
        me = 4 * x + 2 * y + c
        copies = [pltpu.make_async_copy(a_ref, o_ref.at[me], local_sem)]
        for kk, (fx, fy, fc) in enumerate(flips):
            peer = (1 - x if fx else x, 1 - y if fy else y, 1 - c if fc else c)
            copies.append(pltpu.make_async_remote_copy(
                src_ref=a_ref, dst_ref=o_ref.at[me], send_sem=send_sems.at[kk], recv_sem=recv_sems.at[kk],
                device_id=peer, device_id_type=MESH))
        for cp in copies:
            cp.start()
        for cp in copies:
            cp.wait()

    return pl.pallas_call(
        body, in_specs=[HBM], out_specs=HBM, out_shape=jax.ShapeDtypeStruct((8,) + arr.shape, arr.dtype),
        scratch_shapes=[pltpu.SemaphoreType.DMA((7,)), pltpu.SemaphoreType.DMA((7,)), pltpu.SemaphoreType.DMA],
        name=name)(arr)


def _swap_sibling(arrs):
    n = len(arrs)

    def body(*refs):
        ins, outs, send_sems, recv_sems = refs[:n], refs[n:2 * n], refs[2 * n], refs[2 * n + 1]
        x, y, c = lax.axis_index("x"), lax.axis_index("y"), lax.axis_index("c")
        copies = [pltpu.make_async_remote_copy(src_ref=ins[ti], dst_ref=outs[ti], send_sem=send_sems.at[ti],
                                               recv_sem=recv_sems.at[ti], device_id=(x, y, 1 - c), device_id_type=MESH)
                  for ti in range(n)]
        for cp in copies:
            cp.start()
        for cp in copies:
            cp.wait()

    return pl.pallas_call(
        body, in_specs=[HBM] * n, out_specs=[HBM] * n, out_shape=[jax.ShapeDtypeStruct(a.shape, a.dtype) for a in arrs],
        scratch_shapes=[pltpu.SemaphoreType.DMA((n,)), pltpu.SemaphoreType.DMA((n,))], name="swap_sibling")(*arrs)


MATMUL_VMEM_BUDGET = 36 * 1024 * 1024
TILE_CANDIDATES = (2944, 2816, 1408, 1024, 768, 512, 256, 128)
STEP_COST_BYTES = 1 << 20


def _matmul_tiles(m, n, k, ab, bb, ob, tn_fixed=None, tk_fixed=None):
    def cands(dim, whole_up_to):
        out = [c for c in TILE_CANDIDATES if dim % c == 0]
        if dim <= whole_up_to and dim not in out:
            out.append(dim)
        return out

    best = None
    for tm in cands(m, 3072):
        for tn in ([tn_fixed] if tn_fixed else cands(n, 3072)):
            for tk in ([tk_fixed] if tk_fixed else cands(k, 2048)):
                gi, gj, gk = m // tm, n // tn, k // tk
                vmem = 2 * (tm * tk * ab + tk * tn * bb + tm * tn * ob) + (tm * tn * 4 if gk > 1 else 0)
                if vmem > MATMUL_VMEM_BUDGET:
                    continue
                a_reads = 1 if gk == 1 else gj
                b_reads = 1 if (gk == 1 and gj == 1) else gi
                cost = (m * k * ab * a_reads + k * n * bb * b_reads + m * n * ob + gi * gj * gk * STEP_COST_BYTES)
                if best is None or cost < best[0]:
                    best = (cost, tm, tn, tk)
    assert best is not None, (m, n, k)
    return best[1:]


def _matmul(a, b, form, name, out_dtype=F32, stacked_out=False, ride=None, halves=None):
    dims = {'nn': (((1,), (0,)), ((), ())), 'nt': (((1,), (1,)), ((), ())), 'tn': (((0,), (0,)), ((), ()))}[form]
    stacked_b = b.ndim == 3 and halves != 'b'
    ns = b.shape[2] if stacked_b else None
    if form == 'nn':
        m, k = a.shape
        n = b.shape[0] * ns if stacked_b else b.shape[1]
    elif form == 'nt':
        m, k = (a.shape[1], 2 * a.shape[2]) if halves == 'a' else a.shape
        n = b.shape[1] if stacked_b else b.shape[0]
    else:
        k, m = a.shape
        n = 2 * b.shape[2] if halves == 'b' else b.shape[1]
    ob = jnp.dtype(out_dtype).itemsize
    tm, tn, tk = _matmul_tiles(m, n, k, a.dtype.itemsize, b.dtype.itemsize, ob,
                               tn_fixed=(n // N_PLANE if stacked_out else ns if (stacked_b and form == 'nn') else None),
                               tk_fixed=(ns if (stacked_b and form == 'nt') else None))
    nk = k // tk
    grid = (m // tm, n // tn, nk)
    n_ride = ride.n if ride is not None else 0

    def body(*refs):
        a_ref, b_ref = refs[0], refs[1]
        ride_in = refs[2:2 + n_ride]
        o_ref = refs[2 + n_ride]
        ride_out = refs[3 + n_ride:3 + 2 * n_ride]
        acc_ref = refs[3 + 2 * n_ride]
        sems = refs[4 + 2 * n_ride:]
        i, j, kk = pl.program_id(0), pl.program_id(1), pl.program_id(2)
        if ride is not None:
            @pl.when((i == 0) & (j == 0) & (kk == 0))
            def _():
                ride.start(ride_in, ride_out, sems)

        def put(val):
            if stacked_out or halves == 'out':
                o_ref[0] = val.astype(o_ref.dtype)
            else:
                o_ref[...] = val.astype(o_ref.dtype)

        av = a_ref[0] if halves == 'a' else a_ref[...]
        bv = b_ref[0] if (stacked_b or halves == 'b') else b_ref[...]
        part = lax.dot_general(av.astype(_MXU_DTYPE), bv.astype(_MXU_DTYPE), dims, preferred_element_type=F32)
        if nk == 1:
            put(part)
        else:
            @pl.when(kk == 0)
            def _():
                acc_ref[...] = part

            @pl.when(kk > 0)
            def _():
                acc_ref[...] += part

            @pl.when(kk == nk - 1)
            def _():
                put(acc_ref[...])

        if ride is not None:
            @pl.when((i == grid[0] - 1) & (j == grid[1] - 1) & (kk == nk - 1))
            def _():
                ride.wait(ride_in, ride_out, sems)

    if form == 'nn':
        a_spec = pl.BlockSpec((tm, tk), lambda i, j, kk: (i, kk))
        b_spec = (pl.BlockSpec((1, tk, tn), lambda i, j, kk: (j, kk, 0)) if stacked_b
                  else pl.BlockSpec((tk, tn), lambda i, j, kk: (kk, j)))
    elif form == 'nt':
        a_spec = pl.BlockSpec((tm, tk), lambda i, j, kk: (i, kk))
        b_spec = (pl.BlockSpec((1, tn, tk), lambda i, j, kk: (kk, j, 0)) if stacked_b
                  else pl.BlockSpec((tn, tk), lambda i, j, kk: (j, kk)))
    else:
        a_spec = pl.BlockSpec((tk, tm), lambda i, j, kk: (kk, i))
        b_spec = pl.BlockSpec((tk, tn), lambda i, j, kk: (kk, j))
    hj, hk = grid[1] // 2, nk // 2
    if halves == 'a':
        a_spec = pl.BlockSpec((1, tm, tk), lambda i, j, kk: (kk // hk, i, kk % hk))
    if halves == 'b':
        b_spec = pl.BlockSpec((1, tk, tn), lambda i, j, kk: (j // hj, kk, j % hj))
    if stacked_out:
        o_spec = pl.BlockSpec((1, tm, tn), lambda i, j, kk: (j, i, 0))
        o_shape = jax.ShapeDtypeStruct((N_PLANE, m, tn), out_dtype)
    elif halves == 'out':
        o_spec = pl.BlockSpec((1, tm, tn), lambda i, j, kk: (j // hj, i, j % hj))
        o_shape = jax.ShapeDtypeStruct((2, m, n // 2), out_dtype)
    else:
        o_spec = pl.BlockSpec((tm, tn), lambda i, j, kk: (i, j))
        o_shape = jax.ShapeDtypeStruct((m, n), out_dtype)
    acc_shape = (tm, tn) if nk > 1 else (8, 128)
    if ride is None:
        return pl.pallas_call(
            body, grid=grid, in_specs=[a_spec, b_spec], out_specs=o_spec, out_shape=o_shape,
            scratch_shapes=[pltpu.VMEM(acc_shape, F32)],
            compiler_params=_cp(("parallel", "parallel", "arbitrary")), name=name)(a, b)
    res = pl.pallas_call(
        body, grid=grid, in_specs=[a_spec, b_spec] + ride.in_specs, out_specs=[o_spec] + ride.out_specs,
        out_shape=[o_shape] + ride.out_shape, scratch_shapes=[pltpu.VMEM(acc_shape, F32)] + ride.scratch,
        compiler_params=_cp(("arbitrary", "arbitrary", "arbitrary")), name=name)(a, b, *ride.arrs)
    return res[0], res[1:]


class Op:
    def __init__(self, arr, bs, im, load=None):
        self.arr, self.bs, self.im = arr, bs, im
        self.arrs = list(arr) if isinstance(arr, list) else [arr]
        self.load = load or (lambda r: r[...].astype(F32))

    def spec(self):
        return pl.BlockSpec(self.bs, self.im)

    def value(self, it):
        return functools.reduce(lambda u, w: u + w, [self.load(next(it)) for _ in self.arrs])


def _op_specs(ops):
    return [op.spec() for op in ops for _ in op.arrs]


def _op_arrays(ops):
    return [a for op in ops for a in op.arrs]


def _sum_dirs(r):
    return r[0].astype(F32) + r[1].astype(F32)


def _tw_fwd(name, fn, grid, ins, outs):
    def body(*refs):
        info = (pl.program_id(0), pl.program_id(1))
        it = iter(refs)
        vals = [op.value(it) for op in ins]
        res = fn(info, *vals)
        for r, v in zip(it, res):
            r[...] = v.astype(r.dtype)

    return pl.pallas_call(
        body, grid=grid, in_specs=_op_specs(ins), out_specs=[op.spec() for op in outs],
        out_shape=[jax.ShapeDtypeStruct(*op.arr) for op in outs],
        compiler_params=_cp(("parallel", "arbitrary")), name=name)(*_op_arrays(ins))


def _tw_bwd(name, fn, grid, tok, par, cots, tok_out, par_out, tok_add=None, sem=("parallel", "arbitrary"), into=None):
    n_tok = len(tok)
    flat_cots = [op for group in cots for op in group]
    extra = [tok_add] if tok_add is not None else []
    out_ops = list(tok_out) + list(par_out)
    into = into or {}

    def body(*refs):
        info = (pl.program_id(0), pl.program_id(1))
        it = iter(refs)
        tok_v = [op.value(it) for op in tok]
        par_v = [op.value(it) for op in par]
        cot_v = [functools.reduce(lambda u, w: u + w, [op.value(it) for op in group]) for group in cots]
        add_v = [op.value(it) for op in extra]
        for _ in into:
            next(it)
        _, pull = jax.vjp(lambda *a: fn(info, *a), *tok_v, *par_v)
        grads = pull(tuple(cot_v))
        for i in range(n_tok):
            r = next(it)
            g = grads[i] + add_v[0] if (i == 0 and add_v) else grads[i]
            if r.shape[-1] > g.shape[-1]:
                g = jnp.concatenate([g, jnp.zeros((g.shape[0], r.shape[-1] - g.shape[-1]), g.dtype)], axis=1)
            r[...] = g.astype(r.dtype)
        first = pl.program_id(1) == 0
        for i in range(len(par)):
            r = next(it)
            g = grads[n_tok + i]

            @pl.when(first)
            def _(r=r, g=g):
                r[...] = g

            @pl.when(jnp.logical_not(first))
            def _(r=r, g=g):
                r[...] += g

    ops = tok + par + flat_cots + extra
    n_in = len(_op_arrays(ops))
    return pl.pallas_call(
        body, grid=grid, in_specs=_op_specs(ops) + [HBM] * len(into), out_specs=[op.spec() for op in out_ops],
        out_shape=[jax.ShapeDtypeStruct(*op.arr) for op in out_ops],
        input_output_aliases={n_in + k: i for k, i in enumerate(into)},
        compiler_params=_cp(sem), name=name)(*_op_arrays(ops), *into.values())


def _silu(x):
    return x * jax.nn.sigmoid(x)


def _rms(x):
    return x * lax.rsqrt(jnp.mean(x * x, axis=-1, keepdims=True) + EPS)


@functools.partial(jax.custom_vjp, nondiff_argnums=(1,))
def _shift_rows(x, k):
    return pltpu.roll(x, k % x.shape[0], 0)


def _shift_rows_fwd(x, k):
    return _shift_rows(x, k), None


def _shift_rows_bwd(k, _, g):
    return (_shift_rows(g, -k),)


_shift_rows.defvjp(_shift_rows_fwd, _shift_rows_bwd)


def _prenorm_fn(nctx_t, info, x, w, sc8, sh8):
    is_ctx = info[1] < nctx_t
    sc = jnp.where(is_ctx, sc8[1:2], sc8[0:1])
    sh = jnp.where(is_ctx, sh8[1:2], sh8[0:1])
    return (_rms(x) * w * (1.0 + sc) + sh,)


def _conv_fn(lc, mode, info, x, w, b):
    n, c = x.shape
    g = info[1] * n + lax.broadcasted_iota(jnp.int32, (n, 1), 0)
    is_ctx = g < lc
    rr = jnp.where(is_ctx, g, g % GRID_W)
    first = rr == 0
    last = rr == jnp.where(is_ctx, lc - 1, GRID_W - 1)
    prev = jnp.where(first, 0.0, _shift_rows(x, 1))
    nxt = jnp.where(last, 0.0, _shift_rows(x, -1))
    y = b + prev * w[0:1] + x * w[1:2] + nxt * w[2:3]
    y = _silu(y)
    if mode == 'none':
        return (y,)
    scale = GDN_DK ** -0.5 if mode == 'q' else 1.0
    outs = []
    for h in range(c // 128):
        yh = y[:, h * 128:(h + 1) * 128]
        outs.append(yh * lax.rsqrt(jnp.sum(yh * yh, axis=-1, keepdims=True) + EPS) * scale)
    return (jnp.concatenate(outs, axis=1),)


def _act_fn(info, x, p0, p1):
    lane = lax.broadcasted_iota(jnp.int32, x.shape, 1)
    sp = jax.nn.softplus(x + p1)
    g = -jnp.exp(p0) * sp
    bt = jax.nn.sigmoid(x)
    return (jnp.where(lane < 16, g, jnp.where(lane < 32, bt, jnp.where(lane < 96, sp, 0.0))),)


def _mixg_fn(info, o, zg, gnw):
    outs = []
    for h in range(GDN_HEADS):
        outs.append(_rms(o[:, h * 128:(h + 1) * 128]) * gnw)
    return (jnp.concatenate(outs, axis=1) * _silu(zg),)


def _mixs_fn(info, y, xs, zs, dl, snw):
    yy = (y + dl * xs) * _silu(zs)
    outs = []
    for g in range(4):
        outs.append(_rms(yy[:, g * 512:(g + 1) * 512]))
    return (jnp.concatenate(outs, axis=1) * snw,)


def _merge_fn(info, gates, pg, ps):
    half = gates.shape[1] // 2
    return (jax.nn.sigmoid(gates[:, :half]) * pg + jax.nn.sigmoid(gates[:, half:]) * ps,)


def _norm2_fn(info, xt, mo, g8, w, sc8, sh8):
    h1 = xt + g8[0:1] * mo
    return (h1, _rms(h1) * w * (1.0 + sc8[0:1]) + sh8[0:1])


def _swiglu_fn(info, u2):
    return (_silu(u2[0]) * u2[1],)


_NN = (((1,), (0,)), ((), ()))
_NT = (((1,), (1,)), ((), ()))
_TN = (((0,), (0,)), ((), ()))


def _mmh(a, b):
    return lax.dot_general(a, b, _NN, precision=HI, preferred_element_type=F32)


def _dot1(a, b, dims):
    return lax.dot_general(a.astype(_SCAN_DTYPE), b.astype(_SCAN_DTYPE), dims, preferred_element_type=F32)


def _mm(a, b):
    return _dot1(a, b, _NN)


def _mm_nt(a, b):
    return _dot1(a, b, _NT)


def _mm_tn(a, b):
    return _dot1(a, b, _TN)


def _split2(a):
    hi = a.astype(_SCAN_DTYPE)
    return hi, (a - hi.astype(F32)).astype(_SCAN_DTYPE)


def _dot3(a, b, dims):
    ah, al = _split2(a)
    bh, bl = _split2(b)
    d = lambda u, w: lax.dot_general(u, w, dims, preferred_element_type=F32)
    return d(ah, bh) + (d(ah, bl) + d(al, bh))


def _order_masks(d):
    i = lax.broadcasted_iota(jnp.int32, (CHUNK, CHUNK), 0)
    j = lax.broadcasted_iota(jnp.int32, (CHUNK, CHUNK), 1)
    s = jnp.where(d == 0, 1, -1) * (i - j)
    return (s >= 0).astype(F32), (s > 0).astype(F32)


@jax.custom_vjp
def _unit_tri_inv(mats):
    i = lax.broadcasted_iota(jnp.int32, (CHUNK, CHUNK), 0)
    j = lax.broadcasted_iota(jnp.int32, (CHUNK, CHUNK), 1)
    eye = (i == j).astype(F32)
    ps = [-a for a in mats]
    ts = [eye + p for p in ps]
    for _ in range(5):
        ps = [_dot3(p, p, _NN) for p in ps]
        ts = [t + _dot3(t, p, _NN) for t, p in zip(ts, ps)]
    return tuple(ts)


def _uti_fwd(mats):
    ts = _unit_tri_inv(mats)
    return ts, ts


def _uti_bwd(ts, gs):
    inner = [_dot3(g, t, _NT) for g, t in zip(gs, ts)]
    return (tuple(-_dot3(t, u, _TN) for t, u in zip(ts, inner)),)


_unit_tri_inv.defvjp(_uti_fwd, _uti_bwd)


@jax.custom_vjp
def _unit_tri_inv_given(mats, ts):
    return ts


def _utig_fwd(mats, ts):
    return ts, ts


def _utig_bwd(ts, gs):
    return _uti_bwd(ts, gs)[0], tuple(jnp.zeros_like(t) for t in ts)


_unit_tri_inv_given.defvjp(_utig_fwd, _utig_bwd)


def _lane_col(blk, lane_idx):
    lane = lax.broadcasted_iota(jnp.int32, blk.shape, 1)
    return jnp.sum(jnp.where(lane == lane_idx, blk, 0.0), axis=1, keepdims=True)


def _decay_mat(cum, incl):
    cb = jnp.broadcast_to(cum, (CHUNK, CHUNK))
    return jnp.exp(jnp.minimum(cb - cb.T, 0.0)) * incl


def _gdn_chunks(dirs, streams, states, aux=None, want_aux=False, group=None):
    ns = len(dirs)
    masks = [_order_masks(d) for d in dirs]
    cum = [_mmh(masks[i][0], streams[i][3]) for i in range(ns)]
    tot = [jnp.sum(streams[i][3], axis=0, keepdims=True) for i in range(ns)]
    all_units = [(i, h) for i in range(ns) for h in range(GDN_HEADS)]
    group = group or len(all_units)
    cat = jnp.concatenate
    outs, new_states, ts_all = [], [], []
    for g0 in range(0, len(all_units), group):
        units = all_units[g0:g0 + group]
        us = range(len(units))
        st = states[g0:g0 + group]
        qs = [streams[i][0][:, h * 128:(h + 1) * 128] for i, h in units]
        ks = [streams[i][1][:, h * 128:(h + 1) * 128] for i, h in units]
        vs = [streams[i][2][:, h * 128:(h + 1) * 128] for i, h in units]
        gcum = [_lane_col(cum[i], dirs[i] * GDN_HEADS + h) for i, h in units]
        glast = [_lane_col(tot[i], dirs[i] * GDN_HEADS + h) for i, h in units]
        beta = [_lane_col(streams[i][3], 16 + dirs[i] * GDN_HEADS + h) for i, h in units]
        decay = [_decay_mat(gcum[u], masks[units[u][0]][0]) for u in us]
        egc = [jnp.exp(gcum[u]) for u in us]
        kb = [ks[u] * beta[u] for u in us]
        kq = [_mm_nt(cat([kb[u], qs[u]], axis=0), ks[u]) for u in us]
        mats = tuple(kq[u][:CHUNK] * decay[u] * masks[units[u][0]][1] for u in us)
        ts = _unit_tri_inv(mats) if aux is None else _unit_tri_inv_given(mats, tuple(aux[g0:g0 + group]))
        wu = [_mm(ts[u], cat([kb[u] * egc[u], vs[u] * beta[u]], axis=1)) for u in us]
        ws = [_mm(cat([wu[u][:, :128], qs[u] * egc[u]], axis=0), st[u]) for u in us]
        vn = [wu[u][:, 128:] - ws[u][:CHUNK] for u in us]
        outs += [ws[u][CHUNK:] + _mm(kq[u][CHUNK:] * decay[u], vn[u]) for u in us]
        new_states += [st[u] * jnp.exp(glast[u]) + _mm_tn(ks[u] * jnp.exp(glast[u] - gcum[u]), vn[u]) for u in us]
        ts_all += list(ts)
    per_stream = [cat(outs[i * GDN_HEADS:(i + 1) * GDN_HEADS], axis=1) for i in range(ns)]
    return (*per_stream, *new_states, *(ts_all if want_aux else ()))


def _gdn_step(d, q, k, v, spb, *states, aux=None, want_aux=False):
    return _gdn_chunks([d], [(q, k, v, spb)], states, aux, want_aux, group=4)


def _gdn_step2(q0, k0, v0, sp0, q1, k1, v1, sp1, *states, aux=None, want_aux=False):
    return _gdn_chunks([0, 1], [(q0, k0, v0, sp0), (q1, k1, v1, sp1)], states, aux, want_aux)


def _split3(a):
    a1 = a.astype(_SCAN_DTYPE)
    r = a - a1.astype(F32)
    a2 = r.astype(_SCAN_DTYPE)
    return a1, a2, (r - a2.astype(F32)).astype(_SCAN_DTYPE)


def _exact_dot(a, e, dims, split_lhs, passes=3):
    parts = _split3(a if split_lhs else e)[:passes]
    d = (lambda u: lax.dot_general(u, e, dims, preferred_element_type=F32)) if split_lhs else \
        (lambda u: lax.dot_general(a, u, dims, preferred_element_type=F32))
    return functools.reduce(lambda u, w: u + w, [d(p) for p in reversed(parts)])


@jax.custom_vjp
def _spread(a, e):
    return _exact_dot(a, e, _NN, True)


def _spread_fwd(a, e):
    return _spread(a, e), e


def _spread_bwd(e, g):
    return _exact_dot(g, e, _NT, True, passes=2), jnp.zeros_like(e)


_spread.defvjp(_spread_fwd, _spread_bwd)


@jax.custom_vjp
def _colsum_bcast(z):
    return _exact_dot(jnp.ones((z.shape[0], z.shape[0]), _SCAN_DTYPE), z, _NN, False)


def _colsum_fwd(z):
    return _colsum_bcast(z), None


def _colsum_bwd(_, g):
    return (_exact_dot(jnp.ones((g.shape[0], g.shape[0]), _SCAN_DTYPE), g, _NN, False, passes=2),)


_colsum_bcast.defvjp(_colsum_fwd, _colsum_bwd)


def _ssd_consts():
    wdt = SSM_HEADS * 64
    d = lax.broadcasted_iota(jnp.int32, (2, 128, wdt), 0)
    e = (lax.broadcasted_iota(jnp.int32, (2, 128, wdt), 1)
         == 32 + d * SSM_HEADS + lax.broadcasted_iota(jnp.int32, (2, 128, wdt), 2) // 64).astype(_SCAN_DTYPE)
    dd = lax.broadcasted_iota(jnp.int32, (2, CHUNK, wdt), 0)
    ci = lax.broadcasted_iota(jnp.int32, (2, CHUNK, wdt), 1)
    pos = lax.broadcasted_iota(jnp.int32, (2, CHUNK, wdt), 2) % 64
    incl_t = (jnp.where(dd == 0, 1, -1) * (ci - pos) >= 0).astype(F32)
    diag = (ci == pos).astype(F32)
    return [e, incl_t, diag]


def _ssd_step(d, x, bm, cm, spb, alog, *states, consts):
    e, incl_t, diag = consts
    incl, _ = _order_masks(d)
    lane1 = lax.broadcasted_iota(jnp.int32, (1, 128), 1)
    lo_lane = 32 + d * SSM_HEADS
    a_vec = jnp.where(lane1 >= lo_lane, jnp.where(lane1 < lo_lane + SSM_HEADS, -jnp.exp(alog), 0.0), 0.0)
    adt = spb * a_vec
    acum = _mmh(incl, adt)
    alast = jnp.sum(adt, axis=0, keepdims=True)
    dt2 = _spread(spb, e)
    ac2 = _spread(acum, e)
    al2 = _spread(jnp.broadcast_to(alast, (8, 128)), e)[0:1]
    row = _colsum_bcast(ac2 * diag)
    seg = jnp.exp(jnp.minimum(ac2 - row, 0.0)) * incl_t
    xdt = x * dt2
    gam = jnp.exp(ac2)
    xe = xdt * jnp.exp(al2 - ac2)
    low = lax.broadcasted_iota(jnp.int32, (CHUNK, 128), 1) < 64
    row_low = lax.broadcasted_iota(jnp.int32, (128, 1), 0) < 64
    ps = range(SSM_PAIRS)
    sl = [slice(p * 128, (p + 1) * 128) for p in ps]
    bg = [bm[:, g * 128:(g + 1) * 128] for g in range(4)]
    cg = [cm[:, g * 128:(g + 1) * 128] for g in range(4)]
    cb2 = [_mm_nt(cg[g], jnp.concatenate([bg[g], bg[g]], axis=0)) for g in range(4)]
    ys, new_states = [], []
    for p0 in range(0, SSM_PAIRS, SSD_GROUP):
        pg = range(p0, p0 + SSD_GROUP)
        xd = {p: jnp.concatenate([jnp.where(low, xdt[:, sl[p]], 0.0), jnp.where(low, 0.0, xdt[:, sl[p]])], axis=0)
              for p in pg}
        yd = {p: _mm(cb2[p // 4] * seg[:, sl[p]], xd[p]) for p in pg}
        yo = {p: _mm_nt(cg[p // 4], states[p]) for p in pg}
        ys += [yd[p] + gam[:, sl[p]] * yo[p] for p in pg]
        new = {p: _mm_tn(xe[:, sl[p]], bg[p // 4]) for p in pg}
        al0 = {p: _lane_col(alast, lo_lane + 2 * p) for p in pg}
        al1 = {p: _lane_col(alast, lo_lane + 2 * p + 1) for p in pg}
        new_states += [states[p] * jnp.exp(jnp.where(row_low, al0[p], al1[p])) + new[p] for p in pg]
    return (jnp.concatenate(ys, axis=1), *new_states)


def _chunk_of(d, p, nctx, nc):
    return jnp.where(d == 0, p, jnp.where(p < nctx, nctx - 1 - p, nctx + nc - 1 - p))


class _NoRide:
    n, arrs, in_specs, out_specs, out_shape, scratch = 0, [], [], [], [], []


def _const_specs(consts):
    return [pl.BlockSpec((1,) + a.shape[1:], lambda d, p: (d,) + (0,) * (a.ndim - 1)) for a in consts]


def _scan_fwd(name, step, toks, pars, consts, out_width, n_state, nctx):
    t = toks[0].shape[0]
    nc = t // CHUNK
    n_tok, n_par, n_const = len(toks), len(pars), len(consts)

    def body(*refs):
        it = iter(refs)
        tok_refs = [next(it) for _ in range(n_tok)]
        par_refs = [next(it) for _ in range(n_par)]
        const_refs = [next(it) for _ in range(n_const)]
        o_ref, ss_ref, s_scr = next(it), next(it), next(it)
        d, p = pl.program_id(0), pl.program_id(1)

        @pl.when(p == 0)
        def _():
            s_scr[...] = jnp.zeros(s_scr.shape, F32)

        ss_ref[0, 0] = s_scr[...]
        res = step(d, *[r[...] for r in tok_refs], *[r[...] for r in par_refs], *[s_scr[h] for h in range(n_state)],
                   consts=[r[0] for r in const_refs])
        o_ref[0] = res[0]
        for h in range(n_state):
            s_scr[h] = res[1 + h]

    ch = lambda d, p: _chunk_of(d, p, nctx, nc)
    in_specs = [pl.BlockSpec((CHUNK, a.shape[1]), lambda d, p: (ch(d, p), 0)) for a in toks]
    in_specs += [pl.BlockSpec(a.shape, lambda d, p: (0, 0)) for a in pars]
    return pl.pallas_call(
        body, grid=(2, nc), in_specs=in_specs + _const_specs(consts),
        out_specs=[pl.BlockSpec((1, CHUNK, out_width), lambda d, p: (d, ch(d, p), 0)),
                   pl.BlockSpec((1, 1, n_state, 128, 128), lambda d, p: (d, p, 0, 0, 0))],
        out_shape=[jax.ShapeDtypeStruct((2, t, out_width), F32),
                   jax.ShapeDtypeStruct((2, nc, n_state, 128, 128), F32)],
        scratch_shapes=[pltpu.VMEM((n_state, 128, 128), F32)],
        compiler_params=_cp(("arbitrary", "arbitrary")), name=name)(*toks, *pars, *consts)


def _scan_bwd(name, step, toks, pars, ss, dout, n_state, nctx, ride=None, aux=None, consts=()):
    t = toks[0].shape[0]
    nc = t // CHUNK
    n_tok, n_par, n_const = len(toks), len(pars), len(consts)
    rd = ride if ride is not None else _NoRide
    n_aux = aux.shape[2] if aux is not None else 0

    def body(*refs):
        it = iter(refs)
        tok_refs = [next(it) for _ in range(n_tok)]
        par_refs = [next(it) for _ in range(n_par)]
        ss_ref, do_ref = next(it), next(it)
        aux_ref = next(it) if n_aux else None
        const_refs = [next(it) for _ in range(n_const)]
        ride_in = [next(it) for _ in range(rd.n)]
        dtok_refs = [next(it) for _ in range(n_tok)]
        dpar_refs = [next(it) for _ in range(n_par)]
        ride_out = [next(it) for _ in range(rd.n)]
        ds_scr = next(it)
        sems = list(it)
        d, pr = pl.program_id(0), pl.program_id(1)
        if ride is not None:
            @pl.when((d == 0) & (pr == 0))
            def _():
                ride.start(ride_in, ride_out, sems)

            @pl.when((d == 1) & (pr == nc - 1))
            def _():
                ride.wait(ride_in, ride_out, sems)

        @pl.when(pr == 0)
        def _():
            ds_scr[...] = jnp.zeros(ds_scr.shape, F32)

        kw = dict(aux=[aux_ref[0, 0, i] for i in range(n_aux)]) if n_aux else {}
        if n_const:
            kw['consts'] = [r[0] for r in const_refs]
        _, pull = jax.vjp(functools.partial(step, d, **kw), *[r[...] for r in tok_refs], *[r[...] for r in par_refs],
                          *[ss_ref[0, 0, h] for h in range(n_state)])
        grads = pull((do_ref[...], *[ds_scr[h] for h in range(n_state)]))
        for r, g in zip(dtok_refs, grads[:n_tok]):
            r[0] = g
        for h in range(n_state):
            ds_scr[h] = grads[n_tok + n_par + h]
        first = (d == 0) & (pr == 0)
        for r, g in zip(dpar_refs, grads[n_tok:n_tok + n_par]):
            @pl.when(first)
            def _(r=r, g=g):
                r[...] = g

            @pl.when(jnp.logical_not(first))
            def _(r=r, g=g):
                r[...] += g

    ch = lambda d, pr: _chunk_of(d, nc - 1 - pr, nctx, nc)
    in_specs = [pl.BlockSpec((CHUNK, a.shape[1]), lambda d, pr: (ch(d, pr), 0)) for a in toks]
    in_specs += [pl.BlockSpec(a.shape, lambda d, pr: (0, 0)) for a in pars]
    in_specs += [pl.BlockSpec((1, 1, n_state, 128, 128), lambda d, pr: (d, nc - 1 - pr, 0, 0, 0)),
                 pl.BlockSpec((CHUNK, dout.shape[1]), lambda d, pr: (ch(d, pr), 0))]
    if n_aux:
        in_specs += [pl.BlockSpec((1, 1, n_aux, CHUNK, CHUNK), lambda d, pr: (d, nc - 1 - pr, 0, 0, 0))]
    in_specs += _const_specs(consts)
    out_specs = [pl.BlockSpec((1, CHUNK, a.shape[1]), lambda d, pr: (d, ch(d, pr), 0)) for a in toks]
    out_specs += [pl.BlockSpec(a.shape, lambda d, pr: (0, 0)) for a in pars]
    out_shape = [jax.ShapeDtypeStruct((2, t, a.shape[1]), F32) for a in toks]
    out_shape += [jax.ShapeDtypeStruct(a.shape, F32) for a in pars]
    return pl.pallas_call(
        body, grid=(2, nc), in_specs=in_specs + rd.in_specs, out_specs=out_specs + rd.out_specs,
        out_shape=out_shape + rd.out_shape, scratch_shapes=[pltpu.VMEM((n_state, 128, 128), F32)] + rd.scratch,
        compiler_params=_cp(("arbitrary", "arbitrary")), name=name)(
            *toks, *pars, ss, dout, *([aux] if n_aux else []), *consts, *rd.arrs)


def _scan2_fwd(name, step2, toks, out_width, n_state, nctx, n_aux, ride=None):
    t = toks[0].shape[0]
    nc = t // CHUNK
    n_tok = len(toks)
    rd = ride if ride is not None else _NoRide

    def body(*refs):
        it = iter(refs)
        tok_refs = [next(it) for _ in range(2 * n_tok)]
        ride_in = [next(it) for _ in range(rd.n)]
        o_refs = [next(it), next(it)]
        ss_ref, aux_ref = next(it), next(it)
        ride_out = [next(it) for _ in range(rd.n)]
        s_scr = next(it)
        sems = list(it)
        p = pl.program_id(0)
        if ride is not None:
            @pl.when(p == 0)
            def _():
                ride.start(ride_in, ride_out, sems)

        @pl.when(p == 0)
        def _():
            s_scr[...] = jnp.zeros(s_scr.shape, F32)

        for d in range(2):
            ss_ref[d, 0] = s_scr[d * n_state:(d + 1) * n_state]
        res = step2(*[r[...] for r in tok_refs], *[s_scr[u] for u in range(2 * n_state)], want_aux=True)
        for d in range(2):
            o_refs[d][...] = res[d]
            for i in range(n_aux):
                aux_ref[d, 0, i] = res[2 + 2 * n_state + d * n_aux + i]
        for u in range(2 * n_state):
            s_scr[u] = res[2 + u]
        if ride is not None:
            @pl.when(p == nc - 1)
            def _():
                ride.wait(ride_in, ride_out, sems)

    def tok_spec(a, d):
        return pl.BlockSpec((CHUNK, a.shape[1]), lambda p: (_chunk_of(d, p, nctx, nc), 0))

    return pl.pallas_call(
        body, grid=(nc,), in_specs=[tok_spec(a, d) for d in range(2) for a in toks] + rd.in_specs,
        out_specs=[pl.BlockSpec((CHUNK, out_width), lambda p: (_chunk_of(0, p, nctx, nc), 0)),
                   pl.BlockSpec((CHUNK, out_width), lambda p: (_chunk_of(1, p, nctx, nc), 0)),
                   pl.BlockSpec((2, 1, n_state, 128, 128), lambda p: (0, p, 0, 0, 0)),
                   pl.BlockSpec((2, 1, n_aux, CHUNK, CHUNK), lambda p: (0, p, 0, 0, 0))] + rd.out_specs,
        out_shape=[jax.ShapeDtypeStruct((t, out_width), F32), jax.ShapeDtypeStruct((t, out_width), F32),
                   jax.ShapeDtypeStruct((2, nc, n_state, 128, 128), F32),
                   jax.ShapeDtypeStruct((2, nc, n_aux, CHUNK, CHUNK), F32)] + rd.out_shape,
        scratch_shapes=[pltpu.VMEM((2 * n_state, 128, 128), F32)] + rd.scratch,
        compiler_params=_cp(("arbitrary",)), name=name)(*toks, *toks, *rd.arrs)


ADA_TN = 512


def _ada_part(cc16, w_shard):
    n = w_shard.shape[1]

    def body(c_ref, w_ref, o_ref):
        s = _silu(c_ref[...]).astype(_MXU_DTYPE)
        o_ref[...] = jnp.dot(s, w_ref[...].astype(_MXU_DTYPE), preferred_element_type=F32)

    return pl.pallas_call(
        body, grid=(n // ADA_TN,),
        in_specs=[pl.BlockSpec((16, D_MODEL), lambda j: (0, 0)), pl.BlockSpec((D_MODEL, ADA_TN), lambda j: (0, j))],
        out_specs=pl.BlockSpec((16, ADA_TN), lambda j: (0, j)), out_shape=jax.ShapeDtypeStruct((16, n), F32),
        compiler_params=_cp(("parallel",)), name="ada_part")(cc16, w_shard)


def _ada_bwd_shard(cc16, w_shard, d_lat, d_ctx):
    n = w_shard.shape[1]
    nj = n // ADA_TN

    def body(c_ref, w_ref, gl_ref, gc_ref, dw_ref, dc_ref):
        j = pl.program_id(0)
        s, pull = jax.vjp(_silu, c_ref[...])
        row = lax.broadcasted_iota(jnp.int32, (8, 1), 0)
        dctx = jnp.where(row == 0, jnp.sum(gc_ref[...], axis=0, keepdims=True), 0.0)
        mxu = lambda v: v.astype(_MXU_DTYPE)
        dw_ref[...] = (lax.dot_general(mxu(s[0:8]), mxu(gl_ref[...]), _TN, preferred_element_type=F32)
                       + lax.dot_general(mxu(s[8:16]), mxu(dctx), _TN, preferred_element_type=F32))
        ds = lax.dot_general(mxu(dctx), mxu(w_ref[...]), _NT, preferred_element_type=F32)

        @pl.when(j == 0)
        def _():
            dc_ref[...] = ds

        @pl.when(j > 0)
        def _():
            dc_ref[...] += ds

        @pl.when(j == nj - 1)
        def _():
            ct = jnp.concatenate([jnp.zeros((8, D_MODEL), F32), dc_ref[...]], axis=0)
            dc_ref[...] = 0.5 * pull(ct)[0][8:16]

    tile = pl.BlockSpec((8, ADA_TN), lambda j: (0, j))
    wspec = pl.BlockSpec((D_MODEL, ADA_TN), lambda j: (0, j))
    return pl.pallas_call(
        body, grid=(nj,), in_specs=[pl.BlockSpec((16, D_MODEL), lambda j: (0, 0)), wspec, tile, tile],
        out_specs=[wspec, pl.BlockSpec((8, D_MODEL), lambda j: (0, 0))],
        out_shape=[jax.ShapeDtypeStruct((D_MODEL, n), F32), jax.ShapeDtypeStruct((8, D_MODEL), F32)],
        compiler_params=_cp(("arbitrary",)), name="ada_bwd_shard")(cc16, w_shard, d_lat, d_ctx)


def _rowsum2(a, b):
    def body(a_ref, b_ref, o_ref):
        o_ref[...] = jnp.sum(a_ref[...], axis=0, keepdims=True) + jnp.sum(b_ref[...], axis=0, keepdims=True)

    return pl.pallas_call(body, out_shape=jax.ShapeDtypeStruct((1, a.shape[1]), F32), name="ada_b_grad")(a, b)


def _tail(h1, ff, mods, wf, tgt, nctx_t, tl):
    t = h1.shape[0]
    nt = t // tl

    def loss_fn(valid, h1v, ffv, g8, w, tg):
        h2 = h1v + g8[0:1] * ffv
        y = _rms(h2) * w
        err = (y - tg) ** 2
        return 0.5 * jnp.sum(jnp.mean(err, axis=-1, keepdims=True), axis=0, keepdims=True) * valid

    def body(h1_ref, ff_ref, g_ref, w_ref, t_ref, loss_ref, dh_ref, dff_ref, dg_ref, dw_ref):
        i = pl.program_id(0)
        valid = jnp.where(i < nctx_t, 0.0, 1.0)
        lv, pull = jax.vjp(functools.partial(loss_fn, valid), h1_ref[...], ff_ref[...].astype(F32), g_ref[...],
                           w_ref[...], t_ref[...])
        dh, dff, dg, dw, _ = pull(jnp.ones((1, 1), F32))
        dh_ref[...] = dh
        dff_ref[...] = dff.astype(dff_ref.dtype)
        lb = jnp.broadcast_to(lv, loss_ref.shape)

        @pl.when(i == 0)
        def _():
            loss_ref[...] = lb
            dg_ref[...] = dg
            dw_ref[...] = dw

        @pl.when(i > 0)
        def _():
            loss_ref[...] += lb
            dg_ref[...] += dg
            dw_ref[...] += dw

    tok = pl.BlockSpec((tl, D_MODEL), lambda i: (i, 0))
    return pl.pallas_call(
        body, grid=(nt,),
        in_specs=[tok, tok, pl.BlockSpec((8, D_MODEL), lambda i: (0, 5)), pl.BlockSpec((1, D_MODEL), lambda i: (0, 0)),
                  pl.BlockSpec((tl, D_MODEL), lambda i: (jnp.maximum(i - nctx_t, 0), 0))],
        out_specs=[pl.BlockSpec((8, 128), lambda i: (0, 0)), tok, tok, pl.BlockSpec((8, D_MODEL), lambda i: (0, 0)),
                   pl.BlockSpec((1, D_MODEL), lambda i: (0, 0))],
        out_shape=[jax.ShapeDtypeStruct((8, 128), F32), jax.ShapeDtypeStruct((t, D_MODEL), F32),
                   jax.ShapeDtypeStruct((t, D_MODEL), _MXU_DTYPE), jax.ShapeDtypeStruct((8, D_MODEL), F32),
                   jax.ShapeDtypeStruct((1, D_MODEL), F32)],
        compiler_params=_cp(("arbitrary",)), name="tail_loss")(h1, ff, mods, wf, tgt)


def _pack_w_in(w4):
    ns = w4.shape[2]
    placed = []
    for s0, s1, p0 in IN_SEGMENTS:
        for j in range(N_PLANE):
            lo, hi = max(s0, j * ns), min(s1, (j + 1) * ns)
            if lo < hi:
                placed.append((p0 + lo - s0, w4[j][:, lo - j * ns:hi - j * ns]))
    placed.sort(key=lambda e: e[0])
    pieces, end = [], 0
    for pos, piece in placed:
        assert pos == end, (pos, end)
        pieces.append(piece)
        end = pos + piece.shape[1]
    pieces.append(jnp.zeros((w4.shape[1], P_TOTAL - end), w4.dtype))
    return jnp.concatenate(pieces, axis=1)


def _unpack_w_in(g):
    ns = D_IN_PROJ // N_PLANE
    shards = []
    for j in range(N_PLANE):
        pieces = []
        for s0, s1, p0 in IN_SEGMENTS:
            lo, hi = max(s0, j * ns), min(s1, (j + 1) * ns)
            if lo < hi:
                pieces.append(g[:, p0 + lo - s0:p0 + hi - s0])
        shards.append(jnp.concatenate(pieces, axis=1))
    return jnp.stack(shards)


LATE_WEIGHTS = ['w_br_gdn', 'w_br_ssm', 'w_out', 'w_ffn_in', 'w_ffn_out']
COL_STACKED = ('w_in', 'w_ffn_in')


def _from_shards(n, g):
    if n in COL_STACKED:
        return g
    if SHARD_AXIS[n] == 0:
        return g.reshape(N_PLANE * g.shape[1], g.shape[2])
    return jnp.concatenate([g[j] for j in range(N_PLANE)], axis=1)


def _to_dest_blocks(n, g):
    if g.ndim == 3:
        return g
    if SHARD_AXIS[n] == 0:
        return g.reshape(N_PLANE, g.shape[0] // N_PLANE, g.shape[1])
    sz = g.shape[1] // N_PLANE
    return jnp.stack([g[:, j * sz:(j + 1) * sz] for j in range(N_PLANE)])


def _local_step(x, c, ctx, tgt, W, late_shards):
    lc, l = ctx.shape[0], x.shape[0]
    t = lc + l
    tl = 256
    assert lc == tl and l % tl == 0 and lc % CHUNK == 0
    nt, nctx_t, nctx = t // tl, lc // tl, lc // CHUNK
    act = _MXU_DTYPE
    r1 = lambda v: v.reshape(1, -1)

    xt = jnp.concatenate([ctx, x], axis=0)
    lin = 4 * lax.axis_index("x") + 2 * lax.axis_index("y") + lax.axis_index("c")
    c_all = _gather_all("gather_c", jnp.broadcast_to(c, (8, D_MODEL)))[:, 0]
    cc16 = jnp.concatenate([c_all, r1(W['c_ctx']), jnp.zeros((7, D_MODEL), F32)], axis=0)
    (parts,) = _plane_exchange("gather_mods", [_ada_part(cc16, W['ada_w'])], gather=True)
    mods_all = jnp.transpose(parts, (1, 0, 2)).reshape(16, -1) + r1(W['ada_b'])
    mods = jnp.concatenate([lax.dynamic_slice_in_dim(mods_all, lin, 1, axis=0), mods_all[8:9],
                            jnp.zeros((6, mods_all.shape[1]), F32)], axis=0)

    def mod(kk):
        return Op(mods, (8, D_MODEL), lambda j, i, kk=kk: (0, kk))

    def tokop(arr, w=D_MODEL, off=0, tl_=tl):
        if not isinstance(arr, list) and arr.ndim == 3:
            return Op(arr, (2, tl_, w), lambda j, i: (0, i, off + j), load=_sum_dirs)
        return Op(arr, (tl_, w), lambda j, i: (i, off + j))

    def outop(n, dtype, w=D_MODEL, tl_=tl):
        return Op(((t, n), dtype), (tl_, w), lambda j, i: (i, j))

    def parop(arr, w, off=0):
        return Op(arr, (arr.shape[0], w), lambda j, i: (0, off + j))

    def parout(rows, n, w):
        return Op(((rows, n), F32), (rows, w), lambda j, i: (0, j))

    n1w = r1(W['norm1_w'])
    pre_fn = functools.partial(_prenorm_fn, nctx_t)
    pre_tok, pre_par = [tokop(xt)], [parop(n1w, D_MODEL), mod(1), mod(0)]
    (a,) = _tw_fwd("prenorm_fwd", pre_fn, (1, nt), pre_tok + pre_par, [outop(D_MODEL, act)])
    wp = _pack_w_in(W['w_in'])
    proj = _matmul(a, wp, 'nn', "in_proj")

    gcw, gcb = W['gdn_conv_w'], r1(W['gdn_conv_b'])
    scw, scb = W['ssm_conv_w'], r1(W['ssm_conv_b'])
    conv_parts = {}

    tlf = 768 if t % 768 == 0 else tl

    def conv_part(name, mode, poff, cw, cb, woff, width):
        fn = functools.partial(_conv_fn, lc, mode)
        bw = min(width, 1024)
        par_ = [parop(cw, bw, woff // bw), parop(cb, bw, woff // bw)]
        conv_parts[name] = (fn, [tokop(proj, bw, poff // bw)], par_, width, bw, poff)
        (o,) = _tw_fwd("conv_" + name, fn, (width // bw, t // tlf), [tokop(proj, bw, poff // bw, tlf)] + par_,
                       [outop(width, F32, bw, tlf)])
        return o

    q = conv_part('q', 'q', P_QKV, gcw, gcb, 0, 1024)
    k = conv_part('k', 'k', P_QKV + 1024, gcw, gcb, 1024, 1024)
    v = conv_part('v', 'none', P_QKV + 2048, gcw, gcb, 2048, 1024)
    xs = conv_part('xs', 'none', P_XBC, scw, scb, 0, 2048)
    bm = conv_part('bm', 'none', P_XBC + 2048, scw, scb, 2048, 512)
    cm = conv_part('cm', 'none', P_XBC + 2560, scw, scb, 2560, 512)

    z16, z32 = jnp.zeros((16,), F32), jnp.zeros((32,), F32)
    p0 = jnp.concatenate([W['gdn_a_log'].reshape(-1), jnp.zeros((112,), F32)]).reshape(1, 128)
    p1 = jnp.concatenate([W['gdn_dt_bias'].reshape(-1), z16, W['ssm_dt_bias'].reshape(-1), z32]).reshape(1, 128)
    alog = jnp.concatenate([z32, W['ssm_a_log'].reshape(-1), z32]).reshape(1, 128)
    act_tok, act_par = [tokop(proj, 128, P_SMALL // 128)], [parop(p0, 128), parop(p1, 128)]
    (sp,) = _tw_fwd("small_act", _act_fn, (1, t // tlf), [tokop(proj, 128, P_SMALL // 128, tlf)] + act_par,
                    [outop(128, F32, 128, tlf)])

    gather_late = PlaneExchange([late_shards[n] for n in LATE_WEIGHTS], gather=True)
    o_f, o_b, ss, tri, *late = _scan2_fwd("gdn_scan_fwd", _gdn_step2, [q, k, v, sp], 1024, GDN_HEADS, nctx,
                                          GDN_HEADS, ride=gather_late)
    W = dict(W, **{n: _from_shards(n, g) for n, g in zip(LATE_WEIGHTS, late)})
    ssd_consts = _ssd_consts()
    y2, hs = _scan_fwd("ssd_scan_fwd", _ssd_step, [xs, bm, cm, sp], [alog], ssd_consts, 2048, SSM_PAIRS, nctx)

    tlm = 128
    ntm = t // tlm
    gnw = r1(W['gdn_norm_w'])
    dl = jnp.repeat(W['ssm_d'], 64).reshape(1, 2048)
    snw = r1(W['ssm_norm_w'])
    mg_tok = [tokop([o_f, o_b], 1024, 0, tlm), tokop(proj, 1024, P_ZG // 1024, tlm)]
    mg_par = [parop(gnw, 128)]
    (og,) = _tw_fwd("mix_gdn", _mixg_fn, (1, nt),
                    [tokop([o_f, o_b], 1024, 0), tokop(proj, 1024, P_ZG // 1024)] + mg_par, [outop(1024, act, 1024)])
    ms_tok = [tokop(y2, 2048, 0, tlm), tokop(xs, 2048, 0, tlm), tokop(proj, 2048, P_ZS // 2048, tlm)]
    ms_par = [parop(dl, 2048), parop(snw, 2048)]
    (yg,) = _tw_fwd("mix_ssm", _mixs_fn, (1, nt),
                    [tokop(y2, 2048, 0), tokop(xs, 2048, 0), tokop(proj, 2048, P_ZS // 2048)] + ms_par,
                    [outop(2048, act, 2048)])

    pg = _matmul(og, W['w_br_gdn'], 'nn', "br_gdn", out_dtype=act)
    ps = _matmul(yg, W['w_br_ssm'], 'nn', "br_ssm", out_dtype=act)
    mr_tok = [tokop(proj, 2048, P_GATE // 2048), tokop(pg), tokop(ps)]
    (mrg,) = _tw_fwd("merge", _merge_fn, (1, t // tlf),
                     [tokop(proj, 2048, P_GATE // 2048, tlf), tokop(pg, tl_=tlf), tokop(ps, tl_=tlf)],
                     [outop(1024, act, tl_=tlf)])
    mo = _matmul(mrg, W['w_out'], 'nn', "w_out")

    n2w = r1(W['norm2_w'])
    n2_tok, n2_par = [tokop(xt), tokop(mo)], [mod(2), parop(n2w, D_MODEL), mod(4), mod(3)]
    h1, f = _tw_fwd("norm2_fwd", _norm2_fn, (1, t // tlf), [tokop(xt, tl_=tlf), tokop(mo, tl_=tlf)] + n2_par,
                    [outop(1024, F32, tl_=tlf), outop(1024, act, tl_=tlf)])
    u2 = _matmul(f, W['w_ffn_in'], 'nn', "ffn_in", out_dtype=act, halves='out')
    swb = D_FF // 2
    both = lambda arr, tl_=tl: Op(arr, (2, tl_, swb), lambda j, i: (0, i, j))
    sw_tok = [both(u2)]
    (sw,) = _tw_fwd("swiglu", _swiglu_fn, (D_FF // swb, t // tlf), [both(u2, tlf)], [outop(D_FF, act, swb, tlf)])
    ff = _matmul(sw, W['w_ffn_out'], 'nn', "ffn_out")

    loss8, dh1, dff, dg2, dnf = _tail(h1, ff, mods, r1(W['norm_f_w']), tgt, nctx_t, tl)
    loss = loss8[0, 0]

    G = {}
    G['norm_f_w'] = dnf.reshape(-1)
    G['w_ffn_out'] = _matmul(sw, dff, 'tn', "d_ffn_out", out_dtype=_GRAD_DTYPE)
    dsw = _matmul(dff, W['w_ffn_out'], 'nt', "d_sw", out_dtype=act)
    (du2,) = _tw_bwd("swiglu_bwd", _swiglu_fn, (D_FF // swb, nt), sw_tok, [], [[tokop(dsw, swb)]],
                     [both(((2, t, D_FF), act))], [])
    G['w_ffn_in'] = _matmul(f, du2, 'tn', "d_ffn_in", out_dtype=_GRAD_DTYPE, stacked_out=True, halves='b')
    df = _matmul(du2, W['w_ffn_in'], 'nt', "d_f", halves='a')
    dxt1, dmo, dg1, dn2, dsc2, dsh2 = _tw_bwd(
        "norm2_bwd", _norm2_fn, (1, nt), n2_tok, n2_par, [[tokop(dh1)], [tokop(df)]],
        [outop(1024, F32), outop(1024, act)],
        [parout(8, 1024, 1024), parout(1, 1024, 1024), parout(8, 1024, 1024), parout(8, 1024, 1024)])
    G['norm2_w'] = dn2.reshape(-1)
    G['w_out'] = _matmul(mrg, dmo, 'tn', "d_w_out", out_dtype=_GRAD_DTYPE)
    dmrg = _matmul(dmo, W['w_out'], 'nt', "d_mrg", out_dtype=act)
    def win(off, w, tl_=tl):
        return Op(((t, P_TOTAL), act), (tl_, w), lambda j, i: (i, off // w + j))

    dproj = lax.empty((t, P_TOTAL), act)
    dproj, dpg, dps = _tw_bwd("merge_bwd", _merge_fn, (1, nt), mr_tok, [], [[tokop(dmrg)]],
                              [win(P_GATE, 2048), outop(1024, act), outop(1024, act)], [], into={0: dproj})
    G['w_br_gdn'] = _matmul(og, dpg, 'tn', "d_br_gdn", out_dtype=_GRAD_DTYPE)
    G['w_br_ssm'] = _matmul(yg, dps, 'tn', "d_br_ssm", out_dtype=_GRAD_DTYPE)
    dog = _matmul(dpg, W['w_br_gdn'], 'nt', "d_og")
    dyg = _matmul(dps, W['w_br_ssm'], 'nt', "d_yg")

    do, dproj, dgnw = _tw_bwd("mix_gdn_bwd", _mixg_fn, (1, ntm), mg_tok, mg_par, [[tokop(dog, 1024, 0, tlm)]],
                              [outop(1024, F32, 1024, tlm), win(P_ZG, 1024, tlm)], [parout(1, 128, 128)],
                              into={1: dproj})
    G['gdn_norm_w'] = dgnw.reshape(-1)
    dy, dxs_a, dproj, ddl, dsnw = _tw_bwd(
        "mix_ssm_bwd", _mixs_fn, (1, ntm), ms_tok, ms_par, [[tokop(dyg, 2048, 0, tlm)]],
        [outop(2048, F32, 2048, tlm), outop(2048, F32, 2048, tlm), win(P_ZS, 2048, tlm)],
        [parout(1, 2048, 2048), parout(1, 2048, 2048)], into={2: dproj})
    G['ssm_d'] = ddl.reshape(SSM_HEADS, 64).sum(axis=1)
    G['ssm_norm_w'] = dsnw.reshape(-1)

    dq2, dk2, dv2, dsp_g = _scan_bwd("gdn_scan_bwd", _gdn_step, [q, k, v, sp], [], ss, do, GDN_HEADS, nctx, aux=tri)
    received = {}
    scatter_late = PlaneExchange([_to_dest_blocks(n, G.pop(n)) for n in LATE_WEIGHTS], gather=False)
    dxs2, dbm2, dcm2, dsp_s, dalog, *got = _scan_bwd("ssd_scan_bwd", _ssd_step, [xs, bm, cm, sp], [alog], hs, dy,
                                                     SSM_PAIRS, nctx, ride=scatter_late, consts=ssd_consts)
    received.update(zip(LATE_WEIGHTS, got))
    G['ssm_a_log'] = dalog[0, 32:96].reshape(2, SSM_HEADS)

    dconv_w, dconv_b = {}, {}

    def conv_bwd(name, cots_, buf):
        fn, tok_, par_, width, bw, poff = conv_parts[name]
        buf, dconv_w[name], dconv_b[name] = _tw_bwd(
            "conv_" + name + "_bwd", fn, (width // bw, nt), tok_, par_, [[tokop(c_, bw) for c_ in cots_]],
            [win(poff, bw)], [parout(3, width, bw), parout(1, width, bw)], into={0: buf})
        return buf

    dproj = conv_bwd('q', [dq2], dproj)
    dproj = conv_bwd('k', [dk2], dproj)
    dproj = conv_bwd('v', [dv2], dproj)
    dproj = conv_bwd('xs', [dxs2, dxs_a], dproj)
    dproj = conv_bwd('bm', [dbm2], dproj)
    dproj = conv_bwd('cm', [dcm2], dproj)
    G['gdn_conv_w'] = jnp.concatenate([dconv_w['q'], dconv_w['k'], dconv_w['v']], axis=1)
    G['gdn_conv_b'] = [dconv_b['q'], dconv_b['k'], dconv_b['v']]
    G['ssm_conv_w'] = jnp.concatenate([dconv_w['xs'], dconv_w['bm'], dconv_w['cm']], axis=1)
    G['ssm_conv_b'] = [dconv_b['xs'], dconv_b['bm'], dconv_b['cm']]

    dproj, dp0, dp1 = _tw_bwd("small_act_bwd", _act_fn, (1, nt), act_tok, act_par,
                              [[tokop(dsp_g, 128), tokop(dsp_s, 128)]], [win(P_SMALL, P_TOTAL - P_SMALL)],
                              [parout(1, 128, 128), parout(1, 128, 128)], into={0: dproj})
    G['gdn_a_log'] = dp0[0, 0:16].reshape(2, GDN_HEADS)
    G['gdn_dt_bias'] = dp1[0, 0:16].reshape(2, GDN_HEADS)
    G['ssm_dt_bias'] = dp1[0, 32:96].reshape(2, SSM_HEADS)

    g_w_in = _unpack_w_in(_matmul(a, dproj, 'tn', "d_w_in", out_dtype=_GRAD_DTYPE))
    da, (received['w_in'],) = _matmul(dproj, wp, 'nt', "d_a", ride=PlaneExchange([g_w_in], gather=False))

    gx_out = Op(((l, D_MODEL), F32), (tl, D_MODEL), lambda j, i: (jnp.maximum(i - nctx_t, 0), 0))
    grad_x, dn1, dsc1, dsh1 = _tw_bwd(
        "prenorm_bwd", pre_fn, (1, nt), pre_tok, pre_par, [[tokop(da)]], [gx_out],
        [parout(1, 1024, 1024), parout(8, 1024, 1024), parout(8, 1024, 1024)], tok_add=tokop(dxt1),
        sem=("arbitrary", "arbitrary"))
    G['norm1_w'] = dn1.reshape(-1)
    dmods = jnp.concatenate([dsh1, dsc1, dg1, dsh2, dsc2, dg2], axis=1)
    dm_all = _gather_all("gather_dmods", dmods)
    ns = W['ada_w'].shape[1]
    mine = lax.dynamic_slice_in_dim(dm_all, (2 * lax.axis_index("x") + lax.axis_index("y")) * ns, ns, axis=2)
    G['ada_w'], dcc = _ada_bwd_shard(cc16, W['ada_w'], mine[:, 0], mine[:, 1])
    G['ada_b'] = jnp.where(lin == 0, _rowsum2(dm_all[:, 0], dm_all[:, 1]).reshape(-1), 0.0)
    G['c_ctx'] = dcc[0]
    return loss, grad_x, G, received


def _row_tile(r, c):
    for tr in (512, 256, 128, 64, 32, 16, 8):
        if r % tr == 0 and tr * c * 4 <= (1 << 20):
            return tr
    return r


def _sum4(name, rv):
    _, r, c = rv.shape
    tr = _row_tile(r, c)

    def body(r_ref, o_ref):
        o_ref[...] = ((r_ref[0].astype(F32) + r_ref[1].astype(F32)) + r_ref[2].astype(F32)) + r_ref[3].astype(F32)

    return pl.pallas_call(
        body, grid=(r // tr,), in_specs=[pl.BlockSpec((N_PLANE, tr, c), lambda i: (0, i, 0))],
        out_specs=pl.BlockSpec((tr, c), lambda i: (i, 0)), out_shape=jax.ShapeDtypeStruct((r, c), F32),
        compiler_params=_cp(("parallel",)), name=name)(rv)


def _adamw(name, w, m, v, p, q=None):
    r, c = w.shape[-2:]
    tr = _row_tile(r, c)
    grads = [p] if q is None else [p, q]
    lead = (0,) * (w.ndim - 2)

    def body(w_ref, m_ref, v_ref, *refs):
        g_ref, d_ref, mo_ref, vo_ref = refs[len(grads):]
        g = refs[0][...] if q is None else refs[0][...] + refs[1][...]
        at = lead if lead else Ellipsis
        mn = ADAM_B1 * m_ref[at] + (1.0 - ADAM_B1) * g
        vn = ADAM_B2 * v_ref[at] + (1.0 - ADAM_B2) * jnp.square(g)
        m_hat = mn / (1.0 - ADAM_B1 ** ADAM_STEP)
        v_hat = vn / (1.0 - ADAM_B2 ** ADAM_STEP)
        g_ref[at] = g
        d_ref[at] = -ADAM_LR * (m_hat / (jnp.sqrt(v_hat) + ADAM_EPS) + ADAM_WD * w_ref[at])
        mo_ref[at] = mn
        vo_ref[at] = vn

    gspec = pl.BlockSpec((tr, c), lambda i: (i, 0))
    wspec = pl.BlockSpec((1,) * len(lead) + (tr, c), lambda i: lead + (i, 0))
    return pl.pallas_call(
        body, grid=(r // tr,), in_specs=[wspec] * 3 + [gspec] * len(grads), out_specs=[wspec] * 4,
        out_shape=[jax.ShapeDtypeStruct(w.shape, F32)] * 4, compiler_params=_cp(("parallel",)), name=name)(w, m, v, *grads)


SMALL_ROWS = 24


def _pack_small(d):
    pieces = [p.reshape(-1) for n in SMALL for p in (d[n] if isinstance(d[n], list) else [d[n]])]
    v = jnp.concatenate(pieces)
    return jnp.pad(v, (0, SMALL_ROWS * 1024 - v.shape[0])).reshape(SMALL_ROWS, 1024)


def _unpack_small(buf, like):
    v = buf.reshape(-1)
    out, off = {}, 0
    for n in SMALL:
        sz = like[n].size
        out[n] = v[off:off + sz].reshape(like[n].shape)
        off += sz
    return out


def kernel(x, c, ctx, c_ctx, ada_w, ada_b, norm1_w, w_in, gdn_conv_w, gdn_conv_b, gdn_a_log, gdn_dt_bias, gdn_norm_w, ssm_conv_w, ssm_conv_b, ssm_a_log, ssm_dt_bias, ssm_d, ssm_norm_w, w_br_gdn, w_br_ssm, w_out, norm2_w, w_ffn_in, w_ffn_out, norm_f_w, loss_target, m_c_ctx, m_ada_w, m_ada_b, m_norm1_w, m_w_in, m_gdn_conv_w, m_gdn_conv_b, m_gdn_a_log, m_gdn_dt_bias, m_gdn_norm_w, m_ssm_conv_w, m_ssm_conv_b, m_ssm_a_log, m_ssm_dt_bias, m_ssm_d, m_ssm_norm_w, m_w_br_gdn, m_w_br_ssm, m_w_out, m_norm2_w, m_w_ffn_in, m_w_ffn_out, m_norm_f_w, v_c_ctx, v_ada_w, v_ada_b, v_norm1_w, v_w_in, v_gdn_conv_w, v_gdn_conv_b, v_gdn_a_log, v_gdn_dt_bias, v_gdn_norm_w, v_ssm_conv_w, v_ssm_conv_b, v_ssm_a_log, v_ssm_dt_bias, v_ssm_d, v_ssm_norm_w, v_w_br_gdn, v_w_br_ssm, v_w_out, v_norm2_w, v_w_ffn_in, v_w_ffn_out, v_norm_f_w):
    args = dict(locals())
    wl = {n: args[n] for n in WEIGHTS}
    ml = {n: args['m_' + n] for n in WEIGHTS}
    vl = {n: args['v_' + n] for n in WEIGHTS}

    def nodepth(n, a):
        return a if n in ('c_ctx', 'norm_f_w') else a[0]

    shard = {n: nodepth(n, wl[n]).astype(_MXU_DTYPE if n in MXU_WEIGHTS else F32) for n in SHARDED}
    first = [n for n in SHARDED if n not in LATE_WEIGHTS and n != 'ada_w']
    W = {n: nodepth(n, wl[n]) for n in SMALL}
    W['ada_w'] = shard['ada_w']
    for n, g in zip(first, _gather_two_level("all_gather_plane", [shard[n] for n in first])):
        W[n] = _from_shards(n, g)

    loss_local, grad_x, G, received = _local_step(x[0], c, ctx[0], loss_target[0], W,
                                                  late_shards={n: shard[n] for n in LATE_WEIGHTS})
    loss = lax.psum(loss_local, ("x", "y", "c"))

    small_g = _pack_small(G)
    last = [n for n in SHARDED if n not in received and n != 'ada_w']
    send = [_to_dest_blocks(n, G[n]) for n in last] + [jnp.broadcast_to(small_g[None], (N_PLANE,) + small_g.shape)]
    received.update(zip(last + ['small'], _plane_exchange("scatter_plane", send, gather=False)))
    names = [n for n in SHARDED if n != 'ada_w'] + ['small']
    plane_sum = [_sum4("sum4_" + n, received[n]) for n in names]
    other = _swap_sibling(plane_sum)

    wd = {n: nodepth(n, wl[n]) for n in WEIGHTS}
    md = {n: nodepth(n, ml[n]) for n in WEIGHTS}
    vd = {n: nodepth(n, vl[n]) for n in WEIGHTS}
    res = {'ada_w': _adamw("adamw_ada_w", wl['ada_w'], ml['ada_w'], vl['ada_w'], G['ada_w'])}
    for n, p, q in zip(names, plane_sum, other):
        if n == 'small':
            outs = _adamw("adamw_small", _pack_small(wd), _pack_small(md), _pack_small(vd), p, q)
            unpacked = [_unpack_small(o, wd) for o in outs]
            for sn in SMALL:
                res[sn] = [u[sn].reshape(wl[sn].shape) for u in unpacked]
        else:
            res[n] = _adamw("adamw_" + n, wl[n], ml[n], vl[n], p, q)
    flat = [res[n][kind] for kind in range(4) for n in WEIGHTS]
    return (loss, grad_x[None], *flat)
```

```python
import functools

import jax
import jax.numpy as jnp
from jax import lax
from jax.experimental import pallas as pl
from jax.experimental.pallas import tpu as pltpu

F32 = jnp.float32
HI = lax.Precision.HIGHEST
_MXU_DTYPE = jnp.bfloat16
_SCAN_DTYPE = jnp.bfloat16
_GRAD_DTYPE = jnp.bfloat16

D_MODEL = 1024
EPS = 1e-6
CHUNK = 64
GRID_W = 64
GDN_HEADS = 8
GDN_DK = 128
SSM_HEADS = 32
SSM_PAIRS = 16
SSD_GROUP = 1
D_FF = 2816
D_IN_PROJ = 11360
N_PLANE = 4
P_QKV, P_ZG, P_GATE, P_ZS, P_XBC, P_SMALL, P_TOTAL = 0, 3072, 4096, 6144, 8192, 11264, 11776
IN_SEGMENTS = [(0, 3072, P_QKV), (3072, 4096, P_ZG), (4096, 4112, P_SMALL), (4112, 4128, P_SMALL + 16),
               (4128, 6176, P_ZS), (6176, 9248, P_XBC), (9248, 9312, P_SMALL + 32), (9312, 11360, P_GATE)]
ADAM_LR, ADAM_B1, ADAM_B2, ADAM_EPS, ADAM_WD, ADAM_STEP = 0.001, 0.9, 0.999, 1e-08, 0.01, 10

VMEM_LIMIT = 48 * 1024 * 1024

WEIGHTS = ['c_ctx', 'ada_w', 'ada_b', 'norm1_w', 'w_in', 'gdn_conv_w', 'gdn_conv_b', 'gdn_a_log', 'gdn_dt_bias',
           'gdn_norm_w', 'ssm_conv_w', 'ssm_conv_b', 'ssm_a_log', 'ssm_dt_bias', 'ssm_d', 'ssm_norm_w', 'w_br_gdn',
           'w_br_ssm', 'w_out', 'norm2_w', 'w_ffn_in', 'w_ffn_out', 'norm_f_w']
SHARD_AXIS = {'ada_w': 1, 'w_in': 1, 'gdn_conv_w': 1, 'ssm_conv_w': 1, 'w_br_gdn': 0, 'w_br_ssm': 0, 'w_out': 0,
              'w_ffn_in': 1, 'w_ffn_out': 0}
SHARDED = [n for n in WEIGHTS if n in SHARD_AXIS]
SMALL = [n for n in WEIGHTS if n not in SHARD_AXIS]
MXU_WEIGHTS = ['ada_w', 'w_in', 'w_br_gdn', 'w_br_ssm', 'w_out', 'w_ffn_in', 'w_ffn_out']


def _cp(sem):
    return pltpu.CompilerParams(dimension_semantics=sem, vmem_limit_bytes=VMEM_LIMIT)


def _pick(n, cands):
    for c in cands:
        if n % c == 0:
            return c
    raise ValueError(f"no tile for {n}")


HBM = pl.BlockSpec(memory_space=pl.ANY)
MESH = pl.DeviceIdType.MESH


def _plane_peers():
    x, y, c = lax.axis_index("x"), lax.axis_index("y"), lax.axis_index("c")
    return (x, y, c), [(1 - x, y, c), (x, 1 - y, c), (1 - x, 1 - y, c)]


class PlaneExchange:
    def __init__(self, arrs, gather):
        self.arrs, self.gather, self.n = list(arrs), gather, len(arrs)
        self.in_specs = [HBM] * self.n
        self.out_specs = [HBM] * self.n
        self.out_shape = [jax.ShapeDtypeStruct((N_PLANE,) + a.shape if gather else a.shape, a.dtype) for a in self.arrs]
        self.scratch = [pltpu.SemaphoreType.DMA((3 * self.n,)), pltpu.SemaphoreType.DMA((3 * self.n,)),
                        pltpu.SemaphoreType.DMA((self.n,))]

    def _copies(self, ins, outs, sems):
        send_sems, recv_sems, local_sems = sems
        (x, y, c), peers = _plane_peers()
        me = 2 * x + y
        copies = []
        for ti in range(self.n):
            src = ins[ti] if self.gather else ins[ti].at[me]
            copies.append(pltpu.make_async_copy(src, outs[ti].at[me], local_sems.at[ti]))
            for kk, (px, py, pc) in enumerate(peers):
                src = ins[ti] if self.gather else ins[ti].at[2 * px + py]
                copies.append(pltpu.make_async_remote_copy(
                    src_ref=src, dst_ref=outs[ti].at[me], send_sem=send_sems.at[3 * ti + kk],
                    recv_sem=recv_sems.at[3 * ti + kk], device_id=(px, py, pc), device_id_type=MESH))
        return copies

    def start(self, ins, outs, sems):
        for cp in self._copies(ins, outs, sems):
            cp.start()

    def wait(self, ins, outs, sems):
        for cp in self._copies(ins, outs, sems):
            cp.wait()


def _plane_exchange(name, arrs, gather):
    ex = PlaneExchange(arrs, gather)
    n = ex.n

    def body(*refs):
        ins, outs, sems = refs[:n], refs[n:2 * n], refs[2 * n:]
        ex.start(ins, outs, sems)
        ex.wait(ins, outs, sems)

    return pl.pallas_call(body, in_specs=ex.in_specs, out_specs=ex.out_specs, out_shape=ex.out_shape,
                          scratch_shapes=ex.scratch, name=name)(*arrs)


def _gather_two_level(name, arrs):
    n = len(arrs)
    split = [a.shape[0] % 32 == 0 for a in arrs]

    def body(*refs):
        ins, outs = refs[:n], refs[n:2 * n]
        ici_send, ici_recv, d2d_send, d2d_recv, local_sems = refs[2 * n:]
        (x, y, c), peers = _plane_peers()
        me = 2 * x + y

        def part(ref, ti):
            if not split[ti]:
                return ref
            rows = arrs[ti].shape[0] // 2
            return ref.at[pl.ds(c * rows, rows)]

        local = [pltpu.make_async_copy(ins[ti], outs[ti].at[me], local_sems.at[ti]) for ti in range(n)]
        ici = [pltpu.make_async_remote_copy(
            src_ref=part(ins[ti], ti), dst_ref=part(outs[ti].at[me], ti), send_sem=ici_send.at[3 * ti + kk],
            recv_sem=ici_recv.at[3 * ti + kk], device_id=peer, device_id_type=MESH)
            for ti in range(n) for kk, peer in enumerate(peers)]
        for cp in local + ici:
            cp.start()
        d2d = []
        for ti in range(n):
            for kk, (px, py, pc) in enumerate(peers):
                ici[3 * ti + kk].wait_recv()
                if split[ti]:
                    piece = part(outs[ti].at[2 * px + py], ti)
                    cp = pltpu.make_async_remote_copy(
                        src_ref=piece, dst_ref=piece, send_sem=d2d_send.at[3 * ti + kk],
                        recv_sem=d2d_recv.at[3 * ti + kk], device_id=(x, y, 1 - c), device_id_type=MESH)
                    cp.start()
                    d2d.append(cp)
        for cp in ici:
            cp.wait_send()
        for cp in d2d:
            cp.wait()
        for cp in local:
            cp.wait()

    return pl.pallas_call(
        body, in_specs=[HBM] * n, out_specs=[HBM] * n,
        out_shape=[jax.ShapeDtypeStruct((N_PLANE,) + a.shape, a.dtype) for a in arrs],
        scratch_shapes=[pltpu.SemaphoreType.DMA((3 * n,))] * 4 + [pltpu.SemaphoreType.DMA((n,))], name=name)(*arrs)


def _gather_all(name, arr):
    flips = [(fx, fy, fc) for fx in (0, 1) for fy in (0, 1) for fc in (0, 1)][1:]

    def body(a_ref, o_ref, send_sems, recv_sems, local_sem):
        x, y, c = lax.axis_index("x"), lax.axis_index("y"), lax.axis_index("c")
        me = 4 * x + 2 * y + c
        copies = [pltpu.make_async_copy(a_ref, o_ref.at[me], local_sem)]
        for kk, (fx, fy, fc) in enumerate(flips):
            peer = (1 - x if fx else x, 1 - y if fy else y, 1 - c if fc else c)
            copies.append(pltpu.make_async_remote_copy(
                src_ref=a_ref, dst_ref=o_ref.at[me], send_sem=send_sems.at[kk], recv_sem=recv_sems.at[kk],
                device_id=peer, device_id_type=MESH))
        for cp in copies:
            cp.start()
        for cp in copies:
            cp.wait()

    return pl.pallas_call(
        body, in_specs=[HBM], out_specs=HBM, out_shape=jax.ShapeDtypeStruct((8,) + arr.shape, arr.dtype),
        scratch_shapes=[pltpu.SemaphoreType.DMA((7,)), pltpu.SemaphoreType.DMA((7,)), pltpu.SemaphoreType.DMA],
        name=name)(arr)


def _swap_sibling(arrs):
    n = len(arrs)

    def body(*refs):
        ins, outs, send_sems, recv_sems = refs[:n], refs[n:2 * n], refs[2 * n], refs[2 * n + 1]
        x, y, c = lax.axis_index("x"), lax.axis_index("y"), lax.axis_index("c")
        copies = [pltpu.make_async_remote_copy(src_ref=ins[ti], dst_ref=outs[ti], send_sem=send_sems.at[ti],
                                               recv_sem=recv_sems.at[ti], device_id=(x, y, 1 - c), device_id_type=MESH)
                  for ti in range(n)]
        for cp in copies:
            cp.start()
        for cp in copies:
            cp.wait()

    return pl.pallas_call(
        body, in_specs=[HBM] * n, out_specs=[HBM] * n, out_shape=[jax.ShapeDtypeStruct(a.shape, a.dtype) for a in arrs],
        scratch_shapes=[pltpu.SemaphoreType.DMA((n,)), pltpu.SemaphoreType.DMA((n,))], name="swap_sibling")(*arrs)


MATMUL_VMEM_BUDGET = 36 * 1024 * 1024
TILE_CANDIDATES = (2944, 2816, 1408, 1024, 768, 512, 256, 128)
STEP_COST_BYTES = 1 << 20


def _matmul_tiles(m, n, k, ab, bb, ob, tn_fixed=None, tk_fixed=None):
    def cands(dim, whole_up_to):
        out = [c for c in TILE_CANDIDATES if dim % c == 0]
        if dim <= whole_up_to and dim not in out:
            out.append(dim)
        return out

    best = None
    for tm in cands(m, 3072):
        for tn in ([tn_fixed] if tn_fixed else cands(n, 3072)):
            for tk in ([tk_fixed] if tk_fixed else cands(k, 2048)):
                gi, gj, gk = m // tm, n // tn, k // tk
                vmem = 2 * (tm * tk * ab + tk * tn * bb + tm * tn * ob) + (tm * tn * 4 if gk > 1 else 0)
                if vmem > MATMUL_VMEM_BUDGET:
                    continue
                a_reads = 1 if gk == 1 else gj
                b_reads = 1 if (gk == 1 and gj == 1) else gi
                cost = (m * k * ab * a_reads + k * n * bb * b_reads + m * n * ob + gi * gj * gk * STEP_COST_BYTES)
                if best is None or cost < best[0]:
                    best = (cost, tm, tn, tk)
    assert best is not None, (m, n, k)
    return best[1:]


def _matmul(a, b, form, name, out_dtype=F32, stacked_out=False, ride=None, halves=None, m_half=None):
    dims = {'nn': (((1,), (0,)), ((), ())), 'nt': (((1,), (1,)), ((), ())), 'tn': (((0,), (0,)), ((), ()))}[form]
    stacked_b = b.ndim == 3 and halves != 'b'
    ns = b.shape[2] if stacked_b else None
    if form == 'nn':
        m, k = a.shape
        n = b.shape[0] * ns if stacked_b else b.shape[1]
    elif form == 'nt':
        m, k = (a.shape[1], 2 * a.shape[2]) if halves == 'a' else a.shape
        n = b.shape[1] if stacked_b else b.shape[0]
    else:
        k, m = a.shape
        m = m // 2 if m_half is not None else m
        n = 2 * b.shape[2] if halves == 'b' else b.shape[1]
    ob = jnp.dtype(out_dtype).itemsize
    tm, tn, tk = _matmul_tiles(m, n, k, a.dtype.itemsize, b.dtype.itemsize, ob,
                               tn_fixed=(n // N_PLANE if stacked_out else ns if (stacked_b and form == 'nn') else None),
                               tk_fixed=(ns if (stacked_b and form == 'nt') else None))
    nk = k // tk
    grid = (m // tm, n // tn, nk)
    n_ride = ride.n if ride is not None else 0

    def body(*refs):
        a_ref, b_ref = refs[0], refs[1]
        ride_in = refs[2:2 + n_ride]
        o_ref = refs[2 + n_ride]
        ride_out = refs[3 + n_ride:3 + 2 * n_ride]
        acc_ref = refs[3 + 2 * n_ride]
        sems = refs[4 + 2 * n_ride:]
        i, j, kk = pl.program_id(0), pl.program_id(1), pl.program_id(2)
        if ride is not None:
            @pl.when((i == 0) & (j == 0) & (kk == 0))
            def _():
                ride.start(ride_in, ride_out, sems)

        def put(val):
            if stacked_out or halves == 'out':
                o_ref[0] = val.astype(o_ref.dtype)
            else:
                o_ref[...] = val.astype(o_ref.dtype)

        av = a_ref[0] if halves == 'a' else a_ref[...]
        bv = b_ref[0] if (stacked_b or halves == 'b') else b_ref[...]
        part = lax.dot_general(av.astype(_MXU_DTYPE), bv.astype(_MXU_DTYPE), dims, preferred_element_type=F32)
        if nk == 1:
            put(part)
        else:
            @pl.when(kk == 0)
            def _():
                acc_ref[...] = part

            @pl.when(kk > 0)
            def _():
                acc_ref[...] += part

            @pl.when(kk == nk - 1)
            def _():
                put(acc_ref[...])

        if ride is not None:
            @pl.when((i == grid[0] - 1) & (j == grid[1] - 1) & (kk == nk - 1))
            def _():
                ride.wait(ride_in, ride_out, sems)

    if form == 'nn':
        a_spec = pl.BlockSpec((tm, tk), lambda i, j, kk: (i, kk))
        b_spec = (pl.BlockSpec((1, tk, tn), lambda i, j, kk: (j, kk, 0)) if stacked_b
                  else pl.BlockSpec((tk, tn), lambda i, j, kk: (kk, j)))
    elif form == 'nt':
        a_spec = pl.BlockSpec((tm, tk), lambda i, j, kk: (i, kk))
        b_spec = (pl.BlockSpec((1, tn, tk), lambda i, j, kk: (kk, j, 0)) if stacked_b
                  else pl.BlockSpec((tn, tk), lambda i, j, kk: (j, kk)))
    else:
        i0 = (m_half or 0) * grid[0]
        a_spec = pl.BlockSpec((tk, tm), lambda i, j, kk: (kk, i + i0))
        b_spec = pl.BlockSpec((tk, tn), lambda i, j, kk: (kk, j))
    hj, hk = grid[1] // 2, nk // 2
    if halves == 'a':
        a_spec = pl.BlockSpec((1, tm, tk), lambda i, j, kk: (kk // hk, i, kk % hk))
    if halves == 'b':
        b_spec = pl.BlockSpec((1, tk, tn), lambda i, j, kk: (j // hj, kk, j % hj))
    if stacked_out:
        o_spec = pl.BlockSpec((1, tm, tn), lambda i, j, kk: (j, i, 0))
        o_shape = jax.ShapeDtypeStruct((N_PLANE, m, tn), out_dtype)
    elif halves == 'out':
        o_spec = pl.BlockSpec((1, tm, tn), lambda i, j, kk: (j // hj, i, j % hj))
        o_shape = jax.ShapeDtypeStruct((2, m, n // 2), out_dtype)
    else:
        o_spec = pl.BlockSpec((tm, tn), lambda i, j, kk: (i, j))
        o_shape = jax.ShapeDtypeStruct((m, n), out_dtype)
    acc_shape = (tm, tn) if nk > 1 else (8, 128)
    if ride is None:
        return pl.pallas_call(
            body, grid=grid, in_specs=[a_spec, b_spec], out_specs=o_spec, out_shape=o_shape,
            scratch_shapes=[pltpu.VMEM(acc_shape, F32)],
            compiler_params=_cp(("parallel", "parallel", "arbitrary")), name=name)(a, b)
    res = pl.pallas_call(
        body, grid=grid, in_specs=[a_spec, b_spec] + ride.in_specs, out_specs=[o_spec] + ride.out_specs,
        out_shape=[o_shape] + ride.out_shape, scratch_shapes=[pltpu.VMEM(acc_shape, F32)] + ride.scratch,
        compiler_params=_cp(("arbitrary", "arbitrary", "arbitrary")), name=name)(a, b, *ride.arrs)
    return res[0], res[1:]


class Op:
    def __init__(self, arr, bs, im, load=None):
        self.arr, self.bs, self.im = arr, bs, im
        self.arrs = list(arr) if isinstance(arr, list) else [arr]
        self.load = load or (lambda r: r[...].astype(F32))

    def spec(self):
        return pl.BlockSpec(self.bs, self.im)

    def value(self, it):
        return functools.reduce(lambda u, w: u + w, [self.load(next(it)) for _ in self.arrs])


def _op_specs(ops):
    return [op.spec() for op in ops for _ in op.arrs]


def _op_arrays(ops):
    return [a for op in ops for a in op.arrs]


def _sum_dirs(r):
    return r[0].astype(F32) + r[1].astype(F32)


def _tw_fwd(name, fn, grid, ins, outs):
    def body(*refs):
        info = (pl.program_id(0), pl.program_id(1))
        it = iter(refs)
        vals = [op.value(it) for op in ins]
        res = fn(info, *vals)
        for r, v in zip(it, res):
            r[...] = v.astype(r.dtype)

    return pl.pallas_call(
        body, grid=grid, in_specs=_op_specs(ins), out_specs=[op.spec() for op in outs],
        out_shape=[jax.ShapeDtypeStruct(*op.arr) for op in outs],
        compiler_params=_cp(("parallel", "arbitrary")), name=name)(*_op_arrays(ins))


def _tw_bwd(name, fn, grid, tok, par, cots, tok_out, par_out, tok_add=None, sem=("parallel", "arbitrary"), into=None):
    n_tok = len(tok)
    flat_cots = [op for group in cots for op in group]
    extra = [tok_add] if tok_add is not None else []
    out_ops = list(tok_out) + list(par_out)
    into = into or {}

    def body(*refs):
        info = (pl.program_id(0), pl.program_id(1))
        it = iter(refs)
        tok_v = [op.value(it) for op in tok]
        par_v = [op.value(it) for op in par]
        cot_v = [functools.reduce(lambda u, w: u + w, [op.value(it) for op in group]) for group in cots]
        add_v = [op.value(it) for op in extra]
        for _ in into:
            next(it)
        _, pull = jax.vjp(lambda *a: fn(info, *a), *tok_v, *par_v)
        grads = pull(tuple(cot_v))
        for i in range(n_tok):
            r = next(it)
            g = grads[i] + add_v[0] if (i == 0 and add_v) else grads[i]
            if r.shape[-1] > g.shape[-1]:
                g = jnp.concatenate([g, jnp.zeros((g.shape[0], r.shape[-1] - g.shape[-1]), g.dtype)], axis=1)
            r[...] = g.astype(r.dtype)
        first = pl.program_id(1) == 0
        for i in range(len(par)):
            r = next(it)
            g = grads[n_tok + i]

            @pl.when(first)
            def _(r=r, g=g):
                r[...] = g

            @pl.when(jnp.logical_not(first))
            def _(r=r, g=g):
                r[...] += g

    ops = tok + par + flat_cots + extra
    n_in = len(_op_arrays(ops))
    return pl.pallas_call(
        body, grid=grid, in_specs=_op_specs(ops) + [HBM] * len(into), out_specs=[op.spec() for op in out_ops],
        out_shape=[jax.ShapeDtypeStruct(*op.arr) for op in out_ops],
        input_output_aliases={n_in + k: i for k, i in enumerate(into)},
        compiler_params=_cp(sem), name=name)(*_op_arrays(ops), *into.values())


def _silu(x):
    return x * jax.nn.sigmoid(x)


def _rms(x):
    return x * lax.rsqrt(jnp.mean(x * x, axis=-1, keepdims=True) + EPS)


@functools.partial(jax.custom_vjp, nondiff_argnums=(1,))
def _shift_rows(x, k):
    return pltpu.roll(x, k % x.shape[0], 0)


def _shift_rows_fwd(x, k):
    return _shift_rows(x, k), None


def _shift_rows_bwd(k, _, g):
    return (_shift_rows(g, -k),)


_shift_rows.defvjp(_shift_rows_fwd, _shift_rows_bwd)


def _prenorm_fn(nctx_t, info, x, w, sc8, sh8):
    is_ctx = info[1] < nctx_t
    sc = jnp.where(is_ctx, sc8[1:2], sc8[0:1])
    sh = jnp.where(is_ctx, sh8[1:2], sh8[0:1])
    return (_rms(x) * w * (1.0 + sc) + sh,)


def _conv_fn(lc, mode, info, x, w, b):
    n, c = x.shape
    g = info[1] * n + lax.broadcasted_iota(jnp.int32, (n, 1), 0)
    is_ctx = g < lc
    rr = jnp.where(is_ctx, g, g % GRID_W)
    first = rr == 0
    last = rr == jnp.where(is_ctx, lc - 1, GRID_W - 1)
    prev = jnp.where(first, 0.0, _shift_rows(x, 1))
    nxt = jnp.where(last, 0.0, _shift_rows(x, -1))
    y = b + prev * w[0:1] + x * w[1:2] + nxt * w[2:3]
    y = _silu(y)
    if mode == 'none':
        return (y,)
    scale = GDN_DK ** -0.5 if mode == 'q' else 1.0
    outs = []
    for h in range(c // 128):
        yh = y[:, h * 128:(h + 1) * 128]
        outs.append(yh * lax.rsqrt(jnp.sum(yh * yh, axis=-1, keepdims=True) + EPS) * scale)
    return (jnp.concatenate(outs, axis=1),)


def _act_fn(info, x, p0, p1):
    lane = lax.broadcasted_iota(jnp.int32, x.shape, 1)
    sp = jax.nn.softplus(x + p1)
    g = -jnp.exp(p0) * sp
    bt = jax.nn.sigmoid(x)
    return (jnp.where(lane < 16, g, jnp.where(lane < 32, bt, jnp.where(lane < 96, sp, 0.0))),)


def _mixg_fn(info, o, zg, gnw):
    outs = []
    for h in range(GDN_HEADS):
        outs.append(_rms(o[:, h * 128:(h + 1) * 128]) * gnw)
    return (jnp.concatenate(outs, axis=1) * _silu(zg),)


def _mixs_fn(info, y, xs, zs, dl, snw):
    yy = (y + dl * xs) * _silu(zs)
    outs = []
    for g in range(4):
        outs.append(_rms(yy[:, g * 512:(g + 1) * 512]))
    return (jnp.concatenate(outs, axis=1) * snw,)


def _merge_fn(info, gates, pg, ps):
    half = gates.shape[1] // 2
    return (jax.nn.sigmoid(gates[:, :half]) * pg + jax.nn.sigmoid(gates[:, half:]) * ps,)


def _norm2_fn(info, xt, mo, g8, w, sc8, sh8):
    h1 = xt + g8[0:1] * mo
    return (h1, _rms(h1) * w * (1.0 + sc8[0:1]) + sh8[0:1])


def _swiglu_fn(info, u2):
    return (_silu(u2[0]) * u2[1],)


_NN = (((1,), (0,)), ((), ()))
_NT = (((1,), (1,)), ((), ()))
_TN = (((0,), (0,)), ((), ()))


def _mmh(a, b):
    return lax.dot_general(a, b, _NN, precision=HI, preferred_element_type=F32)


def _dot1(a, b, dims):
    return lax.dot_general(a.astype(_SCAN_DTYPE), b.astype(_SCAN_DTYPE), dims, preferred_element_type=F32)


def _mm(a, b):
    return _dot1(a, b, _NN)


def _mm_nt(a, b):
    return _dot1(a, b, _NT)


def _mm_tn(a, b):
    return _dot1(a, b, _TN)


def _split2(a):
    hi = a.astype(_SCAN_DTYPE)
    return hi, (a - hi.astype(F32)).astype(_SCAN_DTYPE)


def _dot3(a, b, dims):
    ah, al = _split2(a)
    bh, bl = _split2(b)
    d = lambda u, w: lax.dot_general(u, w, dims, preferred_element_type=F32)
    return d(ah, bh) + (d(ah, bl) + d(al, bh))


def _order_masks(d):
    i = lax.broadcasted_iota(jnp.int32, (CHUNK, CHUNK), 0)
    j = lax.broadcasted_iota(jnp.int32, (CHUNK, CHUNK), 1)
    s = jnp.where(d == 0, 1, -1) * (i - j)
    return (s >= 0).astype(F32), (s > 0).astype(F32)


@jax.custom_vjp
def _unit_tri_inv(mats):
    i = lax.broadcasted_iota(jnp.int32, (CHUNK, CHUNK), 0)
    j = lax.broadcasted_iota(jnp.int32, (CHUNK, CHUNK), 1)
    eye = (i == j).astype(F32)
    ps = [-a for a in mats]
    ts = [eye + p for p in ps]
    for _ in range(5):
        ps = [_dot3(p, p, _NN) for p in ps]
        ts = [t + _dot3(t, p, _NN) for t, p in zip(ts, ps)]
    return tuple(ts)


def _uti_fwd(mats):
    ts = _unit_tri_inv(mats)
    return ts, ts


def _uti_bwd(ts, gs):
    inner = [_dot3(g, t, _NT) for g, t in zip(gs, ts)]
    return (tuple(-_dot3(t, u, _TN) for t, u in zip(ts, inner)),)


_unit_tri_inv.defvjp(_uti_fwd, _uti_bwd)


@jax.custom_vjp
def _unit_tri_inv_given(mats, ts):
    return ts


def _utig_fwd(mats, ts):
    return ts, ts


def _utig_bwd(ts, gs):
    return _uti_bwd(ts, gs)[0], tuple(jnp.zeros_like(t) for t in ts)


_unit_tri_inv_given.defvjp(_utig_fwd, _utig_bwd)


def _lane_col(blk, lane_idx):
    lane = lax.broadcasted_iota(jnp.int32, blk.shape, 1)
    return jnp.sum(jnp.where(lane == lane_idx, blk, 0.0), axis=1, keepdims=True)


def _decay_mat(cum, incl):
    cb = jnp.broadcast_to(cum, (CHUNK, CHUNK))
    return jnp.exp(jnp.minimum(cb - cb.T, 0.0)) * incl


def _gdn_chunks(dirs, streams, states, aux=None, want_aux=False, group=None):
    ns = len(dirs)
    masks = [_order_masks(d) for d in dirs]
    cum = [_mmh(masks[i][0], streams[i][3]) for i in range(ns)]
    tot = [jnp.sum(streams[i][3], axis=0, keepdims=True) for i in range(ns)]
    all_units = [(i, h) for i in range(ns) for h in range(GDN_HEADS)]
    group = group or len(all_units)
    cat = jnp.concatenate
    outs, new_states, ts_all = [], [], []
    for g0 in range(0, len(all_units), group):
        units = all_units[g0:g0 + group]
        us = range(len(units))
        st = states[g0:g0 + group]
        qs = [streams[i][0][:, h * 128:(h + 1) * 128] for i, h in units]
        ks = [streams[i][1][:, h * 128:(h + 1) * 128] for i, h in units]
        vs = [streams[i][2][:, h * 128:(h + 1) * 128] for i, h in units]
        gcum = [_lane_col(cum[i], dirs[i] * GDN_HEADS + h) for i, h in units]
        glast = [_lane_col(tot[i], dirs[i] * GDN_HEADS + h) for i, h in units]
        beta = [_lane_col(streams[i][3], 16 + dirs[i] * GDN_HEADS + h) for i, h in units]
        decay = [_decay_mat(gcum[u], masks[units[u][0]][0]) for u in us]
        egc = [jnp.exp(gcum[u]) for u in us]
        kb = [ks[u] * beta[u] for u in us]
        kq = [_mm_nt(cat([kb[u], qs[u]], axis=0), ks[u]) for u in us]
        mats = tuple(kq[u][:CHUNK] * decay[u] * masks[units[u][0]][1] for u in us)
        ts = _unit_tri_inv(mats) if aux is None else _unit_tri_inv_given(mats, tuple(aux[g0:g0 + group]))
        wu = [_mm(ts[u], cat([kb[u] * egc[u], vs[u] * beta[u]], axis=1)) for u in us]
        ws = [_mm(cat([wu[u][:, :128], qs[u] * egc[u]], axis=0), st[u]) for u in us]
        vn = [wu[u][:, 128:] - ws[u][:CHUNK] for u in us]
        outs += [ws[u][CHUNK:] + _mm(kq[u][CHUNK:] * decay[u], vn[u]) for u in us]
        new_states += [st[u] * jnp.exp(glast[u]) + _mm_tn(ks[u] * jnp.exp(glast[u] - gcum[u]), vn[u]) for u in us]
        ts_all += list(ts)
    per_stream = [cat(outs[i * GDN_HEADS:(i + 1) * GDN_HEADS], axis=1) for i in range(ns)]
    return (*per_stream, *new_states, *(ts_all if want_aux else ()))


def _gdn_step(d, q, k, v, spb, *states, aux=None, want_aux=False):
    return _gdn_chunks([d], [(q, k, v, spb)], states, aux, want_aux, group=4)


def _gdn_step2(q0, k0, v0, sp0, q1, k1, v1, sp1, *states, aux=None, want_aux=False):
    return _gdn_chunks([0, 1], [(q0, k0, v0, sp0), (q1, k1, v1, sp1)], states, aux, want_aux)


def _split3(a):
    a1 = a.astype(_SCAN_DTYPE)
    r = a - a1.astype(F32)
    a2 = r.astype(_SCAN_DTYPE)
    return a1, a2, (r - a2.astype(F32)).astype(_SCAN_DTYPE)


def _exact_dot(a, e, dims, split_lhs, passes=3):
    parts = _split3(a if split_lhs else e)[:passes]
    d = (lambda u: lax.dot_general(u, e, dims, preferred_element_type=F32)) if split_lhs else \
        (lambda u: lax.dot_general(a, u, dims, preferred_element_type=F32))
    return functools.reduce(lambda u, w: u + w, [d(p) for p in reversed(parts)])


@jax.custom_vjp
def _spread(a, e):
    return _exact_dot(a, e, _NN, True)


def _spread_fwd(a, e):
    return _spread(a, e), e


def _spread_bwd(e, g):
    return _exact_dot(g, e, _NT, True, passes=2), jnp.zeros_like(e)


_spread.defvjp(_spread_fwd, _spread_bwd)


@jax.custom_vjp
def _colsum_bcast(z):
    return _exact_dot(jnp.ones((z.shape[0], z.shape[0]), _SCAN_DTYPE), z, _NN, False)


def _colsum_fwd(z):
    return _colsum_bcast(z), None


def _colsum_bwd(_, g):
    return (_exact_dot(jnp.ones((g.shape[0], g.shape[0]), _SCAN_DTYPE), g, _NN, False, passes=2),)


_colsum_bcast.defvjp(_colsum_fwd, _colsum_bwd)


def _ssd_consts():
    wdt = SSM_HEADS * 64
    d = lax.broadcasted_iota(jnp.int32, (2, 128, wdt), 0)
    e = (lax.broadcasted_iota(jnp.int32, (2, 128, wdt), 1)
         == 32 + d * SSM_HEADS + lax.broadcasted_iota(jnp.int32, (2, 128, wdt), 2) // 64).astype(_SCAN_DTYPE)
    dd = lax.broadcasted_iota(jnp.int32, (2, CHUNK, wdt), 0)
    ci = lax.broadcasted_iota(jnp.int32, (2, CHUNK, wdt), 1)
    pos = lax.broadcasted_iota(jnp.int32, (2, CHUNK, wdt), 2) % 64
    incl_t = (jnp.where(dd == 0, 1, -1) * (ci - pos) >= 0).astype(F32)
    diag = (ci == pos).astype(F32)
    return [e, incl_t, diag]


def _ssd_step(d, x, bm, cm, spb, alog, *states, consts):
    e, incl_t, diag = consts
    incl, _ = _order_masks(d)
    lane1 = lax.broadcasted_iota(jnp.int32, (1, 128), 1)
    lo_lane = 32 + d * SSM_HEADS
    a_vec = jnp.where(lane1 >= lo_lane, jnp.where(lane1 < lo_lane + SSM_HEADS, -jnp.exp(alog), 0.0), 0.0)
    adt = spb * a_vec
    acum = _mmh(incl, adt)
    alast = jnp.sum(adt, axis=0, keepdims=True)
    dt2 = _spread(spb, e)
    ac2 = _spread(acum, e)
    al2 = _spread(jnp.broadcast_to(alast, (8, 128)), e)[0:1]
    row = _colsum_bcast(ac2 * diag)
    seg = jnp.exp(jnp.minimum(ac2 - row, 0.0)) * incl_t
    xdt = x * dt2
    gam = jnp.exp(ac2)
    xe = xdt * jnp.exp(al2 - ac2)
    low = lax.broadcasted_iota(jnp.int32, (CHUNK, 128), 1) < 64
    row_low = lax.broadcasted_iota(jnp.int32, (128, 1), 0) < 64
    ps = range(SSM_PAIRS)
    sl = [slice(p * 128, (p + 1) * 128) for p in ps]
    bg = [bm[:, g * 128:(g + 1) * 128] for g in range(4)]
    cg = [cm[:, g * 128:(g + 1) * 128] for g in range(4)]
    cb2 = [_mm_nt(cg[g], jnp.concatenate([bg[g], bg[g]], axis=0)) for g in range(4)]
    ys, new_states = [], []
    for p0 in range(0, SSM_PAIRS, SSD_GROUP):
        pg = range(p0, p0 + SSD_GROUP)
        xd = {p: jnp.concatenate([jnp.where(low, xdt[:, sl[p]], 0.0), jnp.where(low, 0.0, xdt[:, sl[p]])], axis=0)
              for p in pg}
        yd = {p: _mm(cb2[p // 4] * seg[:, sl[p]], xd[p]) for p in pg}
        yo = {p: _mm_nt(cg[p // 4], states[p]) for p in pg}
        ys += [yd[p] + gam[:, sl[p]] * yo[p] for p in pg]
        new = {p: _mm_tn(xe[:, sl[p]], bg[p // 4]) for p in pg}
        al0 = {p: _lane_col(alast, lo_lane + 2 * p) for p in pg}
        al1 = {p: _lane_col(alast, lo_lane + 2 * p + 1) for p in pg}
        new_states += [states[p] * jnp.exp(jnp.where(row_low, al0[p], al1[p])) + new[p] for p in pg]
    return (jnp.concatenate(ys, axis=1), *new_states)


def _chunk_of(d, p, nctx, nc):
    return jnp.where(d == 0, p, jnp.where(p < nctx, nctx - 1 - p, nctx + nc - 1 - p))


class _NoRide:
    n, arrs, in_specs, out_specs, out_shape, scratch = 0, [], [], [], [], []


def _const_specs(consts):
    return [pl.BlockSpec((1,) + a.shape[1:], lambda d, p: (d,) + (0,) * (a.ndim - 1)) for a in consts]


def _scan_fwd(name, step, toks, pars, consts, out_width, n_state, nctx):
    t = toks[0].shape[0]
    nc = t // CHUNK
    n_tok, n_par, n_const = len(toks), len(pars), len(consts)

    def body(*refs):
        it = iter(refs)
        tok_refs = [next(it) for _ in range(n_tok)]
        par_refs = [next(it) for _ in range(n_par)]
        const_refs = [next(it) for _ in range(n_const)]
        o_ref, ss_ref, s_scr = next(it), next(it), next(it)
        d, p = pl.program_id(0), pl.program_id(1)

        @pl.when(p == 0)
        def _():
            s_scr[...] = jnp.zeros(s_scr.shape, F32)

        ss_ref[0, 0] = s_scr[...]
        res = step(d, *[r[...] for r in tok_refs], *[r[...] for r in par_refs], *[s_scr[h] for h in range(n_state)],
                   consts=[r[0] for r in const_refs])
        o_ref[0] = res[0]
        for h in range(n_state):
            s_scr[h] = res[1 + h]

    ch = lambda d, p: _chunk_of(d, p, nctx, nc)
    in_specs = [pl.BlockSpec((CHUNK, a.shape[1]), lambda d, p: (ch(d, p), 0)) for a in toks]
    in_specs += [pl.BlockSpec(a.shape, lambda d, p: (0, 0)) for a in pars]
    return pl.pallas_call(
        body, grid=(2, nc), in_specs=in_specs + _const_specs(consts),
        out_specs=[pl.BlockSpec((1, CHUNK, out_width), lambda d, p: (d, ch(d, p), 0)),
                   pl.BlockSpec((1, 1, n_state, 128, 128), lambda d, p: (d, p, 0, 0, 0))],
        out_shape=[jax.ShapeDtypeStruct((2, t, out_width), F32),
                   jax.ShapeDtypeStruct((2, nc, n_state, 128, 128), F32)],
        scratch_shapes=[pltpu.VMEM((n_state, 128, 128), F32)],
        compiler_params=_cp(("arbitrary", "arbitrary")), name=name)(*toks, *pars, *consts)


def _scan_bwd(name, step, toks, pars, ss, dout, n_state, nctx, ride=None, aux=None, consts=()):
    t = toks[0].shape[0]
    nc = t // CHUNK
    n_tok, n_par, n_const = len(toks), len(pars), len(consts)
    rd = ride if ride is not None else _NoRide
    n_aux = aux.shape[2] if aux is not None else 0

    def body(*refs):
        it = iter(refs)
        tok_refs = [next(it) for _ in range(n_tok)]
        par_refs = [next(it) for _ in range(n_par)]
        ss_ref, do_ref = next(it), next(it)
        aux_ref = next(it) if n_aux else None
        const_refs = [next(it) for _ in range(n_const)]
        ride_in = [next(it) for _ in range(rd.n)]
        dtok_refs = [next(it) for _ in range(n_tok)]
        dpar_refs = [next(it) for _ in range(n_par)]
        ride_out = [next(it) for _ in range(rd.n)]
        ds_scr = next(it)
        sems = list(it)
        d, pr = pl.program_id(0), pl.program_id(1)
        if ride is not None:
            @pl.when((d == 0) & (pr == 0))
            def _():
                ride.start(ride_in, ride_out, sems)

            @pl.when((d == 1) & (pr == nc - 1))
            def _():
                ride.wait(ride_in, ride_out, sems)

        @pl.when(pr == 0)
        def _():
            ds_scr[...] = jnp.zeros(ds_scr.shape, F32)

        kw = dict(aux=[aux_ref[0, 0, i] for i in range(n_aux)]) if n_aux else {}
        if n_const:
            kw['consts'] = [r[0] for r in const_refs]
        _, pull = jax.vjp(functools.partial(step, d, **kw), *[r[...] for r in tok_refs], *[r[...] for r in par_refs],
                          *[ss_ref[0, 0, h] for h in range(n_state)])
        grads = pull((do_ref[...], *[ds_scr[h] for h in range(n_state)]))
        for r, g in zip(dtok_refs, grads[:n_tok]):
            r[0] = g
        for h in range(n_state):
            ds_scr[h] = grads[n_tok + n_par + h]
        first = (d == 0) & (pr == 0)
        for r, g in zip(dpar_refs, grads[n_tok:n_tok + n_par]):
            @pl.when(first)
            def _(r=r, g=g):
                r[...] = g

            @pl.when(jnp.logical_not(first))
            def _(r=r, g=g):
                r[...] += g

    ch = lambda d, pr: _chunk_of(d, nc - 1 - pr, nctx, nc)
    in_specs = [pl.BlockSpec((CHUNK, a.shape[1]), lambda d, pr: (ch(d, pr), 0)) for a in toks]
    in_specs += [pl.BlockSpec(a.shape, lambda d, pr: (0, 0)) for a in pars]
    in_specs += [pl.BlockSpec((1, 1, n_state, 128, 128), lambda d, pr: (d, nc - 1 - pr, 0, 0, 0)),
                 pl.BlockSpec((CHUNK, dout.shape[1]), lambda d, pr: (ch(d, pr), 0))]
    if n_aux:
        in_specs += [pl.BlockSpec((1, 1, n_aux, CHUNK, CHUNK), lambda d, pr: (d, nc - 1 - pr, 0, 0, 0))]
    in_specs += _const_specs(consts)
    out_specs = [pl.BlockSpec((1, CHUNK, a.shape[1]), lambda d, pr: (d, ch(d, pr), 0)) for a in toks]
    out_specs += [pl.BlockSpec(a.shape, lambda d, pr: (0, 0)) for a in pars]
    out_shape = [jax.ShapeDtypeStruct((2, t, a.shape[1]), F32) for a in toks]
    out_shape += [jax.ShapeDtypeStruct(a.shape, F32) for a in pars]
    return pl.pallas_call(
        body, grid=(2, nc), in_specs=in_specs + rd.in_specs, out_specs=out_specs + rd.out_specs,
        out_shape=out_shape + rd.out_shape, scratch_shapes=[pltpu.VMEM((n_state, 128, 128), F32)] + rd.scratch,
        compiler_params=_cp(("arbitrary", "arbitrary")), name=name)(
            *toks, *pars, ss, dout, *([aux] if n_aux else []), *consts, *rd.arrs)


def _scan2_fwd(name, step2, toks, out_width, n_state, nctx, n_aux, ride=None):
    t = toks[0].shape[0]
    nc = t // CHUNK
    n_tok = len(toks)
    rd = ride if ride is not None else _NoRide

    def body(*refs):
        it = iter(refs)
        tok_refs = [next(it) for _ in range(2 * n_tok)]
        ride_in = [next(it) for _ in range(rd.n)]
        o_refs = [next(it), next(it)]
        ss_ref, aux_ref = next(it), next(it)
        ride_out = [next(it) for _ in range(rd.n)]
        s_scr = next(it)
        sems = list(it)
        p = pl.program_id(0)
        if ride is not None:
            @pl.when(p == 0)
            def _():
                ride.start(ride_in, ride_out, sems)

        @pl.when(p == 0)
        def _():
            s_scr[...] = jnp.zeros(s_scr.shape, F32)

        for d in range(2):
            ss_ref[d, 0] = s_scr[d * n_state:(d + 1) * n_state]
        res = step2(*[r[...] for r in tok_refs], *[s_scr[u] for u in range(2 * n_state)], want_aux=True)
        for d in range(2):
            o_refs[d][...] = res[d]
            for i in range(n_aux):
                aux_ref[d, 0, i] = res[2 + 2 * n_state + d * n_aux + i]
        for u in range(2 * n_state):
            s_scr[u] = res[2 + u]
        if ride is not None:
            @pl.when(p == nc - 1)
            def _():
                ride.wait(ride_in, ride_out, sems)

    def tok_spec(a, d):
        return pl.BlockSpec((CHUNK, a.shape[1]), lambda p: (_chunk_of(d, p, nctx, nc), 0))

    return pl.pallas_call(
        body, grid=(nc,), in_specs=[tok_spec(a, d) for d in range(2) for a in toks] + rd.in_specs,
        out_specs=[pl.BlockSpec((CHUNK, out_width), lambda p: (_chunk_of(0, p, nctx, nc), 0)),
                   pl.BlockSpec((CHUNK, out_width), lambda p: (_chunk_of(1, p, nctx, nc), 0)),
                   pl.BlockSpec((2, 1, n_state, 128, 128), lambda p: (0, p, 0, 0, 0)),
                   pl.BlockSpec((2, 1, n_aux, CHUNK, CHUNK), lambda p: (0, p, 0, 0, 0))] + rd.out_specs,
        out_shape=[jax.ShapeDtypeStruct((t, out_width), F32), jax.ShapeDtypeStruct((t, out_width), F32),
                   jax.ShapeDtypeStruct((2, nc, n_state, 128, 128), F32),
                   jax.ShapeDtypeStruct((2, nc, n_aux, CHUNK, CHUNK), F32)] + rd.out_shape,
        scratch_shapes=[pltpu.VMEM((2 * n_state, 128, 128), F32)] + rd.scratch,
        compiler_params=_cp(("arbitrary",)), name=name)(*toks, *toks, *rd.arrs)


ADA_TN = 512


def _ada_part(cc16, w_shard):
    n = w_shard.shape[1]

    def body(c_ref, w_ref, o_ref):
        s = _silu(c_ref[...]).astype(_MXU_DTYPE)
        o_ref[...] = jnp.dot(s, w_ref[...].astype(_MXU_DTYPE), preferred_element_type=F32)

    return pl.pallas_call(
        body, grid=(n // ADA_TN,),
        in_specs=[pl.BlockSpec((16, D_MODEL), lambda j: (0, 0)), pl.BlockSpec((D_MODEL, ADA_TN), lambda j: (0, j))],
        out_specs=pl.BlockSpec((16, ADA_TN), lambda j: (0, j)), out_shape=jax.ShapeDtypeStruct((16, n), F32),
        compiler_params=_cp(("parallel",)), name="ada_part")(cc16, w_shard)


def _ada_bwd_shard(cc16, w_shard, d_lat, d_ctx):
    n = w_shard.shape[1]
    nj = n // ADA_TN

    def body(c_ref, w_ref, gl_ref, gc_ref, dw_ref, dc_ref):
        j = pl.program_id(0)
        s, pull = jax.vjp(_silu, c_ref[...])
        row = lax.broadcasted_iota(jnp.int32, (8, 1), 0)
        dctx = jnp.where(row == 0, jnp.sum(gc_ref[...], axis=0, keepdims=True), 0.0)
        mxu = lambda v: v.astype(_MXU_DTYPE)
        dw_ref[...] = (lax.dot_general(mxu(s[0:8]), mxu(gl_ref[...]), _TN, preferred_element_type=F32)
                       + lax.dot_general(mxu(s[8:16]), mxu(dctx), _TN, preferred_element_type=F32))
        ds = lax.dot_general(mxu(dctx), mxu(w_ref[...]), _NT, preferred_element_type=F32)

        @pl.when(j == 0)
        def _():
            dc_ref[...] = ds

        @pl.when(j > 0)
        def _():
            dc_ref[...] += ds

        @pl.when(j == nj - 1)
        def _():
            ct = jnp.concatenate([jnp.zeros((8, D_MODEL), F32), dc_ref[...]], axis=0)
            dc_ref[...] = 0.5 * pull(ct)[0][8:16]

    tile = pl.BlockSpec((8, ADA_TN), lambda j: (0, j))
    wspec = pl.BlockSpec((D_MODEL, ADA_TN), lambda j: (0, j))
    return pl.pallas_call(
        body, grid=(nj,), in_specs=[pl.BlockSpec((16, D_MODEL), lambda j: (0, 0)), wspec, tile, tile],
        out_specs=[wspec, pl.BlockSpec((8, D_MODEL), lambda j: (0, 0))],
        out_shape=[jax.ShapeDtypeStruct((D_MODEL, n), F32), jax.ShapeDtypeStruct((8, D_MODEL), F32)],
        compiler_params=_cp(("arbitrary",)), name="ada_bwd_shard")(cc16, w_shard, d_lat, d_ctx)


def _rowsum2(a, b):
    def body(a_ref, b_ref, o_ref):
        o_ref[...] = jnp.sum(a_ref[...], axis=0, keepdims=True) + jnp.sum(b_ref[...], axis=0, keepdims=True)

    return pl.pallas_call(body, out_shape=jax.ShapeDtypeStruct((1, a.shape[1]), F32), name="ada_b_grad")(a, b)


def _tail(h1, ff, mods, wf, tgt, nctx_t, tl):
    t = h1.shape[0]
    nt = t // tl

    def loss_fn(valid, h1v, ffv, g8, w, tg):
        h2 = h1v + g8[0:1] * ffv
        y = _rms(h2) * w
        err = (y - tg) ** 2
        return 0.5 * jnp.sum(jnp.mean(err, axis=-1, keepdims=True), axis=0, keepdims=True) * valid

    def body(h1_ref, ff_ref, g_ref, w_ref, t_ref, loss_ref, dh_ref, dff_ref, dg_ref, dw_ref):
        i = pl.program_id(0)
        valid = jnp.where(i < nctx_t, 0.0, 1.0)
        lv, pull = jax.vjp(functools.partial(loss_fn, valid), h1_ref[...], ff_ref[...].astype(F32), g_ref[...],
                           w_ref[...], t_ref[...])
        dh, dff, dg, dw, _ = pull(jnp.ones((1, 1), F32))
        dh_ref[...] = dh
        dff_ref[...] = dff.astype(dff_ref.dtype)
        lb = jnp.broadcast_to(lv, loss_ref.shape)

        @pl.when(i == 0)
        def _():
            loss_ref[...] = lb
            dg_ref[...] = dg
            dw_ref[...] = dw

        @pl.when(i > 0)
        def _():
            loss_ref[...] += lb
            dg_ref[...] += dg
            dw_ref[...] += dw

    tok = pl.BlockSpec((tl, D_MODEL), lambda i: (i, 0))
    return pl.pallas_call(
        body, grid=(nt,),
        in_specs=[tok, tok, pl.BlockSpec((8, D_MODEL), lambda i: (0, 5)), pl.BlockSpec((1, D_MODEL), lambda i: (0, 0)),
                  pl.BlockSpec((tl, D_MODEL), lambda i: (jnp.maximum(i - nctx_t, 0), 0))],
        out_specs=[pl.BlockSpec((8, 128), lambda i: (0, 0)), tok, tok, pl.BlockSpec((8, D_MODEL), lambda i: (0, 0)),
                   pl.BlockSpec((1, D_MODEL), lambda i: (0, 0))],
        out_shape=[jax.ShapeDtypeStruct((8, 128), F32), jax.ShapeDtypeStruct((t, D_MODEL), F32),
                   jax.ShapeDtypeStruct((t, D_MODEL), _MXU_DTYPE), jax.ShapeDtypeStruct((8, D_MODEL), F32),
                   jax.ShapeDtypeStruct((1, D_MODEL), F32)],
        compiler_params=_cp(("arbitrary",)), name="tail_loss")(h1, ff, mods, wf, tgt)


def _pack_w_in(w4):
    ns = w4.shape[2]
    placed = []
    for s0, s1, p0 in IN_SEGMENTS:
        for j in range(N_PLANE):
            lo, hi = max(s0, j * ns), min(s1, (j + 1) * ns)
            if lo < hi:
                placed.append((p0 + lo - s0, w4[j][:, lo - j * ns:hi - j * ns]))
    placed.sort(key=lambda e: e[0])
    pieces, end = [], 0
    for pos, piece in placed:
        assert pos == end, (pos, end)
        pieces.append(piece)
        end = pos + piece.shape[1]
    pieces.append(jnp.zeros((w4.shape[1], P_TOTAL - end), w4.dtype))
    return jnp.concatenate(pieces, axis=1)


def _unpack_w_in(g):
    ns = D_IN_PROJ // N_PLANE
    shards = []
    for j in range(N_PLANE):
        pieces = []
        for s0, s1, p0 in IN_SEGMENTS:
            lo, hi = max(s0, j * ns), min(s1, (j + 1) * ns)
            if lo < hi:
                pieces.append(g[:, p0 + lo - s0:p0 + hi - s0])
        shards.append(jnp.concatenate(pieces, axis=1))
    return jnp.stack(shards)


LATE_WEIGHTS = ['w_br_gdn', 'w_br_ssm', 'w_out', 'w_ffn_in', 'w_ffn_out']
COL_STACKED = ('w_in', 'w_ffn_in')


def _from_shards(n, g):
    if n in COL_STACKED:
        return g
    if SHARD_AXIS[n] == 0:
        return g.reshape(N_PLANE * g.shape[1], g.shape[2])
    return jnp.concatenate([g[j] for j in range(N_PLANE)], axis=1)


def _to_dest_blocks(n, g):
    if g.ndim == 3:
        return g
    if SHARD_AXIS[n] == 0:
        return g.reshape(N_PLANE, g.shape[0] // N_PLANE, g.shape[1])
    sz = g.shape[1] // N_PLANE
    return jnp.stack([g[:, j * sz:(j + 1) * sz] for j in range(N_PLANE)])


def _local_step(x, c, ctx, tgt, W, late_shards):
    lc, l = ctx.shape[0], x.shape[0]
    t = lc + l
    tl = 256
    assert lc == tl and l % tl == 0 and lc % CHUNK == 0
    nt, nctx_t, nctx = t // tl, lc // tl, lc // CHUNK
    act = _MXU_DTYPE
    r1 = lambda v: v.reshape(1, -1)

    xt = jnp.concatenate([ctx, x], axis=0)
    lin = 4 * lax.axis_index("x") + 2 * lax.axis_index("y") + lax.axis_index("c")
    c_all = _gather_all("gather_c", jnp.broadcast_to(c, (8, D_MODEL)))[:, 0]
    cc16 = jnp.concatenate([c_all, r1(W['c_ctx']), jnp.zeros((7, D_MODEL), F32)], axis=0)
    (parts,) = _plane_exchange("gather_mods", [_ada_part(cc16, W['ada_w'])], gather=True)
    mods_all = jnp.transpose(parts, (1, 0, 2)).reshape(16, -1) + r1(W['ada_b'])
    mods = jnp.concatenate([lax.dynamic_slice_in_dim(mods_all, lin, 1, axis=0), mods_all[8:9],
                            jnp.zeros((6, mods_all.shape[1]), F32)], axis=0)

    def mod(kk):
        return Op(mods, (8, D_MODEL), lambda j, i, kk=kk: (0, kk))

    def tokop(arr, w=D_MODEL, off=0, tl_=tl):
        if not isinstance(arr, list) and arr.ndim == 3:
            return Op(arr, (2, tl_, w), lambda j, i: (0, i, off + j), load=_sum_dirs)
        return Op(arr, (tl_, w), lambda j, i: (i, off + j))

    def outop(n, dtype, w=D_MODEL, tl_=tl):
        return Op(((t, n), dtype), (tl_, w), lambda j, i: (i, j))

    def parop(arr, w, off=0):
        return Op(arr, (arr.shape[0], w), lambda j, i: (0, off + j))

    def parout(rows, n, w):
        return Op(((rows, n), F32), (rows, w), lambda j, i: (0, j))

    n1w = r1(W['norm1_w'])
    pre_fn = functools.partial(_prenorm_fn, nctx_t)
    pre_tok, pre_par = [tokop(xt)], [parop(n1w, D_MODEL), mod(1), mod(0)]
    (a,) = _tw_fwd("prenorm_fwd", pre_fn, (1, nt), pre_tok + pre_par, [outop(D_MODEL, act)])
    wp = _pack_w_in(W['w_in'])
    proj = _matmul(a, wp, 'nn', "in_proj")

    gcw, gcb = W['gdn_conv_w'], r1(W['gdn_conv_b'])
    scw, scb = W['ssm_conv_w'], r1(W['ssm_conv_b'])
    conv_parts = {}

    tlf = 768 if t % 768 == 0 else tl

    def conv_part(name, mode, poff, cw, cb, woff, width):
        fn = functools.partial(_conv_fn, lc, mode)
        bw = min(width, 1024)
        par_ = [parop(cw, bw, woff // bw), parop(cb, bw, woff // bw)]
        conv_parts[name] = (fn, [tokop(proj, bw, poff // bw)], par_, width, bw, poff)
        (o,) = _tw_fwd("conv_" + name, fn, (width // bw, t // tlf), [tokop(proj, bw, poff // bw, tlf)] + par_,
                       [outop(width, F32, bw, tlf)])
        return o

    q = conv_part('q', 'q', P_QKV, gcw, gcb, 0, 1024)
    k = conv_part('k', 'k', P_QKV + 1024, gcw, gcb, 1024, 1024)
    v = conv_part('v', 'none', P_QKV + 2048, gcw, gcb, 2048, 1024)
    xs = conv_part('xs', 'none', P_XBC, scw, scb, 0, 2048)
    bm = conv_part('bm', 'none', P_XBC + 2048, scw, scb, 2048, 512)
    cm = conv_part('cm', 'none', P_XBC + 2560, scw, scb, 2560, 512)

    z16, z32 = jnp.zeros((16,), F32), jnp.zeros((32,), F32)
    p0 = jnp.concatenate([W['gdn_a_log'].reshape(-1), jnp.zeros((112,), F32)]).reshape(1, 128)
    p1 = jnp.concatenate([W['gdn_dt_bias'].reshape(-1), z16, W['ssm_dt_bias'].reshape(-1), z32]).reshape(1, 128)
    alog = jnp.concatenate([z32, W['ssm_a_log'].reshape(-1), z32]).reshape(1, 128)
    act_tok, act_par = [tokop(proj, 128, P_SMALL // 128)], [parop(p0, 128), parop(p1, 128)]
    (sp,) = _tw_fwd("small_act", _act_fn, (1, t // tlf), [tokop(proj, 128, P_SMALL // 128, tlf)] + act_par,
                    [outop(128, F32, 128, tlf)])

    gather_late = PlaneExchange([late_shards[n] for n in LATE_WEIGHTS], gather=True)
    o_f, o_b, ss, tri, *late = _scan2_fwd("gdn_scan_fwd", _gdn_step2, [q, k, v, sp], 1024, GDN_HEADS, nctx,
                                          GDN_HEADS, ride=gather_late)
    W = dict(W, **{n: _from_shards(n, g) for n, g in zip(LATE_WEIGHTS, late)})
    ssd_consts = _ssd_consts()
    y2, hs = _scan_fwd("ssd_scan_fwd", _ssd_step, [xs, bm, cm, sp], [alog], ssd_consts, 2048, SSM_PAIRS, nctx)

    tlm = 128
    ntm = t // tlm
    gnw = r1(W['gdn_norm_w'])
    dl = jnp.repeat(W['ssm_d'], 64).reshape(1, 2048)
    snw = r1(W['ssm_norm_w'])
    mg_tok = [tokop([o_f, o_b], 1024, 0, tlm), tokop(proj, 1024, P_ZG // 1024, tlm)]
    mg_par = [parop(gnw, 128)]
    (og,) = _tw_fwd("mix_gdn", _mixg_fn, (1, nt),
                    [tokop([o_f, o_b], 1024, 0), tokop(proj, 1024, P_ZG // 1024)] + mg_par, [outop(1024, act, 1024)])
    ms_tok = [tokop(y2, 2048, 0, tlm), tokop(xs, 2048, 0, tlm), tokop(proj, 2048, P_ZS // 2048, tlm)]
    ms_par = [parop(dl, 2048), parop(snw, 2048)]
    (yg,) = _tw_fwd("mix_ssm", _mixs_fn, (1, nt),
                    [tokop(y2, 2048, 0), tokop(xs, 2048, 0), tokop(proj, 2048, P_ZS // 2048)] + ms_par,
                    [outop(2048, act, 2048)])

    pg = _matmul(og, W['w_br_gdn'], 'nn', "br_gdn", out_dtype=act)
    ps = _matmul(yg, W['w_br_ssm'], 'nn', "br_ssm", out_dtype=act)
    mr_tok = [tokop(proj, 2048, P_GATE // 2048), tokop(pg), tokop(ps)]
    (mrg,) = _tw_fwd("merge", _merge_fn, (1, t // tlf),
                     [tokop(proj, 2048, P_GATE // 2048, tlf), tokop(pg, tl_=tlf), tokop(ps, tl_=tlf)],
                     [outop(1024, act, tl_=tlf)])
    mo = _matmul(mrg, W['w_out'], 'nn', "w_out")

    n2w = r1(W['norm2_w'])
    n2_tok, n2_par = [tokop(xt), tokop(mo)], [mod(2), parop(n2w, D_MODEL), mod(4), mod(3)]
    h1, f = _tw_fwd("norm2_fwd", _norm2_fn, (1, t // tlf), [tokop(xt, tl_=tlf), tokop(mo, tl_=tlf)] + n2_par,
                    [outop(1024, F32, tl_=tlf), outop(1024, act, tl_=tlf)])
    u2 = _matmul(f, W['w_ffn_in'], 'nn', "ffn_in", out_dtype=act, halves='out')
    swb = D_FF // 2
    both = lambda arr, tl_=tl: Op(arr, (2, tl_, swb), lambda j, i: (0, i, j))
    sw_tok = [both(u2)]
    (sw,) = _tw_fwd("swiglu", _swiglu_fn, (D_FF // swb, t // tlf), [both(u2, tlf)], [outop(D_FF, act, swb, tlf)])
    ff = _matmul(sw, W['w_ffn_out'], 'nn', "ffn_out")

    loss8, dh1, dff, dg2, dnf = _tail(h1, ff, mods, r1(W['norm_f_w']), tgt, nctx_t, tl)
    loss = loss8[0, 0]

    G = {}
    G['norm_f_w'] = dnf.reshape(-1)
    G['w_ffn_out'] = _matmul(sw, dff, 'tn', "d_ffn_out", out_dtype=_GRAD_DTYPE)
    dsw = _matmul(dff, W['w_ffn_out'], 'nt', "d_sw", out_dtype=act)
    (du2,) = _tw_bwd("swiglu_bwd", _swiglu_fn, (D_FF // swb, nt), sw_tok, [], [[tokop(dsw, swb)]],
                     [both(((2, t, D_FF), act))], [])
    G['w_ffn_in'] = _matmul(f, du2, 'tn', "d_ffn_in", out_dtype=_GRAD_DTYPE, stacked_out=True, halves='b')
    df = _matmul(du2, W['w_ffn_in'], 'nt', "d_f", halves='a')
    dxt1, dmo, dg1, dn2, dsc2, dsh2 = _tw_bwd(
        "norm2_bwd", _norm2_fn, (1, nt), n2_tok, n2_par, [[tokop(dh1)], [tokop(df)]],
        [outop(1024, F32), outop(1024, act)],
        [parout(8, 1024, 1024), parout(1, 1024, 1024), parout(8, 1024, 1024), parout(8, 1024, 1024)])
    G['norm2_w'] = dn2.reshape(-1)
    G['w_out'] = _matmul(mrg, dmo, 'tn', "d_w_out", out_dtype=_GRAD_DTYPE)
    dmrg = _matmul(dmo, W['w_out'], 'nt', "d_mrg", out_dtype=act)
    def win(off, w, tl_=tl):
        return Op(((t, P_TOTAL), act), (tl_, w), lambda j, i: (i, off // w + j))

    dproj = lax.empty((t, P_TOTAL), act)
    dproj, dpg, dps = _tw_bwd("merge_bwd", _merge_fn, (1, nt), mr_tok, [], [[tokop(dmrg)]],
                              [win(P_GATE, 2048), outop(1024, act), outop(1024, act)], [], into={0: dproj})
    G['w_br_gdn'] = _matmul(og, dpg, 'tn', "d_br_gdn", out_dtype=_GRAD_DTYPE)
    G['w_br_ssm'] = _matmul(yg, dps, 'tn', "d_br_ssm", out_dtype=_GRAD_DTYPE)
    dog = _matmul(dpg, W['w_br_gdn'], 'nt', "d_og")
    dyg = _matmul(dps, W['w_br_ssm'], 'nt', "d_yg")

    do, dproj, dgnw = _tw_bwd("mix_gdn_bwd", _mixg_fn, (1, ntm), mg_tok, mg_par, [[tokop(dog, 1024, 0, tlm)]],
                              [outop(1024, F32, 1024, tlm), win(P_ZG, 1024, tlm)], [parout(1, 128, 128)],
                              into={1: dproj})
    G['gdn_norm_w'] = dgnw.reshape(-1)
    dy, dxs_a, dproj, ddl, dsnw = _tw_bwd(
        "mix_ssm_bwd", _mixs_fn, (1, ntm), ms_tok, ms_par, [[tokop(dyg, 2048, 0, tlm)]],
        [outop(2048, F32, 2048, tlm), outop(2048, F32, 2048, tlm), win(P_ZS, 2048, tlm)],
        [parout(1, 2048, 2048), parout(1, 2048, 2048)], into={2: dproj})
    G['ssm_d'] = ddl.reshape(SSM_HEADS, 64).sum(axis=1)
    G['ssm_norm_w'] = dsnw.reshape(-1)

    dq2, dk2, dv2, dsp_g = _scan_bwd("gdn_scan_bwd", _gdn_step, [q, k, v, sp], [], ss, do, GDN_HEADS, nctx, aux=tri)
    received = {}
    scatter_late = PlaneExchange([_to_dest_blocks(n, G.pop(n)) for n in LATE_WEIGHTS], gather=False)
    dxs2, dbm2, dcm2, dsp_s, dalog, *got = _scan_bwd("ssd_scan_bwd", _ssd_step, [xs, bm, cm, sp], [alog], hs, dy,
                                                     SSM_PAIRS, nctx, ride=scatter_late, consts=ssd_consts)
    received.update(zip(LATE_WEIGHTS, got))
    G['ssm_a_log'] = dalog[0, 32:96].reshape(2, SSM_HEADS)

    dconv_w, dconv_b = {}, {}

    def conv_bwd(name, cots_, buf):
        fn, tok_, par_, width, bw, poff = conv_parts[name]
        buf, dconv_w[name], dconv_b[name] = _tw_bwd(
            "conv_" + name + "_bwd", fn, (width // bw, nt), tok_, par_, [[tokop(c_, bw) for c_ in cots_]],
            [win(poff, bw)], [parout(3, width, bw), parout(1, width, bw)], into={0: buf})
        return buf

    dproj = conv_bwd('q', [dq2], dproj)
    dproj = conv_bwd('k', [dk2], dproj)
    dproj = conv_bwd('v', [dv2], dproj)
    dproj = conv_bwd('xs', [dxs2, dxs_a], dproj)
    dproj = conv_bwd('bm', [dbm2], dproj)
    dproj = conv_bwd('cm', [dcm2], dproj)
    G['gdn_conv_w'] = jnp.concatenate([dconv_w['q'], dconv_w['k'], dconv_w['v']], axis=1)
    G['gdn_conv_b'] = [dconv_b['q'], dconv_b['k'], dconv_b['v']]
    G['ssm_conv_w'] = jnp.concatenate([dconv_w['xs'], dconv_w['bm'], dconv_w['cm']], axis=1)
    G['ssm_conv_b'] = [dconv_b['xs'], dconv_b['bm'], dconv_b['cm']]

    dproj, dp0, dp1 = _tw_bwd("small_act_bwd", _act_fn, (1, nt), act_tok, act_par,
                              [[tokop(dsp_g, 128), tokop(dsp_s, 128)]], [win(P_SMALL, P_TOTAL - P_SMALL)],
                              [parout(1, 128, 128), parout(1, 128, 128)], into={0: dproj})
    G['gdn_a_log'] = dp0[0, 0:16].reshape(2, GDN_HEADS)
    G['gdn_dt_bias'] = dp1[0, 0:16].reshape(2, GDN_HEADS)
    G['ssm_dt_bias'] = dp1[0, 32:96].reshape(2, SSM_HEADS)

    g_top = _unpack_w_in(_matmul(a, dproj, 'tn', "d_w_in_top", out_dtype=_GRAD_DTYPE, m_half=0))
    g_bot, (r_top,) = _matmul(a, dproj, 'tn', "d_w_in_bot", out_dtype=_GRAD_DTYPE, m_half=1,
                              ride=PlaneExchange([g_top], gather=False))
    da, (r_bot,) = _matmul(dproj, wp, 'nt', "d_a", ride=PlaneExchange([_unpack_w_in(g_bot)], gather=False))
    received['w_in'] = (r_top, r_bot)

    gx_out = Op(((l, D_MODEL), F32), (tl, D_MODEL), lambda j, i: (jnp.maximum(i - nctx_t, 0), 0))
    grad_x, dn1, dsc1, dsh1 = _tw_bwd(
        "prenorm_bwd", pre_fn, (1, nt), pre_tok, pre_par, [[tokop(da)]], [gx_out],
        [parout(1, 1024, 1024), parout(8, 1024, 1024), parout(8, 1024, 1024)], tok_add=tokop(dxt1),
        sem=("arbitrary", "arbitrary"))
    G['norm1_w'] = dn1.reshape(-1)
    dmods = jnp.concatenate([dsh1, dsc1, dg1, dsh2, dsc2, dg2], axis=1)
    dm_all = _gather_all("gather_dmods", dmods)
    ns = W['ada_w'].shape[1]
    mine = lax.dynamic_slice_in_dim(dm_all, (2 * lax.axis_index("x") + lax.axis_index("y")) * ns, ns, axis=2)
    G['ada_w'], dcc = _ada_bwd_shard(cc16, W['ada_w'], mine[:, 0], mine[:, 1])
    G['ada_b'] = jnp.where(lin == 0, _rowsum2(dm_all[:, 0], dm_all[:, 1]).reshape(-1), 0.0)
    G['c_ctx'] = dcc[0]
    return loss, grad_x, G, received


def _row_tile(r, c):
    for tr in (512, 256, 128, 64, 32, 16, 8):
        if r % tr == 0 and tr * c * 4 <= (1 << 20):
            return tr
    return r


def _sum4(name, rv):
    _, r, c = rv.shape
    tr = _row_tile(r, c)

    def body(r_ref, o_ref):
        o_ref[...] = ((r_ref[0].astype(F32) + r_ref[1].astype(F32)) + r_ref[2].astype(F32)) + r_ref[3].astype(F32)

    return pl.pallas_call(
        body, grid=(r // tr,), in_specs=[pl.BlockSpec((N_PLANE, tr, c), lambda i: (0, i, 0))],
        out_specs=pl.BlockSpec((tr, c), lambda i: (i, 0)), out_shape=jax.ShapeDtypeStruct((r, c), F32),
        compiler_params=_cp(("parallel",)), name=name)(rv)


def _sum4_halves(name, top, bot):
    _, rh, c = top.shape
    tr = _row_tile(rh, c)
    nt = rh // tr

    def body(t_ref, b_ref, o_ref):
        lower = pl.program_id(0) == 1
        blk = lambda j: jnp.where(lower, b_ref[j].astype(F32), t_ref[j].astype(F32))
        o_ref[...] = ((blk(0) + blk(1)) + blk(2)) + blk(3)

    spec = pl.BlockSpec((N_PLANE, tr, c), lambda h, i: (0, i, 0))
    return pl.pallas_call(
        body, grid=(2, nt), in_specs=[spec, spec], out_specs=pl.BlockSpec((tr, c), lambda h, i: (h * nt + i, 0)),
        out_shape=jax.ShapeDtypeStruct((2 * rh, c), F32), compiler_params=_cp(("parallel", "parallel")), name=name)(top, bot)


def _adamw(name, w, m, v, p, q=None):
    r, c = w.shape[-2:]
    tr = _row_tile(r, c)
    grads = [p] if q is None else [p, q]
    lead = (0,) * (w.ndim - 2)

    def body(w_ref, m_ref, v_ref, *refs):
        g_ref, d_ref, mo_ref, vo_ref = refs[len(grads):]
        g = refs[0][...] if q is None else refs[0][...] + refs[1][...]
        at = lead if lead else Ellipsis
        mn = ADAM_B1 * m_ref[at] + (1.0 - ADAM_B1) * g
        vn = ADAM_B2 * v_ref[at] + (1.0 - ADAM_B2) * jnp.square(g)
        m_hat = mn / (1.0 - ADAM_B1 ** ADAM_STEP)
        v_hat = vn / (1.0 - ADAM_B2 ** ADAM_STEP)
        g_ref[at] = g
        d_ref[at] = -ADAM_LR * (m_hat / (jnp.sqrt(v_hat) + ADAM_EPS) + ADAM_WD * w_ref[at])
        mo_ref[at] = mn
        vo_ref[at] = vn

    gspec = pl.BlockSpec((tr, c), lambda i: (i, 0))
    wspec = pl.BlockSpec((1,) * len(lead) + (tr, c), lambda i: lead + (i, 0))
    return pl.pallas_call(
        body, grid=(r // tr,), in_specs=[wspec] * 3 + [gspec] * len(grads), out_specs=[wspec] * 4,
        out_shape=[jax.ShapeDtypeStruct(w.shape, F32)] * 4, compiler_params=_cp(("parallel",)), name=name)(w, m, v, *grads)


SMALL_ROWS = 24


def _pack_small(d):
    pieces = [p.reshape(-1) for n in SMALL for p in (d[n] if isinstance(d[n], list) else [d[n]])]
    v = jnp.concatenate(pieces)
    return jnp.pad(v, (0, SMALL_ROWS * 1024 - v.shape[0])).reshape(SMALL_ROWS, 1024)


def _unpack_small(buf, like):
    v = buf.reshape(-1)
    out, off = {}, 0
    for n in SMALL:
        sz = like[n].size
        out[n] = v[off:off + sz].reshape(like[n].shape)
        off += sz
    return out


def kernel(x, c, ctx, c_ctx, ada_w, ada_b, norm1_w, w_in, gdn_conv_w, gdn_conv_b, gdn_a_log, gdn_dt_bias, gdn_norm_w, ssm_conv_w, ssm_conv_b, ssm_a_log, ssm_dt_bias, ssm_d, ssm_norm_w, w_br_gdn, w_br_ssm, w_out, norm2_w, w_ffn_in, w_ffn_out, norm_f_w, loss_target, m_c_ctx, m_ada_w, m_ada_b, m_norm1_w, m_w_in, m_gdn_conv_w, m_gdn_conv_b, m_gdn_a_log, m_gdn_dt_bias, m_gdn_norm_w, m_ssm_conv_w, m_ssm_conv_b, m_ssm_a_log, m_ssm_dt_bias, m_ssm_d, m_ssm_norm_w, m_w_br_gdn, m_w_br_ssm, m_w_out, m_norm2_w, m_w_ffn_in, m_w_ffn_out, m_norm_f_w, v_c_ctx, v_ada_w, v_ada_b, v_norm1_w, v_w_in, v_gdn_conv_w, v_gdn_conv_b, v_gdn_a_log, v_gdn_dt_bias, v_gdn_norm_w, v_ssm_conv_w, v_ssm_conv_b, v_ssm_a_log, v_ssm_dt_bias, v_ssm_d, v_ssm_norm_w, v_w_br_gdn, v_w_br_ssm, v_w_out, v_norm2_w, v_w_ffn_in, v_w_ffn_out, v_norm_f_w):
    args = dict(locals())
    wl = {n: args[n] for n in WEIGHTS}
    ml = {n: args['m_' + n] for n in WEIGHTS}
    vl = {n: args['v_' + n] for n in WEIGHTS}

    def nodepth(n, a):
        return a if n in ('c_ctx', 'norm_f_w') else a[0]

    shard = {n: nodepth(n, wl[n]).astype(_MXU_DTYPE if n in MXU_WEIGHTS else F32) for n in SHARDED}
    first = [n for n in SHARDED if n not in LATE_WEIGHTS and n != 'ada_w']
    W = {n: nodepth(n, wl[n]) for n in SMALL}
    W['ada_w'] = shard['ada_w']
    for n, g in zip(first, _gather_two_level("all_gather_plane", [shard[n] for n in first])):
        W[n] = _from_shards(n, g)

    loss_local, grad_x, G, received = _local_step(x[0], c, ctx[0], loss_target[0], W,
                                                  late_shards={n: shard[n] for n in LATE_WEIGHTS})
    loss = lax.psum(loss_local, ("x", "y", "c"))

    small_g = _pack_small(G)
    last = [n for n in SHARDED if n not in received and n != 'ada_w']
    send = [_to_dest_blocks(n, G[n]) for n in last] + [jnp.broadcast_to(small_g[None], (N_PLANE,) + small_g.shape)]
    received.update(zip(last + ['small'], _plane_exchange("scatter_plane", send, gather=False)))
    names = [n for n in SHARDED if n != 'ada_w'] + ['small']
    plane_sum = [_sum4_halves("sum4_" + n, *received[n]) if isinstance(received[n], tuple)
                 else _sum4("sum4_" + n, received[n]) for n in names]
    other = _swap_sibling(plane_sum)

    wd = {n: nodepth(n, wl[n]) for n in WEIGHTS}
    md = {n: nodepth(n, ml[n]) for n in WEIGHTS}
    vd = {n: nodepth(n, vl[n]) for n in WEIGHTS}
    res = {'ada_w': _adamw("adamw_ada_w", wl['ada_w'], ml['ada_w'], vl['ada_w'], G['ada_w'])}
    for n, p, q in zip(names, plane_sum, other):
        if n == 'small':
            outs = _adamw("adamw_small", _pack_small(wd), _pack_small(md), _pack_small(vd), p, q)
            unpacked = [_unpack_small(o, wd) for o in outs]
            for sn in SMALL:
                res[sn] = [u[sn].reshape(wl[sn].shape) for u in unpacked]
        else:
            res[n] = _adamw("adamw_" + n, wl[n], ml[n], vl[n], p, q)
    flat = [res[n][kind] for kind in range(4) for n in WEIGHTS]
    return (loss, grad_x[None], *flat)
```

```python
import functools

import jax
import jax.numpy as jnp
from jax import lax
from jax.experimental import pallas as pl
from jax.experimental.pallas import tpu as pltpu

F32 = jnp.float32
HI = lax.Precision.HIGHEST
_MXU_DTYPE = jnp.bfloat16
_SCAN_DTYPE = jnp.bfloat16
_GRAD_DTYPE = jnp.bfloat16

D_MODEL = 1024
EPS = 1e-6
CHUNK = 64
GRID_W = 64
GDN_HEADS = 8
GDN_DK = 128
SSM_HEADS = 32
SSM_PAIRS = 16
SSD_GROUP = 1
D_FF = 2816
D_IN_PROJ = 11360
N_PLANE = 4
P_QKV, P_ZG, P_GATE, P_ZS, P_XBC, P_SMALL, P_TOTAL = 0, 3072, 4096, 6144, 8192, 11264, 11776
IN_SEGMENTS = [(0, 3072, P_QKV), (3072, 4096, P_ZG), (4096, 4112, P_SMALL), (4112, 4128, P_SMALL + 16),
               (4128, 6176, P_ZS), (6176, 9248, P_XBC), (9248, 9312, P_SMALL + 32), (9312, 11360, P_GATE)]
ADAM_LR, ADAM_B1, ADAM_B2, ADAM_EPS, ADAM_WD, ADAM_STEP = 0.001, 0.9, 0.999, 1e-08, 0.01, 10

VMEM_LIMIT = 48 * 1024 * 1024

WEIGHTS = ['c_ctx', 'ada_w', 'ada_b', 'norm1_w', 'w_in', 'gdn_conv_w', 'gdn_conv_b', 'gdn_a_log', 'gdn_dt_bias',
           'gdn_norm_w', 'ssm_conv_w', 'ssm_conv_b', 'ssm_a_log', 'ssm_dt_bias', 'ssm_d', 'ssm_norm_w', 'w_br_gdn',
           'w_br_ssm', 'w_out', 'norm2_w', 'w_ffn_in', 'w_ffn_out', 'norm_f_w']
SHARD_AXIS = {'ada_w': 1, 'w_in': 1, 'gdn_conv_w': 1, 'ssm_conv_w': 1, 'w_br_gdn': 0, 'w_br_ssm': 0, 'w_out': 0,
              'w_ffn_in': 1, 'w_ffn_out': 0}
SHARDED = [n for n in WEIGHTS if n in SHARD_AXIS]
SMALL = [n for n in WEIGHTS if n not in SHARD_AXIS]
MXU_WEIGHTS = ['ada_w', 'w_in', 'w_br_gdn', 'w_br_ssm', 'w_out', 'w_ffn_in', 'w_ffn_out']


def _cp(sem):
    return pltpu.CompilerParams(dimension_semantics=sem, vmem_limit_bytes=VMEM_LIMIT)


def _pick(n, cands):
    for c in cands:
        if n % c == 0:
            return c
    raise ValueError(f"no tile for {n}")


HBM = pl.BlockSpec(memory_space=pl.ANY)
MESH = pl.DeviceIdType.MESH


def _plane_peers():
    x, y, c = lax.axis_index("x"), lax.axis_index("y"), lax.axis_index("c")
    return (x, y, c), [(1 - x, y, c), (x, 1 - y, c), (1 - x, 1 - y, c)]


class PlaneExchange:
    def __init__(self, arrs, gather):
        self.arrs, self.gather, self.n = list(arrs), gather, len(arrs)
        self.in_specs = [HBM] * self.n
        self.out_specs = [HBM] * self.n
        self.out_shape = [jax.ShapeDtypeStruct((N_PLANE,) + a.shape if gather else a.shape, a.dtype) for a in self.arrs]
        self.scratch = [pltpu.SemaphoreType.DMA((3 * self.n,)), pltpu.SemaphoreType.DMA((3 * self.n,)),
                        pltpu.SemaphoreType.DMA((self.n,))]

    def _copies(self, ins, outs, sems):
        send_sems, recv_sems, local_sems = sems
        (x, y, c), peers = _plane_peers()
        me = 2 * x + y
        copies = []
        for ti in range(self.n):
            src = ins[ti] if self.gather else ins[ti].at[me]
            copies.append(pltpu.make_async_copy(src, outs[ti].at[me], local_sems.at[ti]))
            for kk, (px, py, pc) in enumerate(peers):
                src = ins[ti] if self.gather else ins[ti].at[2 * px + py]
                copies.append(pltpu.make_async_remote_copy(
                    src_ref=src, dst_ref=outs[ti].at[me], send_sem=send_sems.at[3 * ti + kk],
                    recv_sem=recv_sems.at[3 * ti + kk], device_id=(px, py, pc), device_id_type=MESH))
        return copies

    def start(self, ins, outs, sems):
        for cp in self._copies(ins, outs, sems):
            cp.start()

    def wait(self, ins, outs, sems):
        for cp in self._copies(ins, outs, sems):
            cp.wait()


def _plane_exchange(name, arrs, gather):
    ex = PlaneExchange(arrs, gather)
    n = ex.n

    def body(*refs):
        ins, outs, sems = refs[:n], refs[n:2 * n], refs[2 * n:]
        ex.start(ins, outs, sems)
        ex.wait(ins, outs, sems)

    return pl.pallas_call(body, in_specs=ex.in_specs, out_specs=ex.out_specs, out_shape=ex.out_shape,
                          scratch_shapes=ex.scratch, name=name)(*arrs)


def _gather_two_level(name, arrs):
    n = len(arrs)
    split = [a.shape[0] % 32 == 0 for a in arrs]

    def body(*refs):
        ins, outs = refs[:n], refs[n:2 * n]
        ici_send, ici_recv, d2d_send, d2d_recv, local_sems = refs[2 * n:]
        (x, y, c), peers = _plane_peers()
        me = 2 * x + y

        def part(ref, ti):
            if not split[ti]:
                return ref
            rows = arrs[ti].shape[0] // 2
            return ref.at[pl.ds(c * rows, rows)]

        local = [pltpu.make_async_copy(ins[ti], outs[ti].at[me], local_sems.at[ti]) for ti in range(n)]
        ici = [pltpu.make_async_remote_copy(
            src_ref=part(ins[ti], ti), dst_ref=part(outs[ti].at[me], ti), send_sem=ici_send.at[3 * ti + kk],
            recv_sem=ici_recv.at[3 * ti + kk], device_id=peer, device_id_type=MESH)
            for ti in range(n) for kk, peer in enumerate(peers)]
        for cp in local + ici:
            cp.start()
        d2d = []
        for ti in range(n):
            for kk, (px, py, pc) in enumerate(peers):
                ici[3 * ti + kk].wait_recv()
                if split[ti]:
                    piece = part(outs[ti].at[2 * px + py], ti)
                    cp = pltpu.make_async_remote_copy(
                        src_ref=piece, dst_ref=piece, send_sem=d2d_send.at[3 * ti + kk],
                        recv_sem=d2d_recv.at[3 * ti + kk], device_id=(x, y, 1 - c), device_id_type=MESH)
                    cp.start()
                    d2d.append(cp)
        for cp in ici:
            cp.wait_send()
        for cp in d2d:
            cp.wait()
        for cp in local:
            cp.wait()

    return pl.pallas_call(
        body, in_specs=[HBM] * n, out_specs=[HBM] * n,
        out_shape=[jax.ShapeDtypeStruct((N_PLANE,) + a.shape, a.dtype) for a in arrs],
        scratch_shapes=[pltpu.SemaphoreType.DMA((3 * n,))] * 4 + [pltpu.SemaphoreType.DMA((n,))], name=name)(*arrs)


def _gather_all(name, arr):
    flips = [(fx, fy, fc) for fx in (0, 1) for fy in (0, 1) for fc in (0, 1)][1:]

    def body(a_ref, o_ref, send_sems, recv_sems, local_sem):
        x, y, c = lax.axis_index("x"), lax.axis_index("y"), lax.axis_index("c")
        me = 4 * x + 2 * y + c
        copies = [pltpu.make_async_copy(a_ref, o_ref.at[me], local_sem)]
        for kk, (fx, fy, fc) in enumerate(flips):
            peer = (1 - x if fx else x, 1 - y if fy else y, 1 - c if fc else c)
            copies.append(pltpu.make_async_remote_copy(
                src_ref=a_ref, dst_ref=o_ref.at[me], send_sem=send_sems.at[kk], recv_sem=recv_sems.at[kk],
                device_id=peer, device_id_type=MESH))
        for cp in copies:
            cp.start()
        for cp in copies:
            cp.wait()

    return pl.pallas_call(
        body, in_specs=[HBM], out_specs=HBM, out_shape=jax.ShapeDtypeStruct((8,) + arr.shape, arr.dtype),
        scratch_shapes=[pltpu.SemaphoreType.DMA((7,)), pltpu.SemaphoreType.DMA((7,)), pltpu.SemaphoreType.DMA],
        name=name)(arr)


def _swap_sibling(arrs):
    n = len(arrs)

    def body(*refs):
        ins, outs, send_sems, recv_sems = refs[:n], refs[n:2 * n], refs[2 * n], refs[2 * n + 1]
        x, y, c = lax.axis_index("x"), lax.axis_index("y"), lax.axis_index("c")
        copies = [pltpu.make_async_remote_copy(src_ref=ins[ti], dst_ref=outs[ti], send_sem=send_sems.at[ti],
                                               recv_sem=recv_sems.at[ti], device_id=(x, y, 1 - c), device_id_type=MESH)
                  for ti in range(n)]
        for cp in copies:
            cp.start()
        for cp in copies:
            cp.wait()

    return pl.pallas_call(
        body, in_specs=[HBM] * n, out_specs=[HBM] * n, out_shape=[jax.ShapeDtypeStruct(a.shape, a.dtype) for a in arrs],
        scratch_shapes=[pltpu.SemaphoreType.DMA((n,)), pltpu.SemaphoreType.DMA((n,))], name="swap_sibling")(*arrs)


MATMUL_VMEM_BUDGET = 36 * 1024 * 1024
TILE_CANDIDATES = (2944, 2816, 1408, 1024, 768, 512, 256, 128)
STEP_COST_BYTES = 1 << 20
ACC_PASS_WEIGHT = 0.35


def _matmul_tiles(m, n, k, ab, bb, ob, tn_fixed=None, tk_fixed=None):
    def cands(dim, whole_up_to):
        out = [c for c in TILE_CANDIDATES if dim % c == 0]
        if dim <= whole_up_to and dim not in out:
            out.append(dim)
        return out

    best = None
    for tm in cands(m, 3072):
        for tn in ([tn_fixed] if tn_fixed else cands(n, 3072)):
            for tk in ([tk_fixed] if tk_fixed else cands(k, 2048)):
                gi, gj, gk = m // tm, n // tn, k // tk
                vmem = 2 * (tm * tk * ab + tk * tn * bb + tm * tn * ob) + (tm * tn * 4 if gk > 1 else 0)
                if vmem > MATMUL_VMEM_BUDGET:
                    continue
                a_reads = 1 if gk == 1 else gj
                b_reads = 1 if (gk == 1 and gj == 1) else gi
                acc = m * n * 8 * gk * ACC_PASS_WEIGHT if gk > 1 else 0
                cost = (m * k * ab * a_reads + k * n * bb * b_reads + m * n * ob + gi * gj * gk * STEP_COST_BYTES + acc)
                if best is None or cost < best[0]:
                    best = (cost, tm, tn, tk)
    assert best is not None, (m, n, k)
    return best[1:]


def _matmul(a, b, form, name, out_dtype=F32, stacked_out=False, ride=None, halves=None):
    dims = {'nn': (((1,), (0,)), ((), ())), 'nt': (((1,), (1,)), ((), ())), 'tn': (((0,), (0,)), ((), ()))}[form]
    stacked_b = b.ndim == 3 and halves != 'b'
    ns = b.shape[2] if stacked_b else None
    if form == 'nn':
        m, k = a.shape
        n = b.shape[0] * ns if stacked_b else b.shape[1]
    elif form == 'nt':
        m, k = (a.shape[1], 2 * a.shape[2]) if halves == 'a' else a.shape
        n = b.shape[1] if stacked_b else b.shape[0]
    else:
        k, m = a.shape
        n = 2 * b.shape[2] if halves == 'b' else b.shape[1]
    ob = jnp.dtype(out_dtype).itemsize
    tm, tn, tk = _matmul_tiles(m, n, k, a.dtype.itemsize, b.dtype.itemsize, ob,
                               tn_fixed=(n // N_PLANE if stacked_out else ns if (stacked_b and form == 'nn') else None),
                               tk_fixed=(ns if (stacked_b and form == 'nt') else None))
    nk = k // tk
    grid = (m // tm, n // tn, nk)
    n_ride = ride.n if ride is not None else 0

    def body(*refs):
        a_ref, b_ref = refs[0], refs[1]
        ride_in = refs[2:2 + n_ride]
        o_ref = refs[2 + n_ride]
        ride_out = refs[3 + n_ride:3 + 2 * n_ride]
        acc_ref = refs[3 + 2 * n_ride]
        sems = refs[4 + 2 * n_ride:]
        i, j, kk = pl.program_id(0), pl.program_id(1), pl.program_id(2)
        if ride is not None:
            @pl.when((i == 0) & (j == 0) & (kk == 0))
            def _():
                ride.start(ride_in, ride_out, sems)

        def put(val):
            if stacked_out or halves == 'out':
                o_ref[0] = val.astype(o_ref.dtype)
            else:
                o_ref[...] = val.astype(o_ref.dtype)

        av = a_ref[0] if halves == 'a' else a_ref[...]
        bv = b_ref[0] if (stacked_b or halves == 'b') else b_ref[...]
        part = lax.dot_general(av.astype(_MXU_DTYPE), bv.astype(_MXU_DTYPE), dims, preferred_element_type=F32)
        if nk == 1:
            put(part)
        else:
            @pl.when(kk == 0)
            def _():
                acc_ref[...] = part

            @pl.when(kk > 0)
            def _():
                acc_ref[...] += part

            @pl.when(kk == nk - 1)
            def _():
                put(acc_ref[...])

        if ride is not None:
            @pl.when((i == grid[0] - 1) & (j == grid[1] - 1) & (kk == nk - 1))
            def _():
                ride.wait(ride_in, ride_out, sems)

    if form == 'nn':
        a_spec = pl.BlockSpec((tm, tk), lambda i, j, kk: (i, kk))
        b_spec = (pl.BlockSpec((1, tk, tn), lambda i, j, kk: (j, kk, 0)) if stacked_b
                  else pl.BlockSpec((tk, tn), lambda i, j, kk: (kk, j)))
    elif form == 'nt':
        a_spec = pl.BlockSpec((tm, tk), lambda i, j, kk: (i, kk))
        b_spec = (pl.BlockSpec((1, tn, tk), lambda i, j, kk: (kk, j, 0)) if stacked_b
                  else pl.BlockSpec((tn, tk), lambda i, j, kk: (j, kk)))
    else:
        a_spec = pl.BlockSpec((tk, tm), lambda i, j, kk: (kk, i))
        b_spec = pl.BlockSpec((tk, tn), lambda i, j, kk: (kk, j))
    hj, hk = grid[1] // 2, nk // 2
    if halves == 'a':
        a_spec = pl.BlockSpec((1, tm, tk), lambda i, j, kk: (kk // hk, i, kk % hk))
    if halves == 'b':
        b_spec = pl.BlockSpec((1, tk, tn), lambda i, j, kk: (j // hj, kk, j % hj))
    if stacked_out:
        o_spec = pl.BlockSpec((1, tm, tn), lambda i, j, kk: (j, i, 0))
        o_shape = jax.ShapeDtypeStruct((N_PLANE, m, tn), out_dtype)
    elif halves == 'out':
        o_spec = pl.BlockSpec((1, tm, tn), lambda i, j, kk: (j // hj, i, j % hj))
        o_shape = jax.ShapeDtypeStruct((2, m, n // 2), out_dtype)
    else:
        o_spec = pl.BlockSpec((tm, tn), lambda i, j, kk: (i, j))
        o_shape = jax.ShapeDtypeStruct((m, n), out_dtype)
    acc_shape = (tm, tn) if nk > 1 else (8, 128)
    if ride is None:
        return pl.pallas_call(
            body, grid=grid, in_specs=[a_spec, b_spec], out_specs=o_spec, out_shape=o_shape,
            scratch_shapes=[pltpu.VMEM(acc_shape, F32)],
            compiler_params=_cp(("parallel", "parallel", "arbitrary")), name=name)(a, b)
    res = pl.pallas_call(
        body, grid=grid, in_specs=[a_spec, b_spec] + ride.in_specs, out_specs=[o_spec] + ride.out_specs,
        out_shape=[o_shape] + ride.out_shape, scratch_shapes=[pltpu.VMEM(acc_shape, F32)] + ride.scratch,
        compiler_params=_cp(("arbitrary", "arbitrary", "arbitrary")), name=name)(a, b, *ride.arrs)
    return res[0], res[1:]


class Op:
    def __init__(self, arr, bs, im, load=None):
        self.arr, self.bs, self.im = arr, bs, im
        self.arrs = list(arr) if isinstance(arr, list) else [arr]
        self.load = load or (lambda r: r[...].astype(F32))

    def spec(self):
        return pl.BlockSpec(self.bs, self.im)

    def value(self, it):
        return functools.reduce(lambda u, w: u + w, [self.load(next(it)) for _ in self.arrs])


def _op_specs(ops):
    return [op.spec() for op in ops for _ in op.arrs]


def _op_arrays(ops):
    return [a for op in ops for a in op.arrs]


def _sum_dirs(r):
    return r[0].astype(F32) + r[1].astype(F32)


def _tw_fwd(name, fn, grid, ins, outs):
    def body(*refs):
        info = (pl.program_id(0), pl.program_id(1))
        it = iter(refs)
        vals = [op.value(it) for op in ins]
        res = fn(info, *vals)
        for r, v in zip(it, res):
            r[...] = v.astype(r.dtype)

    return pl.pallas_call(
        body, grid=grid, in_specs=_op_specs(ins), out_specs=[op.spec() for op in outs],
        out_shape=[jax.ShapeDtypeStruct(*op.arr) for op in outs],
        compiler_params=_cp(("parallel", "arbitrary")), name=name)(*_op_arrays(ins))


def _tw_bwd(name, fn, grid, tok, par, cots, tok_out, par_out, tok_add=None, sem=("parallel", "arbitrary"), into=None):
    n_tok = len(tok)
    flat_cots = [op for group in cots for op in group]
    extra = [tok_add] if tok_add is not None else []
    out_ops = list(tok_out) + list(par_out)
    into = into or {}

    def body(*refs):
        info = (pl.program_id(0), pl.program_id(1))
        it = iter(refs)
        tok_v = [op.value(it) for op in tok]
        par_v = [op.value(it) for op in par]
        cot_v = [functools.reduce(lambda u, w: u + w, [op.value(it) for op in group]) for group in cots]
        add_v = [op.value(it) for op in extra]
        for _ in into:
            next(it)
        _, pull = jax.vjp(lambda *a: fn(info, *a), *tok_v, *par_v)
        grads = pull(tuple(cot_v))
        for i in range(n_tok):
            r = next(it)
            g = grads[i] + add_v[0] if (i == 0 and add_v) else grads[i]
            if r.shape[-1] > g.shape[-1]:
                g = jnp.concatenate([g, jnp.zeros((g.shape[0], r.shape[-1] - g.shape[-1]), g.dtype)], axis=1)
            r[...] = g.astype(r.dtype)
        first = pl.program_id(1) == 0
        for i in range(len(par)):
            r = next(it)
            g = grads[n_tok + i]

            @pl.when(first)
            def _(r=r, g=g):
                r[...] = g

            @pl.when(jnp.logical_not(first))
            def _(r=r, g=g):
                r[...] += g

    ops = tok + par + flat_cots + extra
    n_in = len(_op_arrays(ops))
    return pl.pallas_call(
        body, grid=grid, in_specs=_op_specs(ops) + [HBM] * len(into), out_specs=[op.spec() for op in out_ops],
        out_shape=[jax.ShapeDtypeStruct(*op.arr) for op in out_ops],
        input_output_aliases={n_in + k: i for k, i in enumerate(into)},
        compiler_params=_cp(sem), name=name)(*_op_arrays(ops), *into.values())


def _silu(x):
    return x * jax.nn.sigmoid(x)


def _rms(x):
    return x * lax.rsqrt(jnp.mean(x * x, axis=-1, keepdims=True) + EPS)


@functools.partial(jax.custom_vjp, nondiff_argnums=(1,))
def _shift_rows(x, k):
    return pltpu.roll(x, k % x.shape[0], 0)


def _shift_rows_fwd(x, k):
    return _shift_rows(x, k), None


def _shift_rows_bwd(k, _, g):
    return (_shift_rows(g, -k),)


_shift_rows.defvjp(_shift_rows_fwd, _shift_rows_bwd)


def _prenorm_fn(nctx_t, info, x, w, sc8, sh8):
    is_ctx = info[1] < nctx_t
    sc = jnp.where(is_ctx, sc8[1:2], sc8[0:1])
    sh = jnp.where(is_ctx, sh8[1:2], sh8[0:1])
    return (_rms(x) * w * (1.0 + sc) + sh,)


def _conv_fn(lc, mode, info, x, w, b):
    n, c = x.shape
    g = info[1] * n + lax.broadcasted_iota(jnp.int32, (n, 1), 0)
    is_ctx = g < lc
    rr = jnp.where(is_ctx, g, g % GRID_W)
    first = rr == 0
    last = rr == jnp.where(is_ctx, lc - 1, GRID_W - 1)
    prev = jnp.where(first, 0.0, _shift_rows(x, 1))
    nxt = jnp.where(last, 0.0, _shift_rows(x, -1))
    y = b + prev * w[0:1] + x * w[1:2] + nxt * w[2:3]
    y = _silu(y)
    if mode == 'none':
        return (y,)
    scale = GDN_DK ** -0.5 if mode == 'q' else 1.0
    outs = []
    for h in range(c // 128):
        yh = y[:, h * 128:(h + 1) * 128]
        outs.append(yh * lax.rsqrt(jnp.sum(yh * yh, axis=-1, keepdims=True) + EPS) * scale)
    return (jnp.concatenate(outs, axis=1),)


def _act_fn(info, x, p0, p1):
    lane = lax.broadcasted_iota(jnp.int32, x.shape, 1)
    sp = jax.nn.softplus(x + p1)
    g = -jnp.exp(p0) * sp
    bt = jax.nn.sigmoid(x)
    return (jnp.where(lane < 16, g, jnp.where(lane < 32, bt, jnp.where(lane < 96, sp, 0.0))),)


def _mixg_fn(info, o, zg, gnw):
    outs = []
    for h in range(GDN_HEADS):
        outs.append(_rms(o[:, h * 128:(h + 1) * 128]) * gnw)
    return (jnp.concatenate(outs, axis=1) * _silu(zg),)


def _mixs_fn(info, y, xs, zs, dl, snw):
    yy = (y + dl * xs) * _silu(zs)
    outs = []
    for g in range(4):
        outs.append(_rms(yy[:, g * 512:(g + 1) * 512]))
    return (jnp.concatenate(outs, axis=1) * snw,)


def _merge_fn(info, gates, pg, ps):
    half = gates.shape[1] // 2
    return (jax.nn.sigmoid(gates[:, :half]) * pg + jax.nn.sigmoid(gates[:, half:]) * ps,)


def _norm2_fn(info, xt, mo, g8, w, sc8, sh8):
    h1 = xt + g8[0:1] * mo
    return (h1, _rms(h1) * w * (1.0 + sc8[0:1]) + sh8[0:1])


def _swiglu_fn(info, u2):
    return (_silu(u2[0]) * u2[1],)


_NN = (((1,), (0,)), ((), ()))
_NT = (((1,), (1,)), ((), ()))
_TN = (((0,), (0,)), ((), ()))


def _mmh(a, b):
    return lax.dot_general(a, b, _NN, precision=HI, preferred_element_type=F32)


def _dot1(a, b, dims):
    return lax.dot_general(a.astype(_SCAN_DTYPE), b.astype(_SCAN_DTYPE), dims, preferred_element_type=F32)


def _mm(a, b):
    return _dot1(a, b, _NN)


def _mm_nt(a, b):
    return _dot1(a, b, _NT)


def _mm_tn(a, b):
    return _dot1(a, b, _TN)


def _split2(a):
    hi = a.astype(_SCAN_DTYPE)
    return hi, (a - hi.astype(F32)).astype(_SCAN_DTYPE)


def _dot3(a, b, dims):
    ah, al = _split2(a)
    bh, bl = _split2(b)
    d = lambda u, w: lax.dot_general(u, w, dims, preferred_element_type=F32)
    return d(ah, bh) + (d(ah, bl) + d(al, bh))


def _order_masks(d):
    i = lax.broadcasted_iota(jnp.int32, (CHUNK, CHUNK), 0)
    j = lax.broadcasted_iota(jnp.int32, (CHUNK, CHUNK), 1)
    s = jnp.where(d == 0, 1, -1) * (i - j)
    return (s >= 0).astype(F32), (s > 0).astype(F32)


@jax.custom_vjp
def _unit_tri_inv(mats):
    i = lax.broadcasted_iota(jnp.int32, (CHUNK, CHUNK), 0)
    j = lax.broadcasted_iota(jnp.int32, (CHUNK, CHUNK), 1)
    eye = (i == j).astype(F32)
    ps = [-a for a in mats]
    ts = [eye + p for p in ps]
    for _ in range(5):
        ps = [_dot3(p, p, _NN) for p in ps]
        ts = [t + _dot3(t, p, _NN) for t, p in zip(ts, ps)]
    return tuple(ts)


def _uti_fwd(mats):
    ts = _unit_tri_inv(mats)
    return ts, ts


def _uti_bwd(ts, gs):
    inner = [_dot3(g, t, _NT) for g, t in zip(gs, ts)]
    return (tuple(-_dot3(t, u, _TN) for t, u in zip(ts, inner)),)


_unit_tri_inv.defvjp(_uti_fwd, _uti_bwd)


@jax.custom_vjp
def _unit_tri_inv_given(mats, ts):
    return ts


def _utig_fwd(mats, ts):
    return ts, ts


def _utig_bwd(ts, gs):
    return _uti_bwd(ts, gs)[0], tuple(jnp.zeros_like(t) for t in ts)


_unit_tri_inv_given.defvjp(_utig_fwd, _utig_bwd)


def _lane_col(blk, lane_idx):
    lane = lax.broadcasted_iota(jnp.int32, blk.shape, 1)
    return jnp.sum(jnp.where(lane == lane_idx, blk, 0.0), axis=1, keepdims=True)


def _decay_mat(cum, incl):
    cb = jnp.broadcast_to(cum, (CHUNK, CHUNK))
    return jnp.exp(jnp.minimum(cb - cb.T, 0.0)) * incl


def _gdn_chunks(dirs, streams, states, aux=None, want_aux=False, group=None):
    ns = len(dirs)
    masks = [_order_masks(d) for d in dirs]
    cum = [_mmh(masks[i][0], streams[i][3]) for i in range(ns)]
    tot = [jnp.sum(streams[i][3], axis=0, keepdims=True) for i in range(ns)]
    all_units = [(i, h) for i in range(ns) for h in range(GDN_HEADS)]
    group = group or len(all_units)
    cat = jnp.concatenate
    outs, new_states, ts_all = [], [], []
    for g0 in range(0, len(all_units), group):
        units = all_units[g0:g0 + group]
        us = range(len(units))
        st = states[g0:g0 + group]
        qs = [streams[i][0][:, h * 128:(h + 1) * 128] for i, h in units]
        ks = [streams[i][1][:, h * 128:(h + 1) * 128] for i, h in units]
        vs = [streams[i][2][:, h * 128:(h + 1) * 128] for i, h in units]
        gcum = [_lane_col(cum[i], dirs[i] * GDN_HEADS + h) for i, h in units]
        glast = [_lane_col(tot[i], dirs[i] * GDN_HEADS + h) for i, h in units]
        beta = [_lane_col(streams[i][3], 16 + dirs[i] * GDN_HEADS + h) for i, h in units]
        decay = [_decay_mat(gcum[u], masks[units[u][0]][0]) for u in us]
        egc = [jnp.exp(gcum[u]) for u in us]
        kb = [ks[u] * beta[u] for u in us]
        kq = [_mm_nt(cat([kb[u], qs[u]], axis=0), ks[u]) for u in us]
        mats = tuple(kq[u][:CHUNK] * decay[u] * masks[units[u][0]][1] for u in us)
        ts = _unit_tri_inv(mats) if aux is None else _unit_tri_inv_given(mats, tuple(aux[g0:g0 + group]))
        wu = [_mm(ts[u], cat([kb[u] * egc[u], vs[u] * beta[u]], axis=1)) for u in us]
        ws = [_mm(cat([wu[u][:, :128], qs[u] * egc[u]], axis=0), st[u]) for u in us]
        vn = [wu[u][:, 128:] - ws[u][:CHUNK] for u in us]
        outs += [ws[u][CHUNK:] + _mm(kq[u][CHUNK:] * decay[u], vn[u]) for u in us]
        new_states += [st[u] * jnp.exp(glast[u]) + _mm_tn(ks[u] * jnp.exp(glast[u] - gcum[u]), vn[u]) for u in us]
        ts_all += list(ts)
    per_stream = [cat(outs[i * GDN_HEADS:(i + 1) * GDN_HEADS], axis=1) for i in range(ns)]
    return (*per_stream, *new_states, *(ts_all if want_aux else ()))


def _gdn_step(d, q, k, v, spb, *states, aux=None, want_aux=False):
    return _gdn_chunks([d], [(q, k, v, spb)], states, aux, want_aux, group=4)


def _gdn_step2(q0, k0, v0, sp0, q1, k1, v1, sp1, *states, aux=None, want_aux=False):
    return _gdn_chunks([0, 1], [(q0, k0, v0, sp0), (q1, k1, v1, sp1)], states, aux, want_aux)


def _split3(a):
    a1 = a.astype(_SCAN_DTYPE)
    r = a - a1.astype(F32)
    a2 = r.astype(_SCAN_DTYPE)
    return a1, a2, (r - a2.astype(F32)).astype(_SCAN_DTYPE)


def _exact_dot(a, e, dims, split_lhs, passes=3):
    parts = _split3(a if split_lhs else e)[:passes]
    d = (lambda u: lax.dot_general(u, e, dims, preferred_element_type=F32)) if split_lhs else \
        (lambda u: lax.dot_general(a, u, dims, preferred_element_type=F32))
    return functools.reduce(lambda u, w: u + w, [d(p) for p in reversed(parts)])


@jax.custom_vjp
def _spread(a, e):
    return _exact_dot(a, e, _NN, True)


def _spread_fwd(a, e):
    return _spread(a, e), e


def _spread_bwd(e, g):
    return _exact_dot(g, e, _NT, True, passes=2), jnp.zeros_like(e)


_spread.defvjp(_spread_fwd, _spread_bwd)


@jax.custom_vjp
def _colsum_bcast(z):
    return _exact_dot(jnp.ones((z.shape[0], z.shape[0]), _SCAN_DTYPE), z, _NN, False)


def _colsum_fwd(z):
    return _colsum_bcast(z), None


def _colsum_bwd(_, g):
    return (_exact_dot(jnp.ones((g.shape[0], g.shape[0]), _SCAN_DTYPE), g, _NN, False, passes=2),)


_colsum_bcast.defvjp(_colsum_fwd, _colsum_bwd)


def _ssd_consts():
    wdt = SSM_HEADS * 64
    d = lax.broadcasted_iota(jnp.int32, (2, 128, wdt), 0)
    e = (lax.broadcasted_iota(jnp.int32, (2, 128, wdt), 1)
         == 32 + d * SSM_HEADS + lax.broadcasted_iota(jnp.int32, (2, 128, wdt), 2) // 64).astype(_SCAN_DTYPE)
    dd = lax.broadcasted_iota(jnp.int32, (2, CHUNK, wdt), 0)
    ci = lax.broadcasted_iota(jnp.int32, (2, CHUNK, wdt), 1)
    pos = lax.broadcasted_iota(jnp.int32, (2, CHUNK, wdt), 2) % 64
    incl_t = (jnp.where(dd == 0, 1, -1) * (ci - pos) >= 0).astype(F32)
    diag = (ci == pos).astype(F32)
    return [e, incl_t, diag]


def _ssd_step(d, x, bm, cm, spb, alog, *states, consts):
    e, incl_t, diag = consts
    incl, _ = _order_masks(d)
    lane1 = lax.broadcasted_iota(jnp.int32, (1, 128), 1)
    lo_lane = 32 + d * SSM_HEADS
    a_vec = jnp.where(lane1 >= lo_lane, jnp.where(lane1 < lo_lane + SSM_HEADS, -jnp.exp(alog), 0.0), 0.0)
    adt = spb * a_vec
    acum = _mmh(incl, adt)
    alast = jnp.sum(adt, axis=0, keepdims=True)
    dt2 = _spread(spb, e)
    ac2 = _spread(acum, e)
    al2 = _spread(jnp.broadcast_to(alast, (8, 128)), e)[0:1]
    row = _colsum_bcast(ac2 * diag)
    seg = jnp.exp(jnp.minimum(ac2 - row, 0.0)) * incl_t
    xdt = x * dt2
    gam = jnp.exp(ac2)
    xe = xdt * jnp.exp(al2 - ac2)
    low = lax.broadcasted_iota(jnp.int32, (CHUNK, 128), 1) < 64
    row_low = lax.broadcasted_iota(jnp.int32, (128, 1), 0) < 64
    ps = range(SSM_PAIRS)
    sl = [slice(p * 128, (p + 1) * 128) for p in ps]
    bg = [bm[:, g * 128:(g + 1) * 128] for g in range(4)]
    cg = [cm[:, g * 128:(g + 1) * 128] for g in range(4)]
    cb2 = [_mm_nt(cg[g], jnp.concatenate([bg[g], bg[g]], axis=0)) for g in range(4)]
    ys, new_states = [], []
    for p0 in range(0, SSM_PAIRS, SSD_GROUP):
        pg = range(p0, p0 + SSD_GROUP)
        xd = {p: jnp.concatenate([jnp.where(low, xdt[:, sl[p]], 0.0), jnp.where(low, 0.0, xdt[:, sl[p]])], axis=0)
              for p in pg}
        yd = {p: _mm(cb2[p // 4] * seg[:, sl[p]], xd[p]) for p in pg}
        yo = {p: _mm_nt(cg[p // 4], states[p]) for p in pg}
        ys += [yd[p] + gam[:, sl[p]] * yo[p] for p in pg]
        new = {p: _mm_tn(xe[:, sl[p]], bg[p // 4]) for p in pg}
        al0 = {p: _lane_col(alast, lo_lane + 2 * p) for p in pg}
        al1 = {p: _lane_col(alast, lo_lane + 2 * p + 1) for p in pg}
        new_states += [states[p] * jnp.exp(jnp.where(row_low, al0[p], al1[p])) + new[p] for p in pg]
    return (jnp.concatenate(ys, axis=1), *new_states)


def _chunk_of(d, p, nctx, nc):
    return jnp.where(d == 0, p, jnp.where(p < nctx, nctx - 1 - p, nctx + nc - 1 - p))


class _NoRide:
    n, arrs, in_specs, out_specs, out_shape, scratch = 0, [], [], [], [], []


def _const_specs(consts):
    return [pl.BlockSpec((1,) + a.shape[1:], lambda d, p: (d,) + (0,) * (a.ndim - 1)) for a in consts]


def _scan_fwd(name, step, toks, pars, consts, out_width, n_state, nctx):
    t = toks[0].shape[0]
    nc = t // CHUNK
    n_tok, n_par, n_const = len(toks), len(pars), len(consts)

    def body(*refs):
        it = iter(refs)
        tok_refs = [next(it) for _ in range(n_tok)]
        par_refs = [next(it) for _ in range(n_par)]
        const_refs = [next(it) for _ in range(n_const)]
        o_ref, ss_ref, s_scr = next(it), next(it), next(it)
        d, p = pl.program_id(0), pl.program_id(1)

        @pl.when(p == 0)
        def _():
            s_scr[...] = jnp.zeros(s_scr.shape, F32)

        ss_ref[0, 0] = s_scr[...]
        res = step(d, *[r[...] for r in tok_refs], *[r[...] for r in par_refs], *[s_scr[h] for h in range(n_state)],
                   consts=[r[0] for r in const_refs])
        o_ref[0] = res[0]
        for h in range(n_state):
            s_scr[h] = res[1 + h]

    ch = lambda d, p: _chunk_of(d, p, nctx, nc)
    in_specs = [pl.BlockSpec((CHUNK, a.shape[1]), lambda d, p: (ch(d, p), 0)) for a in toks]
    in_specs += [pl.BlockSpec(a.shape, lambda d, p: (0, 0)) for a in pars]
    return pl.pallas_call(
        body, grid=(2, nc), in_specs=in_specs + _const_specs(consts),
        out_specs=[pl.BlockSpec((1, CHUNK, out_width), lambda d, p: (d, ch(d, p), 0)),
                   pl.BlockSpec((1, 1, n_state, 128, 128), lambda d, p: (d, p, 0, 0, 0))],
        out_shape=[jax.ShapeDtypeStruct((2, t, out_width), F32),
                   jax.ShapeDtypeStruct((2, nc, n_state, 128, 128), F32)],
        scratch_shapes=[pltpu.VMEM((n_state, 128, 128), F32)],
        compiler_params=_cp(("arbitrary", "arbitrary")), name=name)(*toks, *pars, *consts)


def _scan_bwd(name, step, toks, pars, ss, dout, n_state, nctx, ride=None, aux=None, consts=()):
    t = toks[0].shape[0]
    nc = t // CHUNK
    n_tok, n_par, n_const = len(toks), len(pars), len(consts)
    rd = ride if ride is not None else _NoRide
    n_aux = aux.shape[2] if aux is not None else 0

    def body(*refs):
        it = iter(refs)
        tok_refs = [next(it) for _ in range(n_tok)]
        par_refs = [next(it) for _ in range(n_par)]
        ss_ref, do_ref = next(it), next(it)
        aux_ref = next(it) if n_aux else None
        const_refs = [next(it) for _ in range(n_const)]
        ride_in = [next(it) for _ in range(rd.n)]
        dtok_refs = [next(it) for _ in range(n_tok)]
        dpar_refs = [next(it) for _ in range(n_par)]
        ride_out = [next(it) for _ in range(rd.n)]
        ds_scr = next(it)
        sems = list(it)
        d, pr = pl.program_id(0), pl.program_id(1)
        if ride is not None:
            @pl.when((d == 0) & (pr == 0))
            def _():
                ride.start(ride_in, ride_out, sems)

            @pl.when((d == 1) & (pr == nc - 1))
            def _():
                ride.wait(ride_in, ride_out, sems)

        @pl.when(pr == 0)
        def _():
            ds_scr[...] = jnp.zeros(ds_scr.shape, F32)

        kw = dict(aux=[aux_ref[0, 0, i] for i in range(n_aux)]) if n_aux else {}
        if n_const:
            kw['consts'] = [r[0] for r in const_refs]
        _, pull = jax.vjp(functools.partial(step, d, **kw), *[r[...] for r in tok_refs], *[r[...] for r in par_refs],
                          *[ss_ref[0, 0, h] for h in range(n_state)])
        grads = pull((do_ref[...], *[ds_scr[h] for h in range(n_state)]))
        for r, g in zip(dtok_refs, grads[:n_tok]):
            r[0] = g
        for h in range(n_state):
            ds_scr[h] = grads[n_tok + n_par + h]
        first = (d == 0) & (pr == 0)
        for r, g in zip(dpar_refs, grads[n_tok:n_tok + n_par]):
            @pl.when(first)
            def _(r=r, g=g):
                r[...] = g

            @pl.when(jnp.logical_not(first))
            def _(r=r, g=g):
                r[...] += g

    ch = lambda d, pr: _chunk_of(d, nc - 1 - pr, nctx, nc)
    in_specs = [pl.BlockSpec((CHUNK, a.shape[1]), lambda d, pr: (ch(d, pr), 0)) for a in toks]
    in_specs += [pl.BlockSpec(a.shape, lambda d, pr: (0, 0)) for a in pars]
    in_specs += [pl.BlockSpec((1, 1, n_state, 128, 128), lambda d, pr: (d, nc - 1 - pr, 0, 0, 0)),
                 pl.BlockSpec((CHUNK, dout.shape[1]), lambda d, pr: (ch(d, pr), 0))]
    if n_aux:
        in_specs += [pl.BlockSpec((1, 1, n_aux, CHUNK, CHUNK), lambda d, pr: (d, nc - 1 - pr, 0, 0, 0))]
    in_specs += _const_specs(consts)
    out_specs = [pl.BlockSpec((1, CHUNK, a.shape[1]), lambda d, pr: (d, ch(d, pr), 0)) for a in toks]
    out_specs += [pl.BlockSpec(a.shape, lambda d, pr: (0, 0)) for a in pars]
    out_shape = [jax.ShapeDtypeStruct((2, t, a.shape[1]), F32) for a in toks]
    out_shape += [jax.ShapeDtypeStruct(a.shape, F32) for a in pars]
    return pl.pallas_call(
        body, grid=(2, nc), in_specs=in_specs + rd.in_specs, out_specs=out_specs + rd.out_specs,
        out_shape=out_shape + rd.out_shape, scratch_shapes=[pltpu.VMEM((n_state, 128, 128), F32)] + rd.scratch,
        compiler_params=_cp(("arbitrary", "arbitrary")), name=name)(
            *toks, *pars, ss, dout, *([aux] if n_aux else []), *consts, *rd.arrs)


def _scan2_fwd(name, step2, toks, out_width, n_state, nctx, n_aux, ride=None):
    t = toks[0].shape[0]
    nc = t // CHUNK
    n_tok = len(toks)
    rd = ride if ride is not None else _NoRide

    def body(*refs):
        it = iter(refs)
        tok_refs = [next(it) for _ in range(2 * n_tok)]
        ride_in = [next(it) for _ in range(rd.n)]
        o_refs = [next(it), next(it)]
        ss_ref, aux_ref = next(it), next(it)
        ride_out = [next(it) for _ in range(rd.n)]
        s_scr = next(it)
        sems = list(it)
        p = pl.program_id(0)
        if ride is not None:
            @pl.when(p == 0)
            def _():
                ride.start(ride_in, ride_out, sems)

        @pl.when(p == 0)
        def _():
            s_scr[...] = jnp.zeros(s_scr.shape, F32)

        for d in range(2):
            ss_ref[d, 0] = s_scr[d * n_state:(d + 1) * n_state]
        res = step2(*[r[...] for r in tok_refs], *[s_scr[u] for u in range(2 * n_state)], want_aux=True)
        for d in range(2):
            o_refs[d][...] = res[d]
            for i in range(n_aux):
                aux_ref[d, 0, i] = res[2 + 2 * n_state + d * n_aux + i]
        for u in range(2 * n_state):
            s_scr[u] = res[2 + u]
        if ride is not None:
            @pl.when(p == nc - 1)
            def _():
                ride.wait(ride_in, ride_out, sems)

    def tok_spec(a, d):
        return pl.BlockSpec((CHUNK, a.shape[1]), lambda p: (_chunk_of(d, p, nctx, nc), 0))

    return pl.pallas_call(
        body, grid=(nc,), in_specs=[tok_spec(a, d) for d in range(2) for a in toks] + rd.in_specs,
        out_specs=[pl.BlockSpec((CHUNK, out_width), lambda p: (_chunk_of(0, p, nctx, nc), 0)),
                   pl.BlockSpec((CHUNK, out_width), lambda p: (_chunk_of(1, p, nctx, nc), 0)),
                   pl.BlockSpec((2, 1, n_state, 128, 128), lambda p: (0, p, 0, 0, 0)),
                   pl.BlockSpec((2, 1, n_aux, CHUNK, CHUNK), lambda p: (0, p, 0, 0, 0))] + rd.out_specs,
        out_shape=[jax.ShapeDtypeStruct((t, out_width), F32), jax.ShapeDtypeStruct((t, out_width), F32),
                   jax.ShapeDtypeStruct((2, nc, n_state, 128, 128), F32),
                   jax.ShapeDtypeStruct((2, nc, n_aux, CHUNK, CHUNK), F32)] + rd.out_shape,
        scratch_shapes=[pltpu.VMEM((2 * n_state, 128, 128), F32)] + rd.scratch,
        compiler_params=_cp(("arbitrary",)), name=name)(*toks, *toks, *rd.arrs)


ADA_TN = 512


def _ada_part(cc16, w_shard):
    n = w_shard.shape[1]

    def body(c_ref, w_ref, o_ref):
        s = _silu(c_ref[...]).astype(_MXU_DTYPE)
        o_ref[...] = jnp.dot(s, w_ref[...].astype(_MXU_DTYPE), preferred_element_type=F32)

    return pl.pallas_call(
        body, grid=(n // ADA_TN,),
        in_specs=[pl.BlockSpec((16, D_MODEL), lambda j: (0, 0)), pl.BlockSpec((D_MODEL, ADA_TN), lambda j: (0, j))],
        out_specs=pl.BlockSpec((16, ADA_TN), lambda j: (0, j)), out_shape=jax.ShapeDtypeStruct((16, n), F32),
        compiler_params=_cp(("parallel",)), name="ada_part")(cc16, w_shard)


def _ada_bwd_shard(cc16, w_shard, d_lat, d_ctx):
    n = w_shard.shape[1]
    nj = n // ADA_TN

    def body(c_ref, w_ref, gl_ref, gc_ref, dw_ref, dc_ref):
        j = pl.program_id(0)
        s, pull = jax.vjp(_silu, c_ref[...])
        row = lax.broadcasted_iota(jnp.int32, (8, 1), 0)
        dctx = jnp.where(row == 0, jnp.sum(gc_ref[...], axis=0, keepdims=True), 0.0)
        mxu = lambda v: v.astype(_MXU_DTYPE)
        dw_ref[...] = (lax.dot_general(mxu(s[0:8]), mxu(gl_ref[...]), _TN, preferred_element_type=F32)
                       + lax.dot_general(mxu(s[8:16]), mxu(dctx), _TN, preferred_element_type=F32))
        ds = lax.dot_general(mxu(dctx), mxu(w_ref[...]), _NT, preferred_element_type=F32)

        @pl.when(j == 0)
        def _():
            dc_ref[...] = ds

        @pl.when(j > 0)
        def _():
            dc_ref[...] += ds

        @pl.when(j == nj - 1)
        def _():
            ct = jnp.concatenate([jnp.zeros((8, D_MODEL), F32), dc_ref[...]], axis=0)
            dc_ref[...] = 0.5 * pull(ct)[0][8:16]

    tile = pl.BlockSpec((8, ADA_TN), lambda j: (0, j))
    wspec = pl.BlockSpec((D_MODEL, ADA_TN), lambda j: (0, j))
    return pl.pallas_call(
        body, grid=(nj,), in_specs=[pl.BlockSpec((16, D_MODEL), lambda j: (0, 0)), wspec, tile, tile],
        out_specs=[wspec, pl.BlockSpec((8, D_MODEL), lambda j: (0, 0))],
        out_shape=[jax.ShapeDtypeStruct((D_MODEL, n), F32), jax.ShapeDtypeStruct((8, D_MODEL), F32)],
        compiler_params=_cp(("arbitrary",)), name="ada_bwd_shard")(cc16, w_shard, d_lat, d_ctx)


def _rowsum2(a, b):
    def body(a_ref, b_ref, o_ref):
        o_ref[...] = jnp.sum(a_ref[...], axis=0, keepdims=True) + jnp.sum(b_ref[...], axis=0, keepdims=True)

    return pl.pallas_call(body, out_shape=jax.ShapeDtypeStruct((1, a.shape[1]), F32), name="ada_b_grad")(a, b)


def _tail(h1, ff, mods, wf, tgt, nctx_t, tl):
    t = h1.shape[0]
    nt = t // tl

    def loss_fn(valid, h1v, ffv, g8, w, tg):
        h2 = h1v + g8[0:1] * ffv
        y = _rms(h2) * w
        err = (y - tg) ** 2
        return 0.5 * jnp.sum(jnp.mean(err, axis=-1, keepdims=True), axis=0, keepdims=True) * valid

    def body(h1_ref, ff_ref, g_ref, w_ref, t_ref, loss_ref, dh_ref, dff_ref, dg_ref, dw_ref):
        i = pl.program_id(0)
        valid = jnp.where(i < nctx_t, 0.0, 1.0)
        lv, pull = jax.vjp(functools.partial(loss_fn, valid), h1_ref[...], ff_ref[...].astype(F32), g_ref[...],
                           w_ref[...], t_ref[...])
        dh, dff, dg, dw, _ = pull(jnp.ones((1, 1), F32))
        dh_ref[...] = dh
        dff_ref[...] = dff.astype(dff_ref.dtype)
        lb = jnp.broadcast_to(lv, loss_ref.shape)

        @pl.when(i == 0)
        def _():
            loss_ref[...] = lb
            dg_ref[...] = dg
            dw_ref[...] = dw

        @pl.when(i > 0)
        def _():
            loss_ref[...] += lb
            dg_ref[...] += dg
            dw_ref[...] += dw

    tok = pl.BlockSpec((tl, D_MODEL), lambda i: (i, 0))
    return pl.pallas_call(
        body, grid=(nt,),
        in_specs=[tok, tok, pl.BlockSpec((8, D_MODEL), lambda i: (0, 5)), pl.BlockSpec((1, D_MODEL), lambda i: (0, 0)),
                  pl.BlockSpec((tl, D_MODEL), lambda i: (jnp.maximum(i - nctx_t, 0), 0))],
        out_specs=[pl.BlockSpec((8, 128), lambda i: (0, 0)), tok, tok, pl.BlockSpec((8, D_MODEL), lambda i: (0, 0)),
                   pl.BlockSpec((1, D_MODEL), lambda i: (0, 0))],
        out_shape=[jax.ShapeDtypeStruct((8, 128), F32), jax.ShapeDtypeStruct((t, D_MODEL), F32),
                   jax.ShapeDtypeStruct((t, D_MODEL), _MXU_DTYPE), jax.ShapeDtypeStruct((8, D_MODEL), F32),
                   jax.ShapeDtypeStruct((1, D_MODEL), F32)],
        compiler_params=_cp(("arbitrary",)), name="tail_loss")(h1, ff, mods, wf, tgt)


def _pack_w_in(w4):
    ns = w4.shape[2]
    placed = []
    for s0, s1, p0 in IN_SEGMENTS:
        for j in range(N_PLANE):
            lo, hi = max(s0, j * ns), min(s1, (j + 1) * ns)
            if lo < hi:
                placed.append((p0 + lo - s0, w4[j][:, lo - j * ns:hi - j * ns]))
    placed.sort(key=lambda e: e[0])
    pieces, end = [], 0
    for pos, piece in placed:
        assert pos == end, (pos, end)
        pieces.append(piece)
        end = pos + piece.shape[1]
    pieces.append(jnp.zeros((w4.shape[1], P_TOTAL - end), w4.dtype))
    return jnp.concatenate(pieces, axis=1)


def _unpack_w_in(g):
    ns = D_IN_PROJ // N_PLANE
    shards = []
    for j in range(N_PLANE):
        pieces = []
        for s0, s1, p0 in IN_SEGMENTS:
            lo, hi = max(s0, j * ns), min(s1, (j + 1) * ns)
            if lo < hi:
                pieces.append(g[:, p0 + lo - s0:p0 + hi - s0])
        shards.append(jnp.concatenate(pieces, axis=1))
    return jnp.stack(shards)


LATE_WEIGHTS = ['w_br_gdn', 'w_br_ssm', 'w_out', 'w_ffn_in', 'w_ffn_out']
COL_STACKED = ('w_in', 'w_ffn_in')


def _from_shards(n, g):
    if n in COL_STACKED:
        return g
    if SHARD_AXIS[n] == 0:
        return g.reshape(N_PLANE * g.shape[1], g.shape[2])
    return jnp.concatenate([g[j] for j in range(N_PLANE)], axis=1)


def _to_dest_blocks(n, g):
    if g.ndim == 3:
        return g
    if SHARD_AXIS[n] == 0:
        return g.reshape(N_PLANE, g.shape[0] // N_PLANE, g.shape[1])
    sz = g.shape[1] // N_PLANE
    return jnp.stack([g[:, j * sz:(j + 1) * sz] for j in range(N_PLANE)])


def _local_step(x, c, ctx, tgt, W, late_shards):
    lc, l = ctx.shape[0], x.shape[0]
    t = lc + l
    tl = 256
    assert lc == tl and l % tl == 0 and lc % CHUNK == 0
    nt, nctx_t, nctx = t // tl, lc // tl, lc // CHUNK
    act = _MXU_DTYPE
    r1 = lambda v: v.reshape(1, -1)

    xt = jnp.concatenate([ctx, x], axis=0)
    lin = 4 * lax.axis_index("x") + 2 * lax.axis_index("y") + lax.axis_index("c")
    c_all = _gather_all("gather_c", jnp.broadcast_to(c, (8, D_MODEL)))[:, 0]
    cc16 = jnp.concatenate([c_all, r1(W['c_ctx']), jnp.zeros((7, D_MODEL), F32)], axis=0)
    (parts,) = _plane_exchange("gather_mods", [_ada_part(cc16, W['ada_w'])], gather=True)
    mods_all = jnp.transpose(parts, (1, 0, 2)).reshape(16, -1) + r1(W['ada_b'])
    mods = jnp.concatenate([lax.dynamic_slice_in_dim(mods_all, lin, 1, axis=0), mods_all[8:9],
                            jnp.zeros((6, mods_all.shape[1]), F32)], axis=0)

    def mod(kk):
        return Op(mods, (8, D_MODEL), lambda j, i, kk=kk: (0, kk))

    def tokop(arr, w=D_MODEL, off=0, tl_=tl):
        if not isinstance(arr, list) and arr.ndim == 3:
            return Op(arr, (2, tl_, w), lambda j, i: (0, i, off + j), load=_sum_dirs)
        return Op(arr, (tl_, w), lambda j, i: (i, off + j))

    def outop(n, dtype, w=D_MODEL, tl_=tl):
        return Op(((t, n), dtype), (tl_, w), lambda j, i: (i, j))

    def parop(arr, w, off=0):
        return Op(arr, (arr.shape[0], w), lambda j, i: (0, off + j))

    def parout(rows, n, w):
        return Op(((rows, n), F32), (rows, w), lambda j, i: (0, j))

    n1w = r1(W['norm1_w'])
    pre_fn = functools.partial(_prenorm_fn, nctx_t)
    pre_tok, pre_par = [tokop(xt)], [parop(n1w, D_MODEL), mod(1), mod(0)]
    (a,) = _tw_fwd("prenorm_fwd", pre_fn, (1, nt), pre_tok + pre_par, [outop(D_MODEL, act)])
    wp = _pack_w_in(W['w_in'])
    proj = _matmul(a, wp, 'nn', "in_proj")

    gcw, gcb = W['gdn_conv_w'], r1(W['gdn_conv_b'])
    scw, scb = W['ssm_conv_w'], r1(W['ssm_conv_b'])
    conv_parts = {}

    tlf = 768 if t % 768 == 0 else tl

    def conv_part(name, mode, poff, cw, cb, woff, width):
        fn = functools.partial(_conv_fn, lc, mode)
        bw = min(width, 1024)
        par_ = [parop(cw, bw, woff // bw), parop(cb, bw, woff // bw)]
        conv_parts[name] = (fn, [tokop(proj, bw, poff // bw)], par_, width, bw, poff)
        (o,) = _tw_fwd("conv_" + name, fn, (width // bw, t // tlf), [tokop(proj, bw, poff // bw, tlf)] + par_,
                       [outop(width, F32, bw, tlf)])
        return o

    q = conv_part('q', 'q', P_QKV, gcw, gcb, 0, 1024)
    k = conv_part('k', 'k', P_QKV + 1024, gcw, gcb, 1024, 1024)
    v = conv_part('v', 'none', P_QKV + 2048, gcw, gcb, 2048, 1024)
    xs = conv_part('xs', 'none', P_XBC, scw, scb, 0, 2048)
    bm = conv_part('bm', 'none', P_XBC + 2048, scw, scb, 2048, 512)
    cm = conv_part('cm', 'none', P_XBC + 2560, scw, scb, 2560, 512)

    z16, z32 = jnp.zeros((16,), F32), jnp.zeros((32,), F32)
    p0 = jnp.concatenate([W['gdn_a_log'].reshape(-1), jnp.zeros((112,), F32)]).reshape(1, 128)
    p1 = jnp.concatenate([W['gdn_dt_bias'].reshape(-1), z16, W['ssm_dt_bias'].reshape(-1), z32]).reshape(1, 128)
    alog = jnp.concatenate([z32, W['ssm_a_log'].reshape(-1), z32]).reshape(1, 128)
    act_tok, act_par = [tokop(proj, 128, P_SMALL // 128)], [parop(p0, 128), parop(p1, 128)]
    (sp,) = _tw_fwd("small_act", _act_fn, (1, t // tlf), [tokop(proj, 128, P_SMALL // 128, tlf)] + act_par,
                    [outop(128, F32, 128, tlf)])

    gather_late = PlaneExchange([late_shards[n] for n in LATE_WEIGHTS], gather=True)
    o_f, o_b, ss, tri, *late = _scan2_fwd("gdn_scan_fwd", _gdn_step2, [q, k, v, sp], 1024, GDN_HEADS, nctx,
                                          GDN_HEADS, ride=gather_late)
    W = dict(W, **{n: _from_shards(n, g) for n, g in zip(LATE_WEIGHTS, late)})
    ssd_consts = _ssd_consts()
    y2, hs = _scan_fwd("ssd_scan_fwd", _ssd_step, [xs, bm, cm, sp], [alog], ssd_consts, 2048, SSM_PAIRS, nctx)

    tlm = 128
    ntm = t // tlm
    gnw = r1(W['gdn_norm_w'])
    dl = jnp.repeat(W['ssm_d'], 64).reshape(1, 2048)
    snw = r1(W['ssm_norm_w'])
    mg_tok = [tokop([o_f, o_b], 1024, 0, tlm), tokop(proj, 1024, P_ZG // 1024, tlm)]
    mg_par = [parop(gnw, 128)]
    (og,) = _tw_fwd("mix_gdn", _mixg_fn, (1, nt),
                    [tokop([o_f, o_b], 1024, 0), tokop(proj, 1024, P_ZG // 1024)] + mg_par, [outop(1024, act, 1024)])
    ms_tok = [tokop(y2, 2048, 0, tlm), tokop(xs, 2048, 0, tlm), tokop(proj, 2048, P_ZS // 2048, tlm)]
    ms_par = [parop(dl, 2048), parop(snw, 2048)]
    (yg,) = _tw_fwd("mix_ssm", _mixs_fn, (1, nt),
                    [tokop(y2, 2048, 0), tokop(xs, 2048, 0), tokop(proj, 2048, P_ZS // 2048)] + ms_par,
                    [outop(2048, act, 2048)])

    pg = _matmul(og, W['w_br_gdn'], 'nn', "br_gdn", out_dtype=act)
    ps = _matmul(yg, W['w_br_ssm'], 'nn', "br_ssm", out_dtype=act)
    mr_tok = [tokop(proj, 2048, P_GATE // 2048), tokop(pg), tokop(ps)]
    (mrg,) = _tw_fwd("merge", _merge_fn, (1, t // tlf),
                     [tokop(proj, 2048, P_GATE // 2048, tlf), tokop(pg, tl_=tlf), tokop(ps, tl_=tlf)],
                     [outop(1024, act, tl_=tlf)])
    mo = _matmul(mrg, W['w_out'], 'nn', "w_out")

    n2w = r1(W['norm2_w'])
    n2_tok, n2_par = [tokop(xt), tokop(mo)], [mod(2), parop(n2w, D_MODEL), mod(4), mod(3)]
    h1, f = _tw_fwd("norm2_fwd", _norm2_fn, (1, t // tlf), [tokop(xt, tl_=tlf), tokop(mo, tl_=tlf)] + n2_par,
                    [outop(1024, F32, tl_=tlf), outop(1024, act, tl_=tlf)])
    u2 = _matmul(f, W['w_ffn_in'], 'nn', "ffn_in", out_dtype=act, halves='out')
    swb = D_FF // 2
    both = lambda arr, tl_=tl: Op(arr, (2, tl_, swb), lambda j, i: (0, i, j))
    sw_tok = [both(u2)]
    (sw,) = _tw_fwd("swiglu", _swiglu_fn, (D_FF // swb, t // tlf), [both(u2, tlf)], [outop(D_FF, act, swb, tlf)])
    ff = _matmul(sw, W['w_ffn_out'], 'nn', "ffn_out")

    loss8, dh1, dff, dg2, dnf = _tail(h1, ff, mods, r1(W['norm_f_w']), tgt, nctx_t, tl)
    loss = loss8[0, 0]

    G = {}
    G['norm_f_w'] = dnf.reshape(-1)
    G['w_ffn_out'] = _matmul(sw, dff, 'tn', "d_ffn_out", out_dtype=_GRAD_DTYPE)
    dsw = _matmul(dff, W['w_ffn_out'], 'nt', "d_sw", out_dtype=act)
    (du2,) = _tw_bwd("swiglu_bwd", _swiglu_fn, (D_FF // swb, nt), sw_tok, [], [[tokop(dsw, swb)]],
                     [both(((2, t, D_FF), act))], [])
    G['w_ffn_in'] = _matmul(f, du2, 'tn', "d_ffn_in", out_dtype=_GRAD_DTYPE, stacked_out=True, halves='b')
    df = _matmul(du2, W['w_ffn_in'], 'nt', "d_f", halves='a')
    dxt1, dmo, dg1, dn2, dsc2, dsh2 = _tw_bwd(
        "norm2_bwd", _norm2_fn, (1, nt), n2_tok, n2_par, [[tokop(dh1)], [tokop(df)]],
        [outop(1024, F32), outop(1024, act)],
        [parout(8, 1024, 1024), parout(1, 1024, 1024), parout(8, 1024, 1024), parout(8, 1024, 1024)])
    G['norm2_w'] = dn2.reshape(-1)
    G['w_out'] = _matmul(mrg, dmo, 'tn', "d_w_out", out_dtype=_GRAD_DTYPE)
    dmrg = _matmul(dmo, W['w_out'], 'nt', "d_mrg", out_dtype=act)
    def win(off, w, tl_=tl):
        return Op(((t, P_TOTAL), act), (tl_, w), lambda j, i: (i, off // w + j))

    dproj = lax.empty((t, P_TOTAL), act)
    dproj, dpg, dps = _tw_bwd("merge_bwd", _merge_fn, (1, nt), mr_tok, [], [[tokop(dmrg)]],
                              [win(P_GATE, 2048), outop(1024, act), outop(1024, act)], [], into={0: dproj})
    G['w_br_gdn'] = _matmul(og, dpg, 'tn', "d_br_gdn", out_dtype=_GRAD_DTYPE)
    G['w_br_ssm'] = _matmul(yg, dps, 'tn', "d_br_ssm", out_dtype=_GRAD_DTYPE)
    dog = _matmul(dpg, W['w_br_gdn'], 'nt', "d_og")
    dyg = _matmul(dps, W['w_br_ssm'], 'nt', "d_yg")

    do, dproj, dgnw = _tw_bwd("mix_gdn_bwd", _mixg_fn, (1, ntm), mg_tok, mg_par, [[tokop(dog, 1024, 0, tlm)]],
                              [outop(1024, F32, 1024, tlm), win(P_ZG, 1024, tlm)], [parout(1, 128, 128)],
                              into={1: dproj})
    G['gdn_norm_w'] = dgnw.reshape(-1)
    dy, dxs_a, dproj, ddl, dsnw = _tw_bwd(
        "mix_ssm_bwd", _mixs_fn, (1, ntm), ms_tok, ms_par, [[tokop(dyg, 2048, 0, tlm)]],
        [outop(2048, F32, 2048, tlm), outop(2048, F32, 2048, tlm), win(P_ZS, 2048, tlm)],
        [parout(1, 2048, 2048), parout(1, 2048, 2048)], into={2: dproj})
    G['ssm_d'] = ddl.reshape(SSM_HEADS, 64).sum(axis=1)
    G['ssm_norm_w'] = dsnw.reshape(-1)

    dq2, dk2, dv2, dsp_g = _scan_bwd("gdn_scan_bwd", _gdn_step, [q, k, v, sp], [], ss, do, GDN_HEADS, nctx, aux=tri)
    received = {}
    scatter_late = PlaneExchange([_to_dest_blocks(n, G.pop(n)) for n in LATE_WEIGHTS], gather=False)
    dxs2, dbm2, dcm2, dsp_s, dalog, *got = _scan_bwd("ssd_scan_bwd", _ssd_step, [xs, bm, cm, sp], [alog], hs, dy,
                                                     SSM_PAIRS, nctx, ride=scatter_late, consts=ssd_consts)
    received.update(zip(LATE_WEIGHTS, got))
    G['ssm_a_log'] = dalog[0, 32:96].reshape(2, SSM_HEADS)

    dconv_w, dconv_b = {}, {}

    def conv_bwd(name, cots_, buf):
        fn, tok_, par_, width, bw, poff = conv_parts[name]
        buf, dconv_w[name], dconv_b[name] = _tw_bwd(
            "conv_" + name + "_bwd", fn, (width // bw, nt), tok_, par_, [[tokop(c_, bw) for c_ in cots_]],
            [win(poff, bw)], [parout(3, width, bw), parout(1, width, bw)], into={0: buf})
        return buf

    dproj = conv_bwd('q', [dq2], dproj)
    dproj = conv_bwd('k', [dk2], dproj)
    dproj = conv_bwd('v', [dv2], dproj)
    dproj = conv_bwd('xs', [dxs2, dxs_a], dproj)
    dproj = conv_bwd('bm', [dbm2], dproj)
    dproj = conv_bwd('cm', [dcm2], dproj)
    G['gdn_conv_w'] = jnp.concatenate([dconv_w['q'], dconv_w['k'], dconv_w['v']], axis=1)
    G['gdn_conv_b'] = [dconv_b['q'], dconv_b['k'], dconv_b['v']]
    G['ssm_conv_w'] = jnp.concatenate([dconv_w['xs'], dconv_w['bm'], dconv_w['cm']], axis=1)
    G['ssm_conv_b'] = [dconv_b['xs'], dconv_b['bm'], dconv_b['cm']]

    dproj, dp0, dp1 = _tw_bwd("small_act_bwd", _act_fn, (1, nt), act_tok, act_par,
                              [[tokop(dsp_g, 128), tokop(dsp_s, 128)]], [win(P_SMALL, P_TOTAL - P_SMALL)],
                              [parout(1, 128, 128), parout(1, 128, 128)], into={0: dproj})
    G['gdn_a_log'] = dp0[0, 0:16].reshape(2, GDN_HEADS)
    G['gdn_dt_bias'] = dp1[0, 0:16].reshape(2, GDN_HEADS)
    G['ssm_dt_bias'] = dp1[0, 32:96].reshape(2, SSM_HEADS)

    g_w_in = _unpack_w_in(_matmul(a, dproj, 'tn', "d_w_in", out_dtype=_GRAD_DTYPE))
    da, (received['w_in'],) = _matmul(dproj, wp, 'nt', "d_a", ride=PlaneExchange([g_w_in], gather=False))

    gx_out = Op(((l, D_MODEL), F32), (tl, D_MODEL), lambda j, i: (jnp.maximum(i - nctx_t, 0), 0))
    grad_x, dn1, dsc1, dsh1 = _tw_bwd(
        "prenorm_bwd", pre_fn, (1, nt), pre_tok, pre_par, [[tokop(da)]], [gx_out],
        [parout(1, 1024, 1024), parout(8, 1024, 1024), parout(8, 1024, 1024)], tok_add=tokop(dxt1),
        sem=("arbitrary", "arbitrary"))
    G['norm1_w'] = dn1.reshape(-1)
    dmods = jnp.concatenate([dsh1, dsc1, dg1, dsh2, dsc2, dg2], axis=1)
    dm_all = _gather_all("gather_dmods", dmods)
    ns = W['ada_w'].shape[1]
    mine = lax.dynamic_slice_in_dim(dm_all, (2 * lax.axis_index("x") + lax.axis_index("y")) * ns, ns, axis=2)
    G['ada_w'], dcc = _ada_bwd_shard(cc16, W['ada_w'], mine[:, 0], mine[:, 1])
    G['ada_b'] = jnp.where(lin == 0, _rowsum2(dm_all[:, 0], dm_all[:, 1]).reshape(-1), 0.0)
    G['c_ctx'] = dcc[0]
    return loss, grad_x, G, received


def _row_tile(r, c):
    for tr in (512, 256, 128, 64, 32, 16, 8):
        if r % tr == 0 and tr * c * 4 <= (1 << 20):
            return tr
    return r


def _sum4(name, rv):
    _, r, c = rv.shape
    tr = _row_tile(r, c)

    def body(r_ref, o_ref):
        o_ref[...] = ((r_ref[0].astype(F32) + r_ref[1].astype(F32)) + r_ref[2].astype(F32)) + r_ref[3].astype(F32)

    return pl.pallas_call(
        body, grid=(r // tr,), in_specs=[pl.BlockSpec((N_PLANE, tr, c), lambda i: (0, i, 0))],
        out_specs=pl.BlockSpec((tr, c), lambda i: (i, 0)), out_shape=jax.ShapeDtypeStruct((r, c), F32),
        compiler_params=_cp(("parallel",)), name=name)(rv)


def _adamw(name, w, m, v, p, q=None):
    r, c = w.shape[-2:]
    tr = _row_tile(r, c)
    grads = [p] if q is None else [p, q]
    lead = (0,) * (w.ndim - 2)

    def body(w_ref, m_ref, v_ref, *refs):
        g_ref, d_ref, mo_ref, vo_ref = refs[len(grads):]
        g = refs[0][...] if q is None else refs[0][...] + refs[1][...]
        at = lead if lead else Ellipsis
        mn = ADAM_B1 * m_ref[at] + (1.0 - ADAM_B1) * g
        vn = ADAM_B2 * v_ref[at] + (1.0 - ADAM_B2) * jnp.square(g)
        m_hat = mn / (1.0 - ADAM_B1 ** ADAM_STEP)
        v_hat = vn / (1.0 - ADAM_B2 ** ADAM_STEP)
        g_ref[at] = g
        d_ref[at] = -ADAM_LR * (m_hat / (jnp.sqrt(v_hat) + ADAM_EPS) + ADAM_WD * w_ref[at])
        mo_ref[at] = mn
        vo_ref[at] = vn

    gspec = pl.BlockSpec((tr, c), lambda i: (i, 0))
    wspec = pl.BlockSpec((1,) * len(lead) + (tr, c), lambda i: lead + (i, 0))
    return pl.pallas_call(
        body, grid=(r // tr,), in_specs=[wspec] * 3 + [gspec] * len(grads), out_specs=[wspec] * 4,
        out_shape=[jax.ShapeDtypeStruct(w.shape, F32)] * 4, compiler_params=_cp(("parallel",)), name=name)(w, m, v, *grads)


SMALL_ROWS = 24


def _pack_small(d):
    pieces = [p.reshape(-1) for n in SMALL for p in (d[n] if isinstance(d[n], list) else [d[n]])]
    v = jnp.concatenate(pieces)
    return jnp.pad(v, (0, SMALL_ROWS * 1024 - v.shape[0])).reshape(SMALL_ROWS, 1024)


def _unpack_small(buf, like):
    v = buf.reshape(-1)
    out, off = {}, 0
    for n in SMALL:
        sz = like[n].size
        out[n] = v[off:off + sz].reshape(like[n].shape)
        off += sz
    return out


def kernel(x, c, ctx, c_ctx, ada_w, ada_b, norm1_w, w_in, gdn_conv_w, gdn_conv_b, gdn_a_log, gdn_dt_bias, gdn_norm_w, ssm_conv_w, ssm_conv_b, ssm_a_log, ssm_dt_bias, ssm_d, ssm_norm_w, w_br_gdn, w_br_ssm, w_out, norm2_w, w_ffn_in, w_ffn_out, norm_f_w, loss_target, m_c_ctx, m_ada_w, m_ada_b, m_norm1_w, m_w_in, m_gdn_conv_w, m_gdn_conv_b, m_gdn_a_log, m_gdn_dt_bias, m_gdn_norm_w, m_ssm_conv_w, m_ssm_conv_b, m_ssm_a_log, m_ssm_dt_bias, m_ssm_d, m_ssm_norm_w, m_w_br_gdn, m_w_br_ssm, m_w_out, m_norm2_w, m_w_ffn_in, m_w_ffn_out, m_norm_f_w, v_c_ctx, v_ada_w, v_ada_b, v_norm1_w, v_w_in, v_gdn_conv_w, v_gdn_conv_b, v_gdn_a_log, v_gdn_dt_bias, v_gdn_norm_w, v_ssm_conv_w, v_ssm_conv_b, v_ssm_a_log, v_ssm_dt_bias, v_ssm_d, v_ssm_norm_w, v_w_br_gdn, v_w_br_ssm, v_w_out, v_norm2_w, v_w_ffn_in, v_w_ffn_out, v_norm_f_w):
    args = dict(locals())
    wl = {n: args[n] for n in WEIGHTS}
    ml = {n: args['m_' + n] for n in WEIGHTS}
    vl = {n: args['v_' + n] for n in WEIGHTS}

    def nodepth(n, a):
        return a if n in ('c_ctx', 'norm_f_w') else a[0]

    shard = {n: nodepth(n, wl[n]).astype(_MXU_DTYPE if n in MXU_WEIGHTS else F32) for n in SHARDED}
    first = [n for n in SHARDED if n not in LATE_WEIGHTS and n != 'ada_w']
    W = {n: nodepth(n, wl[n]) for n in SMALL}
    W['ada_w'] = shard['ada_w']
    for n, g in zip(first, _gather_two_level("all_gather_plane", [shard[n] for n in first])):
        W[n] = _from_shards(n, g)

    loss_local, grad_x, G, received = _local_step(x[0], c, ctx[0], loss_target[0], W,
                                                  late_shards={n: shard[n] for n in LATE_WEIGHTS})
    loss = lax.psum(loss_local, ("x", "y", "c"))

    small_g = _pack_small(G)
    last = [n for n in SHARDED if n not in received and n != 'ada_w']
    send = [_to_dest_blocks(n, G[n]) for n in last] + [jnp.broadcast_to(small_g[None], (N_PLANE,) + small_g.shape)]
    received.update(zip(last + ['small'], _plane_exchange("scatter_plane", send, gather=False)))
    names = [n for n in SHARDED if n != 'ada_w'] + ['small']
    plane_sum = [_sum4("sum4_" + n, received[n]) for n in names]
    other = _swap_sibling(plane_sum)

    wd = {n: nodepth(n, wl[n]) for n in WEIGHTS}
    md = {n: nodepth(n, ml[n]) for n in WEIGHTS}
    vd = {n: nodepth(n, vl[n]) for n in WEIGHTS}
    res = {'ada_w': _adamw("adamw_ada_w", wl['ada_w'], ml['ada_w'], vl['ada_w'], G['ada_w'])}
    for n, p, q in zip(names, plane_sum, other):
        if n == 'small':
            outs = _adamw("adamw_small", _pack_small(wd), _pack_small(md), _pack_small(vd), p, q)
            unpacked = [_unpack_small(o, wd) for o in outs]
            for sn in SMALL:
                res[sn] = [u[sn].reshape(wl[sn].shape) for u in unpacked]
        else:
            res[n] = _adamw("adamw_" + n, wl[n], ml[n], vl[n], p, q)
    flat = [res[n][kind] for kind in range(4) for n in WEIGHTS]
    return (loss, grad_x[None], *flat)
```

```python
import functools

import jax
import jax.numpy as jnp
from jax import lax
from jax.experimental import pallas as pl
from jax.experimental.pallas import tpu as pltpu

F32 = jnp.float32
HI = lax.Precision.HIGHEST
_MXU_DTYPE = jnp.bfloat16
_SCAN_DTYPE = jnp.bfloat16
_GRAD_DTYPE = jnp.bfloat16

D_MODEL = 1024
EPS = 1e-6
CHUNK = 64
GRID_W = 64
GDN_HEADS = 8
GDN_DK = 128
SSM_HEADS = 32
SSM_PAIRS = 16
SSD_GROUP = 1
D_FF = 2816
D_IN_PROJ = 11360
N_PLANE = 4
P_QKV, P_ZG, P_GATE, P_ZS, P_XBC, P_SMALL, P_TOTAL = 0, 3072, 4096, 6144, 8192, 11264, 11776
IN_SEGMENTS = [(0, 3072, P_QKV), (3072, 4096, P_ZG), (4096, 4112, P_SMALL), (4112, 4128, P_SMALL + 16),
               (4128, 6176, P_ZS), (6176, 9248, P_XBC), (9248, 9312, P_SMALL + 32), (9312, 11360, P_GATE)]
ADAM_LR, ADAM_B1, ADAM_B2, ADAM_EPS, ADAM_WD, ADAM_STEP = 0.001, 0.9, 0.999, 1e-08, 0.01, 10

VMEM_LIMIT = 56 * 1024 * 1024

WEIGHTS = ['c_ctx', 'ada_w', 'ada_b', 'norm1_w', 'w_in', 'gdn_conv_w', 'gdn_conv_b', 'gdn_a_log', 'gdn_dt_bias',
           'gdn_norm_w', 'ssm_conv_w', 'ssm_conv_b', 'ssm_a_log', 'ssm_dt_bias', 'ssm_d', 'ssm_norm_w', 'w_br_gdn',
           'w_br_ssm', 'w_out', 'norm2_w', 'w_ffn_in', 'w_ffn_out', 'norm_f_w']
SHARD_AXIS = {'ada_w': 1, 'w_in': 1, 'gdn_conv_w': 1, 'ssm_conv_w': 1, 'w_br_gdn': 0, 'w_br_ssm': 0, 'w_out': 0,
              'w_ffn_in': 1, 'w_ffn_out': 0}
SHARDED = [n for n in WEIGHTS if n in SHARD_AXIS]
SMALL = [n for n in WEIGHTS if n not in SHARD_AXIS]
MXU_WEIGHTS = ['ada_w', 'w_in', 'w_br_gdn', 'w_br_ssm', 'w_out', 'w_ffn_in', 'w_ffn_out']


def _cp(sem):
    return pltpu.CompilerParams(dimension_semantics=sem, vmem_limit_bytes=VMEM_LIMIT)


def _pick(n, cands):
    for c in cands:
        if n % c == 0:
            return c
    raise ValueError(f"no tile for {n}")


HBM = pl.BlockSpec(memory_space=pl.ANY)
MESH = pl.DeviceIdType.MESH


def _plane_peers():
    x, y, c = lax.axis_index("x"), lax.axis_index("y"), lax.axis_index("c")
    return (x, y, c), [(1 - x, y, c), (x, 1 - y, c), (1 - x, 1 - y, c)]


class PlaneExchange:
    def __init__(self, arrs, gather):
        self.arrs, self.gather, self.n = list(arrs), gather, len(arrs)
        self.in_specs = [HBM] * self.n
        self.out_specs = [HBM] * self.n
        self.out_shape = [jax.ShapeDtypeStruct((N_PLANE,) + a.shape if gather else a.shape, a.dtype) for a in self.arrs]
        self.scratch = [pltpu.SemaphoreType.DMA((3 * self.n,)), pltpu.SemaphoreType.DMA((3 * self.n,)),
                        pltpu.SemaphoreType.DMA((self.n,))]

    def _copies(self, ins, outs, sems):
        send_sems, recv_sems, local_sems = sems
        (x, y, c), peers = _plane_peers()
        me = 2 * x + y
        copies = []
        for ti in range(self.n):
            src = ins[ti] if self.gather else ins[ti].at[me]
            copies.append(pltpu.make_async_copy(src, outs[ti].at[me], local_sems.at[ti]))
            for kk, (px, py, pc) in enumerate(peers):
                src = ins[ti] if self.gather else ins[ti].at[2 * px + py]
                copies.append(pltpu.make_async_remote_copy(
                    src_ref=src, dst_ref=outs[ti].at[me], send_sem=send_sems.at[3 * ti + kk],
                    recv_sem=recv_sems.at[3 * ti + kk], device_id=(px, py, pc), device_id_type=MESH))
        return copies

    def start(self, ins, outs, sems):
        for cp in self._copies(ins, outs, sems):
            cp.start()

    def wait(self, ins, outs, sems):
        for cp in self._copies(ins, outs, sems):
            cp.wait()


def _plane_exchange(name, arrs, gather):
    ex = PlaneExchange(arrs, gather)
    n = ex.n

    def body(*refs):
        ins, outs, sems = refs[:n], refs[n:2 * n], refs[2 * n:]
        ex.start(ins, outs, sems)
        ex.wait(ins, outs, sems)

    return pl.pallas_call(body, in_specs=ex.in_specs, out_specs=ex.out_specs, out_shape=ex.out_shape,
                          scratch_shapes=ex.scratch, name=name)(*arrs)


def _gather_two_level(name, arrs):
    n = len(arrs)
    split = [a.shape[0] % 32 == 0 for a in arrs]

    def body(*refs):
        ins, outs = refs[:n], refs[n:2 * n]
        ici_send, ici_recv, d2d_send, d2d_recv, local_sems = refs[2 * n:]
        (x, y, c), peers = _plane_peers()
        me = 2 * x + y

        def part(ref, ti):
            if not split[ti]:
                return ref
            rows = arrs[ti].shape[0] // 2
            return ref.at[pl.ds(c * rows, rows)]

        local = [pltpu.make_async_copy(ins[ti], outs[ti].at[me], local_sems.at[ti]) for ti in range(n)]
        ici = [pltpu.make_async_remote_copy(
            src_ref=part(ins[ti], ti), dst_ref=part(outs[ti].at[me], ti), send_sem=ici_send.at[3 * ti + kk],
            recv_sem=ici_recv.at[3 * ti + kk], device_id=peer, device_id_type=MESH)
            for ti in range(n) for kk, peer in enumerate(peers)]
        for cp in local + ici:
            cp.start()
        d2d = []
        for ti in range(n):
            for kk, (px, py, pc) in enumerate(peers):
                ici[3 * ti + kk].wait_recv()
                if split[ti]:
                    piece = part(outs[ti].at[2 * px + py], ti)
                    cp = pltpu.make_async_remote_copy(
                        src_ref=piece, dst_ref=piece, send_sem=d2d_send.at[3 * ti + kk],
                        recv_sem=d2d_recv.at[3 * ti + kk], device_id=(x, y, 1 - c), device_id_type=MESH)
                    cp.start()
                    d2d.append(cp)
        for cp in ici:
            cp.wait_send()
        for cp in d2d:
            cp.wait()
        for cp in local:
            cp.wait()

    return pl.pallas_call(
        body, in_specs=[HBM] * n, out_specs=[HBM] * n,
        out_shape=[jax.ShapeDtypeStruct((N_PLANE,) + a.shape, a.dtype) for a in arrs],
        scratch_shapes=[pltpu.SemaphoreType.DMA((3 * n,))] * 4 + [pltpu.SemaphoreType.DMA((n,))], name=name)(*arrs)


def _gather_all(name, arr):
    flips = [(fx, fy, fc) for fx in (0, 1) for fy in (0, 1) for fc in (0, 1)][1:]

    def body(a_ref, o_ref, send_sems, recv_sems, local_sem):
        x, y, c = lax.axis_index("x"), lax.axis_index("y"), lax.axis_index("c")
        me = 4 * x + 2 * y + c
        copies = [pltpu.make_async_copy(a_ref, o_ref.at[me], local_sem)]
        for kk, (fx, fy, fc) in enumerate(flips):
            peer = (1 - x if fx else x, 1 - y if fy else y, 1 - c if fc else c)
            copies.append(pltpu.make_async_remote_copy(
                src_ref=a_ref, dst_ref=o_ref.at[me], send_sem=send_sems.at[kk], recv_sem=recv_sems.at[kk],
                device_id=peer, device_id_type=MESH))
        for cp in copies:
            cp.start()
        for cp in copies:
            cp.wait()

    return pl.pallas_call(
        body, in_specs=[HBM], out_specs=HBM, out_shape=jax.ShapeDtypeStruct((8,) + arr.shape, arr.dtype),
        scratch_shapes=[pltpu.SemaphoreType.DMA((7,)), pltpu.SemaphoreType.DMA((7,)), pltpu.SemaphoreType.DMA],
        name=name)(arr)


def _swap_sibling(arrs):
    n = len(arrs)

    def body(*refs):
        ins, outs, send_sems, recv_sems = refs[:n], refs[n:2 * n], refs[2 * n], refs[2 * n + 1]
        x, y, c = lax.axis_index("x"), lax.axis_index("y"), lax.axis_index("c")
        copies = [pltpu.make_async_remote_copy(src_ref=ins[ti], dst_ref=outs[ti], send_sem=send_sems.at[ti],
                                               recv_sem=recv_sems.at[ti], device_id=(x, y, 1 - c), device_id_type=MESH)
                  for ti in range(n)]
        for cp in copies:
            cp.start()
        for cp in copies:
            cp.wait()

    return pl.pallas_call(
        body, in_specs=[HBM] * n, out_specs=[HBM] * n, out_shape=[jax.ShapeDtypeStruct(a.shape, a.dtype) for a in arrs],
        scratch_shapes=[pltpu.SemaphoreType.DMA((n,)), pltpu.SemaphoreType.DMA((n,))], name="swap_sibling")(*arrs)


MATMUL_VMEM_BUDGET = 44 * 1024 * 1024
TILE_CANDIDATES = (2944, 2816, 1408, 1024, 768, 512, 256, 128)
STEP_COST_BYTES = 1 << 20
ACC_PASS_WEIGHT = 0.35


def _matmul_tiles(m, n, k, ab, bb, ob, tn_fixed=None, tk_fixed=None):
    def cands(dim, whole_up_to):
        out = [c for c in TILE_CANDIDATES if dim % c == 0]
        if dim <= whole_up_to and dim not in out:
            out.append(dim)
        return out

    best = None
    for tm in cands(m, 3072):
        for tn in ([tn_fixed] if tn_fixed else cands(n, 3072)):
            for tk in ([tk_fixed] if tk_fixed else cands(k, 2048)):
                gi, gj, gk = m // tm, n // tn, k // tk
                vmem = 2 * (tm * tk * ab + tk * tn * bb + tm * tn * ob) + (tm * tn * 4 if gk > 1 else 0)
                if vmem > MATMUL_VMEM_BUDGET:
                    continue
                a_reads = 1 if gk == 1 else gj
                b_reads = 1 if (gk == 1 and gj == 1) else gi
                acc = m * n * 8 * gk * ACC_PASS_WEIGHT if gk > 1 else 0
                cost = (m * k * ab * a_reads + k * n * bb * b_reads + m * n * ob + gi * gj * gk * STEP_COST_BYTES + acc)
                if best is None or cost < best[0]:
                    best = (cost, tm, tn, tk)
    assert best is not None, (m, n, k)
    return best[1:]


def _matmul(a, b, form, name, out_dtype=F32, stacked_out=False, ride=None, halves=None):
    dims = {'nn': (((1,), (0,)), ((), ())), 'nt': (((1,), (1,)), ((), ())), 'tn': (((0,), (0,)), ((), ()))}[form]
    stacked_b = b.ndim == 3 and halves != 'b'
    ns = b.shape[2] if stacked_b else None
    if form == 'nn':
        m, k = a.shape
        n = b.shape[0] * ns if stacked_b else b.shape[1]
    elif form == 'nt':
        m, k = (a.shape[1], 2 * a.shape[2]) if halves == 'a' else a.shape
        n = b.shape[1] if stacked_b else b.shape[0]
    else:
        k, m = a.shape
        n = 2 * b.shape[2] if halves == 'b' else b.shape[1]
    ob = jnp.dtype(out_dtype).itemsize
    tm, tn, tk = _matmul_tiles(m, n, k, a.dtype.itemsize, b.dtype.itemsize, ob,
                               tn_fixed=(n // N_PLANE if stacked_out else ns if (stacked_b and form == 'nn') else None),
                               tk_fixed=(ns if (stacked_b and form == 'nt') else None))
    nk = k // tk
    grid = (m // tm, n // tn, nk)
    n_ride = ride.n if ride is not None else 0

    def body(*refs):
        a_ref, b_ref = refs[0], refs[1]
        ride_in = refs[2:2 + n_ride]
        o_ref = refs[2 + n_ride]
        ride_out = refs[3 + n_ride:3 + 2 * n_ride]
        acc_ref = refs[3 + 2 * n_ride]
        sems = refs[4 + 2 * n_ride:]
        i, j, kk = pl.program_id(0), pl.program_id(1), pl.program_id(2)
        if ride is not None:
            @pl.when((i == 0) & (j == 0) & (kk == 0))
            def _():
                ride.start(ride_in, ride_out, sems)

        def put(val):
            if stacked_out or halves == 'out':
                o_ref[0] = val.astype(o_ref.dtype)
            else:
                o_ref[...] = val.astype(o_ref.dtype)

        av = a_ref[0] if halves == 'a' else a_ref[...]
        bv = b_ref[0] if (stacked_b or halves == 'b') else b_ref[...]
        part = lax.dot_general(av.astype(_MXU_DTYPE), bv.astype(_MXU_DTYPE), dims, preferred_element_type=F32)
        if nk == 1:
            put(part)
        else:
            @pl.when(kk == 0)
            def _():
                acc_ref[...] = part

            @pl.when(kk > 0)
            def _():
                acc_ref[...] += part

            @pl.when(kk == nk - 1)
            def _():
                put(acc_ref[...])

        if ride is not None:
            @pl.when((i == grid[0] - 1) & (j == grid[1] - 1) & (kk == nk - 1))
            def _():
                ride.wait(ride_in, ride_out, sems)

    if form == 'nn':
        a_spec = pl.BlockSpec((tm, tk), lambda i, j, kk: (i, kk))
        b_spec = (pl.BlockSpec((1, tk, tn), lambda i, j, kk: (j, kk, 0)) if stacked_b
                  else pl.BlockSpec((tk, tn), lambda i, j, kk: (kk, j)))
    elif form == 'nt':
        a_spec = pl.BlockSpec((tm, tk), lambda i, j, kk: (i, kk))
        b_spec = (pl.BlockSpec((1, tn, tk), lambda i, j, kk: (kk, j, 0)) if stacked_b
                  else pl.BlockSpec((tn, tk), lambda i, j, kk: (j, kk)))
    else:
        a_spec = pl.BlockSpec((tk, tm), lambda i, j, kk: (kk, i))
        b_spec = pl.BlockSpec((tk, tn), lambda i, j, kk: (kk, j))
    hj, hk = grid[1] // 2, nk // 2
    if halves == 'a':
        a_spec = pl.BlockSpec((1, tm, tk), lambda i, j, kk: (kk // hk, i, kk % hk))
    if halves == 'b':
        b_spec = pl.BlockSpec((1, tk, tn), lambda i, j, kk: (j // hj, kk, j % hj))
    if stacked_out:
        o_spec = pl.BlockSpec((1, tm, tn), lambda i, j, kk: (j, i, 0))
        o_shape = jax.ShapeDtypeStruct((N_PLANE, m, tn), out_dtype)
    elif halves == 'out':
        o_spec = pl.BlockSpec((1, tm, tn), lambda i, j, kk: (j // hj, i, j % hj))
        o_shape = jax.ShapeDtypeStruct((2, m, n // 2), out_dtype)
    else:
        o_spec = pl.BlockSpec((tm, tn), lambda i, j, kk: (i, j))
        o_shape = jax.ShapeDtypeStruct((m, n), out_dtype)
    acc_shape = (tm, tn) if nk > 1 else (8, 128)
    if ride is None:
        return pl.pallas_call(
            body, grid=grid, in_specs=[a_spec, b_spec], out_specs=o_spec, out_shape=o_shape,
            scratch_shapes=[pltpu.VMEM(acc_shape, F32)],
            compiler_params=_cp(("parallel", "parallel", "arbitrary")), name=name)(a, b)
    res = pl.pallas_call(
        body, grid=grid, in_specs=[a_spec, b_spec] + ride.in_specs, out_specs=[o_spec] + ride.out_specs,
        out_shape=[o_shape] + ride.out_shape, scratch_shapes=[pltpu.VMEM(acc_shape, F32)] + ride.scratch,
        compiler_params=_cp(("arbitrary", "arbitrary", "arbitrary")), name=name)(a, b, *ride.arrs)
    return res[0], res[1:]


class Op:
    def __init__(self, arr, bs, im, load=None):
        self.arr, self.bs, self.im = arr, bs, im
        self.arrs = list(arr) if isinstance(arr, list) else [arr]
        self.load = load or (lambda r: r[...].astype(F32))

    def spec(self):
        return pl.BlockSpec(self.bs, self.im)

    def value(self, it):
        return functools.reduce(lambda u, w: u + w, [self.load(next(it)) for _ in self.arrs])


def _op_specs(ops):
    return [op.spec() for op in ops for _ in op.arrs]


def _op_arrays(ops):
    return [a for op in ops for a in op.arrs]


def _sum_dirs(r):
    return r[0].astype(F32) + r[1].astype(F32)


def _tw_fwd(name, fn, grid, ins, outs):
    def body(*refs):
        info = (pl.program_id(0), pl.program_id(1))
        it = iter(refs)
        vals = [op.value(it) for op in ins]
        res = fn(info, *vals)
        for r, v in zip(it, res):
            r[...] = v.astype(r.dtype)

    return pl.pallas_call(
        body, grid=grid, in_specs=_op_specs(ins), out_specs=[op.spec() for op in outs],
        out_shape=[jax.ShapeDtypeStruct(*op.arr) for op in outs],
        compiler_params=_cp(("parallel", "arbitrary")), name=name)(*_op_arrays(ins))


def _tw_bwd(name, fn, grid, tok, par, cots, tok_out, par_out, tok_add=None, sem=("parallel", "arbitrary"), into=None):
    n_tok = len(tok)
    flat_cots = [op for group in cots for op in group]
    extra = [tok_add] if tok_add is not None else []
    out_ops = list(tok_out) + list(par_out)
    into = into or {}

    def body(*refs):
        info = (pl.program_id(0), pl.program_id(1))
        it = iter(refs)
        tok_v = [op.value(it) for op in tok]
        par_v = [op.value(it) for op in par]
        cot_v = [functools.reduce(lambda u, w: u + w, [op.value(it) for op in group]) for group in cots]
        add_v = [op.value(it) for op in extra]
        for _ in into:
            next(it)
        _, pull = jax.vjp(lambda *a: fn(info, *a), *tok_v, *par_v)
        grads = pull(tuple(cot_v))
        for i in range(n_tok):
            r = next(it)
            g = grads[i] + add_v[0] if (i == 0 and add_v) else grads[i]
            if r.shape[-1] > g.shape[-1]:
                g = jnp.concatenate([g, jnp.zeros((g.shape[0], r.shape[-1] - g.shape[-1]), g.dtype)], axis=1)
            r[...] = g.astype(r.dtype)
        first = pl.program_id(1) == 0
        for i in range(len(par)):
            r = next(it)
            g = grads[n_tok + i]

            @pl.when(first)
            def _(r=r, g=g):
                r[...] = g

            @pl.when(jnp.logical_not(first))
            def _(r=r, g=g):
                r[...] += g

    ops = tok + par + flat_cots + extra
    n_in = len(_op_arrays(ops))
    return pl.pallas_call(
        body, grid=grid, in_specs=_op_specs(ops) + [HBM] * len(into), out_specs=[op.spec() for op in out_ops],
        out_shape=[jax.ShapeDtypeStruct(*op.arr) for op in out_ops],
        input_output_aliases={n_in + k: i for k, i in enumerate(into)},
        compiler_params=_cp(sem), name=name)(*_op_arrays(ops), *into.values())


def _silu(x):
    return x * jax.nn.sigmoid(x)


def _rms(x):
    return x * lax.rsqrt(jnp.mean(x * x, axis=-1, keepdims=True) + EPS)


@functools.partial(jax.custom_vjp, nondiff_argnums=(1,))
def _shift_rows(x, k):
    return pltpu.roll(x, k % x.shape[0], 0)


def _shift_rows_fwd(x, k):
    return _shift_rows(x, k), None


def _shift_rows_bwd(k, _, g):
    return (_shift_rows(g, -k),)


_shift_rows.defvjp(_shift_rows_fwd, _shift_rows_bwd)


def _prenorm_fn(nctx_t, info, x, w, sc8, sh8):
    is_ctx = info[1] < nctx_t
    sc = jnp.where(is_ctx, sc8[1:2], sc8[0:1])
    sh = jnp.where(is_ctx, sh8[1:2], sh8[0:1])
    return (_rms(x) * w * (1.0 + sc) + sh,)


def _conv_fn(lc, mode, info, x, w, b):
    n, c = x.shape
    g = info[1] * n + lax.broadcasted_iota(jnp.int32, (n, 1), 0)
    is_ctx = g < lc
    rr = jnp.where(is_ctx, g, g % GRID_W)
    first = rr == 0
    last = rr == jnp.where(is_ctx, lc - 1, GRID_W - 1)
    prev = jnp.where(first, 0.0, _shift_rows(x, 1))
    nxt = jnp.where(last, 0.0, _shift_rows(x, -1))
    y = b + prev * w[0:1] + x * w[1:2] + nxt * w[2:3]
    y = _silu(y)
    if mode == 'none':
        return (y,)
    scale = GDN_DK ** -0.5 if mode == 'q' else 1.0
    outs = []
    for h in range(c // 128):
        yh = y[:, h * 128:(h + 1) * 128]
        outs.append(yh * lax.rsqrt(jnp.sum(yh * yh, axis=-1, keepdims=True) + EPS) * scale)
    return (jnp.concatenate(outs, axis=1),)


def _act_fn(info, x, p0, p1):
    lane = lax.broadcasted_iota(jnp.int32, x.shape, 1)
    sp = jax.nn.softplus(x + p1)
    g = -jnp.exp(p0) * sp
    bt = jax.nn.sigmoid(x)
    return (jnp.where(lane < 16, g, jnp.where(lane < 32, bt, jnp.where(lane < 96, sp, 0.0))),)


def _mixg_fn(info, o, zg, gnw):
    outs = []
    for h in range(GDN_HEADS):
        outs.append(_rms(o[:, h * 128:(h + 1) * 128]) * gnw)
    return (jnp.concatenate(outs, axis=1) * _silu(zg),)


def _mixs_fn(info, y, xs, zs, dl, snw):
    yy = (y + dl * xs) * _silu(zs)
    outs = []
    for g in range(4):
        outs.append(_rms(yy[:, g * 512:(g + 1) * 512]))
    return (jnp.concatenate(outs, axis=1) * snw,)


def _merge_fn(info, gates, pg, ps):
    half = gates.shape[1] // 2
    return (jax.nn.sigmoid(gates[:, :half]) * pg + jax.nn.sigmoid(gates[:, half:]) * ps,)


def _norm2_fn(info, xt, mo, g8, w, sc8, sh8):
    h1 = xt + g8[0:1] * mo
    return (h1, _rms(h1) * w * (1.0 + sc8[0:1]) + sh8[0:1])


def _swiglu_fn(info, u2):
    return (_silu(u2[0]) * u2[1],)


_NN = (((1,), (0,)), ((), ()))
_NT = (((1,), (1,)), ((), ()))
_TN = (((0,), (0,)), ((), ()))


def _mmh(a, b):
    return lax.dot_general(a, b, _NN, precision=HI, preferred_element_type=F32)


def _dot1(a, b, dims):
    return lax.dot_general(a.astype(_SCAN_DTYPE), b.astype(_SCAN_DTYPE), dims, preferred_element_type=F32)


def _mm(a, b):
    return _dot1(a, b, _NN)


def _mm_nt(a, b):
    return _dot1(a, b, _NT)


def _mm_tn(a, b):
    return _dot1(a, b, _TN)


def _split2(a):
    hi = a.astype(_SCAN_DTYPE)
    return hi, (a - hi.astype(F32)).astype(_SCAN_DTYPE)


def _dot3(a, b, dims):
    ah, al = _split2(a)
    bh, bl = _split2(b)
    d = lambda u, w: lax.dot_general(u, w, dims, preferred_element_type=F32)
    return d(ah, bh) + (d(ah, bl) + d(al, bh))


def _order_masks(d):
    i = lax.broadcasted_iota(jnp.int32, (CHUNK, CHUNK), 0)
    j = lax.broadcasted_iota(jnp.int32, (CHUNK, CHUNK), 1)
    s = jnp.where(d == 0, 1, -1) * (i - j)
    return (s >= 0).astype(F32), (s > 0).astype(F32)


@jax.custom_vjp
def _unit_tri_inv(mats):
    i = lax.broadcasted_iota(jnp.int32, (CHUNK, CHUNK), 0)
    j = lax.broadcasted_iota(jnp.int32, (CHUNK, CHUNK), 1)
    eye = (i == j).astype(F32)
    ps = [-a for a in mats]
    ts = [eye + p for p in ps]
    for _ in range(5):
        ps = [_dot3(p, p, _NN) for p in ps]
        ts = [t + _dot3(t, p, _NN) for t, p in zip(ts, ps)]
    return tuple(ts)


def _uti_fwd(mats):
    ts = _unit_tri_inv(mats)
    return ts, ts


def _uti_bwd(ts, gs):
    inner = [_dot3(g, t, _NT) for g, t in zip(gs, ts)]
    return (tuple(-_dot3(t, u, _TN) for t, u in zip(ts, inner)),)


_unit_tri_inv.defvjp(_uti_fwd, _uti_bwd)


@jax.custom_vjp
def _unit_tri_inv_given(mats, ts):
    return ts


def _utig_fwd(mats, ts):
    return ts, ts


def _utig_bwd(ts, gs):
    return _uti_bwd(ts, gs)[0], tuple(jnp.zeros_like(t) for t in ts)


_unit_tri_inv_given.defvjp(_utig_fwd, _utig_bwd)


def _lane_col(blk, lane_idx):
    lane = lax.broadcasted_iota(jnp.int32, blk.shape, 1)
    return jnp.sum(jnp.where(lane == lane_idx, blk, 0.0), axis=1, keepdims=True)


def _decay_mat(cum, incl):
    cb = jnp.broadcast_to(cum, (CHUNK, CHUNK))
    return jnp.exp(jnp.minimum(cb - cb.T, 0.0)) * incl


def _gdn_chunks(dirs, streams, states, aux=None, want_aux=False, group=None):
    ns = len(dirs)
    masks = [_order_masks(d) for d in dirs]
    cum = [_mmh(masks[i][0], streams[i][3]) for i in range(ns)]
    tot = [jnp.sum(streams[i][3], axis=0, keepdims=True) for i in range(ns)]
    all_units = [(i, h) for i in range(ns) for h in range(GDN_HEADS)]
    group = group or len(all_units)
    cat = jnp.concatenate
    outs, new_states, ts_all = [], [], []
    for g0 in range(0, len(all_units), group):
        units = all_units[g0:g0 + group]
        us = range(len(units))
        st = states[g0:g0 + group]
        qs = [streams[i][0][:, h * 128:(h + 1) * 128] for i, h in units]
        ks = [streams[i][1][:, h * 128:(h + 1) * 128] for i, h in units]
        vs = [streams[i][2][:, h * 128:(h + 1) * 128] for i, h in units]
        gcum = [_lane_col(cum[i], dirs[i] * GDN_HEADS + h) for i, h in units]
        glast = [_lane_col(tot[i], dirs[i] * GDN_HEADS + h) for i, h in units]
        beta = [_lane_col(streams[i][3], 16 + dirs[i] * GDN_HEADS + h) for i, h in units]
        decay = [_decay_mat(gcum[u], masks[units[u][0]][0]) for u in us]
        egc = [jnp.exp(gcum[u]) for u in us]
        kb = [ks[u] * beta[u] for u in us]
        kq = [_mm_nt(cat([kb[u], qs[u]], axis=0), ks[u]) for u in us]
        mats = tuple(kq[u][:CHUNK] * decay[u] * masks[units[u][0]][1] for u in us)
        ts = _unit_tri_inv(mats) if aux is None else _unit_tri_inv_given(mats, tuple(aux[g0:g0 + group]))
        wu = [_mm(ts[u], cat([kb[u] * egc[u], vs[u] * beta[u]], axis=1)) for u in us]
        ws = [_mm(cat([wu[u][:, :128], qs[u] * egc[u]], axis=0), st[u]) for u in us]
        vn = [wu[u][:, 128:] - ws[u][:CHUNK] for u in us]
        outs += [ws[u][CHUNK:] + _mm(kq[u][CHUNK:] * decay[u], vn[u]) for u in us]
        new_states += [st[u] * jnp.exp(glast[u]) + _mm_tn(ks[u] * jnp.exp(glast[u] - gcum[u]), vn[u]) for u in us]
        ts_all += list(ts)
    per_stream = [cat(outs[i * GDN_HEADS:(i + 1) * GDN_HEADS], axis=1) for i in range(ns)]
    return (*per_stream, *new_states, *(ts_all if want_aux else ()))


def _gdn_step(d, q, k, v, spb, *states, aux=None, want_aux=False):
    return _gdn_chunks([d], [(q, k, v, spb)], states, aux, want_aux, group=4)


def _gdn_step2(q0, k0, v0, sp0, q1, k1, v1, sp1, *states, aux=None, want_aux=False):
    return _gdn_chunks([0, 1], [(q0, k0, v0, sp0), (q1, k1, v1, sp1)], states, aux, want_aux)


def _split3(a):
    a1 = a.astype(_SCAN_DTYPE)
    r = a - a1.astype(F32)
    a2 = r.astype(_SCAN_DTYPE)
    return a1, a2, (r - a2.astype(F32)).astype(_SCAN_DTYPE)


def _exact_dot(a, e, dims, split_lhs, passes=3):
    parts = _split3(a if split_lhs else e)[:passes]
    d = (lambda u: lax.dot_general(u, e, dims, preferred_element_type=F32)) if split_lhs else \
        (lambda u: lax.dot_general(a, u, dims, preferred_element_type=F32))
    return functools.reduce(lambda u, w: u + w, [d(p) for p in reversed(parts)])


@jax.custom_vjp
def _spread(a, e):
    return _exact_dot(a, e, _NN, True)


def _spread_fwd(a, e):
    return _spread(a, e), e


def _spread_bwd(e, g):
    return _exact_dot(g, e, _NT, True, passes=2), jnp.zeros_like(e)


_spread.defvjp(_spread_fwd, _spread_bwd)


@jax.custom_vjp
def _colsum_bcast(z):
    return _exact_dot(jnp.ones((z.shape[0], z.shape[0]), _SCAN_DTYPE), z, _NN, False)


def _colsum_fwd(z):
    return _colsum_bcast(z), None


def _colsum_bwd(_, g):
    return (_exact_dot(jnp.ones((g.shape[0], g.shape[0]), _SCAN_DTYPE), g, _NN, False, passes=2),)


_colsum_bcast.defvjp(_colsum_fwd, _colsum_bwd)


def _ssd_consts():
    wdt = SSM_HEADS * 64
    d = lax.broadcasted_iota(jnp.int32, (2, 128, wdt), 0)
    e = (lax.broadcasted_iota(jnp.int32, (2, 128, wdt), 1)
         == 32 + d * SSM_HEADS + lax.broadcasted_iota(jnp.int32, (2, 128, wdt), 2) // 64).astype(_SCAN_DTYPE)
    dd = lax.broadcasted_iota(jnp.int32, (2, CHUNK, wdt), 0)
    ci = lax.broadcasted_iota(jnp.int32, (2, CHUNK, wdt), 1)
    pos = lax.broadcasted_iota(jnp.int32, (2, CHUNK, wdt), 2) % 64
    incl_t = (jnp.where(dd == 0, 1, -1) * (ci - pos) >= 0).astype(F32)
    diag = (ci == pos).astype(F32)
    return [e, incl_t, diag]


def _ssd_step(d, x, bm, cm, spb, alog, *states, consts):
    e, incl_t, diag = consts
    incl, _ = _order_masks(d)
    lane1 = lax.broadcasted_iota(jnp.int32, (1, 128), 1)
    lo_lane = 32 + d * SSM_HEADS
    a_vec = jnp.where(lane1 >= lo_lane, jnp.where(lane1 < lo_lane + SSM_HEADS, -jnp.exp(alog), 0.0), 0.0)
    adt = spb * a_vec
    acum = _mmh(incl, adt)
    alast = jnp.sum(adt, axis=0, keepdims=True)
    dt2 = _spread(spb, e)
    ac2 = _spread(acum, e)
    al2 = _spread(jnp.broadcast_to(alast, (8, 128)), e)[0:1]
    row = _colsum_bcast(ac2 * diag)
    seg = jnp.exp(jnp.minimum(ac2 - row, 0.0)) * incl_t
    xdt = x * dt2
    gam = jnp.exp(ac2)
    xe = xdt * jnp.exp(al2 - ac2)
    low = lax.broadcasted_iota(jnp.int32, (CHUNK, 128), 1) < 64
    row_low = lax.broadcasted_iota(jnp.int32, (128, 1), 0) < 64
    ps = range(SSM_PAIRS)
    sl = [slice(p * 128, (p + 1) * 128) for p in ps]
    bg = [bm[:, g * 128:(g + 1) * 128] for g in range(4)]
    cg = [cm[:, g * 128:(g + 1) * 128] for g in range(4)]
    cb2 = [_mm_nt(cg[g], jnp.concatenate([bg[g], bg[g]], axis=0)) for g in range(4)]
    ys, new_states = [], []
    for p0 in range(0, SSM_PAIRS, SSD_GROUP):
        pg = range(p0, p0 + SSD_GROUP)
        xd = {p: jnp.concatenate([jnp.where(low, xdt[:, sl[p]], 0.0), jnp.where(low, 0.0, xdt[:, sl[p]])], axis=0)
              for p in pg}
        yd = {p: _mm(cb2[p // 4] * seg[:, sl[p]], xd[p]) for p in pg}
        yo = {p: _mm_nt(cg[p // 4], states[p]) for p in pg}
        ys += [yd[p] + gam[:, sl[p]] * yo[p] for p in pg]
        new = {p: _mm_tn(xe[:, sl[p]], bg[p // 4]) for p in pg}
        al0 = {p: _lane_col(alast, lo_lane + 2 * p) for p in pg}
        al1 = {p: _lane_col(alast, lo_lane + 2 * p + 1) for p in pg}
        new_states += [states[p] * jnp.exp(jnp.where(row_low, al0[p], al1[p])) + new[p] for p in pg]
    return (jnp.concatenate(ys, axis=1), *new_states)


def _chunk_of(d, p, nctx, nc):
    return jnp.where(d == 0, p, jnp.where(p < nctx, nctx - 1 - p, nctx + nc - 1 - p))


class _NoRide:
    n, arrs, in_specs, out_specs, out_shape, scratch = 0, [], [], [], [], []


def _const_specs(consts):
    return [pl.BlockSpec((1,) + a.shape[1:], lambda d, p: (d,) + (0,) * (a.ndim - 1)) for a in consts]


def _scan_fwd(name, step, toks, pars, consts, out_width, n_state, nctx):
    t = toks[0].shape[0]
    nc = t // CHUNK
    n_tok, n_par, n_const = len(toks), len(pars), len(consts)

    def body(*refs):
        it = iter(refs)
        tok_refs = [next(it) for _ in range(n_tok)]
        par_refs = [next(it) for _ in range(n_par)]
        const_refs = [next(it) for _ in range(n_const)]
        o_ref, ss_ref, s_scr = next(it), next(it), next(it)
        d, p = pl.program_id(0), pl.program_id(1)

        @pl.when(p == 0)
        def _():
            s_scr[...] = jnp.zeros(s_scr.shape, F32)

        ss_ref[0, 0] = s_scr[...]
        res = step(d, *[r[...] for r in tok_refs], *[r[...] for r in par_refs], *[s_scr[h] for h in range(n_state)],
                   consts=[r[0] for r in const_refs])
        o_ref[0] = res[0]
        for h in range(n_state):
            s_scr[h] = res[1 + h]

    ch = lambda d, p: _chunk_of(d, p, nctx, nc)
    in_specs = [pl.BlockSpec((CHUNK, a.shape[1]), lambda d, p: (ch(d, p), 0)) for a in toks]
    in_specs += [pl.BlockSpec(a.shape, lambda d, p: (0, 0)) for a in pars]
    return pl.pallas_call(
        body, grid=(2, nc), in_specs=in_specs + _const_specs(consts),
        out_specs=[pl.BlockSpec((1, CHUNK, out_width), lambda d, p: (d, ch(d, p), 0)),
                   pl.BlockSpec((1, 1, n_state, 128, 128), lambda d, p: (d, p, 0, 0, 0))],
        out_shape=[jax.ShapeDtypeStruct((2, t, out_width), F32),
                   jax.ShapeDtypeStruct((2, nc, n_state, 128, 128), F32)],
        scratch_shapes=[pltpu.VMEM((n_state, 128, 128), F32)],
        compiler_params=_cp(("arbitrary", "arbitrary")), name=name)(*toks, *pars, *consts)


def _scan_bwd(name, step, toks, pars, ss, dout, n_state, nctx, ride=None, aux=None, consts=()):
    t = toks[0].shape[0]
    nc = t // CHUNK
    n_tok, n_par, n_const = len(toks), len(pars), len(consts)
    rd = ride if ride is not None else _NoRide
    n_aux = aux.shape[2] if aux is not None else 0

    def body(*refs):
        it = iter(refs)
        tok_refs = [next(it) for _ in range(n_tok)]
        par_refs = [next(it) for _ in range(n_par)]
        ss_ref, do_ref = next(it), next(it)
        aux_ref = next(it) if n_aux else None
        const_refs = [next(it) for _ in range(n_const)]
        ride_in = [next(it) for _ in range(rd.n)]
        dtok_refs = [next(it) for _ in range(n_tok)]
        dpar_refs = [next(it) for _ in range(n_par)]
        ride_out = [next(it) for _ in range(rd.n)]
        ds_scr = next(it)
        sems = list(it)
        d, pr = pl.program_id(0), pl.program_id(1)
        if ride is not None:
            @pl.when((d == 0) & (pr == 0))
            def _():
                ride.start(ride_in, ride_out, sems)

            @pl.when((d == 1) & (pr == nc - 1))
            def _():
                ride.wait(ride_in, ride_out, sems)

        @pl.when(pr == 0)
        def _():
            ds_scr[...] = jnp.zeros(ds_scr.shape, F32)

        kw = dict(aux=[aux_ref[0, 0, i] for i in range(n_aux)]) if n_aux else {}
        if n_const:
            kw['consts'] = [r[0] for r in const_refs]
        _, pull = jax.vjp(functools.partial(step, d, **kw), *[r[...] for r in tok_refs], *[r[...] for r in par_refs],
                          *[ss_ref[0, 0, h] for h in range(n_state)])
        grads = pull((do_ref[...], *[ds_scr[h] for h in range(n_state)]))
        for r, g in zip(dtok_refs, grads[:n_tok]):
            r[0] = g
        for h in range(n_state):
            ds_scr[h] = grads[n_tok + n_par + h]
        first = (d == 0) & (pr == 0)
        for r, g in zip(dpar_refs, grads[n_tok:n_tok + n_par]):
            @pl.when(first)
            def _(r=r, g=g):
                r[...] = g

            @pl.when(jnp.logical_not(first))
            def _(r=r, g=g):
                r[...] += g

    ch = lambda d, pr: _chunk_of(d, nc - 1 - pr, nctx, nc)
    in_specs = [pl.BlockSpec((CHUNK, a.shape[1]), lambda d, pr: (ch(d, pr), 0)) for a in toks]
    in_specs += [pl.BlockSpec(a.shape, lambda d, pr: (0, 0)) for a in pars]
    in_specs += [pl.BlockSpec((1, 1, n_state, 128, 128), lambda d, pr: (d, nc - 1 - pr, 0, 0, 0)),
                 pl.BlockSpec((CHUNK, dout.shape[1]), lambda d, pr: (ch(d, pr), 0))]
    if n_aux:
        in_specs += [pl.BlockSpec((1, 1, n_aux, CHUNK, CHUNK), lambda d, pr: (d, nc - 1 - pr, 0, 0, 0))]
    in_specs += _const_specs(consts)
    out_specs = [pl.BlockSpec((1, CHUNK, a.shape[1]), lambda d, pr: (d, ch(d, pr), 0)) for a in toks]
    out_specs += [pl.BlockSpec(a.shape, lambda d, pr: (0, 0)) for a in pars]
    out_shape = [jax.ShapeDtypeStruct((2, t, a.shape[1]), F32) for a in toks]
    out_shape += [jax.ShapeDtypeStruct(a.shape, F32) for a in pars]
    return pl.pallas_call(
        body, grid=(2, nc), in_specs=in_specs + rd.in_specs, out_specs=out_specs + rd.out_specs,
        out_shape=out_shape + rd.out_shape, scratch_shapes=[pltpu.VMEM((n_state, 128, 128), F32)] + rd.scratch,
        compiler_params=_cp(("arbitrary", "arbitrary")), name=name)(
            *toks, *pars, ss, dout, *([aux] if n_aux else []), *consts, *rd.arrs)


def _scan2_fwd(name, step2, toks, out_width, n_state, nctx, n_aux, ride=None):
    t = toks[0].shape[0]
    nc = t // CHUNK
    n_tok = len(toks)
    rd = ride if ride is not None else _NoRide

    def body(*refs):
        it = iter(refs)
        tok_refs = [next(it) for _ in range(2 * n_tok)]
        ride_in = [next(it) for _ in range(rd.n)]
        o_refs = [next(it), next(it)]
        ss_ref, aux_ref = next(it), next(it)
        ride_out = [next(it) for _ in range(rd.n)]
        s_scr = next(it)
        sems = list(it)
        p = pl.program_id(0)
        if ride is not None:
            @pl.when(p == 0)
            def _():
                ride.start(ride_in, ride_out, sems)

        @pl.when(p == 0)
        def _():
            s_scr[...] = jnp.zeros(s_scr.shape, F32)

        for d in range(2):
            ss_ref[d, 0] = s_scr[d * n_state:(d + 1) * n_state]
        res = step2(*[r[...] for r in tok_refs], *[s_scr[u] for u in range(2 * n_state)], want_aux=True)
        for d in range(2):
            o_refs[d][...] = res[d]
            for i in range(n_aux):
                aux_ref[d, 0, i] = res[2 + 2 * n_state + d * n_aux + i]
        for u in range(2 * n_state):
            s_scr[u] = res[2 + u]
        if ride is not None:
            @pl.when(p == nc - 1)
            def _():
                ride.wait(ride_in, ride_out, sems)

    def tok_spec(a, d):
        return pl.BlockSpec((CHUNK, a.shape[1]), lambda p: (_chunk_of(d, p, nctx, nc), 0))

    return pl.pallas_call(
        body, grid=(nc,), in_specs=[tok_spec(a, d) for d in range(2) for a in toks] + rd.in_specs,
        out_specs=[pl.BlockSpec((CHUNK, out_width), lambda p: (_chunk_of(0, p, nctx, nc), 0)),
                   pl.BlockSpec((CHUNK, out_width), lambda p: (_chunk_of(1, p, nctx, nc), 0)),
                   pl.BlockSpec((2, 1, n_state, 128, 128), lambda p: (0, p, 0, 0, 0)),
                   pl.BlockSpec((2, 1, n_aux, CHUNK, CHUNK), lambda p: (0, p, 0, 0, 0))] + rd.out_specs,
        out_shape=[jax.ShapeDtypeStruct((t, out_width), F32), jax.ShapeDtypeStruct((t, out_width), F32),
                   jax.ShapeDtypeStruct((2, nc, n_state, 128, 128), F32),
                   jax.ShapeDtypeStruct((2, nc, n_aux, CHUNK, CHUNK), F32)] + rd.out_shape,
        scratch_shapes=[pltpu.VMEM((2 * n_state, 128, 128), F32)] + rd.scratch,
        compiler_params=_cp(("arbitrary",)), name=name)(*toks, *toks, *rd.arrs)


ADA_TN = 512


def _ada_part(cc16, w_shard):
    n = w_shard.shape[1]

    def body(c_ref, w_ref, o_ref):
        s = _silu(c_ref[...]).astype(_MXU_DTYPE)
        o_ref[...] = jnp.dot(s, w_ref[...].astype(_MXU_DTYPE), preferred_element_type=F32)

    return pl.pallas_call(
        body, grid=(n // ADA_TN,),
        in_specs=[pl.BlockSpec((16, D_MODEL), lambda j: (0, 0)), pl.BlockSpec((D_MODEL, ADA_TN), lambda j: (0, j))],
        out_specs=pl.BlockSpec((16, ADA_TN), lambda j: (0, j)), out_shape=jax.ShapeDtypeStruct((16, n), F32),
        compiler_params=_cp(("parallel",)), name="ada_part")(cc16, w_shard)


def _ada_bwd_shard(cc16, w_shard, d_lat, d_ctx):
    n = w_shard.shape[1]
    nj = n // ADA_TN

    def body(c_ref, w_ref, gl_ref, gc_ref, dw_ref, dc_ref):
        j = pl.program_id(0)
        s, pull = jax.vjp(_silu, c_ref[...])
        row = lax.broadcasted_iota(jnp.int32, (8, 1), 0)
        dctx = jnp.where(row == 0, jnp.sum(gc_ref[...], axis=0, keepdims=True), 0.0)
        mxu = lambda v: v.astype(_MXU_DTYPE)
        dw_ref[...] = (lax.dot_general(mxu(s[0:8]), mxu(gl_ref[...]), _TN, preferred_element_type=F32)
                       + lax.dot_general(mxu(s[8:16]), mxu(dctx), _TN, preferred_element_type=F32))
        ds = lax.dot_general(mxu(dctx), mxu(w_ref[...]), _NT, preferred_element_type=F32)

        @pl.when(j == 0)
        def _():
            dc_ref[...] = ds

        @pl.when(j > 0)
        def _():
            dc_ref[...] += ds

        @pl.when(j == nj - 1)
        def _():
            ct = jnp.concatenate([jnp.zeros((8, D_MODEL), F32), dc_ref[...]], axis=0)
            dc_ref[...] = 0.5 * pull(ct)[0][8:16]

    tile = pl.BlockSpec((8, ADA_TN), lambda j: (0, j))
    wspec = pl.BlockSpec((D_MODEL, ADA_TN), lambda j: (0, j))
    return pl.pallas_call(
        body, grid=(nj,), in_specs=[pl.BlockSpec((16, D_MODEL), lambda j: (0, 0)), wspec, tile, tile],
        out_specs=[wspec, pl.BlockSpec((8, D_MODEL), lambda j: (0, 0))],
        out_shape=[jax.ShapeDtypeStruct((D_MODEL, n), F32), jax.ShapeDtypeStruct((8, D_MODEL), F32)],
        compiler_params=_cp(("arbitrary",)), name="ada_bwd_shard")(cc16, w_shard, d_lat, d_ctx)


def _rowsum2(a, b):
    def body(a_ref, b_ref, o_ref):
        o_ref[...] = jnp.sum(a_ref[...], axis=0, keepdims=True) + jnp.sum(b_ref[...], axis=0, keepdims=True)

    return pl.pallas_call(body, out_shape=jax.ShapeDtypeStruct((1, a.shape[1]), F32), name="ada_b_grad")(a, b)


def _tail(h1, ff, mods, wf, tgt, nctx_t, tl):
    t = h1.shape[0]
    nt = t // tl

    def loss_fn(valid, h1v, ffv, g8, w, tg):
        h2 = h1v + g8[0:1] * ffv
        y = _rms(h2) * w
        err = (y - tg) ** 2
        return 0.5 * jnp.sum(jnp.mean(err, axis=-1, keepdims=True), axis=0, keepdims=True) * valid

    def body(h1_ref, ff_ref, g_ref, w_ref, t_ref, loss_ref, dh_ref, dff_ref, dg_ref, dw_ref):
        i = pl.program_id(0)
        valid = jnp.where(i < nctx_t, 0.0, 1.0)
        lv, pull = jax.vjp(functools.partial(loss_fn, valid), h1_ref[...], ff_ref[...].astype(F32), g_ref[...],
                           w_ref[...], t_ref[...])
        dh, dff, dg, dw, _ = pull(jnp.ones((1, 1), F32))
        dh_ref[...] = dh
        dff_ref[...] = dff.astype(dff_ref.dtype)
        lb = jnp.broadcast_to(lv, loss_ref.shape)

        @pl.when(i == 0)
        def _():
            loss_ref[...] = lb
            dg_ref[...] = dg
            dw_ref[...] = dw

        @pl.when(i > 0)
        def _():
            loss_ref[...] += lb
            dg_ref[...] += dg
            dw_ref[...] += dw

    tok = pl.BlockSpec((tl, D_MODEL), lambda i: (i, 0))
    return pl.pallas_call(
        body, grid=(nt,),
        in_specs=[tok, tok, pl.BlockSpec((8, D_MODEL), lambda i: (0, 5)), pl.BlockSpec((1, D_MODEL), lambda i: (0, 0)),
                  pl.BlockSpec((tl, D_MODEL), lambda i: (jnp.maximum(i - nctx_t, 0), 0))],
        out_specs=[pl.BlockSpec((8, 128), lambda i: (0, 0)), tok, tok, pl.BlockSpec((8, D_MODEL), lambda i: (0, 0)),
                   pl.BlockSpec((1, D_MODEL), lambda i: (0, 0))],
        out_shape=[jax.ShapeDtypeStruct((8, 128), F32), jax.ShapeDtypeStruct((t, D_MODEL), F32),
                   jax.ShapeDtypeStruct((t, D_MODEL), _MXU_DTYPE), jax.ShapeDtypeStruct((8, D_MODEL), F32),
                   jax.ShapeDtypeStruct((1, D_MODEL), F32)],
        compiler_params=_cp(("arbitrary",)), name="tail_loss")(h1, ff, mods, wf, tgt)


def _pack_w_in(w4):
    ns = w4.shape[2]
    placed = []
    for s0, s1, p0 in IN_SEGMENTS:
        for j in range(N_PLANE):
            lo, hi = max(s0, j * ns), min(s1, (j + 1) * ns)
            if lo < hi:
                placed.append((p0 + lo - s0, w4[j][:, lo - j * ns:hi - j * ns]))
    placed.sort(key=lambda e: e[0])
    pieces, end = [], 0
    for pos, piece in placed:
        assert pos == end, (pos, end)
        pieces.append(piece)
        end = pos + piece.shape[1]
    pieces.append(jnp.zeros((w4.shape[1], P_TOTAL - end), w4.dtype))
    return jnp.concatenate(pieces, axis=1)


def _unpack_w_in(g):
    ns = D_IN_PROJ // N_PLANE
    shards = []
    for j in range(N_PLANE):
        pieces = []
        for s0, s1, p0 in IN_SEGMENTS:
            lo, hi = max(s0, j * ns), min(s1, (j + 1) * ns)
            if lo < hi:
                pieces.append(g[:, p0 + lo - s0:p0 + hi - s0])
        shards.append(jnp.concatenate(pieces, axis=1))
    return jnp.stack(shards)


LATE_WEIGHTS = ['w_br_gdn', 'w_br_ssm', 'w_out', 'w_ffn_in', 'w_ffn_out']
COL_STACKED = ('w_in', 'w_ffn_in')


def _from_shards(n, g):
    if n in COL_STACKED:
        return g
    if SHARD_AXIS[n] == 0:
        return g.reshape(N_PLANE * g.shape[1], g.shape[2])
    return jnp.concatenate([g[j] for j in range(N_PLANE)], axis=1)


def _to_dest_blocks(n, g):
    if g.ndim == 3:
        return g
    if SHARD_AXIS[n] == 0:
        return g.reshape(N_PLANE, g.shape[0] // N_PLANE, g.shape[1])
    sz = g.shape[1] // N_PLANE
    return jnp.stack([g[:, j * sz:(j + 1) * sz] for j in range(N_PLANE)])


def _local_step(x, c, ctx, tgt, W, late_shards):
    lc, l = ctx.shape[0], x.shape[0]
    t = lc + l
    tl = 256
    assert lc == tl and l % tl == 0 and lc % CHUNK == 0
    nt, nctx_t, nctx = t // tl, lc // tl, lc // CHUNK
    act = _MXU_DTYPE
    r1 = lambda v: v.reshape(1, -1)

    xt = jnp.concatenate([ctx, x], axis=0)
    lin = 4 * lax.axis_index("x") + 2 * lax.axis_index("y") + lax.axis_index("c")
    c_all = _gather_all("gather_c", jnp.broadcast_to(c, (8, D_MODEL)))[:, 0]
    cc16 = jnp.concatenate([c_all, r1(W['c_ctx']), jnp.zeros((7, D_MODEL), F32)], axis=0)
    (parts,) = _plane_exchange("gather_mods", [_ada_part(cc16, W['ada_w'])], gather=True)
    mods_all = jnp.transpose(parts, (1, 0, 2)).reshape(16, -1) + r1(W['ada_b'])
    mods = jnp.concatenate([lax.dynamic_slice_in_dim(mods_all, lin, 1, axis=0), mods_all[8:9],
                            jnp.zeros((6, mods_all.shape[1]), F32)], axis=0)

    def mod(kk):
        return Op(mods, (8, D_MODEL), lambda j, i, kk=kk: (0, kk))

    def tokop(arr, w=D_MODEL, off=0, tl_=tl):
        if not isinstance(arr, list) and arr.ndim == 3:
            return Op(arr, (2, tl_, w), lambda j, i: (0, i, off + j), load=_sum_dirs)
        return Op(arr, (tl_, w), lambda j, i: (i, off + j))

    def outop(n, dtype, w=D_MODEL, tl_=tl):
        return Op(((t, n), dtype), (tl_, w), lambda j, i: (i, j))

    def parop(arr, w, off=0):
        return Op(arr, (arr.shape[0], w), lambda j, i: (0, off + j))

    def parout(rows, n, w):
        return Op(((rows, n), F32), (rows, w), lambda j, i: (0, j))

    n1w = r1(W['norm1_w'])
    pre_fn = functools.partial(_prenorm_fn, nctx_t)
    pre_tok, pre_par = [tokop(xt)], [parop(n1w, D_MODEL), mod(1), mod(0)]
    (a,) = _tw_fwd("prenorm_fwd", pre_fn, (1, nt), pre_tok + pre_par, [outop(D_MODEL, act)])
    wp = _pack_w_in(W['w_in'])
    proj = _matmul(a, wp, 'nn', "in_proj")

    gcw, gcb = W['gdn_conv_w'], r1(W['gdn_conv_b'])
    scw, scb = W['ssm_conv_w'], r1(W['ssm_conv_b'])
    conv_parts = {}

    tlf = 768 if t % 768 == 0 else tl

    def conv_part(name, mode, poff, cw, cb, woff, width):
        fn = functools.partial(_conv_fn, lc, mode)
        bw = min(width, 1024)
        par_ = [parop(cw, bw, woff // bw), parop(cb, bw, woff // bw)]
        conv_parts[name] = (fn, [tokop(proj, bw, poff // bw)], par_, width, bw, poff)
        (o,) = _tw_fwd("conv_" + name, fn, (width // bw, t // tlf), [tokop(proj, bw, poff // bw, tlf)] + par_,
                       [outop(width, F32, bw, tlf)])
        return o

    q = conv_part('q', 'q', P_QKV, gcw, gcb, 0, 1024)
    k = conv_part('k', 'k', P_QKV + 1024, gcw, gcb, 1024, 1024)
    v = conv_part('v', 'none', P_QKV + 2048, gcw, gcb, 2048, 1024)
    xs = conv_part('xs', 'none', P_XBC, scw, scb, 0, 2048)
    bm = conv_part('bm', 'none', P_XBC + 2048, scw, scb, 2048, 512)
    cm = conv_part('cm', 'none', P_XBC + 2560, scw, scb, 2560, 512)

    z16, z32 = jnp.zeros((16,), F32), jnp.zeros((32,), F32)
    p0 = jnp.concatenate([W['gdn_a_log'].reshape(-1), jnp.zeros((112,), F32)]).reshape(1, 128)
    p1 = jnp.concatenate([W['gdn_dt_bias'].reshape(-1), z16, W['ssm_dt_bias'].reshape(-1), z32]).reshape(1, 128)
    alog = jnp.concatenate([z32, W['ssm_a_log'].reshape(-1), z32]).reshape(1, 128)
    act_tok, act_par = [tokop(proj, 128, P_SMALL // 128)], [parop(p0, 128), parop(p1, 128)]
    (sp,) = _tw_fwd("small_act", _act_fn, (1, t // tlf), [tokop(proj, 128, P_SMALL // 128, tlf)] + act_par,
                    [outop(128, F32, 128, tlf)])

    gather_late = PlaneExchange([late_shards[n] for n in LATE_WEIGHTS], gather=True)
    o_f, o_b, ss, tri, *late = _scan2_fwd("gdn_scan_fwd", _gdn_step2, [q, k, v, sp], 1024, GDN_HEADS, nctx,
                                          GDN_HEADS, ride=gather_late)
    W = dict(W, **{n: _from_shards(n, g) for n, g in zip(LATE_WEIGHTS, late)})
    ssd_consts = _ssd_consts()
    y2, hs = _scan_fwd("ssd_scan_fwd", _ssd_step, [xs, bm, cm, sp], [alog], ssd_consts, 2048, SSM_PAIRS, nctx)

    tlm = 128
    ntm = t // tlm
    gnw = r1(W['gdn_norm_w'])
    dl = jnp.repeat(W['ssm_d'], 64).reshape(1, 2048)
    snw = r1(W['ssm_norm_w'])
    mg_tok = [tokop([o_f, o_b], 1024, 0, tlm), tokop(proj, 1024, P_ZG // 1024, tlm)]
    mg_par = [parop(gnw, 128)]
    (og,) = _tw_fwd("mix_gdn", _mixg_fn, (1, nt),
                    [tokop([o_f, o_b], 1024, 0), tokop(proj, 1024, P_ZG // 1024)] + mg_par, [outop(1024, act, 1024)])
    ms_tok = [tokop(y2, 2048, 0, tlm), tokop(xs, 2048, 0, tlm), tokop(proj, 2048, P_ZS // 2048, tlm)]
    ms_par = [parop(dl, 2048), parop(snw, 2048)]
    (yg,) = _tw_fwd("mix_ssm", _mixs_fn, (1, nt),
                    [tokop(y2, 2048, 0), tokop(xs, 2048, 0), tokop(proj, 2048, P_ZS // 2048)] + ms_par,
                    [outop(2048, act, 2048)])

    pg = _matmul(og, W['w_br_gdn'], 'nn', "br_gdn", out_dtype=act)
    ps = _matmul(yg, W['w_br_ssm'], 'nn', "br_ssm", out_dtype=act)
    mr_tok = [tokop(proj, 2048, P_GATE // 2048), tokop(pg), tokop(ps)]
    (mrg,) = _tw_fwd("merge", _merge_fn, (1, t // tlf),
                     [tokop(proj, 2048, P_GATE // 2048, tlf), tokop(pg, tl_=tlf), tokop(ps, tl_=tlf)],
                     [outop(1024, act, tl_=tlf)])
    mo = _matmul(mrg, W['w_out'], 'nn', "w_out")

    n2w = r1(W['norm2_w'])
    n2_tok, n2_par = [tokop(xt), tokop(mo)], [mod(2), parop(n2w, D_MODEL), mod(4), mod(3)]
    h1, f = _tw_fwd("norm2_fwd", _norm2_fn, (1, t // tlf), [tokop(xt, tl_=tlf), tokop(mo, tl_=tlf)] + n2_par,
                    [outop(1024, F32, tl_=tlf), outop(1024, act, tl_=tlf)])
    u2 = _matmul(f, W['w_ffn_in'], 'nn', "ffn_in", out_dtype=act, halves='out')
    swb = D_FF // 2
    both = lambda arr, tl_=tl: Op(arr, (2, tl_, swb), lambda j, i: (0, i, j))
    sw_tok = [both(u2)]
    (sw,) = _tw_fwd("swiglu", _swiglu_fn, (D_FF // swb, t // tlf), [both(u2, tlf)], [outop(D_FF, act, swb, tlf)])
    ff = _matmul(sw, W['w_ffn_out'], 'nn', "ffn_out")

    loss8, dh1, dff, dg2, dnf = _tail(h1, ff, mods, r1(W['norm_f_w']), tgt, nctx_t, tl)
    loss = loss8[0, 0]

    G = {}
    G['norm_f_w'] = dnf.reshape(-1)
    G['w_ffn_out'] = _matmul(sw, dff, 'tn', "d_ffn_out", out_dtype=_GRAD_DTYPE)
    dsw = _matmul(dff, W['w_ffn_out'], 'nt', "d_sw", out_dtype=act)
    (du2,) = _tw_bwd("swiglu_bwd", _swiglu_fn, (D_FF // swb, nt), sw_tok, [], [[tokop(dsw, swb)]],
                     [both(((2, t, D_FF), act))], [])
    G['w_ffn_in'] = _matmul(f, du2, 'tn', "d_ffn_in", out_dtype=_GRAD_DTYPE, stacked_out=True, halves='b')
    df = _matmul(du2, W['w_ffn_in'], 'nt', "d_f", halves='a')
    dxt1, dmo, dg1, dn2, dsc2, dsh2 = _tw_bwd(
        "norm2_bwd", _norm2_fn, (1, nt), n2_tok, n2_par, [[tokop(dh1)], [tokop(df)]],
        [outop(1024, F32), outop(1024, act)],
        [parout(8, 1024, 1024), parout(1, 1024, 1024), parout(8, 1024, 1024), parout(8, 1024, 1024)])
    G['norm2_w'] = dn2.reshape(-1)
    G['w_out'] = _matmul(mrg, dmo, 'tn', "d_w_out", out_dtype=_GRAD_DTYPE)
    dmrg = _matmul(dmo, W['w_out'], 'nt', "d_mrg", out_dtype=act)
    def win(off, w, tl_=tl):
        return Op(((t, P_TOTAL), act), (tl_, w), lambda j, i: (i, off // w + j))

    dproj = lax.empty((t, P_TOTAL), act)
    dproj, dpg, dps = _tw_bwd("merge_bwd", _merge_fn, (1, nt), mr_tok, [], [[tokop(dmrg)]],
                              [win(P_GATE, 2048), outop(1024, act), outop(1024, act)], [], into={0: dproj})
    G['w_br_gdn'] = _matmul(og, dpg, 'tn', "d_br_gdn", out_dtype=_GRAD_DTYPE)
    G['w_br_ssm'] = _matmul(yg, dps, 'tn', "d_br_ssm", out_dtype=_GRAD_DTYPE)
    dog = _matmul(dpg, W['w_br_gdn'], 'nt', "d_og")
    dyg = _matmul(dps, W['w_br_ssm'], 'nt', "d_yg")

    do, dproj, dgnw = _tw_bwd("mix_gdn_bwd", _mixg_fn, (1, ntm), mg_tok, mg_par, [[tokop(dog, 1024, 0, tlm)]],
                              [outop(1024, F32, 1024, tlm), win(P_ZG, 1024, tlm)], [parout(1, 128, 128)],
                              into={1: dproj})
    G['gdn_norm_w'] = dgnw.reshape(-1)
    dy, dxs_a, dproj, ddl, dsnw = _tw_bwd(
        "mix_ssm_bwd", _mixs_fn, (1, ntm), ms_tok, ms_par, [[tokop(dyg, 2048, 0, tlm)]],
        [outop(2048, F32, 2048, tlm), outop(2048, F32, 2048, tlm), win(P_ZS, 2048, tlm)],
        [parout(1, 2048, 2048), parout(1, 2048, 2048)], into={2: dproj})
    G['ssm_d'] = ddl.reshape(SSM_HEADS, 64).sum(axis=1)
    G['ssm_norm_w'] = dsnw.reshape(-1)

    dq2, dk2, dv2, dsp_g = _scan_bwd("gdn_scan_bwd", _gdn_step, [q, k, v, sp], [], ss, do, GDN_HEADS, nctx, aux=tri)
    received = {}
    scatter_late = PlaneExchange([_to_dest_blocks(n, G.pop(n)) for n in LATE_WEIGHTS], gather=False)
    dxs2, dbm2, dcm2, dsp_s, dalog, *got = _scan_bwd("ssd_scan_bwd", _ssd_step, [xs, bm, cm, sp], [alog], hs, dy,
                                                     SSM_PAIRS, nctx, ride=scatter_late, consts=ssd_consts)
    received.update(zip(LATE_WEIGHTS, got))
    G['ssm_a_log'] = dalog[0, 32:96].reshape(2, SSM_HEADS)

    dconv_w, dconv_b = {}, {}

    def conv_bwd(name, cots_, buf):
        fn, tok_, par_, width, bw, poff = conv_parts[name]
        buf, dconv_w[name], dconv_b[name] = _tw_bwd(
            "conv_" + name + "_bwd", fn, (width // bw, nt), tok_, par_, [[tokop(c_, bw) for c_ in cots_]],
            [win(poff, bw)], [parout(3, width, bw), parout(1, width, bw)], into={0: buf})
        return buf

    dproj = conv_bwd('q', [dq2], dproj)
    dproj = conv_bwd('k', [dk2], dproj)
    dproj = conv_bwd('v', [dv2], dproj)
    dproj = conv_bwd('xs', [dxs2, dxs_a], dproj)
    dproj = conv_bwd('bm', [dbm2], dproj)
    dproj = conv_bwd('cm', [dcm2], dproj)
    G['gdn_conv_w'] = jnp.concatenate([dconv_w['q'], dconv_w['k'], dconv_w['v']], axis=1)
    G['gdn_conv_b'] = [dconv_b['q'], dconv_b['k'], dconv_b['v']]
    G['ssm_conv_w'] = jnp.concatenate([dconv_w['xs'], dconv_w['bm'], dconv_w['cm']], axis=1)
    G['ssm_conv_b'] = [dconv_b['xs'], dconv_b['bm'], dconv_b['cm']]

    dproj, dp0, dp1 = _tw_bwd("small_act_bwd", _act_fn, (1, nt), act_tok, act_par,
                              [[tokop(dsp_g, 128), tokop(dsp_s, 128)]], [win(P_SMALL, P_TOTAL - P_SMALL)],
                              [parout(1, 128, 128), parout(1, 128, 128)], into={0: dproj})
    G['gdn_a_log'] = dp0[0, 0:16].reshape(2, GDN_HEADS)
    G['gdn_dt_bias'] = dp1[0, 0:16].reshape(2, GDN_HEADS)
    G['ssm_dt_bias'] = dp1[0, 32:96].reshape(2, SSM_HEADS)

    g_w_in = _unpack_w_in(_matmul(a, dproj, 'tn', "d_w_in", out_dtype=_GRAD_DTYPE))
    da, (received['w_in'],) = _matmul(dproj, wp, 'nt', "d_a", ride=PlaneExchange([g_w_in], gather=False))

    gx_out = Op(((l, D_MODEL), F32), (tl, D_MODEL), lambda j, i: (jnp.maximum(i - nctx_t, 0), 0))
    grad_x, dn1, dsc1, dsh1 = _tw_bwd(
        "prenorm_bwd", pre_fn, (1, nt), pre_tok, pre_par, [[tokop(da)]], [gx_out],
        [parout(1, 1024, 1024), parout(8, 1024, 1024), parout(8, 1024, 1024)], tok_add=tokop(dxt1),
        sem=("arbitrary", "arbitrary"))
    G['norm1_w'] = dn1.reshape(-1)
    dmods = jnp.concatenate([dsh1, dsc1, dg1, dsh2, dsc2, dg2], axis=1)
    dm_all = _gather_all("gather_dmods", dmods)
    ns = W['ada_w'].shape[1]
    mine = lax.dynamic_slice_in_dim(dm_all, (2 * lax.axis_index("x") + lax.axis_index("y")) * ns, ns, axis=2)
    G['ada_w'], dcc = _ada_bwd_shard(cc16, W['ada_w'], mine[:, 0], mine[:, 1])
    G['ada_b'] = jnp.where(lin == 0, _rowsum2(dm_all[:, 0], dm_all[:, 1]).reshape(-1), 0.0)
    G['c_ctx'] = dcc[0]
    return loss, grad_x, G, received


def _row_tile(r, c):
    for tr in (512, 256, 128, 64, 32, 16, 8):
        if r % tr == 0 and tr * c * 4 <= (1 << 20):
            return tr
    return r


def _sum4(name, rv):
    _, r, c = rv.shape
    tr = _row_tile(r, c)

    def body(r_ref, o_ref):
        o_ref[...] = ((r_ref[0].astype(F32) + r_ref[1].astype(F32)) + r_ref[2].astype(F32)) + r_ref[3].astype(F32)

    return pl.pallas_call(
        body, grid=(r // tr,), in_specs=[pl.BlockSpec((N_PLANE, tr, c), lambda i: (0, i, 0))],
        out_specs=pl.BlockSpec((tr, c), lambda i: (i, 0)), out_shape=jax.ShapeDtypeStruct((r, c), F32),
        compiler_params=_cp(("parallel",)), name=name)(rv)


def _adamw(name, w, m, v, p, q=None):
    r, c = w.shape[-2:]
    tr = _row_tile(r, c)
    grads = [p] if q is None else [p, q]
    lead = (0,) * (w.ndim - 2)

    def body(w_ref, m_ref, v_ref, *refs):
        g_ref, d_ref, mo_ref, vo_ref = refs[len(grads):]
        g = refs[0][...] if q is None else refs[0][...] + refs[1][...]
        at = lead if lead else Ellipsis
        mn = ADAM_B1 * m_ref[at] + (1.0 - ADAM_B1) * g
        vn = ADAM_B2 * v_ref[at] + (1.0 - ADAM_B2) * jnp.square(g)
        m_hat = mn / (1.0 - ADAM_B1 ** ADAM_STEP)
        v_hat = vn / (1.0 - ADAM_B2 ** ADAM_STEP)
        g_ref[at] = g
        d_ref[at] = -ADAM_LR * (m_hat / (jnp.sqrt(v_hat) + ADAM_EPS) + ADAM_WD * w_ref[at])
        mo_ref[at] = mn
        vo_ref[at] = vn

    gspec = pl.BlockSpec((tr, c), lambda i: (i, 0))
    wspec = pl.BlockSpec((1,) * len(lead) + (tr, c), lambda i: lead + (i, 0))
    return pl.pallas_call(
        body, grid=(r // tr,), in_specs=[wspec] * 3 + [gspec] * len(grads), out_specs=[wspec] * 4,
        out_shape=[jax.ShapeDtypeStruct(w.shape, F32)] * 4, compiler_params=_cp(("parallel",)), name=name)(w, m, v, *grads)


SMALL_ROWS = 24


def _pack_small(d):
    pieces = [p.reshape(-1) for n in SMALL for p in (d[n] if isinstance(d[n], list) else [d[n]])]
    v = jnp.concatenate(pieces)
    return jnp.pad(v, (0, SMALL_ROWS * 1024 - v.shape[0])).reshape(SMALL_ROWS, 1024)


def _unpack_small(buf, like):
    v = buf.reshape(-1)
    out, off = {}, 0
    for n in SMALL:
        sz = like[n].size
        out[n] = v[off:off + sz].reshape(like[n].shape)
        off += sz
    return out


def kernel(x, c, ctx, c_ctx, ada_w, ada_b, norm1_w, w_in, gdn_conv_w, gdn_conv_b, gdn_a_log, gdn_dt_bias, gdn_norm_w, ssm_conv_w, ssm_conv_b, ssm_a_log, ssm_dt_bias, ssm_d, ssm_norm_w, w_br_gdn, w_br_ssm, w_out, norm2_w, w_ffn_in, w_ffn_out, norm_f_w, loss_target, m_c_ctx, m_ada_w, m_ada_b, m_norm1_w, m_w_in, m_gdn_conv_w, m_gdn_conv_b, m_gdn_a_log, m_gdn_dt_bias, m_gdn_norm_w, m_ssm_conv_w, m_ssm_conv_b, m_ssm_a_log, m_ssm_dt_bias, m_ssm_d, m_ssm_norm_w, m_w_br_gdn, m_w_br_ssm, m_w_out, m_norm2_w, m_w_ffn_in, m_w_ffn_out, m_norm_f_w, v_c_ctx, v_ada_w, v_ada_b, v_norm1_w, v_w_in, v_gdn_conv_w, v_gdn_conv_b, v_gdn_a_log, v_gdn_dt_bias, v_gdn_norm_w, v_ssm_conv_w, v_ssm_conv_b, v_ssm_a_log, v_ssm_dt_bias, v_ssm_d, v_ssm_norm_w, v_w_br_gdn, v_w_br_ssm, v_w_out, v_norm2_w, v_w_ffn_in, v_w_ffn_out, v_norm_f_w):
    args = dict(locals())
    wl = {n: args[n] for n in WEIGHTS}
    ml = {n: args['m_' + n] for n in WEIGHTS}
    vl = {n: args['v_' + n] for n in WEIGHTS}

    def nodepth(n, a):
        return a if n in ('c_ctx', 'norm_f_w') else a[0]

    shard = {n: nodepth(n, wl[n]).astype(_MXU_DTYPE if n in MXU_WEIGHTS else F32) for n in SHARDED}
    first = [n for n in SHARDED if n not in LATE_WEIGHTS and n != 'ada_w']
    W = {n: nodepth(n, wl[n]) for n in SMALL}
    W['ada_w'] = shard['ada_w']
    for n, g in zip(first, _gather_two_level("all_gather_plane", [shard[n] for n in first])):
        W[n] = _from_shards(n, g)

    loss_local, grad_x, G, received = _local_step(x[0], c, ctx[0], loss_target[0], W,
                                                  late_shards={n: shard[n] for n in LATE_WEIGHTS})
    loss = lax.psum(loss_local, ("x", "y", "c"))

    small_g = _pack_small(G)
    last = [n for n in SHARDED if n not in received and n != 'ada_w']
    send = [_to_dest_blocks(n, G[n]) for n in last] + [jnp.broadcast_to(small_g[None], (N_PLANE,) + small_g.shape)]
    received.update(zip(last + ['small'], _plane_exchange("scatter_plane", send, gather=False)))
    names = [n for n in SHARDED if n != 'ada_w'] + ['small']
    plane_sum = [_sum4("sum4_" + n, received[n]) for n in names]
    other = _swap_sibling(plane_sum)

    wd = {n: nodepth(n, wl[n]) for n in WEIGHTS}
    md = {n: nodepth(n, ml[n]) for n in WEIGHTS}
    vd = {n: nodepth(n, vl[n]) for n in WEIGHTS}
    res = {'ada_w': _adamw("adamw_ada_w", wl['ada_w'], ml['ada_w'], vl['ada_w'], G['ada_w'])}
    for n, p, q in zip(names, plane_sum, other):
        if n == 'small':
            outs = _adamw("adamw_small", _pack_small(wd), _pack_small(md), _pack_small(vd), p, q)
            unpacked = [_unpack_small(o, wd) for o in outs]
            for sn in SMALL:
                res[sn] = [u[sn].reshape(wl[sn].shape) for u in unpacked]
        else:
            res[n] = _adamw("adamw_" + n, wl[n], ml[n], vl[n], p, q)
    flat = [res[n][kind] for kind in range(4) for n in WEIGHTS]
    return (loss, grad_x[None], *flat)
```

```python
import functools

import jax
import jax.numpy as jnp
from jax import lax
from jax.experimental import pallas as pl
from jax.experimental.pallas import tpu as pltpu

F32 = jnp.float32
HI = lax.Precision.HIGHEST
_MXU_DTYPE = jnp.bfloat16
_SCAN_DTYPE = jnp.bfloat16
_GRAD_DTYPE = jnp.bfloat16

D_MODEL = 1024
EPS = 1e-6
CHUNK = 64
GRID_W = 64
GDN_HEADS = 8
GDN_DK = 128
SSM_HEADS = 32
SSM_PAIRS = 16
SSD_GROUP = 1
D_FF = 2816
D_IN_PROJ = 11360
N_PLANE = 4
P_QKV, P_ZG, P_GATE, P_ZS, P_XBC, P_SMALL, P_TOTAL = 0, 3072, 4096, 6144, 8192, 11264, 11776
IN_SEGMENTS = [(0, 3072, P_QKV), (3072, 4096, P_ZG), (4096, 4112, P_SMALL), (4112, 4128, P_SMALL + 16),
               (4128, 6176, P_ZS), (6176, 9248, P_XBC), (9248, 9312, P_SMALL + 32), (9312, 11360, P_GATE)]
ADAM_LR, ADAM_B1, ADAM_B2, ADAM_EPS, ADAM_WD, ADAM_STEP = 0.001, 0.9, 0.999, 1e-08, 0.01, 10

VMEM_LIMIT = 48 * 1024 * 1024

WEIGHTS = ['c_ctx', 'ada_w', 'ada_b', 'norm1_w', 'w_in', 'gdn_conv_w', 'gdn_conv_b', 'gdn_a_log', 'gdn_dt_bias',
           'gdn_norm_w', 'ssm_conv_w', 'ssm_conv_b', 'ssm_a_log', 'ssm_dt_bias', 'ssm_d', 'ssm_norm_w', 'w_br_gdn',
           'w_br_ssm', 'w_out', 'norm2_w', 'w_ffn_in', 'w_ffn_out', 'norm_f_w']
SHARD_AXIS = {'ada_w': 1, 'w_in': 1, 'gdn_conv_w': 1, 'ssm_conv_w': 1, 'w_br_gdn': 0, 'w_br_ssm': 0, 'w_out': 0,
              'w_ffn_in': 1, 'w_ffn_out': 0}
SHARDED = [n for n in WEIGHTS if n in SHARD_AXIS]
SMALL = [n for n in WEIGHTS if n not in SHARD_AXIS]
MXU_WEIGHTS = ['ada_w', 'w_in', 'w_br_gdn', 'w_br_ssm', 'w_out', 'w_ffn_in', 'w_ffn_out']


def _cp(sem):
    return pltpu.CompilerParams(dimension_semantics=sem, vmem_limit_bytes=VMEM_LIMIT)


def _pick(n, cands):
    for c in cands:
        if n % c == 0:
            return c
    raise ValueError(f"no tile for {n}")


HBM = pl.BlockSpec(memory_space=pl.ANY)
MESH = pl.DeviceIdType.MESH


def _plane_peers():
    x, y, c = lax.axis_index("x"), lax.axis_index("y"), lax.axis_index("c")
    return (x, y, c), [(1 - x, y, c), (x, 1 - y, c), (1 - x, 1 - y, c)]


class PlaneExchange:
    def __init__(self, arrs, gather):
        self.arrs, self.gather, self.n = list(arrs), gather, len(arrs)
        self.in_specs = [HBM] * self.n
        self.out_specs = [HBM] * self.n
        self.out_shape = [jax.ShapeDtypeStruct((N_PLANE,) + a.shape if gather else a.shape, a.dtype) for a in self.arrs]
        self.scratch = [pltpu.SemaphoreType.DMA((3 * self.n,)), pltpu.SemaphoreType.DMA((3 * self.n,)),
                        pltpu.SemaphoreType.DMA((self.n,))]

    def _copies(self, ins, outs, sems):
        send_sems, recv_sems, local_sems = sems
        (x, y, c), peers = _plane_peers()
        me = 2 * x + y
        copies = []
        for ti in range(self.n):
            src = ins[ti] if self.gather else ins[ti].at[me]
            copies.append(pltpu.make_async_copy(src, outs[ti].at[me], local_sems.at[ti]))
            for kk, (px, py, pc) in enumerate(peers):
                src = ins[ti] if self.gather else ins[ti].at[2 * px + py]
                copies.append(pltpu.make_async_remote_copy(
                    src_ref=src, dst_ref=outs[ti].at[me], send_sem=send_sems.at[3 * ti + kk],
                    recv_sem=recv_sems.at[3 * ti + kk], device_id=(px, py, pc), device_id_type=MESH))
        return copies

    def start(self, ins, outs, sems):
        for cp in self._copies(ins, outs, sems):
            cp.start()

    def wait(self, ins, outs, sems):
        for cp in self._copies(ins, outs, sems):
            cp.wait()


def _plane_exchange(name, arrs, gather):
    ex = PlaneExchange(arrs, gather)
    n = ex.n

    def body(*refs):
        ins, outs, sems = refs[:n], refs[n:2 * n], refs[2 * n:]
        ex.start(ins, outs, sems)
        ex.wait(ins, outs, sems)

    return pl.pallas_call(body, in_specs=ex.in_specs, out_specs=ex.out_specs, out_shape=ex.out_shape,
                          scratch_shapes=ex.scratch, name=name)(*arrs)


def _gather_two_level(name, arrs):
    n = len(arrs)
    split = [a.shape[0] % 32 == 0 for a in arrs]

    def body(*refs):
        ins, outs = refs[:n], refs[n:2 * n]
        ici_send, ici_recv, d2d_send, d2d_recv, local_sems = refs[2 * n:]
        (x, y, c), peers = _plane_peers()
        me = 2 * x + y

        def part(ref, ti):
            if not split[ti]:
                return ref
            rows = arrs[ti].shape[0] // 2
            return ref.at[pl.ds(c * rows, rows)]

        local = [pltpu.make_async_copy(ins[ti], outs[ti].at[me], local_sems.at[ti]) for ti in range(n)]
        ici = [pltpu.make_async_remote_copy(
            src_ref=part(ins[ti], ti), dst_ref=part(outs[ti].at[me], ti), send_sem=ici_send.at[3 * ti + kk],
            recv_sem=ici_recv.at[3 * ti + kk], device_id=peer, device_id_type=MESH)
            for ti in range(n) for kk, peer in enumerate(peers)]
        for cp in local + ici:
            cp.start()
        d2d = []
        for ti in range(n):
            for kk, (px, py, pc) in enumerate(peers):
                ici[3 * ti + kk].wait_recv()
                if split[ti]:
                    piece = part(outs[ti].at[2 * px + py], ti)
                    cp = pltpu.make_async_remote_copy(
                        src_ref=piece, dst_ref=piece, send_sem=d2d_send.at[3 * ti + kk],
                        recv_sem=d2d_recv.at[3 * ti + kk], device_id=(x, y, 1 - c), device_id_type=MESH)
                    cp.start()
                    d2d.append(cp)
        for cp in ici:
            cp.wait_send()
        for cp in d2d:
            cp.wait()
        for cp in local:
            cp.wait()

    return pl.pallas_call(
        body, in_specs=[HBM] * n, out_specs=[HBM] * n,
        out_shape=[jax.ShapeDtypeStruct((N_PLANE,) + a.shape, a.dtype) for a in arrs],
        scratch_shapes=[pltpu.SemaphoreType.DMA((3 * n,))] * 4 + [pltpu.SemaphoreType.DMA((n,))], name=name)(*arrs)


def _gather_all(name, arr):
    flips = [(fx, fy, fc) for fx in (0, 1) for fy in (0, 1) for fc in (0, 1)][1:]

    def body(a_ref, o_ref, send_sems, recv_sems, local_sem):
        x, y, c = lax.axis_index("x"), lax.axis_index("y"), lax.axis_index("c")
        me = 4 * x + 2 * y + c
        copies = [pltpu.make_async_copy(a_ref, o_ref.at[me], local_sem)]
        for kk, (fx, fy, fc) in enumerate(flips):
            peer = (1 - x if fx else x, 1 - y if fy else y, 1 - c if fc else c)
            copies.append(pltpu.make_async_remote_copy(
                src_ref=a_ref, dst_ref=o_ref.at[me], send_sem=send_sems.at[kk], recv_sem=recv_sems.at[kk],
                device_id=peer, device_id_type=MESH))
        for cp in copies:
            cp.start()
        for cp in copies:
            cp.wait()

    return pl.pallas_call(
        body, in_specs=[HBM], out_specs=HBM, out_shape=jax.ShapeDtypeStruct((8,) + arr.shape, arr.dtype),
        scratch_shapes=[pltpu.SemaphoreType.DMA((7,)), pltpu.SemaphoreType.DMA((7,)), pltpu.SemaphoreType.DMA],
        name=name)(arr)


def _swap_sibling(arrs):
    n = len(arrs)

    def body(*refs):
        ins, outs, send_sems, recv_sems = refs[:n], refs[n:2 * n], refs[2 * n], refs[2 * n + 1]
        x, y, c = lax.axis_index("x"), lax.axis_index("y"), lax.axis_index("c")
        copies = [pltpu.make_async_remote_copy(src_ref=ins[ti], dst_ref=outs[ti], send_sem=send_sems.at[ti],
                                               recv_sem=recv_sems.at[ti], device_id=(x, y, 1 - c), device_id_type=MESH)
                  for ti in range(n)]
        for cp in copies:
            cp.start()
        for cp in copies:
            cp.wait()

    return pl.pallas_call(
        body, in_specs=[HBM] * n, out_specs=[HBM] * n, out_shape=[jax.ShapeDtypeStruct(a.shape, a.dtype) for a in arrs],
        scratch_shapes=[pltpu.SemaphoreType.DMA((n,)), pltpu.SemaphoreType.DMA((n,))], name="swap_sibling")(*arrs)


MATMUL_VMEM_BUDGET = 36 * 1024 * 1024
TILE_CANDIDATES = (2944, 2816, 1408, 1024, 768, 512, 256, 128)
STEP_COST_BYTES = 1 << 20
ACC_PASS_WEIGHT = 0.35


def _matmul_tiles(m, n, k, ab, bb, ob, tn_fixed=None, tk_fixed=None):
    def cands(dim, whole_up_to):
        out = [c for c in TILE_CANDIDATES if dim % c == 0]
        if dim <= whole_up_to and dim not in out:
            out.append(dim)
        return out

    best = None
    for tm in cands(m, 3072):
        for tn in ([tn_fixed] if tn_fixed else cands(n, 3072)):
            for tk in ([tk_fixed] if tk_fixed else cands(k, 2048)):
                gi, gj, gk = m // tm, n // tn, k // tk
                vmem = 2 * (tm * tk * ab + tk * tn * bb + tm * tn * ob) + (tm * tn * 4 if gk > 1 else 0)
                if vmem > MATMUL_VMEM_BUDGET:
                    continue
                a_reads = 1 if gk == 1 else gj
                b_reads = 1 if (gk == 1 and gj == 1) else gi
                acc = m * n * 8 * gk * ACC_PASS_WEIGHT if gk > 1 else 0
                cost = (m * k * ab * a_reads + k * n * bb * b_reads + m * n * ob + gi * gj * gk * STEP_COST_BYTES + acc)
                if best is None or cost < best[0]:
                    best = (cost, tm, tn, tk)
    assert best is not None, (m, n, k)
    return best[1:]


def _matmul(a, b, form, name, out_dtype=F32, stacked_out=False, ride=None, halves=None):
    dims = {'nn': (((1,), (0,)), ((), ())), 'nt': (((1,), (1,)), ((), ())), 'tn': (((0,), (0,)), ((), ()))}[form]
    stacked_b = b.ndim == 3 and halves != 'b'
    ns = b.shape[2] if stacked_b else None
    if form == 'nn':
        m, k = a.shape
        n = b.shape[0] * ns if stacked_b else b.shape[1]
    elif form == 'nt':
        m, k = (a.shape[1], 2 * a.shape[2]) if halves == 'a' else a.shape
        n = b.shape[1] if stacked_b else b.shape[0]
    else:
        k, m = a.shape
        n = 2 * b.shape[2] if halves == 'b' else b.shape[1]
    ob = jnp.dtype(out_dtype).itemsize
    tm, tn, tk = _matmul_tiles(m, n, k, a.dtype.itemsize, b.dtype.itemsize, ob,
                               tn_fixed=(n // N_PLANE if stacked_out else ns if (stacked_b and form == 'nn') else None),
                               tk_fixed=(ns if (stacked_b and form == 'nt') else None))
    nk = k // tk
    grid = (m // tm, n // tn, nk)
    n_ride = ride.n if ride is not None else 0

    def body(*refs):
        a_ref, b_ref = refs[0], refs[1]
        ride_in = refs[2:2 + n_ride]
        o_ref = refs[2 + n_ride]
        ride_out = refs[3 + n_ride:3 + 2 * n_ride]
        acc_ref = refs[3 + 2 * n_ride]
        sems = refs[4 + 2 * n_ride:]
        i, j, kk = pl.program_id(0), pl.program_id(1), pl.program_id(2)
        if ride is not None:
            @pl.when((i == 0) & (j == 0) & (kk == 0))
            def _():
                ride.start(ride_in, ride_out, sems)

        def put(val):
            if stacked_out or halves == 'out':
                o_ref[0] = val.astype(o_ref.dtype)
            else:
                o_ref[...] = val.astype(o_ref.dtype)

        av = a_ref[0] if halves == 'a' else a_ref[...]
        bv = b_ref[0] if (stacked_b or halves == 'b') else b_ref[...]
        part = lax.dot_general(av.astype(_MXU_DTYPE), bv.astype(_MXU_DTYPE), dims, preferred_element_type=F32)
        if nk == 1:
            put(part)
        else:
            @pl.when(kk == 0)
            def _():
                acc_ref[...] = part

            @pl.when(kk > 0)
            def _():
                acc_ref[...] += part

            @pl.when(kk == nk - 1)
            def _():
                put(acc_ref[...])

        if ride is not None:
            @pl.when((i == grid[0] - 1) & (j == grid[1] - 1) & (kk == nk - 1))
            def _():
                ride.wait(ride_in, ride_out, sems)

    if form == 'nn':
        a_spec = pl.BlockSpec((tm, tk), lambda i, j, kk: (i, kk))
        b_spec = (pl.BlockSpec((1, tk, tn), lambda i, j, kk: (j, kk, 0)) if stacked_b
                  else pl.BlockSpec((tk, tn), lambda i, j, kk: (kk, j)))
    elif form == 'nt':
        a_spec = pl.BlockSpec((tm, tk), lambda i, j, kk: (i, kk))
        b_spec = (pl.BlockSpec((1, tn, tk), lambda i, j, kk: (kk, j, 0)) if stacked_b
                  else pl.BlockSpec((tn, tk), lambda i, j, kk: (j, kk)))
    else:
        a_spec = pl.BlockSpec((tk, tm), lambda i, j, kk: (kk, i))
        b_spec = pl.BlockSpec((tk, tn), lambda i, j, kk: (kk, j))
    hj, hk = grid[1] // 2, nk // 2
    if halves == 'a':
        a_spec = pl.BlockSpec((1, tm, tk), lambda i, j, kk: (kk // hk, i, kk % hk))
    if halves == 'b':
        b_spec = pl.BlockSpec((1, tk, tn), lambda i, j, kk: (j // hj, kk, j % hj))
    if stacked_out:
        o_spec = pl.BlockSpec((1, tm, tn), lambda i, j, kk: (j, i, 0))
        o_shape = jax.ShapeDtypeStruct((N_PLANE, m, tn), out_dtype)
    elif halves == 'out':
        o_spec = pl.BlockSpec((1, tm, tn), lambda i, j, kk: (j // hj, i, j % hj))
        o_shape = jax.ShapeDtypeStruct((2, m, n // 2), out_dtype)
    else:
        o_spec = pl.BlockSpec((tm, tn), lambda i, j, kk: (i, j))
        o_shape = jax.ShapeDtypeStruct((m, n), out_dtype)
    acc_shape = (tm, tn) if nk > 1 else (8, 128)
    if ride is None:
        return pl.pallas_call(
            body, grid=grid, in_specs=[a_spec, b_spec], out_specs=o_spec, out_shape=o_shape,
            scratch_shapes=[pltpu.VMEM(acc_shape, F32)],
            compiler_params=_cp(("parallel", "parallel", "arbitrary")), name=name)(a, b)
    res = pl.pallas_call(
        body, grid=grid, in_specs=[a_spec, b_spec] + ride.in_specs, out_specs=[o_spec] + ride.out_specs,
        out_shape=[o_shape] + ride.out_shape, scratch_shapes=[pltpu.VMEM(acc_shape, F32)] + ride.scratch,
        compiler_params=_cp(("arbitrary", "arbitrary", "arbitrary")), name=name)(a, b, *ride.arrs)
    return res[0], res[1:]


class Op:
    def __init__(self, arr, bs, im, load=None):
        self.arr, self.bs, self.im = arr, bs, im
        self.arrs = list(arr) if isinstance(arr, list) else [arr]
        self.load = load or (lambda r: r[...].astype(F32))

    def spec(self):
        return pl.BlockSpec(self.bs, self.im)

    def value(self, it):
        return functools.reduce(lambda u, w: u + w, [self.load(next(it)) for _ in self.arrs])


def _op_specs(ops):
    return [op.spec() for op in ops for _ in op.arrs]


def _op_arrays(ops):
    return [a for op in ops for a in op.arrs]


def _sum_dirs(r):
    return r[0].astype(F32) + r[1].astype(F32)


def _tw_fwd(name, fn, grid, ins, outs):
    def body(*refs):
        info = (pl.program_id(0), pl.program_id(1))
        it = iter(refs)
        vals = [op.value(it) for op in ins]
        res = fn(info, *vals)
        for r, v in zip(it, res):
            r[...] = v.astype(r.dtype)

    return pl.pallas_call(
        body, grid=grid, in_specs=_op_specs(ins), out_specs=[op.spec() for op in outs],
        out_shape=[jax.ShapeDtypeStruct(*op.arr) for op in outs],
        compiler_params=_cp(("parallel", "arbitrary")), name=name)(*_op_arrays(ins))


def _tw_bwd(name, fn, grid, tok, par, cots, tok_out, par_out, tok_add=None, sem=("parallel", "arbitrary"), into=None):
    n_tok = len(tok)
    flat_cots = [op for group in cots for op in group]
    extra = [tok_add] if tok_add is not None else []
    out_ops = list(tok_out) + list(par_out)
    into = into or {}

    def body(*refs):
        info = (pl.program_id(0), pl.program_id(1))
        it = iter(refs)
        tok_v = [op.value(it) for op in tok]
        par_v = [op.value(it) for op in par]
        cot_v = [functools.reduce(lambda u, w: u + w, [op.value(it) for op in group]) for group in cots]
        add_v = [op.value(it) for op in extra]
        for _ in into:
            next(it)
        _, pull = jax.vjp(lambda *a: fn(info, *a), *tok_v, *par_v)
        grads = pull(tuple(cot_v))
        for i in range(n_tok):
            r = next(it)
            g = grads[i] + add_v[0] if (i == 0 and add_v) else grads[i]
            if r.shape[-1] > g.shape[-1]:
                g = jnp.concatenate([g, jnp.zeros((g.shape[0], r.shape[-1] - g.shape[-1]), g.dtype)], axis=1)
            r[...] = g.astype(r.dtype)
        first = pl.program_id(1) == 0
        for i in range(len(par)):
            r = next(it)
            g = grads[n_tok + i]

            @pl.when(first)
            def _(r=r, g=g):
                r[...] = g

            @pl.when(jnp.logical_not(first))
            def _(r=r, g=g):
                r[...] += g

    ops = tok + par + flat_cots + extra
    n_in = len(_op_arrays(ops))
    return pl.pallas_call(
        body, grid=grid, in_specs=_op_specs(ops) + [HBM] * len(into), out_specs=[op.spec() for op in out_ops],
        out_shape=[jax.ShapeDtypeStruct(*op.arr) for op in out_ops],
        input_output_aliases={n_in + k: i for k, i in enumerate(into)},
        compiler_params=_cp(sem), name=name)(*_op_arrays(ops), *into.values())


def _silu(x):
    return x * jax.nn.sigmoid(x)


def _rms(x):
    return x * lax.rsqrt(jnp.mean(x * x, axis=-1, keepdims=True) + EPS)


@functools.partial(jax.custom_vjp, nondiff_argnums=(1,))
def _shift_rows(x, k):
    return pltpu.roll(x, k % x.shape[0], 0)


def _shift_rows_fwd(x, k):
    return _shift_rows(x, k), None


def _shift_rows_bwd(k, _, g):
    return (_shift_rows(g, -k),)


_shift_rows.defvjp(_shift_rows_fwd, _shift_rows_bwd)


def _prenorm_fn(nctx_t, info, x, w, sc8, sh8):
    is_ctx = info[1] < nctx_t
    sc = jnp.where(is_ctx, sc8[1:2], sc8[0:1])
    sh = jnp.where(is_ctx, sh8[1:2], sh8[0:1])
    return (_rms(x) * w * (1.0 + sc) + sh,)


def _conv_fn(lc, mode, info, x, w, b):
    n, c = x.shape
    g = info[1] * n + lax.broadcasted_iota(jnp.int32, (n, 1), 0)
    is_ctx = g < lc
    rr = jnp.where(is_ctx, g, g % GRID_W)
    first = rr == 0
    last = rr == jnp.where(is_ctx, lc - 1, GRID_W - 1)
    prev = jnp.where(first, 0.0, _shift_rows(x, 1))
    nxt = jnp.where(last, 0.0, _shift_rows(x, -1))
    y = b + prev * w[0:1] + x * w[1:2] + nxt * w[2:3]
    y = _silu(y)
    if mode == 'none':
        return (y,)
    scale = GDN_DK ** -0.5 if mode == 'q' else 1.0
    outs = []
    for h in range(c // 128):
        yh = y[:, h * 128:(h + 1) * 128]
        outs.append(yh * lax.rsqrt(jnp.sum(yh * yh, axis=-1, keepdims=True) + EPS) * scale)
    return (jnp.concatenate(outs, axis=1),)


def _act_fn(info, x, p0, p1):
    lane = lax.broadcasted_iota(jnp.int32, x.shape, 1)
    sp = jax.nn.softplus(x + p1)
    g = -jnp.exp(p0) * sp
    bt = jax.nn.sigmoid(x)
    return (jnp.where(lane < 16, g, jnp.where(lane < 32, bt, jnp.where(lane < 96, sp, 0.0))),)


def _mixg_fn(info, o, zg, gnw):
    outs = []
    for h in range(GDN_HEADS):
        outs.append(_rms(o[:, h * 128:(h + 1) * 128]) * gnw)
    return (jnp.concatenate(outs, axis=1) * _silu(zg),)


def _mixs_fn(info, y, xs, zs, dl, snw):
    yy = (y + dl * xs) * _silu(zs)
    outs = []
    for g in range(4):
        outs.append(_rms(yy[:, g * 512:(g + 1) * 512]))
    return (jnp.concatenate(outs, axis=1) * snw,)


def _merge_fn(info, gates, pg, ps):
    half = gates.shape[1] // 2
    return (jax.nn.sigmoid(gates[:, :half]) * pg + jax.nn.sigmoid(gates[:, half:]) * ps,)


def _norm2_fn(info, xt, mo, g8, w, sc8, sh8):
    h1 = xt + g8[0:1] * mo
    return (h1, _rms(h1) * w * (1.0 + sc8[0:1]) + sh8[0:1])


def _swiglu_fn(info, u2):
    return (_silu(u2[0]) * u2[1],)


_NN = (((1,), (0,)), ((), ()))
_NT = (((1,), (1,)), ((), ()))
_TN = (((0,), (0,)), ((), ()))


def _mmh(a, b):
    return lax.dot_general(a, b, _NN, precision=HI, preferred_element_type=F32)


def _dot1(a, b, dims):
    return lax.dot_general(a.astype(_SCAN_DTYPE), b.astype(_SCAN_DTYPE), dims, preferred_element_type=F32)


def _mm(a, b):
    return _dot1(a, b, _NN)


def _mm_nt(a, b):
    return _dot1(a, b, _NT)


def _mm_tn(a, b):
    return _dot1(a, b, _TN)


def _split2(a):
    hi = a.astype(_SCAN_DTYPE)
    return hi, (a - hi.astype(F32)).astype(_SCAN_DTYPE)


def _dot3(a, b, dims):
    ah, al = _split2(a)
    bh, bl = _split2(b)
    d = lambda u, w: lax.dot_general(u, w, dims, preferred_element_type=F32)
    return d(ah, bh) + (d(ah, bl) + d(al, bh))


def _order_masks(d):
    i = lax.broadcasted_iota(jnp.int32, (CHUNK, CHUNK), 0)
    j = lax.broadcasted_iota(jnp.int32, (CHUNK, CHUNK), 1)
    s = jnp.where(d == 0, 1, -1) * (i - j)
    return (s >= 0).astype(F32), (s > 0).astype(F32)


@jax.custom_vjp
def _unit_tri_inv(mats):
    i = lax.broadcasted_iota(jnp.int32, (CHUNK, CHUNK), 0)
    j = lax.broadcasted_iota(jnp.int32, (CHUNK, CHUNK), 1)
    eye = (i == j).astype(F32)
    ps = [-a for a in mats]
    ts = [eye + p for p in ps]
    for _ in range(5):
        ps = [_dot3(p, p, _NN) for p in ps]
        ts = [t + _dot3(t, p, _NN) for t, p in zip(ts, ps)]
    return tuple(ts)


def _uti_fwd(mats):
    ts = _unit_tri_inv(mats)
    return ts, ts


def _uti_bwd(ts, gs):
    inner = [_dot3(g, t, _NT) for g, t in zip(gs, ts)]
    return (tuple(-_dot3(t, u, _TN) for t, u in zip(ts, inner)),)


_unit_tri_inv.defvjp(_uti_fwd, _uti_bwd)


@jax.custom_vjp
def _unit_tri_inv_given(mats, ts):
    return ts


def _utig_fwd(mats, ts):
    return ts, ts


def _utig_bwd(ts, gs):
    return _uti_bwd(ts, gs)[0], tuple(jnp.zeros_like(t) for t in ts)


_unit_tri_inv_given.defvjp(_utig_fwd, _utig_bwd)


def _lane_col(blk, lane_idx):
    lane = lax.broadcasted_iota(jnp.int32, blk.shape, 1)
    return jnp.sum(jnp.where(lane == lane_idx, blk, 0.0), axis=1, keepdims=True)


def _decay_mat(cum, incl):
    cb = jnp.broadcast_to(cum, (CHUNK, CHUNK))
    return jnp.exp(jnp.minimum(cb - cb.T, 0.0)) * incl


def _gdn_chunks(dirs, streams, states, aux=None, want_aux=False, group=None):
    ns = len(dirs)
    masks = [_order_masks(d) for d in dirs]
    cum = [_mmh(masks[i][0], streams[i][3]) for i in range(ns)]
    tot = [jnp.sum(streams[i][3], axis=0, keepdims=True) for i in range(ns)]
    all_units = [(i, h) for i in range(ns) for h in range(GDN_HEADS)]
    group = group or len(all_units)
    cat = jnp.concatenate
    outs, new_states, ts_all = [], [], []
    for g0 in range(0, len(all_units), group):
        units = all_units[g0:g0 + group]
        us = range(len(units))
        st = states[g0:g0 + group]
        qs = [streams[i][0][:, h * 128:(h + 1) * 128] for i, h in units]
        ks = [streams[i][1][:, h * 128:(h + 1) * 128] for i, h in units]
        vs = [streams[i][2][:, h * 128:(h + 1) * 128] for i, h in units]
        gcum = [_lane_col(cum[i], dirs[i] * GDN_HEADS + h) for i, h in units]
        glast = [_lane_col(tot[i], dirs[i] * GDN_HEADS + h) for i, h in units]
        beta = [_lane_col(streams[i][3], 16 + dirs[i] * GDN_HEADS + h) for i, h in units]
        decay = [_decay_mat(gcum[u], masks[units[u][0]][0]) for u in us]
        egc = [jnp.exp(gcum[u]) for u in us]
        kb = [ks[u] * beta[u] for u in us]
        kq = [_mm_nt(cat([kb[u], qs[u]], axis=0), ks[u]) for u in us]
        mats = tuple(kq[u][:CHUNK] * decay[u] * masks[units[u][0]][1] for u in us)
        ts = _unit_tri_inv(mats) if aux is None else _unit_tri_inv_given(mats, tuple(aux[g0:g0 + group]))
        wu = [_mm(ts[u], cat([kb[u] * egc[u], vs[u] * beta[u]], axis=1)) for u in us]
        ws = [_mm(cat([wu[u][:, :128], qs[u] * egc[u]], axis=0), st[u]) for u in us]
        vn = [wu[u][:, 128:] - ws[u][:CHUNK] for u in us]
        outs += [ws[u][CHUNK:] + _mm(kq[u][CHUNK:] * decay[u], vn[u]) for u in us]
        new_states += [st[u] * jnp.exp(glast[u]) + _mm_tn(ks[u] * jnp.exp(glast[u] - gcum[u]), vn[u]) for u in us]
        ts_all += list(ts)
    per_stream = [cat(outs[i * GDN_HEADS:(i + 1) * GDN_HEADS], axis=1) for i in range(ns)]
    return (*per_stream, *new_states, *(ts_all if want_aux else ()))


def _gdn_step(d, q, k, v, spb, *states, aux=None, want_aux=False):
    return _gdn_chunks([d], [(q, k, v, spb)], states, aux, want_aux, group=4)


def _gdn_step2(q0, k0, v0, sp0, q1, k1, v1, sp1, *states, aux=None, want_aux=False):
    return _gdn_chunks([0, 1], [(q0, k0, v0, sp0), (q1, k1, v1, sp1)], states, aux, want_aux)


def _split3(a):
    a1 = a.astype(_SCAN_DTYPE)
    r = a - a1.astype(F32)
    a2 = r.astype(_SCAN_DTYPE)
    return a1, a2, (r - a2.astype(F32)).astype(_SCAN_DTYPE)


def _exact_dot(a, e, dims, split_lhs, passes=3):
    parts = _split3(a if split_lhs else e)[:passes]
    d = (lambda u: lax.dot_general(u, e, dims, preferred_element_type=F32)) if split_lhs else \
        (lambda u: lax.dot_general(a, u, dims, preferred_element_type=F32))
    return functools.reduce(lambda u, w: u + w, [d(p) for p in reversed(parts)])


@jax.custom_vjp
def _spread(a, e):
    return _exact_dot(a, e, _NN, True)


def _spread_fwd(a, e):
    return _spread(a, e), e


def _spread_bwd(e, g):
    return _exact_dot(g, e, _NT, True, passes=2), jnp.zeros_like(e)


_spread.defvjp(_spread_fwd, _spread_bwd)


@jax.custom_vjp
def _colsum_bcast(z):
    return _exact_dot(jnp.ones((z.shape[0], z.shape[0]), _SCAN_DTYPE), z, _NN, False)


def _colsum_fwd(z):
    return _colsum_bcast(z), None


def _colsum_bwd(_, g):
    return (_exact_dot(jnp.ones((g.shape[0], g.shape[0]), _SCAN_DTYPE), g, _NN, False, passes=2),)


_colsum_bcast.defvjp(_colsum_fwd, _colsum_bwd)


def _ssd_consts():
    wdt = SSM_HEADS * 64
    d = lax.broadcasted_iota(jnp.int32, (2, 128, wdt), 0)
    e = (lax.broadcasted_iota(jnp.int32, (2, 128, wdt), 1)
         == 32 + d * SSM_HEADS + lax.broadcasted_iota(jnp.int32, (2, 128, wdt), 2) // 64).astype(_SCAN_DTYPE)
    dd = lax.broadcasted_iota(jnp.int32, (2, CHUNK, wdt), 0)
    ci = lax.broadcasted_iota(jnp.int32, (2, CHUNK, wdt), 1)
    pos = lax.broadcasted_iota(jnp.int32, (2, CHUNK, wdt), 2) % 64
    incl_t = (jnp.where(dd == 0, 1, -1) * (ci - pos) >= 0).astype(F32)
    diag = (ci == pos).astype(F32)
    return [e, incl_t, diag]


def _ssd_step(d, x, bm, cm, spb, alog, *states, consts):
    e, incl_t, diag = consts
    incl, _ = _order_masks(d)
    lane1 = lax.broadcasted_iota(jnp.int32, (1, 128), 1)
    lo_lane = 32 + d * SSM_HEADS
    a_vec = jnp.where(lane1 >= lo_lane, jnp.where(lane1 < lo_lane + SSM_HEADS, -jnp.exp(alog), 0.0), 0.0)
    adt = spb * a_vec
    acum = _mmh(incl, adt)
    alast = jnp.sum(adt, axis=0, keepdims=True)
    dt2 = _spread(spb, e)
    ac2 = _spread(acum, e)
    al2 = _spread(jnp.broadcast_to(alast, (8, 128)), e)[0:1]
    row = _colsum_bcast(ac2 * diag)
    seg = jnp.exp(jnp.minimum(ac2 - row, 0.0)) * incl_t
    xdt = x * dt2
    gam = jnp.exp(ac2)
    xe = xdt * jnp.exp(al2 - ac2)
    low = lax.broadcasted_iota(jnp.int32, (CHUNK, 128), 1) < 64
    row_low = lax.broadcasted_iota(jnp.int32, (128, 1), 0) < 64
    ps = range(SSM_PAIRS)
    sl = [slice(p * 128, (p + 1) * 128) for p in ps]
    bg = [bm[:, g * 128:(g + 1) * 128] for g in range(4)]
    cg = [cm[:, g * 128:(g + 1) * 128] for g in range(4)]
    cb2 = [_mm_nt(cg[g], jnp.concatenate([bg[g], bg[g]], axis=0)) for g in range(4)]
    ys, new_states = [], []
    for p0 in range(0, SSM_PAIRS, SSD_GROUP):
        pg = range(p0, p0 + SSD_GROUP)
        xd = {p: jnp.concatenate([jnp.where(low, xdt[:, sl[p]], 0.0), jnp.where(low, 0.0, xdt[:, sl[p]])], axis=0)
              for p in pg}
        yd = {p: _mm(cb2[p // 4] * seg[:, sl[p]], xd[p]) for p in pg}
        yo = {p: _mm_nt(cg[p // 4], states[p]) for p in pg}
        ys += [yd[p] + gam[:, sl[p]] * yo[p] for p in pg]
        new = {p: _mm_tn(xe[:, sl[p]], bg[p // 4]) for p in pg}
        al0 = {p: _lane_col(alast, lo_lane + 2 * p) for p in pg}
        al1 = {p: _lane_col(alast, lo_lane + 2 * p + 1) for p in pg}
        new_states += [states[p] * jnp.exp(jnp.where(row_low, al0[p], al1[p])) + new[p] for p in pg]
    return (jnp.concatenate(ys, axis=1), *new_states)


def _chunk_of(d, p, nctx, nc):
    return jnp.where(d == 0, p, jnp.where(p < nctx, nctx - 1 - p, nctx + nc - 1 - p))


class _NoRide:
    n, arrs, in_specs, out_specs, out_shape, scratch = 0, [], [], [], [], []


def _const_specs(consts):
    return [pl.BlockSpec((1,) + a.shape[1:], lambda d, p: (d,) + (0,) * (a.ndim - 1)) for a in consts]


def _scan_fwd(name, step, toks, pars, consts, out_width, n_state, nctx):
    t = toks[0].shape[0]
    nc = t // CHUNK
    n_tok, n_par, n_const = len(toks), len(pars), len(consts)

    def body(*refs):
        it = iter(refs)
        tok_refs = [next(it) for _ in range(n_tok)]
        par_refs = [next(it) for _ in range(n_par)]
        const_refs = [next(it) for _ in range(n_const)]
        o_ref, ss_ref, s_scr = next(it), next(it), next(it)
        d, p = pl.program_id(0), pl.program_id(1)

        @pl.when(p == 0)
        def _():
            s_scr[...] = jnp.zeros(s_scr.shape, F32)

        ss_ref[0, 0] = s_scr[...]
        res = step(d, *[r[...] for r in tok_refs], *[r[...] for r in par_refs], *[s_scr[h] for h in range(n_state)],
                   consts=[r[0] for r in const_refs])
        o_ref[0] = res[0]
        for h in range(n_state):
            s_scr[h] = res[1 + h]

    ch = lambda d, p: _chunk_of(d, p, nctx, nc)
    in_specs = [pl.BlockSpec((CHUNK, a.shape[1]), lambda d, p: (ch(d, p), 0)) for a in toks]
    in_specs += [pl.BlockSpec(a.shape, lambda d, p: (0, 0)) for a in pars]
    return pl.pallas_call(
        body, grid=(2, nc), in_specs=in_specs + _const_specs(consts),
        out_specs=[pl.BlockSpec((1, CHUNK, out_width), lambda d, p: (d, ch(d, p), 0)),
                   pl.BlockSpec((1, 1, n_state, 128, 128), lambda d, p: (d, p, 0, 0, 0))],
        out_shape=[jax.ShapeDtypeStruct((2, t, out_width), F32),
                   jax.ShapeDtypeStruct((2, nc, n_state, 128, 128), F32)],
        scratch_shapes=[pltpu.VMEM((n_state, 128, 128), F32)],
        compiler_params=_cp(("arbitrary", "arbitrary")), name=name)(*toks, *pars, *consts)


def _scan_bwd(name, step, toks, pars, ss, dout, n_state, nctx, ride=None, aux=None, consts=()):
    t = toks[0].shape[0]
    nc = t // CHUNK
    n_tok, n_par, n_const = len(toks), len(pars), len(consts)
    rd = ride if ride is not None else _NoRide
    n_aux = aux.shape[2] if aux is not None else 0

    def body(*refs):
        it = iter(refs)
        tok_refs = [next(it) for _ in range(n_tok)]
        par_refs = [next(it) for _ in range(n_par)]
        ss_ref, do_ref = next(it), next(it)
        aux_ref = next(it) if n_aux else None
        const_refs = [next(it) for _ in range(n_const)]
        ride_in = [next(it) for _ in range(rd.n)]
        dtok_refs = [next(it) for _ in range(n_tok)]
        dpar_refs = [next(it) for _ in range(n_par)]
        ride_out = [next(it) for _ in range(rd.n)]
        ds_scr = next(it)
        sems = list(it)
        d, pr = pl.program_id(0), pl.program_id(1)
        if ride is not None:
            @pl.when((d == 0) & (pr == 0))
            def _():
                ride.start(ride_in, ride_out, sems)

            @pl.when((d == 1) & (pr == nc - 1))
            def _():
                ride.wait(ride_in, ride_out, sems)

        @pl.when(pr == 0)
        def _():
            ds_scr[...] = jnp.zeros(ds_scr.shape, F32)

        kw = dict(aux=[aux_ref[0, 0, i] for i in range(n_aux)]) if n_aux else {}
        if n_const:
            kw['consts'] = [r[0] for r in const_refs]
        _, pull = jax.vjp(functools.partial(step, d, **kw), *[r[...] for r in tok_refs], *[r[...] for r in par_refs],
                          *[ss_ref[0, 0, h] for h in range(n_state)])
        grads = pull((do_ref[...], *[ds_scr[h] for h in range(n_state)]))
        for r, g in zip(dtok_refs, grads[:n_tok]):
            r[0] = g
        for h in range(n_state):
            ds_scr[h] = grads[n_tok + n_par + h]
        first = (d == 0) & (pr == 0)
        for r, g in zip(dpar_refs, grads[n_tok:n_tok + n_par]):
            @pl.when(first)
            def _(r=r, g=g):
                r[...] = g

            @pl.when(jnp.logical_not(first))
            def _(r=r, g=g):
                r[...] += g

    ch = lambda d, pr: _chunk_of(d, nc - 1 - pr, nctx, nc)
    in_specs = [pl.BlockSpec((CHUNK, a.shape[1]), lambda d, pr: (ch(d, pr), 0)) for a in toks]
    in_specs += [pl.BlockSpec(a.shape, lambda d, pr: (0, 0)) for a in pars]
    in_specs += [pl.BlockSpec((1, 1, n_state, 128, 128), lambda d, pr: (d, nc - 1 - pr, 0, 0, 0)),
                 pl.BlockSpec((CHUNK, dout.shape[1]), lambda d, pr: (ch(d, pr), 0))]
    if n_aux:
        in_specs += [pl.BlockSpec((1, 1, n_aux, CHUNK, CHUNK), lambda d, pr: (d, nc - 1 - pr, 0, 0, 0))]
    in_specs += _const_specs(consts)
    out_specs = [pl.BlockSpec((1, CHUNK, a.shape[1]), lambda d, pr: (d, ch(d, pr), 0)) for a in toks]
    out_specs += [pl.BlockSpec(a.shape, lambda d, pr: (0, 0)) for a in pars]
    out_shape = [jax.ShapeDtypeStruct((2, t, a.shape[1]), F32) for a in toks]
    out_shape += [jax.ShapeDtypeStruct(a.shape, F32) for a in pars]
    return pl.pallas_call(
        body, grid=(2, nc), in_specs=in_specs + rd.in_specs, out_specs=out_specs + rd.out_specs,
        out_shape=out_shape + rd.out_shape, scratch_shapes=[pltpu.VMEM((n_state, 128, 128), F32)] + rd.scratch,
        compiler_params=_cp(("arbitrary", "arbitrary")), name=name)(
            *toks, *pars, ss, dout, *([aux] if n_aux else []), *consts, *rd.arrs)


def _scan2_fwd(name, step2, toks, out_width, n_state, nctx, n_aux, ride=None):
    t = toks[0].shape[0]
    nc = t // CHUNK
    n_tok = len(toks)
    rd = ride if ride is not None else _NoRide

    def body(*refs):
        it = iter(refs)
        tok_refs = [next(it) for _ in range(2 * n_tok)]
        ride_in = [next(it) for _ in range(rd.n)]
        o_refs = [next(it), next(it)]
        ss_ref, aux_ref = next(it), next(it)
        ride_out = [next(it) for _ in range(rd.n)]
        s_scr = next(it)
        sems = list(it)
        p = pl.program_id(0)
        if ride is not None:
            @pl.when(p == 0)
            def _():
                ride.start(ride_in, ride_out, sems)

        @pl.when(p == 0)
        def _():
            s_scr[...] = jnp.zeros(s_scr.shape, F32)

        for d in range(2):
            ss_ref[d, 0] = s_scr[d * n_state:(d + 1) * n_state]
        res = step2(*[r[...] for r in tok_refs], *[s_scr[u] for u in range(2 * n_state)], want_aux=True)
        for d in range(2):
            o_refs[d][...] = res[d]
            for i in range(n_aux):
                aux_ref[d, 0, i] = res[2 + 2 * n_state + d * n_aux + i]
        for u in range(2 * n_state):
            s_scr[u] = res[2 + u]
        if ride is not None:
            @pl.when(p == nc - 1)
            def _():
                ride.wait(ride_in, ride_out, sems)

    def tok_spec(a, d):
        return pl.BlockSpec((CHUNK, a.shape[1]), lambda p: (_chunk_of(d, p, nctx, nc), 0))

    return pl.pallas_call(
        body, grid=(nc,), in_specs=[tok_spec(a, d) for d in range(2) for a in toks] + rd.in_specs,
        out_specs=[pl.BlockSpec((CHUNK, out_width), lambda p: (_chunk_of(0, p, nctx, nc), 0)),
                   pl.BlockSpec((CHUNK, out_width), lambda p: (_chunk_of(1, p, nctx, nc), 0)),
                   pl.BlockSpec((2, 1, n_state, 128, 128), lambda p: (0, p, 0, 0, 0)),
                   pl.BlockSpec((2, 1, n_aux, CHUNK, CHUNK), lambda p: (0, p, 0, 0, 0))] + rd.out_specs,
        out_shape=[jax.ShapeDtypeStruct((t, out_width), F32), jax.ShapeDtypeStruct((t, out_width), F32),
                   jax.ShapeDtypeStruct((2, nc, n_state, 128, 128), F32),
                   jax.ShapeDtypeStruct((2, nc, n_aux, CHUNK, CHUNK), F32)] + rd.out_shape,
        scratch_shapes=[pltpu.VMEM((2 * n_state, 128, 128), F32)] + rd.scratch,
        compiler_params=_cp(("arbitrary",)), name=name)(*toks, *toks, *rd.arrs)


ADA_TN = 512


def _ada_part(cc16, w_shard):
    n = w_shard.shape[1]

    def body(c_ref, w_ref, o_ref):
        s = _silu(c_ref[...]).astype(_MXU_DTYPE)
        o_ref[...] = jnp.dot(s, w_ref[...].astype(_MXU_DTYPE), preferred_element_type=F32)

    return pl.pallas_call(
        body, grid=(n // ADA_TN,),
        in_specs=[pl.BlockSpec((16, D_MODEL), lambda j: (0, 0)), pl.BlockSpec((D_MODEL, ADA_TN), lambda j: (0, j))],
        out_specs=pl.BlockSpec((16, ADA_TN), lambda j: (0, j)), out_shape=jax.ShapeDtypeStruct((16, n), F32),
        compiler_params=_cp(("parallel",)), name="ada_part")(cc16, w_shard)


def _ada_bwd_shard(cc16, w_shard, d_lat, d_ctx):
    n = w_shard.shape[1]
    nj = n // ADA_TN

    def body(c_ref, w_ref, gl_ref, gc_ref, dw_ref, dc_ref):
        j = pl.program_id(0)
        s, pull = jax.vjp(_silu, c_ref[...])
        row = lax.broadcasted_iota(jnp.int32, (8, 1), 0)
        dctx = jnp.where(row == 0, jnp.sum(gc_ref[...], axis=0, keepdims=True), 0.0)
        mxu = lambda v: v.astype(_MXU_DTYPE)
        dw_ref[...] = (lax.dot_general(mxu(s[0:8]), mxu(gl_ref[...]), _TN, preferred_element_type=F32)
                       + lax.dot_general(mxu(s[8:16]), mxu(dctx), _TN, preferred_element_type=F32))
        ds = lax.dot_general(mxu(dctx), mxu(w_ref[...]), _NT, preferred_element_type=F32)

        @pl.when(j == 0)
        def _():
            dc_ref[...] = ds

        @pl.when(j > 0)
        def _():
            dc_ref[...] += ds

        @pl.when(j == nj - 1)
        def _():
            ct = jnp.concatenate([jnp.zeros((8, D_MODEL), F32), dc_ref[...]], axis=0)
            dc_ref[...] = 0.5 * pull(ct)[0][8:16]

    tile = pl.BlockSpec((8, ADA_TN), lambda j: (0, j))
    wspec = pl.BlockSpec((D_MODEL, ADA_TN), lambda j: (0, j))
    return pl.pallas_call(
        body, grid=(nj,), in_specs=[pl.BlockSpec((16, D_MODEL), lambda j: (0, 0)), wspec, tile, tile],
        out_specs=[wspec, pl.BlockSpec((8, D_MODEL), lambda j: (0, 0))],
        out_shape=[jax.ShapeDtypeStruct((D_MODEL, n), F32), jax.ShapeDtypeStruct((8, D_MODEL), F32)],
        compiler_params=_cp(("arbitrary",)), name="ada_bwd_shard")(cc16, w_shard, d_lat, d_ctx)


def _rowsum2(a, b):
    def body(a_ref, b_ref, o_ref):
        o_ref[...] = jnp.sum(a_ref[...], axis=0, keepdims=True) + jnp.sum(b_ref[...], axis=0, keepdims=True)

    return pl.pallas_call(body, out_shape=jax.ShapeDtypeStruct((1, a.shape[1]), F32), name="ada_b_grad")(a, b)


def _tail(h1, ff, mods, wf, tgt, nctx_t, tl):
    t = h1.shape[0]
    nt = t // tl

    def loss_fn(valid, h1v, ffv, g8, w, tg):
        h2 = h1v + g8[0:1] * ffv
        y = _rms(h2) * w
        err = (y - tg) ** 2
        return 0.5 * jnp.sum(jnp.mean(err, axis=-1, keepdims=True), axis=0, keepdims=True) * valid

    def body(h1_ref, ff_ref, g_ref, w_ref, t_ref, loss_ref, dh_ref, dff_ref, dg_ref, dw_ref):
        i = pl.program_id(0)
        valid = jnp.where(i < nctx_t, 0.0, 1.0)
        lv, pull = jax.vjp(functools.partial(loss_fn, valid), h1_ref[...], ff_ref[...].astype(F32), g_ref[...],
                           w_ref[...], t_ref[...])
        dh, dff, dg, dw, _ = pull(jnp.ones((1, 1), F32))
        dh_ref[...] = dh
        dff_ref[...] = dff.astype(dff_ref.dtype)
        lb = jnp.broadcast_to(lv, loss_ref.shape)

        @pl.when(i == 0)
        def _():
            loss_ref[...] = lb
            dg_ref[...] = dg
            dw_ref[...] = dw

        @pl.when(i > 0)
        def _():
            loss_ref[...] += lb
            dg_ref[...] += dg
            dw_ref[...] += dw

    tok = pl.BlockSpec((tl, D_MODEL), lambda i: (i, 0))
    return pl.pallas_call(
        body, grid=(nt,),
        in_specs=[tok, tok, pl.BlockSpec((8, D_MODEL), lambda i: (0, 5)), pl.BlockSpec((1, D_MODEL), lambda i: (0, 0)),
                  pl.BlockSpec((tl, D_MODEL), lambda i: (jnp.maximum(i - nctx_t, 0), 0))],
        out_specs=[pl.BlockSpec((8, 128), lambda i: (0, 0)), tok, tok, pl.BlockSpec((8, D_MODEL), lambda i: (0, 0)),
                   pl.BlockSpec((1, D_MODEL), lambda i: (0, 0))],
        out_shape=[jax.ShapeDtypeStruct((8, 128), F32), jax.ShapeDtypeStruct((t, D_MODEL), F32),
                   jax.ShapeDtypeStruct((t, D_MODEL), _MXU_DTYPE), jax.ShapeDtypeStruct((8, D_MODEL), F32),
                   jax.ShapeDtypeStruct((1, D_MODEL), F32)],
        compiler_params=_cp(("arbitrary",)), name="tail_loss")(h1, ff, mods, wf, tgt)


def _pack_w_in(w4):
    ns = w4.shape[2]
    placed = []
    for s0, s1, p0 in IN_SEGMENTS:
        for j in range(N_PLANE):
            lo, hi = max(s0, j * ns), min(s1, (j + 1) * ns)
            if lo < hi:
                placed.append((p0 + lo - s0, w4[j][:, lo - j * ns:hi - j * ns]))
    placed.sort(key=lambda e: e[0])
    pieces, end = [], 0
    for pos, piece in placed:
        assert pos == end, (pos, end)
        pieces.append(piece)
        end = pos + piece.shape[1]
    pieces.append(jnp.zeros((w4.shape[1], P_TOTAL - end), w4.dtype))
    return jnp.concatenate(pieces, axis=1)


def _unpack_w_in(g):
    ns = D_IN_PROJ // N_PLANE
    shards = []
    for j in range(N_PLANE):
        pieces = []
        for s0, s1, p0 in IN_SEGMENTS:
            lo, hi = max(s0, j * ns), min(s1, (j + 1) * ns)
            if lo < hi:
                pieces.append(g[:, p0 + lo - s0:p0 + hi - s0])
        shards.append(jnp.concatenate(pieces, axis=1))
    return jnp.stack(shards)


LATE_WEIGHTS = ['w_br_gdn', 'w_br_ssm', 'w_out', 'w_ffn_in', 'w_ffn_out']
COL_STACKED = ('w_in', 'w_ffn_in')


def _from_shards(n, g):
    if n in COL_STACKED:
        return g
    if SHARD_AXIS[n] == 0:
        return g.reshape(N_PLANE * g.shape[1], g.shape[2])
    return jnp.concatenate([g[j] for j in range(N_PLANE)], axis=1)


def _to_dest_blocks(n, g):
    if g.ndim == 3:
        return g
    if SHARD_AXIS[n] == 0:
        return g.reshape(N_PLANE, g.shape[0] // N_PLANE, g.shape[1])
    sz = g.shape[1] // N_PLANE
    return jnp.stack([g[:, j * sz:(j + 1) * sz] for j in range(N_PLANE)])


def _local_step(x, c, ctx, tgt, W, late_shards):
    lc, l = ctx.shape[0], x.shape[0]
    t = lc + l
    tl = 256
    assert lc == tl and l % tl == 0 and lc % CHUNK == 0
    nt, nctx_t, nctx = t // tl, lc // tl, lc // CHUNK
    act = _MXU_DTYPE
    r1 = lambda v: v.reshape(1, -1)

    xt = jnp.concatenate([ctx, x], axis=0)
    lin = 4 * lax.axis_index("x") + 2 * lax.axis_index("y") + lax.axis_index("c")
    c_all = _gather_all("gather_c", jnp.broadcast_to(c, (8, D_MODEL)))[:, 0]
    cc16 = jnp.concatenate([c_all, r1(W['c_ctx']), jnp.zeros((7, D_MODEL), F32)], axis=0)
    (parts,) = _plane_exchange("gather_mods", [_ada_part(cc16, W['ada_w'])], gather=True)
    mods_all = jnp.transpose(parts, (1, 0, 2)).reshape(16, -1) + r1(W['ada_b'])
    mods = jnp.concatenate([lax.dynamic_slice_in_dim(mods_all, lin, 1, axis=0), mods_all[8:9],
                            jnp.zeros((6, mods_all.shape[1]), F32)], axis=0)

    def mod(kk):
        return Op(mods, (8, D_MODEL), lambda j, i, kk=kk: (0, kk))

    def tokop(arr, w=D_MODEL, off=0, tl_=tl):
        if not isinstance(arr, list) and arr.ndim == 3:
            return Op(arr, (2, tl_, w), lambda j, i: (0, i, off + j), load=_sum_dirs)
        return Op(arr, (tl_, w), lambda j, i: (i, off + j))

    def outop(n, dtype, w=D_MODEL, tl_=tl):
        return Op(((t, n), dtype), (tl_, w), lambda j, i: (i, j))

    def parop(arr, w, off=0):
        return Op(arr, (arr.shape[0], w), lambda j, i: (0, off + j))

    def parout(rows, n, w):
        return Op(((rows, n), F32), (rows, w), lambda j, i: (0, j))

    n1w = r1(W['norm1_w'])
    pre_fn = functools.partial(_prenorm_fn, nctx_t)
    pre_tok, pre_par = [tokop(xt)], [parop(n1w, D_MODEL), mod(1), mod(0)]
    (a,) = _tw_fwd("prenorm_fwd", pre_fn, (1, nt), pre_tok + pre_par, [outop(D_MODEL, act)])
    wp = _pack_w_in(W['w_in'])
    proj = _matmul(a, wp, 'nn', "in_proj")

    gcw, gcb = W['gdn_conv_w'], r1(W['gdn_conv_b'])
    scw, scb = W['ssm_conv_w'], r1(W['ssm_conv_b'])
    conv_parts = {}

    tlf = 768 if t % 768 == 0 else tl

    def conv_part(name, mode, poff, cw, cb, woff, width):
        fn = functools.partial(_conv_fn, lc, mode)
        bw = min(width, 1024)
        par_ = [parop(cw, bw, woff // bw), parop(cb, bw, woff // bw)]
        conv_parts[name] = (fn, cw, cb, woff, width, poff)
        (o,) = _tw_fwd("conv_" + name, fn, (width // bw, t // tlf), [tokop(proj, bw, poff // bw, tlf)] + par_,
                       [outop(width, F32, bw, tlf)])
        return o

    q = conv_part('q', 'q', P_QKV, gcw, gcb, 0, 1024)
    k = conv_part('k', 'k', P_QKV + 1024, gcw, gcb, 1024, 1024)
    v = conv_part('v', 'none', P_QKV + 2048, gcw, gcb, 2048, 1024)
    xs = conv_part('xs', 'none', P_XBC, scw, scb, 0, 2048)
    bm = conv_part('bm', 'none', P_XBC + 2048, scw, scb, 2048, 512)
    cm = conv_part('cm', 'none', P_XBC + 2560, scw, scb, 2560, 512)

    z16, z32 = jnp.zeros((16,), F32), jnp.zeros((32,), F32)
    p0 = jnp.concatenate([W['gdn_a_log'].reshape(-1), jnp.zeros((112,), F32)]).reshape(1, 128)
    p1 = jnp.concatenate([W['gdn_dt_bias'].reshape(-1), z16, W['ssm_dt_bias'].reshape(-1), z32]).reshape(1, 128)
    alog = jnp.concatenate([z32, W['ssm_a_log'].reshape(-1), z32]).reshape(1, 128)
    act_tok, act_par = [tokop(proj, 128, P_SMALL // 128)], [parop(p0, 128), parop(p1, 128)]
    (sp,) = _tw_fwd("small_act", _act_fn, (1, t // tlf), [tokop(proj, 128, P_SMALL // 128, tlf)] + act_par,
                    [outop(128, F32, 128, tlf)])

    gather_late = PlaneExchange([late_shards[n] for n in LATE_WEIGHTS], gather=True)
    o_f, o_b, ss, tri, *late = _scan2_fwd("gdn_scan_fwd", _gdn_step2, [q, k, v, sp], 1024, GDN_HEADS, nctx,
                                          GDN_HEADS, ride=gather_late)
    W = dict(W, **{n: _from_shards(n, g) for n, g in zip(LATE_WEIGHTS, late)})
    ssd_consts = _ssd_consts()
    y2, hs = _scan_fwd("ssd_scan_fwd", _ssd_step, [xs, bm, cm, sp], [alog], ssd_consts, 2048, SSM_PAIRS, nctx)

    tlm = 128
    ntm = t // tlm
    gnw = r1(W['gdn_norm_w'])
    dl = jnp.repeat(W['ssm_d'], 64).reshape(1, 2048)
    snw = r1(W['ssm_norm_w'])
    mg_tok = [tokop([o_f, o_b], 1024, 0, tlm), tokop(proj, 1024, P_ZG // 1024, tlm)]
    mg_par = [parop(gnw, 128)]
    (og,) = _tw_fwd("mix_gdn", _mixg_fn, (1, nt),
                    [tokop([o_f, o_b], 1024, 0), tokop(proj, 1024, P_ZG // 1024)] + mg_par, [outop(1024, act, 1024)])
    ms_tok = [tokop(y2, 2048, 0, tlm), tokop(xs, 2048, 0, tlm), tokop(proj, 2048, P_ZS // 2048, tlm)]
    ms_par = [parop(dl, 2048), parop(snw, 2048)]
    (yg,) = _tw_fwd("mix_ssm", _mixs_fn, (1, nt),
                    [tokop(y2, 2048, 0), tokop(xs, 2048, 0), tokop(proj, 2048, P_ZS // 2048)] + ms_par,
                    [outop(2048, act, 2048)])

    pg = _matmul(og, W['w_br_gdn'], 'nn', "br_gdn", out_dtype=act)
    ps = _matmul(yg, W['w_br_ssm'], 'nn', "br_ssm", out_dtype=act)
    mr_tok = [tokop(proj, 2048, P_GATE // 2048), tokop(pg), tokop(ps)]
    (mrg,) = _tw_fwd("merge", _merge_fn, (1, t // tlf),
                     [tokop(proj, 2048, P_GATE // 2048, tlf), tokop(pg, tl_=tlf), tokop(ps, tl_=tlf)],
                     [outop(1024, act, tl_=tlf)])
    mo = _matmul(mrg, W['w_out'], 'nn', "w_out")

    n2w = r1(W['norm2_w'])
    n2_tok, n2_par = [tokop(xt), tokop(mo)], [mod(2), parop(n2w, D_MODEL), mod(4), mod(3)]
    h1, f = _tw_fwd("norm2_fwd", _norm2_fn, (1, t // tlf), [tokop(xt, tl_=tlf), tokop(mo, tl_=tlf)] + n2_par,
                    [outop(1024, F32, tl_=tlf), outop(1024, act, tl_=tlf)])
    u2 = _matmul(f, W['w_ffn_in'], 'nn', "ffn_in", out_dtype=act, halves='out')
    swb = D_FF // 2
    both = lambda arr, tl_=tl: Op(arr, (2, tl_, swb), lambda j, i: (0, i, j))
    sw_tok = [both(u2)]
    (sw,) = _tw_fwd("swiglu", _swiglu_fn, (D_FF // swb, t // tlf), [both(u2, tlf)], [outop(D_FF, act, swb, tlf)])
    ff = _matmul(sw, W['w_ffn_out'], 'nn', "ffn_out")

    loss8, dh1, dff, dg2, dnf = _tail(h1, ff, mods, r1(W['norm_f_w']), tgt, nctx_t, tl)
    loss = loss8[0, 0]

    G = {}
    G['norm_f_w'] = dnf.reshape(-1)
    G['w_ffn_out'] = _matmul(sw, dff, 'tn', "d_ffn_out", out_dtype=_GRAD_DTYPE)
    dsw = _matmul(dff, W['w_ffn_out'], 'nt', "d_sw", out_dtype=act)
    (du2,) = _tw_bwd("swiglu_bwd", _swiglu_fn, (D_FF // swb, nt), sw_tok, [], [[tokop(dsw, swb)]],
                     [both(((2, t, D_FF), act))], [])
    G['w_ffn_in'] = _matmul(f, du2, 'tn', "d_ffn_in", out_dtype=_GRAD_DTYPE, stacked_out=True, halves='b')
    df = _matmul(du2, W['w_ffn_in'], 'nt', "d_f", halves='a')
    dxt1, dmo, dg1, dn2, dsc2, dsh2 = _tw_bwd(
        "norm2_bwd", _norm2_fn, (1, nt), n2_tok, n2_par, [[tokop(dh1)], [tokop(df)]],
        [outop(1024, F32), outop(1024, act)],
        [parout(8, 1024, 1024), parout(1, 1024, 1024), parout(8, 1024, 1024), parout(8, 1024, 1024)])
    G['norm2_w'] = dn2.reshape(-1)
    G['w_out'] = _matmul(mrg, dmo, 'tn', "d_w_out", out_dtype=_GRAD_DTYPE)
    dmrg = _matmul(dmo, W['w_out'], 'nt', "d_mrg", out_dtype=act)
    def win(off, w, tl_=tl):
        return Op(((t, P_TOTAL), act), (tl_, w), lambda j, i: (i, off // w + j))

    dproj = lax.empty((t, P_TOTAL), act)
    dproj, dpg, dps = _tw_bwd("merge_bwd", _merge_fn, (1, nt), mr_tok, [], [[tokop(dmrg)]],
                              [win(P_GATE, 2048), outop(1024, act), outop(1024, act)], [], into={0: dproj})
    G['w_br_gdn'] = _matmul(og, dpg, 'tn', "d_br_gdn", out_dtype=_GRAD_DTYPE)
    G['w_br_ssm'] = _matmul(yg, dps, 'tn', "d_br_ssm", out_dtype=_GRAD_DTYPE)
    dog = _matmul(dpg, W['w_br_gdn'], 'nt', "d_og")
    dyg = _matmul(dps, W['w_br_ssm'], 'nt', "d_yg")

    do, dproj, dgnw = _tw_bwd("mix_gdn_bwd", _mixg_fn, (1, ntm), mg_tok, mg_par, [[tokop(dog, 1024, 0, tlm)]],
                              [outop(1024, F32, 1024, tlm), win(P_ZG, 1024, tlm)], [parout(1, 128, 128)],
                              into={1: dproj})
    G['gdn_norm_w'] = dgnw.reshape(-1)
    dy, dxs_a, dproj, ddl, dsnw = _tw_bwd(
        "mix_ssm_bwd", _mixs_fn, (1, ntm), ms_tok, ms_par, [[tokop(dyg, 2048, 0, tlm)]],
        [outop(2048, F32, 2048, tlm), outop(2048, F32, 2048, tlm), win(P_ZS, 2048, tlm)],
        [parout(1, 2048, 2048), parout(1, 2048, 2048)], into={2: dproj})
    G['ssm_d'] = ddl.reshape(SSM_HEADS, 64).sum(axis=1)
    G['ssm_norm_w'] = dsnw.reshape(-1)

    dq2, dk2, dv2, dsp_g = _scan_bwd("gdn_scan_bwd", _gdn_step, [q, k, v, sp], [], ss, do, GDN_HEADS, nctx, aux=tri)
    received = {}
    scatter_late = PlaneExchange([_to_dest_blocks(n, G.pop(n)) for n in LATE_WEIGHTS], gather=False)
    dxs2, dbm2, dcm2, dsp_s, dalog, *got = _scan_bwd("ssd_scan_bwd", _ssd_step, [xs, bm, cm, sp], [alog], hs, dy,
                                                     SSM_PAIRS, nctx, ride=scatter_late, consts=ssd_consts)
    received.update(zip(LATE_WEIGHTS, got))
    G['ssm_a_log'] = dalog[0, 32:96].reshape(2, SSM_HEADS)

    dconv_w, dconv_b = {}, {}

    def conv_bwd(name, cots_, buf):
        fn, cw, cb, woff, width, poff = conv_parts[name]
        bw = 512
        buf, dconv_w[name], dconv_b[name] = _tw_bwd(
            "conv_" + name + "_bwd", fn, (width // bw, t // tlf), [tokop(proj, bw, poff // bw, tlf)],
            [parop(cw, bw, woff // bw), parop(cb, bw, woff // bw)], [[tokop(c_, bw, 0, tlf) for c_ in cots_]],
            [win(poff, bw, tlf)], [parout(3, width, bw), parout(1, width, bw)], into={0: buf})
        return buf

    dproj = conv_bwd('q', [dq2], dproj)
    dproj = conv_bwd('k', [dk2], dproj)
    dproj = conv_bwd('v', [dv2], dproj)
    dproj = conv_bwd('xs', [dxs2, dxs_a], dproj)
    dproj = conv_bwd('bm', [dbm2], dproj)
    dproj = conv_bwd('cm', [dcm2], dproj)
    G['gdn_conv_w'] = jnp.concatenate([dconv_w['q'], dconv_w['k'], dconv_w['v']], axis=1)
    G['gdn_conv_b'] = [dconv_b['q'], dconv_b['k'], dconv_b['v']]
    G['ssm_conv_w'] = jnp.concatenate([dconv_w['xs'], dconv_w['bm'], dconv_w['cm']], axis=1)
    G['ssm_conv_b'] = [dconv_b['xs'], dconv_b['bm'], dconv_b['cm']]

    dproj, dp0, dp1 = _tw_bwd("small_act_bwd", _act_fn, (1, nt), act_tok, act_par,
                              [[tokop(dsp_g, 128), tokop(dsp_s, 128)]], [win(P_SMALL, P_TOTAL - P_SMALL)],
                              [parout(1, 128, 128), parout(1, 128, 128)], into={0: dproj})
    G['gdn_a_log'] = dp0[0, 0:16].reshape(2, GDN_HEADS)
    G['gdn_dt_bias'] = dp1[0, 0:16].reshape(2, GDN_HEADS)
    G['ssm_dt_bias'] = dp1[0, 32:96].reshape(2, SSM_HEADS)

    g_w_in = _unpack_w_in(_matmul(a, dproj, 'tn', "d_w_in", out_dtype=_GRAD_DTYPE))
    da, (received['w_in'],) = _matmul(dproj, wp, 'nt', "d_a", ride=PlaneExchange([g_w_in], gather=False))

    gx_out = Op(((l, D_MODEL), F32), (tl, D_MODEL), lambda j, i: (jnp.maximum(i - nctx_t, 0), 0))
    grad_x, dn1, dsc1, dsh1 = _tw_bwd(
        "prenorm_bwd", pre_fn, (1, nt), pre_tok, pre_par, [[tokop(da)]], [gx_out],
        [parout(1, 1024, 1024), parout(8, 1024, 1024), parout(8, 1024, 1024)], tok_add=tokop(dxt1),
        sem=("arbitrary", "arbitrary"))
    G['norm1_w'] = dn1.reshape(-1)
    dmods = jnp.concatenate([dsh1, dsc1, dg1, dsh2, dsc2, dg2], axis=1)
    dm_all = _gather_all("gather_dmods", dmods)
    ns = W['ada_w'].shape[1]
    mine = lax.dynamic_slice_in_dim(dm_all, (2 * lax.axis_index("x") + lax.axis_index("y")) * ns, ns, axis=2)
    G['ada_w'], dcc = _ada_bwd_shard(cc16, W['ada_w'], mine[:, 0], mine[:, 1])
    G['ada_b'] = jnp.where(lin == 0, _rowsum2(dm_all[:, 0], dm_all[:, 1]).reshape(-1), 0.0)
    G['c_ctx'] = dcc[0]
    return loss, grad_x, G, received


def _row_tile(r, c):
    for tr in (512, 256, 128, 64, 32, 16, 8):
        if r % tr == 0 and tr * c * 4 <= (1 << 20):
            return tr
    return r


def _sum4(name, rv):
    _, r, c = rv.shape
    tr = _row_tile(r, c)

    def body(r_ref, o_ref):
        o_ref[...] = ((r_ref[0].astype(F32) + r_ref[1].astype(F32)) + r_ref[2].astype(F32)) + r_ref[3].astype(F32)

    return pl.pallas_call(
        body, grid=(r // tr,), in_specs=[pl.BlockSpec((N_PLANE, tr, c), lambda i: (0, i, 0))],
        out_specs=pl.BlockSpec((tr, c), lambda i: (i, 0)), out_shape=jax.ShapeDtypeStruct((r, c), F32),
        compiler_params=_cp(("parallel",)), name=name)(rv)


def _adamw(name, w, m, v, p, q=None):
    r, c = w.shape[-2:]
    tr = _row_tile(r, c)
    grads = [p] if q is None else [p, q]
    lead = (0,) * (w.ndim - 2)

    def body(w_ref, m_ref, v_ref, *refs):
        g_ref, d_ref, mo_ref, vo_ref = refs[len(grads):]
        g = refs[0][...] if q is None else refs[0][...] + refs[1][...]
        at = lead if lead else Ellipsis
        mn = ADAM_B1 * m_ref[at] + (1.0 - ADAM_B1) * g
        vn = ADAM_B2 * v_ref[at] + (1.0 - ADAM_B2) * jnp.square(g)
        m_hat = mn / (1.0 - ADAM_B1 ** ADAM_STEP)
        v_hat = vn / (1.0 - ADAM_B2 ** ADAM_STEP)
        g_ref[at] = g
        d_ref[at] = -ADAM_LR * (m_hat / (jnp.sqrt(v_hat) + ADAM_EPS) + ADAM_WD * w_ref[at])
        mo_ref[at] = mn
        vo_ref[at] = vn

    gspec = pl.BlockSpec((tr, c), lambda i: (i, 0))
    wspec = pl.BlockSpec((1,) * len(lead) + (tr, c), lambda i: lead + (i, 0))
    return pl.pallas_call(
        body, grid=(r // tr,), in_specs=[wspec] * 3 + [gspec] * len(grads), out_specs=[wspec] * 4,
        out_shape=[jax.ShapeDtypeStruct(w.shape, F32)] * 4, compiler_params=_cp(("parallel",)), name=name)(w, m, v, *grads)


SMALL_ROWS = 24


def _pack_small(d):
    pieces = [p.reshape(-1) for n in SMALL for p in (d[n] if isinstance(d[n], list) else [d[n]])]
    v = jnp.concatenate(pieces)
    return jnp.pad(v, (0, SMALL_ROWS * 1024 - v.shape[0])).reshape(SMALL_ROWS, 1024)


def _unpack_small(buf, like):
    v = buf.reshape(-1)
    out, off = {}, 0
    for n in SMALL:
        sz = like[n].size
        out[n] = v[off:off + sz].reshape(like[n].shape)
        off += sz
    return out


def kernel(x, c, ctx, c_ctx, ada_w, ada_b, norm1_w, w_in, gdn_conv_w, gdn_conv_b, gdn_a_log, gdn_dt_bias, gdn_norm_w, ssm_conv_w, ssm_conv_b, ssm_a_log, ssm_dt_bias, ssm_d, ssm_norm_w, w_br_gdn, w_br_ssm, w_out, norm2_w, w_ffn_in, w_ffn_out, norm_f_w, loss_target, m_c_ctx, m_ada_w, m_ada_b, m_norm1_w, m_w_in, m_gdn_conv_w, m_gdn_conv_b, m_gdn_a_log, m_gdn_dt_bias, m_gdn_norm_w, m_ssm_conv_w, m_ssm_conv_b, m_ssm_a_log, m_ssm_dt_bias, m_ssm_d, m_ssm_norm_w, m_w_br_gdn, m_w_br_ssm, m_w_out, m_norm2_w, m_w_ffn_in, m_w_ffn_out, m_norm_f_w, v_c_ctx, v_ada_w, v_ada_b, v_norm1_w, v_w_in, v_gdn_conv_w, v_gdn_conv_b, v_gdn_a_log, v_gdn_dt_bias, v_gdn_norm_w, v_ssm_conv_w, v_ssm_conv_b, v_ssm_a_log, v_ssm_dt_bias, v_ssm_d, v_ssm_norm_w, v_w_br_gdn, v_w_br_ssm, v_w_out, v_norm2_w, v_w_ffn_in, v_w_ffn_out, v_norm_f_w):
    args = dict(locals())
    wl = {n: args[n] for n in WEIGHTS}
    ml = {n: args['m_' + n] for n in WEIGHTS}
    vl = {n: args['v_' + n] for n in WEIGHTS}

    def nodepth(n, a):
        return a if n in ('c_ctx', 'norm_f_w') else a[0]

    shard = {n: nodepth(n, wl[n]).astype(_MXU_DTYPE if n in MXU_WEIGHTS else F32) for n in SHARDED}
    first = [n for n in SHARDED if n not in LATE_WEIGHTS and n != 'ada_w']
    W = {n: nodepth(n, wl[n]) for n in SMALL}
    W['ada_w'] = shard['ada_w']
    for n, g in zip(first, _gather_two_level("all_gather_plane", [shard[n] for n in first])):
        W[n] = _from_shards(n, g)

    loss_local, grad_x, G, received = _local_step(x[0], c, ctx[0], loss_target[0], W,
                                                  late_shards={n: shard[n] for n in LATE_WEIGHTS})
    loss = lax.psum(loss_local, ("x", "y", "c"))

    small_g = _pack_small(G)
    last = [n for n in SHARDED if n not in received and n != 'ada_w']
    send = [_to_dest_blocks(n, G[n]) for n in last] + [jnp.broadcast_to(small_g[None], (N_PLANE,) + small_g.shape)]
    received.update(zip(last + ['small'], _plane_exchange("scatter_plane", send, gather=False)))
    names = [n for n in SHARDED if n != 'ada_w'] + ['small']
    plane_sum = [_sum4("sum4_" + n, received[n]) for n in names]
    other = _swap_sibling(plane_sum)

    wd = {n: nodepth(n, wl[n]) for n in WEIGHTS}
    md = {n: nodepth(n, ml[n]) for n in WEIGHTS}
    vd = {n: nodepth(n, vl[n]) for n in WEIGHTS}
    res = {'ada_w': _adamw("adamw_ada_w", wl['ada_w'], ml['ada_w'], vl['ada_w'], G['ada_w'])}
    for n, p, q in zip(names, plane_sum, other):
        if n == 'small':
            outs = _adamw("adamw_small", _pack_small(wd), _pack_small(md), _pack_small(vd), p, q)
            unpacked = [_unpack_small(o, wd) for o in outs]
            for sn in SMALL:
                res[sn] = [u[sn].reshape(wl[sn].shape) for u in unpacked]
        else:
            res[n] = _adamw("adamw_" + n, wl[n], ml[n], vl[n], p, q)
    flat = [res[n][kind] for kind in range(4) for n in WEIGHTS]
    return (loss, grad_x[None], *flat)
```

```python
import functools

import jax
import jax.numpy as jnp
from jax import lax
from jax.experimental import pallas as pl
from jax.experimental.pallas import tpu as pltpu

F32 = jnp.float32
HI = lax.Precision.HIGHEST
_MXU_DTYPE = jnp.bfloat16
_SCAN_DTYPE = jnp.bfloat16
_GRAD_DTYPE = jnp.bfloat16

D_MODEL = 1024
EPS = 1e-6
CHUNK = 64
GRID_W = 64
GDN_HEADS = 8
GDN_DK = 128
SSM_HEADS = 32
SSM_PAIRS = 16
SSD_GROUP = 1
D_FF = 2816
D_IN_PROJ = 11360
N_PLANE = 4
P_QKV, P_ZG, P_GATE, P_ZS, P_XBC, P_SMALL, P_TOTAL = 0, 3072, 4096, 6144, 8192, 11264, 11776
IN_SEGMENTS = [(0, 3072, P_QKV), (3072, 4096, P_ZG), (4096, 4112, P_SMALL), (4112, 4128, P_SMALL + 16),
               (4128, 6176, P_ZS), (6176, 9248, P_XBC), (9248, 9312, P_SMALL + 32), (9312, 11360, P_GATE)]
ADAM_LR, ADAM_B1, ADAM_B2, ADAM_EPS, ADAM_WD, ADAM_STEP = 0.001, 0.9, 0.999, 1e-08, 0.01, 10

VMEM_LIMIT = 48 * 1024 * 1024

WEIGHTS = ['c_ctx', 'ada_w', 'ada_b', 'norm1_w', 'w_in', 'gdn_conv_w', 'gdn_conv_b', 'gdn_a_log', 'gdn_dt_bias',
           'gdn_norm_w', 'ssm_conv_w', 'ssm_conv_b', 'ssm_a_log', 'ssm_dt_bias', 'ssm_d', 'ssm_norm_w', 'w_br_gdn',
           'w_br_ssm', 'w_out', 'norm2_w', 'w_ffn_in', 'w_ffn_out', 'norm_f_w']
SHARD_AXIS = {'ada_w': 1, 'w_in': 1, 'gdn_conv_w': 1, 'ssm_conv_w': 1, 'w_br_gdn': 0, 'w_br_ssm': 0, 'w_out': 0,
              'w_ffn_in': 1, 'w_ffn_out': 0}
SHARDED = [n for n in WEIGHTS if n in SHARD_AXIS]
SMALL = [n for n in WEIGHTS if n not in SHARD_AXIS]
MXU_WEIGHTS = ['ada_w', 'w_in', 'w_br_gdn', 'w_br_ssm', 'w_out', 'w_ffn_in', 'w_ffn_out']


def _cp(sem):
    return pltpu.CompilerParams(dimension_semantics=sem, vmem_limit_bytes=VMEM_LIMIT)


def _pick(n, cands):
    for c in cands:
        if n % c == 0:
            return c
    raise ValueError(f"no tile for {n}")


HBM = pl.BlockSpec(memory_space=pl.ANY)
MESH = pl.DeviceIdType.MESH


def _plane_peers():
    x, y, c = lax.axis_index("x"), lax.axis_index("y"), lax.axis_index("c")
    return (x, y, c), [(1 - x, y, c), (x, 1 - y, c), (1 - x, 1 - y, c)]


class PlaneExchange:
    def __init__(self, arrs, gather):
        self.arrs, self.gather, self.n = list(arrs), gather, len(arrs)
        self.in_specs = [HBM] * self.n
        self.out_specs = [HBM] * self.n
        self.out_shape = [jax.ShapeDtypeStruct((N_PLANE,) + a.shape if gather else a.shape, a.dtype) for a in self.arrs]
        self.scratch = [pltpu.SemaphoreType.DMA((3 * self.n,)), pltpu.SemaphoreType.DMA((3 * self.n,)),
                        pltpu.SemaphoreType.DMA((self.n,))]

    def _copies(self, ins, outs, sems):
        send_sems, recv_sems, local_sems = sems
        (x, y, c), peers = _plane_peers()
        me = 2 * x + y
        copies = []
        for ti in range(self.n):
            src = ins[ti] if self.gather else ins[ti].at[me]
            copies.append(pltpu.make_async_copy(src, outs[ti].at[me], local_sems.at[ti]))
            for kk, (px, py, pc) in enumerate(peers):
                src = ins[ti] if self.gather else ins[ti].at[2 * px + py]
                copies.append(pltpu.make_async_remote_copy(
                    src_ref=src, dst_ref=outs[ti].at[me], send_sem=send_sems.at[3 * ti + kk],
                    recv_sem=recv_sems.at[3 * ti + kk], device_id=(px, py, pc), device_id_type=MESH))
        return copies

    def start(self, ins, outs, sems):
        for i, cp in enumerate(self._copies(ins, outs, sems)):
            cp.start(priority=int(i % N_PLANE == 0))

    def wait(self, ins, outs, sems):
        for cp in self._copies(ins, outs, sems):
            cp.wait()


def _plane_exchange(name, arrs, gather):
    ex = PlaneExchange(arrs, gather)
    n = ex.n

    def body(*refs):
        ins, outs, sems = refs[:n], refs[n:2 * n], refs[2 * n:]
        ex.start(ins, outs, sems)
        ex.wait(ins, outs, sems)

    return pl.pallas_call(body, in_specs=ex.in_specs, out_specs=ex.out_specs, out_shape=ex.out_shape,
                          scratch_shapes=ex.scratch, name=name)(*arrs)


def _gather_two_level(name, arrs):
    n = len(arrs)
    split = [a.shape[0] % 32 == 0 for a in arrs]

    def body(*refs):
        ins, outs = refs[:n], refs[n:2 * n]
        ici_send, ici_recv, d2d_send, d2d_recv, local_sems = refs[2 * n:]
        (x, y, c), peers = _plane_peers()
        me = 2 * x + y

        def part(ref, ti):
            if not split[ti]:
                return ref
            rows = arrs[ti].shape[0] // 2
            return ref.at[pl.ds(c * rows, rows)]

        local = [pltpu.make_async_copy(ins[ti], outs[ti].at[me], local_sems.at[ti]) for ti in range(n)]
        ici = [pltpu.make_async_remote_copy(
            src_ref=part(ins[ti], ti), dst_ref=part(outs[ti].at[me], ti), send_sem=ici_send.at[3 * ti + kk],
            recv_sem=ici_recv.at[3 * ti + kk], device_id=peer, device_id_type=MESH)
            for ti in range(n) for kk, peer in enumerate(peers)]
        for cp in local + ici:
            cp.start()
        d2d = []
        for ti in range(n):
            for kk, (px, py, pc) in enumerate(peers):
                ici[3 * ti + kk].wait_recv()
                if split[ti]:
                    piece = part(outs[ti].at[2 * px + py], ti)
                    cp = pltpu.make_async_remote_copy(
                        src_ref=piece, dst_ref=piece, send_sem=d2d_send.at[3 * ti + kk],
                        recv_sem=d2d_recv.at[3 * ti + kk], device_id=(x, y, 1 - c), device_id_type=MESH)
                    cp.start()
                    d2d.append(cp)
        for cp in ici:
            cp.wait_send()
        for cp in d2d:
            cp.wait()
        for cp in local:
            cp.wait()

    return pl.pallas_call(
        body, in_specs=[HBM] * n, out_specs=[HBM] * n,
        out_shape=[jax.ShapeDtypeStruct((N_PLANE,) + a.shape, a.dtype) for a in arrs],
        scratch_shapes=[pltpu.SemaphoreType.DMA((3 * n,))] * 4 + [pltpu.SemaphoreType.DMA((n,))], name=name)(*arrs)


def _gather_all(name, arr):
    flips = [(fx, fy, fc) for fx in (0, 1) for fy in (0, 1) for fc in (0, 1)][1:]

    def body(a_ref, o_ref, send_sems, recv_sems, local_sem):
        x, y, c = lax.axis_index("x"), lax.axis_index("y"), lax.axis_index("c")
        me = 4 * x + 2 * y + c
        copies = [pltpu.make_async_copy(a_ref, o_ref.at[me], local_sem)]
        for kk, (fx, fy, fc) in enumerate(flips):
            peer = (1 - x if fx else x, 1 - y if fy else y, 1 - c if fc else c)
            copies.append(pltpu.make_async_remote_copy(
                src_ref=a_ref, dst_ref=o_ref.at[me], send_sem=send_sems.at[kk], recv_sem=recv_sems.at[kk],
                device_id=peer, device_id_type=MESH))
        for cp in copies:
            cp.start()
        for cp in copies:
            cp.wait()

    return pl.pallas_call(
        body, in_specs=[HBM], out_specs=HBM, out_shape=jax.ShapeDtypeStruct((8,) + arr.shape, arr.dtype),
        scratch_shapes=[pltpu.SemaphoreType.DMA((7,)), pltpu.SemaphoreType.DMA((7,)), pltpu.SemaphoreType.DMA],
        name=name)(arr)


def _swap_sibling(arrs):
    n = len(arrs)

    def body(*refs):
        ins, outs, send_sems, recv_sems = refs[:n], refs[n:2 * n], refs[2 * n], refs[2 * n + 1]
        x, y, c = lax.axis_index("x"), lax.axis_index("y"), lax.axis_index("c")
        copies = [pltpu.make_async_remote_copy(src_ref=ins[ti], dst_ref=outs[ti], send_sem=send_sems.at[ti],
                                               recv_sem=recv_sems.at[ti], device_id=(x, y, 1 - c), device_id_type=MESH)
                  for ti in range(n)]
        for cp in copies:
            cp.start()
        for cp in copies:
            cp.wait()

    return pl.pallas_call(
        body, in_specs=[HBM] * n, out_specs=[HBM] * n, out_shape=[jax.ShapeDtypeStruct(a.shape, a.dtype) for a in arrs],
        scratch_shapes=[pltpu.SemaphoreType.DMA((n,)), pltpu.SemaphoreType.DMA((n,))], name="swap_sibling")(*arrs)


MATMUL_VMEM_BUDGET = 36 * 1024 * 1024
TILE_CANDIDATES = (2944, 2816, 1408, 1024, 768, 512, 256, 128)
STEP_COST_BYTES = 1 << 20
ACC_PASS_WEIGHT = 0.35


def _matmul_tiles(m, n, k, ab, bb, ob, tn_fixed=None, tk_fixed=None):
    def cands(dim, whole_up_to):
        out = [c for c in TILE_CANDIDATES if dim % c == 0]
        if dim <= whole_up_to and dim not in out:
            out.append(dim)
        return out

    best = None
    for tm in cands(m, 3072):
        for tn in ([tn_fixed] if tn_fixed else cands(n, 3072)):
            for tk in ([tk_fixed] if tk_fixed else cands(k, 2048)):
                gi, gj, gk = m // tm, n // tn, k // tk
                vmem = 2 * (tm * tk * ab + tk * tn * bb + tm * tn * ob) + (tm * tn * 4 if gk > 1 else 0)
                if vmem > MATMUL_VMEM_BUDGET:
                    continue
                a_reads = 1 if gk == 1 else gj
                b_reads = 1 if (gk == 1 and gj == 1) else gi
                acc = m * n * 8 * gk * ACC_PASS_WEIGHT if gk > 1 else 0
                cost = (m * k * ab * a_reads + k * n * bb * b_reads + m * n * ob + gi * gj * gk * STEP_COST_BYTES + acc)
                if best is None or cost < best[0]:
                    best = (cost, tm, tn, tk)
    assert best is not None, (m, n, k)
    return best[1:]


def _matmul(a, b, form, name, out_dtype=F32, stacked_out=False, ride=None, halves=None):
    dims = {'nn': (((1,), (0,)), ((), ())), 'nt': (((1,), (1,)), ((), ())), 'tn': (((0,), (0,)), ((), ()))}[form]
    stacked_b = b.ndim == 3 and halves != 'b'
    ns = b.shape[2] if stacked_b else None
    if form == 'nn':
        m, k = a.shape
        n = b.shape[0] * ns if stacked_b else b.shape[1]
    elif form == 'nt':
        m, k = (a.shape[1], 2 * a.shape[2]) if halves == 'a' else a.shape
        n = b.shape[1] if stacked_b else b.shape[0]
    else:
        k, m = a.shape
        n = 2 * b.shape[2] if halves == 'b' else b.shape[1]
    ob = jnp.dtype(out_dtype).itemsize
    tm, tn, tk = _matmul_tiles(m, n, k, a.dtype.itemsize, b.dtype.itemsize, ob,
                               tn_fixed=(n // N_PLANE if stacked_out else ns if (stacked_b and form == 'nn') else None),
                               tk_fixed=(ns if (stacked_b and form == 'nt') else None))
    nk = k // tk
    grid = (m // tm, n // tn, nk)
    n_ride = ride.n if ride is not None else 0

    def body(*refs):
        a_ref, b_ref = refs[0], refs[1]
        ride_in = refs[2:2 + n_ride]
        o_ref = refs[2 + n_ride]
        ride_out = refs[3 + n_ride:3 + 2 * n_ride]
        acc_ref = refs[3 + 2 * n_ride]
        sems = refs[4 + 2 * n_ride:]
        i, j, kk = pl.program_id(0), pl.program_id(1), pl.program_id(2)
        if ride is not None:
            @pl.when((i == 0) & (j == 0) & (kk == 0))
            def _():
                ride.start(ride_in, ride_out, sems)

        def put(val):
            if stacked_out or halves == 'out':
                o_ref[0] = val.astype(o_ref.dtype)
            else:
                o_ref[...] = val.astype(o_ref.dtype)

        av = a_ref[0] if halves == 'a' else a_ref[...]
        bv = b_ref[0] if (stacked_b or halves == 'b') else b_ref[...]
        part = lax.dot_general(av.astype(_MXU_DTYPE), bv.astype(_MXU_DTYPE), dims, preferred_element_type=F32)
        if nk == 1:
            put(part)
        else:
            @pl.when(kk == 0)
            def _():
                acc_ref[...] = part

            @pl.when(kk > 0)
            def _():
                acc_ref[...] += part

            @pl.when(kk == nk - 1)
            def _():
                put(acc_ref[...])

        if ride is not None:
            @pl.when((i == grid[0] - 1) & (j == grid[1] - 1) & (kk == nk - 1))
            def _():
                ride.wait(ride_in, ride_out, sems)

    if form == 'nn':
        a_spec = pl.BlockSpec((tm, tk), lambda i, j, kk: (i, kk))
        b_spec = (pl.BlockSpec((1, tk, tn), lambda i, j, kk: (j, kk, 0)) if stacked_b
                  else pl.BlockSpec((tk, tn), lambda i, j, kk: (kk, j)))
    elif form == 'nt':
        a_spec = pl.BlockSpec((tm, tk), lambda i, j, kk: (i, kk))
        b_spec = (pl.BlockSpec((1, tn, tk), lambda i, j, kk: (kk, j, 0)) if stacked_b
                  else pl.BlockSpec((tn, tk), lambda i, j, kk: (j, kk)))
    else:
        a_spec = pl.BlockSpec((tk, tm), lambda i, j, kk: (kk, i))
        b_spec = pl.BlockSpec((tk, tn), lambda i, j, kk: (kk, j))
    hj, hk = grid[1] // 2, nk // 2
    if halves == 'a':
        a_spec = pl.BlockSpec((1, tm, tk), lambda i, j, kk: (kk // hk, i, kk % hk))
    if halves == 'b':
        b_spec = pl.BlockSpec((1, tk, tn), lambda i, j, kk: (j // hj, kk, j % hj))
    if stacked_out:
        o_spec = pl.BlockSpec((1, tm, tn), lambda i, j, kk: (j, i, 0))
        o_shape = jax.ShapeDtypeStruct((N_PLANE, m, tn), out_dtype)
    elif halves == 'out':
        o_spec = pl.BlockSpec((1, tm, tn), lambda i, j, kk: (j // hj, i, j % hj))
        o_shape = jax.ShapeDtypeStruct((2, m, n // 2), out_dtype)
    else:
        o_spec = pl.BlockSpec((tm, tn), lambda i, j, kk: (i, j))
        o_shape = jax.ShapeDtypeStruct((m, n), out_dtype)
    acc_shape = (tm, tn) if nk > 1 else (8, 128)
    if ride is None:
        return pl.pallas_call(
            body, grid=grid, in_specs=[a_spec, b_spec], out_specs=o_spec, out_shape=o_shape,
            scratch_shapes=[pltpu.VMEM(acc_shape, F32)],
            compiler_params=_cp(("parallel", "parallel", "arbitrary")), name=name)(a, b)
    res = pl.pallas_call(
        body, grid=grid, in_specs=[a_spec, b_spec] + ride.in_specs, out_specs=[o_spec] + ride.out_specs,
        out_shape=[o_shape] + ride.out_shape, scratch_shapes=[pltpu.VMEM(acc_shape, F32)] + ride.scratch,
        compiler_params=_cp(("arbitrary", "arbitrary", "arbitrary")), name=name)(a, b, *ride.arrs)
    return res[0], res[1:]


class Op:
    def __init__(self, arr, bs, im, load=None):
        self.arr, self.bs, self.im = arr, bs, im
        self.arrs = list(arr) if isinstance(arr, list) else [arr]
        self.load = load or (lambda r: r[...].astype(F32))

    def spec(self):
        return pl.BlockSpec(self.bs, self.im)

    def value(self, it):
        return functools.reduce(lambda u, w: u + w, [self.load(next(it)) for _ in self.arrs])


def _op_specs(ops):
    return [op.spec() for op in ops for _ in op.arrs]


def _op_arrays(ops):
    return [a for op in ops for a in op.arrs]


def _sum_dirs(r):
    return r[0].astype(F32) + r[1].astype(F32)


def _tw_fwd(name, fn, grid, ins, outs):
    def body(*refs):
        info = (pl.program_id(0), pl.program_id(1))
        it = iter(refs)
        vals = [op.value(it) for op in ins]
        res = fn(info, *vals)
        for r, v in zip(it, res):
            r[...] = v.astype(r.dtype)

    return pl.pallas_call(
        body, grid=grid, in_specs=_op_specs(ins), out_specs=[op.spec() for op in outs],
        out_shape=[jax.ShapeDtypeStruct(*op.arr) for op in outs],
        compiler_params=_cp(("parallel", "arbitrary")), name=name)(*_op_arrays(ins))


def _tw_bwd(name, fn, grid, tok, par, cots, tok_out, par_out, tok_add=None, sem=("parallel", "arbitrary"), into=None):
    n_tok = len(tok)
    flat_cots = [op for group in cots for op in group]
    extra = [tok_add] if tok_add is not None else []
    out_ops = list(tok_out) + list(par_out)
    into = into or {}

    def body(*refs):
        info = (pl.program_id(0), pl.program_id(1))
        it = iter(refs)
        tok_v = [op.value(it) for op in tok]
        par_v = [op.value(it) for op in par]
        cot_v = [functools.reduce(lambda u, w: u + w, [op.value(it) for op in group]) for group in cots]
        add_v = [op.value(it) for op in extra]
        for _ in into:
            next(it)
        _, pull = jax.vjp(lambda *a: fn(info, *a), *tok_v, *par_v)
        grads = pull(tuple(cot_v))
        for i in range(n_tok):
            r = next(it)
            g = grads[i] + add_v[0] if (i == 0 and add_v) else grads[i]
            if r.shape[-1] > g.shape[-1]:
                g = jnp.concatenate([g, jnp.zeros((g.shape[0], r.shape[-1] - g.shape[-1]), g.dtype)], axis=1)
            r[...] = g.astype(r.dtype)
        first = pl.program_id(1) == 0
        for i in range(len(par)):
            r = next(it)
            g = grads[n_tok + i]

            @pl.when(first)
            def _(r=r, g=g):
                r[...] = g

            @pl.when(jnp.logical_not(first))
            def _(r=r, g=g):
                r[...] += g

    ops = tok + par + flat_cots + extra
    n_in = len(_op_arrays(ops))
    return pl.pallas_call(
        body, grid=grid, in_specs=_op_specs(ops) + [HBM] * len(into), out_specs=[op.spec() for op in out_ops],
        out_shape=[jax.ShapeDtypeStruct(*op.arr) for op in out_ops],
        input_output_aliases={n_in + k: i for k, i in enumerate(into)},
        compiler_params=_cp(sem), name=name)(*_op_arrays(ops), *into.values())


def _silu(x):
    return x * jax.nn.sigmoid(x)


def _rms(x):
    return x * lax.rsqrt(jnp.mean(x * x, axis=-1, keepdims=True) + EPS)


@functools.partial(jax.custom_vjp, nondiff_argnums=(1,))
def _shift_rows(x, k):
    return pltpu.roll(x, k % x.shape[0], 0)


def _shift_rows_fwd(x, k):
    return _shift_rows(x, k), None


def _shift_rows_bwd(k, _, g):
    return (_shift_rows(g, -k),)


_shift_rows.defvjp(_shift_rows_fwd, _shift_rows_bwd)


def _prenorm_fn(nctx_t, info, x, w, sc8, sh8):
    is_ctx = info[1] < nctx_t
    sc = jnp.where(is_ctx, sc8[1:2], sc8[0:1])
    sh = jnp.where(is_ctx, sh8[1:2], sh8[0:1])
    return (_rms(x) * w * (1.0 + sc) + sh,)


def _conv_fn(lc, mode, info, x, w, b):
    n, c = x.shape
    g = info[1] * n + lax.broadcasted_iota(jnp.int32, (n, 1), 0)
    is_ctx = g < lc
    rr = jnp.where(is_ctx, g, g % GRID_W)
    first = rr == 0
    last = rr == jnp.where(is_ctx, lc - 1, GRID_W - 1)
    prev = jnp.where(first, 0.0, _shift_rows(x, 1))
    nxt = jnp.where(last, 0.0, _shift_rows(x, -1))
    y = b + prev * w[0:1] + x * w[1:2] + nxt * w[2:3]
    y = _silu(y)
    if mode == 'none':
        return (y,)
    scale = GDN_DK ** -0.5 if mode == 'q' else 1.0
    outs = []
    for h in range(c // 128):
        yh = y[:, h * 128:(h + 1) * 128]
        outs.append(yh * lax.rsqrt(jnp.sum(yh * yh, axis=-1, keepdims=True) + EPS) * scale)
    return (jnp.concatenate(outs, axis=1),)


def _act_fn(info, x, p0, p1):
    lane = lax.broadcasted_iota(jnp.int32, x.shape, 1)
    sp = jax.nn.softplus(x + p1)
    g = -jnp.exp(p0) * sp
    bt = jax.nn.sigmoid(x)
    return (jnp.where(lane < 16, g, jnp.where(lane < 32, bt, jnp.where(lane < 96, sp, 0.0))),)


def _mixg_fn(info, o, zg, gnw):
    outs = []
    for h in range(GDN_HEADS):
        outs.append(_rms(o[:, h * 128:(h + 1) * 128]) * gnw)
    return (jnp.concatenate(outs, axis=1) * _silu(zg),)


def _mixs_fn(info, y, xs, zs, dl, snw):
    yy = (y + dl * xs) * _silu(zs)
    outs = []
    for g in range(4):
        outs.append(_rms(yy[:, g * 512:(g + 1) * 512]))
    return (jnp.concatenate(outs, axis=1) * snw,)


def _merge_fn(info, gates, pg, ps):
    half = gates.shape[1] // 2
    return (jax.nn.sigmoid(gates[:, :half]) * pg + jax.nn.sigmoid(gates[:, half:]) * ps,)


def _norm2_fn(info, xt, mo, g8, w, sc8, sh8):
    h1 = xt + g8[0:1] * mo
    return (h1, _rms(h1) * w * (1.0 + sc8[0:1]) + sh8[0:1])


def _swiglu_fn(info, u2):
    return (_silu(u2[0]) * u2[1],)


_NN = (((1,), (0,)), ((), ()))
_NT = (((1,), (1,)), ((), ()))
_TN = (((0,), (0,)), ((), ()))


def _mmh(a, b):
    return lax.dot_general(a, b, _NN, precision=HI, preferred_element_type=F32)


def _dot1(a, b, dims):
    return lax.dot_general(a.astype(_SCAN_DTYPE), b.astype(_SCAN_DTYPE), dims, preferred_element_type=F32)


def _mm(a, b):
    return _dot1(a, b, _NN)


def _mm_nt(a, b):
    return _dot1(a, b, _NT)


def _mm_tn(a, b):
    return _dot1(a, b, _TN)


def _split2(a):
    hi = a.astype(_SCAN_DTYPE)
    return hi, (a - hi.astype(F32)).astype(_SCAN_DTYPE)


def _dot3(a, b, dims):
    ah, al = _split2(a)
    bh, bl = _split2(b)
    d = lambda u, w: lax.dot_general(u, w, dims, preferred_element_type=F32)
    return d(ah, bh) + (d(ah, bl) + d(al, bh))


def _order_masks(d):
    i = lax.broadcasted_iota(jnp.int32, (CHUNK, CHUNK), 0)
    j = lax.broadcasted_iota(jnp.int32, (CHUNK, CHUNK), 1)
    s = jnp.where(d == 0, 1, -1) * (i - j)
    return (s >= 0).astype(F32), (s > 0).astype(F32)


@jax.custom_vjp
def _unit_tri_inv(mats):
    i = lax.broadcasted_iota(jnp.int32, (CHUNK, CHUNK), 0)
    j = lax.broadcasted_iota(jnp.int32, (CHUNK, CHUNK), 1)
    eye = (i == j).astype(F32)
    ps = [-a for a in mats]
    ts = [eye + p for p in ps]
    for _ in range(5):
        ps = [_dot3(p, p, _NN) for p in ps]
        ts = [t + _dot3(t, p, _NN) for t, p in zip(ts, ps)]
    return tuple(ts)


def _uti_fwd(mats):
    ts = _unit_tri_inv(mats)
    return ts, ts


def _uti_bwd(ts, gs):
    inner = [_dot3(g, t, _NT) for g, t in zip(gs, ts)]
    return (tuple(-_dot3(t, u, _TN) for t, u in zip(ts, inner)),)


_unit_tri_inv.defvjp(_uti_fwd, _uti_bwd)


@jax.custom_vjp
def _unit_tri_inv_given(mats, ts):
    return ts


def _utig_fwd(mats, ts):
    return ts, ts


def _utig_bwd(ts, gs):
    return _uti_bwd(ts, gs)[0], tuple(jnp.zeros_like(t) for t in ts)


_unit_tri_inv_given.defvjp(_utig_fwd, _utig_bwd)


def _lane_col(blk, lane_idx):
    lane = lax.broadcasted_iota(jnp.int32, blk.shape, 1)
    return jnp.sum(jnp.where(lane == lane_idx, blk, 0.0), axis=1, keepdims=True)


def _decay_mat(cum, incl):
    cb = jnp.broadcast_to(cum, (CHUNK, CHUNK))
    return jnp.exp(jnp.minimum(cb - cb.T, 0.0)) * incl


def _gdn_chunks(dirs, streams, states, aux=None, want_aux=False, group=None):
    ns = len(dirs)
    masks = [_order_masks(d) for d in dirs]
    cum = [_mmh(masks[i][0], streams[i][3]) for i in range(ns)]
    tot = [jnp.sum(streams[i][3], axis=0, keepdims=True) for i in range(ns)]
    all_units = [(i, h) for i in range(ns) for h in range(GDN_HEADS)]
    group = group or len(all_units)
    cat = jnp.concatenate
    outs, new_states, ts_all = [], [], []
    for g0 in range(0, len(all_units), group):
        units = all_units[g0:g0 + group]
        us = range(len(units))
        st = states[g0:g0 + group]
        qs = [streams[i][0][:, h * 128:(h + 1) * 128] for i, h in units]
        ks = [streams[i][1][:, h * 128:(h + 1) * 128] for i, h in units]
        vs = [streams[i][2][:, h * 128:(h + 1) * 128] for i, h in units]
        gcum = [_lane_col(cum[i], dirs[i] * GDN_HEADS + h) for i, h in units]
        glast = [_lane_col(tot[i], dirs[i] * GDN_HEADS + h) for i, h in units]
        beta = [_lane_col(streams[i][3], 16 + dirs[i] * GDN_HEADS + h) for i, h in units]
        decay = [_decay_mat(gcum[u], masks[units[u][0]][0]) for u in us]
        egc = [jnp.exp(gcum[u]) for u in us]
        kb = [ks[u] * beta[u] for u in us]
        kq = [_mm_nt(cat([kb[u], qs[u]], axis=0), ks[u]) for u in us]
        mats = tuple(kq[u][:CHUNK] * decay[u] * masks[units[u][0]][1] for u in us)
        ts = _unit_tri_inv(mats) if aux is None else _unit_tri_inv_given(mats, tuple(aux[g0:g0 + group]))
        wu = [_mm(ts[u], cat([kb[u] * egc[u], vs[u] * beta[u]], axis=1)) for u in us]
        ws = [_mm(cat([wu[u][:, :128], qs[u] * egc[u]], axis=0), st[u]) for u in us]
        vn = [wu[u][:, 128:] - ws[u][:CHUNK] for u in us]
        outs += [ws[u][CHUNK:] + _mm(kq[u][CHUNK:] * decay[u], vn[u]) for u in us]
        new_states += [st[u] * jnp.exp(glast[u]) + _mm_tn(ks[u] * jnp.exp(glast[u] - gcum[u]), vn[u]) for u in us]
        ts_all += list(ts)
    per_stream = [cat(outs[i * GDN_HEADS:(i + 1) * GDN_HEADS], axis=1) for i in range(ns)]
    return (*per_stream, *new_states, *(ts_all if want_aux else ()))


def _gdn_step(d, q, k, v, spb, *states, aux=None, want_aux=False):
    return _gdn_chunks([d], [(q, k, v, spb)], states, aux, want_aux, group=4)


def _gdn_step2(q0, k0, v0, sp0, q1, k1, v1, sp1, *states, aux=None, want_aux=False):
    return _gdn_chunks([0, 1], [(q0, k0, v0, sp0), (q1, k1, v1, sp1)], states, aux, want_aux)


def _split3(a):
    a1 = a.astype(_SCAN_DTYPE)
    r = a - a1.astype(F32)
    a2 = r.astype(_SCAN_DTYPE)
    return a1, a2, (r - a2.astype(F32)).astype(_SCAN_DTYPE)


def _exact_dot(a, e, dims, split_lhs, passes=3):
    parts = _split3(a if split_lhs else e)[:passes]
    d = (lambda u: lax.dot_general(u, e, dims, preferred_element_type=F32)) if split_lhs else \
        (lambda u: lax.dot_general(a, u, dims, preferred_element_type=F32))
    return functools.reduce(lambda u, w: u + w, [d(p) for p in reversed(parts)])


@jax.custom_vjp
def _spread(a, e):
    return _exact_dot(a, e, _NN, True)


def _spread_fwd(a, e):
    return _spread(a, e), e


def _spread_bwd(e, g):
    return _exact_dot(g, e, _NT, True, passes=2), jnp.zeros_like(e)


_spread.defvjp(_spread_fwd, _spread_bwd)


@jax.custom_vjp
def _colsum_bcast(z):
    return _exact_dot(jnp.ones((z.shape[0], z.shape[0]), _SCAN_DTYPE), z, _NN, False)


def _colsum_fwd(z):
    return _colsum_bcast(z), None


def _colsum_bwd(_, g):
    return (_exact_dot(jnp.ones((g.shape[0], g.shape[0]), _SCAN_DTYPE), g, _NN, False, passes=2),)


_colsum_bcast.defvjp(_colsum_fwd, _colsum_bwd)


def _ssd_consts():
    wdt = SSM_HEADS * 64
    d = lax.broadcasted_iota(jnp.int32, (2, 128, wdt), 0)
    e = (lax.broadcasted_iota(jnp.int32, (2, 128, wdt), 1)
         == 32 + d * SSM_HEADS + lax.broadcasted_iota(jnp.int32, (2, 128, wdt), 2) // 64).astype(_SCAN_DTYPE)
    dd = lax.broadcasted_iota(jnp.int32, (2, CHUNK, wdt), 0)
    ci = lax.broadcasted_iota(jnp.int32, (2, CHUNK, wdt), 1)
    pos = lax.broadcasted_iota(jnp.int32, (2, CHUNK, wdt), 2) % 64
    incl_t = (jnp.where(dd == 0, 1, -1) * (ci - pos) >= 0).astype(F32)
    diag = (ci == pos).astype(F32)
    return [e, incl_t, diag]


def _ssd_step(d, x, bm, cm, spb, alog, *states, consts):
    e, incl_t, diag = consts
    incl, _ = _order_masks(d)
    lane1 = lax.broadcasted_iota(jnp.int32, (1, 128), 1)
    lo_lane = 32 + d * SSM_HEADS
    a_vec = jnp.where(lane1 >= lo_lane, jnp.where(lane1 < lo_lane + SSM_HEADS, -jnp.exp(alog), 0.0), 0.0)
    adt = spb * a_vec
    acum = _mmh(incl, adt)
    alast = jnp.sum(adt, axis=0, keepdims=True)
    dt2 = _spread(spb, e)
    ac2 = _spread(acum, e)
    al2 = _spread(jnp.broadcast_to(alast, (8, 128)), e)[0:1]
    row = _colsum_bcast(ac2 * diag)
    seg = jnp.exp(jnp.minimum(ac2 - row, 0.0)) * incl_t
    xdt = x * dt2
    gam = jnp.exp(ac2)
    xe = xdt * jnp.exp(al2 - ac2)
    low = lax.broadcasted_iota(jnp.int32, (CHUNK, 128), 1) < 64
    row_low = lax.broadcasted_iota(jnp.int32, (128, 1), 0) < 64
    ps = range(SSM_PAIRS)
    sl = [slice(p * 128, (p + 1) * 128) for p in ps]
    bg = [bm[:, g * 128:(g + 1) * 128] for g in range(4)]
    cg = [cm[:, g * 128:(g + 1) * 128] for g in range(4)]
    cb2 = [_mm_nt(cg[g], jnp.concatenate([bg[g], bg[g]], axis=0)) for g in range(4)]
    ys, new_states = [], []
    for p0 in range(0, SSM_PAIRS, SSD_GROUP):
        pg = range(p0, p0 + SSD_GROUP)
        xd = {p: jnp.concatenate([jnp.where(low, xdt[:, sl[p]], 0.0), jnp.where(low, 0.0, xdt[:, sl[p]])], axis=0)
              for p in pg}
        yd = {p: _mm(cb2[p // 4] * seg[:, sl[p]], xd[p]) for p in pg}
        yo = {p: _mm_nt(cg[p // 4], states[p]) for p in pg}
        ys += [yd[p] + gam[:, sl[p]] * yo[p] for p in pg]
        new = {p: _mm_tn(xe[:, sl[p]], bg[p // 4]) for p in pg}
        al0 = {p: _lane_col(alast, lo_lane + 2 * p) for p in pg}
        al1 = {p: _lane_col(alast, lo_lane + 2 * p + 1) for p in pg}
        new_states += [states[p] * jnp.exp(jnp.where(row_low, al0[p], al1[p])) + new[p] for p in pg]
    return (jnp.concatenate(ys, axis=1), *new_states)


def _chunk_of(d, p, nctx, nc):
    return jnp.where(d == 0, p, jnp.where(p < nctx, nctx - 1 - p, nctx + nc - 1 - p))


class _NoRide:
    n, arrs, in_specs, out_specs, out_shape, scratch = 0, [], [], [], [], []


def _const_specs(consts):
    return [pl.BlockSpec((1,) + a.shape[1:], lambda d, p: (d,) + (0,) * (a.ndim - 1)) for a in consts]


def _scan_fwd(name, step, toks, pars, consts, out_width, n_state, nctx):
    t = toks[0].shape[0]
    nc = t // CHUNK
    n_tok, n_par, n_const = len(toks), len(pars), len(consts)

    def body(*refs):
        it = iter(refs)
        tok_refs = [next(it) for _ in range(n_tok)]
        par_refs = [next(it) for _ in range(n_par)]
        const_refs = [next(it) for _ in range(n_const)]
        o_ref, ss_ref, s_scr = next(it), next(it), next(it)
        d, p = pl.program_id(0), pl.program_id(1)

        @pl.when(p == 0)
        def _():
            s_scr[...] = jnp.zeros(s_scr.shape, F32)

        ss_ref[0, 0] = s_scr[...]
        res = step(d, *[r[...] for r in tok_refs], *[r[...] for r in par_refs], *[s_scr[h] for h in range(n_state)],
                   consts=[r[0] for r in const_refs])
        o_ref[0] = res[0]
        for h in range(n_state):
            s_scr[h] = res[1 + h]

    ch = lambda d, p: _chunk_of(d, p, nctx, nc)
    in_specs = [pl.BlockSpec((CHUNK, a.shape[1]), lambda d, p: (ch(d, p), 0)) for a in toks]
    in_specs += [pl.BlockSpec(a.shape, lambda d, p: (0, 0)) for a in pars]
    return pl.pallas_call(
        body, grid=(2, nc), in_specs=in_specs + _const_specs(consts),
        out_specs=[pl.BlockSpec((1, CHUNK, out_width), lambda d, p: (d, ch(d, p), 0)),
                   pl.BlockSpec((1, 1, n_state, 128, 128), lambda d, p: (d, p, 0, 0, 0))],
        out_shape=[jax.ShapeDtypeStruct((2, t, out_width), F32),
                   jax.ShapeDtypeStruct((2, nc, n_state, 128, 128), F32)],
        scratch_shapes=[pltpu.VMEM((n_state, 128, 128), F32)],
        compiler_params=_cp(("arbitrary", "arbitrary")), name=name)(*toks, *pars, *consts)


def _scan_bwd(name, step, toks, pars, ss, dout, n_state, nctx, ride=None, aux=None, consts=()):
    t = toks[0].shape[0]
    nc = t // CHUNK
    n_tok, n_par, n_const = len(toks), len(pars), len(consts)
    rd = ride if ride is not None else _NoRide
    n_aux = aux.shape[2] if aux is not None else 0

    def body(*refs):
        it = iter(refs)
        tok_refs = [next(it) for _ in range(n_tok)]
        par_refs = [next(it) for _ in range(n_par)]
        ss_ref, do_ref = next(it), next(it)
        aux_ref = next(it) if n_aux else None
        const_refs = [next(it) for _ in range(n_const)]
        ride_in = [next(it) for _ in range(rd.n)]
        dtok_refs = [next(it) for _ in range(n_tok)]
        dpar_refs = [next(it) for _ in range(n_par)]
        ride_out = [next(it) for _ in range(rd.n)]
        ds_scr = next(it)
        sems = list(it)
        d, pr = pl.program_id(0), pl.program_id(1)
        if ride is not None:
            @pl.when((d == 0) & (pr == 0))
            def _():
                ride.start(ride_in, ride_out, sems)

            @pl.when((d == 1) & (pr == nc - 1))
            def _():
                ride.wait(ride_in, ride_out, sems)

        @pl.when(pr == 0)
        def _():
            ds_scr[...] = jnp.zeros(ds_scr.shape, F32)

        kw = dict(aux=[aux_ref[0, 0, i] for i in range(n_aux)]) if n_aux else {}
        if n_const:
            kw['consts'] = [r[0] for r in const_refs]
        _, pull = jax.vjp(functools.partial(step, d, **kw), *[r[...] for r in tok_refs], *[r[...] for r in par_refs],
                          *[ss_ref[0, 0, h] for h in range(n_state)])
        grads = pull((do_ref[...], *[ds_scr[h] for h in range(n_state)]))
        for r, g in zip(dtok_refs, grads[:n_tok]):
            r[0] = g
        for h in range(n_state):
            ds_scr[h] = grads[n_tok + n_par + h]
        first = (d == 0) & (pr == 0)
        for r, g in zip(dpar_refs, grads[n_tok:n_tok + n_par]):
            @pl.when(first)
            def _(r=r, g=g):
                r[...] = g

            @pl.when(jnp.logical_not(first))
            def _(r=r, g=g):
                r[...] += g

    ch = lambda d, pr: _chunk_of(d, nc - 1 - pr, nctx, nc)
    in_specs = [pl.BlockSpec((CHUNK, a.shape[1]), lambda d, pr: (ch(d, pr), 0)) for a in toks]
    in_specs += [pl.BlockSpec(a.shape, lambda d, pr: (0, 0)) for a in pars]
    in_specs += [pl.BlockSpec((1, 1, n_state, 128, 128), lambda d, pr: (d, nc - 1 - pr, 0, 0, 0)),
                 pl.BlockSpec((CHUNK, dout.shape[1]), lambda d, pr: (ch(d, pr), 0))]
    if n_aux:
        in_specs += [pl.BlockSpec((1, 1, n_aux, CHUNK, CHUNK), lambda d, pr: (d, nc - 1 - pr, 0, 0, 0))]
    in_specs += _const_specs(consts)
    out_specs = [pl.BlockSpec((1, CHUNK, a.shape[1]), lambda d, pr: (d, ch(d, pr), 0)) for a in toks]
    out_specs += [pl.BlockSpec(a.shape, lambda d, pr: (0, 0)) for a in pars]
    out_shape = [jax.ShapeDtypeStruct((2, t, a.shape[1]), F32) for a in toks]
    out_shape += [jax.ShapeDtypeStruct(a.shape, F32) for a in pars]
    return pl.pallas_call(
        body, grid=(2, nc), in_specs=in_specs + rd.in_specs, out_specs=out_specs + rd.out_specs,
        out_shape=out_shape + rd.out_shape, scratch_shapes=[pltpu.VMEM((n_state, 128, 128), F32)] + rd.scratch,
        compiler_params=_cp(("arbitrary", "arbitrary")), name=name)(
            *toks, *pars, ss, dout, *([aux] if n_aux else []), *consts, *rd.arrs)


def _scan2_fwd(name, step2, toks, out_width, n_state, nctx, n_aux, ride=None):
    t = toks[0].shape[0]
    nc = t // CHUNK
    n_tok = len(toks)
    rd = ride if ride is not None else _NoRide

    def body(*refs):
        it = iter(refs)
        tok_refs = [next(it) for _ in range(2 * n_tok)]
        ride_in = [next(it) for _ in range(rd.n)]
        o_refs = [next(it), next(it)]
        ss_ref, aux_ref = next(it), next(it)
        ride_out = [next(it) for _ in range(rd.n)]
        s_scr = next(it)
        sems = list(it)
        p = pl.program_id(0)
        if ride is not None:
            @pl.when(p == 0)
            def _():
                ride.start(ride_in, ride_out, sems)

        @pl.when(p == 0)
        def _():
            s_scr[...] = jnp.zeros(s_scr.shape, F32)

        for d in range(2):
            ss_ref[d, 0] = s_scr[d * n_state:(d + 1) * n_state]
        res = step2(*[r[...] for r in tok_refs], *[s_scr[u] for u in range(2 * n_state)], want_aux=True)
        for d in range(2):
            o_refs[d][...] = res[d]
            for i in range(n_aux):
                aux_ref[d, 0, i] = res[2 + 2 * n_state + d * n_aux + i]
        for u in range(2 * n_state):
            s_scr[u] = res[2 + u]
        if ride is not None:
            @pl.when(p == nc - 1)
            def _():
                ride.wait(ride_in, ride_out, sems)

    def tok_spec(a, d):
        return pl.BlockSpec((CHUNK, a.shape[1]), lambda p: (_chunk_of(d, p, nctx, nc), 0))

    return pl.pallas_call(
        body, grid=(nc,), in_specs=[tok_spec(a, d) for d in range(2) for a in toks] + rd.in_specs,
        out_specs=[pl.BlockSpec((CHUNK, out_width), lambda p: (_chunk_of(0, p, nctx, nc), 0)),
                   pl.BlockSpec((CHUNK, out_width), lambda p: (_chunk_of(1, p, nctx, nc), 0)),
                   pl.BlockSpec((2, 1, n_state, 128, 128), lambda p: (0, p, 0, 0, 0)),
                   pl.BlockSpec((2, 1, n_aux, CHUNK, CHUNK), lambda p: (0, p, 0, 0, 0))] + rd.out_specs,
        out_shape=[jax.ShapeDtypeStruct((t, out_width), F32), jax.ShapeDtypeStruct((t, out_width), F32),
                   jax.ShapeDtypeStruct((2, nc, n_state, 128, 128), F32),
                   jax.ShapeDtypeStruct((2, nc, n_aux, CHUNK, CHUNK), F32)] + rd.out_shape,
        scratch_shapes=[pltpu.VMEM((2 * n_state, 128, 128), F32)] + rd.scratch,
        compiler_params=_cp(("arbitrary",)), name=name)(*toks, *toks, *rd.arrs)


ADA_TN = 512


def _ada_part(cc16, w_shard):
    n = w_shard.shape[1]

    def body(c_ref, w_ref, o_ref):
        s = _silu(c_ref[...]).astype(_MXU_DTYPE)
        o_ref[...] = jnp.dot(s, w_ref[...].astype(_MXU_DTYPE), preferred_element_type=F32)

    return pl.pallas_call(
        body, grid=(n // ADA_TN,),
        in_specs=[pl.BlockSpec((16, D_MODEL), lambda j: (0, 0)), pl.BlockSpec((D_MODEL, ADA_TN), lambda j: (0, j))],
        out_specs=pl.BlockSpec((16, ADA_TN), lambda j: (0, j)), out_shape=jax.ShapeDtypeStruct((16, n), F32),
        compiler_params=_cp(("parallel",)), name="ada_part")(cc16, w_shard)


def _ada_bwd_shard(cc16, w_shard, d_lat, d_ctx):
    n = w_shard.shape[1]
    nj = n // ADA_TN

    def body(c_ref, w_ref, gl_ref, gc_ref, dw_ref, dc_ref):
        j = pl.program_id(0)
        s, pull = jax.vjp(_silu, c_ref[...])
        row = lax.broadcasted_iota(jnp.int32, (8, 1), 0)
        dctx = jnp.where(row == 0, jnp.sum(gc_ref[...], axis=0, keepdims=True), 0.0)
        mxu = lambda v: v.astype(_MXU_DTYPE)
        dw_ref[...] = (lax.dot_general(mxu(s[0:8]), mxu(gl_ref[...]), _TN, preferred_element_type=F32)
                       + lax.dot_general(mxu(s[8:16]), mxu(dctx), _TN, preferred_element_type=F32))
        ds = lax.dot_general(mxu(dctx), mxu(w_ref[...]), _NT, preferred_element_type=F32)

        @pl.when(j == 0)
        def _():
            dc_ref[...] = ds

        @pl.when(j > 0)
        def _():
            dc_ref[...] += ds

        @pl.when(j == nj - 1)
        def _():
            ct = jnp.concatenate([jnp.zeros((8, D_MODEL), F32), dc_ref[...]], axis=0)
            dc_ref[...] = 0.5 * pull(ct)[0][8:16]

    tile = pl.BlockSpec((8, ADA_TN), lambda j: (0, j))
    wspec = pl.BlockSpec((D_MODEL, ADA_TN), lambda j: (0, j))
    return pl.pallas_call(
        body, grid=(nj,), in_specs=[pl.BlockSpec((16, D_MODEL), lambda j: (0, 0)), wspec, tile, tile],
        out_specs=[wspec, pl.BlockSpec((8, D_MODEL), lambda j: (0, 0))],
        out_shape=[jax.ShapeDtypeStruct((D_MODEL, n), F32), jax.ShapeDtypeStruct((8, D_MODEL), F32)],
        compiler_params=_cp(("arbitrary",)), name="ada_bwd_shard")(cc16, w_shard, d_lat, d_ctx)


def _rowsum2(a, b):
    def body(a_ref, b_ref, o_ref):
        o_ref[...] = jnp.sum(a_ref[...], axis=0, keepdims=True) + jnp.sum(b_ref[...], axis=0, keepdims=True)

    return pl.pallas_call(body, out_shape=jax.ShapeDtypeStruct((1, a.shape[1]), F32), name="ada_b_grad")(a, b)


def _tail(h1, ff, mods, wf, tgt, nctx_t, tl):
    t = h1.shape[0]
    nt = t // tl

    def loss_fn(valid, h1v, ffv, g8, w, tg):
        h2 = h1v + g8[0:1] * ffv
        y = _rms(h2) * w
        err = (y - tg) ** 2
        return 0.5 * jnp.sum(jnp.mean(err, axis=-1, keepdims=True), axis=0, keepdims=True) * valid

    def body(h1_ref, ff_ref, g_ref, w_ref, t_ref, loss_ref, dh_ref, dff_ref, dg_ref, dw_ref):
        i = pl.program_id(0)
        valid = jnp.where(i < nctx_t, 0.0, 1.0)
        lv, pull = jax.vjp(functools.partial(loss_fn, valid), h1_ref[...], ff_ref[...].astype(F32), g_ref[...],
                           w_ref[...], t_ref[...])
        dh, dff, dg, dw, _ = pull(jnp.ones((1, 1), F32))
        dh_ref[...] = dh
        dff_ref[...] = dff.astype(dff_ref.dtype)
        lb = jnp.broadcast_to(lv, loss_ref.shape)

        @pl.when(i == 0)
        def _():
            loss_ref[...] = lb
            dg_ref[...] = dg
            dw_ref[...] = dw

        @pl.when(i > 0)
        def _():
            loss_ref[...] += lb
            dg_ref[...] += dg
            dw_ref[...] += dw

    tok = pl.BlockSpec((tl, D_MODEL), lambda i: (i, 0))
    return pl.pallas_call(
        body, grid=(nt,),
        in_specs=[tok, tok, pl.BlockSpec((8, D_MODEL), lambda i: (0, 5)), pl.BlockSpec((1, D_MODEL), lambda i: (0, 0)),
                  pl.BlockSpec((tl, D_MODEL), lambda i: (jnp.maximum(i - nctx_t, 0), 0))],
        out_specs=[pl.BlockSpec((8, 128), lambda i: (0, 0)), tok, tok, pl.BlockSpec((8, D_MODEL), lambda i: (0, 0)),
                   pl.BlockSpec((1, D_MODEL), lambda i: (0, 0))],
        out_shape=[jax.ShapeDtypeStruct((8, 128), F32), jax.ShapeDtypeStruct((t, D_MODEL), F32),
                   jax.ShapeDtypeStruct((t, D_MODEL), _MXU_DTYPE), jax.ShapeDtypeStruct((8, D_MODEL), F32),
                   jax.ShapeDtypeStruct((1, D_MODEL), F32)],
        compiler_params=_cp(("arbitrary",)), name="tail_loss")(h1, ff, mods, wf, tgt)


def _pack_w_in(w4):
    ns = w4.shape[2]
    placed = []
    for s0, s1, p0 in IN_SEGMENTS:
        for j in range(N_PLANE):
            lo, hi = max(s0, j * ns), min(s1, (j + 1) * ns)
            if lo < hi:
                placed.append((p0 + lo - s0, w4[j][:, lo - j * ns:hi - j * ns]))
    placed.sort(key=lambda e: e[0])
    pieces, end = [], 0
    for pos, piece in placed:
        assert pos == end, (pos, end)
        pieces.append(piece)
        end = pos + piece.shape[1]
    pieces.append(jnp.zeros((w4.shape[1], P_TOTAL - end), w4.dtype))
    return jnp.concatenate(pieces, axis=1)


def _unpack_w_in(g):
    ns = D_IN_PROJ // N_PLANE
    shards = []
    for j in range(N_PLANE):
        pieces = []
        for s0, s1, p0 in IN_SEGMENTS:
            lo, hi = max(s0, j * ns), min(s1, (j + 1) * ns)
            if lo < hi:
                pieces.append(g[:, p0 + lo - s0:p0 + hi - s0])
        shards.append(jnp.concatenate(pieces, axis=1))
    return jnp.stack(shards)


LATE_WEIGHTS = ['w_br_gdn', 'w_br_ssm', 'w_out', 'w_ffn_in', 'w_ffn_out']
COL_STACKED = ('w_in', 'w_ffn_in')


def _from_shards(n, g):
    if n in COL_STACKED:
        return g
    if SHARD_AXIS[n] == 0:
        return g.reshape(N_PLANE * g.shape[1], g.shape[2])
    return jnp.concatenate([g[j] for j in range(N_PLANE)], axis=1)


def _to_dest_blocks(n, g):
    if g.ndim == 3:
        return g
    if SHARD_AXIS[n] == 0:
        return g.reshape(N_PLANE, g.shape[0] // N_PLANE, g.shape[1])
    sz = g.shape[1] // N_PLANE
    return jnp.stack([g[:, j * sz:(j + 1) * sz] for j in range(N_PLANE)])


def _local_step(x, c, ctx, tgt, W, late_shards):
    lc, l = ctx.shape[0], x.shape[0]
    t = lc + l
    tl = 256
    assert lc == tl and l % tl == 0 and lc % CHUNK == 0
    nt, nctx_t, nctx = t // tl, lc // tl, lc // CHUNK
    act = _MXU_DTYPE
    r1 = lambda v: v.reshape(1, -1)

    xt = jnp.concatenate([ctx, x], axis=0)
    lin = 4 * lax.axis_index("x") + 2 * lax.axis_index("y") + lax.axis_index("c")
    c_all = _gather_all("gather_c", jnp.broadcast_to(c, (8, D_MODEL)))[:, 0]
    cc16 = jnp.concatenate([c_all, r1(W['c_ctx']), jnp.zeros((7, D_MODEL), F32)], axis=0)
    (parts,) = _plane_exchange("gather_mods", [_ada_part(cc16, W['ada_w'])], gather=True)
    mods_all = jnp.transpose(parts, (1, 0, 2)).reshape(16, -1) + r1(W['ada_b'])
    mods = jnp.concatenate([lax.dynamic_slice_in_dim(mods_all, lin, 1, axis=0), mods_all[8:9],
                            jnp.zeros((6, mods_all.shape[1]), F32)], axis=0)

    def mod(kk):
        return Op(mods, (8, D_MODEL), lambda j, i, kk=kk: (0, kk))

    def tokop(arr, w=D_MODEL, off=0, tl_=tl):
        if not isinstance(arr, list) and arr.ndim == 3:
            return Op(arr, (2, tl_, w), lambda j, i: (0, i, off + j), load=_sum_dirs)
        return Op(arr, (tl_, w), lambda j, i: (i, off + j))

    def outop(n, dtype, w=D_MODEL, tl_=tl):
        return Op(((t, n), dtype), (tl_, w), lambda j, i: (i, j))

    def parop(arr, w, off=0):
        return Op(arr, (arr.shape[0], w), lambda j, i: (0, off + j))

    def parout(rows, n, w):
        return Op(((rows, n), F32), (rows, w), lambda j, i: (0, j))

    n1w = r1(W['norm1_w'])
    pre_fn = functools.partial(_prenorm_fn, nctx_t)
    pre_tok, pre_par = [tokop(xt)], [parop(n1w, D_MODEL), mod(1), mod(0)]
    (a,) = _tw_fwd("prenorm_fwd", pre_fn, (1, nt), pre_tok + pre_par, [outop(D_MODEL, act)])
    wp = _pack_w_in(W['w_in'])
    proj = _matmul(a, wp, 'nn', "in_proj")

    gcw, gcb = W['gdn_conv_w'], r1(W['gdn_conv_b'])
    scw, scb = W['ssm_conv_w'], r1(W['ssm_conv_b'])
    conv_parts = {}

    tlf = 768 if t % 768 == 0 else tl

    def conv_part(name, mode, poff, cw, cb, woff, width):
        fn = functools.partial(_conv_fn, lc, mode)
        bw = min(width, 1024)
        par_ = [parop(cw, bw, woff // bw), parop(cb, bw, woff // bw)]
        conv_parts[name] = (fn, cw, cb, woff, width, poff)
        (o,) = _tw_fwd("conv_" + name, fn, (width // bw, t // tlf), [tokop(proj, bw, poff // bw, tlf)] + par_,
                       [outop(width, F32, bw, tlf)])
        return o

    q = conv_part('q', 'q', P_QKV, gcw, gcb, 0, 1024)
    k = conv_part('k', 'k', P_QKV + 1024, gcw, gcb, 1024, 1024)
    v = conv_part('v', 'none', P_QKV + 2048, gcw, gcb, 2048, 1024)
    xs = conv_part('xs', 'none', P_XBC, scw, scb, 0, 2048)
    bm = conv_part('bm', 'none', P_XBC + 2048, scw, scb, 2048, 512)
    cm = conv_part('cm', 'none', P_XBC + 2560, scw, scb, 2560, 512)

    z16, z32 = jnp.zeros((16,), F32), jnp.zeros((32,), F32)
    p0 = jnp.concatenate([W['gdn_a_log'].reshape(-1), jnp.zeros((112,), F32)]).reshape(1, 128)
    p1 = jnp.concatenate([W['gdn_dt_bias'].reshape(-1), z16, W['ssm_dt_bias'].reshape(-1), z32]).reshape(1, 128)
    alog = jnp.concatenate([z32, W['ssm_a_log'].reshape(-1), z32]).reshape(1, 128)
    act_tok, act_par = [tokop(proj, 128, P_SMALL // 128)], [parop(p0, 128), parop(p1, 128)]
    (sp,) = _tw_fwd("small_act", _act_fn, (1, t // tlf), [tokop(proj, 128, P_SMALL // 128, tlf)] + act_par,
                    [outop(128, F32, 128, tlf)])

    gather_late = PlaneExchange([late_shards[n] for n in LATE_WEIGHTS], gather=True)
    o_f, o_b, ss, tri, *late = _scan2_fwd("gdn_scan_fwd", _gdn_step2, [q, k, v, sp], 1024, GDN_HEADS, nctx,
                                          GDN_HEADS, ride=gather_late)
    W = dict(W, **{n: _from_shards(n, g) for n, g in zip(LATE_WEIGHTS, late)})
    ssd_consts = _ssd_consts()
    y2, hs = _scan_fwd("ssd_scan_fwd", _ssd_step, [xs, bm, cm, sp], [alog], ssd_consts, 2048, SSM_PAIRS, nctx)

    tlm = 128
    ntm = t // tlm
    gnw = r1(W['gdn_norm_w'])
    dl = jnp.repeat(W['ssm_d'], 64).reshape(1, 2048)
    snw = r1(W['ssm_norm_w'])
    mg_tok = [tokop([o_f, o_b], 1024, 0, tlm), tokop(proj, 1024, P_ZG // 1024, tlm)]
    mg_par = [parop(gnw, 128)]
    (og,) = _tw_fwd("mix_gdn", _mixg_fn, (1, nt),
                    [tokop([o_f, o_b], 1024, 0), tokop(proj, 1024, P_ZG // 1024)] + mg_par, [outop(1024, act, 1024)])
    ms_tok = [tokop(y2, 2048, 0, tlm), tokop(xs, 2048, 0, tlm), tokop(proj, 2048, P_ZS // 2048, tlm)]
    ms_par = [parop(dl, 2048), parop(snw, 2048)]
    (yg,) = _tw_fwd("mix_ssm", _mixs_fn, (1, nt),
                    [tokop(y2, 2048, 0), tokop(xs, 2048, 0), tokop(proj, 2048, P_ZS // 2048)] + ms_par,
                    [outop(2048, act, 2048)])

    pg = _matmul(og, W['w_br_gdn'], 'nn', "br_gdn", out_dtype=act)
    ps = _matmul(yg, W['w_br_ssm'], 'nn', "br_ssm", out_dtype=act)
    mr_tok = [tokop(proj, 2048, P_GATE // 2048), tokop(pg), tokop(ps)]
    (mrg,) = _tw_fwd("merge", _merge_fn, (1, t // tlf),
                     [tokop(proj, 2048, P_GATE // 2048, tlf), tokop(pg, tl_=tlf), tokop(ps, tl_=tlf)],
                     [outop(1024, act, tl_=tlf)])
    mo = _matmul(mrg, W['w_out'], 'nn', "w_out")

    n2w = r1(W['norm2_w'])
    n2_tok, n2_par = [tokop(xt), tokop(mo)], [mod(2), parop(n2w, D_MODEL), mod(4), mod(3)]
    h1, f = _tw_fwd("norm2_fwd", _norm2_fn, (1, t // tlf), [tokop(xt, tl_=tlf), tokop(mo, tl_=tlf)] + n2_par,
                    [outop(1024, F32, tl_=tlf), outop(1024, act, tl_=tlf)])
    u2 = _matmul(f, W['w_ffn_in'], 'nn', "ffn_in", out_dtype=act, halves='out')
    swb = D_FF // 2
    both = lambda arr, tl_=tl: Op(arr, (2, tl_, swb), lambda j, i: (0, i, j))
    sw_tok = [both(u2)]
    (sw,) = _tw_fwd("swiglu", _swiglu_fn, (D_FF // swb, t // tlf), [both(u2, tlf)], [outop(D_FF, act, swb, tlf)])
    ff = _matmul(sw, W['w_ffn_out'], 'nn', "ffn_out")

    loss8, dh1, dff, dg2, dnf = _tail(h1, ff, mods, r1(W['norm_f_w']), tgt, nctx_t, tl)
    loss = loss8[0, 0]

    G = {}
    G['norm_f_w'] = dnf.reshape(-1)
    G['w_ffn_out'] = _matmul(sw, dff, 'tn', "d_ffn_out", out_dtype=_GRAD_DTYPE)
    dsw = _matmul(dff, W['w_ffn_out'], 'nt', "d_sw", out_dtype=act)
    (du2,) = _tw_bwd("swiglu_bwd", _swiglu_fn, (D_FF // swb, nt), sw_tok, [], [[tokop(dsw, swb)]],
                     [both(((2, t, D_FF), act))], [])
    G['w_ffn_in'] = _matmul(f, du2, 'tn', "d_ffn_in", out_dtype=_GRAD_DTYPE, stacked_out=True, halves='b')
    df = _matmul(du2, W['w_ffn_in'], 'nt', "d_f", halves='a')
    dxt1, dmo, dg1, dn2, dsc2, dsh2 = _tw_bwd(
        "norm2_bwd", _norm2_fn, (1, nt), n2_tok, n2_par, [[tokop(dh1)], [tokop(df)]],
        [outop(1024, F32), outop(1024, act)],
        [parout(8, 1024, 1024), parout(1, 1024, 1024), parout(8, 1024, 1024), parout(8, 1024, 1024)])
    G['norm2_w'] = dn2.reshape(-1)
    G['w_out'] = _matmul(mrg, dmo, 'tn', "d_w_out", out_dtype=_GRAD_DTYPE)
    dmrg = _matmul(dmo, W['w_out'], 'nt', "d_mrg", out_dtype=act)
    def win(off, w, tl_=tl):
        return Op(((t, P_TOTAL), act), (tl_, w), lambda j, i: (i, off // w + j))

    dproj = lax.empty((t, P_TOTAL), act)
    dproj, dpg, dps = _tw_bwd("merge_bwd", _merge_fn, (1, nt), mr_tok, [], [[tokop(dmrg)]],
                              [win(P_GATE, 2048), outop(1024, act), outop(1024, act)], [], into={0: dproj})
    G['w_br_gdn'] = _matmul(og, dpg, 'tn', "d_br_gdn", out_dtype=_GRAD_DTYPE)
    G['w_br_ssm'] = _matmul(yg, dps, 'tn', "d_br_ssm", out_dtype=_GRAD_DTYPE)
    dog = _matmul(dpg, W['w_br_gdn'], 'nt', "d_og")
    dyg = _matmul(dps, W['w_br_ssm'], 'nt', "d_yg")

    do, dproj, dgnw = _tw_bwd("mix_gdn_bwd", _mixg_fn, (1, ntm), mg_tok, mg_par, [[tokop(dog, 1024, 0, tlm)]],
                              [outop(1024, F32, 1024, tlm), win(P_ZG, 1024, tlm)], [parout(1, 128, 128)],
                              into={1: dproj})
    G['gdn_norm_w'] = dgnw.reshape(-1)
    dy, dxs_a, dproj, ddl, dsnw = _tw_bwd(
        "mix_ssm_bwd", _mixs_fn, (1, ntm), ms_tok, ms_par, [[tokop(dyg, 2048, 0, tlm)]],
        [outop(2048, F32, 2048, tlm), outop(2048, F32, 2048, tlm), win(P_ZS, 2048, tlm)],
        [parout(1, 2048, 2048), parout(1, 2048, 2048)], into={2: dproj})
    G['ssm_d'] = ddl.reshape(SSM_HEADS, 64).sum(axis=1)
    G['ssm_norm_w'] = dsnw.reshape(-1)

    dq2, dk2, dv2, dsp_g = _scan_bwd("gdn_scan_bwd", _gdn_step, [q, k, v, sp], [], ss, do, GDN_HEADS, nctx, aux=tri)
    received = {}
    scatter_late = PlaneExchange([_to_dest_blocks(n, G.pop(n)) for n in LATE_WEIGHTS], gather=False)
    dxs2, dbm2, dcm2, dsp_s, dalog, *got = _scan_bwd("ssd_scan_bwd", _ssd_step, [xs, bm, cm, sp], [alog], hs, dy,
                                                     SSM_PAIRS, nctx, ride=scatter_late, consts=ssd_consts)
    received.update(zip(LATE_WEIGHTS, got))
    G['ssm_a_log'] = dalog[0, 32:96].reshape(2, SSM_HEADS)

    dconv_w, dconv_b = {}, {}

    def conv_bwd(name, cots_, buf):
        fn, cw, cb, woff, width, poff = conv_parts[name]
        bw = 512
        buf, dconv_w[name], dconv_b[name] = _tw_bwd(
            "conv_" + name + "_bwd", fn, (width // bw, t // tlf), [tokop(proj, bw, poff // bw, tlf)],
            [parop(cw, bw, woff // bw), parop(cb, bw, woff // bw)], [[tokop(c_, bw, 0, tlf) for c_ in cots_]],
            [win(poff, bw, tlf)], [parout(3, width, bw), parout(1, width, bw)], into={0: buf})
        return buf

    dproj = conv_bwd('q', [dq2], dproj)
    dproj = conv_bwd('k', [dk2], dproj)
    dproj = conv_bwd('v', [dv2], dproj)
    dproj = conv_bwd('xs', [dxs2, dxs_a], dproj)
    dproj = conv_bwd('bm', [dbm2], dproj)
    dproj = conv_bwd('cm', [dcm2], dproj)
    G['gdn_conv_w'] = jnp.concatenate([dconv_w['q'], dconv_w['k'], dconv_w['v']], axis=1)
    G['gdn_conv_b'] = [dconv_b['q'], dconv_b['k'], dconv_b['v']]
    G['ssm_conv_w'] = jnp.concatenate([dconv_w['xs'], dconv_w['bm'], dconv_w['cm']], axis=1)
    G['ssm_conv_b'] = [dconv_b['xs'], dconv_b['bm'], dconv_b['cm']]

    dproj, dp0, dp1 = _tw_bwd("small_act_bwd", _act_fn, (1, nt), act_tok, act_par,
                              [[tokop(dsp_g, 128), tokop(dsp_s, 128)]], [win(P_SMALL, P_TOTAL - P_SMALL)],
                              [parout(1, 128, 128), parout(1, 128, 128)], into={0: dproj})
    G['gdn_a_log'] = dp0[0, 0:16].reshape(2, GDN_HEADS)
    G['gdn_dt_bias'] = dp1[0, 0:16].reshape(2, GDN_HEADS)
    G['ssm_dt_bias'] = dp1[0, 32:96].reshape(2, SSM_HEADS)

    g_w_in = _unpack_w_in(_matmul(a, dproj, 'tn', "d_w_in", out_dtype=_GRAD_DTYPE))
    da, (received['w_in'],) = _matmul(dproj, wp, 'nt', "d_a", ride=PlaneExchange([g_w_in], gather=False))

    gx_out = Op(((l, D_MODEL), F32), (tl, D_MODEL), lambda j, i: (jnp.maximum(i - nctx_t, 0), 0))
    grad_x, dn1, dsc1, dsh1 = _tw_bwd(
        "prenorm_bwd", pre_fn, (1, nt), pre_tok, pre_par, [[tokop(da)]], [gx_out],
        [parout(1, 1024, 1024), parout(8, 1024, 1024), parout(8, 1024, 1024)], tok_add=tokop(dxt1),
        sem=("arbitrary", "arbitrary"))
    G['norm1_w'] = dn1.reshape(-1)
    dmods = jnp.concatenate([dsh1, dsc1, dg1, dsh2, dsc2, dg2], axis=1)
    dm_all = _gather_all("gather_dmods", dmods)
    ns = W['ada_w'].shape[1]
    mine = lax.dynamic_slice_in_dim(dm_all, (2 * lax.axis_index("x") + lax.axis_index("y")) * ns, ns, axis=2)
    G['ada_w'], dcc = _ada_bwd_shard(cc16, W['ada_w'], mine[:, 0], mine[:, 1])
    G['ada_b'] = jnp.where(lin == 0, _rowsum2(dm_all[:, 0], dm_all[:, 1]).reshape(-1), 0.0)
    G['c_ctx'] = dcc[0]
    return loss, grad_x, G, received


def _row_tile(r, c):
    for tr in (512, 256, 128, 64, 32, 16, 8):
        if r % tr == 0 and tr * c * 4 <= (1 << 20):
            return tr
    return r


def _sum4(name, rv):
    _, r, c = rv.shape
    tr = _row_tile(r, c)

    def body(r_ref, o_ref):
        o_ref[...] = ((r_ref[0].astype(F32) + r_ref[1].astype(F32)) + r_ref[2].astype(F32)) + r_ref[3].astype(F32)

    return pl.pallas_call(
        body, grid=(r // tr,), in_specs=[pl.BlockSpec((N_PLANE, tr, c), lambda i: (0, i, 0))],
        out_specs=pl.BlockSpec((tr, c), lambda i: (i, 0)), out_shape=jax.ShapeDtypeStruct((r, c), F32),
        compiler_params=_cp(("parallel",)), name=name)(rv)


def _adamw(name, w, m, v, p, q=None):
    r, c = w.shape[-2:]
    tr = _row_tile(r, c)
    grads = [p] if q is None else [p, q]
    lead = (0,) * (w.ndim - 2)

    def body(w_ref, m_ref, v_ref, *refs):
        g_ref, d_ref, mo_ref, vo_ref = refs[len(grads):]
        g = refs[0][...] if q is None else refs[0][...] + refs[1][...]
        at = lead if lead else Ellipsis
        mn = ADAM_B1 * m_ref[at] + (1.0 - ADAM_B1) * g
        vn = ADAM_B2 * v_ref[at] + (1.0 - ADAM_B2) * jnp.square(g)
        m_hat = mn / (1.0 - ADAM_B1 ** ADAM_STEP)
        v_hat = vn / (1.0 - ADAM_B2 ** ADAM_STEP)
        g_ref[at] = g
        d_ref[at] = -ADAM_LR * (m_hat / (jnp.sqrt(v_hat) + ADAM_EPS) + ADAM_WD * w_ref[at])
        mo_ref[at] = mn
        vo_ref[at] = vn

    gspec = pl.BlockSpec((tr, c), lambda i: (i, 0))
    wspec = pl.BlockSpec((1,) * len(lead) + (tr, c), lambda i: lead + (i, 0))
    return pl.pallas_call(
        body, grid=(r // tr,), in_specs=[wspec] * 3 + [gspec] * len(grads), out_specs=[wspec] * 4,
        out_shape=[jax.ShapeDtypeStruct(w.shape, F32)] * 4, compiler_params=_cp(("parallel",)), name=name)(w, m, v, *grads)


SMALL_ROWS = 24


def _pack_small(d):
    pieces = [p.reshape(-1) for n in SMALL for p in (d[n] if isinstance(d[n], list) else [d[n]])]
    v = jnp.concatenate(pieces)
    return jnp.pad(v, (0, SMALL_ROWS * 1024 - v.shape[0])).reshape(SMALL_ROWS, 1024)


def _unpack_small(buf, like):
    v = buf.reshape(-1)
    out, off = {}, 0
    for n in SMALL:
        sz = like[n].size
        out[n] = v[off:off + sz].reshape(like[n].shape)
        off += sz
    return out


def kernel(x, c, ctx, c_ctx, ada_w, ada_b, norm1_w, w_in, gdn_conv_w, gdn_conv_b, gdn_a_log, gdn_dt_bias, gdn_norm_w, ssm_conv_w, ssm_conv_b, ssm_a_log, ssm_dt_bias, ssm_d, ssm_norm_w, w_br_gdn, w_br_ssm, w_out, norm2_w, w_ffn_in, w_ffn_out, norm_f_w, loss_target, m_c_ctx, m_ada_w, m_ada_b, m_norm1_w, m_w_in, m_gdn_conv_w, m_gdn_conv_b, m_gdn_a_log, m_gdn_dt_bias, m_gdn_norm_w, m_ssm_conv_w, m_ssm_conv_b, m_ssm_a_log, m_ssm_dt_bias, m_ssm_d, m_ssm_norm_w, m_w_br_gdn, m_w_br_ssm, m_w_out, m_norm2_w, m_w_ffn_in, m_w_ffn_out, m_norm_f_w, v_c_ctx, v_ada_w, v_ada_b, v_norm1_w, v_w_in, v_gdn_conv_w, v_gdn_conv_b, v_gdn_a_log, v_gdn_dt_bias, v_gdn_norm_w, v_ssm_conv_w, v_ssm_conv_b, v_ssm_a_log, v_ssm_dt_bias, v_ssm_d, v_ssm_norm_w, v_w_br_gdn, v_w_br_ssm, v_w_out, v_norm2_w, v_w_ffn_in, v_w_ffn_out, v_norm_f_w):
    args = dict(locals())
    wl = {n: args[n] for n in WEIGHTS}
    ml = {n: args['m_' + n] for n in WEIGHTS}
    vl = {n: args['v_' + n] for n in WEIGHTS}

    def nodepth(n, a):
        return a if n in ('c_ctx', 'norm_f_w') else a[0]

    shard = {n: nodepth(n, wl[n]).astype(_MXU_DTYPE if n in MXU_WEIGHTS else F32) for n in SHARDED}
    first = [n for n in SHARDED if n not in LATE_WEIGHTS and n != 'ada_w']
    W = {n: nodepth(n, wl[n]) for n in SMALL}
    W['ada_w'] = shard['ada_w']
    for n, g in zip(first, _gather_two_level("all_gather_plane", [shard[n] for n in first])):
        W[n] = _from_shards(n, g)

    loss_local, grad_x, G, received = _local_step(x[0], c, ctx[0], loss_target[0], W,
                                                  late_shards={n: shard[n] for n in LATE_WEIGHTS})
    loss = lax.psum(loss_local, ("x", "y", "c"))

    small_g = _pack_small(G)
    last = [n for n in SHARDED if n not in received and n != 'ada_w']
    send = [_to_dest_blocks(n, G[n]) for n in last] + [jnp.broadcast_to(small_g[None], (N_PLANE,) + small_g.shape)]
    received.update(zip(last + ['small'], _plane_exchange("scatter_plane", send, gather=False)))
    names = [n for n in SHARDED if n != 'ada_w'] + ['small']
    plane_sum = [_sum4("sum4_" + n, received[n]) for n in names]
    other = _swap_sibling(plane_sum)

    wd = {n: nodepth(n, wl[n]) for n in WEIGHTS}
    md = {n: nodepth(n, ml[n]) for n in WEIGHTS}
    vd = {n: nodepth(n, vl[n]) for n in WEIGHTS}
    res = {'ada_w': _adamw("adamw_ada_w", wl['ada_w'], ml['ada_w'], vl['ada_w'], G['ada_w'])}
    for n, p, q in zip(names, plane_sum, other):
        if n == 'small':
            outs = _adamw("adamw_small", _pack_small(wd), _pack_small(md), _pack_small(vd), p, q)
            unpacked = [_unpack_small(o, wd) for o in outs]
            for sn in SMALL:
                res[sn] = [u[sn].reshape(wl[sn].shape) for u in unpacked]
        else:
            res[n] = _adamw("adamw_" + n, wl[n], ml[n], vl[n], p, q)
    flat = [res[n][kind] for kind in range(4) for n in WEIGHTS]
    return (loss, grad_x[None], *flat)
```
